```python
import jax, jax.numpy as jnp
from jax import lax
import numpy as np

D_MODEL = 1024
BATCH = 16
SEQ = 4096
DEPTH = 2

N_MIXERS = 2
N_META = 16
BLOCK = 128
N_PAD = BLOCK - N_META
BRANCH = D_MODEL
HG_DK = 128
HG_DV = 128
HG_HEADS = BRANCH // HG_DV
HG_CHUNK = BLOCK
HG_SUB = 16
N_SUB = HG_CHUNK // HG_SUB
SB_DH = 128
SB_HEADS = BRANCH // SB_DH
SB_SCALE = SB_DH ** -0.5
N_HGRN = (DEPTH + 1) // 2
N_SB = DEPTH // 2
EPS = 1e-6

kernel_name = "hybrid_hgrn2_stickbreaking_meta"


def _rmsnorm(x, w):
    xf = x.astype(jnp.float32)
    y = xf * lax.rsqrt(jnp.mean(xf * xf, axis=-1, keepdims=True) + EPS)
    return (y * w.astype(jnp.float32)).astype(x.dtype)


def _heads(t, n):
    b, l, _ = t.shape
    return t.reshape(b, l, n, -1).transpose(0, 2, 1, 3)


def _pad_front(t):
    return jnp.pad(t, ((0, 0), (0, 0), (N_PAD, 0), (0, 0)))


def _to_blocks(t):
    b, h, lp, d = t.shape
    return t.reshape(b, h, lp // BLOCK, BLOCK, d).transpose(2, 0, 1, 3, 4)


def _hgrn2_chunk(S, inp):
    q, k, v, g = inp
    bsz, h = q.shape[:2]
    b = jnp.cumsum(g, axis=2)
    o = jnp.einsum('bhtd,bhde->bhte', q * jnp.exp(b), S)
    qs = q.reshape(bsz, h, N_SUB, HG_SUB, HG_DK)
    ks = k.reshape(bsz, h, N_SUB, HG_SUB, HG_DK)
    vs = v.reshape(bsz, h, N_SUB, HG_SUB, HG_DV)
    bs = b.reshape(bsz, h, N_SUB, HG_SUB, HG_DK)
    tri = jnp.tril(jnp.ones((HG_SUB, HG_SUB), bool))[:, :, None]
    diff = bs[:, :, :, :, None, :] - bs[:, :, :, None, :, :]
    dec = jnp.exp(jnp.where(tri, diff, -jnp.inf))
    a_diag = jnp.einsum('bhntd,bhntsd,bhnsd->bhnts', qs, dec, ks)
    o_diag = jnp.einsum('bhnts,bhnse->bhnte', a_diag, vs)
    ref = jnp.concatenate([jnp.zeros_like(bs[:, :, :1, 0, :]), bs[:, :, :-1, -1, :]], axis=2)
    q_hat = qs * jnp.exp(bs - ref[:, :, :, None, :])
    lower = (jnp.arange(N_SUB)[None, :] < jnp.arange(N_SUB)[:, None])[:, :, None, None]
    expo = ref[:, :, :, None, None, :] - bs[:, :, None, :, :, :]
    k_hat = ks[:, :, None] * jnp.exp(jnp.where(lower, expo, -jnp.inf))
    a_off = jnp.einsum('bhitd,bhijsd->bhitjs', q_hat, k_hat)
    o_off = jnp.einsum('bhitjs,bhjse->bhite', a_off, vs)
    o = o + (o_diag + o_off).reshape(bsz, h, HG_CHUNK, HG_DV)
    b_last = b[:, :, -1, :]
    S = jnp.exp(b_last)[..., None] * S + jnp.einsum('bhsd,bhse->bhde', k * jnp.exp(b_last[:, :, None, :] - b), v)
    return S, o


def _hgrn2_mixer(y, w_in, lb, out_norm, w_out):
    bsz, l, _ = y.shape
    q, fz, v, gate = jnp.split(y @ w_in, 4, axis=-1)
    q, fz, v = (_heads(t, HG_HEADS).astype(jnp.float32) for t in (q, fz, v))
    lb = lb.reshape(1, HG_HEADS, 1, HG_DK)
    g = jnp.log(lb + (1.0 - lb) * jax.nn.sigmoid(fz))
    k = (1.0 - lb) * jax.nn.sigmoid(-fz)
    xs = tuple(_to_blocks(_pad_front(t)) for t in (q, k, v, g))
    s0 = jnp.zeros((bsz, HG_HEADS, HG_DK, HG_DV), jnp.float32)
    _, o = lax.scan(_hgrn2_chunk, s0, xs)
    nb = o.shape[0]
    o = o.transpose(1, 2, 0, 3, 4).reshape(bsz, HG_HEADS, nb * HG_CHUNK, HG_DV)[:, :, N_PAD:]
    o = _rmsnorm(o.transpose(0, 2, 1, 3), out_norm).reshape(bsz, l, BRANCH)
    return (o.astype(y.dtype) * jax.nn.silu(gate)) @ w_out


def _stick_breaking_mixer(y, w_in, w_out):
    bsz, l, _ = y.shape
    q, k, v, gate = jnp.split(y @ w_in, 4, axis=-1)
    q, k, v = (_pad_front(_heads(t, SB_HEADS).astype(jnp.float32)) for t in (q, k, v))
    lp = l + N_PAD
    s_pos = jnp.arange(lp)
    key_ok = s_pos >= N_PAD

    def block(args):
        qb, n = args
        t_pos = n * BLOCK + jnp.arange(BLOCK)
        valid = (s_pos[None, :] < t_pos[:, None]) & key_ok[None, :]
        z = jnp.einsum('bhtd,bhsd->bhts', qb, k) * SB_SCALE
        log_beta = jax.nn.log_sigmoid(z)
        log_keep = jnp.where(valid, log_beta - z, 0.0)
        later = lax.cumsum(log_keep, axis=3, reverse=True) - log_keep
        a = jnp.where(valid, jnp.exp(log_beta + later), 0.0)
        return jnp.einsum('bhts,bhse->bhte', a, v)

    o = lax.map(block, (_to_blocks(q), jnp.arange(lp // BLOCK)))
    o = o.transpose(1, 0, 3, 2, 4).reshape(bsz, lp, BRANCH)[:, N_PAD:]
    return (o.astype(y.dtype) * jax.nn.silu(gate)) @ w_out


def _fwd_setup_inputs(seed: int = 0) -> dict:
    key = jax.random.key(seed)
    ks = jax.random.split(key, 11)
    f32 = jnp.float32
    x = jax.random.normal(ks[0], (BATCH, SEQ, D_MODEL), f32)
    meta_tokens = jax.random.normal(ks[1], (N_META, D_MODEL), f32)
    pre_norm = 1.0 + 0.05 * jax.random.normal(ks[2], (DEPTH, D_MODEL), f32)
    post_norm = 1.0 + 0.05 * jax.random.normal(ks[3], (DEPTH, D_MODEL), f32)
    hgrn_w_in = jax.random.normal(ks[4], (N_HGRN, D_MODEL, 4 * BRANCH), f32) * D_MODEL ** -0.5
    hgrn_lb = 0.5 * jax.random.normal(ks[5], (N_HGRN + 1, BRANCH), f32)
    hgrn_out_norm = 1.0 + 0.05 * jax.random.normal(ks[6], (N_HGRN, HG_DV), f32)
    hgrn_w_out = jax.random.normal(ks[7], (N_HGRN, BRANCH, D_MODEL), f32) * BRANCH ** -0.5
    sb_w_in = jax.random.normal(ks[8], (N_SB, D_MODEL, 4 * BRANCH), f32) * D_MODEL ** -0.5
    sb_w_out = jax.random.normal(ks[9], (N_SB, BRANCH, D_MODEL), f32) * BRANCH ** -0.5
    return {"x": x, "meta_tokens": meta_tokens, "pre_norm": pre_norm, "post_norm": post_norm,
            "hgrn_w_in": hgrn_w_in, "hgrn_lb": hgrn_lb, "hgrn_out_norm": hgrn_out_norm,
            "hgrn_w_out": hgrn_w_out, "sb_w_in": sb_w_in, "sb_w_out": sb_w_out}


def _fwd_reference(x, meta_tokens, pre_norm, post_norm, hgrn_w_in, hgrn_lb, hgrn_out_norm, hgrn_w_out, sb_w_in, sb_w_out):
    bsz = x.shape[0]
    meta = jnp.broadcast_to(meta_tokens[None].astype(x.dtype), (bsz, N_META, D_MODEL))
    h = jnp.concatenate([meta, x], axis=1)
    lbs = jnp.cumsum(jax.nn.softmax(hgrn_lb.astype(jnp.float32), axis=0), axis=0)
    for i in range(DEPTH):
        y = _rmsnorm(h, pre_norm[i])
        j = i // N_MIXERS
        if i % N_MIXERS == 0:
            y = _hgrn2_mixer(y, hgrn_w_in[j], lbs[j], hgrn_out_norm[j], hgrn_w_out[j])
        else:
            y = _stick_breaking_mixer(y, sb_w_in[j], sb_w_out[j])
        h = h + _rmsnorm(y, post_norm[i])
    return h[:, N_META:]


import jax as _jax
import jax.numpy as _jnp

TWIN_FORMAT = 'train_step'
FWD_PARAMS = ['x', 'meta_tokens', 'pre_norm', 'post_norm', 'hgrn_w_in', 'hgrn_lb', 'hgrn_out_norm', 'hgrn_w_out', 'sb_w_in', 'sb_w_out']
TWIN_WEIGHTS = ['meta_tokens', 'pre_norm', 'post_norm', 'hgrn_w_in', 'hgrn_lb', 'hgrn_out_norm', 'hgrn_w_out', 'sb_w_in', 'sb_w_out']
TWIN_DIFF_INPUT = 'x'
TWIN_INPUTS = ['x', 'meta_tokens', 'pre_norm', 'post_norm', 'hgrn_w_in', 'hgrn_lb', 'hgrn_out_norm', 'hgrn_w_out', 'sb_w_in', 'sb_w_out', 'loss_target', 'm_meta_tokens', 'm_pre_norm', 'm_post_norm', 'm_hgrn_w_in', 'm_hgrn_lb', 'm_hgrn_out_norm', 'm_hgrn_w_out', 'm_sb_w_in', 'm_sb_w_out', 'v_meta_tokens', 'v_pre_norm', 'v_post_norm', 'v_hgrn_w_in', 'v_hgrn_lb', 'v_hgrn_out_norm', 'v_hgrn_w_out', 'v_sb_w_in', 'v_sb_w_out']
TWIN_OUTPUTS = ['loss', 'grad_x', 'grad_meta_tokens', 'grad_pre_norm', 'grad_post_norm', 'grad_hgrn_w_in', 'grad_hgrn_lb', 'grad_hgrn_out_norm', 'grad_hgrn_w_out', 'grad_sb_w_in', 'grad_sb_w_out', 'delta_meta_tokens', 'delta_pre_norm', 'delta_post_norm', 'delta_hgrn_w_in', 'delta_hgrn_lb', 'delta_hgrn_out_norm', 'delta_hgrn_w_out', 'delta_sb_w_in', 'delta_sb_w_out', 'new_m_meta_tokens', 'new_m_pre_norm', 'new_m_post_norm', 'new_m_hgrn_w_in', 'new_m_hgrn_lb', 'new_m_hgrn_out_norm', 'new_m_hgrn_w_out', 'new_m_sb_w_in', 'new_m_sb_w_out', 'new_v_meta_tokens', 'new_v_pre_norm', 'new_v_post_norm', 'new_v_hgrn_w_in', 'new_v_hgrn_lb', 'new_v_hgrn_out_norm', 'new_v_hgrn_w_out', 'new_v_sb_w_in', 'new_v_sb_w_out']
TWIN_LEAF_KINDS = {'loss': 'loss', 'grad_x': 'grad_x', 'grad_meta_tokens': 'grad_w', 'grad_pre_norm': 'grad_w', 'grad_post_norm': 'grad_w', 'grad_hgrn_w_in': 'grad_w', 'grad_hgrn_lb': 'grad_w', 'grad_hgrn_out_norm': 'grad_w', 'grad_hgrn_w_out': 'grad_w', 'grad_sb_w_in': 'grad_w', 'grad_sb_w_out': 'grad_w', 'delta_meta_tokens': 'delta_w', 'delta_pre_norm': 'delta_w', 'delta_post_norm': 'delta_w', 'delta_hgrn_w_in': 'delta_w', 'delta_hgrn_lb': 'delta_w', 'delta_hgrn_out_norm': 'delta_w', 'delta_hgrn_w_out': 'delta_w', 'delta_sb_w_in': 'delta_w', 'delta_sb_w_out': 'delta_w', 'new_m_meta_tokens': 'new_m', 'new_m_pre_norm': 'new_m', 'new_m_post_norm': 'new_m', 'new_m_hgrn_w_in': 'new_m', 'new_m_hgrn_lb': 'new_m', 'new_m_hgrn_out_norm': 'new_m', 'new_m_hgrn_w_out': 'new_m', 'new_m_sb_w_in': 'new_m', 'new_m_sb_w_out': 'new_m', 'new_v_meta_tokens': 'new_v', 'new_v_pre_norm': 'new_v', 'new_v_post_norm': 'new_v', 'new_v_hgrn_w_in': 'new_v', 'new_v_hgrn_lb': 'new_v', 'new_v_hgrn_out_norm': 'new_v', 'new_v_hgrn_w_out': 'new_v', 'new_v_sb_w_in': 'new_v', 'new_v_sb_w_out': 'new_v'}


def _forward(args):
    return _fwd_reference(*[args[k] for k in FWD_PARAMS])


def _output_shape():
    out = _jax.eval_shape(lambda: _forward(_fwd_setup_inputs(0)))
    return out.shape, out.dtype

N_MICROBATCH = 1
ADAM_LR = 0.001
ADAM_B1 = 0.9
ADAM_B2 = 0.999
ADAM_EPS = 1e-08
ADAM_WD = 0.01
ADAM_STEP = 10
PER_EXAMPLE_BATCH_AXIS = {'x': 0, 'loss_target': 0}
SHARED_INPUTS = []
_WEIGHT_DTYPES = {'meta_tokens': _jnp.float32, 'pre_norm': _jnp.float32, 'post_norm': _jnp.float32, 'hgrn_w_in': _jnp.float32, 'hgrn_lb': _jnp.float32, 'hgrn_out_norm': _jnp.float32, 'hgrn_w_out': _jnp.float32, 'sb_w_in': _jnp.float32, 'sb_w_out': _jnp.float32}
MOMENT_SCALE = {'meta_tokens': 2.439534e-02, 'pre_norm': 1.117152e+00, 'post_norm': 6.389893e+01, 'hgrn_w_in': 7.042664e-01, 'hgrn_lb': 4.099717e-01, 'hgrn_out_norm': 1.767025e+00, 'hgrn_w_out': 6.550496e-01, 'sb_w_in': 3.457607e-01, 'sb_w_out': 4.412229e-01}


def _to_microbatches(a, axis):
    t = _jnp.moveaxis(a, axis, 0)
    t = t.reshape((N_MICROBATCH, t.shape[0] // N_MICROBATCH) + t.shape[1:])
    return _jnp.moveaxis(t, 1, axis + 1)


def setup_inputs(seed: int = 0) -> dict:
    inp = _fwd_setup_inputs(seed)
    key = _jax.random.fold_in(_jax.random.key(seed), 7919)
    shape, _ = _output_shape()
    out = dict(inp)
    out["loss_target"] = _jax.random.normal(_jax.random.fold_in(key, 0), shape, _jnp.float32)
    for i, name in enumerate(TWIN_WEIGHTS):
        w = inp[name].astype(_jnp.float32)
        if MOMENT_SCALE is None:
            s = _jnp.sqrt(_jnp.mean(_jnp.square(w)) + 1e-30)
        else:
            s = MOMENT_SCALE[name]
        km, kv = _jax.random.split(_jax.random.fold_in(key, i + 1))
        out[name] = w
        out["m_" + name] = s * _jax.random.normal(km, w.shape, _jnp.float32)
        out["v_" + name] = (s * s) * _jax.random.uniform(kv, w.shape, _jnp.float32, 0.5, 1.5)
    if N_MICROBATCH > 1:
        for name, axis in PER_EXAMPLE_BATCH_AXIS.items():
            out[name] = _to_microbatches(out[name], axis)
    return {'x': out['x'], 'meta_tokens': out['meta_tokens'], 'pre_norm': out['pre_norm'], 'post_norm': out['post_norm'], 'hgrn_w_in': out['hgrn_w_in'], 'hgrn_lb': out['hgrn_lb'], 'hgrn_out_norm': out['hgrn_out_norm'], 'hgrn_w_out': out['hgrn_w_out'], 'sb_w_in': out['sb_w_in'], 'sb_w_out': out['sb_w_out'], 'loss_target': out['loss_target'], 'm_meta_tokens': out['m_meta_tokens'], 'm_pre_norm': out['m_pre_norm'], 'm_post_norm': out['m_post_norm'], 'm_hgrn_w_in': out['m_hgrn_w_in'], 'm_hgrn_lb': out['m_hgrn_lb'], 'm_hgrn_out_norm': out['m_hgrn_out_norm'], 'm_hgrn_w_out': out['m_hgrn_w_out'], 'm_sb_w_in': out['m_sb_w_in'], 'm_sb_w_out': out['m_sb_w_out'], 'v_meta_tokens': out['v_meta_tokens'], 'v_pre_norm': out['v_pre_norm'], 'v_post_norm': out['v_post_norm'], 'v_hgrn_w_in': out['v_hgrn_w_in'], 'v_hgrn_lb': out['v_hgrn_lb'], 'v_hgrn_out_norm': out['v_hgrn_out_norm'], 'v_hgrn_w_out': out['v_hgrn_w_out'], 'v_sb_w_in': out['v_sb_w_in'], 'v_sb_w_out': out['v_sb_w_out']}


def _loss(weights, diff, rest, loss_target):
    with _jax.named_scope("forward"):
        args = {**rest, TWIN_DIFF_INPUT: diff, **{k: w.astype(_WEIGHT_DTYPES[k]) for k, w in weights.items()}}
        y = _forward(args)
    with _jax.named_scope("loss_head"):
        err = _jnp.square(y.astype(_jnp.float32) - loss_target)
        return 0.5 * _jnp.sum(_jnp.mean(err, axis=-1)) if err.ndim else 0.5 * err


def _adamw(w, g, m, v):
    m = ADAM_B1 * m + (1.0 - ADAM_B1) * g
    v = ADAM_B2 * v + (1.0 - ADAM_B2) * _jnp.square(g)
    m_hat = m / (1.0 - ADAM_B1 ** ADAM_STEP)
    v_hat = v / (1.0 - ADAM_B2 ** ADAM_STEP)
    delta = -ADAM_LR * (m_hat / (_jnp.sqrt(v_hat) + ADAM_EPS) + ADAM_WD * w)
    return delta, m, v


def reference(x, meta_tokens, pre_norm, post_norm, hgrn_w_in, hgrn_lb, hgrn_out_norm, hgrn_w_out, sb_w_in, sb_w_out, loss_target, m_meta_tokens, m_pre_norm, m_post_norm, m_hgrn_w_in, m_hgrn_lb, m_hgrn_out_norm, m_hgrn_w_out, m_sb_w_in, m_sb_w_out, v_meta_tokens, v_pre_norm, v_post_norm, v_hgrn_w_in, v_hgrn_lb, v_hgrn_out_norm, v_hgrn_w_out, v_sb_w_in, v_sb_w_out):
    given = dict(x=x, meta_tokens=meta_tokens, pre_norm=pre_norm, post_norm=post_norm, hgrn_w_in=hgrn_w_in, hgrn_lb=hgrn_lb, hgrn_out_norm=hgrn_out_norm, hgrn_w_out=hgrn_w_out, sb_w_in=sb_w_in, sb_w_out=sb_w_out, loss_target=loss_target, m_meta_tokens=m_meta_tokens, m_pre_norm=m_pre_norm, m_post_norm=m_post_norm, m_hgrn_w_in=m_hgrn_w_in, m_hgrn_lb=m_hgrn_lb, m_hgrn_out_norm=m_hgrn_out_norm, m_hgrn_w_out=m_hgrn_w_out, m_sb_w_in=m_sb_w_in, m_sb_w_out=m_sb_w_out, v_meta_tokens=v_meta_tokens, v_pre_norm=v_pre_norm, v_post_norm=v_post_norm, v_hgrn_w_in=v_hgrn_w_in, v_hgrn_lb=v_hgrn_lb, v_hgrn_out_norm=v_hgrn_out_norm, v_hgrn_w_out=v_hgrn_w_out, v_sb_w_in=v_sb_w_in, v_sb_w_out=v_sb_w_out)
    weights = {n: given[n] for n in TWIN_WEIGHTS}
    shared = {n: given[n] for n in SHARED_INPUTS}
    per_example = {n: given[n] for n in ['x']}
    grad_fn = _jax.value_and_grad(_loss, argnums=(0, 1))

    def one_microbatch(ex, loss_target):
        ex = dict(ex)
        diff = ex.pop(TWIN_DIFF_INPUT)
        return grad_fn(weights, diff, {**shared, **ex}, loss_target)

    if N_MICROBATCH == 1:
        loss, (grad_w, grad_x) = one_microbatch(per_example, given["loss_target"])
    else:
        def body(carry, xs):
            loss_sum, grad_sum = carry
            l_k, (gw_k, gx_k) = one_microbatch(xs[0], xs[1])
            with _jax.named_scope("update"):
                return (loss_sum + l_k, _jax.tree.map(_jnp.add, grad_sum, gw_k)), gx_k

        init = (_jnp.zeros((), _jnp.float32), _jax.tree.map(_jnp.zeros_like, weights))
        (loss, grad_w), grad_x = _jax.lax.scan(body, init, (per_example, given["loss_target"]))
    with _jax.named_scope("update"):
        delta_w, new_m, new_v = {}, {}, {}
        for n in TWIN_WEIGHTS:
            delta_w[n], new_m[n], new_v[n] = _adamw(weights[n], grad_w[n], given["m_" + n], given["v_" + n])
    return (loss, grad_x, *[grad_w[n] for n in TWIN_WEIGHTS], *[delta_w[n] for n in TWIN_WEIGHTS],
            *[new_m[n] for n in TWIN_WEIGHTS], *[new_v[n] for n in TWIN_WEIGHTS])
```

```python
import jax
import jax.numpy as jnp
from jax import lax
from jax.experimental import pallas as pl
from jax.experimental.pallas import tpu as pltpu

F32 = jnp.float32
BF16 = jnp.bfloat16

N_DEV = 8
BLOCK = 128
N_META = 16
N_PAD = BLOCK - N_META
HEAD = 128
SUB = 16
N_SUB = BLOCK // SUB
EPS = 1e-6
ROW_TILE = 256
NEG_BIG = -1e30

ADAM_LR = 0.001
ADAM_B1 = 0.9
ADAM_B2 = 0.999
ADAM_EPS = 1e-08
ADAM_WD = 0.01
ADAM_STEP = 10

VMEM_LIMIT = 56 * 1024 * 1024


def _pcall(body, **kw):
    return pl.pallas_call(body, **kw)


def _params(*sem):
    return pltpu.CompilerParams(dimension_semantics=sem, vmem_limit_bytes=VMEM_LIMIT)


def _dot(a, b):
    return jnp.dot(a, b, preferred_element_type=F32)


def _dot_nt(a, b):
    return lax.dot_general(a, b, (((1,), (1,)), ((), ())), preferred_element_type=F32)


def _dot_tn(a, b):
    return lax.dot_general(a, b, (((0,), (0,)), ((), ())), preferred_element_type=F32)


def _split(x, pieces):
    out = []
    for _ in range(pieces):
        p = x.astype(BF16)
        out.append(p)
        x = x - p.astype(F32)
    return out


def _tri_right(x, tri, pieces=2):
    return sum(_dot(p, tri) for p in _split(x, pieces))


def _tri_left(tri, x, pieces=3):
    return sum(_dot(tri, p) for p in _split(x, pieces))


def _iota2(shape, dim):
    return lax.broadcasted_iota(jnp.int32, shape, dim)


def _tri(cond):
    return jnp.where(cond, 1.0, 0.0).astype(BF16)


def _sig_pair(x):
    e = jnp.exp(-jnp.abs(x))
    r = 1.0 / (1.0 + e)
    er = e * r
    pos = x >= 0
    return jnp.where(pos, r, er), jnp.where(pos, er, r)


def _expand(rows):
    return jnp.concatenate([jnp.broadcast_to(r, (SUB, HEAD)) for r in rows], axis=0)


def _exchange(arrays, modes, name):
    n = len(arrays)
    out_shape = []
    for a, m in zip(arrays, modes):
        shp = a.shape[1:] if m == "scatter" else a.shape
        out_shape.append(jax.ShapeDtypeStruct((N_DEV,) + tuple(shp), a.dtype))

    def body(*refs):
        ins, outs = refs[:n], refs[n:2 * n]
        send_sems, recv_sems, local_sems = refs[2 * n:]
        mx, my, mc = lax.axis_index("x"), lax.axis_index("y"), lax.axis_index("c")
        me = 4 * mx + 2 * my + mc

        def src(i, slot):
            return ins[i].at[slot] if modes[i] == "scatter" else ins[i]

        def peer_of(mask):
            px = 1 - mx if mask & 4 else mx
            py = 1 - my if mask & 2 else my
            pc = 1 - mc if mask & 1 else mc
            return px, py, pc

        def copy(i, mask, dst_slot):
            px, py, pc = peer_of(mask)
            return pltpu.make_async_remote_copy(
                src_ref=src(i, 4 * px + 2 * py + pc), dst_ref=outs[i].at[dst_slot],
                send_sem=send_sems.at[i, mask - 1], recv_sem=recv_sems.at[i, mask - 1],
                device_id=(px, py, pc), device_id_type=pl.DeviceIdType.MESH)

        sends = []
        for mask in range(1, N_DEV):
            for i in range(n):
                cp = copy(i, mask, me)
                cp.start()
                sends.append(cp)
        own = []
        for i in range(n):
            cp = pltpu.make_async_copy(src(i, me), outs[i].at[me], local_sems.at[i])
            cp.start()
            own.append(cp)
        for mask in range(1, N_DEV):
            px, py, pc = peer_of(mask)
            for i in range(n):
                copy(i, mask, 4 * px + 2 * py + pc).wait_recv()
        for cp in sends:
            cp.wait_send()
        for cp in own:
            cp.wait()

    any_spec = pl.BlockSpec(memory_space=pl.ANY)
    return _pcall(
        body, name=name, out_shape=tuple(out_shape),
        in_specs=[any_spec] * n, out_specs=tuple([any_spec] * n),
        scratch_shapes=[pltpu.SemaphoreType.DMA((n, N_DEV - 1)), pltpu.SemaphoreType.DMA((n, N_DEV - 1)),
                        pltpu.SemaphoreType.DMA((n,))],
    )(*arrays)


def _norm_inproj(h, wnorm, w_all, main_dtype, name):
    t, d = h.shape
    p_n, _, s = w_all.shape
    n_main = 3 * d // s
    tm = ROW_TILE

    def body(h_ref, wn_ref, w_ref, main_ref, gate_ref, yn_ref):
        x = h_ref[...]
        y = x * lax.rsqrt(jnp.mean(x * x, axis=-1, keepdims=True) + EPS) * wn_ref[...]
        yb = y.astype(BF16)
        yn_ref[...] = yb
        for p in range(p_n):
            r = _dot(yb, w_ref[p])
            if p < n_main:
                main_ref[:, p * s:(p + 1) * s] = r.astype(main_dtype)
            else:
                gate_ref[:, (p - n_main) * s:(p - n_main + 1) * s] = r

    return _pcall(
        body, name=name, grid=(t // tm,),
        in_specs=[pl.BlockSpec((tm, d), lambda i: (i, 0)), pl.BlockSpec((1, d), lambda i: (0, 0)),
                  pl.BlockSpec((p_n, d, s), lambda i: (0, 0, 0))],
        out_specs=(pl.BlockSpec((tm, 3 * d), lambda i: (i, 0)), pl.BlockSpec((tm, d), lambda i: (i, 0)),
                   pl.BlockSpec((tm, d), lambda i: (i, 0))),
        out_shape=(jax.ShapeDtypeStruct((t, 3 * d), main_dtype), jax.ShapeDtypeStruct((t, d), F32),
                   jax.ShapeDtypeStruct((t, d), BF16)),
        compiler_params=_params("parallel"),
    )(h, wnorm, w_all)


def _mix_out(o, gate, h_in, w_out, w_on, w_post, head_norm, name):
    t, d = o.shape
    n_heads = d // HEAD
    tm = ROW_TILE

    def body(o_ref, g_ref, h_ref, w_ref, won_ref, wp_ref, hout_ref, y_ref, u_scr):
        for hh in range(n_heads):
            cs = slice(hh * HEAD, (hh + 1) * HEAD)
            oh = o_ref[:, cs]
            gt = g_ref[:, cs]
            if head_norm:
                oh = oh * lax.rsqrt(jnp.mean(oh * oh, axis=-1, keepdims=True) + EPS) * won_ref[...]
            u_scr[:, cs] = (oh * (gt * jax.nn.sigmoid(gt))).astype(BF16)
        y = _dot(u_scr[...], w_ref[...])
        y_ref[...] = y
        r = y * lax.rsqrt(jnp.mean(y * y, axis=-1, keepdims=True) + EPS) * wp_ref[...]
        hout_ref[...] = h_ref[...] + r

    row = pl.BlockSpec((tm, d), lambda i: (i, 0))
    return _pcall(
        body, name=name, grid=(t // tm,),
        in_specs=[row, row, row, pl.BlockSpec((d, d), lambda i: (0, 0)),
                  pl.BlockSpec((1, HEAD), lambda i: (0, 0)), pl.BlockSpec((1, d), lambda i: (0, 0))],
        out_specs=(row, row),
        out_shape=(jax.ShapeDtypeStruct((t, d), F32), jax.ShapeDtypeStruct((t, d), F32)),
        scratch_shapes=[pltpu.VMEM((tm, d), BF16)],
        compiler_params=_params("parallel"),
    )(o, gate, h_in, w_out, w_on, w_post)


def _loss_grad(h_out, target, n_blocks, name):
    t, d = h_out.shape

    def body(h_ref, t_ref, dh_ref, loss_ref):
        i = pl.program_id(0)

        @pl.when(i == 0)
        def _():
            loss_ref[...] = jnp.zeros_like(loss_ref)

        real = (i % n_blocks) > 0
        err = jnp.where(real, h_ref[...] - t_ref[...], 0.0)
        dh_ref[...] = err * (1.0 / d)
        part = jnp.sum(jnp.sum(err * err, axis=-1, keepdims=True), axis=0, keepdims=True)
        loss_ref[...] += part * (0.5 / d)

    return _pcall(
        body, name=name, grid=(t // BLOCK,),
        in_specs=[pl.BlockSpec((BLOCK, d), lambda i: (i, 0)),
                  pl.BlockSpec((None, BLOCK, d), lambda i: (i // n_blocks, jnp.maximum(i % n_blocks - 1, 0), 0))],
        out_specs=(pl.BlockSpec((BLOCK, d), lambda i: (i, 0)), pl.BlockSpec((8, 128), lambda i: (0, 0))),
        out_shape=(jax.ShapeDtypeStruct((t, d), F32), jax.ShapeDtypeStruct((8, 128), F32)),
        compiler_params=_params("arbitrary"),
    )(h_out, target)


def _mix_out_bwd(dh, y, o, gate, w_out, w_on, w_post, head_norm, name):
    t, d = o.shape
    n_heads = d // HEAD
    tm = ROW_TILE
    last = t // tm - 1

    def body(dh_ref, y_ref, o_ref, g_ref, w_ref, won_ref, wp_ref,
             do_ref, dg_ref, dw_ref, dwp_ref, dwon_ref, u_scr, acc):
        i = pl.program_id(0)

        @pl.when(i == 0)
        def _():
            acc[...] = jnp.zeros_like(acc)
            dwp_ref[...] = jnp.zeros_like(dwp_ref)
            dwon_ref[...] = jnp.zeros_like(dwon_ref)

        yv = y_ref[...]
        rs = lax.rsqrt(jnp.mean(yv * yv, axis=-1, keepdims=True) + EPS)
        yh = yv * rs
        dr = dh_ref[...]
        dwp_ref[...] += jnp.sum(dr * yh, axis=0, keepdims=True)
        wd = dr * wp_ref[...]
        dy = rs * (wd - yh * jnp.mean(wd * yh, axis=-1, keepdims=True))
        dyb = dy.astype(BF16)
        du = _dot_nt(dyb, w_ref[...])
        for hh in range(n_heads):
            cs = slice(hh * HEAD, (hh + 1) * HEAD)
            oh = o_ref[:, cs]
            gt = g_ref[:, cs]
            sg = jax.nn.sigmoid(gt)
            sl = gt * sg
            duh = du[:, cs]
            if head_norm:
                rsh = lax.rsqrt(jnp.mean(oh * oh, axis=-1, keepdims=True) + EPS)
                ohat = oh * rsh
                on = ohat * won_ref[...]
            else:
                on = oh
            u_scr[:, cs] = (on * sl).astype(BF16)
            don = duh * sl
            dg_ref[:, cs] = (duh * on * (sg * (1.0 + gt * (1.0 - sg)))).astype(BF16)
            if head_norm:
                dwon_ref[...] += jnp.sum(don * ohat, axis=0, keepdims=True)
                wdn = don * won_ref[...]
                do_ref[:, cs] = rsh * (wdn - ohat * jnp.mean(wdn * ohat, axis=-1, keepdims=True))
            else:
                do_ref[:, cs] = don
        acc[...] += _dot_tn(u_scr[...], dyb)

        @pl.when(i == last)
        def _():
            dw_ref[...] = acc[...].astype(BF16)

    row = pl.BlockSpec((tm, d), lambda i: (i, 0))
    return _pcall(
        body, name=name, grid=(t // tm,),
        in_specs=[row, row, row, row, pl.BlockSpec((d, d), lambda i: (0, 0)),
                  pl.BlockSpec((1, HEAD), lambda i: (0, 0)), pl.BlockSpec((1, d), lambda i: (0, 0))],
        out_specs=(row, row, pl.BlockSpec((d, d), lambda i: (0, 0)), pl.BlockSpec((1, d), lambda i: (0, 0)),
                   pl.BlockSpec((1, HEAD), lambda i: (0, 0))),
        out_shape=(jax.ShapeDtypeStruct((t, d), F32), jax.ShapeDtypeStruct((t, d), BF16),
                   jax.ShapeDtypeStruct((d, d), BF16), jax.ShapeDtypeStruct((1, d), F32),
                   jax.ShapeDtypeStruct((1, HEAD), F32)),
        scratch_shapes=[pltpu.VMEM((tm, d), BF16), pltpu.VMEM((d, d), F32)],
        compiler_params=_params("arbitrary"),
    )(dh, y, o, gate, w_out, w_on, w_post)


def _inproj_bwd_x(dmain, dgate, w_all, h, wnorm, dres, name):
    t, d = h.shape
    p_n, _, s = w_all.shape
    n_main = 3 * d // s
    tm = ROW_TILE

    def body(dm_ref, dg_ref, w_ref, h_ref, wn_ref, dres_ref, dh_ref, dwn_ref):
        i = pl.program_id(0)

        @pl.when(i == 0)
        def _():
            dwn_ref[...] = jnp.zeros_like(dwn_ref)

        dyn = jnp.zeros((tm, d), F32)
        for p in range(p_n):
            if p < n_main:
                blk = dm_ref[:, p * s:(p + 1) * s]
            else:
                blk = dg_ref[:, (p - n_main) * s:(p - n_main + 1) * s]
            dyn = dyn + _dot_nt(blk, w_ref[p])
        x = h_ref[...]
        rs = lax.rsqrt(jnp.mean(x * x, axis=-1, keepdims=True) + EPS)
        xh = x * rs
        dwn_ref[...] += jnp.sum(dyn * xh, axis=0, keepdims=True)
        wd = dyn * wn_ref[...]
        dh_ref[...] = dres_ref[...] + rs * (wd - xh * jnp.mean(wd * xh, axis=-1, keepdims=True))

    row = pl.BlockSpec((tm, d), lambda i: (i, 0))
    return _pcall(
        body, name=name, grid=(t // tm,),
        in_specs=[pl.BlockSpec((tm, 3 * d), lambda i: (i, 0)), row, pl.BlockSpec((p_n, d, s), lambda i: (0, 0, 0)),
                  row, pl.BlockSpec((1, d), lambda i: (0, 0)), row],
        out_specs=(row, pl.BlockSpec((1, d), lambda i: (0, 0))),
        out_shape=(jax.ShapeDtypeStruct((t, d), F32), jax.ShapeDtypeStruct((1, d), F32)),
        compiler_params=_params("arbitrary"),
    )(dmain, dgate, w_all, h, wnorm, dres)


def _inproj_bwd_w(yn, dproj, s, name):
    t, d = yn.shape
    n_sh = dproj.shape[1] // s
    tm = ROW_TILE
    last = t // tm - 1

    def body(yn_ref, dp_ref, dw_ref, acc):
        i = pl.program_id(1)

        @pl.when(i == 0)
        def _():
            acc[...] = jnp.zeros_like(acc)

        acc[...] += _dot_tn(yn_ref[...], dp_ref[...])

        @pl.when(i == last)
        def _():
            dw_ref[...] = acc[...].astype(BF16)

    return _pcall(
        body, name=name, grid=(n_sh, t // tm),
        in_specs=[pl.BlockSpec((tm, d), lambda p, i: (i, 0)), pl.BlockSpec((tm, s), lambda p, i: (i, p))],
        out_specs=pl.BlockSpec((None, d, s), lambda p, i: (p, 0, 0)),
        out_shape=jax.ShapeDtypeStruct((n_sh, d, s), BF16),
        scratch_shapes=[pltpu.VMEM((d, s), F32)],
        compiler_params=_params("parallel", "arbitrary"),
    )(yn, dproj)


def _hgrn_common(q_ref, fz_ref, lb, b_scr, k_scr):
    fz = fz_ref[...]
    sig, nsig = _sig_pair(fz)
    f = lb + (1.0 - lb) * sig
    k = (1.0 - lb) * nsig
    rr, cc = _iota2((BLOCK, BLOCK), 0), _iota2((BLOCK, BLOCK), 1)
    b = _tri_left(_tri(cc <= rr), jnp.log(f))
    b_scr[...] = b
    k_scr[...] = k
    bend = [b_scr[pl.ds(SUB * j + SUB - 1, 1), :] for j in range(N_SUB)]
    bref = [jnp.zeros((1, HEAD), F32)] + bend[:-1]
    refrow, bendrow = _expand(bref), _expand(bend)
    e_q = jnp.exp(b - refrow)
    e_k = jnp.exp(bendrow - b)
    qt = q_ref[...] * e_q
    kh = k * e_k
    bl = bend[-1]
    return dict(sig=sig, nsig=nsig, f=f, k=k, b=b, bend=bend, bref=bref, refrow=refrow,
                e_q=e_q, e_k=e_k, qt=qt, kh=kh, bl=bl)


def _hgrn_fwd(main, lbrow, bsz, n_blocks, name):
    t, d3 = main.shape
    d = d3 // 3
    n_heads = d // HEAD

    def body(q_ref, fz_ref, v_ref, lb_ref, o_ref, st_ref, s_scr, b_scr, k_scr, o_acc):
        n = pl.program_id(2)

        @pl.when(n == 0)
        def _():
            s_scr[...] = jnp.zeros_like(s_scr)

        c = _hgrn_common(q_ref, fz_ref, lb_ref[...], b_scr, k_scr)
        s_t = s_scr[...]
        st_ref[...] = s_t
        v = v_ref[...]
        vb = v.astype(BF16)
        q_state = c["qt"] * _expand([jnp.exp(r) for r in c["bref"]])
        o_acc[...] = _dot_nt(q_state.astype(BF16), s_t.astype(BF16))
        for j in range(N_SUB - 1):
            r0 = SUB * (j + 1)
            dj = jnp.exp(c["refrow"][r0:] - c["bend"][j])
            lhs = (c["qt"][r0:] * dj).astype(BF16)
            a_j = _dot_nt(lhs, c["kh"][SUB * j:SUB * (j + 1)].astype(BF16))
            o_acc[r0:, :] += _dot(a_j.astype(BF16), vb[SUB * j:SUB * (j + 1)])
        t_loc = _iota2((SUB, HEAD), 0)
        for i in range(N_SUB):
            r0 = SUB * i
            qi = q_ref[pl.ds(r0, SUB), :]
            bi = b_scr[pl.ds(r0, SUB), :]
            oi = jnp.zeros((SUB, HEAD), F32)
            for s in range(SUB):
                brow = b_scr[pl.ds(r0 + s, 1), :]
                krow = k_scr[pl.ds(r0 + s, 1), :]
                vrow = v_ref[pl.ds(r0 + s, 1), :]
                w = jnp.exp(jnp.where(t_loc >= s, bi - brow, NEG_BIG))
                col = jnp.sum(qi * krow * w, axis=-1, keepdims=True)
                oi = oi + col * vrow
            o_acc[pl.ds(r0, SUB), :] += oi
        o_ref[...] = o_acc[...]
        k_state = c["kh"] * _expand([jnp.exp(c["bl"] - r) for r in c["bend"]])
        s_scr[...] = s_t * jnp.exp(c["bl"]) + _dot_tn(vb, k_state.astype(BF16))

    def blk(col0):
        return pl.BlockSpec((BLOCK, HEAD), lambda b, h, n: (b * n_blocks + n, col0 + h))

    return _pcall(
        body, name=name, grid=(bsz, n_heads, n_blocks),
        in_specs=[blk(0), blk(n_heads), blk(2 * n_heads), pl.BlockSpec((1, HEAD), lambda b, h, n: (0, h))],
        out_specs=(blk(0), pl.BlockSpec((None, HEAD, HEAD), lambda b, h, n: ((b * n_heads + h) * n_blocks + n, 0, 0))),
        out_shape=(jax.ShapeDtypeStruct((t, d), F32),
                   jax.ShapeDtypeStruct((bsz * n_heads * n_blocks, HEAD, HEAD), F32)),
        scratch_shapes=[pltpu.VMEM((HEAD, HEAD), F32), pltpu.VMEM((BLOCK, HEAD), F32),
                        pltpu.VMEM((BLOCK, HEAD), F32), pltpu.VMEM((BLOCK, HEAD), F32)],
        compiler_params=_params("parallel", "parallel", "arbitrary"),
    )(main, main, main, lbrow)


def _hgrn_bwd(main, lbrow, states, do, bsz, n_blocks, name):
    t, d3 = main.shape
    d = d3 // 3
    n_heads = d // HEAD

    def body(q_ref, fz_ref, v_ref, lb_ref, st_ref, do_ref, dq_ref, dfz_ref, dv_ref, dlb_ref,
             ds_scr, b_scr, k_scr, dqt_acc, dkh_acc, dv_acc, dqd_acc, dkd_acc):
        step = pl.program_id(2)
        n = n_blocks - 1 - step

        @pl.when(step == 0)
        def _():
            ds_scr[...] = jnp.zeros_like(ds_scr)
            dlb_ref[...] = jnp.zeros_like(dlb_ref)

        lb = lb_ref[...]
        c = _hgrn_common(q_ref, fz_ref, lb, b_scr, k_scr)
        q, k, b = q_ref[...], c["k"], c["b"]
        v = v_ref[...]
        vb = v.astype(BF16)
        dout = do_ref[...]
        dob = dout.astype(BF16)
        s0_t = st_ref[...]
        ds1_t = ds_scr[...]
        e_ref = _expand([jnp.exp(r) for r in c["bref"]])
        e_end = _expand([jnp.exp(c["bl"] - r) for r in c["bend"]])
        e_bl = jnp.exp(c["bl"])
        q_state = c["qt"] * e_ref
        k_state = c["kh"] * e_end
        dqt_acc[...] = _dot(dob, s0_t.astype(BF16)) * e_ref
        dk_state = _dot(vb, ds1_t.astype(BF16))
        dkh_acc[...] = dk_state * e_end
        dv_acc[...] = _dot_nt(k_state.astype(BF16), ds1_t.astype(BF16))
        dbl = (jnp.sum(s0_t * ds1_t, axis=0, keepdims=True) * e_bl
               + jnp.sum(k_state * dk_state, axis=0, keepdims=True))
        ds_scr[...] = ds1_t * e_bl + _dot_tn(dob, q_state.astype(BF16))
        for j in range(N_SUB - 1):
            r0 = SUB * (j + 1)
            js = slice(SUB * j, SUB * (j + 1))
            dj = jnp.exp(c["refrow"][r0:] - c["bend"][j])
            lhs = (c["qt"][r0:] * dj).astype(BF16)
            khj = c["kh"][js].astype(BF16)
            a_j = _dot_nt(lhs, khj)
            da_j = _dot_nt(dob[r0:], vb[js]).astype(BF16)
            dv_acc[js, :] += _dot_tn(a_j.astype(BF16), dob[r0:])
            dqt_acc[r0:, :] += _dot(da_j, khj) * dj
            dkh_acc[js, :] += _dot_tn(da_j, lhs)
        t_loc = _iota2((SUB, HEAD), 0)
        for i in range(N_SUB):
            r0 = SUB * i
            qi = q_ref[pl.ds(r0, SUB), :]
            bi = b_scr[pl.ds(r0, SUB), :]
            doi = do_ref[pl.ds(r0, SUB), :]
            dqi = jnp.zeros((SUB, HEAD), F32)
            for s in range(SUB):
                brow = b_scr[pl.ds(r0 + s, 1), :]
                krow = k_scr[pl.ds(r0 + s, 1), :]
                vrow = v_ref[pl.ds(r0 + s, 1), :]
                w = jnp.exp(jnp.where(t_loc >= s, bi - brow, NEG_BIG))
                qw = qi * w
                a_col = jnp.sum(qw * krow, axis=-1, keepdims=True)
                da_col = jnp.sum(doi * vrow, axis=-1, keepdims=True)
                dv_acc[pl.ds(r0 + s, 1), :] += jnp.sum(a_col * doi, axis=0, keepdims=True)
                dqi = dqi + da_col * (w * krow)
                dkd_acc[pl.ds(r0 + s, 1), :] = jnp.sum(da_col * qw, axis=0, keepdims=True)
            dqd_acc[pl.ds(r0, SUB), :] = dqi
        dq = dqt_acc[...] * c["e_q"] + dqd_acc[...]
        dk = dkh_acc[...] * c["e_k"] + dkd_acc[...]
        rr, cc = _iota2((BLOCK, BLOCK), 0), _iota2((BLOCK, BLOCK), 1)
        db = q * dq - k * dk + jnp.where(_iota2((BLOCK, HEAD), 0) == BLOCK - 1, dbl, 0.0)
        dg = _tri_left(_tri(cc >= rr), db)
        real = jnp.logical_or(n > 0, _iota2((BLOCK, HEAD), 0) >= N_PAD)
        df = jnp.where(real, dg / c["f"] - dk, 0.0)
        dq_ref[...] = dq.astype(BF16)
        dv_ref[...] = dv_acc[...].astype(BF16)
        dfz_ref[...] = (df * (1.0 - lb) * c["sig"] * c["nsig"]).astype(BF16)
        dlb_ref[...] += jnp.sum(df * c["nsig"], axis=0, keepdims=True)

    def blk(col0):
        return pl.BlockSpec((BLOCK, HEAD), lambda b, h, s: (b * n_blocks + n_blocks - 1 - s, col0 + h))

    return _pcall(
        body, name=name, grid=(bsz, n_heads, n_blocks),
        in_specs=[blk(0), blk(n_heads), blk(2 * n_heads), pl.BlockSpec((1, HEAD), lambda b, h, s: (0, h)),
                  pl.BlockSpec((None, HEAD, HEAD),
                               lambda b, h, s: ((b * n_heads + h) * n_blocks + n_blocks - 1 - s, 0, 0)),
                  blk(0)],
        out_specs=(blk(0), blk(0), blk(0), pl.BlockSpec((None, 1, HEAD), lambda b, h, s: (b, 0, h))),
        out_shape=(jax.ShapeDtypeStruct((t, d), BF16),) * 3 + (jax.ShapeDtypeStruct((bsz, 1, d), F32),),
        scratch_shapes=[pltpu.VMEM((HEAD, HEAD), F32)] + [pltpu.VMEM((BLOCK, HEAD), F32)] * 7,
        compiler_params=_params("parallel", "parallel", "arbitrary"),
    )(main, main, main, lbrow, states, do)


def _sb_scores(q, ks, n, m, scale):
    z = _dot_nt(q, ks) * scale
    t_idx = n * BLOCK + _iota2((BLOCK, BLOCK), 0)
    s_idx = m * BLOCK + _iota2((BLOCK, BLOCK), 1)
    valid = jnp.logical_and(s_idx < t_idx, s_idx >= N_PAD)
    l1p = jnp.log1p(jnp.exp(-jnp.abs(z)))
    log_beta = jnp.minimum(z, 0.0) - l1p
    log_keep = jnp.where(valid, -jnp.maximum(z, 0.0) - l1p, 0.0)
    return z, valid, log_beta, log_keep


def _sb_fwd(qkv, bsz, n_blocks, name):
    t, d3 = qkv.shape
    d = d3 // 3
    n_heads = d // HEAD
    lp = n_blocks * BLOCK
    scale = HEAD ** -0.5

    def body(q_ref, k_ref, v_ref, o_ref, tot_ref):
        n = pl.program_id(2)
        q = q_ref[...]
        upper = _tri(_iota2((BLOCK, BLOCK), 0) > _iota2((BLOCK, BLOCK), 1))

        def step(i, carry):
            c, acc = carry
            m = n - i
            rows = pl.ds(pl.multiple_of(m * BLOCK, BLOCK), BLOCK)
            _, valid, log_beta, log_keep = _sb_scores(q, k_ref[rows, :], n, m, scale)
            later = c + _tri_right(log_keep, upper)
            a = jnp.where(valid, jnp.exp(log_beta + later), 0.0)
            acc = acc + _dot(a.astype(BF16), v_ref[rows, :])
            return c + jnp.sum(log_keep, axis=-1, keepdims=True), acc

        c, acc = lax.fori_loop(0, n + 1, step, (jnp.zeros((BLOCK, 1), F32), jnp.zeros((BLOCK, HEAD), F32)))
        o_ref[...] = acc
        tot_ref[...] = jnp.broadcast_to(c, (BLOCK, HEAD))

    qblk = pl.BlockSpec((BLOCK, HEAD), lambda b, h, n: (b * n_blocks + n, h))
    return _pcall(
        body, name=name, grid=(bsz, n_heads, n_blocks),
        in_specs=[qblk, pl.BlockSpec((lp, HEAD), lambda b, h, n: (b, n_heads + h)),
                  pl.BlockSpec((lp, HEAD), lambda b, h, n: (b, 2 * n_heads + h))],
        out_specs=(qblk, qblk),
        out_shape=(jax.ShapeDtypeStruct((t, d), F32), jax.ShapeDtypeStruct((t, d), F32)),
        compiler_params=_params("parallel", "parallel", "arbitrary"),
    )(qkv, qkv, qkv)


def _sb_bwd(qkv, do, tot, bsz, n_blocks, name):
    t, d3 = qkv.shape
    d = d3 // 3
    n_heads = d // HEAD
    lp = n_blocks * BLOCK
    scale = HEAD ** -0.5

    def body(q_ref, k_ref, v_ref, do_ref, tot_ref, dq_ref, dk_ref, dv_ref, dk_acc, dv_acc):
        n = pl.program_id(2)

        @pl.when(n == 0)
        def _():
            dk_acc[...] = jnp.zeros_like(dk_acc)
            dv_acc[...] = jnp.zeros_like(dv_acc)

        q = q_ref[...]
        dob = do_ref[...].astype(BF16)
        tot_c = tot_ref[:, 0:1]
        rr, cc = _iota2((BLOCK, BLOCK), 0), _iota2((BLOCK, BLOCK), 1)
        incl = _tri(rr <= cc)
        excl = _tri(rr < cc)

        def step(m, carry):
            p, e, dq = carry
            rows = pl.ds(pl.multiple_of(m * BLOCK, BLOCK), BLOCK)
            ks, vs = k_ref[rows, :], v_ref[rows, :]
            z, valid, log_beta, log_keep = _sb_scores(q, ks, n, m, scale)
            later = tot_c - p - _tri_right(log_keep, incl)
            a = jnp.where(valid, jnp.exp(log_beta + later), 0.0)
            g = _dot_nt(dob, vs) * a
            dv_acc[rows, :] += _dot_tn(a.astype(BF16), dob)
            cum = e + _tri_right(g, excl)
            sig, nsig = _sig_pair(z)
            dz = (jnp.where(valid, g * nsig - cum * sig, 0.0) * scale).astype(BF16)
            dk_acc[rows, :] += _dot_tn(dz, q)
            return (p + jnp.sum(log_keep, axis=-1, keepdims=True), e + jnp.sum(g, axis=-1, keepdims=True),
                    dq + _dot(dz, ks))

        zero_c = jnp.zeros((BLOCK, 1), F32)
        _, _, dq = lax.fori_loop(0, n + 1, step, (zero_c, zero_c, jnp.zeros((BLOCK, HEAD), F32)))
        dq_ref[...] = dq.astype(BF16)

        @pl.when(n == n_blocks - 1)
        def _():
            dk_ref[...] = dk_acc[...].astype(BF16)
            dv_ref[...] = dv_acc[...].astype(BF16)

    qblk = pl.BlockSpec((BLOCK, HEAD), lambda b, h, n: (b * n_blocks + n, h))
    kblk = pl.BlockSpec((lp, HEAD), lambda b, h, n: (b, n_heads + h))
    vblk = pl.BlockSpec((lp, HEAD), lambda b, h, n: (b, 2 * n_heads + h))
    hblk = pl.BlockSpec((lp, HEAD), lambda b, h, n: (b, h))
    return _pcall(
        body, name=name, grid=(bsz, n_heads, n_blocks),
        in_specs=[qblk, kblk, vblk, qblk, qblk],
        out_specs=(qblk, hblk, hblk),
        out_shape=(jax.ShapeDtypeStruct((t, d), BF16),) * 3,
        scratch_shapes=[pltpu.VMEM((lp, HEAD), F32), pltpu.VMEM((lp, HEAD), F32)],
        compiler_params=_params("parallel", "parallel", "arbitrary"),
    )(qkv, qkv, qkv, do, tot)


def _adamw(w, g, m, v):
    m = ADAM_B1 * m + (1.0 - ADAM_B1) * g
    v = ADAM_B2 * v + (1.0 - ADAM_B2) * (g * g)
    m_hat = m / (1.0 - ADAM_B1 ** ADAM_STEP)
    v_hat = v / (1.0 - ADAM_B2 ** ADAM_STEP)
    delta = -ADAM_LR * (m_hat / (jnp.sqrt(v_hat) + ADAM_EPS) + ADAM_WD * w)
    return delta, m, v


def _update_sharded(w, parts, m, v, name):
    r, c = w.shape
    tr = min(r, ROW_TILE)

    def body(w_ref, p_ref, m_ref, v_ref, g_ref, d_ref, nm_ref, nv_ref):
        g = p_ref[0].astype(F32)
        for q in range(1, N_DEV):
            g = g + p_ref[q].astype(F32)
        g_ref[...] = g
        d_ref[...], nm_ref[...], nv_ref[...] = _adamw(w_ref[...], g, m_ref[...], v_ref[...])

    row = pl.BlockSpec((tr, c), lambda i: (i, 0))
    return _pcall(
        body, name=name, grid=(r // tr,),
        in_specs=[row, pl.BlockSpec((N_DEV, tr, c), lambda i: (0, i, 0)), row, row],
        out_specs=(row,) * 4, out_shape=(jax.ShapeDtypeStruct((r, c), F32),) * 4,
        compiler_params=_params("parallel"),
    )(w, parts, m, v)


SMALL_ROWS = 8


def _pack_small(dpre0, dpre1, dpost0, dpost1, dlb, dwon, loss, name):
    d = dpre0.shape[1]
    bsz = dlb.shape[0]

    def body(a0, a1, p0, p1, lb_ref, on_ref, loss_ref, out_ref):
        out_ref[...] = jnp.zeros_like(out_ref)
        out_ref[pl.ds(0, 1), :] = a0[...]
        out_ref[pl.ds(1, 1), :] = a1[...]
        out_ref[pl.ds(2, 1), :] = p0[...]
        out_ref[pl.ds(3, 1), :] = p1[...]
        acc = lb_ref[0]
        for b in range(1, bsz):
            acc = acc + lb_ref[b]
        out_ref[pl.ds(4, 1), :] = acc
        out_ref[pl.ds(5, 1), pl.ds(0, HEAD)] = on_ref[...]
        out_ref[pl.ds(6, 1), pl.ds(0, HEAD)] = loss_ref[pl.ds(0, 1), :]

    return _pcall(body, name=name, out_shape=jax.ShapeDtypeStruct((SMALL_ROWS, d), F32))(
        dpre0, dpre1, dpost0, dpost1, dlb, dwon, loss)


def _update_small(parts, pre, post, lbw, on, moments, name):
    d = pre.shape[1]

    def body(p_ref, pre_ref, post_ref, lbw_ref, on_ref, mpre, mpost, mlb, mon, vpre, vpost, vlb, von,
             loss_ref, *outs):
        def total(r0, nr, width):
            acc = p_ref[0, pl.ds(r0, nr), pl.ds(0, width)]
            for q in range(1, N_DEV):
                acc = acc + p_ref[q, pl.ds(r0, nr), pl.ds(0, width)]
            return acc

        def put(k, w, g, m, v):
            dl, nm, nv = _adamw(w, g, m, v)
            outs[4 * k][...] = g
            outs[4 * k + 1][...] = dl
            outs[4 * k + 2][...] = nm
            outs[4 * k + 3][...] = nv

        put(0, pre_ref[...], total(0, 2, d), mpre[...], vpre[...])
        put(1, post_ref[...], total(2, 2, d), mpost[...], vpost[...])
        a0, a1 = lbw_ref[pl.ds(0, 1), :], lbw_ref[pl.ds(1, 1), :]
        mx = jnp.maximum(a0, a1)
        e0, e1 = jnp.exp(a0 - mx), jnp.exp(a1 - mx)
        p0 = e0 / (e0 + e1)
        g0 = total(4, 1, d) * p0 * (1.0 - p0)
        for r, w, g in ((0, a0, g0), (1, a1, -g0)):
            row = pl.ds(r, 1)
            dl, nm, nv = _adamw(w, g, mlb[row, :], vlb[row, :])
            outs[8][row, :] = g
            outs[9][row, :] = dl
            outs[10][row, :] = nm
            outs[11][row, :] = nv
        put(3, on_ref[...], total(5, 1, HEAD), mon[...], von[...])
        loss_ref[...] = jnp.broadcast_to(total(6, 1, HEAD), loss_ref.shape)

    shapes = []
    for w in (pre, post, lbw, on):
        shapes += [jax.ShapeDtypeStruct(w.shape, F32)] * 4
    return _pcall(body, name=name, out_shape=(jax.ShapeDtypeStruct((8, 128), F32), *shapes))(
        parts, pre, post, lbw, on, *moments)


def kernel(x, meta_tokens, pre_norm, post_norm, hgrn_w_in, hgrn_lb, hgrn_out_norm, hgrn_w_out, sb_w_in, sb_w_out, loss_target, m_meta_tokens, m_pre_norm, m_post_norm, m_hgrn_w_in, m_hgrn_lb, m_hgrn_out_norm, m_hgrn_w_out, m_sb_w_in, m_sb_w_out, v_meta_tokens, v_pre_norm, v_post_norm, v_hgrn_w_in, v_hgrn_lb, v_hgrn_out_norm, v_hgrn_w_out, v_sb_w_in, v_sb_w_out):
    bsz, seq, d = x.shape
    n_blocks = seq // BLOCK + 1
    lp = n_blocks * BLOCK
    t = bsz * lp
    s = hgrn_w_in.shape[2]
    dsh = d // N_DEV

    w_in_h, w_out_h, w_in_s, w_out_s, meta_all = _exchange(
        [hgrn_w_in[0].astype(BF16), hgrn_w_out[0].astype(BF16), sb_w_in[0].astype(BF16),
         sb_w_out[0].astype(BF16), meta_tokens],
        ["gather"] * 5, "gather_weights")
    w_out_h = w_out_h.reshape(d, d)
    w_out_s = w_out_s.reshape(d, d)
    meta_full = jnp.transpose(meta_all, (1, 0, 2)).reshape(N_META, d)

    h0 = jnp.concatenate(
        [jnp.zeros((bsz, N_PAD, d), F32), jnp.broadcast_to(meta_full[None], (bsz, N_META, d)), x], axis=1
    ).reshape(t, d)
    lbrow = jnp.cumsum(jax.nn.softmax(hgrn_lb, axis=0), axis=0)[0:1]

    main0, gate0, yn0 = _norm_inproj(h0, pre_norm[0:1], w_in_h, F32, "inproj_hgrn")
    o0, states = _hgrn_fwd(main0, lbrow, bsz, n_blocks, "hgrn_fwd")
    h1, y0 = _mix_out(o0, gate0, h0, w_out_h, hgrn_out_norm, post_norm[0:1], True, "mix_out_hgrn")
    main1, gate1, yn1 = _norm_inproj(h1, pre_norm[1:2], w_in_s, BF16, "inproj_sb")
    o1, tot = _sb_fwd(main1, bsz, n_blocks, "sb_fwd")
    h2, y1 = _mix_out(o1, gate1, h1, w_out_s, hgrn_out_norm, post_norm[1:2], False, "mix_out_sb")
    dh2, loss_part = _loss_grad(h2, loss_target, n_blocks, "loss_grad")

    do1, dgate1, dw_out_s, dpost1, _ = _mix_out_bwd(
        dh2, y1, o1, gate1, w_out_s, hgrn_out_norm, post_norm[1:2], False, "mix_out_sb_bwd")
    dq1, dk1, dv1 = _sb_bwd(main1, do1, tot, bsz, n_blocks, "sb_bwd")
    dmain1 = jnp.concatenate([dq1, dk1, dv1], axis=1)
    dh1, dpre1 = _inproj_bwd_x(dmain1, dgate1, w_in_s, h1, pre_norm[1:2], dh2, "inproj_sb_bwd_x")
    dw_in_s = jnp.concatenate([_inproj_bwd_w(yn1, dmain1, s, "inproj_sb_bwd_w_main"),
                               _inproj_bwd_w(yn1, dgate1, s, "inproj_sb_bwd_w_gate")], axis=0)

    do0, dgate0, dw_out_h, dpost0, dwon = _mix_out_bwd(
        dh1, y0, o0, gate0, w_out_h, hgrn_out_norm, post_norm[0:1], True, "mix_out_hgrn_bwd")
    dq0, dfz0, dv0, dlb = _hgrn_bwd(main0, lbrow, states, do0, bsz, n_blocks, "hgrn_bwd")
    dmain0 = jnp.concatenate([dq0, dfz0, dv0], axis=1)
    dh0, dpre0 = _inproj_bwd_x(dmain0, dgate0, w_in_h, h0, pre_norm[0:1], dh1, "inproj_hgrn_bwd_x")
    dw_in_h = jnp.concatenate([_inproj_bwd_w(yn0, dmain0, s, "inproj_hgrn_bwd_w_main"),
                               _inproj_bwd_w(yn0, dgate0, s, "inproj_hgrn_bwd_w_gate")], axis=0)

    dh0 = dh0.reshape(bsz, lp, d)
    grad_x = dh0[:, BLOCK:]
    dmeta = jnp.sum(dh0[:, N_PAD:BLOCK], axis=0)
    dmeta = jnp.transpose(dmeta.reshape(N_META, N_DEV, dsh), (1, 0, 2))
    small = _pack_small(dpre0, dpre1, dpost0, dpost1, dlb, dwon, loss_part, "pack_small")

    p_in_h, p_out_h, p_in_s, p_out_s, p_meta, p_small = _exchange(
        [dw_in_h, dw_out_h.reshape(N_DEV, dsh, d), dw_in_s, dw_out_s.reshape(N_DEV, dsh, d), dmeta, small],
        ["scatter"] * 5 + ["gather"], "exchange_grads")

    u_meta = _update_sharded(meta_tokens, p_meta, m_meta_tokens, v_meta_tokens, "update_meta")
    u_in_h = _update_sharded(hgrn_w_in[0], p_in_h, m_hgrn_w_in[0], v_hgrn_w_in[0], "update_hgrn_w_in")
    u_out_h = _update_sharded(hgrn_w_out[0], p_out_h, m_hgrn_w_out[0], v_hgrn_w_out[0], "update_hgrn_w_out")
    u_in_s = _update_sharded(sb_w_in[0], p_in_s, m_sb_w_in[0], v_sb_w_in[0], "update_sb_w_in")
    u_out_s = _update_sharded(sb_w_out[0], p_out_s, m_sb_w_out[0], v_sb_w_out[0], "update_sb_w_out")
    sm = _update_small(p_small, pre_norm, post_norm, hgrn_lb, hgrn_out_norm,
                       (m_pre_norm, m_post_norm, m_hgrn_lb, m_hgrn_out_norm,
                        v_pre_norm, v_post_norm, v_hgrn_lb, v_hgrn_out_norm), "update_small")
    loss = sm[0][0, 0]
    u_pre, u_post, u_lb, u_on = sm[1:5], sm[5:9], sm[9:13], sm[13:17]

    per_w = [u_meta, u_pre, u_post, tuple(a[None] for a in u_in_h), u_lb, u_on,
             tuple(a[None] for a in u_out_h), tuple(a[None] for a in u_in_s), tuple(a[None] for a in u_out_s)]
    outs = [loss, grad_x]
    for k in range(4):
        outs += [u[k] for u in per_w]
    return tuple(outs)
```

```python
import jax
import jax.numpy as jnp
from jax import lax
from jax.experimental import pallas as pl
from jax.experimental.pallas import tpu as pltpu

F32 = jnp.float32
BF16 = jnp.bfloat16

N_DEV = 8
BLOCK = 128
N_META = 16
N_PAD = BLOCK - N_META
HEAD = 128
SUB = 16
N_SUB = BLOCK // SUB
EPS = 1e-6
ROW_TILE = 256
NEG_BIG = -1e30

ADAM_LR = 0.001
ADAM_B1 = 0.9
ADAM_B2 = 0.999
ADAM_EPS = 1e-08
ADAM_WD = 0.01
ADAM_STEP = 10

VMEM_LIMIT = 56 * 1024 * 1024


def _pcall(body, **kw):
    return pl.pallas_call(body, **kw)


def _params(*sem):
    return pltpu.CompilerParams(dimension_semantics=sem, vmem_limit_bytes=VMEM_LIMIT)


def _dot(a, b):
    return jnp.dot(a, b, preferred_element_type=F32)


def _dot_nt(a, b):
    return lax.dot_general(a, b, (((1,), (1,)), ((), ())), preferred_element_type=F32)


def _dot_tn(a, b):
    return lax.dot_general(a, b, (((0,), (0,)), ((), ())), preferred_element_type=F32)


def _split(x, pieces):
    out = []
    for _ in range(pieces):
        p = x.astype(BF16)
        out.append(p)
        x = x - p.astype(F32)
    return out


def _tri_right(x, tri, pieces=2):
    return sum(_dot(p, tri) for p in _split(x, pieces))


def _tri_left(tri, x, pieces=3):
    return sum(_dot(tri, p) for p in _split(x, pieces))


def _iota2(shape, dim):
    return lax.broadcasted_iota(jnp.int32, shape, dim)


def _tri(cond):
    return jnp.where(cond, 1.0, 0.0).astype(BF16)


def _sig_pair(x):
    e = jnp.exp(-jnp.abs(x))
    r = 1.0 / (1.0 + e)
    er = e * r
    pos = x >= 0
    return jnp.where(pos, r, er), jnp.where(pos, er, r)


def _expand(rows):
    return jnp.concatenate([jnp.broadcast_to(r, (SUB, HEAD)) for r in rows], axis=0)


def _exchange(arrays, modes, name):
    n = len(arrays)
    out_shape = []
    for a, m in zip(arrays, modes):
        shp = a.shape[1:] if m == "scatter" else a.shape
        out_shape.append(jax.ShapeDtypeStruct((N_DEV,) + tuple(shp), a.dtype))

    def body(*refs):
        ins, outs = refs[:n], refs[n:2 * n]
        send_sems, recv_sems, local_sems = refs[2 * n:]
        mx, my, mc = lax.axis_index("x"), lax.axis_index("y"), lax.axis_index("c")
        me = 4 * mx + 2 * my + mc

        def src(i, slot):
            return ins[i].at[slot] if modes[i] == "scatter" else ins[i]

        def peer_of(mask):
            px = 1 - mx if mask & 4 else mx
            py = 1 - my if mask & 2 else my
            pc = 1 - mc if mask & 1 else mc
            return px, py, pc

        def copy(i, mask, dst_slot):
            px, py, pc = peer_of(mask)
            return pltpu.make_async_remote_copy(
                src_ref=src(i, 4 * px + 2 * py + pc), dst_ref=outs[i].at[dst_slot],
                send_sem=send_sems.at[i, mask - 1], recv_sem=recv_sems.at[i, mask - 1],
                device_id=(px, py, pc), device_id_type=pl.DeviceIdType.MESH)

        sends = []
        for mask in range(1, N_DEV):
            for i in range(n):
                cp = copy(i, mask, me)
                cp.start()
                sends.append(cp)
        own = []
        for i in range(n):
            cp = pltpu.make_async_copy(src(i, me), outs[i].at[me], local_sems.at[i])
            cp.start()
            own.append(cp)
        for mask in range(1, N_DEV):
            px, py, pc = peer_of(mask)
            for i in range(n):
                copy(i, mask, 4 * px + 2 * py + pc).wait_recv()
        for cp in sends:
            cp.wait_send()
        for cp in own:
            cp.wait()

    any_spec = pl.BlockSpec(memory_space=pl.ANY)
    return _pcall(
        body, name=name, out_shape=tuple(out_shape),
        in_specs=[any_spec] * n, out_specs=tuple([any_spec] * n),
        scratch_shapes=[pltpu.SemaphoreType.DMA((n, N_DEV - 1)), pltpu.SemaphoreType.DMA((n, N_DEV - 1)),
                        pltpu.SemaphoreType.DMA((n,))],
    )(*arrays)


def _norm_inproj(h, wnorm, w_all, main_dtype, name):
    t, d = h.shape
    p_n, _, s = w_all.shape
    n_main = 3 * d // s
    tm = ROW_TILE

    def body(h_ref, wn_ref, w_ref, main_ref, gate_ref, yn_ref):
        x = h_ref[...]
        y = x * lax.rsqrt(jnp.mean(x * x, axis=-1, keepdims=True) + EPS) * wn_ref[...]
        yb = y.astype(BF16)
        yn_ref[...] = yb
        for p in range(p_n):
            r = _dot(yb, w_ref[p])
            if p < n_main:
                main_ref[:, p * s:(p + 1) * s] = r.astype(main_dtype)
            else:
                gate_ref[:, (p - n_main) * s:(p - n_main + 1) * s] = r

    return _pcall(
        body, name=name, grid=(t // tm,),
        in_specs=[pl.BlockSpec((tm, d), lambda i: (i, 0)), pl.BlockSpec((1, d), lambda i: (0, 0)),
                  pl.BlockSpec((p_n, d, s), lambda i: (0, 0, 0))],
        out_specs=(pl.BlockSpec((tm, 3 * d), lambda i: (i, 0)), pl.BlockSpec((tm, d), lambda i: (i, 0)),
                   pl.BlockSpec((tm, d), lambda i: (i, 0))),
        out_shape=(jax.ShapeDtypeStruct((t, 3 * d), main_dtype), jax.ShapeDtypeStruct((t, d), F32),
                   jax.ShapeDtypeStruct((t, d), BF16)),
        compiler_params=_params("parallel"),
    )(h, wnorm, w_all)


def _mix_out(o, gate, h_in, w_out, w_on, w_post, head_norm, name):
    t, d = o.shape
    n_heads = d // HEAD
    tm = ROW_TILE

    def body(o_ref, g_ref, h_ref, w_ref, won_ref, wp_ref, hout_ref, y_ref, u_scr):
        for hh in range(n_heads):
            cs = slice(hh * HEAD, (hh + 1) * HEAD)
            oh = o_ref[:, cs]
            gt = g_ref[:, cs]
            if head_norm:
                oh = oh * lax.rsqrt(jnp.mean(oh * oh, axis=-1, keepdims=True) + EPS) * won_ref[...]
            u_scr[:, cs] = (oh * (gt * jax.nn.sigmoid(gt))).astype(BF16)
        y = _dot(u_scr[...], w_ref[...])
        y_ref[...] = y
        r = y * lax.rsqrt(jnp.mean(y * y, axis=-1, keepdims=True) + EPS) * wp_ref[...]
        hout_ref[...] = h_ref[...] + r

    row = pl.BlockSpec((tm, d), lambda i: (i, 0))
    return _pcall(
        body, name=name, grid=(t // tm,),
        in_specs=[row, row, row, pl.BlockSpec((d, d), lambda i: (0, 0)),
                  pl.BlockSpec((1, HEAD), lambda i: (0, 0)), pl.BlockSpec((1, d), lambda i: (0, 0))],
        out_specs=(row, row),
        out_shape=(jax.ShapeDtypeStruct((t, d), F32), jax.ShapeDtypeStruct((t, d), F32)),
        scratch_shapes=[pltpu.VMEM((tm, d), BF16)],
        compiler_params=_params("parallel"),
    )(o, gate, h_in, w_out, w_on, w_post)


def _loss_grad(h_out, target, n_blocks, name):
    t, d = h_out.shape

    def body(h_ref, t_ref, dh_ref, loss_ref):
        i = pl.program_id(0)

        @pl.when(i == 0)
        def _():
            loss_ref[...] = jnp.zeros_like(loss_ref)

        real = (i % n_blocks) > 0
        err = jnp.where(real, h_ref[...] - t_ref[...], 0.0)
        dh_ref[...] = err * (1.0 / d)
        part = jnp.sum(jnp.sum(err * err, axis=-1, keepdims=True), axis=0, keepdims=True)
        loss_ref[...] += part * (0.5 / d)

    return _pcall(
        body, name=name, grid=(t // BLOCK,),
        in_specs=[pl.BlockSpec((BLOCK, d), lambda i: (i, 0)),
                  pl.BlockSpec((None, BLOCK, d), lambda i: (i // n_blocks, jnp.maximum(i % n_blocks - 1, 0), 0))],
        out_specs=(pl.BlockSpec((BLOCK, d), lambda i: (i, 0)), pl.BlockSpec((8, 128), lambda i: (0, 0))),
        out_shape=(jax.ShapeDtypeStruct((t, d), F32), jax.ShapeDtypeStruct((8, 128), F32)),
        compiler_params=_params("arbitrary"),
    )(h_out, target)


def _mix_out_bwd(dh, y, o, gate, w_out, w_on, w_post, head_norm, name):
    t, d = o.shape
    n_heads = d // HEAD
    tm = ROW_TILE
    last = t // tm - 1

    def body(dh_ref, y_ref, o_ref, g_ref, w_ref, won_ref, wp_ref,
             do_ref, dg_ref, dw_ref, dwp_ref, dwon_ref, u_scr, acc):
        i = pl.program_id(0)

        @pl.when(i == 0)
        def _():
            acc[...] = jnp.zeros_like(acc)
            dwp_ref[...] = jnp.zeros_like(dwp_ref)
            dwon_ref[...] = jnp.zeros_like(dwon_ref)

        yv = y_ref[...]
        rs = lax.rsqrt(jnp.mean(yv * yv, axis=-1, keepdims=True) + EPS)
        yh = yv * rs
        dr = dh_ref[...]
        dwp_ref[...] += jnp.sum(dr * yh, axis=0, keepdims=True)
        wd = dr * wp_ref[...]
        dy = rs * (wd - yh * jnp.mean(wd * yh, axis=-1, keepdims=True))
        dyb = dy.astype(BF16)
        du = _dot_nt(dyb, w_ref[...])
        for hh in range(n_heads):
            cs = slice(hh * HEAD, (hh + 1) * HEAD)
            oh = o_ref[:, cs]
            gt = g_ref[:, cs]
            sg = jax.nn.sigmoid(gt)
            sl = gt * sg
            duh = du[:, cs]
            if head_norm:
                rsh = lax.rsqrt(jnp.mean(oh * oh, axis=-1, keepdims=True) + EPS)
                ohat = oh * rsh
                on = ohat * won_ref[...]
            else:
                on = oh
            u_scr[:, cs] = (on * sl).astype(BF16)
            don = duh * sl
            dg_ref[:, cs] = (duh * on * (sg * (1.0 + gt * (1.0 - sg)))).astype(BF16)
            if head_norm:
                dwon_ref[...] += jnp.sum(don * ohat, axis=0, keepdims=True)
                wdn = don * won_ref[...]
                do_ref[:, cs] = rsh * (wdn - ohat * jnp.mean(wdn * ohat, axis=-1, keepdims=True))
            else:
                do_ref[:, cs] = don
        acc[...] += _dot_tn(u_scr[...], dyb)

        @pl.when(i == last)
        def _():
            dw_ref[...] = acc[...].astype(BF16)

    row = pl.BlockSpec((tm, d), lambda i: (i, 0))
    return _pcall(
        body, name=name, grid=(t // tm,),
        in_specs=[row, row, row, row, pl.BlockSpec((d, d), lambda i: (0, 0)),
                  pl.BlockSpec((1, HEAD), lambda i: (0, 0)), pl.BlockSpec((1, d), lambda i: (0, 0))],
        out_specs=(row, row, pl.BlockSpec((d, d), lambda i: (0, 0)), pl.BlockSpec((1, d), lambda i: (0, 0)),
                   pl.BlockSpec((1, HEAD), lambda i: (0, 0))),
        out_shape=(jax.ShapeDtypeStruct((t, d), F32), jax.ShapeDtypeStruct((t, d), BF16),
                   jax.ShapeDtypeStruct((d, d), BF16), jax.ShapeDtypeStruct((1, d), F32),
                   jax.ShapeDtypeStruct((1, HEAD), F32)),
        scratch_shapes=[pltpu.VMEM((tm, d), BF16), pltpu.VMEM((d, d), F32)],
        compiler_params=_params("arbitrary"),
    )(dh, y, o, gate, w_out, w_on, w_post)


def _inproj_bwd_x(dmain, dgate, w_all, h, wnorm, dres, name):
    t, d = h.shape
    p_n, _, s = w_all.shape
    n_main = 3 * d // s
    tm = ROW_TILE

    def body(dm_ref, dg_ref, w_ref, h_ref, wn_ref, dres_ref, dh_ref, dwn_ref):
        i = pl.program_id(0)

        @pl.when(i == 0)
        def _():
            dwn_ref[...] = jnp.zeros_like(dwn_ref)

        dyn = jnp.zeros((tm, d), F32)
        for p in range(p_n):
            if p < n_main:
                blk = dm_ref[:, p * s:(p + 1) * s]
            else:
                blk = dg_ref[:, (p - n_main) * s:(p - n_main + 1) * s]
            dyn = dyn + _dot_nt(blk, w_ref[p])
        x = h_ref[...]
        rs = lax.rsqrt(jnp.mean(x * x, axis=-1, keepdims=True) + EPS)
        xh = x * rs
        dwn_ref[...] += jnp.sum(dyn * xh, axis=0, keepdims=True)
        wd = dyn * wn_ref[...]
        dh_ref[...] = dres_ref[...] + rs * (wd - xh * jnp.mean(wd * xh, axis=-1, keepdims=True))

    row = pl.BlockSpec((tm, d), lambda i: (i, 0))
    return _pcall(
        body, name=name, grid=(t // tm,),
        in_specs=[pl.BlockSpec((tm, 3 * d), lambda i: (i, 0)), row, pl.BlockSpec((p_n, d, s), lambda i: (0, 0, 0)),
                  row, pl.BlockSpec((1, d), lambda i: (0, 0)), row],
        out_specs=(row, pl.BlockSpec((1, d), lambda i: (0, 0))),
        out_shape=(jax.ShapeDtypeStruct((t, d), F32), jax.ShapeDtypeStruct((1, d), F32)),
        compiler_params=_params("arbitrary"),
    )(dmain, dgate, w_all, h, wnorm, dres)


def _inproj_bwd_w(yn, dproj, s, name):
    t, d = yn.shape
    n_sh = dproj.shape[1] // s
    tm = ROW_TILE
    last = t // tm - 1

    def body(yn_ref, dp_ref, dw_ref, acc):
        i = pl.program_id(1)

        @pl.when(i == 0)
        def _():
            acc[...] = jnp.zeros_like(acc)

        acc[...] += _dot_tn(yn_ref[...], dp_ref[...])

        @pl.when(i == last)
        def _():
            dw_ref[...] = acc[...].astype(BF16)

    return _pcall(
        body, name=name, grid=(n_sh, t // tm),
        in_specs=[pl.BlockSpec((tm, d), lambda p, i: (i, 0)), pl.BlockSpec((tm, s), lambda p, i: (i, p))],
        out_specs=pl.BlockSpec((None, d, s), lambda p, i: (p, 0, 0)),
        out_shape=jax.ShapeDtypeStruct((n_sh, d, s), BF16),
        scratch_shapes=[pltpu.VMEM((d, s), F32)],
        compiler_params=_params("parallel", "arbitrary"),
    )(yn, dproj)


def _hgrn_common(q_ref, fz_ref, lb, b_scr, k_scr):
    fz = fz_ref[...]
    sig, nsig = _sig_pair(fz)
    f = lb + (1.0 - lb) * sig
    k = (1.0 - lb) * nsig
    rr, cc = _iota2((BLOCK, BLOCK), 0), _iota2((BLOCK, BLOCK), 1)
    b = _tri_left(_tri(cc <= rr), jnp.log(f))
    b_scr[...] = b
    k_scr[...] = k
    bend = [b_scr[pl.ds(SUB * j + SUB - 1, 1), :] for j in range(N_SUB)]
    bref = [jnp.zeros((1, HEAD), F32)] + bend[:-1]
    refrow, bendrow = _expand(bref), _expand(bend)
    e_q = jnp.exp(b - refrow)
    e_k = jnp.exp(bendrow - b)
    qt = q_ref[...] * e_q
    kh = k * e_k
    bl = bend[-1]
    return dict(sig=sig, nsig=nsig, f=f, k=k, b=b, bend=bend, bref=bref, refrow=refrow,
                e_q=e_q, e_k=e_k, qt=qt, kh=kh, bl=bl)


def _hgrn_fwd(main, lbrow, bsz, n_blocks, name):
    t, d3 = main.shape
    d = d3 // 3
    n_heads = d // HEAD

    def body(q_ref, fz_ref, v_ref, lb_ref, o_ref, st_ref, s_scr, b_scr, k_scr, o_acc):
        n = pl.program_id(2)

        @pl.when(n == 0)
        def _():
            s_scr[...] = jnp.zeros_like(s_scr)

        c = _hgrn_common(q_ref, fz_ref, lb_ref[...], b_scr, k_scr)
        s_t = s_scr[...]
        st_ref[...] = s_t
        v = v_ref[...]
        vb = v.astype(BF16)
        q_state = c["qt"] * _expand([jnp.exp(r) for r in c["bref"]])
        o_acc[...] = _dot_nt(q_state.astype(BF16), s_t.astype(BF16))
        for j in range(N_SUB - 1):
            r0 = SUB * (j + 1)
            dj = jnp.exp(c["refrow"][r0:] - c["bend"][j])
            lhs = (c["qt"][r0:] * dj).astype(BF16)
            a_j = _dot_nt(lhs, c["kh"][SUB * j:SUB * (j + 1)].astype(BF16))
            o_acc[r0:, :] += _dot(a_j.astype(BF16), vb[SUB * j:SUB * (j + 1)])
        t_loc = _iota2((SUB, HEAD), 0)
        for i in range(N_SUB):
            r0 = SUB * i
            qi = q_ref[pl.ds(r0, SUB), :]
            bi = b_scr[pl.ds(r0, SUB), :]
            oi = jnp.zeros((SUB, HEAD), F32)
            for s in range(SUB):
                brow = b_scr[pl.ds(r0 + s, 1), :]
                krow = k_scr[pl.ds(r0 + s, 1), :]
                vrow = v_ref[pl.ds(r0 + s, 1), :]
                w = jnp.exp(jnp.where(t_loc >= s, bi - brow, NEG_BIG))
                col = jnp.sum(qi * krow * w, axis=-1, keepdims=True)
                oi = oi + col * vrow
            o_acc[pl.ds(r0, SUB), :] += oi
        o_ref[...] = o_acc[...]
        k_state = c["kh"] * _expand([jnp.exp(c["bl"] - r) for r in c["bend"]])
        s_scr[...] = s_t * jnp.exp(c["bl"]) + _dot_tn(vb, k_state.astype(BF16))

    def blk(col0):
        return pl.BlockSpec((BLOCK, HEAD), lambda b, h, n: (b * n_blocks + n, col0 + h))

    return _pcall(
        body, name=name, grid=(bsz, n_heads, n_blocks),
        in_specs=[blk(0), blk(n_heads), blk(2 * n_heads), pl.BlockSpec((1, HEAD), lambda b, h, n: (0, h))],
        out_specs=(blk(0), pl.BlockSpec((None, HEAD, HEAD), lambda b, h, n: ((b * n_heads + h) * n_blocks + n, 0, 0))),
        out_shape=(jax.ShapeDtypeStruct((t, d), F32),
                   jax.ShapeDtypeStruct((bsz * n_heads * n_blocks, HEAD, HEAD), F32)),
        scratch_shapes=[pltpu.VMEM((HEAD, HEAD), F32), pltpu.VMEM((BLOCK, HEAD), F32),
                        pltpu.VMEM((BLOCK, HEAD), F32), pltpu.VMEM((BLOCK, HEAD), F32)],
        compiler_params=_params("parallel", "parallel", "arbitrary"),
    )(main, main, main, lbrow)


def _hgrn_bwd(main, lbrow, states, do, bsz, n_blocks, name):
    t, d3 = main.shape
    d = d3 // 3
    n_heads = d // HEAD

    def body(q_ref, fz_ref, v_ref, lb_ref, st_ref, do_ref, dq_ref, dfz_ref, dv_ref, dlb_ref,
             ds_scr, b_scr, k_scr, dqt_acc, dkh_acc, dv_acc, dqd_acc, dkd_acc):
        step = pl.program_id(2)
        n = n_blocks - 1 - step

        @pl.when(step == 0)
        def _():
            ds_scr[...] = jnp.zeros_like(ds_scr)
            dlb_ref[...] = jnp.zeros_like(dlb_ref)

        lb = lb_ref[...]
        c = _hgrn_common(q_ref, fz_ref, lb, b_scr, k_scr)
        q, k, b = q_ref[...], c["k"], c["b"]
        v = v_ref[...]
        vb = v.astype(BF16)
        dout = do_ref[...]
        dob = dout.astype(BF16)
        s0_t = st_ref[...]
        ds1_t = ds_scr[...]
        e_ref = _expand([jnp.exp(r) for r in c["bref"]])
        e_end = _expand([jnp.exp(c["bl"] - r) for r in c["bend"]])
        e_bl = jnp.exp(c["bl"])
        q_state = c["qt"] * e_ref
        k_state = c["kh"] * e_end
        dqt_acc[...] = _dot(dob, s0_t.astype(BF16)) * e_ref
        dk_state = _dot(vb, ds1_t.astype(BF16))
        dkh_acc[...] = dk_state * e_end
        dv_acc[...] = _dot_nt(k_state.astype(BF16), ds1_t.astype(BF16))
        dbl = (jnp.sum(s0_t * ds1_t, axis=0, keepdims=True) * e_bl
               + jnp.sum(k_state * dk_state, axis=0, keepdims=True))
        ds_scr[...] = ds1_t * e_bl + _dot_tn(dob, q_state.astype(BF16))
        for j in range(N_SUB - 1):
            r0 = SUB * (j + 1)
            js = slice(SUB * j, SUB * (j + 1))
            dj = jnp.exp(c["refrow"][r0:] - c["bend"][j])
            lhs = (c["qt"][r0:] * dj).astype(BF16)
            khj = c["kh"][js].astype(BF16)
            a_j = _dot_nt(lhs, khj)
            da_j = _dot_nt(dob[r0:], vb[js]).astype(BF16)
            dv_acc[js, :] += _dot_tn(a_j.astype(BF16), dob[r0:])
            dqt_acc[r0:, :] += _dot(da_j, khj) * dj
            dkh_acc[js, :] += _dot_tn(da_j, lhs)
        t_loc = _iota2((SUB, HEAD), 0)
        for i in range(N_SUB):
            r0 = SUB * i
            qi = q_ref[pl.ds(r0, SUB), :]
            bi = b_scr[pl.ds(r0, SUB), :]
            doi = do_ref[pl.ds(r0, SUB), :]
            dqi = jnp.zeros((SUB, HEAD), F32)
            for s in range(SUB):
                brow = b_scr[pl.ds(r0 + s, 1), :]
                krow = k_scr[pl.ds(r0 + s, 1), :]
                vrow = v_ref[pl.ds(r0 + s, 1), :]
                w = jnp.exp(jnp.where(t_loc >= s, bi - brow, NEG_BIG))
                qw = qi * w
                a_col = jnp.sum(qw * krow, axis=-1, keepdims=True)
                da_col = jnp.sum(doi * vrow, axis=-1, keepdims=True)
                dv_acc[pl.ds(r0 + s, 1), :] += jnp.sum(a_col * doi, axis=0, keepdims=True)
                dqi = dqi + da_col * (w * krow)
                dkd_acc[pl.ds(r0 + s, 1), :] = jnp.sum(da_col * qw, axis=0, keepdims=True)
            dqd_acc[pl.ds(r0, SUB), :] = dqi
        dq = dqt_acc[...] * c["e_q"] + dqd_acc[...]
        dk = dkh_acc[...] * c["e_k"] + dkd_acc[...]
        rr, cc = _iota2((BLOCK, BLOCK), 0), _iota2((BLOCK, BLOCK), 1)
        db = q * dq - k * dk + jnp.where(_iota2((BLOCK, HEAD), 0) == BLOCK - 1, dbl, 0.0)
        dg = _tri_left(_tri(cc >= rr), db)
        real = jnp.logical_or(n > 0, _iota2((BLOCK, HEAD), 0) >= N_PAD)
        df = jnp.where(real, dg / c["f"] - dk, 0.0)
        dq_ref[...] = dq.astype(BF16)
        dv_ref[...] = dv_acc[...].astype(BF16)
        dfz_ref[...] = (df * (1.0 - lb) * c["sig"] * c["nsig"]).astype(BF16)
        dlb_ref[...] += jnp.sum(df * c["nsig"], axis=0, keepdims=True)

    def blk(col0):
        return pl.BlockSpec((BLOCK, HEAD), lambda b, h, s: (b * n_blocks + n_blocks - 1 - s, col0 + h))

    return _pcall(
        body, name=name, grid=(bsz, n_heads, n_blocks),
        in_specs=[blk(0), blk(n_heads), blk(2 * n_heads), pl.BlockSpec((1, HEAD), lambda b, h, s: (0, h)),
                  pl.BlockSpec((None, HEAD, HEAD),
                               lambda b, h, s: ((b * n_heads + h) * n_blocks + n_blocks - 1 - s, 0, 0)),
                  blk(0)],
        out_specs=(blk(0), blk(0), blk(0), pl.BlockSpec((None, 1, HEAD), lambda b, h, s: (b, 0, h))),
        out_shape=(jax.ShapeDtypeStruct((t, d), BF16),) * 3 + (jax.ShapeDtypeStruct((bsz, 1, d), F32),),
        scratch_shapes=[pltpu.VMEM((HEAD, HEAD), F32)] + [pltpu.VMEM((BLOCK, HEAD), F32)] * 7,
        compiler_params=_params("parallel", "parallel", "arbitrary"),
    )(main, main, main, lbrow, states, do)


SB_GROUP = 3
SB_ROWS = SB_GROUP * BLOCK
SB_DEAD = -104.0


def _sb_scores(q, ks, g, m, scale):
    z = _dot_nt(q, ks) * scale
    t_idx = g * SB_ROWS + _iota2((SB_ROWS, BLOCK), 0)
    s_idx = m * BLOCK + _iota2((SB_ROWS, BLOCK), 1)
    valid = jnp.logical_and(s_idx < t_idx, s_idx >= N_PAD)
    l1p = jnp.log1p(jnp.exp(-jnp.abs(z)))
    log_beta = jnp.minimum(z, 0.0) - l1p
    log_keep = jnp.where(valid, -jnp.maximum(z, 0.0) - l1p, 0.0)
    return z, valid, log_beta, log_keep


def _sb_fwd(qkv, bsz, n_blocks, name):
    t, d3 = qkv.shape
    d = d3 // 3
    n_heads = d // HEAD
    lp = n_blocks * BLOCK
    n_groups = n_blocks // SB_GROUP
    assert n_groups * SB_GROUP == n_blocks
    scale = HEAD ** -0.5

    def body(q_ref, k_ref, v_ref, o_ref, tot_ref, stop_ref):
        g = pl.program_id(2)
        q = q_ref[...]
        upper = _tri(_iota2((BLOCK, BLOCK), 0) > _iota2((BLOCK, BLOCK), 1))

        def live(carry):
            m, c_max, _, _ = carry
            return jnp.logical_and(m >= 0, c_max >= SB_DEAD)

        def step(carry):
            m, _, c, acc = carry
            rows = pl.ds(pl.multiple_of(m * BLOCK, BLOCK), BLOCK)
            _, valid, log_beta, log_keep = _sb_scores(q, k_ref[rows, :], g, m, scale)
            later = c + _tri_right(log_keep, upper)
            a = jnp.where(valid, jnp.exp(log_beta + later), 0.0)
            acc = acc + _dot(a.astype(BF16), v_ref[rows, :])
            c = c + jnp.sum(log_keep, axis=-1, keepdims=True)
            return m - 1, jnp.max(c), c, acc

        m_end, _, c, acc = lax.while_loop(
            live, step, (SB_GROUP * g + SB_GROUP - 1, jnp.float32(0.0),
                         jnp.zeros((SB_ROWS, 1), F32), jnp.zeros((SB_ROWS, HEAD), F32)))
        o_ref[...] = acc
        tot_ref[...] = jnp.broadcast_to(c, (SB_ROWS, HEAD))
        stop_ref[...] = jnp.broadcast_to((m_end + 1).astype(F32), stop_ref.shape)

    qblk = pl.BlockSpec((SB_ROWS, HEAD), lambda b, h, g: (b * n_groups + g, h))
    return _pcall(
        body, name=name, grid=(bsz, n_heads, n_groups),
        in_specs=[qblk, pl.BlockSpec((lp, HEAD), lambda b, h, g: (b, n_heads + h)),
                  pl.BlockSpec((lp, HEAD), lambda b, h, g: (b, 2 * n_heads + h))],
        out_specs=(qblk, qblk, pl.BlockSpec((None, 8, 128), lambda b, h, g: ((b * n_heads + h) * n_groups + g, 0, 0))),
        out_shape=(jax.ShapeDtypeStruct((t, d), F32), jax.ShapeDtypeStruct((t, d), F32),
                   jax.ShapeDtypeStruct((bsz * n_heads * n_groups, 8, 128), F32)),
        compiler_params=_params("parallel", "parallel", "arbitrary"),
    )(qkv, qkv, qkv)


def _sb_bwd(qkv, do, tot, stop, bsz, n_blocks, name):
    t, d3 = qkv.shape
    d = d3 // 3
    n_heads = d // HEAD
    lp = n_blocks * BLOCK
    n_groups = n_blocks // SB_GROUP
    scale = HEAD ** -0.5

    def body(q_ref, k_ref, v_ref, do_ref, tot_ref, stop_ref, dq_ref, dk_ref, dv_ref, dk_acc, dv_acc):
        g = pl.program_id(2)

        @pl.when(g == 0)
        def _():
            dk_acc[...] = jnp.zeros_like(dk_acc)
            dv_acc[...] = jnp.zeros_like(dv_acc)

        q = q_ref[...]
        dob = do_ref[...].astype(BF16)
        tot_c = tot_ref[:, 0:1]
        top = SB_GROUP * g + SB_GROUP - 1
        first = jnp.clip(jnp.max(stop_ref[...]).astype(jnp.int32), 0, top)
        rr, cc = _iota2((BLOCK, BLOCK), 0), _iota2((BLOCK, BLOCK), 1)
        incl = _tri(rr <= cc)
        excl = _tri(rr < cc)

        def step(m, carry):
            p, e, dq = carry
            rows = pl.ds(pl.multiple_of(m * BLOCK, BLOCK), BLOCK)
            ks, vs = k_ref[rows, :], v_ref[rows, :]
            z, valid, log_beta, log_keep = _sb_scores(q, ks, g, m, scale)
            later = tot_c - p - _tri_right(log_keep, incl)
            a = jnp.where(valid, jnp.exp(log_beta + later), 0.0)
            gr = _dot_nt(dob, vs) * a
            dv_acc[rows, :] += _dot_tn(a.astype(BF16), dob)
            cum = e + _tri_right(gr, excl)
            sig, nsig = _sig_pair(z)
            dz = (jnp.where(valid, gr * nsig - cum * sig, 0.0) * scale).astype(BF16)
            dk_acc[rows, :] += _dot_tn(dz, q)
            return (p + jnp.sum(log_keep, axis=-1, keepdims=True), e + jnp.sum(gr, axis=-1, keepdims=True),
                    dq + _dot(dz, ks))

        zero_c = jnp.zeros((SB_ROWS, 1), F32)
        _, _, dq = lax.fori_loop(first, top + 1, step, (zero_c, zero_c, jnp.zeros((SB_ROWS, HEAD), F32)))
        dq_ref[...] = dq.astype(BF16)

        @pl.when(g == n_groups - 1)
        def _():
            dk_ref[...] = dk_acc[...].astype(BF16)
            dv_ref[...] = dv_acc[...].astype(BF16)

    qblk = pl.BlockSpec((SB_ROWS, HEAD), lambda b, h, g: (b * n_groups + g, h))
    kblk = pl.BlockSpec((lp, HEAD), lambda b, h, g: (b, n_heads + h))
    vblk = pl.BlockSpec((lp, HEAD), lambda b, h, g: (b, 2 * n_heads + h))
    hblk = pl.BlockSpec((lp, HEAD), lambda b, h, g: (b, h))
    sblk = pl.BlockSpec((None, 8, 128), lambda b, h, g: ((b * n_heads + h) * n_groups + g, 0, 0))
    return _pcall(
        body, name=name, grid=(bsz, n_heads, n_groups),
        in_specs=[qblk, kblk, vblk, qblk, qblk, sblk],
        out_specs=(qblk, hblk, hblk),
        out_shape=(jax.ShapeDtypeStruct((t, d), BF16),) * 3,
        scratch_shapes=[pltpu.VMEM((lp, HEAD), F32), pltpu.VMEM((lp, HEAD), F32)],
        compiler_params=_params("parallel", "parallel", "arbitrary"),
    )(qkv, qkv, qkv, do, tot, stop)


def _adamw(w, g, m, v):
    m = ADAM_B1 * m + (1.0 - ADAM_B1) * g
    v = ADAM_B2 * v + (1.0 - ADAM_B2) * (g * g)
    m_hat = m / (1.0 - ADAM_B1 ** ADAM_STEP)
    v_hat = v / (1.0 - ADAM_B2 ** ADAM_STEP)
    delta = -ADAM_LR * (m_hat / (jnp.sqrt(v_hat) + ADAM_EPS) + ADAM_WD * w)
    return delta, m, v


def _update_sharded(w, parts, m, v, name):
    r, c = w.shape
    tr = min(r, ROW_TILE)

    def body(w_ref, p_ref, m_ref, v_ref, g_ref, d_ref, nm_ref, nv_ref):
        g = p_ref[0].astype(F32)
        for q in range(1, N_DEV):
            g = g + p_ref[q].astype(F32)
        g_ref[...] = g
        d_ref[...], nm_ref[...], nv_ref[...] = _adamw(w_ref[...], g, m_ref[...], v_ref[...])

    row = pl.BlockSpec((tr, c), lambda i: (i, 0))
    return _pcall(
        body, name=name, grid=(r // tr,),
        in_specs=[row, pl.BlockSpec((N_DEV, tr, c), lambda i: (0, i, 0)), row, row],
        out_specs=(row,) * 4, out_shape=(jax.ShapeDtypeStruct((r, c), F32),) * 4,
        compiler_params=_params("parallel"),
    )(w, parts, m, v)


SMALL_ROWS = 8


def _pack_small(dpre0, dpre1, dpost0, dpost1, dlb, dwon, loss, name):
    d = dpre0.shape[1]
    bsz = dlb.shape[0]

    def body(a0, a1, p0, p1, lb_ref, on_ref, loss_ref, out_ref):
        out_ref[...] = jnp.zeros_like(out_ref)
        out_ref[pl.ds(0, 1), :] = a0[...]
        out_ref[pl.ds(1, 1), :] = a1[...]
        out_ref[pl.ds(2, 1), :] = p0[...]
        out_ref[pl.ds(3, 1), :] = p1[...]
        acc = lb_ref[0]
        for b in range(1, bsz):
            acc = acc + lb_ref[b]
        out_ref[pl.ds(4, 1), :] = acc
        out_ref[pl.ds(5, 1), pl.ds(0, HEAD)] = on_ref[...]
        out_ref[pl.ds(6, 1), pl.ds(0, HEAD)] = loss_ref[pl.ds(0, 1), :]

    return _pcall(body, name=name, out_shape=jax.ShapeDtypeStruct((SMALL_ROWS, d), F32))(
        dpre0, dpre1, dpost0, dpost1, dlb, dwon, loss)


def _update_small(parts, pre, post, lbw, on, moments, name):
    d = pre.shape[1]

    def body(p_ref, pre_ref, post_ref, lbw_ref, on_ref, mpre, mpost, mlb, mon, vpre, vpost, vlb, von,
             loss_ref, *outs):
        def total(r0, nr, width):
            acc = p_ref[0, pl.ds(r0, nr), pl.ds(0, width)]
            for q in range(1, N_DEV):
                acc = acc + p_ref[q, pl.ds(r0, nr), pl.ds(0, width)]
            return acc

        def put(k, w, g, m, v):
            dl, nm, nv = _adamw(w, g, m, v)
            outs[4 * k][...] = g
            outs[4 * k + 1][...] = dl
            outs[4 * k + 2][...] = nm
            outs[4 * k + 3][...] = nv

        put(0, pre_ref[...], total(0, 2, d), mpre[...], vpre[...])
        put(1, post_ref[...], total(2, 2, d), mpost[...], vpost[...])
        a0, a1 = lbw_ref[pl.ds(0, 1), :], lbw_ref[pl.ds(1, 1), :]
        mx = jnp.maximum(a0, a1)
        e0, e1 = jnp.exp(a0 - mx), jnp.exp(a1 - mx)
        p0 = e0 / (e0 + e1)
        g0 = total(4, 1, d) * p0 * (1.0 - p0)
        for r, w, g in ((0, a0, g0), (1, a1, -g0)):
            row = pl.ds(r, 1)
            dl, nm, nv = _adamw(w, g, mlb[row, :], vlb[row, :])
            outs[8][row, :] = g
            outs[9][row, :] = dl
            outs[10][row, :] = nm
            outs[11][row, :] = nv
        put(3, on_ref[...], total(5, 1, HEAD), mon[...], von[...])
        loss_ref[...] = jnp.broadcast_to(total(6, 1, HEAD), loss_ref.shape)

    shapes = []
    for w in (pre, post, lbw, on):
        shapes += [jax.ShapeDtypeStruct(w.shape, F32)] * 4
    return _pcall(body, name=name, out_shape=(jax.ShapeDtypeStruct((8, 128), F32), *shapes))(
        parts, pre, post, lbw, on, *moments)


def kernel(x, meta_tokens, pre_norm, post_norm, hgrn_w_in, hgrn_lb, hgrn_out_norm, hgrn_w_out, sb_w_in, sb_w_out, loss_target, m_meta_tokens, m_pre_norm, m_post_norm, m_hgrn_w_in, m_hgrn_lb, m_hgrn_out_norm, m_hgrn_w_out, m_sb_w_in, m_sb_w_out, v_meta_tokens, v_pre_norm, v_post_norm, v_hgrn_w_in, v_hgrn_lb, v_hgrn_out_norm, v_hgrn_w_out, v_sb_w_in, v_sb_w_out):
    bsz, seq, d = x.shape
    n_blocks = seq // BLOCK + 1
    lp = n_blocks * BLOCK
    t = bsz * lp
    s = hgrn_w_in.shape[2]
    dsh = d // N_DEV

    w_in_h, w_out_h, w_in_s, w_out_s, meta_all = _exchange(
        [hgrn_w_in[0].astype(BF16), hgrn_w_out[0].astype(BF16), sb_w_in[0].astype(BF16),
         sb_w_out[0].astype(BF16), meta_tokens],
        ["gather"] * 5, "gather_weights")
    w_out_h = w_out_h.reshape(d, d)
    w_out_s = w_out_s.reshape(d, d)
    meta_full = jnp.transpose(meta_all, (1, 0, 2)).reshape(N_META, d)

    h0 = jnp.concatenate(
        [jnp.zeros((bsz, N_PAD, d), F32), jnp.broadcast_to(meta_full[None], (bsz, N_META, d)), x], axis=1
    ).reshape(t, d)
    lbrow = jnp.cumsum(jax.nn.softmax(hgrn_lb, axis=0), axis=0)[0:1]

    main0, gate0, yn0 = _norm_inproj(h0, pre_norm[0:1], w_in_h, F32, "inproj_hgrn")
    o0, states = _hgrn_fwd(main0, lbrow, bsz, n_blocks, "hgrn_fwd")
    h1, y0 = _mix_out(o0, gate0, h0, w_out_h, hgrn_out_norm, post_norm[0:1], True, "mix_out_hgrn")
    main1, gate1, yn1 = _norm_inproj(h1, pre_norm[1:2], w_in_s, BF16, "inproj_sb")
    o1, tot, stop = _sb_fwd(main1, bsz, n_blocks, "sb_fwd")
    h2, y1 = _mix_out(o1, gate1, h1, w_out_s, hgrn_out_norm, post_norm[1:2], False, "mix_out_sb")
    dh2, loss_part = _loss_grad(h2, loss_target, n_blocks, "loss_grad")

    do1, dgate1, dw_out_s, dpost1, _ = _mix_out_bwd(
        dh2, y1, o1, gate1, w_out_s, hgrn_out_norm, post_norm[1:2], False, "mix_out_sb_bwd")
    dq1, dk1, dv1 = _sb_bwd(main1, do1, tot, stop, bsz, n_blocks, "sb_bwd")
    dmain1 = jnp.concatenate([dq1, dk1, dv1], axis=1)
    dh1, dpre1 = _inproj_bwd_x(dmain1, dgate1, w_in_s, h1, pre_norm[1:2], dh2, "inproj_sb_bwd_x")
    dw_in_s = jnp.concatenate([_inproj_bwd_w(yn1, dmain1, s, "inproj_sb_bwd_w_main"),
                               _inproj_bwd_w(yn1, dgate1, s, "inproj_sb_bwd_w_gate")], axis=0)

    do0, dgate0, dw_out_h, dpost0, dwon = _mix_out_bwd(
        dh1, y0, o0, gate0, w_out_h, hgrn_out_norm, post_norm[0:1], True, "mix_out_hgrn_bwd")
    dq0, dfz0, dv0, dlb = _hgrn_bwd(main0, lbrow, states, do0, bsz, n_blocks, "hgrn_bwd")
    dmain0 = jnp.concatenate([dq0, dfz0, dv0], axis=1)
    dh0, dpre0 = _inproj_bwd_x(dmain0, dgate0, w_in_h, h0, pre_norm[0:1], dh1, "inproj_hgrn_bwd_x")
    dw_in_h = jnp.concatenate([_inproj_bwd_w(yn0, dmain0, s, "inproj_hgrn_bwd_w_main"),
                               _inproj_bwd_w(yn0, dgate0, s, "inproj_hgrn_bwd_w_gate")], axis=0)

    dh0 = dh0.reshape(bsz, lp, d)
    grad_x = dh0[:, BLOCK:]
    dmeta = jnp.sum(dh0[:, N_PAD:BLOCK], axis=0)
    dmeta = jnp.transpose(dmeta.reshape(N_META, N_DEV, dsh), (1, 0, 2))
    small = _pack_small(dpre0, dpre1, dpost0, dpost1, dlb, dwon, loss_part, "pack_small")

    p_in_h, p_out_h, p_in_s, p_out_s, p_meta, p_small = _exchange(
        [dw_in_h, dw_out_h.reshape(N_DEV, dsh, d), dw_in_s, dw_out_s.reshape(N_DEV, dsh, d), dmeta, small],
        ["scatter"] * 5 + ["gather"], "exchange_grads")

    u_meta = _update_sharded(meta_tokens, p_meta, m_meta_tokens, v_meta_tokens, "update_meta")
    u_in_h = _update_sharded(hgrn_w_in[0], p_in_h, m_hgrn_w_in[0], v_hgrn_w_in[0], "update_hgrn_w_in")
    u_out_h = _update_sharded(hgrn_w_out[0], p_out_h, m_hgrn_w_out[0], v_hgrn_w_out[0], "update_hgrn_w_out")
    u_in_s = _update_sharded(sb_w_in[0], p_in_s, m_sb_w_in[0], v_sb_w_in[0], "update_sb_w_in")
    u_out_s = _update_sharded(sb_w_out[0], p_out_s, m_sb_w_out[0], v_sb_w_out[0], "update_sb_w_out")
    sm = _update_small(p_small, pre_norm, post_norm, hgrn_lb, hgrn_out_norm,
                       (m_pre_norm, m_post_norm, m_hgrn_lb, m_hgrn_out_norm,
                        v_pre_norm, v_post_norm, v_hgrn_lb, v_hgrn_out_norm), "update_small")
    loss = sm[0][0, 0]
    u_pre, u_post, u_lb, u_on = sm[1:5], sm[5:9], sm[9:13], sm[13:17]

    per_w = [u_meta, u_pre, u_post, tuple(a[None] for a in u_in_h), u_lb, u_on,
             tuple(a[None] for a in u_out_h), tuple(a[None] for a in u_in_s), tuple(a[None] for a in u_out_s)]
    outs = [loss, grad_x]
    for k in range(4):
        outs += [u[k] for u in per_w]
    return tuple(outs)
```

```python
import jax
import jax.numpy as jnp
from jax import lax
from jax.experimental import pallas as pl
from jax.experimental.pallas import tpu as pltpu

F32 = jnp.float32
BF16 = jnp.bfloat16

N_DEV = 8
BLOCK = 128
N_META = 16
N_PAD = BLOCK - N_META
HEAD = 128
SUB = 16
N_SUB = BLOCK // SUB
HALF = 8
EPS = 1e-6
ROW_TILE = 256
K_TILE = 768
NEG_BIG = -1e30

ADAM_LR = 0.001
ADAM_B1 = 0.9
ADAM_B2 = 0.999
ADAM_EPS = 1e-08
ADAM_WD = 0.01
ADAM_STEP = 10

VMEM_LIMIT = 56 * 1024 * 1024


def _pcall(body, **kw):
    return pl.pallas_call(body, **kw)


def _params(*sem):
    return pltpu.CompilerParams(dimension_semantics=sem, vmem_limit_bytes=VMEM_LIMIT)


def _dot(a, b):
    return jnp.dot(a, b, preferred_element_type=F32)


def _dot_nt(a, b):
    return lax.dot_general(a, b, (((1,), (1,)), ((), ())), preferred_element_type=F32)


def _dot_tn(a, b):
    return lax.dot_general(a, b, (((0,), (0,)), ((), ())), preferred_element_type=F32)


def _split(x, pieces):
    out = []
    for _ in range(pieces):
        p = x.astype(BF16)
        out.append(p)
        x = x - p.astype(F32)
    return out


def _tri_right(x, tri, pieces=2):
    return sum(_dot(p, tri) for p in _split(x, pieces))


def _tri_left(tri, x, pieces=3):
    return sum(_dot(tri, p) for p in _split(x, pieces))


def _iota2(shape, dim):
    return lax.broadcasted_iota(jnp.int32, shape, dim)


def _tri(cond):
    return jnp.where(cond, 1.0, 0.0).astype(BF16)


def _sig_pair(x):
    e = jnp.exp(-jnp.abs(x))
    r = 1.0 / (1.0 + e)
    er = e * r
    pos = x >= 0
    return jnp.where(pos, r, er), jnp.where(pos, er, r)


def _expand(rows):
    return jnp.concatenate([jnp.broadcast_to(r, (SUB, HEAD)) for r in rows], axis=0)


def _exchange(arrays, modes, name):
    n = len(arrays)
    out_shape = []
    for a, m in zip(arrays, modes):
        shp = a.shape[1:] if m == "scatter" else a.shape
        out_shape.append(jax.ShapeDtypeStruct((N_DEV,) + tuple(shp), a.dtype))

    def body(*refs):
        ins, outs = refs[:n], refs[n:2 * n]
        send_sems, recv_sems, local_sems = refs[2 * n:]
        mx, my, mc = lax.axis_index("x"), lax.axis_index("y"), lax.axis_index("c")
        me = 4 * mx + 2 * my + mc

        def src(i, slot):
            return ins[i].at[slot] if modes[i] == "scatter" else ins[i]

        def peer_of(mask):
            px = 1 - mx if mask & 4 else mx
            py = 1 - my if mask & 2 else my
            pc = 1 - mc if mask & 1 else mc
            return px, py, pc

        def copy(i, mask, dst_slot):
            px, py, pc = peer_of(mask)
            return pltpu.make_async_remote_copy(
                src_ref=src(i, 4 * px + 2 * py + pc), dst_ref=outs[i].at[dst_slot],
                send_sem=send_sems.at[i, mask - 1], recv_sem=recv_sems.at[i, mask - 1],
                device_id=(px, py, pc), device_id_type=pl.DeviceIdType.MESH)

        sends = []
        for mask in range(1, N_DEV):
            for i in range(n):
                cp = copy(i, mask, me)
                cp.start()
                sends.append(cp)
        own = []
        for i in range(n):
            cp = pltpu.make_async_copy(src(i, me), outs[i].at[me], local_sems.at[i])
            cp.start()
            own.append(cp)
        for mask in range(1, N_DEV):
            px, py, pc = peer_of(mask)
            for i in range(n):
                copy(i, mask, 4 * px + 2 * py + pc).wait_recv()
        for cp in sends:
            cp.wait_send()
        for cp in own:
            cp.wait()

    any_spec = pl.BlockSpec(memory_space=pl.ANY)
    return _pcall(
        body, name=name, out_shape=tuple(out_shape),
        in_specs=[any_spec] * n, out_specs=tuple([any_spec] * n),
        scratch_shapes=[pltpu.SemaphoreType.DMA((n, N_DEV - 1)), pltpu.SemaphoreType.DMA((n, N_DEV - 1)),
                        pltpu.SemaphoreType.DMA((n,))],
    )(*arrays)


def _norm_inproj(h, wnorm, w_all, main_dtype, name):
    t, d = h.shape
    p_n, _, s = w_all.shape
    n_main = 3 * d // s
    tm = ROW_TILE

    def body(h_ref, wn_ref, w_ref, main_ref, gate_ref, ynt_ref):
        x = h_ref[...]
        y = x * lax.rsqrt(jnp.mean(x * x, axis=-1, keepdims=True) + EPS) * wn_ref[...]
        yb = y.astype(BF16)
        ynt_ref[...] = y.T.astype(BF16)
        for p in range(p_n):
            r = _dot(yb, w_ref[p])
            if p < n_main:
                main_ref[:, p * s:(p + 1) * s] = r.astype(main_dtype)
            else:
                gate_ref[:, (p - n_main) * s:(p - n_main + 1) * s] = r

    return _pcall(
        body, name=name, grid=(t // tm,),
        in_specs=[pl.BlockSpec((tm, d), lambda i: (i, 0)), pl.BlockSpec((1, d), lambda i: (0, 0)),
                  pl.BlockSpec((p_n, d, s), lambda i: (0, 0, 0))],
        out_specs=(pl.BlockSpec((tm, 3 * d), lambda i: (i, 0)), pl.BlockSpec((tm, d), lambda i: (i, 0)),
                   pl.BlockSpec((d, tm), lambda i: (0, i))),
        out_shape=(jax.ShapeDtypeStruct((t, 3 * d), main_dtype), jax.ShapeDtypeStruct((t, d), F32),
                   jax.ShapeDtypeStruct((d, t), BF16)),
        compiler_params=_params("parallel"),
    )(h, wnorm, w_all)


def _mix_out(o, gate, h_in, w_out, w_on, w_post, head_norm, name):
    t, d = o.shape
    n_heads = d // HEAD
    tm = ROW_TILE

    def body(o_ref, g_ref, h_ref, w_ref, won_ref, wp_ref, hout_ref, y_ref, u_scr):
        for hh in range(n_heads):
            cs = slice(hh * HEAD, (hh + 1) * HEAD)
            oh = o_ref[:, cs]
            gt = g_ref[:, cs]
            if head_norm:
                oh = oh * lax.rsqrt(jnp.mean(oh * oh, axis=-1, keepdims=True) + EPS) * won_ref[...]
            u_scr[:, cs] = (oh * (gt * jax.nn.sigmoid(gt))).astype(BF16)
        y = _dot(u_scr[...], w_ref[...])
        y_ref[...] = y
        r = y * lax.rsqrt(jnp.mean(y * y, axis=-1, keepdims=True) + EPS) * wp_ref[...]
        hout_ref[...] = h_ref[...] + r

    row = pl.BlockSpec((tm, d), lambda i: (i, 0))
    return _pcall(
        body, name=name, grid=(t // tm,),
        in_specs=[row, row, row, pl.BlockSpec((d, d), lambda i: (0, 0)),
                  pl.BlockSpec((1, HEAD), lambda i: (0, 0)), pl.BlockSpec((1, d), lambda i: (0, 0))],
        out_specs=(row, row),
        out_shape=(jax.ShapeDtypeStruct((t, d), F32), jax.ShapeDtypeStruct((t, d), F32)),
        scratch_shapes=[pltpu.VMEM((tm, d), BF16)],
        compiler_params=_params("parallel"),
    )(o, gate, h_in, w_out, w_on, w_post)


def _loss_grad(h_out, target, n_blocks, name):
    t, d = h_out.shape

    def body(h_ref, t_ref, dh_ref, loss_ref):
        i = pl.program_id(0)

        @pl.when(i == 0)
        def _():
            loss_ref[...] = jnp.zeros_like(loss_ref)

        real = (i % n_blocks) > 0
        err = jnp.where(real, h_ref[...] - t_ref[...], 0.0)
        dh_ref[...] = err * (1.0 / d)
        part = jnp.sum(jnp.sum(err * err, axis=-1, keepdims=True), axis=0, keepdims=True)
        loss_ref[...] += part * (0.5 / d)

    return _pcall(
        body, name=name, grid=(t // BLOCK,),
        in_specs=[pl.BlockSpec((BLOCK, d), lambda i: (i, 0)),
                  pl.BlockSpec((None, BLOCK, d), lambda i: (i // n_blocks, jnp.maximum(i % n_blocks - 1, 0), 0))],
        out_specs=(pl.BlockSpec((BLOCK, d), lambda i: (i, 0)), pl.BlockSpec((8, 128), lambda i: (0, 0))),
        out_shape=(jax.ShapeDtypeStruct((t, d), F32), jax.ShapeDtypeStruct((8, 128), F32)),
        compiler_params=_params("arbitrary"),
    )(h_out, target)


def _mix_out_bwd(dh, y, o, gate, w_out, w_on, w_post, head_norm, name):
    t, d = o.shape
    n_heads = d // HEAD
    tm = ROW_TILE
    last = t // tm - 1

    def body(dh_ref, y_ref, o_ref, g_ref, w_ref, won_ref, wp_ref,
             do_ref, dg_ref, dw_ref, dwp_ref, dwon_ref, u_scr, acc):
        i = pl.program_id(0)

        @pl.when(i == 0)
        def _():
            acc[...] = jnp.zeros_like(acc)
            dwp_ref[...] = jnp.zeros_like(dwp_ref)
            dwon_ref[...] = jnp.zeros_like(dwon_ref)

        yv = y_ref[...]
        rs = lax.rsqrt(jnp.mean(yv * yv, axis=-1, keepdims=True) + EPS)
        yh = yv * rs
        dr = dh_ref[...]
        dwp_ref[...] += jnp.sum(dr * yh, axis=0, keepdims=True)
        wd = dr * wp_ref[...]
        dy = rs * (wd - yh * jnp.mean(wd * yh, axis=-1, keepdims=True))
        dyb = dy.astype(BF16)
        du = _dot_nt(dyb, w_ref[...])
        for hh in range(n_heads):
            cs = slice(hh * HEAD, (hh + 1) * HEAD)
            oh = o_ref[:, cs]
            gt = g_ref[:, cs]
            sg = jax.nn.sigmoid(gt)
            sl = gt * sg
            duh = du[:, cs]
            if head_norm:
                rsh = lax.rsqrt(jnp.mean(oh * oh, axis=-1, keepdims=True) + EPS)
                ohat = oh * rsh
                on = ohat * won_ref[...]
            else:
                on = oh
            u_scr[:, cs] = (on * sl).astype(BF16)
            don = duh * sl
            dg_ref[:, cs] = (duh * on * (sg * (1.0 + gt * (1.0 - sg)))).astype(BF16)
            if head_norm:
                dwon_ref[...] += jnp.sum(don * ohat, axis=0, keepdims=True)
                wdn = don * won_ref[...]
                do_ref[:, cs] = rsh * (wdn - ohat * jnp.mean(wdn * ohat, axis=-1, keepdims=True))
            else:
                do_ref[:, cs] = don
        acc[...] += _dot_tn(u_scr[...], dyb)

        @pl.when(i == last)
        def _():
            dw_ref[...] = acc[...].astype(BF16)

    row = pl.BlockSpec((tm, d), lambda i: (i, 0))
    return _pcall(
        body, name=name, grid=(t // tm,),
        in_specs=[row, row, row, row, pl.BlockSpec((d, d), lambda i: (0, 0)),
                  pl.BlockSpec((1, HEAD), lambda i: (0, 0)), pl.BlockSpec((1, d), lambda i: (0, 0))],
        out_specs=(row, row, pl.BlockSpec((d, d), lambda i: (0, 0)), pl.BlockSpec((1, d), lambda i: (0, 0)),
                   pl.BlockSpec((1, HEAD), lambda i: (0, 0))),
        out_shape=(jax.ShapeDtypeStruct((t, d), F32), jax.ShapeDtypeStruct((t, d), BF16),
                   jax.ShapeDtypeStruct((d, d), BF16), jax.ShapeDtypeStruct((1, d), F32),
                   jax.ShapeDtypeStruct((1, HEAD), F32)),
        scratch_shapes=[pltpu.VMEM((tm, d), BF16), pltpu.VMEM((d, d), F32)],
        compiler_params=_params("arbitrary"),
    )(dh, y, o, gate, w_out, w_on, w_post)


def _inproj_bwd_x(dparts, w_all, h, wnorm, dres, name):
    t, d = h.shape
    p_n, _, s = w_all.shape
    per = d // s
    tm = ROW_TILE

    def body(d0_ref, d1_ref, d2_ref, d3_ref, w_ref, h_ref, wn_ref, dres_ref, dh_ref, dwn_ref):
        i = pl.program_id(0)

        @pl.when(i == 0)
        def _():
            dwn_ref[...] = jnp.zeros_like(dwn_ref)

        pieces = (d0_ref, d1_ref, d2_ref, d3_ref)
        dyn = jnp.zeros((tm, d), F32)
        for p in range(p_n):
            blk = pieces[p // per][:, (p % per) * s:(p % per + 1) * s]
            dyn = dyn + _dot_nt(blk, w_ref[p])
        x = h_ref[...]
        rs = lax.rsqrt(jnp.mean(x * x, axis=-1, keepdims=True) + EPS)
        xh = x * rs
        dwn_ref[...] += jnp.sum(dyn * xh, axis=0, keepdims=True)
        wd = dyn * wn_ref[...]
        dh_ref[...] = dres_ref[...] + rs * (wd - xh * jnp.mean(wd * xh, axis=-1, keepdims=True))

    row = pl.BlockSpec((tm, d), lambda i: (i, 0))
    return _pcall(
        body, name=name, grid=(t // tm,),
        in_specs=[row, row, row, row, pl.BlockSpec((p_n, d, s), lambda i: (0, 0, 0)),
                  row, pl.BlockSpec((1, d), lambda i: (0, 0)), row],
        out_specs=(row, pl.BlockSpec((1, d), lambda i: (0, 0))),
        out_shape=(jax.ShapeDtypeStruct((t, d), F32), jax.ShapeDtypeStruct((1, d), F32)),
        compiler_params=_params("arbitrary"),
    )(*dparts, w_all, h, wnorm, dres)


def _inproj_bwd_w(ynt, dparts, s, name):
    d, t = ynt.shape
    per = d // s
    n_sh = len(dparts) * per
    tk = K_TILE
    last = t // tk - 1

    def body(ynt_ref, d0_ref, d1_ref, d2_ref, d3_ref, dw_ref, acc):
        p, i = pl.program_id(0), pl.program_id(1)

        @pl.when(i == 0)
        def _():
            acc[...] = jnp.zeros_like(acc)

        for a, piece in enumerate((d0_ref, d1_ref, d2_ref, d3_ref)):
            @pl.when(p // per == a)
            def _():
                acc[...] += _dot(ynt_ref[...], piece[...])

        @pl.when(i == last)
        def _():
            dw_ref[...] = acc[...].astype(BF16)

    def piece_spec(a):
        return pl.BlockSpec((tk, s), lambda p, i: (jnp.where(p // per == a, i, 0),
                                                   jnp.where(p // per == a, p % per, 0)))

    return _pcall(
        body, name=name, grid=(n_sh, t // tk),
        in_specs=[pl.BlockSpec((d, tk), lambda p, i: (0, i))] + [piece_spec(a) for a in range(4)],
        out_specs=pl.BlockSpec((None, d, s), lambda p, i: (p, 0, 0)),
        out_shape=jax.ShapeDtypeStruct((n_sh, d, s), BF16),
        scratch_shapes=[pltpu.VMEM((d, s), F32)],
        compiler_params=_params("parallel", "arbitrary"),
    )(ynt, *dparts)


def _hgrn_common(q_ref, fz_ref, lb, b_scr, k_scr):
    fz = fz_ref[...]
    sig, nsig = _sig_pair(fz)
    f = lb + (1.0 - lb) * sig
    k = (1.0 - lb) * nsig
    rr, cc = _iota2((BLOCK, BLOCK), 0), _iota2((BLOCK, BLOCK), 1)
    b = _tri_left(_tri(cc <= rr), jnp.log(f))
    b_scr[...] = b
    k_scr[...] = k
    bend = [b_scr[pl.ds(SUB * j + SUB - 1, 1), :] for j in range(N_SUB)]
    bref = [jnp.zeros((1, HEAD), F32)] + bend[:-1]
    refrow, bendrow = _expand(bref), _expand(bend)
    e_q = jnp.exp(b - refrow)
    e_k = jnp.exp(bendrow - b)
    qt = q_ref[...] * e_q
    kh = k * e_k
    bl = bend[-1]
    return dict(sig=sig, nsig=nsig, f=f, k=k, b=b, bend=bend, bref=bref, refrow=refrow,
                e_q=e_q, e_k=e_k, qt=qt, kh=kh, bl=bl)


def _hgrn_fwd(main, lbrow, bsz, n_blocks, name):
    t, d3 = main.shape
    d = d3 // 3
    n_heads = d // HEAD

    def body(q_ref, fz_ref, v_ref, lb_ref, o_ref, st_ref, s_scr, b_scr, k_scr, o_acc):
        n = pl.program_id(2)

        @pl.when(n == 0)
        def _():
            s_scr[...] = jnp.zeros_like(s_scr)

        c = _hgrn_common(q_ref, fz_ref, lb_ref[...], b_scr, k_scr)
        s_t = s_scr[...]
        st_ref[...] = s_t
        v = v_ref[...]
        vb = v.astype(BF16)
        q_state = c["qt"] * _expand([jnp.exp(r) for r in c["bref"]])
        o_acc[...] = _dot_nt(q_state.astype(BF16), s_t.astype(BF16))
        js = range(N_SUB - 1)
        khb = c["kh"].astype(BF16)
        lhs = [(c["qt"][SUB * (j + 1):] * jnp.exp(c["refrow"][SUB * (j + 1):] - c["bend"][j])).astype(BF16)
               for j in js]
        a_js = [_dot_nt(lhs[j], khb[SUB * j:SUB * (j + 1)]) for j in js]
        o_js = [_dot(a_js[j].astype(BF16), vb[SUB * j:SUB * (j + 1)]) for j in js]
        for j in js:
            o_acc[SUB * (j + 1):, :] += o_js[j]
        t_loc = _iota2((HALF, HEAD), 0)
        for i in range(N_SUB):
            r0 = SUB * i
            q_h = [q_ref[pl.ds(r0 + HALF * u, HALF), :] for u in range(2)]
            b_h = [b_scr[pl.ds(r0 + HALF * u, HALF), :] for u in range(2)]
            o_h = [jnp.zeros((HALF, HEAD), F32) for _ in range(2)]
            for s in range(SUB):
                brow = b_scr[pl.ds(r0 + s, 1), :]
                krow = k_scr[pl.ds(r0 + s, 1), :]
                vrow = v_ref[pl.ds(r0 + s, 1), :]
                for u in range(s // HALF, 2):
                    diff = b_h[u] - brow
                    if u == s // HALF:
                        diff = jnp.where(t_loc >= s - HALF * u, diff, NEG_BIG)
                    col = jnp.sum(q_h[u] * krow * jnp.exp(diff), axis=-1, keepdims=True)
                    o_h[u] = o_h[u] + col * vrow
            for u in range(2):
                o_acc[pl.ds(r0 + HALF * u, HALF), :] += o_h[u]
        o_ref[...] = o_acc[...]
        k_state = c["kh"] * _expand([jnp.exp(c["bl"] - r) for r in c["bend"]])
        s_scr[...] = s_t * jnp.exp(c["bl"]) + _dot_tn(vb, k_state.astype(BF16))

    def blk(col0):
        return pl.BlockSpec((BLOCK, HEAD), lambda b, h, n: (b * n_blocks + n, col0 + h))

    return _pcall(
        body, name=name, grid=(bsz, n_heads, n_blocks),
        in_specs=[blk(0), blk(n_heads), blk(2 * n_heads), pl.BlockSpec((1, HEAD), lambda b, h, n: (0, h))],
        out_specs=(blk(0), pl.BlockSpec((None, HEAD, HEAD), lambda b, h, n: ((b * n_heads + h) * n_blocks + n, 0, 0))),
        out_shape=(jax.ShapeDtypeStruct((t, d), F32),
                   jax.ShapeDtypeStruct((bsz * n_heads * n_blocks, HEAD, HEAD), F32)),
        scratch_shapes=[pltpu.VMEM((HEAD, HEAD), F32), pltpu.VMEM((BLOCK, HEAD), F32),
                        pltpu.VMEM((BLOCK, HEAD), F32), pltpu.VMEM((BLOCK, HEAD), F32)],
        compiler_params=_params("parallel", "parallel", "arbitrary"),
    )(main, main, main, lbrow)


def _hgrn_bwd(main, lbrow, states, do, bsz, n_blocks, name):
    t, d3 = main.shape
    d = d3 // 3
    n_heads = d // HEAD

    def body(q_ref, fz_ref, v_ref, lb_ref, st_ref, do_ref, dq_ref, dfz_ref, dv_ref, dlb_ref,
             ds_scr, b_scr, k_scr, dqt_acc, dkh_acc, dv_acc, dqd_acc, dkd_acc, ad_scr):
        step = pl.program_id(2)
        n = n_blocks - 1 - step

        @pl.when(step == 0)
        def _():
            ds_scr[...] = jnp.zeros_like(ds_scr)
            dlb_ref[...] = jnp.zeros_like(dlb_ref)

        lb = lb_ref[...]
        c = _hgrn_common(q_ref, fz_ref, lb, b_scr, k_scr)
        q, k, b = q_ref[...], c["k"], c["b"]
        v = v_ref[...]
        vb = v.astype(BF16)
        dout = do_ref[...]
        dob = dout.astype(BF16)
        s0_t = st_ref[...]
        ds1_t = ds_scr[...]
        e_ref = _expand([jnp.exp(r) for r in c["bref"]])
        e_end = _expand([jnp.exp(c["bl"] - r) for r in c["bend"]])
        e_bl = jnp.exp(c["bl"])
        q_state = c["qt"] * e_ref
        k_state = c["kh"] * e_end
        dqt_acc[...] = _dot(dob, s0_t.astype(BF16)) * e_ref
        dk_state = _dot(vb, ds1_t.astype(BF16))
        dkh_acc[...] = dk_state * e_end
        dv_acc[...] = _dot_nt(k_state.astype(BF16), ds1_t.astype(BF16))
        dbl = (jnp.sum(s0_t * ds1_t, axis=0, keepdims=True) * e_bl
               + jnp.sum(k_state * dk_state, axis=0, keepdims=True))
        ds_scr[...] = ds1_t * e_bl + _dot_tn(dob, q_state.astype(BF16))
        js = range(N_SUB - 1)
        lo = [slice(SUB * j, SUB * (j + 1)) for j in js]
        khb = c["kh"].astype(BF16)
        dj = [jnp.exp(c["refrow"][SUB * (j + 1):] - c["bend"][j]) for j in js]
        lhs = [(c["qt"][SUB * (j + 1):] * dj[j]).astype(BF16) for j in js]
        a_js = [_dot_nt(lhs[j], khb[lo[j]]) for j in js]
        da_js = [_dot_nt(dob[SUB * (j + 1):], vb[lo[j]]).astype(BF16) for j in js]
        dv_js = [_dot_tn(a_js[j].astype(BF16), dob[SUB * (j + 1):]) for j in js]
        dq_js = [_dot(da_js[j], khb[lo[j]]) * dj[j] for j in js]
        dk_js = [_dot_tn(da_js[j], lhs[j]) for j in js]
        for j in js:
            dv_acc[lo[j], :] += dv_js[j]
            dqt_acc[SUB * (j + 1):, :] += dq_js[j]
            dkh_acc[lo[j], :] += dk_js[j]
        t_loc = _iota2((HALF, HEAD), 0)
        lane = _iota2((HALF, HEAD), 1)
        for i in range(N_SUB):
            r0 = SUB * i
            q_h = [q_ref[pl.ds(r0 + HALF * u, HALF), :] for u in range(2)]
            b_h = [b_scr[pl.ds(r0 + HALF * u, HALF), :] for u in range(2)]
            do_h = [do_ref[pl.ds(r0 + HALF * u, HALF), :] for u in range(2)]
            zero = jnp.zeros((HALF, HEAD), F32)
            dq_h, dk_h, a_h = [zero, zero], [zero, zero], [zero, zero]
            for s in range(SUB):
                brow = b_scr[pl.ds(r0 + s, 1), :]
                krow = k_scr[pl.ds(r0 + s, 1), :]
                vrow = v_ref[pl.ds(r0 + s, 1), :]
                dk_row = jnp.zeros((1, HEAD), F32)
                for u in range(s // HALF, 2):
                    diff = b_h[u] - brow
                    if u == s // HALF:
                        diff = jnp.where(t_loc >= s - HALF * u, diff, NEG_BIG)
                    w = jnp.exp(diff)
                    qw = q_h[u] * w
                    a_col = jnp.sum(qw * krow, axis=-1, keepdims=True)
                    da_col = jnp.sum(do_h[u] * vrow, axis=-1, keepdims=True)
                    a_h[u] = jnp.where(lane == r0 + s, a_col, a_h[u])
                    dq_h[u] = dq_h[u] + da_col * (w * krow)
                    dk_row = dk_row + jnp.sum(da_col * qw, axis=0, keepdims=True)
                us = s // HALF
                dk_h[us] = jnp.where(t_loc == s - HALF * us, dk_row, dk_h[us])
            for u in range(2):
                rows = pl.ds(r0 + HALF * u, HALF)
                dqd_acc[rows, :] = dq_h[u]
                dkd_acc[rows, :] = dk_h[u]
                ad_scr[rows, :] = a_h[u]
        dv_acc[...] += _dot_tn(ad_scr[...].astype(BF16), dob)
        dq = dqt_acc[...] * c["e_q"] + dqd_acc[...]
        dk = dkh_acc[...] * c["e_k"] + dkd_acc[...]
        rr, cc = _iota2((BLOCK, BLOCK), 0), _iota2((BLOCK, BLOCK), 1)
        db = q * dq - k * dk + jnp.where(_iota2((BLOCK, HEAD), 0) == BLOCK - 1, dbl, 0.0)
        dg = _tri_left(_tri(cc >= rr), db)
        real = jnp.logical_or(n > 0, _iota2((BLOCK, HEAD), 0) >= N_PAD)
        df = jnp.where(real, dg / c["f"] - dk, 0.0)
        dq_ref[...] = dq.astype(BF16)
        dv_ref[...] = dv_acc[...].astype(BF16)
        dfz_ref[...] = (df * (1.0 - lb) * c["sig"] * c["nsig"]).astype(BF16)
        dlb_ref[...] += jnp.sum(df * c["nsig"], axis=0, keepdims=True)

    def blk(col0):
        return pl.BlockSpec((BLOCK, HEAD), lambda b, h, s: (b * n_blocks + n_blocks - 1 - s, col0 + h))

    return _pcall(
        body, name=name, grid=(bsz, n_heads, n_blocks),
        in_specs=[blk(0), blk(n_heads), blk(2 * n_heads), pl.BlockSpec((1, HEAD), lambda b, h, s: (0, h)),
                  pl.BlockSpec((None, HEAD, HEAD),
                               lambda b, h, s: ((b * n_heads + h) * n_blocks + n_blocks - 1 - s, 0, 0)),
                  blk(0)],
        out_specs=(blk(0), blk(0), blk(0), pl.BlockSpec((None, 1, HEAD), lambda b, h, s: (b, 0, h))),
        out_shape=(jax.ShapeDtypeStruct((t, d), BF16),) * 3 + (jax.ShapeDtypeStruct((bsz, 1, d), F32),),
        scratch_shapes=[pltpu.VMEM((HEAD, HEAD), F32)] + [pltpu.VMEM((BLOCK, HEAD), F32)] * 8,
        compiler_params=_params("parallel", "parallel", "arbitrary"),
    )(main, main, main, lbrow, states, do)


SB_GROUP = 3
SB_ROWS = SB_GROUP * BLOCK
SB_DEAD = -104.0


def _sb_tables():
    j = jnp.bitwise_and(_iota2((2 * BLOCK, 2 * BLOCK), 0), BLOCK - 1)
    s = _iota2((2 * BLOCK, 2 * BLOCK), 1)
    ones = s >= BLOCK
    return (_tri(jnp.logical_or(ones, j > s)), _tri(jnp.logical_or(ones, j <= s)),
            _tri(jnp.logical_or(ones, j < s)))


def _sums(x, table):
    hi = x.astype(BF16)
    lo = (x - hi.astype(F32)).astype(BF16)
    r = _dot(jnp.concatenate([hi, lo], axis=1), table)
    return r[:, :BLOCK], r[:, BLOCK:]


def _sb_logits(q, ks, scale, causal, pad_row):
    z = _dot_nt(q, ks) * scale
    log_keep = -(jnp.maximum(z, 0.0) + jnp.log(1.0 + jnp.exp(-jnp.abs(z))))
    log_beta = z + log_keep
    if causal is not None:
        log_keep = jnp.where(causal, log_keep, 0.0)
    if pad_row is not None:
        log_keep = log_keep * pad_row
    return z, log_beta, log_keep


def _sb_fwd(qkv, bsz, n_blocks, name):
    t, d3 = qkv.shape
    d = d3 // 3
    n_heads = d // HEAD
    lp = n_blocks * BLOCK
    n_groups = n_blocks // SB_GROUP
    assert n_groups * SB_GROUP == n_blocks
    scale = HEAD ** -0.5

    def body(q_ref, k_ref, v_ref, o_ref, tot_ref, stop_ref, c_scr):
        g = pl.program_id(2)
        upper, _, _ = _sb_tables()
        causal = _iota2((BLOCK, BLOCK), 1) < _iota2((BLOCK, BLOCK), 0)
        lane = _iota2((1, BLOCK), 1)
        o_ref[...] = jnp.zeros_like(o_ref)
        c_scr[...] = jnp.zeros_like(c_scr)

        def tiles(jobs):
            qrows = [pl.ds(r * BLOCK, BLOCK) for r, _, _, _ in jobs]
            krows = [pl.ds(pl.multiple_of(m * BLOCK, BLOCK), BLOCK) for _, m, _, _ in jobs]
            lg = [_sb_logits(q_ref[qr, :], k_ref[kr, :], scale, causal if dg else None, pad_row)
                  for qr, kr, (_, _, dg, pad_row) in zip(qrows, krows, jobs)]
            sm = [_sums(x[2], upper) for x in lg]
            a_all = []
            for qr, (_, _, dg, _), (_, log_beta, _), (inner, total) in zip(qrows, jobs, lg, sm):
                c = c_scr[qr, :]
                a = jnp.exp(log_beta + c + inner)
                a_all.append(jnp.where(causal, a, 0.0) if dg else a)
                c_scr[qr, :] = c + total
            out = [_dot(a.astype(BF16), v_ref[kr, :]) for a, kr in zip(a_all, krows)]
            for qr, o_part in zip(qrows, out):
                o_ref[qr, :] += o_part

        def pad_row_of(m):
            return jnp.where(jnp.logical_or(m > 0, lane >= N_PAD), 1.0, 0.0)

        base = SB_GROUP * g
        tiles([(r, base + kb, r == kb, pad_row_of(base) if kb == 0 else None)
               for kb in range(SB_GROUP - 1, -1, -1) for r in range(kb, SB_GROUP)])

        def live(least):
            def cond(carry):
                m, c_max = carry
                return jnp.logical_and(m >= least, c_max >= SB_DEAD)
            return cond

        def step_two(carry):
            m, _ = carry
            pad_row = pad_row_of(m - 1)
            tiles([(r, m, False, None) for r in range(SB_GROUP)]
                  + [(r, m - 1, False, pad_row) for r in range(SB_GROUP)])
            return m - 2, jnp.max(c_scr[...])

        def step_one(carry):
            m, _ = carry
            pad_row = pad_row_of(m)
            tiles([(r, m, False, pad_row) for r in range(SB_GROUP)])
            return m - 1, jnp.max(c_scr[...])

        carry = lax.while_loop(live(1), step_two, (base - 1, jnp.max(c_scr[...])))
        m_end, _ = lax.while_loop(live(0), step_one, carry)
        tot_ref[...] = c_scr[...]
        stop_ref[...] = jnp.broadcast_to((m_end + 1).astype(F32), stop_ref.shape)

    qblk = pl.BlockSpec((SB_ROWS, HEAD), lambda b, h, g: (b * n_groups + g, h))
    return _pcall(
        body, name=name, grid=(bsz, n_heads, n_groups),
        in_specs=[qblk, pl.BlockSpec((lp, HEAD), lambda b, h, g: (b, n_heads + h)),
                  pl.BlockSpec((lp, HEAD), lambda b, h, g: (b, 2 * n_heads + h))],
        out_specs=(qblk, qblk, pl.BlockSpec((None, 8, 128), lambda b, h, g: ((b * n_heads + h) * n_groups + g, 0, 0))),
        out_shape=(jax.ShapeDtypeStruct((t, d), F32), jax.ShapeDtypeStruct((t, d), F32),
                   jax.ShapeDtypeStruct((bsz * n_heads * n_groups, 8, 128), F32)),
        scratch_shapes=[pltpu.VMEM((SB_ROWS, HEAD), F32)],
        compiler_params=_params("parallel", "parallel", "arbitrary"),
    )(qkv, qkv, qkv)


def _sb_bwd(qkv, do, tot, stop, bsz, n_blocks, name):
    t, d3 = qkv.shape
    d = d3 // 3
    n_heads = d // HEAD
    lp = n_blocks * BLOCK
    n_groups = n_blocks // SB_GROUP
    scale = HEAD ** -0.5

    def body(q_ref, k_ref, v_ref, do_ref, tot_ref, stop_ref, dq_ref, dk_ref, dv_ref,
             dk_acc, dv_acc, dq_acc, p_scr, e_scr, dob_scr):
        g = pl.program_id(2)

        @pl.when(g == 0)
        def _():
            dk_acc[...] = jnp.zeros_like(dk_acc)
            dv_acc[...] = jnp.zeros_like(dv_acc)

        _, incl, excl = _sb_tables()
        causal = _iota2((BLOCK, BLOCK), 1) < _iota2((BLOCK, BLOCK), 0)
        lane = _iota2((1, BLOCK), 1)
        dob_scr[...] = do_ref[...].astype(BF16)
        dq_acc[...] = jnp.zeros_like(dq_acc)
        p_scr[...] = jnp.zeros_like(p_scr)
        e_scr[...] = jnp.zeros_like(e_scr)
        first = jnp.clip(jnp.max(stop_ref[...]).astype(jnp.int32), 0, SB_GROUP * g)

        def tiles(jobs):
            n_jobs = range(len(jobs))
            qrows = [pl.ds(r * BLOCK, BLOCK) for r, _, _, _ in jobs]
            krows = [pl.ds(pl.multiple_of(m * BLOCK, BLOCK), BLOCK) for _, m, _, _ in jobs]
            diag = [dg for _, _, dg, _ in jobs]
            lg = [_sb_logits(q_ref[qrows[i], :], k_ref[krows[i], :], scale, causal if diag[i] else None, jobs[i][3])
                  for i in n_jobs]
            d_a = [_dot_nt(dob_scr[qrows[i], :], v_ref[krows[i], :]) for i in n_jobs]
            sm = [_sums(lg[i][2], incl) for i in n_jobs]
            a_all = []
            for i in n_jobs:
                p = p_scr[qrows[i], :]
                a = jnp.exp(lg[i][1] + (tot_ref[qrows[i], :] - p - sm[i][0]))
                a_all.append(jnp.where(causal, a, 0.0) if diag[i] else a)
                p_scr[qrows[i], :] = p + sm[i][1]
            gr = [d_a[i] * a_all[i] for i in n_jobs]
            dv_part = [_dot_tn(a_all[i].astype(BF16), dob_scr[qrows[i], :]) for i in n_jobs]
            gs = [_sums(gr[i], excl) for i in n_jobs]
            dz_all = []
            for i in n_jobs:
                e = e_scr[qrows[i], :]
                dz = gr[i] - (gr[i] + e + gs[i][0]) * jnp.exp(lg[i][1])
                if diag[i]:
                    dz = jnp.where(causal, dz, 0.0)
                dz_all.append((dz * scale).astype(BF16))
                e_scr[qrows[i], :] = e + gs[i][1]
            dk_part = [_dot_tn(dz_all[i], q_ref[qrows[i], :]) for i in n_jobs]
            dq_part = [_dot(dz_all[i], k_ref[krows[i], :]) for i in n_jobs]
            for i in n_jobs:
                dv_acc[krows[i], :] += dv_part[i]
                dk_acc[krows[i], :] += dk_part[i]
                dq_acc[qrows[i], :] += dq_part[i]

        def pad_row_of(m):
            return jnp.where(jnp.logical_or(m > 0, lane >= N_PAD), 1.0, 0.0)

        def step_two(carry):
            m = carry
            pad_row = pad_row_of(m)
            tiles([(r, m, False, pad_row) for r in range(SB_GROUP)]
                  + [(r, m + 1, False, None) for r in range(SB_GROUP)])
            return m + 2

        def step_one(carry):
            m = carry
            pad_row = pad_row_of(m)
            tiles([(r, m, False, pad_row) for r in range(SB_GROUP)])
            return m + 1

        base = SB_GROUP * g
        m = lax.while_loop(lambda m: m + 1 < base, step_two, first)
        lax.while_loop(lambda m: m < base, step_one, m)
        tiles([(r, base + kb, r == kb, pad_row_of(base) if kb == 0 else None)
               for kb in range(SB_GROUP) for r in range(kb, SB_GROUP)])
        dq_ref[...] = dq_acc[...].astype(BF16)

        @pl.when(g == n_groups - 1)
        def _():
            dk_ref[...] = dk_acc[...].astype(BF16)
            dv_ref[...] = dv_acc[...].astype(BF16)

    qblk = pl.BlockSpec((SB_ROWS, HEAD), lambda b, h, g: (b * n_groups + g, h))
    kblk = pl.BlockSpec((lp, HEAD), lambda b, h, g: (b, n_heads + h))
    vblk = pl.BlockSpec((lp, HEAD), lambda b, h, g: (b, 2 * n_heads + h))
    hblk = pl.BlockSpec((lp, HEAD), lambda b, h, g: (b, h))
    sblk = pl.BlockSpec((None, 8, 128), lambda b, h, g: ((b * n_heads + h) * n_groups + g, 0, 0))
    return _pcall(
        body, name=name, grid=(bsz, n_heads, n_groups),
        in_specs=[qblk, kblk, vblk, qblk, qblk, sblk],
        out_specs=(qblk, hblk, hblk),
        out_shape=(jax.ShapeDtypeStruct((t, d), BF16),) * 3,
        scratch_shapes=[pltpu.VMEM((lp, HEAD), F32), pltpu.VMEM((lp, HEAD), F32)]
        + [pltpu.VMEM((SB_ROWS, HEAD), F32)] * 3 + [pltpu.VMEM((SB_ROWS, HEAD), BF16)],
        compiler_params=_params("parallel", "parallel", "arbitrary"),
    )(qkv, qkv, qkv, do, tot, stop)


def _adamw(w, g, m, v):
    m = ADAM_B1 * m + (1.0 - ADAM_B1) * g
    v = ADAM_B2 * v + (1.0 - ADAM_B2) * (g * g)
    m_hat = m / (1.0 - ADAM_B1 ** ADAM_STEP)
    v_hat = v / (1.0 - ADAM_B2 ** ADAM_STEP)
    delta = -ADAM_LR * (m_hat / (jnp.sqrt(v_hat) + ADAM_EPS) + ADAM_WD * w)
    return delta, m, v


def _update_sharded(w, parts, m, v, name):
    r, c = w.shape
    tr = min(r, ROW_TILE)

    def body(w_ref, p_ref, m_ref, v_ref, g_ref, d_ref, nm_ref, nv_ref):
        g = p_ref[0].astype(F32)
        for q in range(1, N_DEV):
            g = g + p_ref[q].astype(F32)
        g_ref[...] = g
        d_ref[...], nm_ref[...], nv_ref[...] = _adamw(w_ref[...], g, m_ref[...], v_ref[...])

    row = pl.BlockSpec((tr, c), lambda i: (i, 0))
    return _pcall(
        body, name=name, grid=(r // tr,),
        in_specs=[row, pl.BlockSpec((N_DEV, tr, c), lambda i: (0, i, 0)), row, row],
        out_specs=(row,) * 4, out_shape=(jax.ShapeDtypeStruct((r, c), F32),) * 4,
        compiler_params=_params("parallel"),
    )(w, parts, m, v)


SMALL_ROWS = 8


def _pack_small(dpre0, dpre1, dpost0, dpost1, dlb, dwon, loss, name):
    d = dpre0.shape[1]
    bsz = dlb.shape[0]

    def body(a0, a1, p0, p1, lb_ref, on_ref, loss_ref, out_ref):
        out_ref[...] = jnp.zeros_like(out_ref)
        out_ref[pl.ds(0, 1), :] = a0[...]
        out_ref[pl.ds(1, 1), :] = a1[...]
        out_ref[pl.ds(2, 1), :] = p0[...]
        out_ref[pl.ds(3, 1), :] = p1[...]
        acc = lb_ref[0]
        for b in range(1, bsz):
            acc = acc + lb_ref[b]
        out_ref[pl.ds(4, 1), :] = acc
        out_ref[pl.ds(5, 1), pl.ds(0, HEAD)] = on_ref[...]
        out_ref[pl.ds(6, 1), pl.ds(0, HEAD)] = loss_ref[pl.ds(0, 1), :]

    return _pcall(body, name=name, out_shape=jax.ShapeDtypeStruct((SMALL_ROWS, d), F32))(
        dpre0, dpre1, dpost0, dpost1, dlb, dwon, loss)


def _update_small(parts, pre, post, lbw, on, moments, name):
    d = pre.shape[1]

    def body(p_ref, pre_ref, post_ref, lbw_ref, on_ref, mpre, mpost, mlb, mon, vpre, vpost, vlb, von,
             loss_ref, *outs):
        def total(r0, nr, width):
            acc = p_ref[0, pl.ds(r0, nr), pl.ds(0, width)]
            for q in range(1, N_DEV):
                acc = acc + p_ref[q, pl.ds(r0, nr), pl.ds(0, width)]
            return acc

        def put(k, w, g, m, v):
            dl, nm, nv = _adamw(w, g, m, v)
            outs[4 * k][...] = g
            outs[4 * k + 1][...] = dl
            outs[4 * k + 2][...] = nm
            outs[4 * k + 3][...] = nv

        put(0, pre_ref[...], total(0, 2, d), mpre[...], vpre[...])
        put(1, post_ref[...], total(2, 2, d), mpost[...], vpost[...])
        a0, a1 = lbw_ref[pl.ds(0, 1), :], lbw_ref[pl.ds(1, 1), :]
        mx = jnp.maximum(a0, a1)
        e0, e1 = jnp.exp(a0 - mx), jnp.exp(a1 - mx)
        p0 = e0 / (e0 + e1)
        g0 = total(4, 1, d) * p0 * (1.0 - p0)
        for r, w, g in ((0, a0, g0), (1, a1, -g0)):
            row = pl.ds(r, 1)
            dl, nm, nv = _adamw(w, g, mlb[row, :], vlb[row, :])
            outs[8][row, :] = g
            outs[9][row, :] = dl
            outs[10][row, :] = nm
            outs[11][row, :] = nv
        put(3, on_ref[...], total(5, 1, HEAD), mon[...], von[...])
        loss_ref[...] = jnp.broadcast_to(total(6, 1, HEAD), loss_ref.shape)

    shapes = []
    for w in (pre, post, lbw, on):
        shapes += [jax.ShapeDtypeStruct(w.shape, F32)] * 4
    return _pcall(body, name=name, out_shape=(jax.ShapeDtypeStruct((8, 128), F32), *shapes))(
        parts, pre, post, lbw, on, *moments)


def kernel(x, meta_tokens, pre_norm, post_norm, hgrn_w_in, hgrn_lb, hgrn_out_norm, hgrn_w_out, sb_w_in, sb_w_out, loss_target, m_meta_tokens, m_pre_norm, m_post_norm, m_hgrn_w_in, m_hgrn_lb, m_hgrn_out_norm, m_hgrn_w_out, m_sb_w_in, m_sb_w_out, v_meta_tokens, v_pre_norm, v_post_norm, v_hgrn_w_in, v_hgrn_lb, v_hgrn_out_norm, v_hgrn_w_out, v_sb_w_in, v_sb_w_out):
    bsz, seq, d = x.shape
    n_blocks = seq // BLOCK + 1
    lp = n_blocks * BLOCK
    t = bsz * lp
    s = hgrn_w_in.shape[2]
    dsh = d // N_DEV

    w_in_h, w_out_h, w_in_s, w_out_s, meta_all = _exchange(
        [hgrn_w_in[0].astype(BF16), hgrn_w_out[0].astype(BF16), sb_w_in[0].astype(BF16),
         sb_w_out[0].astype(BF16), meta_tokens],
        ["gather"] * 5, "gather_weights")
    w_out_h = w_out_h.reshape(d, d)
    w_out_s = w_out_s.reshape(d, d)
    meta_full = jnp.transpose(meta_all, (1, 0, 2)).reshape(N_META, d)

    h0 = jnp.concatenate(
        [jnp.zeros((bsz, N_PAD, d), F32), jnp.broadcast_to(meta_full[None], (bsz, N_META, d)), x], axis=1
    ).reshape(t, d)
    lbrow = jnp.cumsum(jax.nn.softmax(hgrn_lb, axis=0), axis=0)[0:1]

    main0, gate0, yn0 = _norm_inproj(h0, pre_norm[0:1], w_in_h, F32, "inproj_hgrn")
    o0, states = _hgrn_fwd(main0, lbrow, bsz, n_blocks, "hgrn_fwd")
    h1, y0 = _mix_out(o0, gate0, h0, w_out_h, hgrn_out_norm, post_norm[0:1], True, "mix_out_hgrn")
    main1, gate1, yn1 = _norm_inproj(h1, pre_norm[1:2], w_in_s, BF16, "inproj_sb")
    o1, tot, stop = _sb_fwd(main1, bsz, n_blocks, "sb_fwd")
    h2, y1 = _mix_out(o1, gate1, h1, w_out_s, hgrn_out_norm, post_norm[1:2], False, "mix_out_sb")
    dh2, loss_part = _loss_grad(h2, loss_target, n_blocks, "loss_grad")

    do1, dgate1, dw_out_s, dpost1, _ = _mix_out_bwd(
        dh2, y1, o1, gate1, w_out_s, hgrn_out_norm, post_norm[1:2], False, "mix_out_sb_bwd")
    dq1, dk1, dv1 = _sb_bwd(main1, do1, tot, stop, bsz, n_blocks, "sb_bwd")
    dproj1 = (dq1, dk1, dv1, dgate1)
    dh1, dpre1 = _inproj_bwd_x(dproj1, w_in_s, h1, pre_norm[1:2], dh2, "inproj_sb_bwd_x")
    dw_in_s = _inproj_bwd_w(yn1, dproj1, s, "inproj_sb_bwd_w")

    do0, dgate0, dw_out_h, dpost0, dwon = _mix_out_bwd(
        dh1, y0, o0, gate0, w_out_h, hgrn_out_norm, post_norm[0:1], True, "mix_out_hgrn_bwd")
    dq0, dfz0, dv0, dlb = _hgrn_bwd(main0, lbrow, states, do0, bsz, n_blocks, "hgrn_bwd")
    dproj0 = (dq0, dfz0, dv0, dgate0)
    dh0, dpre0 = _inproj_bwd_x(dproj0, w_in_h, h0, pre_norm[0:1], dh1, "inproj_hgrn_bwd_x")
    dw_in_h = _inproj_bwd_w(yn0, dproj0, s, "inproj_hgrn_bwd_w")

    dh0 = dh0.reshape(bsz, lp, d)
    grad_x = dh0[:, BLOCK:]
    dmeta = jnp.sum(dh0[:, N_PAD:BLOCK], axis=0)
    dmeta = jnp.transpose(dmeta.reshape(N_META, N_DEV, dsh), (1, 0, 2))
    small = _pack_small(dpre0, dpre1, dpost0, dpost1, dlb, dwon, loss_part, "pack_small")

    p_in_h, p_out_h, p_in_s, p_out_s, p_meta, p_small = _exchange(
        [dw_in_h, dw_out_h.reshape(N_DEV, dsh, d), dw_in_s, dw_out_s.reshape(N_DEV, dsh, d), dmeta, small],
        ["scatter"] * 5 + ["gather"], "exchange_grads")

    u_meta = _update_sharded(meta_tokens, p_meta, m_meta_tokens, v_meta_tokens, "update_meta")
    u_in_h = _update_sharded(hgrn_w_in[0], p_in_h, m_hgrn_w_in[0], v_hgrn_w_in[0], "update_hgrn_w_in")
    u_out_h = _update_sharded(hgrn_w_out[0], p_out_h, m_hgrn_w_out[0], v_hgrn_w_out[0], "update_hgrn_w_out")
    u_in_s = _update_sharded(sb_w_in[0], p_in_s, m_sb_w_in[0], v_sb_w_in[0], "update_sb_w_in")
    u_out_s = _update_sharded(sb_w_out[0], p_out_s, m_sb_w_out[0], v_sb_w_out[0], "update_sb_w_out")
    sm = _update_small(p_small, pre_norm, post_norm, hgrn_lb, hgrn_out_norm,
                       (m_pre_norm, m_post_norm, m_hgrn_lb, m_hgrn_out_norm,
                        v_pre_norm, v_post_norm, v_hgrn_lb, v_hgrn_out_norm), "update_small")
    loss = sm[0][0, 0]
    u_pre, u_post, u_lb, u_on = sm[1:5], sm[5:9], sm[9:13], sm[13:17]

    per_w = [u_meta, u_pre, u_post, tuple(a[None] for a in u_in_h), u_lb, u_on,
             tuple(a[None] for a in u_out_h), tuple(a[None] for a in u_in_s), tuple(a[None] for a in u_out_s)]
    outs = [loss, grad_x]
    for k in range(4):
        outs += [u[k] for u in per_w]
    return tuple(outs)
```

```python
import jax
import jax.numpy as jnp
from jax import lax
from jax.experimental import pallas as pl
from jax.experimental.pallas import tpu as pltpu

F32 = jnp.float32
BF16 = jnp.bfloat16

N_DEV = 8
BLOCK = 128
N_META = 16
N_PAD = BLOCK - N_META
HEAD = 128
SUB = 16
N_SUB = BLOCK // SUB
HALF = 8
EPS = 1e-6
ROW_TILE = 256
K_TILE = 768
NEG_BIG = -1e30

ADAM_LR = 0.001
ADAM_B1 = 0.9
ADAM_B2 = 0.999
ADAM_EPS = 1e-08
ADAM_WD = 0.01
ADAM_STEP = 10

VMEM_LIMIT = 56 * 1024 * 1024


def _pcall(body, **kw):
    return pl.pallas_call(body, **kw)


def _params(*sem):
    return pltpu.CompilerParams(dimension_semantics=sem, vmem_limit_bytes=VMEM_LIMIT)


def _dot(a, b):
    return jnp.dot(a, b, preferred_element_type=F32)


def _dot_nt(a, b):
    return lax.dot_general(a, b, (((1,), (1,)), ((), ())), preferred_element_type=F32)


def _dot_tn(a, b):
    return lax.dot_general(a, b, (((0,), (0,)), ((), ())), preferred_element_type=F32)


def _split(x, pieces):
    out = []
    for _ in range(pieces):
        p = x.astype(BF16)
        out.append(p)
        x = x - p.astype(F32)
    return out


def _tri_right(x, tri, pieces=2):
    return sum(_dot(p, tri) for p in _split(x, pieces))


def _tri_left(tri, x, pieces=3):
    return sum(_dot(tri, p) for p in _split(x, pieces))


def _iota2(shape, dim):
    return lax.broadcasted_iota(jnp.int32, shape, dim)


def _tri(cond):
    return jnp.where(cond, 1.0, 0.0).astype(BF16)


def _sig_pair(x):
    e = jnp.exp(-jnp.abs(x))
    r = 1.0 / (1.0 + e)
    er = e * r
    pos = x >= 0
    return jnp.where(pos, r, er), jnp.where(pos, er, r)


def _expand(rows):
    return jnp.concatenate([jnp.broadcast_to(r, (SUB, HEAD)) for r in rows], axis=0)


def _exchange_shapes(arrays, modes):
    return tuple(jax.ShapeDtypeStruct((N_DEV,) + tuple(a.shape[1:] if m == "scatter" else a.shape), a.dtype)
                 for a, m in zip(arrays, modes))


def _exchange_sems(n):
    if n == 0:
        return []
    return [pltpu.SemaphoreType.DMA((n, N_DEV - 1)), pltpu.SemaphoreType.DMA((n, N_DEV - 1)),
            pltpu.SemaphoreType.DMA((n,))]


def _exchange_copies(ins, outs, modes, send_sems, recv_sems, local_sems):
    mx, my, mc = lax.axis_index("x"), lax.axis_index("y"), lax.axis_index("c")
    me = 4 * mx + 2 * my + mc

    def src(i, slot):
        return ins[i].at[slot] if modes[i] == "scatter" else ins[i]

    def peer_of(mask):
        px = 1 - mx if mask & 4 else mx
        py = 1 - my if mask & 2 else my
        pc = 1 - mc if mask & 1 else mc
        return px, py, pc

    def copy(i, mask, dst_slot):
        px, py, pc = peer_of(mask)
        return pltpu.make_async_remote_copy(
            src_ref=src(i, 4 * px + 2 * py + pc), dst_ref=outs[i].at[dst_slot],
            send_sem=send_sems.at[i, mask - 1], recv_sem=recv_sems.at[i, mask - 1],
            device_id=(px, py, pc), device_id_type=pl.DeviceIdType.MESH)

    n = len(ins)
    sends = [copy(i, mask, me) for mask in range(1, N_DEV) for i in range(n)]
    own = [pltpu.make_async_copy(src(i, me), outs[i].at[me], local_sems.at[i]) for i in range(n)]
    arrivals = []
    for mask in range(1, N_DEV):
        px, py, pc = peer_of(mask)
        arrivals += [copy(i, mask, 4 * px + 2 * py + pc) for i in range(n)]

    def start():
        for cp in sends + own:
            cp.start()

    def wait():
        for cp in arrivals:
            cp.wait_recv()
        for cp in sends:
            cp.wait_send()
        for cp in own:
            cp.wait()

    return start, wait


def _exchange(arrays, modes, name):
    n = len(arrays)

    def body(*refs):
        start, wait = _exchange_copies(refs[:n], refs[n:2 * n], modes, *refs[2 * n:])
        start()
        wait()

    any_spec = pl.BlockSpec(memory_space=pl.ANY)
    return _pcall(
        body, name=name, out_shape=_exchange_shapes(arrays, modes),
        in_specs=[any_spec] * n, out_specs=tuple([any_spec] * n),
        scratch_shapes=_exchange_sems(n),
    )(*arrays)


def _norm_inproj(h, wnorm, w_all, main_dtype, name):
    t, d = h.shape
    p_n, _, s = w_all.shape
    n_main = 3 * d // s
    tm = ROW_TILE

    def body(h_ref, wn_ref, w_ref, main_ref, gate_ref, ynt_ref):
        x = h_ref[...]
        y = x * lax.rsqrt(jnp.mean(x * x, axis=-1, keepdims=True) + EPS) * wn_ref[...]
        yb = y.astype(BF16)
        ynt_ref[...] = y.T.astype(BF16)
        for p in range(p_n):
            r = _dot(yb, w_ref[p])
            if p < n_main:
                main_ref[:, p * s:(p + 1) * s] = r.astype(main_dtype)
            else:
                gate_ref[:, (p - n_main) * s:(p - n_main + 1) * s] = r

    return _pcall(
        body, name=name, grid=(t // tm,),
        in_specs=[pl.BlockSpec((tm, d), lambda i: (i, 0)), pl.BlockSpec((1, d), lambda i: (0, 0)),
                  pl.BlockSpec((p_n, d, s), lambda i: (0, 0, 0))],
        out_specs=(pl.BlockSpec((tm, 3 * d), lambda i: (i, 0)), pl.BlockSpec((tm, d), lambda i: (i, 0)),
                   pl.BlockSpec((d, tm), lambda i: (0, i))),
        out_shape=(jax.ShapeDtypeStruct((t, 3 * d), main_dtype), jax.ShapeDtypeStruct((t, d), F32),
                   jax.ShapeDtypeStruct((d, t), BF16)),
        compiler_params=_params("parallel"),
    )(h, wnorm, w_all)


def _mix_out(o, gate, h_in, w_out, w_on, w_post, head_norm, name):
    t, d = o.shape
    n_heads = d // HEAD
    tm = ROW_TILE

    def body(o_ref, g_ref, h_ref, w_ref, won_ref, wp_ref, hout_ref, y_ref, u_scr):
        for hh in range(n_heads):
            cs = slice(hh * HEAD, (hh + 1) * HEAD)
            oh = o_ref[:, cs]
            gt = g_ref[:, cs]
            if head_norm:
                oh = oh * lax.rsqrt(jnp.mean(oh * oh, axis=-1, keepdims=True) + EPS) * won_ref[...]
            u_scr[:, cs] = (oh * (gt * jax.nn.sigmoid(gt))).astype(BF16)
        y = _dot(u_scr[...], w_ref[...])
        y_ref[...] = y
        r = y * lax.rsqrt(jnp.mean(y * y, axis=-1, keepdims=True) + EPS) * wp_ref[...]
        hout_ref[...] = h_ref[...] + r

    row = pl.BlockSpec((tm, d), lambda i: (i, 0))
    return _pcall(
        body, name=name, grid=(t // tm,),
        in_specs=[row, row, row, pl.BlockSpec((d, d), lambda i: (0, 0)),
                  pl.BlockSpec((1, HEAD), lambda i: (0, 0)), pl.BlockSpec((1, d), lambda i: (0, 0))],
        out_specs=(row, row),
        out_shape=(jax.ShapeDtypeStruct((t, d), F32), jax.ShapeDtypeStruct((t, d), F32)),
        scratch_shapes=[pltpu.VMEM((tm, d), BF16)],
        compiler_params=_params("parallel"),
    )(o, gate, h_in, w_out, w_on, w_post)


def _loss_grad(h_out, target, n_blocks, name):
    t, d = h_out.shape

    def body(h_ref, t_ref, dh_ref, loss_ref):
        i = pl.program_id(0)

        @pl.when(i == 0)
        def _():
            loss_ref[...] = jnp.zeros_like(loss_ref)

        real = (i % n_blocks) > 0
        err = jnp.where(real, h_ref[...] - t_ref[...], 0.0)
        dh_ref[...] = err * (1.0 / d)
        part = jnp.sum(jnp.sum(err * err, axis=-1, keepdims=True), axis=0, keepdims=True)
        loss_ref[...] += part * (0.5 / d)

    return _pcall(
        body, name=name, grid=(t // BLOCK,),
        in_specs=[pl.BlockSpec((BLOCK, d), lambda i: (i, 0)),
                  pl.BlockSpec((None, BLOCK, d), lambda i: (i // n_blocks, jnp.maximum(i % n_blocks - 1, 0), 0))],
        out_specs=(pl.BlockSpec((BLOCK, d), lambda i: (i, 0)), pl.BlockSpec((8, 128), lambda i: (0, 0))),
        out_shape=(jax.ShapeDtypeStruct((t, d), F32), jax.ShapeDtypeStruct((8, 128), F32)),
        compiler_params=_params("arbitrary"),
    )(h_out, target)


def _mix_out_bwd(dh, y, o, gate, w_out, w_on, w_post, head_norm, name):
    t, d = o.shape
    n_heads = d // HEAD
    tm = ROW_TILE
    last = t // tm - 1

    def body(dh_ref, y_ref, o_ref, g_ref, w_ref, won_ref, wp_ref,
             do_ref, dg_ref, dw_ref, dwp_ref, dwon_ref, u_scr, acc):
        i = pl.program_id(0)

        @pl.when(i == 0)
        def _():
            acc[...] = jnp.zeros_like(acc)
            dwp_ref[...] = jnp.zeros_like(dwp_ref)
            dwon_ref[...] = jnp.zeros_like(dwon_ref)

        yv = y_ref[...]
        rs = lax.rsqrt(jnp.mean(yv * yv, axis=-1, keepdims=True) + EPS)
        yh = yv * rs
        dr = dh_ref[...]
        dwp_ref[...] += jnp.sum(dr * yh, axis=0, keepdims=True)
        wd = dr * wp_ref[...]
        dy = rs * (wd - yh * jnp.mean(wd * yh, axis=-1, keepdims=True))
        dyb = dy.astype(BF16)
        du = _dot_nt(dyb, w_ref[...])
        for hh in range(n_heads):
            cs = slice(hh * HEAD, (hh + 1) * HEAD)
            oh = o_ref[:, cs]
            gt = g_ref[:, cs]
            sg = jax.nn.sigmoid(gt)
            sl = gt * sg
            duh = du[:, cs]
            if head_norm:
                rsh = lax.rsqrt(jnp.mean(oh * oh, axis=-1, keepdims=True) + EPS)
                ohat = oh * rsh
                on = ohat * won_ref[...]
            else:
                on = oh
            u_scr[:, cs] = (on * sl).astype(BF16)
            don = duh * sl
            dg_ref[:, cs] = (duh * on * (sg * (1.0 + gt * (1.0 - sg)))).astype(BF16)
            if head_norm:
                dwon_ref[...] += jnp.sum(don * ohat, axis=0, keepdims=True)
                wdn = don * won_ref[...]
                do_ref[:, cs] = rsh * (wdn - ohat * jnp.mean(wdn * ohat, axis=-1, keepdims=True))
            else:
                do_ref[:, cs] = don
        acc[...] += _dot_tn(u_scr[...], dyb)

        @pl.when(i == last)
        def _():
            dw_ref[...] = acc[...].astype(BF16)

    row = pl.BlockSpec((tm, d), lambda i: (i, 0))
    return _pcall(
        body, name=name, grid=(t // tm,),
        in_specs=[row, row, row, row, pl.BlockSpec((d, d), lambda i: (0, 0)),
                  pl.BlockSpec((1, HEAD), lambda i: (0, 0)), pl.BlockSpec((1, d), lambda i: (0, 0))],
        out_specs=(row, row, pl.BlockSpec((d, d), lambda i: (0, 0)), pl.BlockSpec((1, d), lambda i: (0, 0)),
                   pl.BlockSpec((1, HEAD), lambda i: (0, 0))),
        out_shape=(jax.ShapeDtypeStruct((t, d), F32), jax.ShapeDtypeStruct((t, d), BF16),
                   jax.ShapeDtypeStruct((d, d), BF16), jax.ShapeDtypeStruct((1, d), F32),
                   jax.ShapeDtypeStruct((1, HEAD), F32)),
        scratch_shapes=[pltpu.VMEM((tm, d), BF16), pltpu.VMEM((d, d), F32)],
        compiler_params=_params("arbitrary"),
    )(dh, y, o, gate, w_out, w_on, w_post)


def _inproj_bwd_x(dparts, w_all, h, wnorm, dres, name):
    t, d = h.shape
    p_n, _, s = w_all.shape
    per = d // s
    tm = ROW_TILE

    def body(d0_ref, d1_ref, d2_ref, d3_ref, w_ref, h_ref, wn_ref, dres_ref, dh_ref, dwn_ref):
        i = pl.program_id(0)

        @pl.when(i == 0)
        def _():
            dwn_ref[...] = jnp.zeros_like(dwn_ref)

        pieces = (d0_ref, d1_ref, d2_ref, d3_ref)
        dyn = jnp.zeros((tm, d), F32)
        for p in range(p_n):
            blk = pieces[p // per][:, (p % per) * s:(p % per + 1) * s]
            dyn = dyn + _dot_nt(blk, w_ref[p])
        x = h_ref[...]
        rs = lax.rsqrt(jnp.mean(x * x, axis=-1, keepdims=True) + EPS)
        xh = x * rs
        dwn_ref[...] += jnp.sum(dyn * xh, axis=0, keepdims=True)
        wd = dyn * wn_ref[...]
        dh_ref[...] = dres_ref[...] + rs * (wd - xh * jnp.mean(wd * xh, axis=-1, keepdims=True))

    row = pl.BlockSpec((tm, d), lambda i: (i, 0))
    return _pcall(
        body, name=name, grid=(t // tm,),
        in_specs=[row, row, row, row, pl.BlockSpec((p_n, d, s), lambda i: (0, 0, 0)),
                  row, pl.BlockSpec((1, d), lambda i: (0, 0)), row],
        out_specs=(row, pl.BlockSpec((1, d), lambda i: (0, 0))),
        out_shape=(jax.ShapeDtypeStruct((t, d), F32), jax.ShapeDtypeStruct((1, d), F32)),
        compiler_params=_params("arbitrary"),
    )(*dparts, w_all, h, wnorm, dres)


def _inproj_bwd_w(ynt, dparts, s, name):
    d, t = ynt.shape
    per = d // s
    n_sh = len(dparts) * per
    tk = K_TILE
    last = t // tk - 1

    def body(ynt_ref, d0_ref, d1_ref, d2_ref, d3_ref, dw_ref, acc):
        p, i = pl.program_id(0), pl.program_id(1)

        @pl.when(i == 0)
        def _():
            acc[...] = jnp.zeros_like(acc)

        for a, piece in enumerate((d0_ref, d1_ref, d2_ref, d3_ref)):
            @pl.when(p // per == a)
            def _():
                acc[...] += _dot(ynt_ref[...], piece[...])

        @pl.when(i == last)
        def _():
            dw_ref[...] = acc[...].astype(BF16)

    def piece_spec(a):
        return pl.BlockSpec((tk, s), lambda p, i: (jnp.where(p // per == a, i, 0),
                                                   jnp.where(p // per == a, p % per, 0)))

    return _pcall(
        body, name=name, grid=(n_sh, t // tk),
        in_specs=[pl.BlockSpec((d, tk), lambda p, i: (0, i))] + [piece_spec(a) for a in range(4)],
        out_specs=pl.BlockSpec((None, d, s), lambda p, i: (p, 0, 0)),
        out_shape=jax.ShapeDtypeStruct((n_sh, d, s), BF16),
        scratch_shapes=[pltpu.VMEM((d, s), F32)],
        compiler_params=_params("parallel", "arbitrary"),
    )(ynt, *dparts)


def _hgrn_common(q_ref, fz_ref, lb, b_scr, k_scr):
    fz = fz_ref[...]
    sig, nsig = _sig_pair(fz)
    f = lb + (1.0 - lb) * sig
    k = (1.0 - lb) * nsig
    rr, cc = _iota2((BLOCK, BLOCK), 0), _iota2((BLOCK, BLOCK), 1)
    b = _tri_left(_tri(cc <= rr), jnp.log(f))
    b_scr[...] = b
    k_scr[...] = k
    bend = [b_scr[pl.ds(SUB * j + SUB - 1, 1), :] for j in range(N_SUB)]
    bref = [jnp.zeros((1, HEAD), F32)] + bend[:-1]
    refrow, bendrow = _expand(bref), _expand(bend)
    e_q = jnp.exp(b - refrow)
    e_k = jnp.exp(bendrow - b)
    qt = q_ref[...] * e_q
    kh = k * e_k
    bl = bend[-1]
    return dict(sig=sig, nsig=nsig, f=f, k=k, b=b, bend=bend, bref=bref, refrow=refrow,
                e_q=e_q, e_k=e_k, qt=qt, kh=kh, bl=bl)


def _ride_along(ride_ins, ride_outs, modes, sems, grid):
    if not ride_ins:
        return lambda: None
    ids = [pl.program_id(a) for a in range(len(grid))]
    start, wait = _exchange_copies(ride_ins, ride_outs, modes, *sems)
    first, last = ids[0] == 0, ids[0] == grid[0] - 1
    for a in range(1, len(grid)):
        first = jnp.logical_and(first, ids[a] == 0)
        last = jnp.logical_and(last, ids[a] == grid[a] - 1)
    pl.when(first)(start)
    return lambda: pl.when(last)(wait)


def _hgrn_fwd(main, lbrow, bsz, n_blocks, ride, ride_modes, name):
    t, d3 = main.shape
    d = d3 // 3
    n_heads = d // HEAD
    nr = len(ride)
    grid = (bsz, n_heads, n_blocks)

    def body(*refs):
        q_ref, fz_ref, v_ref, lb_ref = refs[:4]
        o_ref, st_ref = refs[4 + nr:6 + nr]
        s_scr, b_scr, k_scr, o_acc = refs[6 + 2 * nr:10 + 2 * nr]
        finish_ride = _ride_along(refs[4:4 + nr], refs[6 + nr:6 + 2 * nr], ride_modes, refs[10 + 2 * nr:], grid)
        n = pl.program_id(2)

        @pl.when(n == 0)
        def _():
            s_scr[...] = jnp.zeros_like(s_scr)

        c = _hgrn_common(q_ref, fz_ref, lb_ref[...], b_scr, k_scr)
        s_t = s_scr[...]
        st_ref[...] = s_t
        v = v_ref[...]
        vb = v.astype(BF16)
        q_state = c["qt"] * _expand([jnp.exp(r) for r in c["bref"]])
        o_acc[...] = _dot_nt(q_state.astype(BF16), s_t.astype(BF16))
        js = range(N_SUB - 1)
        khb = c["kh"].astype(BF16)
        lhs = [(c["qt"][SUB * (j + 1):] * jnp.exp(c["refrow"][SUB * (j + 1):] - c["bend"][j])).astype(BF16)
               for j in js]
        a_js = [_dot_nt(lhs[j], khb[SUB * j:SUB * (j + 1)]) for j in js]
        o_js = [_dot(a_js[j].astype(BF16), vb[SUB * j:SUB * (j + 1)]) for j in js]
        for j in js:
            o_acc[SUB * (j + 1):, :] += o_js[j]
        t_loc = _iota2((HALF, HEAD), 0)
        for i in range(N_SUB):
            r0 = SUB * i
            q_h = [q_ref[pl.ds(r0 + HALF * u, HALF), :] for u in range(2)]
            b_h = [b_scr[pl.ds(r0 + HALF * u, HALF), :] for u in range(2)]
            o_h = [jnp.zeros((HALF, HEAD), F32) for _ in range(2)]
            for s in range(SUB):
                brow = b_scr[pl.ds(r0 + s, 1), :]
                krow = k_scr[pl.ds(r0 + s, 1), :]
                vrow = v_ref[pl.ds(r0 + s, 1), :]
                for u in range(s // HALF, 2):
                    diff = b_h[u] - brow
                    if u == s // HALF:
                        diff = jnp.where(t_loc >= s - HALF * u, diff, NEG_BIG)
                    col = jnp.sum(q_h[u] * krow * jnp.exp(diff), axis=-1, keepdims=True)
                    o_h[u] = o_h[u] + col * vrow
            for u in range(2):
                o_acc[pl.ds(r0 + HALF * u, HALF), :] += o_h[u]
        o_ref[...] = o_acc[...]
        k_state = c["kh"] * _expand([jnp.exp(c["bl"] - r) for r in c["bend"]])
        s_scr[...] = s_t * jnp.exp(c["bl"]) + _dot_tn(vb, k_state.astype(BF16))
        finish_ride()

    def blk(col0):
        return pl.BlockSpec((BLOCK, HEAD), lambda b, h, n: (b * n_blocks + n, col0 + h))

    any_spec = pl.BlockSpec(memory_space=pl.ANY)
    return _pcall(
        body, name=name, grid=grid,
        in_specs=[blk(0), blk(n_heads), blk(2 * n_heads), pl.BlockSpec((1, HEAD), lambda b, h, n: (0, h))]
        + [any_spec] * nr,
        out_specs=(blk(0), pl.BlockSpec((None, HEAD, HEAD), lambda b, h, n: ((b * n_heads + h) * n_blocks + n, 0, 0)))
        + (any_spec,) * nr,
        out_shape=(jax.ShapeDtypeStruct((t, d), F32),
                   jax.ShapeDtypeStruct((bsz * n_heads * n_blocks, HEAD, HEAD), F32))
        + _exchange_shapes(ride, ride_modes),
        scratch_shapes=[pltpu.VMEM((HEAD, HEAD), F32), pltpu.VMEM((BLOCK, HEAD), F32),
                        pltpu.VMEM((BLOCK, HEAD), F32), pltpu.VMEM((BLOCK, HEAD), F32)] + _exchange_sems(nr),
        compiler_params=_params("arbitrary", "arbitrary", "arbitrary"),
    )(main, main, main, lbrow, *ride)


def _hgrn_bwd(main, lbrow, states, do, bsz, n_blocks, ride, ride_modes, name):
    t, d3 = main.shape
    d = d3 // 3
    n_heads = d // HEAD
    nr = len(ride)
    grid = (bsz, n_heads, n_blocks)

    def body(*refs):
        q_ref, fz_ref, v_ref, lb_ref, st_ref, do_ref = refs[:6]
        dq_ref, dfz_ref, dv_ref, dlb_ref = refs[6 + nr:10 + nr]
        ds_scr, b_scr, k_scr, dqt_acc, dkh_acc, dv_acc, dqd_acc, dkd_acc, ad_scr = refs[10 + 2 * nr:19 + 2 * nr]
        finish_ride = _ride_along(refs[6:6 + nr], refs[10 + nr:10 + 2 * nr], ride_modes, refs[19 + 2 * nr:], grid)
        step = pl.program_id(2)
        n = n_blocks - 1 - step

        @pl.when(step == 0)
        def _():
            ds_scr[...] = jnp.zeros_like(ds_scr)
            dlb_ref[...] = jnp.zeros_like(dlb_ref)

        lb = lb_ref[...]
        c = _hgrn_common(q_ref, fz_ref, lb, b_scr, k_scr)
        q, k, b = q_ref[...], c["k"], c["b"]
        v = v_ref[...]
        vb = v.astype(BF16)
        dout = do_ref[...]
        dob = dout.astype(BF16)
        s0_t = st_ref[...]
        ds1_t = ds_scr[...]
        e_ref = _expand([jnp.exp(r) for r in c["bref"]])
        e_end = _expand([jnp.exp(c["bl"] - r) for r in c["bend"]])
        e_bl = jnp.exp(c["bl"])
        q_state = c["qt"] * e_ref
        k_state = c["kh"] * e_end
        dqt_acc[...] = _dot(dob, s0_t.astype(BF16)) * e_ref
        dk_state = _dot(vb, ds1_t.astype(BF16))
        dkh_acc[...] = dk_state * e_end
        dv_acc[...] = _dot_nt(k_state.astype(BF16), ds1_t.astype(BF16))
        dbl = (jnp.sum(s0_t * ds1_t, axis=0, keepdims=True) * e_bl
               + jnp.sum(k_state * dk_state, axis=0, keepdims=True))
        ds_scr[...] = ds1_t * e_bl + _dot_tn(dob, q_state.astype(BF16))
        js = range(N_SUB - 1)
        lo = [slice(SUB * j, SUB * (j + 1)) for j in js]
        khb = c["kh"].astype(BF16)
        dj = [jnp.exp(c["refrow"][SUB * (j + 1):] - c["bend"][j]) for j in js]
        lhs = [(c["qt"][SUB * (j + 1):] * dj[j]).astype(BF16) for j in js]
        a_js = [_dot_nt(lhs[j], khb[lo[j]]) for j in js]
        da_js = [_dot_nt(dob[SUB * (j + 1):], vb[lo[j]]).astype(BF16) for j in js]
        dv_js = [_dot_tn(a_js[j].astype(BF16), dob[SUB * (j + 1):]) for j in js]
        dq_js = [_dot(da_js[j], khb[lo[j]]) * dj[j] for j in js]
        dk_js = [_dot_tn(da_js[j], lhs[j]) for j in js]
        for j in js:
            dv_acc[lo[j], :] += dv_js[j]
            dqt_acc[SUB * (j + 1):, :] += dq_js[j]
            dkh_acc[lo[j], :] += dk_js[j]
        t_loc = _iota2((HALF, HEAD), 0)
        lane = _iota2((HALF, HEAD), 1)
        for i in range(N_SUB):
            r0 = SUB * i
            q_h = [q_ref[pl.ds(r0 + HALF * u, HALF), :] for u in range(2)]
            b_h = [b_scr[pl.ds(r0 + HALF * u, HALF), :] for u in range(2)]
            do_h = [do_ref[pl.ds(r0 + HALF * u, HALF), :] for u in range(2)]
            zero = jnp.zeros((HALF, HEAD), F32)
            dq_h, dk_h, a_h = [zero, zero], [zero, zero], [zero, zero]
            for s in range(SUB):
                brow = b_scr[pl.ds(r0 + s, 1), :]
                krow = k_scr[pl.ds(r0 + s, 1), :]
                vrow = v_ref[pl.ds(r0 + s, 1), :]
                dk_row = jnp.zeros((1, HEAD), F32)
                for u in range(s // HALF, 2):
                    diff = b_h[u] - brow
                    if u == s // HALF:
                        diff = jnp.where(t_loc >= s - HALF * u, diff, NEG_BIG)
                    w = jnp.exp(diff)
                    qw = q_h[u] * w
                    a_col = jnp.sum(qw * krow, axis=-1, keepdims=True)
                    da_col = jnp.sum(do_h[u] * vrow, axis=-1, keepdims=True)
                    a_h[u] = jnp.where(lane == r0 + s, a_col, a_h[u])
                    dq_h[u] = dq_h[u] + da_col * (w * krow)
                    dk_row = dk_row + jnp.sum(da_col * qw, axis=0, keepdims=True)
                us = s // HALF
                dk_h[us] = jnp.where(t_loc == s - HALF * us, dk_row, dk_h[us])
            for u in range(2):
                rows = pl.ds(r0 + HALF * u, HALF)
                dqd_acc[rows, :] = dq_h[u]
                dkd_acc[rows, :] = dk_h[u]
                ad_scr[rows, :] = a_h[u]
        dv_acc[...] += _dot_tn(ad_scr[...].astype(BF16), dob)
        dq = dqt_acc[...] * c["e_q"] + dqd_acc[...]
        dk = dkh_acc[...] * c["e_k"] + dkd_acc[...]
        rr, cc = _iota2((BLOCK, BLOCK), 0), _iota2((BLOCK, BLOCK), 1)
        db = q * dq - k * dk + jnp.where(_iota2((BLOCK, HEAD), 0) == BLOCK - 1, dbl, 0.0)
        dg = _tri_left(_tri(cc >= rr), db)
        real = jnp.logical_or(n > 0, _iota2((BLOCK, HEAD), 0) >= N_PAD)
        df = jnp.where(real, dg / c["f"] - dk, 0.0)
        dq_ref[...] = dq.astype(BF16)
        dv_ref[...] = dv_acc[...].astype(BF16)
        dfz_ref[...] = (df * (1.0 - lb) * c["sig"] * c["nsig"]).astype(BF16)
        dlb_ref[...] += jnp.sum(df * c["nsig"], axis=0, keepdims=True)
        finish_ride()

    def blk(col0):
        return pl.BlockSpec((BLOCK, HEAD), lambda b, h, s: (b * n_blocks + n_blocks - 1 - s, col0 + h))

    any_spec = pl.BlockSpec(memory_space=pl.ANY)
    return _pcall(
        body, name=name, grid=grid,
        in_specs=[blk(0), blk(n_heads), blk(2 * n_heads), pl.BlockSpec((1, HEAD), lambda b, h, s: (0, h)),
                  pl.BlockSpec((None, HEAD, HEAD),
                               lambda b, h, s: ((b * n_heads + h) * n_blocks + n_blocks - 1 - s, 0, 0)),
                  blk(0)] + [any_spec] * nr,
        out_specs=(blk(0), blk(0), blk(0), pl.BlockSpec((None, 1, HEAD), lambda b, h, s: (b, 0, h)))
        + (any_spec,) * nr,
        out_shape=(jax.ShapeDtypeStruct((t, d), BF16),) * 3 + (jax.ShapeDtypeStruct((bsz, 1, d), F32),)
        + _exchange_shapes(ride, ride_modes),
        scratch_shapes=[pltpu.VMEM((HEAD, HEAD), F32)] + [pltpu.VMEM((BLOCK, HEAD), F32)] * 8 + _exchange_sems(nr),
        compiler_params=_params("arbitrary", "arbitrary", "arbitrary"),
    )(main, main, main, lbrow, states, do, *ride)


SB_GROUP = 3
SB_ROWS = SB_GROUP * BLOCK
SB_DEAD = -104.0


def _sb_tables():
    j = jnp.bitwise_and(_iota2((2 * BLOCK, 2 * BLOCK), 0), BLOCK - 1)
    s = _iota2((2 * BLOCK, 2 * BLOCK), 1)
    ones = s >= BLOCK
    return (_tri(jnp.logical_or(ones, j > s)), _tri(jnp.logical_or(ones, j <= s)),
            _tri(jnp.logical_or(ones, j < s)))


def _sums(x, table):
    hi = x.astype(BF16)
    lo = (x - hi.astype(F32)).astype(BF16)
    r = _dot(jnp.concatenate([hi, lo], axis=1), table)
    return r[:, :BLOCK], r[:, BLOCK:]


def _sb_logits(q, ks, scale, causal, pad_row):
    z = _dot_nt(q, ks) * scale
    log_keep = -(jnp.maximum(z, 0.0) + jnp.log(1.0 + jnp.exp(-jnp.abs(z))))
    log_beta = z + log_keep
    if causal is not None:
        log_keep = jnp.where(causal, log_keep, 0.0)
    if pad_row is not None:
        log_keep = log_keep * pad_row
    return z, log_beta, log_keep


def _sb_fwd(qkv, bsz, n_blocks, name):
    t, d3 = qkv.shape
    d = d3 // 3
    n_heads = d // HEAD
    lp = n_blocks * BLOCK
    n_groups = n_blocks // SB_GROUP
    assert n_groups * SB_GROUP == n_blocks
    scale = HEAD ** -0.5

    def body(q_ref, k_ref, v_ref, o_ref, tot_ref, stop_ref, c_scr):
        g = pl.program_id(2)
        upper, _, _ = _sb_tables()
        causal = _iota2((BLOCK, BLOCK), 1) < _iota2((BLOCK, BLOCK), 0)
        lane = _iota2((1, BLOCK), 1)
        o_ref[...] = jnp.zeros_like(o_ref)
        c_scr[...] = jnp.zeros_like(c_scr)

        def tiles(jobs):
            qrows = [pl.ds(r * BLOCK, BLOCK) for r, _, _, _ in jobs]
            krows = [pl.ds(pl.multiple_of(m * BLOCK, BLOCK), BLOCK) for _, m, _, _ in jobs]
            lg = [_sb_logits(q_ref[qr, :], k_ref[kr, :], scale, causal if dg else None, pad_row)
                  for qr, kr, (_, _, dg, pad_row) in zip(qrows, krows, jobs)]
            sm = [_sums(x[2], upper) for x in lg]
            a_all = []
            for qr, (_, _, dg, _), (_, log_beta, _), (inner, total) in zip(qrows, jobs, lg, sm):
                c = c_scr[qr, :]
                a = jnp.exp(log_beta + c + inner)
                a_all.append(jnp.where(causal, a, 0.0) if dg else a)
                c_scr[qr, :] = c + total
            out = [_dot(a.astype(BF16), v_ref[kr, :]) for a, kr in zip(a_all, krows)]
            for qr, o_part in zip(qrows, out):
                o_ref[qr, :] += o_part

        def pad_row_of(m):
            return jnp.where(jnp.logical_or(m > 0, lane >= N_PAD), 1.0, 0.0)

        base = SB_GROUP * g
        tiles([(r, base + kb, r == kb, pad_row_of(base) if kb == 0 else None)
               for kb in range(SB_GROUP - 1, -1, -1) for r in range(kb, SB_GROUP)])

        def live(least):
            def cond(carry):
                m, c_max = carry
                return jnp.logical_and(m >= least, c_max >= SB_DEAD)
            return cond

        def step_two(carry):
            m, _ = carry
            pad_row = pad_row_of(m - 1)
            tiles([(r, m, False, None) for r in range(SB_GROUP)]
                  + [(r, m - 1, False, pad_row) for r in range(SB_GROUP)])
            return m - 2, jnp.max(c_scr[...])

        def step_one(carry):
            m, _ = carry
            pad_row = pad_row_of(m)
            tiles([(r, m, False, pad_row) for r in range(SB_GROUP)])
            return m - 1, jnp.max(c_scr[...])

        carry = lax.while_loop(live(1), step_two, (base - 1, jnp.max(c_scr[...])))
        m_end, _ = lax.while_loop(live(0), step_one, carry)
        tot_ref[...] = c_scr[...]
        stop_ref[...] = jnp.broadcast_to((m_end + 1).astype(F32), stop_ref.shape)

    qblk = pl.BlockSpec((SB_ROWS, HEAD), lambda b, h, g: (b * n_groups + g, h))
    return _pcall(
        body, name=name, grid=(bsz, n_heads, n_groups),
        in_specs=[qblk, pl.BlockSpec((lp, HEAD), lambda b, h, g: (b, n_heads + h)),
                  pl.BlockSpec((lp, HEAD), lambda b, h, g: (b, 2 * n_heads + h))],
        out_specs=(qblk, qblk, pl.BlockSpec((None, 8, 128), lambda b, h, g: ((b * n_heads + h) * n_groups + g, 0, 0))),
        out_shape=(jax.ShapeDtypeStruct((t, d), F32), jax.ShapeDtypeStruct((t, d), F32),
                   jax.ShapeDtypeStruct((bsz * n_heads * n_groups, 8, 128), F32)),
        scratch_shapes=[pltpu.VMEM((SB_ROWS, HEAD), F32)],
        compiler_params=_params("parallel", "parallel", "arbitrary"),
    )(qkv, qkv, qkv)


def _sb_bwd(qkv, do, tot, stop, bsz, n_blocks, name):
    t, d3 = qkv.shape
    d = d3 // 3
    n_heads = d // HEAD
    lp = n_blocks * BLOCK
    n_groups = n_blocks // SB_GROUP
    scale = HEAD ** -0.5

    def body(q_ref, k_ref, v_ref, do_ref, tot_ref, stop_ref, dq_ref, dk_ref, dv_ref,
             dk_acc, dv_acc, dq_acc, p_scr, e_scr, dob_scr):
        g = pl.program_id(2)

        @pl.when(g == 0)
        def _():
            dk_acc[...] = jnp.zeros_like(dk_acc)
            dv_acc[...] = jnp.zeros_like(dv_acc)

        _, incl, excl = _sb_tables()
        causal = _iota2((BLOCK, BLOCK), 1) < _iota2((BLOCK, BLOCK), 0)
        lane = _iota2((1, BLOCK), 1)
        dob_scr[...] = do_ref[...].astype(BF16)
        dq_acc[...] = jnp.zeros_like(dq_acc)
        p_scr[...] = jnp.zeros_like(p_scr)
        e_scr[...] = jnp.zeros_like(e_scr)
        first = jnp.clip(jnp.max(stop_ref[...]).astype(jnp.int32), 0, SB_GROUP * g)

        def tiles(jobs):
            n_jobs = range(len(jobs))
            qrows = [pl.ds(r * BLOCK, BLOCK) for r, _, _, _ in jobs]
            krows = [pl.ds(pl.multiple_of(m * BLOCK, BLOCK), BLOCK) for _, m, _, _ in jobs]
            diag = [dg for _, _, dg, _ in jobs]
            lg = [_sb_logits(q_ref[qrows[i], :], k_ref[krows[i], :], scale, causal if diag[i] else None, jobs[i][3])
                  for i in n_jobs]
            d_a = [_dot_nt(dob_scr[qrows[i], :], v_ref[krows[i], :]) for i in n_jobs]
            sm = [_sums(lg[i][2], incl) for i in n_jobs]
            a_all = []
            for i in n_jobs:
                p = p_scr[qrows[i], :]
                a = jnp.exp(lg[i][1] + (tot_ref[qrows[i], :] - p - sm[i][0]))
                a_all.append(jnp.where(causal, a, 0.0) if diag[i] else a)
                p_scr[qrows[i], :] = p + sm[i][1]
            gr = [d_a[i] * a_all[i] for i in n_jobs]
            dv_part = [_dot_tn(a_all[i].astype(BF16), dob_scr[qrows[i], :]) for i in n_jobs]
            gs = [_sums(gr[i], excl) for i in n_jobs]
            dz_all = []
            for i in n_jobs:
                e = e_scr[qrows[i], :]
                dz = gr[i] - (gr[i] + e + gs[i][0]) * jnp.exp(lg[i][1])
                if diag[i]:
                    dz = jnp.where(causal, dz, 0.0)
                dz_all.append((dz * scale).astype(BF16))
                e_scr[qrows[i], :] = e + gs[i][1]
            dk_part = [_dot_tn(dz_all[i], q_ref[qrows[i], :]) for i in n_jobs]
            dq_part = [_dot(dz_all[i], k_ref[krows[i], :]) for i in n_jobs]
            for i in n_jobs:
                dv_acc[krows[i], :] += dv_part[i]
                dk_acc[krows[i], :] += dk_part[i]
                dq_acc[qrows[i], :] += dq_part[i]

        def pad_row_of(m):
            return jnp.where(jnp.logical_or(m > 0, lane >= N_PAD), 1.0, 0.0)

        def step_two(carry):
            m = carry
            pad_row = pad_row_of(m)
            tiles([(r, m, False, pad_row) for r in range(SB_GROUP)]
                  + [(r, m + 1, False, None) for r in range(SB_GROUP)])
            return m + 2

        def step_one(carry):
            m = carry
            pad_row = pad_row_of(m)
            tiles([(r, m, False, pad_row) for r in range(SB_GROUP)])
            return m + 1

        base = SB_GROUP * g
        m = lax.while_loop(lambda m: m + 1 < base, step_two, first)
        lax.while_loop(lambda m: m < base, step_one, m)
        tiles([(r, base + kb, r == kb, pad_row_of(base) if kb == 0 else None)
               for kb in range(SB_GROUP) for r in range(kb, SB_GROUP)])
        dq_ref[...] = dq_acc[...].astype(BF16)

        @pl.when(g == n_groups - 1)
        def _():
            dk_ref[...] = dk_acc[...].astype(BF16)
            dv_ref[...] = dv_acc[...].astype(BF16)

    qblk = pl.BlockSpec((SB_ROWS, HEAD), lambda b, h, g: (b * n_groups + g, h))
    kblk = pl.BlockSpec((lp, HEAD), lambda b, h, g: (b, n_heads + h))
    vblk = pl.BlockSpec((lp, HEAD), lambda b, h, g: (b, 2 * n_heads + h))
    hblk = pl.BlockSpec((lp, HEAD), lambda b, h, g: (b, h))
    sblk = pl.BlockSpec((None, 8, 128), lambda b, h, g: ((b * n_heads + h) * n_groups + g, 0, 0))
    return _pcall(
        body, name=name, grid=(bsz, n_heads, n_groups),
        in_specs=[qblk, kblk, vblk, qblk, qblk, sblk],
        out_specs=(qblk, hblk, hblk),
        out_shape=(jax.ShapeDtypeStruct((t, d), BF16),) * 3,
        scratch_shapes=[pltpu.VMEM((lp, HEAD), F32), pltpu.VMEM((lp, HEAD), F32)]
        + [pltpu.VMEM((SB_ROWS, HEAD), F32)] * 3 + [pltpu.VMEM((SB_ROWS, HEAD), BF16)],
        compiler_params=_params("parallel", "parallel", "arbitrary"),
    )(qkv, qkv, qkv, do, tot, stop)


def _adamw(w, g, m, v):
    m = ADAM_B1 * m + (1.0 - ADAM_B1) * g
    v = ADAM_B2 * v + (1.0 - ADAM_B2) * (g * g)
    m_hat = m / (1.0 - ADAM_B1 ** ADAM_STEP)
    v_hat = v / (1.0 - ADAM_B2 ** ADAM_STEP)
    delta = -ADAM_LR * (m_hat / (jnp.sqrt(v_hat) + ADAM_EPS) + ADAM_WD * w)
    return delta, m, v


def _update_sharded(w, parts, m, v, name):
    r, c = w.shape
    tr = min(r, ROW_TILE)

    def body(w_ref, p_ref, m_ref, v_ref, g_ref, d_ref, nm_ref, nv_ref):
        g = p_ref[0].astype(F32)
        for q in range(1, N_DEV):
            g = g + p_ref[q].astype(F32)
        g_ref[...] = g
        d_ref[...], nm_ref[...], nv_ref[...] = _adamw(w_ref[...], g, m_ref[...], v_ref[...])

    row = pl.BlockSpec((tr, c), lambda i: (i, 0))
    return _pcall(
        body, name=name, grid=(r // tr,),
        in_specs=[row, pl.BlockSpec((N_DEV, tr, c), lambda i: (0, i, 0)), row, row],
        out_specs=(row,) * 4, out_shape=(jax.ShapeDtypeStruct((r, c), F32),) * 4,
        compiler_params=_params("parallel"),
    )(w, parts, m, v)


SMALL_ROWS = 8


def _pack_small(dpre0, dpre1, dpost0, dpost1, dlb, dwon, loss, name):
    d = dpre0.shape[1]
    bsz = dlb.shape[0]

    def body(a0, a1, p0, p1, lb_ref, on_ref, loss_ref, out_ref):
        out_ref[...] = jnp.zeros_like(out_ref)
        out_ref[pl.ds(0, 1), :] = a0[...]
        out_ref[pl.ds(1, 1), :] = a1[...]
        out_ref[pl.ds(2, 1), :] = p0[...]
        out_ref[pl.ds(3, 1), :] = p1[...]
        acc = lb_ref[0]
        for b in range(1, bsz):
            acc = acc + lb_ref[b]
        out_ref[pl.ds(4, 1), :] = acc
        out_ref[pl.ds(5, 1), pl.ds(0, HEAD)] = on_ref[...]
        out_ref[pl.ds(6, 1), pl.ds(0, HEAD)] = loss_ref[pl.ds(0, 1), :]

    return _pcall(body, name=name, out_shape=jax.ShapeDtypeStruct((SMALL_ROWS, d), F32))(
        dpre0, dpre1, dpost0, dpost1, dlb, dwon, loss)


def _update_small(parts, pre, post, lbw, on, moments, name):
    d = pre.shape[1]

    def body(p_ref, pre_ref, post_ref, lbw_ref, on_ref, mpre, mpost, mlb, mon, vpre, vpost, vlb, von,
             loss_ref, *outs):
        def total(r0, nr, width):
            acc = p_ref[0, pl.ds(r0, nr), pl.ds(0, width)]
            for q in range(1, N_DEV):
                acc = acc + p_ref[q, pl.ds(r0, nr), pl.ds(0, width)]
            return acc

        def put(k, w, g, m, v):
            dl, nm, nv = _adamw(w, g, m, v)
            outs[4 * k][...] = g
            outs[4 * k + 1][...] = dl
            outs[4 * k + 2][...] = nm
            outs[4 * k + 3][...] = nv

        put(0, pre_ref[...], total(0, 2, d), mpre[...], vpre[...])
        put(1, post_ref[...], total(2, 2, d), mpost[...], vpost[...])
        a0, a1 = lbw_ref[pl.ds(0, 1), :], lbw_ref[pl.ds(1, 1), :]
        mx = jnp.maximum(a0, a1)
        e0, e1 = jnp.exp(a0 - mx), jnp.exp(a1 - mx)
        p0 = e0 / (e0 + e1)
        g0 = total(4, 1, d) * p0 * (1.0 - p0)
        for r, w, g in ((0, a0, g0), (1, a1, -g0)):
            row = pl.ds(r, 1)
            dl, nm, nv = _adamw(w, g, mlb[row, :], vlb[row, :])
            outs[8][row, :] = g
            outs[9][row, :] = dl
            outs[10][row, :] = nm
            outs[11][row, :] = nv
        put(3, on_ref[...], total(5, 1, HEAD), mon[...], von[...])
        loss_ref[...] = jnp.broadcast_to(total(6, 1, HEAD), loss_ref.shape)

    shapes = []
    for w in (pre, post, lbw, on):
        shapes += [jax.ShapeDtypeStruct(w.shape, F32)] * 4
    return _pcall(body, name=name, out_shape=(jax.ShapeDtypeStruct((8, 128), F32), *shapes))(
        parts, pre, post, lbw, on, *moments)


def kernel(x, meta_tokens, pre_norm, post_norm, hgrn_w_in, hgrn_lb, hgrn_out_norm, hgrn_w_out, sb_w_in, sb_w_out, loss_target, m_meta_tokens, m_pre_norm, m_post_norm, m_hgrn_w_in, m_hgrn_lb, m_hgrn_out_norm, m_hgrn_w_out, m_sb_w_in, m_sb_w_out, v_meta_tokens, v_pre_norm, v_post_norm, v_hgrn_w_in, v_hgrn_lb, v_hgrn_out_norm, v_hgrn_w_out, v_sb_w_in, v_sb_w_out):
    bsz, seq, d = x.shape
    n_blocks = seq // BLOCK + 1
    lp = n_blocks * BLOCK
    t = bsz * lp
    s = hgrn_w_in.shape[2]
    dsh = d // N_DEV

    w_in_h, w_out_h, meta_all = _exchange(
        [hgrn_w_in[0].astype(BF16), hgrn_w_out[0].astype(BF16), meta_tokens], ["gather"] * 3, "gather_weights")
    w_out_h = w_out_h.reshape(d, d)
    meta_full = jnp.transpose(meta_all, (1, 0, 2)).reshape(N_META, d)

    h0 = jnp.concatenate(
        [jnp.zeros((bsz, N_PAD, d), F32), jnp.broadcast_to(meta_full[None], (bsz, N_META, d)), x], axis=1
    ).reshape(t, d)
    lbrow = jnp.cumsum(jax.nn.softmax(hgrn_lb, axis=0), axis=0)[0:1]

    main0, gate0, yn0 = _norm_inproj(h0, pre_norm[0:1], w_in_h, F32, "inproj_hgrn")
    o0, states, w_in_s, w_out_s = _hgrn_fwd(
        main0, lbrow, bsz, n_blocks, [sb_w_in[0].astype(BF16), sb_w_out[0].astype(BF16)], ["gather"] * 2, "hgrn_fwd")
    w_out_s = w_out_s.reshape(d, d)
    h1, y0 = _mix_out(o0, gate0, h0, w_out_h, hgrn_out_norm, post_norm[0:1], True, "mix_out_hgrn")
    main1, gate1, yn1 = _norm_inproj(h1, pre_norm[1:2], w_in_s, BF16, "inproj_sb")
    o1, tot, stop = _sb_fwd(main1, bsz, n_blocks, "sb_fwd")
    h2, y1 = _mix_out(o1, gate1, h1, w_out_s, hgrn_out_norm, post_norm[1:2], False, "mix_out_sb")
    dh2, loss_part = _loss_grad(h2, loss_target, n_blocks, "loss_grad")

    do1, dgate1, dw_out_s, dpost1, _ = _mix_out_bwd(
        dh2, y1, o1, gate1, w_out_s, hgrn_out_norm, post_norm[1:2], False, "mix_out_sb_bwd")
    dq1, dk1, dv1 = _sb_bwd(main1, do1, tot, stop, bsz, n_blocks, "sb_bwd")
    dproj1 = (dq1, dk1, dv1, dgate1)
    dh1, dpre1 = _inproj_bwd_x(dproj1, w_in_s, h1, pre_norm[1:2], dh2, "inproj_sb_bwd_x")
    dw_in_s = _inproj_bwd_w(yn1, dproj1, s, "inproj_sb_bwd_w")

    do0, dgate0, dw_out_h, dpost0, dwon = _mix_out_bwd(
        dh1, y0, o0, gate0, w_out_h, hgrn_out_norm, post_norm[0:1], True, "mix_out_hgrn_bwd")
    dq0, dfz0, dv0, dlb, p_in_s, p_out_s = _hgrn_bwd(
        main0, lbrow, states, do0, bsz, n_blocks, [dw_in_s, dw_out_s.reshape(N_DEV, dsh, d)], ["scatter"] * 2,
        "hgrn_bwd")
    dproj0 = (dq0, dfz0, dv0, dgate0)
    dh0, dpre0 = _inproj_bwd_x(dproj0, w_in_h, h0, pre_norm[0:1], dh1, "inproj_hgrn_bwd_x")
    dw_in_h = _inproj_bwd_w(yn0, dproj0, s, "inproj_hgrn_bwd_w")

    dh0 = dh0.reshape(bsz, lp, d)
    grad_x = dh0[:, BLOCK:]
    dmeta = jnp.sum(dh0[:, N_PAD:BLOCK], axis=0)
    dmeta = jnp.transpose(dmeta.reshape(N_META, N_DEV, dsh), (1, 0, 2))
    small = _pack_small(dpre0, dpre1, dpost0, dpost1, dlb, dwon, loss_part, "pack_small")

    p_in_h, p_out_h, p_meta, p_small = _exchange(
        [dw_in_h, dw_out_h.reshape(N_DEV, dsh, d), dmeta, small], ["scatter"] * 3 + ["gather"], "exchange_grads")

    u_meta = _update_sharded(meta_tokens, p_meta, m_meta_tokens, v_meta_tokens, "update_meta")
    u_in_h = _update_sharded(hgrn_w_in[0], p_in_h, m_hgrn_w_in[0], v_hgrn_w_in[0], "update_hgrn_w_in")
    u_out_h = _update_sharded(hgrn_w_out[0], p_out_h, m_hgrn_w_out[0], v_hgrn_w_out[0], "update_hgrn_w_out")
    u_in_s = _update_sharded(sb_w_in[0], p_in_s, m_sb_w_in[0], v_sb_w_in[0], "update_sb_w_in")
    u_out_s = _update_sharded(sb_w_out[0], p_out_s, m_sb_w_out[0], v_sb_w_out[0], "update_sb_w_out")
    sm = _update_small(p_small, pre_norm, post_norm, hgrn_lb, hgrn_out_norm,
                       (m_pre_norm, m_post_norm, m_hgrn_lb, m_hgrn_out_norm,
                        v_pre_norm, v_post_norm, v_hgrn_lb, v_hgrn_out_norm), "update_small")
    loss = sm[0][0, 0]
    u_pre, u_post, u_lb, u_on = sm[1:5], sm[5:9], sm[9:13], sm[13:17]

    per_w = [u_meta, u_pre, u_post, tuple(a[None] for a in u_in_h), u_lb, u_on,
             tuple(a[None] for a in u_out_h), tuple(a[None] for a in u_in_s), tuple(a[None] for a in u_out_s)]
    outs = [loss, grad_x]
    for k in range(4):
        outs += [u[k] for u in per_w]
    return tuple(outs)
```

```python
import jax
import jax.numpy as jnp
from jax import lax
from jax.experimental import pallas as pl
from jax.experimental.pallas import tpu as pltpu

F32 = jnp.float32
BF16 = jnp.bfloat16

N_DEV = 8
BLOCK = 128
N_META = 16
N_PAD = BLOCK - N_META
HEAD = 128
SUB = 16
N_SUB = BLOCK // SUB
HALF = 8
EPS = 1e-6
ROW_TILE = 256
K_TILE = 768
NEG_BIG = -1e30

ADAM_LR = 0.001
ADAM_B1 = 0.9
ADAM_B2 = 0.999
ADAM_EPS = 1e-08
ADAM_WD = 0.01
ADAM_STEP = 10

VMEM_LIMIT = 56 * 1024 * 1024


def _pcall(body, **kw):
    return pl.pallas_call(body, **kw)


def _params(*sem):
    return pltpu.CompilerParams(dimension_semantics=sem, vmem_limit_bytes=VMEM_LIMIT)


def _dot(a, b):
    return jnp.dot(a, b, preferred_element_type=F32)


def _dot_nt(a, b):
    return lax.dot_general(a, b, (((1,), (1,)), ((), ())), preferred_element_type=F32)


def _dot_tn(a, b):
    return lax.dot_general(a, b, (((0,), (0,)), ((), ())), preferred_element_type=F32)


def _split(x, pieces):
    out = []
    for _ in range(pieces):
        p = x.astype(BF16)
        out.append(p)
        x = x - p.astype(F32)
    return out


def _tri_right(x, tri, pieces=2):
    return sum(_dot(p, tri) for p in _split(x, pieces))


def _tri_left(tri, x, pieces=3):
    return sum(_dot(tri, p) for p in _split(x, pieces))


def _iota2(shape, dim):
    return lax.broadcasted_iota(jnp.int32, shape, dim)


def _tri(cond):
    return jnp.where(cond, 1.0, 0.0).astype(BF16)


def _sig_pair(x):
    e = jnp.exp(-jnp.abs(x))
    r = 1.0 / (1.0 + e)
    er = e * r
    pos = x >= 0
    return jnp.where(pos, r, er), jnp.where(pos, er, r)


def _expand(rows):
    return jnp.concatenate([jnp.broadcast_to(r, (SUB, HEAD)) for r in rows], axis=0)


def _exchange_shapes(arrays, modes):
    return tuple(jax.ShapeDtypeStruct((N_DEV,) + tuple(a.shape[1:] if m == "scatter" else a.shape), a.dtype)
                 for a, m in zip(arrays, modes))


def _exchange_sems(n):
    if n == 0:
        return []
    return [pltpu.SemaphoreType.DMA((n, N_DEV - 1)), pltpu.SemaphoreType.DMA((n, N_DEV - 1)),
            pltpu.SemaphoreType.DMA((n,))]


def _exchange_copies(ins, outs, modes, send_sems, recv_sems, local_sems):
    mx, my, mc = lax.axis_index("x"), lax.axis_index("y"), lax.axis_index("c")
    me = 4 * mx + 2 * my + mc

    def src(i, slot):
        return ins[i].at[slot] if modes[i] == "scatter" else ins[i]

    def peer_of(mask):
        px = 1 - mx if mask & 4 else mx
        py = 1 - my if mask & 2 else my
        pc = 1 - mc if mask & 1 else mc
        return px, py, pc

    def copy(i, mask, dst_slot):
        px, py, pc = peer_of(mask)
        return pltpu.make_async_remote_copy(
            src_ref=src(i, 4 * px + 2 * py + pc), dst_ref=outs[i].at[dst_slot],
            send_sem=send_sems.at[i, mask - 1], recv_sem=recv_sems.at[i, mask - 1],
            device_id=(px, py, pc), device_id_type=pl.DeviceIdType.MESH)

    n = len(ins)
    sends = [copy(i, mask, me) for mask in range(1, N_DEV) for i in range(n)]
    own = [pltpu.make_async_copy(src(i, me), outs[i].at[me], local_sems.at[i]) for i in range(n)]
    arrivals = []
    for mask in range(1, N_DEV):
        px, py, pc = peer_of(mask)
        arrivals += [copy(i, mask, 4 * px + 2 * py + pc) for i in range(n)]

    def start():
        for cp in sends + own:
            cp.start()

    def wait():
        for cp in arrivals:
            cp.wait_recv()
        for cp in sends:
            cp.wait_send()
        for cp in own:
            cp.wait()

    return start, wait


def _exchange(arrays, modes, name):
    n = len(arrays)

    def body(*refs):
        start, wait = _exchange_copies(refs[:n], refs[n:2 * n], modes, *refs[2 * n:])
        start()
        wait()

    any_spec = pl.BlockSpec(memory_space=pl.ANY)
    return _pcall(
        body, name=name, out_shape=_exchange_shapes(arrays, modes),
        in_specs=[any_spec] * n, out_specs=tuple([any_spec] * n),
        scratch_shapes=_exchange_sems(n),
    )(*arrays)


def _norm_inproj(h, wnorm, w_all, main_dtype, name):
    t, d = h.shape
    p_n, _, s = w_all.shape
    n_main = 3 * d // s
    tm = ROW_TILE

    def body(h_ref, wn_ref, w_ref, main_ref, gate_ref, ynt_ref):
        x = h_ref[...]
        y = x * lax.rsqrt(jnp.mean(x * x, axis=-1, keepdims=True) + EPS) * wn_ref[...]
        yb = y.astype(BF16)
        ynt_ref[...] = y.T.astype(BF16)
        for p in range(p_n):
            r = _dot(yb, w_ref[p])
            if p < n_main:
                main_ref[:, p * s:(p + 1) * s] = r.astype(main_dtype)
            else:
                gate_ref[:, (p - n_main) * s:(p - n_main + 1) * s] = r

    return _pcall(
        body, name=name, grid=(t // tm,),
        in_specs=[pl.BlockSpec((tm, d), lambda i: (i, 0)), pl.BlockSpec((1, d), lambda i: (0, 0)),
                  pl.BlockSpec((p_n, d, s), lambda i: (0, 0, 0))],
        out_specs=(pl.BlockSpec((tm, 3 * d), lambda i: (i, 0)), pl.BlockSpec((tm, d), lambda i: (i, 0)),
                   pl.BlockSpec((d, tm), lambda i: (0, i))),
        out_shape=(jax.ShapeDtypeStruct((t, 3 * d), main_dtype), jax.ShapeDtypeStruct((t, d), F32),
                   jax.ShapeDtypeStruct((d, t), BF16)),
        compiler_params=_params("parallel"),
    )(h, wnorm, w_all)


def _mix_out(o, gate, h_in, w_out, w_on, w_post, head_norm, name):
    t, d = o.shape
    n_heads = d // HEAD
    tm = ROW_TILE

    def body(o_ref, g_ref, h_ref, w_ref, won_ref, wp_ref, hout_ref, y_ref, u_scr):
        for hh in range(n_heads):
            cs = slice(hh * HEAD, (hh + 1) * HEAD)
            oh = o_ref[:, cs]
            gt = g_ref[:, cs]
            if head_norm:
                oh = oh * lax.rsqrt(jnp.mean(oh * oh, axis=-1, keepdims=True) + EPS) * won_ref[...]
            u_scr[:, cs] = (oh * (gt * jax.nn.sigmoid(gt))).astype(BF16)
        y = _dot(u_scr[...], w_ref[...])
        y_ref[...] = y
        r = y * lax.rsqrt(jnp.mean(y * y, axis=-1, keepdims=True) + EPS) * wp_ref[...]
        hout_ref[...] = h_ref[...] + r

    row = pl.BlockSpec((tm, d), lambda i: (i, 0))
    return _pcall(
        body, name=name, grid=(t // tm,),
        in_specs=[row, row, row, pl.BlockSpec((d, d), lambda i: (0, 0)),
                  pl.BlockSpec((1, HEAD), lambda i: (0, 0)), pl.BlockSpec((1, d), lambda i: (0, 0))],
        out_specs=(row, row),
        out_shape=(jax.ShapeDtypeStruct((t, d), F32), jax.ShapeDtypeStruct((t, d), F32)),
        scratch_shapes=[pltpu.VMEM((tm, d), BF16)],
        compiler_params=_params("parallel"),
    )(o, gate, h_in, w_out, w_on, w_post)


def _loss_grad(h_out, target, n_blocks, name):
    t, d = h_out.shape

    def body(h_ref, t_ref, dh_ref, loss_ref):
        i = pl.program_id(0)

        @pl.when(i == 0)
        def _():
            loss_ref[...] = jnp.zeros_like(loss_ref)

        real = (i % n_blocks) > 0
        err = jnp.where(real, h_ref[...] - t_ref[...], 0.0)
        dh_ref[...] = err * (1.0 / d)
        part = jnp.sum(jnp.sum(err * err, axis=-1, keepdims=True), axis=0, keepdims=True)
        loss_ref[...] += part * (0.5 / d)

    return _pcall(
        body, name=name, grid=(t // BLOCK,),
        in_specs=[pl.BlockSpec((BLOCK, d), lambda i: (i, 0)),
                  pl.BlockSpec((None, BLOCK, d), lambda i: (i // n_blocks, jnp.maximum(i % n_blocks - 1, 0), 0))],
        out_specs=(pl.BlockSpec((BLOCK, d), lambda i: (i, 0)), pl.BlockSpec((8, 128), lambda i: (0, 0))),
        out_shape=(jax.ShapeDtypeStruct((t, d), F32), jax.ShapeDtypeStruct((8, 128), F32)),
        compiler_params=_params("arbitrary"),
    )(h_out, target)


def _mix_out_bwd(dh, y, o, gate, w_out, w_on, w_post, head_norm, name):
    t, d = o.shape
    n_heads = d // HEAD
    tm = ROW_TILE
    last = t // tm - 1

    def body(dh_ref, y_ref, o_ref, g_ref, w_ref, won_ref, wp_ref,
             do_ref, dg_ref, dw_ref, dwp_ref, dwon_ref, u_scr, acc):
        i = pl.program_id(0)

        @pl.when(i == 0)
        def _():
            acc[...] = jnp.zeros_like(acc)
            dwp_ref[...] = jnp.zeros_like(dwp_ref)
            dwon_ref[...] = jnp.zeros_like(dwon_ref)

        yv = y_ref[...]
        rs = lax.rsqrt(jnp.mean(yv * yv, axis=-1, keepdims=True) + EPS)
        yh = yv * rs
        dr = dh_ref[...]
        dwp_ref[...] += jnp.sum(dr * yh, axis=0, keepdims=True)
        wd = dr * wp_ref[...]
        dy = rs * (wd - yh * jnp.mean(wd * yh, axis=-1, keepdims=True))
        dyb = dy.astype(BF16)
        du = _dot_nt(dyb, w_ref[...])
        for hh in range(n_heads):
            cs = slice(hh * HEAD, (hh + 1) * HEAD)
            oh = o_ref[:, cs]
            gt = g_ref[:, cs]
            sg = jax.nn.sigmoid(gt)
            sl = gt * sg
            duh = du[:, cs]
            if head_norm:
                rsh = lax.rsqrt(jnp.mean(oh * oh, axis=-1, keepdims=True) + EPS)
                ohat = oh * rsh
                on = ohat * won_ref[...]
            else:
                on = oh
            u_scr[:, cs] = (on * sl).astype(BF16)
            don = duh * sl
            dg_ref[:, cs] = (duh * on * (sg * (1.0 + gt * (1.0 - sg)))).astype(BF16)
            if head_norm:
                dwon_ref[...] += jnp.sum(don * ohat, axis=0, keepdims=True)
                wdn = don * won_ref[...]
                do_ref[:, cs] = rsh * (wdn - ohat * jnp.mean(wdn * ohat, axis=-1, keepdims=True))
            else:
                do_ref[:, cs] = don
        acc[...] += _dot_tn(u_scr[...], dyb)

        @pl.when(i == last)
        def _():
            dw_ref[...] = acc[...].astype(BF16)

    row = pl.BlockSpec((tm, d), lambda i: (i, 0))
    return _pcall(
        body, name=name, grid=(t // tm,),
        in_specs=[row, row, row, row, pl.BlockSpec((d, d), lambda i: (0, 0)),
                  pl.BlockSpec((1, HEAD), lambda i: (0, 0)), pl.BlockSpec((1, d), lambda i: (0, 0))],
        out_specs=(row, row, pl.BlockSpec((d, d), lambda i: (0, 0)), pl.BlockSpec((1, d), lambda i: (0, 0)),
                   pl.BlockSpec((1, HEAD), lambda i: (0, 0))),
        out_shape=(jax.ShapeDtypeStruct((t, d), F32), jax.ShapeDtypeStruct((t, d), BF16),
                   jax.ShapeDtypeStruct((d, d), BF16), jax.ShapeDtypeStruct((1, d), F32),
                   jax.ShapeDtypeStruct((1, HEAD), F32)),
        scratch_shapes=[pltpu.VMEM((tm, d), BF16), pltpu.VMEM((d, d), F32)],
        compiler_params=_params("arbitrary"),
    )(dh, y, o, gate, w_out, w_on, w_post)


def _inproj_bwd_x(dparts, w_all, h, wnorm, dres, name):
    t, d = h.shape
    p_n, _, s = w_all.shape
    per = d // s
    tm = ROW_TILE

    def body(d0_ref, d1_ref, d2_ref, d3_ref, w_ref, h_ref, wn_ref, dres_ref, dh_ref, dwn_ref):
        i = pl.program_id(0)

        @pl.when(i == 0)
        def _():
            dwn_ref[...] = jnp.zeros_like(dwn_ref)

        pieces = (d0_ref, d1_ref, d2_ref, d3_ref)
        dyn = jnp.zeros((tm, d), F32)
        for p in range(p_n):
            blk = pieces[p // per][:, (p % per) * s:(p % per + 1) * s]
            dyn = dyn + _dot_nt(blk, w_ref[p])
        x = h_ref[...]
        rs = lax.rsqrt(jnp.mean(x * x, axis=-1, keepdims=True) + EPS)
        xh = x * rs
        dwn_ref[...] += jnp.sum(dyn * xh, axis=0, keepdims=True)
        wd = dyn * wn_ref[...]
        dh_ref[...] = dres_ref[...] + rs * (wd - xh * jnp.mean(wd * xh, axis=-1, keepdims=True))

    row = pl.BlockSpec((tm, d), lambda i: (i, 0))
    return _pcall(
        body, name=name, grid=(t // tm,),
        in_specs=[row, row, row, row, pl.BlockSpec((p_n, d, s), lambda i: (0, 0, 0)),
                  row, pl.BlockSpec((1, d), lambda i: (0, 0)), row],
        out_specs=(row, pl.BlockSpec((1, d), lambda i: (0, 0))),
        out_shape=(jax.ShapeDtypeStruct((t, d), F32), jax.ShapeDtypeStruct((1, d), F32)),
        compiler_params=_params("arbitrary"),
    )(*dparts, w_all, h, wnorm, dres)


def _inproj_bwd_w(ynt, dparts, s, name):
    d, t = ynt.shape
    per = d // s
    n_sh = len(dparts) * per
    tk = K_TILE
    last = t // tk - 1

    def body(ynt_ref, d0_ref, d1_ref, d2_ref, d3_ref, dw_ref, acc):
        p, i = pl.program_id(0), pl.program_id(1)

        @pl.when(i == 0)
        def _():
            acc[...] = jnp.zeros_like(acc)

        for a, piece in enumerate((d0_ref, d1_ref, d2_ref, d3_ref)):
            @pl.when(p // per == a)
            def _():
                acc[...] += _dot(ynt_ref[...], piece[...])

        @pl.when(i == last)
        def _():
            dw_ref[...] = acc[...].astype(BF16)

    def piece_spec(a):
        return pl.BlockSpec((tk, s), lambda p, i: (jnp.where(p // per == a, i, 0),
                                                   jnp.where(p // per == a, p % per, 0)))

    return _pcall(
        body, name=name, grid=(n_sh, t // tk),
        in_specs=[pl.BlockSpec((d, tk), lambda p, i: (0, i))] + [piece_spec(a) for a in range(4)],
        out_specs=pl.BlockSpec((None, d, s), lambda p, i: (p, 0, 0)),
        out_shape=jax.ShapeDtypeStruct((n_sh, d, s), BF16),
        scratch_shapes=[pltpu.VMEM((d, s), F32)],
        compiler_params=_params("parallel", "arbitrary"),
    )(ynt, *dparts)


def _hgrn_common(q_ref, fz_ref, lb, b_scr, k_scr):
    fz = fz_ref[...]
    sig, nsig = _sig_pair(fz)
    f = lb + (1.0 - lb) * sig
    k = (1.0 - lb) * nsig
    rr, cc = _iota2((BLOCK, BLOCK), 0), _iota2((BLOCK, BLOCK), 1)
    b = _tri_left(_tri(cc <= rr), jnp.log(f))
    b_scr[...] = b
    k_scr[...] = k
    bend = [b_scr[pl.ds(SUB * j + SUB - 1, 1), :] for j in range(N_SUB)]
    bref = [jnp.zeros((1, HEAD), F32)] + bend[:-1]
    refrow, bendrow = _expand(bref), _expand(bend)
    e_q = jnp.exp(b - refrow)
    e_k = jnp.exp(bendrow - b)
    qt = q_ref[...] * e_q
    kh = k * e_k
    bl = bend[-1]
    return dict(sig=sig, nsig=nsig, f=f, k=k, b=b, bend=bend, bref=bref, refrow=refrow,
                e_q=e_q, e_k=e_k, qt=qt, kh=kh, bl=bl)


HEADS_PER_STEP = 2


def _lockstep(chunks):
    live = list(chunks)
    while live:
        still = []
        for gen in live:
            try:
                next(gen)
                still.append(gen)
            except StopIteration:
                pass
        live = still


def _ride_along(ride_ins, ride_outs, modes, sems, grid):
    if not ride_ins:
        return lambda: None
    ids = [pl.program_id(a) for a in range(len(grid))]
    start, wait = _exchange_copies(ride_ins, ride_outs, modes, *sems)
    first, last = ids[0] == 0, ids[0] == grid[0] - 1
    for a in range(1, len(grid)):
        first = jnp.logical_and(first, ids[a] == 0)
        last = jnp.logical_and(last, ids[a] == grid[a] - 1)
    pl.when(first)(start)
    return lambda: pl.when(last)(wait)


def _hgrn_fwd(main, lbrow, bsz, n_blocks, ride, ride_modes, name):
    t, d3 = main.shape
    d = d3 // 3
    n_heads = d // HEAD
    n_pairs = n_heads // HEADS_PER_STEP
    wide = HEADS_PER_STEP * HEAD
    nr = len(ride)
    grid = (bsz, n_pairs, n_blocks)

    def chunk(q_ref, fz_ref, v_ref, lb, o_ref, st_ref, s_scr, b_scr, k_scr, o_acc):
        c = _hgrn_common(q_ref, fz_ref, lb, b_scr, k_scr)
        yield
        s_t = s_scr[...]
        st_ref[...] = s_t
        vb = v_ref[...].astype(BF16)
        q_state = c["qt"] * _expand([jnp.exp(r) for r in c["bref"]])
        o_state = _dot_nt(q_state.astype(BF16), s_t.astype(BF16))
        js = range(N_SUB - 1)
        khb = c["kh"].astype(BF16)
        lhs = [(c["qt"][SUB * (j + 1):] * jnp.exp(c["refrow"][SUB * (j + 1):] - c["bend"][j])).astype(BF16)
               for j in js]
        yield
        a_js = [_dot_nt(lhs[j], khb[SUB * j:SUB * (j + 1)]) for j in js]
        k_state = c["kh"] * _expand([jnp.exp(c["bl"] - r) for r in c["bend"]])
        s_scr[...] = s_t * jnp.exp(c["bl"]) + _dot_tn(vb, k_state.astype(BF16))
        yield
        o_js = [_dot(a_js[j].astype(BF16), vb[SUB * j:SUB * (j + 1)]) for j in js]
        yield
        o_acc[...] = o_state
        for j in js:
            o_acc[SUB * (j + 1):, :] += o_js[j]
        t_loc = _iota2((HALF, HEAD), 0)
        for i in range(N_SUB):
            r0 = SUB * i
            q_h = [q_ref[pl.ds(r0 + HALF * u, HALF), :] for u in range(2)]
            b_h = [b_scr[pl.ds(r0 + HALF * u, HALF), :] for u in range(2)]
            o_h = [jnp.zeros((HALF, HEAD), F32) for _ in range(2)]
            for s in range(SUB):
                brow = b_scr[pl.ds(r0 + s, 1), :]
                krow = k_scr[pl.ds(r0 + s, 1), :]
                vrow = v_ref[pl.ds(r0 + s, 1), :]
                for u in range(s // HALF, 2):
                    diff = b_h[u] - brow
                    if u == s // HALF:
                        diff = jnp.where(t_loc >= s - HALF * u, diff, NEG_BIG)
                    col = jnp.sum(q_h[u] * krow * jnp.exp(diff), axis=-1, keepdims=True)
                    o_h[u] = o_h[u] + col * vrow
            for u in range(2):
                o_acc[pl.ds(r0 + HALF * u, HALF), :] += o_h[u]
            yield
        o_ref[...] = o_acc[...]

    def body(*refs):
        q_ref, fz_ref, v_ref, lb_ref = refs[:4]
        o_ref, st_ref = refs[4 + nr:6 + nr]
        s_scr, b_scr, k_scr, o_acc = refs[6 + 2 * nr:10 + 2 * nr]
        finish_ride = _ride_along(refs[4:4 + nr], refs[6 + nr:6 + 2 * nr], ride_modes, refs[10 + 2 * nr:], grid)

        @pl.when(pl.program_id(2) == 0)
        def _():
            s_scr[...] = jnp.zeros_like(s_scr)

        def head(hh):
            cols = pl.ds(hh * HEAD, HEAD)
            return chunk(q_ref.at[:, cols], fz_ref.at[:, cols], v_ref.at[:, cols], lb_ref[:, cols],
                         o_ref.at[:, cols], st_ref.at[hh], s_scr.at[hh], b_scr.at[hh], k_scr.at[hh], o_acc.at[hh])

        _lockstep([head(hh) for hh in range(HEADS_PER_STEP)])
        finish_ride()

    def blk(col0):
        return pl.BlockSpec((BLOCK, wide), lambda b, h, n: (b * n_blocks + n, col0 + h))

    any_spec = pl.BlockSpec(memory_space=pl.ANY)
    per_head = pltpu.VMEM((HEADS_PER_STEP, BLOCK, HEAD), F32)
    return _pcall(
        body, name=name, grid=grid,
        in_specs=[blk(0), blk(n_pairs), blk(2 * n_pairs), pl.BlockSpec((1, wide), lambda b, h, n: (0, h))]
        + [any_spec] * nr,
        out_specs=(blk(0), pl.BlockSpec((None, HEADS_PER_STEP, HEAD, HEAD), lambda b, h, n: (b * n_blocks + n, h, 0, 0)))
        + (any_spec,) * nr,
        out_shape=(jax.ShapeDtypeStruct((t, d), F32),
                   jax.ShapeDtypeStruct((bsz * n_blocks, n_heads, HEAD, HEAD), F32))
        + _exchange_shapes(ride, ride_modes),
        scratch_shapes=[per_head] * 4 + _exchange_sems(nr),
        compiler_params=_params("arbitrary", "arbitrary", "arbitrary"),
    )(main, main, main, lbrow, *ride)


def _hgrn_bwd(main, lbrow, states, do, bsz, n_blocks, ride, ride_modes, name):
    t, d3 = main.shape
    d = d3 // 3
    n_pairs = d // HEAD // HEADS_PER_STEP
    wide = HEADS_PER_STEP * HEAD
    nr = len(ride)
    grid = (bsz, n_pairs, n_blocks)

    def chunk(n, q_ref, fz_ref, v_ref, lb, st_ref, do_ref, dq_ref, dfz_ref, dv_ref, dlb_ref,
              ds_scr, b_scr, k_scr, dqt_acc, dkh_acc, dv_acc, dqd_acc, dkd_acc, ad_scr):
        c = _hgrn_common(q_ref, fz_ref, lb, b_scr, k_scr)
        yield
        q, k = q_ref[...], c["k"]
        vb = v_ref[...].astype(BF16)
        dob = do_ref[...].astype(BF16)
        s0_t = st_ref[...]
        ds1_t = ds_scr[...]
        e_ref = _expand([jnp.exp(r) for r in c["bref"]])
        e_end = _expand([jnp.exp(c["bl"] - r) for r in c["bend"]])
        e_bl = jnp.exp(c["bl"])
        q_state = c["qt"] * e_ref
        k_state = c["kh"] * e_end
        dq_state = _dot(dob, s0_t.astype(BF16))
        dk_state = _dot(vb, ds1_t.astype(BF16))
        dv_state = _dot_nt(k_state.astype(BF16), ds1_t.astype(BF16))
        ds_scr[...] = ds1_t * e_bl + _dot_tn(dob, q_state.astype(BF16))
        js = range(N_SUB - 1)
        lo = [slice(SUB * j, SUB * (j + 1)) for j in js]
        khb = c["kh"].astype(BF16)
        dj = [jnp.exp(c["refrow"][SUB * (j + 1):] - c["bend"][j]) for j in js]
        lhs = [(c["qt"][SUB * (j + 1):] * dj[j]).astype(BF16) for j in js]
        yield
        a_js = [_dot_nt(lhs[j], khb[lo[j]]) for j in js]
        da_js = [_dot_nt(dob[SUB * (j + 1):], vb[lo[j]]).astype(BF16) for j in js]
        dqt_acc[...] = dq_state * e_ref
        dkh_acc[...] = dk_state * e_end
        dv_acc[...] = dv_state
        dbl = (jnp.sum(s0_t * ds1_t, axis=0, keepdims=True) * e_bl
               + jnp.sum(k_state * dk_state, axis=0, keepdims=True))
        yield
        dv_js = [_dot_tn(a_js[j].astype(BF16), dob[SUB * (j + 1):]) for j in js]
        dq_js = [_dot(da_js[j], khb[lo[j]]) * dj[j] for j in js]
        dk_js = [_dot_tn(da_js[j], lhs[j]) for j in js]
        yield
        for j in js:
            dv_acc[lo[j], :] += dv_js[j]
            dqt_acc[SUB * (j + 1):, :] += dq_js[j]
            dkh_acc[lo[j], :] += dk_js[j]
        t_loc = _iota2((HALF, HEAD), 0)
        lane = _iota2((HALF, HEAD), 1)
        for i in range(N_SUB):
            r0 = SUB * i
            q_h = [q_ref[pl.ds(r0 + HALF * u, HALF), :] for u in range(2)]
            b_h = [b_scr[pl.ds(r0 + HALF * u, HALF), :] for u in range(2)]
            do_h = [do_ref[pl.ds(r0 + HALF * u, HALF), :] for u in range(2)]
            zero = jnp.zeros((HALF, HEAD), F32)
            dq_h, dk_h, a_h = [zero, zero], [zero, zero], [zero, zero]
            for s in range(SUB):
                brow = b_scr[pl.ds(r0 + s, 1), :]
                krow = k_scr[pl.ds(r0 + s, 1), :]
                vrow = v_ref[pl.ds(r0 + s, 1), :]
                dk_row = jnp.zeros((1, HEAD), F32)
                for u in range(s // HALF, 2):
                    diff = b_h[u] - brow
                    if u == s // HALF:
                        diff = jnp.where(t_loc >= s - HALF * u, diff, NEG_BIG)
                    w = jnp.exp(diff)
                    qw = q_h[u] * w
                    a_col = jnp.sum(qw * krow, axis=-1, keepdims=True)
                    da_col = jnp.sum(do_h[u] * vrow, axis=-1, keepdims=True)
                    a_h[u] = jnp.where(lane == r0 + s, a_col, a_h[u])
                    dq_h[u] = dq_h[u] + da_col * (w * krow)
                    dk_row = dk_row + jnp.sum(da_col * qw, axis=0, keepdims=True)
                us = s // HALF
                dk_h[us] = jnp.where(t_loc == s - HALF * us, dk_row, dk_h[us])
            for u in range(2):
                rows = pl.ds(r0 + HALF * u, HALF)
                dqd_acc[rows, :] = dq_h[u]
                dkd_acc[rows, :] = dk_h[u]
                ad_scr[rows, :] = a_h[u]
            yield
        dv_in = _dot_tn(ad_scr[...].astype(BF16), dob)
        dq = dqt_acc[...] * c["e_q"] + dqd_acc[...]
        dk = dkh_acc[...] * c["e_k"] + dkd_acc[...]
        rr, cc = _iota2((BLOCK, BLOCK), 0), _iota2((BLOCK, BLOCK), 1)
        db = q * dq - k * dk + jnp.where(_iota2((BLOCK, HEAD), 0) == BLOCK - 1, dbl, 0.0)
        yield
        dg = _tri_left(_tri(cc >= rr), db)
        yield
        real = jnp.logical_or(n > 0, _iota2((BLOCK, HEAD), 0) >= N_PAD)
        df = jnp.where(real, dg / c["f"] - dk, 0.0)
        dq_ref[...] = dq.astype(BF16)
        dv_ref[...] = (dv_acc[...] + dv_in).astype(BF16)
        dfz_ref[...] = (df * (1.0 - lb) * c["sig"] * c["nsig"]).astype(BF16)
        dlb_ref[...] += jnp.sum(df * c["nsig"], axis=0, keepdims=True)

    def body(*refs):
        q_ref, fz_ref, v_ref, lb_ref, st_ref, do_ref = refs[:6]
        dq_ref, dfz_ref, dv_ref, dlb_ref = refs[6 + nr:10 + nr]
        scratch = refs[10 + 2 * nr:19 + 2 * nr]
        finish_ride = _ride_along(refs[6:6 + nr], refs[10 + nr:10 + 2 * nr], ride_modes, refs[19 + 2 * nr:], grid)
        step = pl.program_id(2)

        @pl.when(step == 0)
        def _():
            scratch[0][...] = jnp.zeros_like(scratch[0])
            dlb_ref[...] = jnp.zeros_like(dlb_ref)

        def head(hh):
            cols = pl.ds(hh * HEAD, HEAD)
            return chunk(n_blocks - 1 - step, q_ref.at[:, cols], fz_ref.at[:, cols], v_ref.at[:, cols],
                         lb_ref[:, cols], st_ref.at[hh], do_ref.at[:, cols], dq_ref.at[:, cols],
                         dfz_ref.at[:, cols], dv_ref.at[:, cols], dlb_ref.at[:, cols],
                         *[scr.at[hh] for scr in scratch])

        _lockstep([head(hh) for hh in range(HEADS_PER_STEP)])
        finish_ride()

    def blk(col0):
        return pl.BlockSpec((BLOCK, wide), lambda b, h, s: (b * n_blocks + n_blocks - 1 - s, col0 + h))

    any_spec = pl.BlockSpec(memory_space=pl.ANY)
    per_head = pltpu.VMEM((HEADS_PER_STEP, BLOCK, HEAD), F32)
    return _pcall(
        body, name=name, grid=grid,
        in_specs=[blk(0), blk(n_pairs), blk(2 * n_pairs), pl.BlockSpec((1, wide), lambda b, h, s: (0, h)),
                  pl.BlockSpec((None, HEADS_PER_STEP, HEAD, HEAD),
                               lambda b, h, s: (b * n_blocks + n_blocks - 1 - s, h, 0, 0)),
                  blk(0)] + [any_spec] * nr,
        out_specs=(blk(0), blk(0), blk(0), pl.BlockSpec((None, 1, wide), lambda b, h, s: (b, 0, h)))
        + (any_spec,) * nr,
        out_shape=(jax.ShapeDtypeStruct((t, d), BF16),) * 3 + (jax.ShapeDtypeStruct((bsz, 1, d), F32),)
        + _exchange_shapes(ride, ride_modes),
        scratch_shapes=[per_head] * 9 + _exchange_sems(nr),
        compiler_params=_params("arbitrary", "arbitrary", "arbitrary"),
    )(main, main, main, lbrow, states, do, *ride)


SB_GROUP = 3
SB_ROWS = SB_GROUP * BLOCK
SB_DEAD = -104.0


def _sb_tables():
    j = jnp.bitwise_and(_iota2((2 * BLOCK, 2 * BLOCK), 0), BLOCK - 1)
    s = _iota2((2 * BLOCK, 2 * BLOCK), 1)
    ones = s >= BLOCK
    return (_tri(jnp.logical_or(ones, j > s)), _tri(jnp.logical_or(ones, j <= s)),
            _tri(jnp.logical_or(ones, j < s)))


def _sums(x, table):
    hi = x.astype(BF16)
    lo = (x - hi.astype(F32)).astype(BF16)
    r = _dot(jnp.concatenate([hi, lo], axis=1), table)
    return r[:, :BLOCK], r[:, BLOCK:]


def _sb_logits(q, ks, scale, causal, pad_row):
    z = _dot_nt(q, ks) * scale
    log_keep = -(jnp.maximum(z, 0.0) + jnp.log(1.0 + jnp.exp(-jnp.abs(z))))
    log_beta = z + log_keep
    if causal is not None:
        log_keep = jnp.where(causal, log_keep, 0.0)
    if pad_row is not None:
        log_keep = log_keep * pad_row
    return z, log_beta, log_keep


def _sb_fwd(qkv, bsz, n_blocks, name):
    t, d3 = qkv.shape
    d = d3 // 3
    n_heads = d // HEAD
    lp = n_blocks * BLOCK
    n_groups = n_blocks // SB_GROUP
    assert n_groups * SB_GROUP == n_blocks
    scale = HEAD ** -0.5

    def body(q_ref, k_ref, v_ref, o_ref, tot_ref, stop_ref, c_scr):
        g = pl.program_id(2)
        upper, _, _ = _sb_tables()
        causal = _iota2((BLOCK, BLOCK), 1) < _iota2((BLOCK, BLOCK), 0)
        lane = _iota2((1, BLOCK), 1)
        o_ref[...] = jnp.zeros_like(o_ref)
        c_scr[...] = jnp.zeros_like(c_scr)

        def tiles(jobs):
            qrows = [pl.ds(r * BLOCK, BLOCK) for r, _, _, _ in jobs]
            krows = [pl.ds(pl.multiple_of(m * BLOCK, BLOCK), BLOCK) for _, m, _, _ in jobs]
            lg = [_sb_logits(q_ref[qr, :], k_ref[kr, :], scale, causal if dg else None, pad_row)
                  for qr, kr, (_, _, dg, pad_row) in zip(qrows, krows, jobs)]
            sm = [_sums(x[2], upper) for x in lg]
            a_all = []
            for qr, (_, _, dg, _), (_, log_beta, _), (inner, total) in zip(qrows, jobs, lg, sm):
                c = c_scr[qr, :]
                a = jnp.exp(log_beta + c + inner)
                a_all.append(jnp.where(causal, a, 0.0) if dg else a)
                c_scr[qr, :] = c + total
            out = [_dot(a.astype(BF16), v_ref[kr, :]) for a, kr in zip(a_all, krows)]
            for qr, o_part in zip(qrows, out):
                o_ref[qr, :] += o_part

        def pad_row_of(m):
            return jnp.where(jnp.logical_or(m > 0, lane >= N_PAD), 1.0, 0.0)

        base = SB_GROUP * g
        tiles([(r, base + kb, r == kb, pad_row_of(base) if kb == 0 else None)
               for kb in range(SB_GROUP - 1, -1, -1) for r in range(kb, SB_GROUP)])

        def live(least):
            def cond(carry):
                m, c_max = carry
                return jnp.logical_and(m >= least, c_max >= SB_DEAD)
            return cond

        def step_two(carry):
            m, _ = carry
            pad_row = pad_row_of(m - 1)
            tiles([(r, m, False, None) for r in range(SB_GROUP)]
                  + [(r, m - 1, False, pad_row) for r in range(SB_GROUP)])
            return m - 2, jnp.max(c_scr[...])

        def step_one(carry):
            m, _ = carry
            pad_row = pad_row_of(m)
            tiles([(r, m, False, pad_row) for r in range(SB_GROUP)])
            return m - 1, jnp.max(c_scr[...])

        carry = lax.while_loop(live(1), step_two, (base - 1, jnp.max(c_scr[...])))
        m_end, _ = lax.while_loop(live(0), step_one, carry)
        tot_ref[...] = c_scr[...]
        stop_ref[...] = jnp.broadcast_to((m_end + 1).astype(F32), stop_ref.shape)

    qblk = pl.BlockSpec((SB_ROWS, HEAD), lambda b, h, g: (b * n_groups + g, h))
    return _pcall(
        body, name=name, grid=(bsz, n_heads, n_groups),
        in_specs=[qblk, pl.BlockSpec((lp, HEAD), lambda b, h, g: (b, n_heads + h)),
                  pl.BlockSpec((lp, HEAD), lambda b, h, g: (b, 2 * n_heads + h))],
        out_specs=(qblk, qblk, pl.BlockSpec((None, 8, 128), lambda b, h, g: ((b * n_heads + h) * n_groups + g, 0, 0))),
        out_shape=(jax.ShapeDtypeStruct((t, d), F32), jax.ShapeDtypeStruct((t, d), F32),
                   jax.ShapeDtypeStruct((bsz * n_heads * n_groups, 8, 128), F32)),
        scratch_shapes=[pltpu.VMEM((SB_ROWS, HEAD), F32)],
        compiler_params=_params("parallel", "parallel", "arbitrary"),
    )(qkv, qkv, qkv)


def _sb_bwd(qkv, do, tot, stop, bsz, n_blocks, name):
    t, d3 = qkv.shape
    d = d3 // 3
    n_heads = d // HEAD
    lp = n_blocks * BLOCK
    n_groups = n_blocks // SB_GROUP
    scale = HEAD ** -0.5

    def body(q_ref, k_ref, v_ref, do_ref, tot_ref, stop_ref, dq_ref, dk_ref, dv_ref,
             dk_acc, dv_acc, dq_acc, p_scr, e_scr, dob_scr):
        g = pl.program_id(2)

        @pl.when(g == 0)
        def _():
            dk_acc[...] = jnp.zeros_like(dk_acc)
            dv_acc[...] = jnp.zeros_like(dv_acc)

        _, incl, excl = _sb_tables()
        causal = _iota2((BLOCK, BLOCK), 1) < _iota2((BLOCK, BLOCK), 0)
        lane = _iota2((1, BLOCK), 1)
        dob_scr[...] = do_ref[...].astype(BF16)
        dq_acc[...] = jnp.zeros_like(dq_acc)
        p_scr[...] = jnp.zeros_like(p_scr)
        e_scr[...] = jnp.zeros_like(e_scr)
        first = jnp.clip(jnp.max(stop_ref[...]).astype(jnp.int32), 0, SB_GROUP * g)

        def tiles(jobs):
            n_jobs = range(len(jobs))
            qrows = [pl.ds(r * BLOCK, BLOCK) for r, _, _, _ in jobs]
            krows = [pl.ds(pl.multiple_of(m * BLOCK, BLOCK), BLOCK) for _, m, _, _ in jobs]
            diag = [dg for _, _, dg, _ in jobs]
            lg = [_sb_logits(q_ref[qrows[i], :], k_ref[krows[i], :], scale, causal if diag[i] else None, jobs[i][3])
                  for i in n_jobs]
            d_a = [_dot_nt(dob_scr[qrows[i], :], v_ref[krows[i], :]) for i in n_jobs]
            sm = [_sums(lg[i][2], incl) for i in n_jobs]
            a_all = []
            for i in n_jobs:
                p = p_scr[qrows[i], :]
                a = jnp.exp(lg[i][1] + (tot_ref[qrows[i], :] - p - sm[i][0]))
                a_all.append(jnp.where(causal, a, 0.0) if diag[i] else a)
                p_scr[qrows[i], :] = p + sm[i][1]
            gr = [d_a[i] * a_all[i] for i in n_jobs]
            dv_part = [_dot_tn(a_all[i].astype(BF16), dob_scr[qrows[i], :]) for i in n_jobs]
            gs = [_sums(gr[i], excl) for i in n_jobs]
            dz_all = []
            for i in n_jobs:
                e = e_scr[qrows[i], :]
                dz = gr[i] - (gr[i] + e + gs[i][0]) * jnp.exp(lg[i][1])
                if diag[i]:
                    dz = jnp.where(causal, dz, 0.0)
                dz_all.append((dz * scale).astype(BF16))
                e_scr[qrows[i], :] = e + gs[i][1]
            dk_part = [_dot_tn(dz_all[i], q_ref[qrows[i], :]) for i in n_jobs]
            dq_part = [_dot(dz_all[i], k_ref[krows[i], :]) for i in n_jobs]
            for i in n_jobs:
                dv_acc[krows[i], :] += dv_part[i]
                dk_acc[krows[i], :] += dk_part[i]
                dq_acc[qrows[i], :] += dq_part[i]

        def pad_row_of(m):
            return jnp.where(jnp.logical_or(m > 0, lane >= N_PAD), 1.0, 0.0)

        def step_two(carry):
            m = carry
            pad_row = pad_row_of(m)
            tiles([(r, m, False, pad_row) for r in range(SB_GROUP)]
                  + [(r, m + 1, False, None) for r in range(SB_GROUP)])
            return m + 2

        def step_one(carry):
            m = carry
            pad_row = pad_row_of(m)
            tiles([(r, m, False, pad_row) for r in range(SB_GROUP)])
            return m + 1

        base = SB_GROUP * g
        m = lax.while_loop(lambda m: m + 1 < base, step_two, first)
        lax.while_loop(lambda m: m < base, step_one, m)
        tiles([(r, base + kb, r == kb, pad_row_of(base) if kb == 0 else None)
               for kb in range(SB_GROUP) for r in range(kb, SB_GROUP)])
        dq_ref[...] = dq_acc[...].astype(BF16)

        @pl.when(g == n_groups - 1)
        def _():
            dk_ref[...] = dk_acc[...].astype(BF16)
            dv_ref[...] = dv_acc[...].astype(BF16)

    qblk = pl.BlockSpec((SB_ROWS, HEAD), lambda b, h, g: (b * n_groups + g, h))
    kblk = pl.BlockSpec((lp, HEAD), lambda b, h, g: (b, n_heads + h))
    vblk = pl.BlockSpec((lp, HEAD), lambda b, h, g: (b, 2 * n_heads + h))
    hblk = pl.BlockSpec((lp, HEAD), lambda b, h, g: (b, h))
    sblk = pl.BlockSpec((None, 8, 128), lambda b, h, g: ((b * n_heads + h) * n_groups + g, 0, 0))
    return _pcall(
        body, name=name, grid=(bsz, n_heads, n_groups),
        in_specs=[qblk, kblk, vblk, qblk, qblk, sblk],
        out_specs=(qblk, hblk, hblk),
        out_shape=(jax.ShapeDtypeStruct((t, d), BF16),) * 3,
        scratch_shapes=[pltpu.VMEM((lp, HEAD), F32), pltpu.VMEM((lp, HEAD), F32)]
        + [pltpu.VMEM((SB_ROWS, HEAD), F32)] * 3 + [pltpu.VMEM((SB_ROWS, HEAD), BF16)],
        compiler_params=_params("parallel", "parallel", "arbitrary"),
    )(qkv, qkv, qkv, do, tot, stop)


def _adamw(w, g, m, v):
    m = ADAM_B1 * m + (1.0 - ADAM_B1) * g
    v = ADAM_B2 * v + (1.0 - ADAM_B2) * (g * g)
    m_hat = m / (1.0 - ADAM_B1 ** ADAM_STEP)
    v_hat = v / (1.0 - ADAM_B2 ** ADAM_STEP)
    delta = -ADAM_LR * (m_hat / (jnp.sqrt(v_hat) + ADAM_EPS) + ADAM_WD * w)
    return delta, m, v


def _update_sharded(w, parts, m, v, name):
    r, c = w.shape
    tr = min(r, ROW_TILE)

    def body(w_ref, p_ref, m_ref, v_ref, g_ref, d_ref, nm_ref, nv_ref):
        g = p_ref[0].astype(F32)
        for q in range(1, N_DEV):
            g = g + p_ref[q].astype(F32)
        g_ref[...] = g
        d_ref[...], nm_ref[...], nv_ref[...] = _adamw(w_ref[...], g, m_ref[...], v_ref[...])

    row = pl.BlockSpec((tr, c), lambda i: (i, 0))
    return _pcall(
        body, name=name, grid=(r // tr,),
        in_specs=[row, pl.BlockSpec((N_DEV, tr, c), lambda i: (0, i, 0)), row, row],
        out_specs=(row,) * 4, out_shape=(jax.ShapeDtypeStruct((r, c), F32),) * 4,
        compiler_params=_params("parallel"),
    )(w, parts, m, v)


SMALL_ROWS = 8


def _pack_small(dpre0, dpre1, dpost0, dpost1, dlb, dwon, loss, name):
    d = dpre0.shape[1]
    bsz = dlb.shape[0]

    def body(a0, a1, p0, p1, lb_ref, on_ref, loss_ref, out_ref):
        out_ref[...] = jnp.zeros_like(out_ref)
        out_ref[pl.ds(0, 1), :] = a0[...]
        out_ref[pl.ds(1, 1), :] = a1[...]
        out_ref[pl.ds(2, 1), :] = p0[...]
        out_ref[pl.ds(3, 1), :] = p1[...]
        acc = lb_ref[0]
        for b in range(1, bsz):
            acc = acc + lb_ref[b]
        out_ref[pl.ds(4, 1), :] = acc
        out_ref[pl.ds(5, 1), pl.ds(0, HEAD)] = on_ref[...]
        out_ref[pl.ds(6, 1), pl.ds(0, HEAD)] = loss_ref[pl.ds(0, 1), :]

    return _pcall(body, name=name, out_shape=jax.ShapeDtypeStruct((SMALL_ROWS, d), F32))(
        dpre0, dpre1, dpost0, dpost1, dlb, dwon, loss)


def _update_small(parts, pre, post, lbw, on, moments, name):
    d = pre.shape[1]

    def body(p_ref, pre_ref, post_ref, lbw_ref, on_ref, mpre, mpost, mlb, mon, vpre, vpost, vlb, von,
             loss_ref, *outs):
        def total(r0, nr, width):
            acc = p_ref[0, pl.ds(r0, nr), pl.ds(0, width)]
            for q in range(1, N_DEV):
                acc = acc + p_ref[q, pl.ds(r0, nr), pl.ds(0, width)]
            return acc

        def put(k, w, g, m, v):
            dl, nm, nv = _adamw(w, g, m, v)
            outs[4 * k][...] = g
            outs[4 * k + 1][...] = dl
            outs[4 * k + 2][...] = nm
            outs[4 * k + 3][...] = nv

        put(0, pre_ref[...], total(0, 2, d), mpre[...], vpre[...])
        put(1, post_ref[...], total(2, 2, d), mpost[...], vpost[...])
        a0, a1 = lbw_ref[pl.ds(0, 1), :], lbw_ref[pl.ds(1, 1), :]
        mx = jnp.maximum(a0, a1)
        e0, e1 = jnp.exp(a0 - mx), jnp.exp(a1 - mx)
        p0 = e0 / (e0 + e1)
        g0 = total(4, 1, d) * p0 * (1.0 - p0)
        for r, w, g in ((0, a0, g0), (1, a1, -g0)):
            row = pl.ds(r, 1)
            dl, nm, nv = _adamw(w, g, mlb[row, :], vlb[row, :])
            outs[8][row, :] = g
            outs[9][row, :] = dl
            outs[10][row, :] = nm
            outs[11][row, :] = nv
        put(3, on_ref[...], total(5, 1, HEAD), mon[...], von[...])
        loss_ref[...] = jnp.broadcast_to(total(6, 1, HEAD), loss_ref.shape)

    shapes = []
    for w in (pre, post, lbw, on):
        shapes += [jax.ShapeDtypeStruct(w.shape, F32)] * 4
    return _pcall(body, name=name, out_shape=(jax.ShapeDtypeStruct((8, 128), F32), *shapes))(
        parts, pre, post, lbw, on, *moments)


def kernel(x, meta_tokens, pre_norm, post_norm, hgrn_w_in, hgrn_lb, hgrn_out_norm, hgrn_w_out, sb_w_in, sb_w_out, loss_target, m_meta_tokens, m_pre_norm, m_post_norm, m_hgrn_w_in, m_hgrn_lb, m_hgrn_out_norm, m_hgrn_w_out, m_sb_w_in, m_sb_w_out, v_meta_tokens, v_pre_norm, v_post_norm, v_hgrn_w_in, v_hgrn_lb, v_hgrn_out_norm, v_hgrn_w_out, v_sb_w_in, v_sb_w_out):
    bsz, seq, d = x.shape
    n_blocks = seq // BLOCK + 1
    lp = n_blocks * BLOCK
    t = bsz * lp
    s = hgrn_w_in.shape[2]
    dsh = d // N_DEV

    w_in_h, w_out_h, meta_all = _exchange(
        [hgrn_w_in[0].astype(BF16), hgrn_w_out[0].astype(BF16), meta_tokens], ["gather"] * 3, "gather_weights")
    w_out_h = w_out_h.reshape(d, d)
    meta_full = jnp.transpose(meta_all, (1, 0, 2)).reshape(N_META, d)

    h0 = jnp.concatenate(
        [jnp.zeros((bsz, N_PAD, d), F32), jnp.broadcast_to(meta_full[None], (bsz, N_META, d)), x], axis=1
    ).reshape(t, d)
    lbrow = jnp.cumsum(jax.nn.softmax(hgrn_lb, axis=0), axis=0)[0:1]

    main0, gate0, yn0 = _norm_inproj(h0, pre_norm[0:1], w_in_h, F32, "inproj_hgrn")
    o0, states, w_in_s, w_out_s = _hgrn_fwd(
        main0, lbrow, bsz, n_blocks, [sb_w_in[0].astype(BF16), sb_w_out[0].astype(BF16)], ["gather"] * 2, "hgrn_fwd")
    w_out_s = w_out_s.reshape(d, d)
    h1, y0 = _mix_out(o0, gate0, h0, w_out_h, hgrn_out_norm, post_norm[0:1], True, "mix_out_hgrn")
    main1, gate1, yn1 = _norm_inproj(h1, pre_norm[1:2], w_in_s, BF16, "inproj_sb")
    o1, tot, stop = _sb_fwd(main1, bsz, n_blocks, "sb_fwd")
    h2, y1 = _mix_out(o1, gate1, h1, w_out_s, hgrn_out_norm, post_norm[1:2], False, "mix_out_sb")
    dh2, loss_part = _loss_grad(h2, loss_target, n_blocks, "loss_grad")

    do1, dgate1, dw_out_s, dpost1, _ = _mix_out_bwd(
        dh2, y1, o1, gate1, w_out_s, hgrn_out_norm, post_norm[1:2], False, "mix_out_sb_bwd")
    dq1, dk1, dv1 = _sb_bwd(main1, do1, tot, stop, bsz, n_blocks, "sb_bwd")
    dproj1 = (dq1, dk1, dv1, dgate1)
    dh1, dpre1 = _inproj_bwd_x(dproj1, w_in_s, h1, pre_norm[1:2], dh2, "inproj_sb_bwd_x")
    dw_in_s = _inproj_bwd_w(yn1, dproj1, s, "inproj_sb_bwd_w")

    do0, dgate0, dw_out_h, dpost0, dwon = _mix_out_bwd(
        dh1, y0, o0, gate0, w_out_h, hgrn_out_norm, post_norm[0:1], True, "mix_out_hgrn_bwd")
    dq0, dfz0, dv0, dlb, p_in_s, p_out_s = _hgrn_bwd(
        main0, lbrow, states, do0, bsz, n_blocks, [dw_in_s, dw_out_s.reshape(N_DEV, dsh, d)], ["scatter"] * 2,
        "hgrn_bwd")
    dproj0 = (dq0, dfz0, dv0, dgate0)
    dh0, dpre0 = _inproj_bwd_x(dproj0, w_in_h, h0, pre_norm[0:1], dh1, "inproj_hgrn_bwd_x")
    dw_in_h = _inproj_bwd_w(yn0, dproj0, s, "inproj_hgrn_bwd_w")

    dh0 = dh0.reshape(bsz, lp, d)
    grad_x = dh0[:, BLOCK:]
    dmeta = jnp.sum(dh0[:, N_PAD:BLOCK], axis=0)
    dmeta = jnp.transpose(dmeta.reshape(N_META, N_DEV, dsh), (1, 0, 2))
    small = _pack_small(dpre0, dpre1, dpost0, dpost1, dlb, dwon, loss_part, "pack_small")

    p_in_h, p_out_h, p_meta, p_small = _exchange(
        [dw_in_h, dw_out_h.reshape(N_DEV, dsh, d), dmeta, small], ["scatter"] * 3 + ["gather"], "exchange_grads")

    u_meta = _update_sharded(meta_tokens, p_meta, m_meta_tokens, v_meta_tokens, "update_meta")
    u_in_h = _update_sharded(hgrn_w_in[0], p_in_h, m_hgrn_w_in[0], v_hgrn_w_in[0], "update_hgrn_w_in")
    u_out_h = _update_sharded(hgrn_w_out[0], p_out_h, m_hgrn_w_out[0], v_hgrn_w_out[0], "update_hgrn_w_out")
    u_in_s = _update_sharded(sb_w_in[0], p_in_s, m_sb_w_in[0], v_sb_w_in[0], "update_sb_w_in")
    u_out_s = _update_sharded(sb_w_out[0], p_out_s, m_sb_w_out[0], v_sb_w_out[0], "update_sb_w_out")
    sm = _update_small(p_small, pre_norm, post_norm, hgrn_lb, hgrn_out_norm,
                       (m_pre_norm, m_post_norm, m_hgrn_lb, m_hgrn_out_norm,
                        v_pre_norm, v_post_norm, v_hgrn_lb, v_hgrn_out_norm), "update_small")
    loss = sm[0][0, 0]
    u_pre, u_post, u_lb, u_on = sm[1:5], sm[5:9], sm[9:13], sm[13:17]

    per_w = [u_meta, u_pre, u_post, tuple(a[None] for a in u_in_h), u_lb, u_on,
             tuple(a[None] for a in u_out_h), tuple(a[None] for a in u_in_s), tuple(a[None] for a in u_out_s)]
    outs = [loss, grad_x]
    for k in range(4):
        outs += [u[k] for u in per_w]
    return tuple(outs)
```

```python
import jax
import jax.numpy as jnp
from jax import lax
from jax.experimental import pallas as pl
from jax.experimental.pallas import tpu as pltpu

F32 = jnp.float32
BF16 = jnp.bfloat16

N_DEV = 8
BLOCK = 128
N_META = 16
N_PAD = BLOCK - N_META
HEAD = 128
SUB = 16
N_SUB = BLOCK // SUB
HALF = 8
EPS = 1e-6
ROW_TILE = 256
K_TILE = 768
NEG_BIG = -1e30

ADAM_LR = 0.001
ADAM_B1 = 0.9
ADAM_B2 = 0.999
ADAM_EPS = 1e-08
ADAM_WD = 0.01
ADAM_STEP = 10

VMEM_LIMIT = 56 * 1024 * 1024


def _pcall(body, **kw):
    return pl.pallas_call(body, **kw)


def _params(*sem):
    return pltpu.CompilerParams(dimension_semantics=sem, vmem_limit_bytes=VMEM_LIMIT)


def _dot(a, b):
    return jnp.dot(a, b, preferred_element_type=F32)


def _dot_nt(a, b):
    return lax.dot_general(a, b, (((1,), (1,)), ((), ())), preferred_element_type=F32)


def _dot_tn(a, b):
    return lax.dot_general(a, b, (((0,), (0,)), ((), ())), preferred_element_type=F32)


def _split(x, pieces):
    out = []
    for _ in range(pieces):
        p = x.astype(BF16)
        out.append(p)
        x = x - p.astype(F32)
    return out


def _tri_right(x, tri, pieces=2):
    return sum(_dot(p, tri) for p in _split(x, pieces))


def _tri_left(tri, x, pieces=3):
    return sum(_dot(tri, p) for p in _split(x, pieces))


def _iota2(shape, dim):
    return lax.broadcasted_iota(jnp.int32, shape, dim)


def _tri(cond):
    return jnp.where(cond, 1.0, 0.0).astype(BF16)


def _sig_pair(x):
    e = jnp.exp(-jnp.abs(x))
    r = 1.0 / (1.0 + e)
    er = e * r
    pos = x >= 0
    return jnp.where(pos, r, er), jnp.where(pos, er, r)


def _expand(rows):
    return jnp.concatenate([jnp.broadcast_to(r, (SUB, HEAD)) for r in rows], axis=0)


def _exchange_shapes(arrays, modes):
    return tuple(jax.ShapeDtypeStruct((N_DEV,) + tuple(a.shape[1:] if m == "scatter" else a.shape), a.dtype)
                 for a, m in zip(arrays, modes))


def _exchange_sems(n):
    if n == 0:
        return []
    return [pltpu.SemaphoreType.DMA((n, N_DEV - 1)), pltpu.SemaphoreType.DMA((n, N_DEV - 1)),
            pltpu.SemaphoreType.DMA((n,))]


def _exchange_copies(ins, outs, modes, send_sems, recv_sems, local_sems):
    mx, my, mc = lax.axis_index("x"), lax.axis_index("y"), lax.axis_index("c")
    me = 4 * mx + 2 * my + mc

    def src(i, slot):
        return ins[i].at[slot] if modes[i] == "scatter" else ins[i]

    def peer_of(mask):
        px = 1 - mx if mask & 4 else mx
        py = 1 - my if mask & 2 else my
        pc = 1 - mc if mask & 1 else mc
        return px, py, pc

    def copy(i, mask, dst_slot):
        px, py, pc = peer_of(mask)
        return pltpu.make_async_remote_copy(
            src_ref=src(i, 4 * px + 2 * py + pc), dst_ref=outs[i].at[dst_slot],
            send_sem=send_sems.at[i, mask - 1], recv_sem=recv_sems.at[i, mask - 1],
            device_id=(px, py, pc), device_id_type=pl.DeviceIdType.MESH)

    n = len(ins)
    sends = [copy(i, mask, me) for mask in range(1, N_DEV) for i in range(n)]
    own = [pltpu.make_async_copy(src(i, me), outs[i].at[me], local_sems.at[i]) for i in range(n)]
    arrivals = []
    for mask in range(1, N_DEV):
        px, py, pc = peer_of(mask)
        arrivals += [copy(i, mask, 4 * px + 2 * py + pc) for i in range(n)]

    def start():
        for cp in sends + own:
            cp.start()

    def wait():
        for cp in arrivals:
            cp.wait_recv()
        for cp in sends:
            cp.wait_send()
        for cp in own:
            cp.wait()

    return start, wait


def _exchange(arrays, modes, name):
    n = len(arrays)

    def body(*refs):
        start, wait = _exchange_copies(refs[:n], refs[n:2 * n], modes, *refs[2 * n:])
        start()
        wait()

    any_spec = pl.BlockSpec(memory_space=pl.ANY)
    return _pcall(
        body, name=name, out_shape=_exchange_shapes(arrays, modes),
        in_specs=[any_spec] * n, out_specs=tuple([any_spec] * n),
        scratch_shapes=_exchange_sems(n),
    )(*arrays)


def _norm_inproj(h, wnorm, w_all, main_dtype, name):
    t, d = h.shape
    p_n, _, s = w_all.shape
    n_main = 3 * d // s
    tm = ROW_TILE

    def body(h_ref, wn_ref, w_ref, main_ref, gate_ref, ynt_ref):
        x = h_ref[...]
        y = x * lax.rsqrt(jnp.mean(x * x, axis=-1, keepdims=True) + EPS) * wn_ref[...]
        yb = y.astype(BF16)
        ynt_ref[...] = y.T.astype(BF16)
        for p in range(p_n):
            r = _dot(yb, w_ref[p])
            if p < n_main:
                main_ref[:, p * s:(p + 1) * s] = r.astype(main_dtype)
            else:
                gate_ref[:, (p - n_main) * s:(p - n_main + 1) * s] = r

    return _pcall(
        body, name=name, grid=(t // tm,),
        in_specs=[pl.BlockSpec((tm, d), lambda i: (i, 0)), pl.BlockSpec((1, d), lambda i: (0, 0)),
                  pl.BlockSpec((p_n, d, s), lambda i: (0, 0, 0))],
        out_specs=(pl.BlockSpec((tm, 3 * d), lambda i: (i, 0)), pl.BlockSpec((tm, d), lambda i: (i, 0)),
                   pl.BlockSpec((d, tm), lambda i: (0, i))),
        out_shape=(jax.ShapeDtypeStruct((t, 3 * d), main_dtype), jax.ShapeDtypeStruct((t, d), F32),
                   jax.ShapeDtypeStruct((d, t), BF16)),
        compiler_params=_params("parallel"),
    )(h, wnorm, w_all)


def _mix_out(o, gate, h_in, w_out, w_on, w_post, head_norm, name):
    t, d = o.shape
    n_heads = d // HEAD
    tm = ROW_TILE

    def body(o_ref, g_ref, h_ref, w_ref, won_ref, wp_ref, hout_ref, y_ref, u_scr):
        for hh in range(n_heads):
            cs = slice(hh * HEAD, (hh + 1) * HEAD)
            oh = o_ref[:, cs]
            gt = g_ref[:, cs]
            if head_norm:
                oh = oh * lax.rsqrt(jnp.mean(oh * oh, axis=-1, keepdims=True) + EPS) * won_ref[...]
            u_scr[:, cs] = (oh * (gt * jax.nn.sigmoid(gt))).astype(BF16)
        y = _dot(u_scr[...], w_ref[...])
        y_ref[...] = y
        r = y * lax.rsqrt(jnp.mean(y * y, axis=-1, keepdims=True) + EPS) * wp_ref[...]
        hout_ref[...] = h_ref[...] + r

    row = pl.BlockSpec((tm, d), lambda i: (i, 0))
    return _pcall(
        body, name=name, grid=(t // tm,),
        in_specs=[row, row, row, pl.BlockSpec((d, d), lambda i: (0, 0)),
                  pl.BlockSpec((1, HEAD), lambda i: (0, 0)), pl.BlockSpec((1, d), lambda i: (0, 0))],
        out_specs=(row, row),
        out_shape=(jax.ShapeDtypeStruct((t, d), F32), jax.ShapeDtypeStruct((t, d), F32)),
        scratch_shapes=[pltpu.VMEM((tm, d), BF16)],
        compiler_params=_params("parallel"),
    )(o, gate, h_in, w_out, w_on, w_post)


def _loss_grad(h_out, target, n_blocks, name):
    t, d = h_out.shape

    def body(h_ref, t_ref, dh_ref, loss_ref):
        i = pl.program_id(0)

        @pl.when(i == 0)
        def _():
            loss_ref[...] = jnp.zeros_like(loss_ref)

        real = (i % n_blocks) > 0
        err = jnp.where(real, h_ref[...] - t_ref[...], 0.0)
        dh_ref[...] = err * (1.0 / d)
        part = jnp.sum(jnp.sum(err * err, axis=-1, keepdims=True), axis=0, keepdims=True)
        loss_ref[...] += part * (0.5 / d)

    return _pcall(
        body, name=name, grid=(t // BLOCK,),
        in_specs=[pl.BlockSpec((BLOCK, d), lambda i: (i, 0)),
                  pl.BlockSpec((None, BLOCK, d), lambda i: (i // n_blocks, jnp.maximum(i % n_blocks - 1, 0), 0))],
        out_specs=(pl.BlockSpec((BLOCK, d), lambda i: (i, 0)), pl.BlockSpec((8, 128), lambda i: (0, 0))),
        out_shape=(jax.ShapeDtypeStruct((t, d), F32), jax.ShapeDtypeStruct((8, 128), F32)),
        compiler_params=_params("arbitrary"),
    )(h_out, target)


def _mix_out_bwd(dh, y, o, gate, w_out, w_on, w_post, head_norm, name):
    t, d = o.shape
    n_heads = d // HEAD
    tm = ROW_TILE
    last = t // tm - 1

    def body(dh_ref, y_ref, o_ref, g_ref, w_ref, won_ref, wp_ref,
             do_ref, dg_ref, dw_ref, dwp_ref, dwon_ref, u_scr, acc):
        i = pl.program_id(0)

        @pl.when(i == 0)
        def _():
            acc[...] = jnp.zeros_like(acc)
            dwp_ref[...] = jnp.zeros_like(dwp_ref)
            dwon_ref[...] = jnp.zeros_like(dwon_ref)

        yv = y_ref[...]
        rs = lax.rsqrt(jnp.mean(yv * yv, axis=-1, keepdims=True) + EPS)
        yh = yv * rs
        dr = dh_ref[...]
        dwp_ref[...] += jnp.sum(dr * yh, axis=0, keepdims=True)
        wd = dr * wp_ref[...]
        dy = rs * (wd - yh * jnp.mean(wd * yh, axis=-1, keepdims=True))
        dyb = dy.astype(BF16)
        du = _dot_nt(dyb, w_ref[...])
        for hh in range(n_heads):
            cs = slice(hh * HEAD, (hh + 1) * HEAD)
            oh = o_ref[:, cs]
            gt = g_ref[:, cs]
            sg = jax.nn.sigmoid(gt)
            sl = gt * sg
            duh = du[:, cs]
            if head_norm:
                rsh = lax.rsqrt(jnp.mean(oh * oh, axis=-1, keepdims=True) + EPS)
                ohat = oh * rsh
                on = ohat * won_ref[...]
            else:
                on = oh
            u_scr[:, cs] = (on * sl).astype(BF16)
            don = duh * sl
            dg_ref[:, cs] = (duh * on * (sg * (1.0 + gt * (1.0 - sg)))).astype(BF16)
            if head_norm:
                dwon_ref[...] += jnp.sum(don * ohat, axis=0, keepdims=True)
                wdn = don * won_ref[...]
                do_ref[:, cs] = rsh * (wdn - ohat * jnp.mean(wdn * ohat, axis=-1, keepdims=True))
            else:
                do_ref[:, cs] = don
        acc[...] += _dot_tn(u_scr[...], dyb)

        @pl.when(i == last)
        def _():
            dw_ref[...] = acc[...].astype(BF16)

    row = pl.BlockSpec((tm, d), lambda i: (i, 0))
    return _pcall(
        body, name=name, grid=(t // tm,),
        in_specs=[row, row, row, row, pl.BlockSpec((d, d), lambda i: (0, 0)),
                  pl.BlockSpec((1, HEAD), lambda i: (0, 0)), pl.BlockSpec((1, d), lambda i: (0, 0))],
        out_specs=(row, row, pl.BlockSpec((d, d), lambda i: (0, 0)), pl.BlockSpec((1, d), lambda i: (0, 0)),
                   pl.BlockSpec((1, HEAD), lambda i: (0, 0))),
        out_shape=(jax.ShapeDtypeStruct((t, d), F32), jax.ShapeDtypeStruct((t, d), BF16),
                   jax.ShapeDtypeStruct((d, d), BF16), jax.ShapeDtypeStruct((1, d), F32),
                   jax.ShapeDtypeStruct((1, HEAD), F32)),
        scratch_shapes=[pltpu.VMEM((tm, d), BF16), pltpu.VMEM((d, d), F32)],
        compiler_params=_params("arbitrary"),
    )(dh, y, o, gate, w_out, w_on, w_post)


def _inproj_bwd_x(dparts, w_all, h, wnorm, dres, name):
    t, d = h.shape
    p_n, _, s = w_all.shape
    per = d // s
    tm = ROW_TILE

    def body(d0_ref, d1_ref, d2_ref, d3_ref, w_ref, h_ref, wn_ref, dres_ref, dh_ref, dwn_ref):
        i = pl.program_id(0)

        @pl.when(i == 0)
        def _():
            dwn_ref[...] = jnp.zeros_like(dwn_ref)

        pieces = (d0_ref, d1_ref, d2_ref, d3_ref)
        dyn = jnp.zeros((tm, d), F32)
        for p in range(p_n):
            blk = pieces[p // per][:, (p % per) * s:(p % per + 1) * s]
            dyn = dyn + _dot_nt(blk, w_ref[p])
        x = h_ref[...]
        rs = lax.rsqrt(jnp.mean(x * x, axis=-1, keepdims=True) + EPS)
        xh = x * rs
        dwn_ref[...] += jnp.sum(dyn * xh, axis=0, keepdims=True)
        wd = dyn * wn_ref[...]
        dh_ref[...] = dres_ref[...] + rs * (wd - xh * jnp.mean(wd * xh, axis=-1, keepdims=True))

    row = pl.BlockSpec((tm, d), lambda i: (i, 0))
    return _pcall(
        body, name=name, grid=(t // tm,),
        in_specs=[row, row, row, row, pl.BlockSpec((p_n, d, s), lambda i: (0, 0, 0)),
                  row, pl.BlockSpec((1, d), lambda i: (0, 0)), row],
        out_specs=(row, pl.BlockSpec((1, d), lambda i: (0, 0))),
        out_shape=(jax.ShapeDtypeStruct((t, d), F32), jax.ShapeDtypeStruct((1, d), F32)),
        compiler_params=_params("arbitrary"),
    )(*dparts, w_all, h, wnorm, dres)


def _inproj_bwd_w(ynt, dparts, s, name):
    d, t = ynt.shape
    per = d // s
    n_sh = len(dparts) * per
    tk = K_TILE
    last = t // tk - 1

    def body(ynt_ref, d0_ref, d1_ref, d2_ref, d3_ref, dw_ref, acc):
        p, i = pl.program_id(0), pl.program_id(1)

        @pl.when(i == 0)
        def _():
            acc[...] = jnp.zeros_like(acc)

        for a, piece in enumerate((d0_ref, d1_ref, d2_ref, d3_ref)):
            @pl.when(p // per == a)
            def _():
                acc[...] += _dot(ynt_ref[...], piece[...])

        @pl.when(i == last)
        def _():
            dw_ref[...] = acc[...].astype(BF16)

    def piece_spec(a):
        return pl.BlockSpec((tk, s), lambda p, i: (jnp.where(p // per == a, i, 0),
                                                   jnp.where(p // per == a, p % per, 0)))

    return _pcall(
        body, name=name, grid=(n_sh, t // tk),
        in_specs=[pl.BlockSpec((d, tk), lambda p, i: (0, i))] + [piece_spec(a) for a in range(4)],
        out_specs=pl.BlockSpec((None, d, s), lambda p, i: (p, 0, 0)),
        out_shape=jax.ShapeDtypeStruct((n_sh, d, s), BF16),
        scratch_shapes=[pltpu.VMEM((d, s), F32)],
        compiler_params=_params("parallel", "arbitrary"),
    )(ynt, *dparts)


def _hgrn_common(q_ref, fz_ref, lb, b_scr, k_scr):
    fz = fz_ref[...]
    sig, nsig = _sig_pair(fz)
    f = lb + (1.0 - lb) * sig
    k = (1.0 - lb) * nsig
    rr, cc = _iota2((BLOCK, BLOCK), 0), _iota2((BLOCK, BLOCK), 1)
    b = _tri_left(_tri(cc <= rr), jnp.log(f))
    b_scr[...] = b
    k_scr[...] = k
    bend = [b_scr[pl.ds(SUB * j + SUB - 1, 1), :] for j in range(N_SUB)]
    bref = [jnp.zeros((1, HEAD), F32)] + bend[:-1]
    refrow, bendrow = _expand(bref), _expand(bend)
    e_q = jnp.exp(b - refrow)
    e_k = jnp.exp(bendrow - b)
    qt = q_ref[...] * e_q
    kh = k * e_k
    bl = bend[-1]
    return dict(sig=sig, nsig=nsig, f=f, k=k, b=b, bend=bend, bref=bref, refrow=refrow,
                e_q=e_q, e_k=e_k, qt=qt, kh=kh, bl=bl)


HEADS_PER_STEP = 4


def _lockstep(chunks):
    live = list(chunks)
    while live:
        still = []
        for gen in live:
            try:
                next(gen)
                still.append(gen)
            except StopIteration:
                pass
        live = still


def _ride_along(ride_ins, ride_outs, modes, sems, grid):
    if not ride_ins:
        return lambda: None
    ids = [pl.program_id(a) for a in range(len(grid))]
    start, wait = _exchange_copies(ride_ins, ride_outs, modes, *sems)
    first, last = ids[0] == 0, ids[0] == grid[0] - 1
    for a in range(1, len(grid)):
        first = jnp.logical_and(first, ids[a] == 0)
        last = jnp.logical_and(last, ids[a] == grid[a] - 1)
    pl.when(first)(start)
    return lambda: pl.when(last)(wait)


def _hgrn_fwd(main, lbrow, bsz, n_blocks, ride, ride_modes, name):
    t, d3 = main.shape
    d = d3 // 3
    n_heads = d // HEAD
    n_pairs = n_heads // HEADS_PER_STEP
    wide = HEADS_PER_STEP * HEAD
    nr = len(ride)
    grid = (bsz, n_pairs, n_blocks)

    def chunk(q_ref, fz_ref, v_ref, lb, o_ref, st_ref, s_scr, b_scr, k_scr, o_acc):
        c = _hgrn_common(q_ref, fz_ref, lb, b_scr, k_scr)
        yield
        s_t = s_scr[...]
        st_ref[...] = s_t
        vb = v_ref[...].astype(BF16)
        q_state = c["qt"] * _expand([jnp.exp(r) for r in c["bref"]])
        o_state = _dot_nt(q_state.astype(BF16), s_t.astype(BF16))
        js = range(N_SUB - 1)
        khb = c["kh"].astype(BF16)
        lhs = [(c["qt"][SUB * (j + 1):] * jnp.exp(c["refrow"][SUB * (j + 1):] - c["bend"][j])).astype(BF16)
               for j in js]
        yield
        a_js = [_dot_nt(lhs[j], khb[SUB * j:SUB * (j + 1)]) for j in js]
        k_state = c["kh"] * _expand([jnp.exp(c["bl"] - r) for r in c["bend"]])
        s_scr[...] = s_t * jnp.exp(c["bl"]) + _dot_tn(vb, k_state.astype(BF16))
        yield
        o_js = [_dot(a_js[j].astype(BF16), vb[SUB * j:SUB * (j + 1)]) for j in js]
        yield
        o_acc[...] = o_state
        for j in js:
            o_acc[SUB * (j + 1):, :] += o_js[j]
        t_loc = _iota2((HALF, HEAD), 0)
        for i in range(N_SUB):
            r0 = SUB * i
            q_h = [q_ref[pl.ds(r0 + HALF * u, HALF), :] for u in range(2)]
            b_h = [b_scr[pl.ds(r0 + HALF * u, HALF), :] for u in range(2)]
            o_h = [jnp.zeros((HALF, HEAD), F32) for _ in range(2)]
            for s in range(SUB):
                brow = b_scr[pl.ds(r0 + s, 1), :]
                krow = k_scr[pl.ds(r0 + s, 1), :]
                vrow = v_ref[pl.ds(r0 + s, 1), :]
                for u in range(s // HALF, 2):
                    diff = b_h[u] - brow
                    if u == s // HALF:
                        diff = jnp.where(t_loc >= s - HALF * u, diff, NEG_BIG)
                    col = jnp.sum(q_h[u] * krow * jnp.exp(diff), axis=-1, keepdims=True)
                    o_h[u] = o_h[u] + col * vrow
            for u in range(2):
                o_acc[pl.ds(r0 + HALF * u, HALF), :] += o_h[u]
            yield
        o_ref[...] = o_acc[...]

    def body(*refs):
        q_ref, fz_ref, v_ref, lb_ref = refs[:4]
        o_ref, st_ref = refs[4 + nr:6 + nr]
        s_scr, b_scr, k_scr, o_acc = refs[6 + 2 * nr:10 + 2 * nr]
        finish_ride = _ride_along(refs[4:4 + nr], refs[6 + nr:6 + 2 * nr], ride_modes, refs[10 + 2 * nr:], grid)

        @pl.when(pl.program_id(2) == 0)
        def _():
            s_scr[...] = jnp.zeros_like(s_scr)

        def head(hh):
            cols = pl.ds(hh * HEAD, HEAD)
            return chunk(q_ref.at[:, cols], fz_ref.at[:, cols], v_ref.at[:, cols], lb_ref[:, cols],
                         o_ref.at[:, cols], st_ref.at[hh], s_scr.at[hh], b_scr.at[hh], k_scr.at[hh], o_acc.at[hh])

        _lockstep([head(hh) for hh in range(HEADS_PER_STEP)])
        finish_ride()

    def blk(col0):
        return pl.BlockSpec((BLOCK, wide), lambda b, h, n: (b * n_blocks + n, col0 + h))

    any_spec = pl.BlockSpec(memory_space=pl.ANY)
    per_head = pltpu.VMEM((HEADS_PER_STEP, BLOCK, HEAD), F32)
    return _pcall(
        body, name=name, grid=grid,
        in_specs=[blk(0), blk(n_pairs), blk(2 * n_pairs), pl.BlockSpec((1, wide), lambda b, h, n: (0, h))]
        + [any_spec] * nr,
        out_specs=(blk(0), pl.BlockSpec((None, HEADS_PER_STEP, HEAD, HEAD), lambda b, h, n: (b * n_blocks + n, h, 0, 0)))
        + (any_spec,) * nr,
        out_shape=(jax.ShapeDtypeStruct((t, d), F32),
                   jax.ShapeDtypeStruct((bsz * n_blocks, n_heads, HEAD, HEAD), F32))
        + _exchange_shapes(ride, ride_modes),
        scratch_shapes=[per_head] * 4 + _exchange_sems(nr),
        compiler_params=_params("arbitrary", "arbitrary", "arbitrary"),
    )(main, main, main, lbrow, *ride)


def _hgrn_bwd(main, lbrow, states, do, bsz, n_blocks, ride, ride_modes, name):
    t, d3 = main.shape
    d = d3 // 3
    n_pairs = d // HEAD // HEADS_PER_STEP
    wide = HEADS_PER_STEP * HEAD
    nr = len(ride)
    grid = (bsz, n_pairs, n_blocks)

    def chunk(n, q_ref, fz_ref, v_ref, lb, st_ref, do_ref, dq_ref, dfz_ref, dv_ref, dlb_ref,
              ds_scr, b_scr, k_scr, dqt_acc, dkh_acc, dv_acc, dqd_acc, dkd_acc, ad_scr):
        c = _hgrn_common(q_ref, fz_ref, lb, b_scr, k_scr)
        yield
        q, k = q_ref[...], c["k"]
        vb = v_ref[...].astype(BF16)
        dob = do_ref[...].astype(BF16)
        s0_t = st_ref[...]
        ds1_t = ds_scr[...]
        e_ref = _expand([jnp.exp(r) for r in c["bref"]])
        e_end = _expand([jnp.exp(c["bl"] - r) for r in c["bend"]])
        e_bl = jnp.exp(c["bl"])
        q_state = c["qt"] * e_ref
        k_state = c["kh"] * e_end
        dq_state = _dot(dob, s0_t.astype(BF16))
        dk_state = _dot(vb, ds1_t.astype(BF16))
        dv_state = _dot_nt(k_state.astype(BF16), ds1_t.astype(BF16))
        ds_scr[...] = ds1_t * e_bl + _dot_tn(dob, q_state.astype(BF16))
        js = range(N_SUB - 1)
        lo = [slice(SUB * j, SUB * (j + 1)) for j in js]
        khb = c["kh"].astype(BF16)
        dj = [jnp.exp(c["refrow"][SUB * (j + 1):] - c["bend"][j]) for j in js]
        lhs = [(c["qt"][SUB * (j + 1):] * dj[j]).astype(BF16) for j in js]
        yield
        a_js = [_dot_nt(lhs[j], khb[lo[j]]) for j in js]
        da_js = [_dot_nt(dob[SUB * (j + 1):], vb[lo[j]]).astype(BF16) for j in js]
        dqt_acc[...] = dq_state * e_ref
        dkh_acc[...] = dk_state * e_end
        dv_acc[...] = dv_state
        dbl = (jnp.sum(s0_t * ds1_t, axis=0, keepdims=True) * e_bl
               + jnp.sum(k_state * dk_state, axis=0, keepdims=True))
        yield
        dv_js = [_dot_tn(a_js[j].astype(BF16), dob[SUB * (j + 1):]) for j in js]
        dq_js = [_dot(da_js[j], khb[lo[j]]) * dj[j] for j in js]
        dk_js = [_dot_tn(da_js[j], lhs[j]) for j in js]
        yield
        for j in js:
            dv_acc[lo[j], :] += dv_js[j]
            dqt_acc[SUB * (j + 1):, :] += dq_js[j]
            dkh_acc[lo[j], :] += dk_js[j]
        t_loc = _iota2((HALF, HEAD), 0)
        lane = _iota2((HALF, HEAD), 1)
        for i in range(N_SUB):
            r0 = SUB * i
            q_h = [q_ref[pl.ds(r0 + HALF * u, HALF), :] for u in range(2)]
            b_h = [b_scr[pl.ds(r0 + HALF * u, HALF), :] for u in range(2)]
            do_h = [do_ref[pl.ds(r0 + HALF * u, HALF), :] for u in range(2)]
            zero = jnp.zeros((HALF, HEAD), F32)
            dq_h, dk_h, a_h = [zero, zero], [zero, zero], [zero, zero]
            for s in range(SUB):
                brow = b_scr[pl.ds(r0 + s, 1), :]
                krow = k_scr[pl.ds(r0 + s, 1), :]
                vrow = v_ref[pl.ds(r0 + s, 1), :]
                dk_row = jnp.zeros((1, HEAD), F32)
                for u in range(s // HALF, 2):
                    diff = b_h[u] - brow
                    if u == s // HALF:
                        diff = jnp.where(t_loc >= s - HALF * u, diff, NEG_BIG)
                    w = jnp.exp(diff)
                    qw = q_h[u] * w
                    a_col = jnp.sum(qw * krow, axis=-1, keepdims=True)
                    da_col = jnp.sum(do_h[u] * vrow, axis=-1, keepdims=True)
                    a_h[u] = jnp.where(lane == r0 + s, a_col, a_h[u])
                    dq_h[u] = dq_h[u] + da_col * (w * krow)
                    dk_row = dk_row + jnp.sum(da_col * qw, axis=0, keepdims=True)
                us = s // HALF
                dk_h[us] = jnp.where(t_loc == s - HALF * us, dk_row, dk_h[us])
            for u in range(2):
                rows = pl.ds(r0 + HALF * u, HALF)
                dqd_acc[rows, :] = dq_h[u]
                dkd_acc[rows, :] = dk_h[u]
                ad_scr[rows, :] = a_h[u]
            yield
        dv_in = _dot_tn(ad_scr[...].astype(BF16), dob)
        dq = dqt_acc[...] * c["e_q"] + dqd_acc[...]
        dk = dkh_acc[...] * c["e_k"] + dkd_acc[...]
        rr, cc = _iota2((BLOCK, BLOCK), 0), _iota2((BLOCK, BLOCK), 1)
        db = q * dq - k * dk + jnp.where(_iota2((BLOCK, HEAD), 0) == BLOCK - 1, dbl, 0.0)
        yield
        dg = _tri_left(_tri(cc >= rr), db)
        yield
        real = jnp.logical_or(n > 0, _iota2((BLOCK, HEAD), 0) >= N_PAD)
        df = jnp.where(real, dg / c["f"] - dk, 0.0)
        dq_ref[...] = dq.astype(BF16)
        dv_ref[...] = (dv_acc[...] + dv_in).astype(BF16)
        dfz_ref[...] = (df * (1.0 - lb) * c["sig"] * c["nsig"]).astype(BF16)
        dlb_ref[...] += jnp.sum(df * c["nsig"], axis=0, keepdims=True)

    def body(*refs):
        q_ref, fz_ref, v_ref, lb_ref, st_ref, do_ref = refs[:6]
        dq_ref, dfz_ref, dv_ref, dlb_ref = refs[6 + nr:10 + nr]
        scratch = refs[10 + 2 * nr:19 + 2 * nr]
        finish_ride = _ride_along(refs[6:6 + nr], refs[10 + nr:10 + 2 * nr], ride_modes, refs[19 + 2 * nr:], grid)
        step = pl.program_id(2)

        @pl.when(step == 0)
        def _():
            scratch[0][...] = jnp.zeros_like(scratch[0])
            dlb_ref[...] = jnp.zeros_like(dlb_ref)

        def head(hh):
            cols = pl.ds(hh * HEAD, HEAD)
            return chunk(n_blocks - 1 - step, q_ref.at[:, cols], fz_ref.at[:, cols], v_ref.at[:, cols],
                         lb_ref[:, cols], st_ref.at[hh], do_ref.at[:, cols], dq_ref.at[:, cols],
                         dfz_ref.at[:, cols], dv_ref.at[:, cols], dlb_ref.at[:, cols],
                         *[scr.at[hh] for scr in scratch])

        _lockstep([head(hh) for hh in range(HEADS_PER_STEP)])
        finish_ride()

    def blk(col0):
        return pl.BlockSpec((BLOCK, wide), lambda b, h, s: (b * n_blocks + n_blocks - 1 - s, col0 + h))

    any_spec = pl.BlockSpec(memory_space=pl.ANY)
    per_head = pltpu.VMEM((HEADS_PER_STEP, BLOCK, HEAD), F32)
    return _pcall(
        body, name=name, grid=grid,
        in_specs=[blk(0), blk(n_pairs), blk(2 * n_pairs), pl.BlockSpec((1, wide), lambda b, h, s: (0, h)),
                  pl.BlockSpec((None, HEADS_PER_STEP, HEAD, HEAD),
                               lambda b, h, s: (b * n_blocks + n_blocks - 1 - s, h, 0, 0)),
                  blk(0)] + [any_spec] * nr,
        out_specs=(blk(0), blk(0), blk(0), pl.BlockSpec((None, 1, wide), lambda b, h, s: (b, 0, h)))
        + (any_spec,) * nr,
        out_shape=(jax.ShapeDtypeStruct((t, d), BF16),) * 3 + (jax.ShapeDtypeStruct((bsz, 1, d), F32),)
        + _exchange_shapes(ride, ride_modes),
        scratch_shapes=[per_head] * 9 + _exchange_sems(nr),
        compiler_params=_params("arbitrary", "arbitrary", "arbitrary"),
    )(main, main, main, lbrow, states, do, *ride)


SB_GROUP = 3
SB_ROWS = SB_GROUP * BLOCK
SB_DEAD = -104.0


def _sb_tables():
    j = jnp.bitwise_and(_iota2((2 * BLOCK, 2 * BLOCK), 0), BLOCK - 1)
    s = _iota2((2 * BLOCK, 2 * BLOCK), 1)
    ones = s >= BLOCK
    return (_tri(jnp.logical_or(ones, j > s)), _tri(jnp.logical_or(ones, j <= s)),
            _tri(jnp.logical_or(ones, j < s)))


def _sums(x, table):
    hi = x.astype(BF16)
    lo = (x - hi.astype(F32)).astype(BF16)
    r = _dot(jnp.concatenate([hi, lo], axis=1), table)
    return r[:, :BLOCK], r[:, BLOCK:]


def _sb_logits(q, ks, scale, causal, pad_row):
    z = _dot_nt(q, ks) * scale
    log_keep = -(jnp.maximum(z, 0.0) + jnp.log(1.0 + jnp.exp(-jnp.abs(z))))
    log_beta = z + log_keep
    if causal is not None:
        log_keep = jnp.where(causal, log_keep, 0.0)
    if pad_row is not None:
        log_keep = log_keep * pad_row
    return z, log_beta, log_keep


def _sb_fwd(qkv, bsz, n_blocks, name):
    t, d3 = qkv.shape
    d = d3 // 3
    n_heads = d // HEAD
    lp = n_blocks * BLOCK
    n_groups = n_blocks // SB_GROUP
    assert n_groups * SB_GROUP == n_blocks
    scale = HEAD ** -0.5

    def body(q_ref, k_ref, v_ref, upper_ref, o_ref, tot_ref, stop_ref, c_scr):
        g = pl.program_id(2)
        upper = upper_ref[...]
        causal = _iota2((BLOCK, BLOCK), 1) < _iota2((BLOCK, BLOCK), 0)
        lane = _iota2((1, BLOCK), 1)
        o_ref[...] = jnp.zeros_like(o_ref)
        c_scr[...] = jnp.zeros_like(c_scr)

        def tiles(jobs):
            qrows = [pl.ds(r * BLOCK, BLOCK) for r, _, _, _ in jobs]
            krows = [pl.ds(pl.multiple_of(m * BLOCK, BLOCK), BLOCK) for _, m, _, _ in jobs]
            lg = [_sb_logits(q_ref[qr, :], k_ref[kr, :], scale, causal if dg else None, pad_row)
                  for qr, kr, (_, _, dg, pad_row) in zip(qrows, krows, jobs)]
            sm = [_sums(x[2], upper) for x in lg]
            a_all = []
            for qr, (_, _, dg, _), (_, log_beta, _), (inner, total) in zip(qrows, jobs, lg, sm):
                c = c_scr[qr, :]
                a = jnp.exp(log_beta + c + inner)
                a_all.append(jnp.where(causal, a, 0.0) if dg else a)
                c_scr[qr, :] = c + total
            out = [_dot(a.astype(BF16), v_ref[kr, :]) for a, kr in zip(a_all, krows)]
            for qr, o_part in zip(qrows, out):
                o_ref[qr, :] += o_part

        def pad_row_of(m):
            return jnp.where(jnp.logical_or(m > 0, lane >= N_PAD), 1.0, 0.0)

        base = SB_GROUP * g
        tiles([(r, base + kb, r == kb, pad_row_of(base) if kb == 0 else None)
               for kb in range(SB_GROUP - 1, -1, -1) for r in range(kb, SB_GROUP)])

        def live(least):
            def cond(carry):
                m, c_max = carry
                return jnp.logical_and(m >= least, c_max >= SB_DEAD)
            return cond

        def step_two(carry):
            m, _ = carry
            pad_row = pad_row_of(m - 1)
            tiles([(r, m, False, None) for r in range(SB_GROUP)]
                  + [(r, m - 1, False, pad_row) for r in range(SB_GROUP)])
            return m - 2, jnp.max(c_scr[...])

        def step_one(carry):
            m, _ = carry
            pad_row = pad_row_of(m)
            tiles([(r, m, False, pad_row) for r in range(SB_GROUP)])
            return m - 1, jnp.max(c_scr[...])

        carry = lax.while_loop(live(1), step_two, (base - 1, jnp.max(c_scr[...])))
        m_end, _ = lax.while_loop(live(0), step_one, carry)
        tot_ref[...] = c_scr[...]
        stop_ref[...] = jnp.broadcast_to((m_end + 1).astype(F32), stop_ref.shape)

    qblk = pl.BlockSpec((SB_ROWS, HEAD), lambda b, h, g: (b * n_groups + g, h))
    table = pl.BlockSpec((2 * BLOCK, 2 * BLOCK), lambda b, h, g: (0, 0))
    return _pcall(
        body, name=name, grid=(bsz, n_heads, n_groups),
        in_specs=[qblk, pl.BlockSpec((lp, HEAD), lambda b, h, g: (b, n_heads + h)),
                  pl.BlockSpec((lp, HEAD), lambda b, h, g: (b, 2 * n_heads + h)), table],
        out_specs=(qblk, qblk, pl.BlockSpec((None, 8, 128), lambda b, h, g: ((b * n_heads + h) * n_groups + g, 0, 0))),
        out_shape=(jax.ShapeDtypeStruct((t, d), F32), jax.ShapeDtypeStruct((t, d), F32),
                   jax.ShapeDtypeStruct((bsz * n_heads * n_groups, 8, 128), F32)),
        scratch_shapes=[pltpu.VMEM((SB_ROWS, HEAD), F32)],
        compiler_params=_params("parallel", "parallel", "arbitrary"),
    )(qkv, qkv, qkv, _sb_tables()[0])


def _sb_bwd(qkv, do, tot, stop, bsz, n_blocks, name):
    t, d3 = qkv.shape
    d = d3 // 3
    n_heads = d // HEAD
    lp = n_blocks * BLOCK
    n_groups = n_blocks // SB_GROUP
    scale = HEAD ** -0.5

    def body(q_ref, k_ref, v_ref, do_ref, tot_ref, stop_ref, incl_ref, excl_ref, dq_ref, dk_ref, dv_ref,
             dk_acc, dv_acc, dq_acc, p_scr, e_scr, dob_scr):
        g = pl.program_id(2)

        @pl.when(g == 0)
        def _():
            dk_acc[...] = jnp.zeros_like(dk_acc)
            dv_acc[...] = jnp.zeros_like(dv_acc)

        incl, excl = incl_ref[...], excl_ref[...]
        causal = _iota2((BLOCK, BLOCK), 1) < _iota2((BLOCK, BLOCK), 0)
        lane = _iota2((1, BLOCK), 1)
        dob_scr[...] = do_ref[...].astype(BF16)
        dq_acc[...] = jnp.zeros_like(dq_acc)
        p_scr[...] = jnp.zeros_like(p_scr)
        e_scr[...] = jnp.zeros_like(e_scr)
        first = jnp.clip(jnp.max(stop_ref[...]).astype(jnp.int32), 0, SB_GROUP * g)

        def tiles(jobs):
            n_jobs = range(len(jobs))
            qrows = [pl.ds(r * BLOCK, BLOCK) for r, _, _, _ in jobs]
            krows = [pl.ds(pl.multiple_of(m * BLOCK, BLOCK), BLOCK) for _, m, _, _ in jobs]
            diag = [dg for _, _, dg, _ in jobs]
            lg = [_sb_logits(q_ref[qrows[i], :], k_ref[krows[i], :], scale, causal if diag[i] else None, jobs[i][3])
                  for i in n_jobs]
            d_a = [_dot_nt(dob_scr[qrows[i], :], v_ref[krows[i], :]) for i in n_jobs]
            sm = [_sums(lg[i][2], incl) for i in n_jobs]
            a_all = []
            for i in n_jobs:
                p = p_scr[qrows[i], :]
                a = jnp.exp(lg[i][1] + (tot_ref[qrows[i], :] - p - sm[i][0]))
                a_all.append(jnp.where(causal, a, 0.0) if diag[i] else a)
                p_scr[qrows[i], :] = p + sm[i][1]
            gr = [d_a[i] * a_all[i] for i in n_jobs]
            dv_part = [_dot_tn(a_all[i].astype(BF16), dob_scr[qrows[i], :]) for i in n_jobs]
            gs = [_sums(gr[i], excl) for i in n_jobs]
            dz_all = []
            for i in n_jobs:
                e = e_scr[qrows[i], :]
                dz = gr[i] - (gr[i] + e + gs[i][0]) * jnp.exp(lg[i][1])
                if diag[i]:
                    dz = jnp.where(causal, dz, 0.0)
                dz_all.append((dz * scale).astype(BF16))
                e_scr[qrows[i], :] = e + gs[i][1]
            dk_part = [_dot_tn(dz_all[i], q_ref[qrows[i], :]) for i in n_jobs]
            dq_part = [_dot(dz_all[i], k_ref[krows[i], :]) for i in n_jobs]
            for i in n_jobs:
                dv_acc[krows[i], :] += dv_part[i]
                dk_acc[krows[i], :] += dk_part[i]
                dq_acc[qrows[i], :] += dq_part[i]

        def pad_row_of(m):
            return jnp.where(jnp.logical_or(m > 0, lane >= N_PAD), 1.0, 0.0)

        def step_two(carry):
            m = carry
            pad_row = pad_row_of(m)
            tiles([(r, m, False, pad_row) for r in range(SB_GROUP)]
                  + [(r, m + 1, False, None) for r in range(SB_GROUP)])
            return m + 2

        def step_one(carry):
            m = carry
            pad_row = pad_row_of(m)
            tiles([(r, m, False, pad_row) for r in range(SB_GROUP)])
            return m + 1

        base = SB_GROUP * g
        m = lax.while_loop(lambda m: m + 1 < base, step_two, first)
        lax.while_loop(lambda m: m < base, step_one, m)
        tiles([(r, base + kb, r == kb, pad_row_of(base) if kb == 0 else None)
               for kb in range(SB_GROUP) for r in range(kb, SB_GROUP)])
        dq_ref[...] = dq_acc[...].astype(BF16)

        @pl.when(g == n_groups - 1)
        def _():
            dk_ref[...] = dk_acc[...].astype(BF16)
            dv_ref[...] = dv_acc[...].astype(BF16)

    qblk = pl.BlockSpec((SB_ROWS, HEAD), lambda b, h, g: (b * n_groups + g, h))
    kblk = pl.BlockSpec((lp, HEAD), lambda b, h, g: (b, n_heads + h))
    vblk = pl.BlockSpec((lp, HEAD), lambda b, h, g: (b, 2 * n_heads + h))
    hblk = pl.BlockSpec((lp, HEAD), lambda b, h, g: (b, h))
    sblk = pl.BlockSpec((None, 8, 128), lambda b, h, g: ((b * n_heads + h) * n_groups + g, 0, 0))
    table = pl.BlockSpec((2 * BLOCK, 2 * BLOCK), lambda b, h, g: (0, 0))
    _, incl, excl = _sb_tables()
    return _pcall(
        body, name=name, grid=(bsz, n_heads, n_groups),
        in_specs=[qblk, kblk, vblk, qblk, qblk, sblk, table, table],
        out_specs=(qblk, hblk, hblk),
        out_shape=(jax.ShapeDtypeStruct((t, d), BF16),) * 3,
        scratch_shapes=[pltpu.VMEM((lp, HEAD), F32), pltpu.VMEM((lp, HEAD), F32)]
        + [pltpu.VMEM((SB_ROWS, HEAD), F32)] * 3 + [pltpu.VMEM((SB_ROWS, HEAD), BF16)],
        compiler_params=_params("parallel", "parallel", "arbitrary"),
    )(qkv, qkv, qkv, do, tot, stop, incl, excl)


def _adamw(w, g, m, v):
    m = ADAM_B1 * m + (1.0 - ADAM_B1) * g
    v = ADAM_B2 * v + (1.0 - ADAM_B2) * (g * g)
    m_hat = m / (1.0 - ADAM_B1 ** ADAM_STEP)
    v_hat = v / (1.0 - ADAM_B2 ** ADAM_STEP)
    delta = -ADAM_LR * (m_hat / (jnp.sqrt(v_hat) + ADAM_EPS) + ADAM_WD * w)
    return delta, m, v


def _update_sharded(w, parts, m, v, name):
    r, c = w.shape
    tr = min(r, ROW_TILE)

    def body(w_ref, p_ref, m_ref, v_ref, g_ref, d_ref, nm_ref, nv_ref):
        g = p_ref[0].astype(F32)
        for q in range(1, N_DEV):
            g = g + p_ref[q].astype(F32)
        g_ref[...] = g
        d_ref[...], nm_ref[...], nv_ref[...] = _adamw(w_ref[...], g, m_ref[...], v_ref[...])

    row = pl.BlockSpec((tr, c), lambda i: (i, 0))
    return _pcall(
        body, name=name, grid=(r // tr,),
        in_specs=[row, pl.BlockSpec((N_DEV, tr, c), lambda i: (0, i, 0)), row, row],
        out_specs=(row,) * 4, out_shape=(jax.ShapeDtypeStruct((r, c), F32),) * 4,
        compiler_params=_params("parallel"),
    )(w, parts, m, v)


SMALL_ROWS = 8


def _pack_small(dpre0, dpre1, dpost0, dpost1, dlb, dwon, loss, name):
    d = dpre0.shape[1]
    bsz = dlb.shape[0]

    def body(a0, a1, p0, p1, lb_ref, on_ref, loss_ref, out_ref):
        out_ref[...] = jnp.zeros_like(out_ref)
        out_ref[pl.ds(0, 1), :] = a0[...]
        out_ref[pl.ds(1, 1), :] = a1[...]
        out_ref[pl.ds(2, 1), :] = p0[...]
        out_ref[pl.ds(3, 1), :] = p1[...]
        acc = lb_ref[0]
        for b in range(1, bsz):
            acc = acc + lb_ref[b]
        out_ref[pl.ds(4, 1), :] = acc
        out_ref[pl.ds(5, 1), pl.ds(0, HEAD)] = on_ref[...]
        out_ref[pl.ds(6, 1), pl.ds(0, HEAD)] = loss_ref[pl.ds(0, 1), :]

    return _pcall(body, name=name, out_shape=jax.ShapeDtypeStruct((SMALL_ROWS, d), F32))(
        dpre0, dpre1, dpost0, dpost1, dlb, dwon, loss)


def _update_small(parts, pre, post, lbw, on, moments, name):
    d = pre.shape[1]

    def body(p_ref, pre_ref, post_ref, lbw_ref, on_ref, mpre, mpost, mlb, mon, vpre, vpost, vlb, von,
             loss_ref, *outs):
        def total(r0, nr, width):
            acc = p_ref[0, pl.ds(r0, nr), pl.ds(0, width)]
            for q in range(1, N_DEV):
                acc = acc + p_ref[q, pl.ds(r0, nr), pl.ds(0, width)]
            return acc

        def put(k, w, g, m, v):
            dl, nm, nv = _adamw(w, g, m, v)
            outs[4 * k][...] = g
            outs[4 * k + 1][...] = dl
            outs[4 * k + 2][...] = nm
            outs[4 * k + 3][...] = nv

        put(0, pre_ref[...], total(0, 2, d), mpre[...], vpre[...])
        put(1, post_ref[...], total(2, 2, d), mpost[...], vpost[...])
        a0, a1 = lbw_ref[pl.ds(0, 1), :], lbw_ref[pl.ds(1, 1), :]
        mx = jnp.maximum(a0, a1)
        e0, e1 = jnp.exp(a0 - mx), jnp.exp(a1 - mx)
        p0 = e0 / (e0 + e1)
        g0 = total(4, 1, d) * p0 * (1.0 - p0)
        for r, w, g in ((0, a0, g0), (1, a1, -g0)):
            row = pl.ds(r, 1)
            dl, nm, nv = _adamw(w, g, mlb[row, :], vlb[row, :])
            outs[8][row, :] = g
            outs[9][row, :] = dl
            outs[10][row, :] = nm
            outs[11][row, :] = nv
        put(3, on_ref[...], total(5, 1, HEAD), mon[...], von[...])
        loss_ref[...] = jnp.broadcast_to(total(6, 1, HEAD), loss_ref.shape)

    shapes = []
    for w in (pre, post, lbw, on):
        shapes += [jax.ShapeDtypeStruct(w.shape, F32)] * 4
    return _pcall(body, name=name, out_shape=(jax.ShapeDtypeStruct((8, 128), F32), *shapes))(
        parts, pre, post, lbw, on, *moments)


def kernel(x, meta_tokens, pre_norm, post_norm, hgrn_w_in, hgrn_lb, hgrn_out_norm, hgrn_w_out, sb_w_in, sb_w_out, loss_target, m_meta_tokens, m_pre_norm, m_post_norm, m_hgrn_w_in, m_hgrn_lb, m_hgrn_out_norm, m_hgrn_w_out, m_sb_w_in, m_sb_w_out, v_meta_tokens, v_pre_norm, v_post_norm, v_hgrn_w_in, v_hgrn_lb, v_hgrn_out_norm, v_hgrn_w_out, v_sb_w_in, v_sb_w_out):
    bsz, seq, d = x.shape
    n_blocks = seq // BLOCK + 1
    lp = n_blocks * BLOCK
    t = bsz * lp
    s = hgrn_w_in.shape[2]
    dsh = d // N_DEV

    w_in_h, w_out_h, meta_all = _exchange(
        [hgrn_w_in[0].astype(BF16), hgrn_w_out[0].astype(BF16), meta_tokens], ["gather"] * 3, "gather_weights")
    w_out_h = w_out_h.reshape(d, d)
    meta_full = jnp.transpose(meta_all, (1, 0, 2)).reshape(N_META, d)

    h0 = jnp.concatenate(
        [jnp.zeros((bsz, N_PAD, d), F32), jnp.broadcast_to(meta_full[None], (bsz, N_META, d)), x], axis=1
    ).reshape(t, d)
    lbrow = jnp.cumsum(jax.nn.softmax(hgrn_lb, axis=0), axis=0)[0:1]

    main0, gate0, yn0 = _norm_inproj(h0, pre_norm[0:1], w_in_h, F32, "inproj_hgrn")
    o0, states, w_in_s, w_out_s = _hgrn_fwd(
        main0, lbrow, bsz, n_blocks, [sb_w_in[0].astype(BF16), sb_w_out[0].astype(BF16)], ["gather"] * 2, "hgrn_fwd")
    w_out_s = w_out_s.reshape(d, d)
    h1, y0 = _mix_out(o0, gate0, h0, w_out_h, hgrn_out_norm, post_norm[0:1], True, "mix_out_hgrn")
    main1, gate1, yn1 = _norm_inproj(h1, pre_norm[1:2], w_in_s, BF16, "inproj_sb")
    o1, tot, stop = _sb_fwd(main1, bsz, n_blocks, "sb_fwd")
    h2, y1 = _mix_out(o1, gate1, h1, w_out_s, hgrn_out_norm, post_norm[1:2], False, "mix_out_sb")
    dh2, loss_part = _loss_grad(h2, loss_target, n_blocks, "loss_grad")

    do1, dgate1, dw_out_s, dpost1, _ = _mix_out_bwd(
        dh2, y1, o1, gate1, w_out_s, hgrn_out_norm, post_norm[1:2], False, "mix_out_sb_bwd")
    dq1, dk1, dv1 = _sb_bwd(main1, do1, tot, stop, bsz, n_blocks, "sb_bwd")
    dproj1 = (dq1, dk1, dv1, dgate1)
    dh1, dpre1 = _inproj_bwd_x(dproj1, w_in_s, h1, pre_norm[1:2], dh2, "inproj_sb_bwd_x")
    dw_in_s = _inproj_bwd_w(yn1, dproj1, s, "inproj_sb_bwd_w")

    do0, dgate0, dw_out_h, dpost0, dwon = _mix_out_bwd(
        dh1, y0, o0, gate0, w_out_h, hgrn_out_norm, post_norm[0:1], True, "mix_out_hgrn_bwd")
    dq0, dfz0, dv0, dlb, p_in_s, p_out_s = _hgrn_bwd(
        main0, lbrow, states, do0, bsz, n_blocks, [dw_in_s, dw_out_s.reshape(N_DEV, dsh, d)], ["scatter"] * 2,
        "hgrn_bwd")
    dproj0 = (dq0, dfz0, dv0, dgate0)
    dh0, dpre0 = _inproj_bwd_x(dproj0, w_in_h, h0, pre_norm[0:1], dh1, "inproj_hgrn_bwd_x")
    dw_in_h = _inproj_bwd_w(yn0, dproj0, s, "inproj_hgrn_bwd_w")

    dh0 = dh0.reshape(bsz, lp, d)
    grad_x = dh0[:, BLOCK:]
    dmeta = jnp.sum(dh0[:, N_PAD:BLOCK], axis=0)
    dmeta = jnp.transpose(dmeta.reshape(N_META, N_DEV, dsh), (1, 0, 2))
    small = _pack_small(dpre0, dpre1, dpost0, dpost1, dlb, dwon, loss_part, "pack_small")

    p_in_h, p_out_h, p_meta, p_small = _exchange(
        [dw_in_h, dw_out_h.reshape(N_DEV, dsh, d), dmeta, small], ["scatter"] * 3 + ["gather"], "exchange_grads")

    u_meta = _update_sharded(meta_tokens, p_meta, m_meta_tokens, v_meta_tokens, "update_meta")
    u_in_h = _update_sharded(hgrn_w_in[0], p_in_h, m_hgrn_w_in[0], v_hgrn_w_in[0], "update_hgrn_w_in")
    u_out_h = _update_sharded(hgrn_w_out[0], p_out_h, m_hgrn_w_out[0], v_hgrn_w_out[0], "update_hgrn_w_out")
    u_in_s = _update_sharded(sb_w_in[0], p_in_s, m_sb_w_in[0], v_sb_w_in[0], "update_sb_w_in")
    u_out_s = _update_sharded(sb_w_out[0], p_out_s, m_sb_w_out[0], v_sb_w_out[0], "update_sb_w_out")
    sm = _update_small(p_small, pre_norm, post_norm, hgrn_lb, hgrn_out_norm,
                       (m_pre_norm, m_post_norm, m_hgrn_lb, m_hgrn_out_norm,
                        v_pre_norm, v_post_norm, v_hgrn_lb, v_hgrn_out_norm), "update_small")
    loss = sm[0][0, 0]
    u_pre, u_post, u_lb, u_on = sm[1:5], sm[5:9], sm[9:13], sm[13:17]

    per_w = [u_meta, u_pre, u_post, tuple(a[None] for a in u_in_h), u_lb, u_on,
             tuple(a[None] for a in u_out_h), tuple(a[None] for a in u_in_s), tuple(a[None] for a in u_out_s)]
    outs = [loss, grad_x]
    for k in range(4):
        outs += [u[k] for u in per_w]
    return tuple(outs)
```

```python
import jax
import jax.numpy as jnp
from jax import lax
from jax.experimental import pallas as pl
from jax.experimental.pallas import tpu as pltpu

F32 = jnp.float32
BF16 = jnp.bfloat16

N_DEV = 8
BLOCK = 128
N_META = 16
N_PAD = BLOCK - N_META
HEAD = 128
SUB = 16
N_SUB = BLOCK // SUB
HALF = 8
EPS = 1e-6
ROW_TILE = 256
K_TILE = 768
NEG_BIG = -1e30

ADAM_LR = 0.001
ADAM_B1 = 0.9
ADAM_B2 = 0.999
ADAM_EPS = 1e-08
ADAM_WD = 0.01
ADAM_STEP = 10

VMEM_LIMIT = 56 * 1024 * 1024


def _pcall(body, **kw):
    return pl.pallas_call(body, **kw)


def _params(*sem):
    return pltpu.CompilerParams(dimension_semantics=sem, vmem_limit_bytes=VMEM_LIMIT)


def _dot(a, b):
    return jnp.dot(a, b, preferred_element_type=F32)


def _dot_nt(a, b):
    return lax.dot_general(a, b, (((1,), (1,)), ((), ())), preferred_element_type=F32)


def _dot_tn(a, b):
    return lax.dot_general(a, b, (((0,), (0,)), ((), ())), preferred_element_type=F32)


def _split(x, pieces):
    out = []
    for _ in range(pieces):
        p = x.astype(BF16)
        out.append(p)
        x = x - p.astype(F32)
    return out


def _tri_right(x, tri, pieces=2):
    return sum(_dot(p, tri) for p in _split(x, pieces))


def _tri_left(tri, x, pieces=3):
    return sum(_dot(tri, p) for p in _split(x, pieces))


def _iota2(shape, dim):
    return lax.broadcasted_iota(jnp.int32, shape, dim)


def _tri(cond):
    return jnp.where(cond, 1.0, 0.0).astype(BF16)


def _sig_pair(x):
    e = jnp.exp(-jnp.abs(x))
    r = 1.0 / (1.0 + e)
    er = e * r
    pos = x >= 0
    return jnp.where(pos, r, er), jnp.where(pos, er, r)


def _expand(rows):
    return jnp.concatenate([jnp.broadcast_to(r, (SUB, HEAD)) for r in rows], axis=0)


def _exchange_shapes(arrays, modes):
    return tuple(jax.ShapeDtypeStruct((N_DEV,) + tuple(a.shape[1:] if m == "scatter" else a.shape), a.dtype)
                 for a, m in zip(arrays, modes))


def _exchange_sems(n):
    if n == 0:
        return []
    return [pltpu.SemaphoreType.DMA((n, N_DEV - 1)), pltpu.SemaphoreType.DMA((n, N_DEV - 1)),
            pltpu.SemaphoreType.DMA((n,))]


def _exchange_copies(ins, outs, modes, send_sems, recv_sems, local_sems):
    mx, my, mc = lax.axis_index("x"), lax.axis_index("y"), lax.axis_index("c")
    me = 4 * mx + 2 * my + mc

    def src(i, slot):
        return ins[i].at[slot] if modes[i] == "scatter" else ins[i]

    def peer_of(mask):
        px = 1 - mx if mask & 4 else mx
        py = 1 - my if mask & 2 else my
        pc = 1 - mc if mask & 1 else mc
        return px, py, pc

    def copy(i, mask, dst_slot):
        px, py, pc = peer_of(mask)
        return pltpu.make_async_remote_copy(
            src_ref=src(i, 4 * px + 2 * py + pc), dst_ref=outs[i].at[dst_slot],
            send_sem=send_sems.at[i, mask - 1], recv_sem=recv_sems.at[i, mask - 1],
            device_id=(px, py, pc), device_id_type=pl.DeviceIdType.MESH)

    n = len(ins)
    sends = [copy(i, mask, me) for mask in range(1, N_DEV) for i in range(n)]
    own = [pltpu.make_async_copy(src(i, me), outs[i].at[me], local_sems.at[i]) for i in range(n)]
    arrivals = []
    for mask in range(1, N_DEV):
        px, py, pc = peer_of(mask)
        arrivals += [copy(i, mask, 4 * px + 2 * py + pc) for i in range(n)]

    def start():
        for cp in sends + own:
            cp.start()

    def wait():
        for cp in arrivals:
            cp.wait_recv()
        for cp in sends:
            cp.wait_send()
        for cp in own:
            cp.wait()

    return start, wait


def _exchange(arrays, modes, name):
    n = len(arrays)

    def body(*refs):
        start, wait = _exchange_copies(refs[:n], refs[n:2 * n], modes, *refs[2 * n:])
        start()
        wait()

    any_spec = pl.BlockSpec(memory_space=pl.ANY)
    return _pcall(
        body, name=name, out_shape=_exchange_shapes(arrays, modes),
        in_specs=[any_spec] * n, out_specs=tuple([any_spec] * n),
        scratch_shapes=_exchange_sems(n),
    )(*arrays)


def _norm_inproj(h, wnorm, w_all, main_dtype, name):
    t, d = h.shape
    p_n, _, s = w_all.shape
    n_main = 3 * d // s
    tm = ROW_TILE

    def body(h_ref, wn_ref, w_ref, main_ref, gate_ref, ynt_ref):
        x = h_ref[...]
        y = x * lax.rsqrt(jnp.mean(x * x, axis=-1, keepdims=True) + EPS) * wn_ref[...]
        yb = y.astype(BF16)
        ynt_ref[...] = y.T.astype(BF16)
        for p in range(p_n):
            r = _dot(yb, w_ref[p])
            if p < n_main:
                main_ref[:, p * s:(p + 1) * s] = r.astype(main_dtype)
            else:
                gate_ref[:, (p - n_main) * s:(p - n_main + 1) * s] = r

    return _pcall(
        body, name=name, grid=(t // tm,),
        in_specs=[pl.BlockSpec((tm, d), lambda i: (i, 0)), pl.BlockSpec((1, d), lambda i: (0, 0)),
                  pl.BlockSpec((p_n, d, s), lambda i: (0, 0, 0))],
        out_specs=(pl.BlockSpec((tm, 3 * d), lambda i: (i, 0)), pl.BlockSpec((tm, d), lambda i: (i, 0)),
                   pl.BlockSpec((d, tm), lambda i: (0, i))),
        out_shape=(jax.ShapeDtypeStruct((t, 3 * d), main_dtype), jax.ShapeDtypeStruct((t, d), F32),
                   jax.ShapeDtypeStruct((d, t), BF16)),
        compiler_params=_params("parallel"),
    )(h, wnorm, w_all)


def _mix_out(o, gate, h_in, w_out, w_on, w_post, head_norm, name):
    t, d = o.shape
    n_heads = d // HEAD
    tm = ROW_TILE

    def body(o_ref, g_ref, h_ref, w_ref, won_ref, wp_ref, hout_ref, y_ref, u_scr):
        for hh in range(n_heads):
            cs = slice(hh * HEAD, (hh + 1) * HEAD)
            oh = o_ref[:, cs]
            gt = g_ref[:, cs]
            if head_norm:
                oh = oh * lax.rsqrt(jnp.mean(oh * oh, axis=-1, keepdims=True) + EPS) * won_ref[...]
            u_scr[:, cs] = (oh * (gt * jax.nn.sigmoid(gt))).astype(BF16)
        y = _dot(u_scr[...], w_ref[...])
        y_ref[...] = y
        r = y * lax.rsqrt(jnp.mean(y * y, axis=-1, keepdims=True) + EPS) * wp_ref[...]
        hout_ref[...] = h_ref[...] + r

    row = pl.BlockSpec((tm, d), lambda i: (i, 0))
    return _pcall(
        body, name=name, grid=(t // tm,),
        in_specs=[row, row, row, pl.BlockSpec((d, d), lambda i: (0, 0)),
                  pl.BlockSpec((1, HEAD), lambda i: (0, 0)), pl.BlockSpec((1, d), lambda i: (0, 0))],
        out_specs=(row, row),
        out_shape=(jax.ShapeDtypeStruct((t, d), F32), jax.ShapeDtypeStruct((t, d), F32)),
        scratch_shapes=[pltpu.VMEM((tm, d), BF16)],
        compiler_params=_params("parallel"),
    )(o, gate, h_in, w_out, w_on, w_post)


def _loss_grad(h_out, target, n_blocks, name):
    t, d = h_out.shape

    def body(h_ref, t_ref, dh_ref, loss_ref):
        i = pl.program_id(0)

        @pl.when(i == 0)
        def _():
            loss_ref[...] = jnp.zeros_like(loss_ref)

        real = (i % n_blocks) > 0
        err = jnp.where(real, h_ref[...] - t_ref[...], 0.0)
        dh_ref[...] = err * (1.0 / d)
        part = jnp.sum(jnp.sum(err * err, axis=-1, keepdims=True), axis=0, keepdims=True)
        loss_ref[...] += part * (0.5 / d)

    return _pcall(
        body, name=name, grid=(t // BLOCK,),
        in_specs=[pl.BlockSpec((BLOCK, d), lambda i: (i, 0)),
                  pl.BlockSpec((None, BLOCK, d), lambda i: (i // n_blocks, jnp.maximum(i % n_blocks - 1, 0), 0))],
        out_specs=(pl.BlockSpec((BLOCK, d), lambda i: (i, 0)), pl.BlockSpec((8, 128), lambda i: (0, 0))),
        out_shape=(jax.ShapeDtypeStruct((t, d), F32), jax.ShapeDtypeStruct((8, 128), F32)),
        compiler_params=_params("arbitrary"),
    )(h_out, target)


def _mix_out_bwd(dh, y, o, gate, w_out, w_on, w_post, head_norm, name):
    t, d = o.shape
    n_heads = d // HEAD
    tm = ROW_TILE
    last = t // tm - 1

    def body(dh_ref, y_ref, o_ref, g_ref, w_ref, won_ref, wp_ref,
             do_ref, dg_ref, dw_ref, dwp_ref, dwon_ref, u_scr, acc):
        i = pl.program_id(0)

        @pl.when(i == 0)
        def _():
            acc[...] = jnp.zeros_like(acc)
            dwp_ref[...] = jnp.zeros_like(dwp_ref)
            dwon_ref[...] = jnp.zeros_like(dwon_ref)

        yv = y_ref[...]
        rs = lax.rsqrt(jnp.mean(yv * yv, axis=-1, keepdims=True) + EPS)
        yh = yv * rs
        dr = dh_ref[...]
        dwp_ref[...] += jnp.sum(dr * yh, axis=0, keepdims=True)
        wd = dr * wp_ref[...]
        dy = rs * (wd - yh * jnp.mean(wd * yh, axis=-1, keepdims=True))
        dyb = dy.astype(BF16)
        du = _dot_nt(dyb, w_ref[...])
        for hh in range(n_heads):
            cs = slice(hh * HEAD, (hh + 1) * HEAD)
            oh = o_ref[:, cs]
            gt = g_ref[:, cs]
            sg = jax.nn.sigmoid(gt)
            sl = gt * sg
            duh = du[:, cs]
            if head_norm:
                rsh = lax.rsqrt(jnp.mean(oh * oh, axis=-1, keepdims=True) + EPS)
                ohat = oh * rsh
                on = ohat * won_ref[...]
            else:
                on = oh
            u_scr[:, cs] = (on * sl).astype(BF16)
            don = duh * sl
            dg_ref[:, cs] = (duh * on * (sg * (1.0 + gt * (1.0 - sg)))).astype(BF16)
            if head_norm:
                dwon_ref[...] += jnp.sum(don * ohat, axis=0, keepdims=True)
                wdn = don * won_ref[...]
                do_ref[:, cs] = rsh * (wdn - ohat * jnp.mean(wdn * ohat, axis=-1, keepdims=True))
            else:
                do_ref[:, cs] = don
        acc[...] += _dot_tn(u_scr[...], dyb)

        @pl.when(i == last)
        def _():
            dw_ref[...] = acc[...].astype(BF16)

    row = pl.BlockSpec((tm, d), lambda i: (i, 0))
    return _pcall(
        body, name=name, grid=(t // tm,),
        in_specs=[row, row, row, row, pl.BlockSpec((d, d), lambda i: (0, 0)),
                  pl.BlockSpec((1, HEAD), lambda i: (0, 0)), pl.BlockSpec((1, d), lambda i: (0, 0))],
        out_specs=(row, row, pl.BlockSpec((d, d), lambda i: (0, 0)), pl.BlockSpec((1, d), lambda i: (0, 0)),
                   pl.BlockSpec((1, HEAD), lambda i: (0, 0))),
        out_shape=(jax.ShapeDtypeStruct((t, d), F32), jax.ShapeDtypeStruct((t, d), BF16),
                   jax.ShapeDtypeStruct((d, d), BF16), jax.ShapeDtypeStruct((1, d), F32),
                   jax.ShapeDtypeStruct((1, HEAD), F32)),
        scratch_shapes=[pltpu.VMEM((tm, d), BF16), pltpu.VMEM((d, d), F32)],
        compiler_params=_params("arbitrary"),
    )(dh, y, o, gate, w_out, w_on, w_post)


def _inproj_bwd_x(dparts, w_all, h, wnorm, dres, ride, ride_modes, name):
    t, d = h.shape
    p_n, _, s = w_all.shape
    per = d // s
    tm = ROW_TILE
    nr = len(ride)
    grid = (t // tm,)

    def body(*refs):
        d0_ref, d1_ref, d2_ref, d3_ref, w_ref, h_ref, wn_ref, dres_ref = refs[:8]
        dh_ref, dwn_ref = refs[8 + nr:10 + nr]
        finish_ride = _ride_along(refs[8:8 + nr], refs[10 + nr:10 + 2 * nr], ride_modes, refs[10 + 2 * nr:], grid)
        i = pl.program_id(0)

        @pl.when(i == 0)
        def _():
            dwn_ref[...] = jnp.zeros_like(dwn_ref)

        pieces = (d0_ref, d1_ref, d2_ref, d3_ref)
        dyn = jnp.zeros((tm, d), F32)
        for p in range(p_n):
            blk = pieces[p // per][:, (p % per) * s:(p % per + 1) * s]
            dyn = dyn + _dot_nt(blk, w_ref[p])
        x = h_ref[...]
        rs = lax.rsqrt(jnp.mean(x * x, axis=-1, keepdims=True) + EPS)
        xh = x * rs
        dwn_ref[...] += jnp.sum(dyn * xh, axis=0, keepdims=True)
        wd = dyn * wn_ref[...]
        dh_ref[...] = dres_ref[...] + rs * (wd - xh * jnp.mean(wd * xh, axis=-1, keepdims=True))
        finish_ride()

    row = pl.BlockSpec((tm, d), lambda i: (i, 0))
    any_spec = pl.BlockSpec(memory_space=pl.ANY)
    return _pcall(
        body, name=name, grid=grid,
        in_specs=[row, row, row, row, pl.BlockSpec((p_n, d, s), lambda i: (0, 0, 0)),
                  row, pl.BlockSpec((1, d), lambda i: (0, 0)), row] + [any_spec] * nr,
        out_specs=(row, pl.BlockSpec((1, d), lambda i: (0, 0))) + (any_spec,) * nr,
        out_shape=(jax.ShapeDtypeStruct((t, d), F32), jax.ShapeDtypeStruct((1, d), F32))
        + _exchange_shapes(ride, ride_modes),
        scratch_shapes=_exchange_sems(nr),
        compiler_params=_params("arbitrary"),
    )(*dparts, w_all, h, wnorm, dres, *ride)


def _inproj_bwd_w(ynt, dparts, s, name):
    d, t = ynt.shape
    per = d // s
    n_sh = len(dparts) * per
    tk = K_TILE
    last = t // tk - 1

    def body(ynt_ref, d0_ref, d1_ref, d2_ref, d3_ref, dw_ref, acc):
        p, i = pl.program_id(0), pl.program_id(1)

        @pl.when(i == 0)
        def _():
            acc[...] = jnp.zeros_like(acc)

        for a, piece in enumerate((d0_ref, d1_ref, d2_ref, d3_ref)):
            @pl.when(p // per == a)
            def _():
                acc[...] += _dot(ynt_ref[...], piece[...])

        @pl.when(i == last)
        def _():
            dw_ref[...] = acc[...].astype(BF16)

    def piece_spec(a):
        return pl.BlockSpec((tk, s), lambda p, i: (jnp.where(p // per == a, i, 0),
                                                   jnp.where(p // per == a, p % per, 0)))

    return _pcall(
        body, name=name, grid=(n_sh, t // tk),
        in_specs=[pl.BlockSpec((d, tk), lambda p, i: (0, i))] + [piece_spec(a) for a in range(4)],
        out_specs=pl.BlockSpec((None, d, s), lambda p, i: (p, 0, 0)),
        out_shape=jax.ShapeDtypeStruct((n_sh, d, s), BF16),
        scratch_shapes=[pltpu.VMEM((d, s), F32)],
        compiler_params=_params("parallel", "arbitrary"),
    )(ynt, *dparts)


def _hgrn_common(q_ref, fz_ref, lb, b_scr, k_scr):
    fz = fz_ref[...]
    sig, nsig = _sig_pair(fz)
    f = lb + (1.0 - lb) * sig
    k = (1.0 - lb) * nsig
    rr, cc = _iota2((BLOCK, BLOCK), 0), _iota2((BLOCK, BLOCK), 1)
    b = _tri_left(_tri(cc <= rr), jnp.log(f))
    b_scr[...] = b
    k_scr[...] = k
    bend = [b_scr[pl.ds(SUB * j + SUB - 1, 1), :] for j in range(N_SUB)]
    bref = [jnp.zeros((1, HEAD), F32)] + bend[:-1]
    refrow, bendrow = _expand(bref), _expand(bend)
    e_q = jnp.exp(b - refrow)
    e_k = jnp.exp(bendrow - b)
    qt = q_ref[...] * e_q
    kh = k * e_k
    bl = bend[-1]
    return dict(sig=sig, nsig=nsig, f=f, k=k, b=b, bend=bend, bref=bref, refrow=refrow,
                e_q=e_q, e_k=e_k, qt=qt, kh=kh, bl=bl)


HEADS_PER_STEP = 4


def _lockstep(chunks):
    live = list(chunks)
    while live:
        still = []
        for gen in live:
            try:
                next(gen)
                still.append(gen)
            except StopIteration:
                pass
        live = still


def _ride_along(ride_ins, ride_outs, modes, sems, grid):
    if not ride_ins:
        return lambda: None
    ids = [pl.program_id(a) for a in range(len(grid))]
    start, wait = _exchange_copies(ride_ins, ride_outs, modes, *sems)
    first, last = ids[0] == 0, ids[0] == grid[0] - 1
    for a in range(1, len(grid)):
        first = jnp.logical_and(first, ids[a] == 0)
        last = jnp.logical_and(last, ids[a] == grid[a] - 1)
    pl.when(first)(start)
    return lambda: pl.when(last)(wait)


def _hgrn_fwd(main, lbrow, bsz, n_blocks, ride, ride_modes, name):
    t, d3 = main.shape
    d = d3 // 3
    n_heads = d // HEAD
    n_pairs = n_heads // HEADS_PER_STEP
    wide = HEADS_PER_STEP * HEAD
    nr = len(ride)
    grid = (bsz, n_pairs, n_blocks)

    def chunk(q_ref, fz_ref, v_ref, lb, o_ref, st_ref, s_scr, b_scr, k_scr, o_acc):
        c = _hgrn_common(q_ref, fz_ref, lb, b_scr, k_scr)
        yield
        s_t = s_scr[...]
        st_ref[...] = s_t
        vb = v_ref[...].astype(BF16)
        q_state = c["qt"] * _expand([jnp.exp(r) for r in c["bref"]])
        o_state = _dot_nt(q_state.astype(BF16), s_t.astype(BF16))
        js = range(N_SUB - 1)
        khb = c["kh"].astype(BF16)
        lhs = [(c["qt"][SUB * (j + 1):] * jnp.exp(c["refrow"][SUB * (j + 1):] - c["bend"][j])).astype(BF16)
               for j in js]
        yield
        a_js = [_dot_nt(lhs[j], khb[SUB * j:SUB * (j + 1)]) for j in js]
        k_state = c["kh"] * _expand([jnp.exp(c["bl"] - r) for r in c["bend"]])
        s_scr[...] = s_t * jnp.exp(c["bl"]) + _dot_tn(vb, k_state.astype(BF16))
        yield
        o_js = [_dot(a_js[j].astype(BF16), vb[SUB * j:SUB * (j + 1)]) for j in js]
        yield
        o_acc[...] = o_state
        for j in js:
            o_acc[SUB * (j + 1):, :] += o_js[j]
        t_loc = _iota2((HALF, HEAD), 0)
        for i in range(N_SUB):
            r0 = SUB * i
            q_h = [q_ref[pl.ds(r0 + HALF * u, HALF), :] for u in range(2)]
            b_h = [b_scr[pl.ds(r0 + HALF * u, HALF), :] for u in range(2)]
            o_h = [jnp.zeros((HALF, HEAD), F32) for _ in range(2)]
            for s in range(SUB):
                brow = b_scr[pl.ds(r0 + s, 1), :]
                krow = k_scr[pl.ds(r0 + s, 1), :]
                vrow = v_ref[pl.ds(r0 + s, 1), :]
                for u in range(s // HALF, 2):
                    diff = b_h[u] - brow
                    if u == s // HALF:
                        diff = jnp.where(t_loc >= s - HALF * u, diff, NEG_BIG)
                    col = jnp.sum(q_h[u] * krow * jnp.exp(diff), axis=-1, keepdims=True)
                    o_h[u] = o_h[u] + col * vrow
            for u in range(2):
                o_acc[pl.ds(r0 + HALF * u, HALF), :] += o_h[u]
            yield
        o_ref[...] = o_acc[...]

    def body(*refs):
        q_ref, fz_ref, v_ref, lb_ref = refs[:4]
        o_ref, st_ref = refs[4 + nr:6 + nr]
        s_scr, b_scr, k_scr, o_acc = refs[6 + 2 * nr:10 + 2 * nr]
        finish_ride = _ride_along(refs[4:4 + nr], refs[6 + nr:6 + 2 * nr], ride_modes, refs[10 + 2 * nr:], grid)

        @pl.when(pl.program_id(2) == 0)
        def _():
            s_scr[...] = jnp.zeros_like(s_scr)

        def head(hh):
            cols = pl.ds(hh * HEAD, HEAD)
            return chunk(q_ref.at[:, cols], fz_ref.at[:, cols], v_ref.at[:, cols], lb_ref[:, cols],
                         o_ref.at[:, cols], st_ref.at[hh], s_scr.at[hh], b_scr.at[hh], k_scr.at[hh], o_acc.at[hh])

        _lockstep([head(hh) for hh in range(HEADS_PER_STEP)])
        finish_ride()

    def blk(col0):
        return pl.BlockSpec((BLOCK, wide), lambda b, h, n: (b * n_blocks + n, col0 + h))

    any_spec = pl.BlockSpec(memory_space=pl.ANY)
    per_head = pltpu.VMEM((HEADS_PER_STEP, BLOCK, HEAD), F32)
    return _pcall(
        body, name=name, grid=grid,
        in_specs=[blk(0), blk(n_pairs), blk(2 * n_pairs), pl.BlockSpec((1, wide), lambda b, h, n: (0, h))]
        + [any_spec] * nr,
        out_specs=(blk(0), pl.BlockSpec((None, HEADS_PER_STEP, HEAD, HEAD), lambda b, h, n: (b * n_blocks + n, h, 0, 0)))
        + (any_spec,) * nr,
        out_shape=(jax.ShapeDtypeStruct((t, d), F32),
                   jax.ShapeDtypeStruct((bsz * n_blocks, n_heads, HEAD, HEAD), F32))
        + _exchange_shapes(ride, ride_modes),
        scratch_shapes=[per_head] * 4 + _exchange_sems(nr),
        compiler_params=_params("arbitrary", "arbitrary", "arbitrary"),
    )(main, main, main, lbrow, *ride)


def _hgrn_bwd(main, lbrow, states, do, bsz, n_blocks, ride, ride_modes, name):
    t, d3 = main.shape
    d = d3 // 3
    n_pairs = d // HEAD // HEADS_PER_STEP
    wide = HEADS_PER_STEP * HEAD
    nr = len(ride)
    grid = (bsz, n_pairs, n_blocks)

    def chunk(n, q_ref, fz_ref, v_ref, lb, st_ref, do_ref, dq_ref, dfz_ref, dv_ref, dlb_ref,
              ds_scr, b_scr, k_scr, dqt_acc, dkh_acc, dv_acc, dqd_acc, dkd_acc, ad_scr):
        c = _hgrn_common(q_ref, fz_ref, lb, b_scr, k_scr)
        yield
        q, k = q_ref[...], c["k"]
        vb = v_ref[...].astype(BF16)
        dob = do_ref[...].astype(BF16)
        s0_t = st_ref[...]
        ds1_t = ds_scr[...]
        e_ref = _expand([jnp.exp(r) for r in c["bref"]])
        e_end = _expand([jnp.exp(c["bl"] - r) for r in c["bend"]])
        e_bl = jnp.exp(c["bl"])
        q_state = c["qt"] * e_ref
        k_state = c["kh"] * e_end
        dq_state = _dot(dob, s0_t.astype(BF16))
        dk_state = _dot(vb, ds1_t.astype(BF16))
        dv_state = _dot_nt(k_state.astype(BF16), ds1_t.astype(BF16))
        ds_scr[...] = ds1_t * e_bl + _dot_tn(dob, q_state.astype(BF16))
        js = range(N_SUB - 1)
        lo = [slice(SUB * j, SUB * (j + 1)) for j in js]
        khb = c["kh"].astype(BF16)
        dj = [jnp.exp(c["refrow"][SUB * (j + 1):] - c["bend"][j]) for j in js]
        lhs = [(c["qt"][SUB * (j + 1):] * dj[j]).astype(BF16) for j in js]
        yield
        a_js = [_dot_nt(lhs[j], khb[lo[j]]) for j in js]
        da_js = [_dot_nt(dob[SUB * (j + 1):], vb[lo[j]]).astype(BF16) for j in js]
        dqt_acc[...] = dq_state * e_ref
        dkh_acc[...] = dk_state * e_end
        dv_acc[...] = dv_state
        dbl = (jnp.sum(s0_t * ds1_t, axis=0, keepdims=True) * e_bl
               + jnp.sum(k_state * dk_state, axis=0, keepdims=True))
        yield
        dv_js = [_dot_tn(a_js[j].astype(BF16), dob[SUB * (j + 1):]) for j in js]
        dq_js = [_dot(da_js[j], khb[lo[j]]) * dj[j] for j in js]
        dk_js = [_dot_tn(da_js[j], lhs[j]) for j in js]
        yield
        for j in js:
            dv_acc[lo[j], :] += dv_js[j]
            dqt_acc[SUB * (j + 1):, :] += dq_js[j]
            dkh_acc[lo[j], :] += dk_js[j]
        t_loc = _iota2((HALF, HEAD), 0)
        lane = _iota2((HALF, HEAD), 1)
        for i in range(N_SUB):
            r0 = SUB * i
            q_h = [q_ref[pl.ds(r0 + HALF * u, HALF), :] for u in range(2)]
            b_h = [b_scr[pl.ds(r0 + HALF * u, HALF), :] for u in range(2)]
            do_h = [do_ref[pl.ds(r0 + HALF * u, HALF), :] for u in range(2)]
            zero = jnp.zeros((HALF, HEAD), F32)
            dq_h, dk_h, a_h = [zero, zero], [zero, zero], [zero, zero]
            for s in range(SUB):
                brow = b_scr[pl.ds(r0 + s, 1), :]
                krow = k_scr[pl.ds(r0 + s, 1), :]
                vrow = v_ref[pl.ds(r0 + s, 1), :]
                dk_row = jnp.zeros((1, HEAD), F32)
                for u in range(s // HALF, 2):
                    diff = b_h[u] - brow
                    if u == s // HALF:
                        diff = jnp.where(t_loc >= s - HALF * u, diff, NEG_BIG)
                    w = jnp.exp(diff)
                    qw = q_h[u] * w
                    a_col = jnp.sum(qw * krow, axis=-1, keepdims=True)
                    da_col = jnp.sum(do_h[u] * vrow, axis=-1, keepdims=True)
                    a_h[u] = jnp.where(lane == r0 + s, a_col, a_h[u])
                    dq_h[u] = dq_h[u] + da_col * (w * krow)
                    dk_row = dk_row + jnp.sum(da_col * qw, axis=0, keepdims=True)
                us = s // HALF
                dk_h[us] = jnp.where(t_loc == s - HALF * us, dk_row, dk_h[us])
            for u in range(2):
                rows = pl.ds(r0 + HALF * u, HALF)
                dqd_acc[rows, :] = dq_h[u]
                dkd_acc[rows, :] = dk_h[u]
                ad_scr[rows, :] = a_h[u]
            yield
        dv_in = _dot_tn(ad_scr[...].astype(BF16), dob)
        dq = dqt_acc[...] * c["e_q"] + dqd_acc[...]
        dk = dkh_acc[...] * c["e_k"] + dkd_acc[...]
        rr, cc = _iota2((BLOCK, BLOCK), 0), _iota2((BLOCK, BLOCK), 1)
        db = q * dq - k * dk + jnp.where(_iota2((BLOCK, HEAD), 0) == BLOCK - 1, dbl, 0.0)
        yield
        dg = _tri_left(_tri(cc >= rr), db)
        yield
        real = jnp.logical_or(n > 0, _iota2((BLOCK, HEAD), 0) >= N_PAD)
        df = jnp.where(real, dg / c["f"] - dk, 0.0)
        dq_ref[...] = dq.astype(BF16)
        dv_ref[...] = (dv_acc[...] + dv_in).astype(BF16)
        dfz_ref[...] = (df * (1.0 - lb) * c["sig"] * c["nsig"]).astype(BF16)
        dlb_ref[...] += jnp.sum(df * c["nsig"], axis=0, keepdims=True)

    def body(*refs):
        q_ref, fz_ref, v_ref, lb_ref, st_ref, do_ref = refs[:6]
        dq_ref, dfz_ref, dv_ref, dlb_ref = refs[6 + nr:10 + nr]
        scratch = refs[10 + 2 * nr:19 + 2 * nr]
        finish_ride = _ride_along(refs[6:6 + nr], refs[10 + nr:10 + 2 * nr], ride_modes, refs[19 + 2 * nr:], grid)
        step = pl.program_id(2)

        @pl.when(step == 0)
        def _():
            scratch[0][...] = jnp.zeros_like(scratch[0])
            dlb_ref[...] = jnp.zeros_like(dlb_ref)

        def head(hh):
            cols = pl.ds(hh * HEAD, HEAD)
            return chunk(n_blocks - 1 - step, q_ref.at[:, cols], fz_ref.at[:, cols], v_ref.at[:, cols],
                         lb_ref[:, cols], st_ref.at[hh], do_ref.at[:, cols], dq_ref.at[:, cols],
                         dfz_ref.at[:, cols], dv_ref.at[:, cols], dlb_ref.at[:, cols],
                         *[scr.at[hh] for scr in scratch])

        _lockstep([head(hh) for hh in range(HEADS_PER_STEP)])
        finish_ride()

    def blk(col0):
        return pl.BlockSpec((BLOCK, wide), lambda b, h, s: (b * n_blocks + n_blocks - 1 - s, col0 + h))

    any_spec = pl.BlockSpec(memory_space=pl.ANY)
    per_head = pltpu.VMEM((HEADS_PER_STEP, BLOCK, HEAD), F32)
    return _pcall(
        body, name=name, grid=grid,
        in_specs=[blk(0), blk(n_pairs), blk(2 * n_pairs), pl.BlockSpec((1, wide), lambda b, h, s: (0, h)),
                  pl.BlockSpec((None, HEADS_PER_STEP, HEAD, HEAD),
                               lambda b, h, s: (b * n_blocks + n_blocks - 1 - s, h, 0, 0)),
                  blk(0)] + [any_spec] * nr,
        out_specs=(blk(0), blk(0), blk(0), pl.BlockSpec((None, 1, wide), lambda b, h, s: (b, 0, h)))
        + (any_spec,) * nr,
        out_shape=(jax.ShapeDtypeStruct((t, d), BF16),) * 3 + (jax.ShapeDtypeStruct((bsz, 1, d), F32),)
        + _exchange_shapes(ride, ride_modes),
        scratch_shapes=[per_head] * 9 + _exchange_sems(nr),
        compiler_params=_params("arbitrary", "arbitrary", "arbitrary"),
    )(main, main, main, lbrow, states, do, *ride)


SB_GROUP = 3
SB_ROWS = SB_GROUP * BLOCK
SB_DEAD = -104.0


def _sb_tables():
    j = jnp.bitwise_and(_iota2((2 * BLOCK, 2 * BLOCK), 0), BLOCK - 1)
    s = _iota2((2 * BLOCK, 2 * BLOCK), 1)
    ones = s >= BLOCK
    return (_tri(jnp.logical_or(ones, j > s)), _tri(jnp.logical_or(ones, j <= s)),
            _tri(jnp.logical_or(ones, j < s)))


def _sums(x, table, pieces=2):
    hi = x.astype(BF16)
    if pieces == 1:
        r = _dot(hi, table[:BLOCK])
    else:
        lo = (x - hi.astype(F32)).astype(BF16)
        r = _dot(jnp.concatenate([hi, lo], axis=1), table)
    return r[:, :BLOCK], r[:, BLOCK:]


def _sb_logits(q, ks, scale, causal, pad_row):
    z = _dot_nt(q, ks) * scale
    log_keep = -(jnp.maximum(z, 0.0) + jnp.log(1.0 + jnp.exp(-jnp.abs(z))))
    log_beta = z + log_keep
    if causal is not None:
        log_keep = jnp.where(causal, log_keep, 0.0)
    if pad_row is not None:
        log_keep = log_keep * pad_row
    return z, log_beta, log_keep


def _sb_fwd(qkv, bsz, n_blocks, name):
    t, d3 = qkv.shape
    d = d3 // 3
    n_heads = d // HEAD
    lp = n_blocks * BLOCK
    n_groups = n_blocks // SB_GROUP
    assert n_groups * SB_GROUP == n_blocks
    scale = HEAD ** -0.5

    def body(q_ref, k_ref, v_ref, upper_ref, o_ref, tot_ref, stop_ref, c_scr):
        g = pl.program_id(2)
        upper = upper_ref[...]
        causal = _iota2((BLOCK, BLOCK), 1) < _iota2((BLOCK, BLOCK), 0)
        lane = _iota2((1, BLOCK), 1)
        o_ref[...] = jnp.zeros_like(o_ref)
        c_scr[...] = jnp.zeros_like(c_scr)

        def tiles(jobs):
            qrows = [pl.ds(r * BLOCK, BLOCK) for r, _, _, _ in jobs]
            krows = [pl.ds(pl.multiple_of(m * BLOCK, BLOCK), BLOCK) for _, m, _, _ in jobs]
            lg = [_sb_logits(q_ref[qr, :], k_ref[kr, :], scale, causal if dg else None, pad_row)
                  for qr, kr, (_, _, dg, pad_row) in zip(qrows, krows, jobs)]
            sm = [_sums(x[2], upper) for x in lg]
            a_all = []
            for qr, (_, _, dg, _), (_, log_beta, _), (inner, total) in zip(qrows, jobs, lg, sm):
                c = c_scr[qr, :]
                a = jnp.exp(log_beta + c + inner)
                a_all.append(jnp.where(causal, a, 0.0) if dg else a)
                c_scr[qr, :] = c + total
            out = [_dot(a.astype(BF16), v_ref[kr, :]) for a, kr in zip(a_all, krows)]
            for qr, o_part in zip(qrows, out):
                o_ref[qr, :] += o_part

        def pad_row_of(m):
            return jnp.where(jnp.logical_or(m > 0, lane >= N_PAD), 1.0, 0.0)

        base = SB_GROUP * g
        tiles([(r, base + kb, r == kb, pad_row_of(base) if kb == 0 else None)
               for kb in range(SB_GROUP - 1, -1, -1) for r in range(kb, SB_GROUP)])

        def live(least):
            def cond(carry):
                m, c_max = carry
                return jnp.logical_and(m >= least, c_max >= SB_DEAD)
            return cond

        def step_two(carry):
            m, _ = carry
            pad_row = pad_row_of(m - 1)
            tiles([(r, m, False, None) for r in range(SB_GROUP)]
                  + [(r, m - 1, False, pad_row) for r in range(SB_GROUP)])
            return m - 2, jnp.max(c_scr[...])

        def step_one(carry):
            m, _ = carry
            pad_row = pad_row_of(m)
            tiles([(r, m, False, pad_row) for r in range(SB_GROUP)])
            return m - 1, jnp.max(c_scr[...])

        carry = lax.while_loop(live(1), step_two, (base - 1, jnp.max(c_scr[...])))
        m_end, _ = lax.while_loop(live(0), step_one, carry)
        tot_ref[...] = c_scr[...]
        stop_ref[...] = jnp.broadcast_to((m_end + 1).astype(F32), stop_ref.shape)

    qblk = pl.BlockSpec((SB_ROWS, HEAD), lambda b, h, g: (b * n_groups + g, h))
    table = pl.BlockSpec((2 * BLOCK, 2 * BLOCK), lambda b, h, g: (0, 0))
    return _pcall(
        body, name=name, grid=(bsz, n_heads, n_groups),
        in_specs=[qblk, pl.BlockSpec((lp, HEAD), lambda b, h, g: (b, n_heads + h)),
                  pl.BlockSpec((lp, HEAD), lambda b, h, g: (b, 2 * n_heads + h)), table],
        out_specs=(qblk, qblk, pl.BlockSpec((None, 8, 128), lambda b, h, g: ((b * n_heads + h) * n_groups + g, 0, 0))),
        out_shape=(jax.ShapeDtypeStruct((t, d), F32), jax.ShapeDtypeStruct((t, d), F32),
                   jax.ShapeDtypeStruct((bsz * n_heads * n_groups, 8, 128), F32)),
        scratch_shapes=[pltpu.VMEM((SB_ROWS, HEAD), F32)],
        compiler_params=_params("parallel", "parallel", "arbitrary"),
    )(qkv, qkv, qkv, _sb_tables()[0])


def _sb_bwd(qkv, do, tot, stop, bsz, n_blocks, name):
    t, d3 = qkv.shape
    d = d3 // 3
    n_heads = d // HEAD
    lp = n_blocks * BLOCK
    n_groups = n_blocks // SB_GROUP
    scale = HEAD ** -0.5

    def body(q_ref, k_ref, v_ref, do_ref, tot_ref, stop_ref, incl_ref, excl_ref, dq_ref, dk_ref, dv_ref,
             dk_acc, dv_acc, dq_acc, p_scr, e_scr, dob_scr):
        g = pl.program_id(2)

        @pl.when(g == 0)
        def _():
            dk_acc[...] = jnp.zeros_like(dk_acc)
            dv_acc[...] = jnp.zeros_like(dv_acc)

        incl, excl = incl_ref[...], excl_ref[...]
        causal = _iota2((BLOCK, BLOCK), 1) < _iota2((BLOCK, BLOCK), 0)
        lane = _iota2((1, BLOCK), 1)
        dob_scr[...] = do_ref[...].astype(BF16)
        dq_acc[...] = jnp.zeros_like(dq_acc)
        p_scr[...] = jnp.zeros_like(p_scr)
        e_scr[...] = jnp.zeros_like(e_scr)
        first = jnp.clip(jnp.max(stop_ref[...]).astype(jnp.int32), 0, SB_GROUP * g)

        def tiles(jobs):
            n_jobs = range(len(jobs))
            qrows = [pl.ds(r * BLOCK, BLOCK) for r, _, _, _ in jobs]
            krows = [pl.ds(pl.multiple_of(m * BLOCK, BLOCK), BLOCK) for _, m, _, _ in jobs]
            diag = [dg for _, _, dg, _ in jobs]
            lg = [_sb_logits(q_ref[qrows[i], :], k_ref[krows[i], :], scale, causal if diag[i] else None, jobs[i][3])
                  for i in n_jobs]
            d_a = [_dot_nt(dob_scr[qrows[i], :], v_ref[krows[i], :]) for i in n_jobs]
            sm = [_sums(lg[i][2], incl) for i in n_jobs]
            a_all = []
            for i in n_jobs:
                p = p_scr[qrows[i], :]
                a = jnp.exp(lg[i][1] + (tot_ref[qrows[i], :] - p - sm[i][0]))
                a_all.append(jnp.where(causal, a, 0.0) if diag[i] else a)
                p_scr[qrows[i], :] = p + sm[i][1]
            gr = [d_a[i] * a_all[i] for i in n_jobs]
            dv_part = [_dot_tn(a_all[i].astype(BF16), dob_scr[qrows[i], :]) for i in n_jobs]
            gs = [_sums(gr[i], excl, pieces=1) for i in n_jobs]
            dz_all = []
            for i in n_jobs:
                e = e_scr[qrows[i], :]
                dz = gr[i] - (gr[i] + e + gs[i][0]) * jnp.exp(lg[i][1])
                if diag[i]:
                    dz = jnp.where(causal, dz, 0.0)
                dz_all.append((dz * scale).astype(BF16))
                e_scr[qrows[i], :] = e + gs[i][1]
            dk_part = [_dot_tn(dz_all[i], q_ref[qrows[i], :]) for i in n_jobs]
            dq_part = [_dot(dz_all[i], k_ref[krows[i], :]) for i in n_jobs]
            for i in n_jobs:
                dv_acc[krows[i], :] += dv_part[i]
                dk_acc[krows[i], :] += dk_part[i]
                dq_acc[qrows[i], :] += dq_part[i]

        def pad_row_of(m):
            return jnp.where(jnp.logical_or(m > 0, lane >= N_PAD), 1.0, 0.0)

        def step_two(carry):
            m = carry
            pad_row = pad_row_of(m)
            m_next = m + 1
            tiles([(r, m, False, pad_row) for r in range(SB_GROUP)]
                  + [(r, m_next, False, None) for r in range(SB_GROUP)])
            return m + 2

        def step_one(carry):
            m = carry
            pad_row = pad_row_of(m)
            tiles([(r, m, False, pad_row) for r in range(SB_GROUP)])
            return m + 1

        base = SB_GROUP * g
        m = lax.while_loop(lambda m: m + 1 < base, step_two, first)
        lax.while_loop(lambda m: m < base, step_one, m)
        own = [base + kb for kb in range(SB_GROUP)]
        tiles([(r, own[kb], r == kb, pad_row_of(base) if kb == 0 else None)
               for kb in range(SB_GROUP) for r in range(kb, SB_GROUP)])
        dq_ref[...] = dq_acc[...].astype(BF16)

        @pl.when(g == n_groups - 1)
        def _():
            dk_ref[...] = dk_acc[...].astype(BF16)
            dv_ref[...] = dv_acc[...].astype(BF16)

    qblk = pl.BlockSpec((SB_ROWS, HEAD), lambda b, h, g: (b * n_groups + g, h))
    kblk = pl.BlockSpec((lp, HEAD), lambda b, h, g: (b, n_heads + h))
    vblk = pl.BlockSpec((lp, HEAD), lambda b, h, g: (b, 2 * n_heads + h))
    hblk = pl.BlockSpec((lp, HEAD), lambda b, h, g: (b, h))
    sblk = pl.BlockSpec((None, 8, 128), lambda b, h, g: ((b * n_heads + h) * n_groups + g, 0, 0))
    table = pl.BlockSpec((2 * BLOCK, 2 * BLOCK), lambda b, h, g: (0, 0))
    _, incl, excl = _sb_tables()
    return _pcall(
        body, name=name, grid=(bsz, n_heads, n_groups),
        in_specs=[qblk, kblk, vblk, qblk, qblk, sblk, table, table],
        out_specs=(qblk, hblk, hblk),
        out_shape=(jax.ShapeDtypeStruct((t, d), BF16),) * 3,
        scratch_shapes=[pltpu.VMEM((lp, HEAD), F32), pltpu.VMEM((lp, HEAD), F32)]
        + [pltpu.VMEM((SB_ROWS, HEAD), F32)] * 3 + [pltpu.VMEM((SB_ROWS, HEAD), BF16)],
        compiler_params=_params("parallel", "parallel", "arbitrary"),
    )(qkv, qkv, qkv, do, tot, stop, incl, excl)


def _adamw(w, g, m, v):
    m = ADAM_B1 * m + (1.0 - ADAM_B1) * g
    v = ADAM_B2 * v + (1.0 - ADAM_B2) * (g * g)
    m_hat = m / (1.0 - ADAM_B1 ** ADAM_STEP)
    v_hat = v / (1.0 - ADAM_B2 ** ADAM_STEP)
    delta = -ADAM_LR * (m_hat / (jnp.sqrt(v_hat) + ADAM_EPS) + ADAM_WD * w)
    return delta, m, v


def _update_sharded(w, parts, m, v, name):
    r, c = w.shape
    tr = min(r, ROW_TILE)

    def body(w_ref, p_ref, m_ref, v_ref, g_ref, d_ref, nm_ref, nv_ref):
        g = p_ref[0].astype(F32)
        for q in range(1, N_DEV):
            g = g + p_ref[q].astype(F32)
        g_ref[...] = g
        d_ref[...], nm_ref[...], nv_ref[...] = _adamw(w_ref[...], g, m_ref[...], v_ref[...])

    row = pl.BlockSpec((tr, c), lambda i: (i, 0))
    return _pcall(
        body, name=name, grid=(r // tr,),
        in_specs=[row, pl.BlockSpec((N_DEV, tr, c), lambda i: (0, i, 0)), row, row],
        out_specs=(row,) * 4, out_shape=(jax.ShapeDtypeStruct((r, c), F32),) * 4,
        compiler_params=_params("parallel"),
    )(w, parts, m, v)


SMALL_ROWS = 8


def _pack_small(dpre0, dpre1, dpost0, dpost1, dlb, dwon, loss, name):
    d = dpre0.shape[1]
    bsz = dlb.shape[0]

    def body(a0, a1, p0, p1, lb_ref, on_ref, loss_ref, out_ref):
        out_ref[...] = jnp.zeros_like(out_ref)
        out_ref[pl.ds(0, 1), :] = a0[...]
        out_ref[pl.ds(1, 1), :] = a1[...]
        out_ref[pl.ds(2, 1), :] = p0[...]
        out_ref[pl.ds(3, 1), :] = p1[...]
        acc = lb_ref[0]
        for b in range(1, bsz):
            acc = acc + lb_ref[b]
        out_ref[pl.ds(4, 1), :] = acc
        out_ref[pl.ds(5, 1), pl.ds(0, HEAD)] = on_ref[...]
        out_ref[pl.ds(6, 1), pl.ds(0, HEAD)] = loss_ref[pl.ds(0, 1), :]

    return _pcall(body, name=name, out_shape=jax.ShapeDtypeStruct((SMALL_ROWS, d), F32))(
        dpre0, dpre1, dpost0, dpost1, dlb, dwon, loss)


def _update_small(parts, pre, post, lbw, on, moments, name):
    d = pre.shape[1]

    def body(p_ref, pre_ref, post_ref, lbw_ref, on_ref, mpre, mpost, mlb, mon, vpre, vpost, vlb, von,
             loss_ref, *outs):
        def total(r0, nr, width):
            acc = p_ref[0, pl.ds(r0, nr), pl.ds(0, width)]
            for q in range(1, N_DEV):
                acc = acc + p_ref[q, pl.ds(r0, nr), pl.ds(0, width)]
            return acc

        def put(k, w, g, m, v):
            dl, nm, nv = _adamw(w, g, m, v)
            outs[4 * k][...] = g
            outs[4 * k + 1][...] = dl
            outs[4 * k + 2][...] = nm
            outs[4 * k + 3][...] = nv

        put(0, pre_ref[...], total(0, 2, d), mpre[...], vpre[...])
        put(1, post_ref[...], total(2, 2, d), mpost[...], vpost[...])
        a0, a1 = lbw_ref[pl.ds(0, 1), :], lbw_ref[pl.ds(1, 1), :]
        mx = jnp.maximum(a0, a1)
        e0, e1 = jnp.exp(a0 - mx), jnp.exp(a1 - mx)
        p0 = e0 / (e0 + e1)
        g0 = total(4, 1, d) * p0 * (1.0 - p0)
        for r, w, g in ((0, a0, g0), (1, a1, -g0)):
            row = pl.ds(r, 1)
            dl, nm, nv = _adamw(w, g, mlb[row, :], vlb[row, :])
            outs[8][row, :] = g
            outs[9][row, :] = dl
            outs[10][row, :] = nm
            outs[11][row, :] = nv
        put(3, on_ref[...], total(5, 1, HEAD), mon[...], von[...])
        loss_ref[...] = jnp.broadcast_to(total(6, 1, HEAD), loss_ref.shape)

    shapes = []
    for w in (pre, post, lbw, on):
        shapes += [jax.ShapeDtypeStruct(w.shape, F32)] * 4
    return _pcall(body, name=name, out_shape=(jax.ShapeDtypeStruct((8, 128), F32), *shapes))(
        parts, pre, post, lbw, on, *moments)


def kernel(x, meta_tokens, pre_norm, post_norm, hgrn_w_in, hgrn_lb, hgrn_out_norm, hgrn_w_out, sb_w_in, sb_w_out, loss_target, m_meta_tokens, m_pre_norm, m_post_norm, m_hgrn_w_in, m_hgrn_lb, m_hgrn_out_norm, m_hgrn_w_out, m_sb_w_in, m_sb_w_out, v_meta_tokens, v_pre_norm, v_post_norm, v_hgrn_w_in, v_hgrn_lb, v_hgrn_out_norm, v_hgrn_w_out, v_sb_w_in, v_sb_w_out):
    bsz, seq, d = x.shape
    n_blocks = seq // BLOCK + 1
    lp = n_blocks * BLOCK
    t = bsz * lp
    s = hgrn_w_in.shape[2]
    dsh = d // N_DEV

    w_in_h, meta_all = _exchange([hgrn_w_in[0].astype(BF16), meta_tokens], ["gather"] * 2, "gather_weights")
    meta_full = jnp.transpose(meta_all, (1, 0, 2)).reshape(N_META, d)

    h0 = jnp.concatenate(
        [jnp.zeros((bsz, N_PAD, d), F32), jnp.broadcast_to(meta_full[None], (bsz, N_META, d)), x], axis=1
    ).reshape(t, d)
    lbrow = jnp.cumsum(jax.nn.softmax(hgrn_lb, axis=0), axis=0)[0:1]

    main0, gate0, yn0 = _norm_inproj(h0, pre_norm[0:1], w_in_h, F32, "inproj_hgrn")
    o0, states, w_in_s, w_out_s, w_out_h = _hgrn_fwd(
        main0, lbrow, bsz, n_blocks,
        [sb_w_in[0].astype(BF16), sb_w_out[0].astype(BF16), hgrn_w_out[0].astype(BF16)], ["gather"] * 3, "hgrn_fwd")
    w_out_s = w_out_s.reshape(d, d)
    w_out_h = w_out_h.reshape(d, d)
    h1, y0 = _mix_out(o0, gate0, h0, w_out_h, hgrn_out_norm, post_norm[0:1], True, "mix_out_hgrn")
    main1, gate1, yn1 = _norm_inproj(h1, pre_norm[1:2], w_in_s, BF16, "inproj_sb")
    o1, tot, stop = _sb_fwd(main1, bsz, n_blocks, "sb_fwd")
    h2, y1 = _mix_out(o1, gate1, h1, w_out_s, hgrn_out_norm, post_norm[1:2], False, "mix_out_sb")
    dh2, loss_part = _loss_grad(h2, loss_target, n_blocks, "loss_grad")

    do1, dgate1, dw_out_s, dpost1, _ = _mix_out_bwd(
        dh2, y1, o1, gate1, w_out_s, hgrn_out_norm, post_norm[1:2], False, "mix_out_sb_bwd")
    dq1, dk1, dv1 = _sb_bwd(main1, do1, tot, stop, bsz, n_blocks, "sb_bwd")
    dproj1 = (dq1, dk1, dv1, dgate1)
    dh1, dpre1 = _inproj_bwd_x(dproj1, w_in_s, h1, pre_norm[1:2], dh2, [], [], "inproj_sb_bwd_x")
    dw_in_s = _inproj_bwd_w(yn1, dproj1, s, "inproj_sb_bwd_w")

    do0, dgate0, dw_out_h, dpost0, dwon = _mix_out_bwd(
        dh1, y0, o0, gate0, w_out_h, hgrn_out_norm, post_norm[0:1], True, "mix_out_hgrn_bwd")
    dq0, dfz0, dv0, dlb, p_in_s, p_out_s, p_out_h = _hgrn_bwd(
        main0, lbrow, states, do0, bsz, n_blocks,
        [dw_in_s, dw_out_s.reshape(N_DEV, dsh, d), dw_out_h.reshape(N_DEV, dsh, d)], ["scatter"] * 3, "hgrn_bwd")
    dproj0 = (dq0, dfz0, dv0, dgate0)
    dw_in_h = _inproj_bwd_w(yn0, dproj0, s, "inproj_hgrn_bwd_w")
    dh0, dpre0, p_in_h = _inproj_bwd_x(
        dproj0, w_in_h, h0, pre_norm[0:1], dh1, [dw_in_h], ["scatter"], "inproj_hgrn_bwd_x")

    dh0 = dh0.reshape(bsz, lp, d)
    grad_x = dh0[:, BLOCK:]
    dmeta = jnp.sum(dh0[:, N_PAD:BLOCK], axis=0)
    dmeta = jnp.transpose(dmeta.reshape(N_META, N_DEV, dsh), (1, 0, 2))
    small = _pack_small(dpre0, dpre1, dpost0, dpost1, dlb, dwon, loss_part, "pack_small")

    p_meta, p_small = _exchange([dmeta, small], ["scatter", "gather"], "exchange_grads")

    u_meta = _update_sharded(meta_tokens, p_meta, m_meta_tokens, v_meta_tokens, "update_meta")
    u_in_h = _update_sharded(hgrn_w_in[0], p_in_h, m_hgrn_w_in[0], v_hgrn_w_in[0], "update_hgrn_w_in")
    u_out_h = _update_sharded(hgrn_w_out[0], p_out_h, m_hgrn_w_out[0], v_hgrn_w_out[0], "update_hgrn_w_out")
    u_in_s = _update_sharded(sb_w_in[0], p_in_s, m_sb_w_in[0], v_sb_w_in[0], "update_sb_w_in")
    u_out_s = _update_sharded(sb_w_out[0], p_out_s, m_sb_w_out[0], v_sb_w_out[0], "update_sb_w_out")
    sm = _update_small(p_small, pre_norm, post_norm, hgrn_lb, hgrn_out_norm,
                       (m_pre_norm, m_post_norm, m_hgrn_lb, m_hgrn_out_norm,
                        v_pre_norm, v_post_norm, v_hgrn_lb, v_hgrn_out_norm), "update_small")
    loss = sm[0][0, 0]
    u_pre, u_post, u_lb, u_on = sm[1:5], sm[5:9], sm[9:13], sm[13:17]

    per_w = [u_meta, u_pre, u_post, tuple(a[None] for a in u_in_h), u_lb, u_on,
             tuple(a[None] for a in u_out_h), tuple(a[None] for a in u_in_s), tuple(a[None] for a in u_out_s)]
    outs = [loss, grad_x]
    for k in range(4):
        outs += [u[k] for u in per_w]
    return tuple(outs)
```

```python
import jax
import jax.numpy as jnp
from jax import lax
from jax.experimental import pallas as pl
from jax.experimental.pallas import tpu as pltpu

F32 = jnp.float32
BF16 = jnp.bfloat16

N_DEV = 8
BLOCK = 128
N_META = 16
N_PAD = BLOCK - N_META
HEAD = 128
SUB = 16
N_SUB = BLOCK // SUB
HALF = 8
EPS = 1e-6
ROW_TILE = 256
K_TILES = (1408, 768, 384, 128)
NEG_BIG = -1e30

ADAM_LR = 0.001
ADAM_B1 = 0.9
ADAM_B2 = 0.999
ADAM_EPS = 1e-08
ADAM_WD = 0.01
ADAM_STEP = 10

VMEM_LIMIT = 56 * 1024 * 1024


def _pcall(body, **kw):
    return pl.pallas_call(body, **kw)


def _params(*sem):
    return pltpu.CompilerParams(dimension_semantics=sem, vmem_limit_bytes=VMEM_LIMIT)


def _dot(a, b):
    return jnp.dot(a, b, preferred_element_type=F32)


def _dot_nt(a, b):
    return lax.dot_general(a, b, (((1,), (1,)), ((), ())), preferred_element_type=F32)


def _dot_tn(a, b):
    return lax.dot_general(a, b, (((0,), (0,)), ((), ())), preferred_element_type=F32)


def _split(x, pieces):
    out = []
    for _ in range(pieces):
        p = x.astype(BF16)
        out.append(p)
        x = x - p.astype(F32)
    return out


def _tri_right(x, tri, pieces=2):
    return sum(_dot(p, tri) for p in _split(x, pieces))


def _tri_left(tri, x, pieces=3):
    return sum(_dot(tri, p) for p in _split(x, pieces))


def _iota2(shape, dim):
    return lax.broadcasted_iota(jnp.int32, shape, dim)


def _tri(cond):
    return jnp.where(cond, 1.0, 0.0).astype(BF16)


def _sig_pair(x):
    e = jnp.exp(-jnp.abs(x))
    r = 1.0 / (1.0 + e)
    er = e * r
    pos = x >= 0
    return jnp.where(pos, r, er), jnp.where(pos, er, r)


def _expand(rows):
    return jnp.concatenate([jnp.broadcast_to(r, (SUB, HEAD)) for r in rows], axis=0)


def _exchange_shapes(arrays, modes):
    return tuple(jax.ShapeDtypeStruct((N_DEV,) + tuple(a.shape[1:] if m == "scatter" else a.shape), a.dtype)
                 for a, m in zip(arrays, modes))


def _exchange_sems(n):
    if n == 0:
        return []
    return [pltpu.SemaphoreType.DMA((n, N_DEV - 1)), pltpu.SemaphoreType.DMA((n, N_DEV - 1)),
            pltpu.SemaphoreType.DMA((n,))]


def _exchange_copies(ins, outs, modes, send_sems, recv_sems, local_sems):
    mx, my, mc = lax.axis_index("x"), lax.axis_index("y"), lax.axis_index("c")
    me = 4 * mx + 2 * my + mc

    def src(i, slot):
        return ins[i].at[slot] if modes[i] == "scatter" else ins[i]

    def peer_of(mask):
        px = 1 - mx if mask & 4 else mx
        py = 1 - my if mask & 2 else my
        pc = 1 - mc if mask & 1 else mc
        return px, py, pc

    def copy(i, mask, dst_slot):
        px, py, pc = peer_of(mask)
        return pltpu.make_async_remote_copy(
            src_ref=src(i, 4 * px + 2 * py + pc), dst_ref=outs[i].at[dst_slot],
            send_sem=send_sems.at[i, mask - 1], recv_sem=recv_sems.at[i, mask - 1],
            device_id=(px, py, pc), device_id_type=pl.DeviceIdType.MESH)

    n = len(ins)
    sends = [copy(i, mask, me) for mask in range(1, N_DEV) for i in range(n)]
    own = [pltpu.make_async_copy(src(i, me), outs[i].at[me], local_sems.at[i]) for i in range(n)]
    arrivals = []
    for mask in range(1, N_DEV):
        px, py, pc = peer_of(mask)
        arrivals += [copy(i, mask, 4 * px + 2 * py + pc) for i in range(n)]

    def start():
        for cp in sends + own:
            cp.start()

    def wait():
        for cp in arrivals:
            cp.wait_recv()
        for cp in sends:
            cp.wait_send()
        for cp in own:
            cp.wait()

    return start, wait


def _gather_once_per_chip(arrays, name):
    n = len(arrays)

    def body(*refs):
        ins, outs = refs[:n], refs[n:2 * n]
        send_sems, recv_sems, local_sems = refs[2 * n:]
        mx, my, mc = lax.axis_index("x"), lax.axis_index("y"), lax.axis_index("c")
        me, sibling = (mx, my, mc), (mx, my, 1 - mc)
        chips = [(1 - mx, my), (mx, 1 - my), (1 - mx, 1 - my)]

        def slot(px, py, pc):
            return 4 * px + 2 * py + pc

        def copy(i, k, block, to, src=None):
            return pltpu.make_async_remote_copy(
                src_ref=outs[i].at[slot(*block)] if src is None else src, dst_ref=outs[i].at[slot(*block)],
                send_sem=send_sems.at[i, k], recv_sem=recv_sems.at[i, k],
                device_id=to, device_id_type=pl.DeviceIdType.MESH)

        own = [pltpu.make_async_copy(ins[i], outs[i].at[slot(*me)], local_sems.at[i]) for i in range(n)]
        first = [copy(i, 0, me, sibling, src=ins[i]) for i in range(n)]
        first += [copy(i, 1 + j, me, (*chip, mc), src=ins[i]) for j, chip in enumerate(chips) for i in range(n)]
        for cp in own + first:
            cp.start()
        passed = []
        for j, chip in enumerate(chips):
            for i in range(n):
                copy(i, 1 + j, (*chip, mc), me).wait_recv()
                cp = copy(i, 4 + j, (*chip, mc), sibling)
                cp.start()
                passed.append(cp)
        for i in range(n):
            copy(i, 0, sibling, me).wait_recv()
            for j, chip in enumerate(chips):
                copy(i, 4 + j, (*chip, 1 - mc), me).wait_recv()
        for cp in first + passed:
            cp.wait_send()
        for cp in own:
            cp.wait()

    any_spec = pl.BlockSpec(memory_space=pl.ANY)
    return _pcall(
        body, name=name, out_shape=_exchange_shapes(arrays, ["gather"] * n),
        in_specs=[any_spec] * n, out_specs=tuple([any_spec] * n),
        scratch_shapes=_exchange_sems(n),
    )(*arrays)


def _exchange(arrays, modes, name):
    n = len(arrays)

    def body(*refs):
        start, wait = _exchange_copies(refs[:n], refs[n:2 * n], modes, *refs[2 * n:])
        start()
        wait()

    any_spec = pl.BlockSpec(memory_space=pl.ANY)
    return _pcall(
        body, name=name, out_shape=_exchange_shapes(arrays, modes),
        in_specs=[any_spec] * n, out_specs=tuple([any_spec] * n),
        scratch_shapes=_exchange_sems(n),
    )(*arrays)


def _norm_inproj(h, wnorm, w_all, main_dtype, name):
    t, d = h.shape
    p_n, _, s = w_all.shape
    n_main = 3 * d // s
    tm = ROW_TILE

    def body(h_ref, wn_ref, w_ref, main_ref, gate_ref, ynt_ref):
        x = h_ref[...]
        y = x * lax.rsqrt(jnp.mean(x * x, axis=-1, keepdims=True) + EPS) * wn_ref[...]
        yb = y.astype(BF16)
        ynt_ref[...] = y.T.astype(BF16)
        for p in range(p_n):
            r = _dot(yb, w_ref[p])
            if p < n_main:
                main_ref[:, p * s:(p + 1) * s] = r.astype(main_dtype)
            else:
                gate_ref[:, (p - n_main) * s:(p - n_main + 1) * s] = r

    return _pcall(
        body, name=name, grid=(t // tm,),
        in_specs=[pl.BlockSpec((tm, d), lambda i: (i, 0)), pl.BlockSpec((1, d), lambda i: (0, 0)),
                  pl.BlockSpec((p_n, d, s), lambda i: (0, 0, 0))],
        out_specs=(pl.BlockSpec((tm, 3 * d), lambda i: (i, 0)), pl.BlockSpec((tm, d), lambda i: (i, 0)),
                   pl.BlockSpec((d, tm), lambda i: (0, i))),
        out_shape=(jax.ShapeDtypeStruct((t, 3 * d), main_dtype), jax.ShapeDtypeStruct((t, d), F32),
                   jax.ShapeDtypeStruct((d, t), BF16)),
        compiler_params=_params("parallel"),
    )(h, wnorm, w_all)


def _mix_out(o, gate, h_in, w_out, w_on, w_post, head_norm, name):
    t, d = o.shape
    n_heads = d // HEAD
    tm = ROW_TILE

    def body(o_ref, g_ref, h_ref, w_ref, won_ref, wp_ref, hout_ref, y_ref, u_scr):
        for hh in range(n_heads):
            cs = slice(hh * HEAD, (hh + 1) * HEAD)
            oh = o_ref[:, cs]
            gt = g_ref[:, cs]
            if head_norm:
                oh = oh * lax.rsqrt(jnp.mean(oh * oh, axis=-1, keepdims=True) + EPS) * won_ref[...]
            u_scr[:, cs] = (oh * (gt * jax.nn.sigmoid(gt))).astype(BF16)
        y = _dot(u_scr[...], w_ref[...])
        y_ref[...] = y
        r = y * lax.rsqrt(jnp.mean(y * y, axis=-1, keepdims=True) + EPS) * wp_ref[...]
        hout_ref[...] = h_ref[...] + r

    row = pl.BlockSpec((tm, d), lambda i: (i, 0))
    return _pcall(
        body, name=name, grid=(t // tm,),
        in_specs=[row, row, row, pl.BlockSpec((d, d), lambda i: (0, 0)),
                  pl.BlockSpec((1, HEAD), lambda i: (0, 0)), pl.BlockSpec((1, d), lambda i: (0, 0))],
        out_specs=(row, row),
        out_shape=(jax.ShapeDtypeStruct((t, d), F32), jax.ShapeDtypeStruct((t, d), F32)),
        scratch_shapes=[pltpu.VMEM((tm, d), BF16)],
        compiler_params=_params("parallel"),
    )(o, gate, h_in, w_out, w_on, w_post)


def _loss_grad(h_out, target, n_blocks, name):
    t, d = h_out.shape

    def body(h_ref, t_ref, dh_ref, loss_ref):
        i = pl.program_id(0)

        @pl.when(i == 0)
        def _():
            loss_ref[...] = jnp.zeros_like(loss_ref)

        real = (i % n_blocks) > 0
        err = jnp.where(real, h_ref[...] - t_ref[...], 0.0)
        dh_ref[...] = err * (1.0 / d)
        part = jnp.sum(jnp.sum(err * err, axis=-1, keepdims=True), axis=0, keepdims=True)
        loss_ref[...] += part * (0.5 / d)

    return _pcall(
        body, name=name, grid=(t // BLOCK,),
        in_specs=[pl.BlockSpec((BLOCK, d), lambda i: (i, 0)),
                  pl.BlockSpec((None, BLOCK, d), lambda i: (i // n_blocks, jnp.maximum(i % n_blocks - 1, 0), 0))],
        out_specs=(pl.BlockSpec((BLOCK, d), lambda i: (i, 0)), pl.BlockSpec((8, 128), lambda i: (0, 0))),
        out_shape=(jax.ShapeDtypeStruct((t, d), F32), jax.ShapeDtypeStruct((8, 128), F32)),
        compiler_params=_params("arbitrary"),
    )(h_out, target)


def _mix_out_bwd(dh, y, o, gate, w_out, w_on, w_post, head_norm, name):
    t, d = o.shape
    n_heads = d // HEAD
    tm = ROW_TILE
    last = t // tm - 1

    def body(dh_ref, y_ref, o_ref, g_ref, w_ref, won_ref, wp_ref,
             do_ref, dg_ref, dw_ref, dwp_ref, dwon_ref, u_scr, acc):
        i = pl.program_id(0)

        @pl.when(i == 0)
        def _():
            acc[...] = jnp.zeros_like(acc)
            dwp_ref[...] = jnp.zeros_like(dwp_ref)
            dwon_ref[...] = jnp.zeros_like(dwon_ref)

        yv = y_ref[...]
        rs = lax.rsqrt(jnp.mean(yv * yv, axis=-1, keepdims=True) + EPS)
        yh = yv * rs
        dr = dh_ref[...]
        dwp_ref[...] += jnp.sum(dr * yh, axis=0, keepdims=True)
        wd = dr * wp_ref[...]
        dy = rs * (wd - yh * jnp.mean(wd * yh, axis=-1, keepdims=True))
        dyb = dy.astype(BF16)
        du = _dot_nt(dyb, w_ref[...])
        for hh in range(n_heads):
            cs = slice(hh * HEAD, (hh + 1) * HEAD)
            oh = o_ref[:, cs]
            gt = g_ref[:, cs]
            sg = jax.nn.sigmoid(gt)
            sl = gt * sg
            duh = du[:, cs]
            if head_norm:
                rsh = lax.rsqrt(jnp.mean(oh * oh, axis=-1, keepdims=True) + EPS)
                ohat = oh * rsh
                on = ohat * won_ref[...]
            else:
                on = oh
            u_scr[:, cs] = (on * sl).astype(BF16)
            don = duh * sl
            dg_ref[:, cs] = (duh * on * (sg * (1.0 + gt * (1.0 - sg)))).astype(BF16)
            if head_norm:
                dwon_ref[...] += jnp.sum(don * ohat, axis=0, keepdims=True)
                wdn = don * won_ref[...]
                do_ref[:, cs] = rsh * (wdn - ohat * jnp.mean(wdn * ohat, axis=-1, keepdims=True))
            else:
                do_ref[:, cs] = don
        acc[...] += _dot_tn(u_scr[...], dyb)

        @pl.when(i == last)
        def _():
            dw_ref[...] = acc[...].astype(BF16)

    row = pl.BlockSpec((tm, d), lambda i: (i, 0))
    return _pcall(
        body, name=name, grid=(t // tm,),
        in_specs=[row, row, row, row, pl.BlockSpec((d, d), lambda i: (0, 0)),
                  pl.BlockSpec((1, HEAD), lambda i: (0, 0)), pl.BlockSpec((1, d), lambda i: (0, 0))],
        out_specs=(row, row, pl.BlockSpec((d, d), lambda i: (0, 0)), pl.BlockSpec((1, d), lambda i: (0, 0)),
                   pl.BlockSpec((1, HEAD), lambda i: (0, 0))),
        out_shape=(jax.ShapeDtypeStruct((t, d), F32), jax.ShapeDtypeStruct((t, d), BF16),
                   jax.ShapeDtypeStruct((d, d), BF16), jax.ShapeDtypeStruct((1, d), F32),
                   jax.ShapeDtypeStruct((1, HEAD), F32)),
        scratch_shapes=[pltpu.VMEM((tm, d), BF16), pltpu.VMEM((d, d), F32)],
        compiler_params=_params("arbitrary"),
    )(dh, y, o, gate, w_out, w_on, w_post)


def _inproj_bwd_x(dparts, w_all, h, wnorm, dres, ride, ride_modes, name):
    t, d = h.shape
    p_n, _, s = w_all.shape
    per = d // s
    tm = ROW_TILE
    nr = len(ride)
    grid = (t // tm,)

    def body(*refs):
        d0_ref, d1_ref, d2_ref, d3_ref, w_ref, h_ref, wn_ref, dres_ref = refs[:8]
        dh_ref, dwn_ref = refs[8 + nr:10 + nr]
        finish_ride = _ride_along(refs[8:8 + nr], refs[10 + nr:10 + 2 * nr], ride_modes, refs[10 + 2 * nr:], grid)
        i = pl.program_id(0)

        @pl.when(i == 0)
        def _():
            dwn_ref[...] = jnp.zeros_like(dwn_ref)

        pieces = (d0_ref, d1_ref, d2_ref, d3_ref)
        dyn = jnp.zeros((tm, d), F32)
        for p in range(p_n):
            blk = pieces[p // per][:, (p % per) * s:(p % per + 1) * s]
            dyn = dyn + _dot_nt(blk, w_ref[p])
        x = h_ref[...]
        rs = lax.rsqrt(jnp.mean(x * x, axis=-1, keepdims=True) + EPS)
        xh = x * rs
        dwn_ref[...] += jnp.sum(dyn * xh, axis=0, keepdims=True)
        wd = dyn * wn_ref[...]
        dh_ref[...] = dres_ref[...] + rs * (wd - xh * jnp.mean(wd * xh, axis=-1, keepdims=True))
        finish_ride()

    row = pl.BlockSpec((tm, d), lambda i: (i, 0))
    any_spec = pl.BlockSpec(memory_space=pl.ANY)
    return _pcall(
        body, name=name, grid=grid,
        in_specs=[row, row, row, row, pl.BlockSpec((p_n, d, s), lambda i: (0, 0, 0)),
                  row, pl.BlockSpec((1, d), lambda i: (0, 0)), row] + [any_spec] * nr,
        out_specs=(row, pl.BlockSpec((1, d), lambda i: (0, 0))) + (any_spec,) * nr,
        out_shape=(jax.ShapeDtypeStruct((t, d), F32), jax.ShapeDtypeStruct((1, d), F32))
        + _exchange_shapes(ride, ride_modes),
        scratch_shapes=_exchange_sems(nr),
        compiler_params=_params("arbitrary"),
    )(*dparts, w_all, h, wnorm, dres, *ride)


def _inproj_bwd_w(ynt, dparts, s, name):
    d, t = ynt.shape
    per = d // s
    n_sh = len(dparts) * per
    tk = next(c for c in K_TILES if t % c == 0)
    last = t // tk - 1

    def body(ynt_ref, d0_ref, d1_ref, d2_ref, d3_ref, dw_ref, acc):
        p, i = pl.program_id(0), pl.program_id(1)

        @pl.when(i == 0)
        def _():
            acc[...] = jnp.zeros_like(acc)

        for a, piece in enumerate((d0_ref, d1_ref, d2_ref, d3_ref)):
            @pl.when(p // per == a)
            def _():
                acc[...] += _dot(ynt_ref[...], piece[...])

        @pl.when(i == last)
        def _():
            dw_ref[...] = acc[...].astype(BF16)

    def piece_spec(a):
        return pl.BlockSpec((tk, s), lambda p, i: (jnp.where(p // per == a, i, 0),
                                                   jnp.where(p // per == a, p % per, 0)))

    return _pcall(
        body, name=name, grid=(n_sh, t // tk),
        in_specs=[pl.BlockSpec((d, tk), lambda p, i: (0, i))] + [piece_spec(a) for a in range(4)],
        out_specs=pl.BlockSpec((None, d, s), lambda p, i: (p, 0, 0)),
        out_shape=jax.ShapeDtypeStruct((n_sh, d, s), BF16),
        scratch_shapes=[pltpu.VMEM((d, s), F32)],
        compiler_params=_params("parallel", "arbitrary"),
    )(ynt, *dparts)


def _hgrn_common(q_ref, fz_ref, lb, b_scr, k_scr):
    fz = fz_ref[...]
    sig, nsig = _sig_pair(fz)
    f = lb + (1.0 - lb) * sig
    k = (1.0 - lb) * nsig
    rr, cc = _iota2((BLOCK, BLOCK), 0), _iota2((BLOCK, BLOCK), 1)
    b = _tri_left(_tri(cc <= rr), jnp.log(f))
    b_scr[...] = b
    k_scr[...] = k
    bend = [b_scr[pl.ds(SUB * j + SUB - 1, 1), :] for j in range(N_SUB)]
    bref = [jnp.zeros((1, HEAD), F32)] + bend[:-1]
    refrow, bendrow = _expand(bref), _expand(bend)
    e_q = jnp.exp(b - refrow)
    e_k = jnp.exp(bendrow - b)
    qt = q_ref[...] * e_q
    kh = k * e_k
    bl = bend[-1]
    return dict(sig=sig, nsig=nsig, f=f, k=k, b=b, bend=bend, bref=bref, refrow=refrow,
                e_q=e_q, e_k=e_k, qt=qt, kh=kh, bl=bl)


HEADS_PER_STEP = 4


def _lockstep(chunks):
    live = list(chunks)
    while live:
        still = []
        for gen in live:
            try:
                next(gen)
                still.append(gen)
            except StopIteration:
                pass
        live = still


def _ride_along(ride_ins, ride_outs, modes, sems, grid):
    if not ride_ins:
        return lambda: None
    ids = [pl.program_id(a) for a in range(len(grid))]
    start, wait = _exchange_copies(ride_ins, ride_outs, modes, *sems)
    first, last = ids[0] == 0, ids[0] == grid[0] - 1
    for a in range(1, len(grid)):
        first = jnp.logical_and(first, ids[a] == 0)
        last = jnp.logical_and(last, ids[a] == grid[a] - 1)
    pl.when(first)(start)
    return lambda: pl.when(last)(wait)


def _hgrn_fwd(main, lbrow, bsz, n_blocks, ride, ride_modes, name):
    t, d3 = main.shape
    d = d3 // 3
    n_heads = d // HEAD
    n_pairs = n_heads // HEADS_PER_STEP
    wide = HEADS_PER_STEP * HEAD
    nr = len(ride)
    grid = (bsz, n_pairs, n_blocks)

    def chunk(q_ref, fz_ref, v_ref, lb, o_ref, st_ref, s_scr, b_scr, k_scr, o_acc):
        c = _hgrn_common(q_ref, fz_ref, lb, b_scr, k_scr)
        yield
        s_t = s_scr[...]
        st_ref[...] = s_t
        vb = v_ref[...].astype(BF16)
        q_state = c["qt"] * _expand([jnp.exp(r) for r in c["bref"]])
        o_state = _dot_nt(q_state.astype(BF16), s_t.astype(BF16))
        js = range(N_SUB - 1)
        khb = c["kh"].astype(BF16)
        lhs = [(c["qt"][SUB * (j + 1):] * jnp.exp(c["refrow"][SUB * (j + 1):] - c["bend"][j])).astype(BF16)
               for j in js]
        yield
        a_js = [_dot_nt(lhs[j], khb[SUB * j:SUB * (j + 1)]) for j in js]
        k_state = c["kh"] * _expand([jnp.exp(c["bl"] - r) for r in c["bend"]])
        s_scr[...] = s_t * jnp.exp(c["bl"]) + _dot_tn(vb, k_state.astype(BF16))
        yield
        o_js = [_dot(a_js[j].astype(BF16), vb[SUB * j:SUB * (j + 1)]) for j in js]
        yield
        o_acc[...] = o_state
        for j in js:
            o_acc[SUB * (j + 1):, :] += o_js[j]
        t_loc = _iota2((HALF, HEAD), 0)
        for i in range(N_SUB):
            r0 = SUB * i
            q_h = [q_ref[pl.ds(r0 + HALF * u, HALF), :] for u in range(2)]
            b_h = [b_scr[pl.ds(r0 + HALF * u, HALF), :] for u in range(2)]
            o_h = [jnp.zeros((HALF, HEAD), F32) for _ in range(2)]
            for s in range(SUB):
                brow = b_scr[pl.ds(r0 + s, 1), :]
                krow = k_scr[pl.ds(r0 + s, 1), :]
                vrow = v_ref[pl.ds(r0 + s, 1), :]
                for u in range(s // HALF, 2):
                    diff = b_h[u] - brow
                    if u == s // HALF:
                        diff = jnp.where(t_loc >= s - HALF * u, diff, NEG_BIG)
                    col = jnp.sum(q_h[u] * krow * jnp.exp(diff), axis=-1, keepdims=True)
                    o_h[u] = o_h[u] + col * vrow
            for u in range(2):
                o_acc[pl.ds(r0 + HALF * u, HALF), :] += o_h[u]
            yield
        o_ref[...] = o_acc[...]

    def body(*refs):
        q_ref, fz_ref, v_ref, lb_ref = refs[:4]
        o_ref, st_ref = refs[4 + nr:6 + nr]
        s_scr, b_scr, k_scr, o_acc = refs[6 + 2 * nr:10 + 2 * nr]
        finish_ride = _ride_along(refs[4:4 + nr], refs[6 + nr:6 + 2 * nr], ride_modes, refs[10 + 2 * nr:], grid)

        @pl.when(pl.program_id(2) == 0)
        def _():
            s_scr[...] = jnp.zeros_like(s_scr)

        def head(hh):
            cols = pl.ds(hh * HEAD, HEAD)
            return chunk(q_ref.at[:, cols], fz_ref.at[:, cols], v_ref.at[:, cols], lb_ref[:, cols],
                         o_ref.at[:, cols], st_ref.at[hh], s_scr.at[hh], b_scr.at[hh], k_scr.at[hh], o_acc.at[hh])

        _lockstep([head(hh) for hh in range(HEADS_PER_STEP)])
        finish_ride()

    def blk(col0):
        return pl.BlockSpec((BLOCK, wide), lambda b, h, n: (b * n_blocks + n, col0 + h))

    any_spec = pl.BlockSpec(memory_space=pl.ANY)
    per_head = pltpu.VMEM((HEADS_PER_STEP, BLOCK, HEAD), F32)
    return _pcall(
        body, name=name, grid=grid,
        in_specs=[blk(0), blk(n_pairs), blk(2 * n_pairs), pl.BlockSpec((1, wide), lambda b, h, n: (0, h))]
        + [any_spec] * nr,
        out_specs=(blk(0), pl.BlockSpec((None, HEADS_PER_STEP, HEAD, HEAD), lambda b, h, n: (b * n_blocks + n, h, 0, 0)))
        + (any_spec,) * nr,
        out_shape=(jax.ShapeDtypeStruct((t, d), F32),
                   jax.ShapeDtypeStruct((bsz * n_blocks, n_heads, HEAD, HEAD), F32))
        + _exchange_shapes(ride, ride_modes),
        scratch_shapes=[per_head] * 4 + _exchange_sems(nr),
        compiler_params=_params("arbitrary", "arbitrary", "arbitrary"),
    )(main, main, main, lbrow, *ride)


def _hgrn_bwd(main, lbrow, states, do, bsz, n_blocks, ride, ride_modes, name):
    t, d3 = main.shape
    d = d3 // 3
    n_pairs = d // HEAD // HEADS_PER_STEP
    wide = HEADS_PER_STEP * HEAD
    nr = len(ride)
    grid = (bsz, n_pairs, n_blocks)

    def chunk(n, q_ref, fz_ref, v_ref, lb, st_ref, do_ref, dq_ref, dfz_ref, dv_ref, dlb_ref,
              ds_scr, b_scr, k_scr, dqt_acc, dkh_acc, dv_acc, dqd_acc, dkd_acc, ad_scr):
        c = _hgrn_common(q_ref, fz_ref, lb, b_scr, k_scr)
        yield
        q, k = q_ref[...], c["k"]
        vb = v_ref[...].astype(BF16)
        dob = do_ref[...].astype(BF16)
        s0_t = st_ref[...]
        ds1_t = ds_scr[...]
        e_ref = _expand([jnp.exp(r) for r in c["bref"]])
        e_end = _expand([jnp.exp(c["bl"] - r) for r in c["bend"]])
        e_bl = jnp.exp(c["bl"])
        q_state = c["qt"] * e_ref
        k_state = c["kh"] * e_end
        dq_state = _dot(dob, s0_t.astype(BF16))
        dk_state = _dot(vb, ds1_t.astype(BF16))
        dv_state = _dot_nt(k_state.astype(BF16), ds1_t.astype(BF16))
        ds_scr[...] = ds1_t * e_bl + _dot_tn(dob, q_state.astype(BF16))
        js = range(N_SUB - 1)
        lo = [slice(SUB * j, SUB * (j + 1)) for j in js]
        khb = c["kh"].astype(BF16)
        dj = [jnp.exp(c["refrow"][SUB * (j + 1):] - c["bend"][j]) for j in js]
        lhs = [(c["qt"][SUB * (j + 1):] * dj[j]).astype(BF16) for j in js]
        yield
        a_js = [_dot_nt(lhs[j], khb[lo[j]]) for j in js]
        da_js = [_dot_nt(dob[SUB * (j + 1):], vb[lo[j]]).astype(BF16) for j in js]
        dqt_acc[...] = dq_state * e_ref
        dkh_acc[...] = dk_state * e_end
        dv_acc[...] = dv_state
        dbl = (jnp.sum(s0_t * ds1_t, axis=0, keepdims=True) * e_bl
               + jnp.sum(k_state * dk_state, axis=0, keepdims=True))
        yield
        dv_js = [_dot_tn(a_js[j].astype(BF16), dob[SUB * (j + 1):]) for j in js]
        dq_js = [_dot(da_js[j], khb[lo[j]]) * dj[j] for j in js]
        dk_js = [_dot_tn(da_js[j], lhs[j]) for j in js]
        yield
        for j in js:
            dv_acc[lo[j], :] += dv_js[j]
            dqt_acc[SUB * (j + 1):, :] += dq_js[j]
            dkh_acc[lo[j], :] += dk_js[j]
        t_loc = _iota2((HALF, HEAD), 0)
        lane = _iota2((HALF, HEAD), 1)
        for i in range(N_SUB):
            r0 = SUB * i
            q_h = [q_ref[pl.ds(r0 + HALF * u, HALF), :] for u in range(2)]
            b_h = [b_scr[pl.ds(r0 + HALF * u, HALF), :] for u in range(2)]
            do_h = [do_ref[pl.ds(r0 + HALF * u, HALF), :] for u in range(2)]
            zero = jnp.zeros((HALF, HEAD), F32)
            dq_h, dk_h, a_h = [zero, zero], [zero, zero], [zero, zero]
            for s in range(SUB):
                brow = b_scr[pl.ds(r0 + s, 1), :]
                krow = k_scr[pl.ds(r0 + s, 1), :]
                vrow = v_ref[pl.ds(r0 + s, 1), :]
                dk_row = jnp.zeros((1, HEAD), F32)
                for u in range(s // HALF, 2):
                    diff = b_h[u] - brow
                    if u == s // HALF:
                        diff = jnp.where(t_loc >= s - HALF * u, diff, NEG_BIG)
                    w = jnp.exp(diff)
                    qw = q_h[u] * w
                    a_col = jnp.sum(qw * krow, axis=-1, keepdims=True)
                    da_col = jnp.sum(do_h[u] * vrow, axis=-1, keepdims=True)
                    a_h[u] = jnp.where(lane == r0 + s, a_col, a_h[u])
                    dq_h[u] = dq_h[u] + da_col * (w * krow)
                    dk_row = dk_row + jnp.sum(da_col * qw, axis=0, keepdims=True)
                us = s // HALF
                dk_h[us] = jnp.where(t_loc == s - HALF * us, dk_row, dk_h[us])
            for u in range(2):
                rows = pl.ds(r0 + HALF * u, HALF)
                dqd_acc[rows, :] = dq_h[u]
                dkd_acc[rows, :] = dk_h[u]
                ad_scr[rows, :] = a_h[u]
            yield
        dv_in = _dot_tn(ad_scr[...].astype(BF16), dob)
        dq = dqt_acc[...] * c["e_q"] + dqd_acc[...]
        dk = dkh_acc[...] * c["e_k"] + dkd_acc[...]
        rr, cc = _iota2((BLOCK, BLOCK), 0), _iota2((BLOCK, BLOCK), 1)
        db = q * dq - k * dk + jnp.where(_iota2((BLOCK, HEAD), 0) == BLOCK - 1, dbl, 0.0)
        yield
        dg = _tri_left(_tri(cc >= rr), db)
        yield
        real = jnp.logical_or(n > 0, _iota2((BLOCK, HEAD), 0) >= N_PAD)
        df = jnp.where(real, dg / c["f"] - dk, 0.0)
        dq_ref[...] = dq.astype(BF16)
        dv_ref[...] = (dv_acc[...] + dv_in).astype(BF16)
        dfz_ref[...] = (df * (1.0 - lb) * c["sig"] * c["nsig"]).astype(BF16)
        dlb_ref[...] += jnp.sum(df * c["nsig"], axis=0, keepdims=True)

    def body(*refs):
        q_ref, fz_ref, v_ref, lb_ref, st_ref, do_ref = refs[:6]
        dq_ref, dfz_ref, dv_ref, dlb_ref = refs[6 + nr:10 + nr]
        scratch = refs[10 + 2 * nr:19 + 2 * nr]
        finish_ride = _ride_along(refs[6:6 + nr], refs[10 + nr:10 + 2 * nr], ride_modes, refs[19 + 2 * nr:], grid)
        step = pl.program_id(2)

        @pl.when(step == 0)
        def _():
            scratch[0][...] = jnp.zeros_like(scratch[0])
            dlb_ref[...] = jnp.zeros_like(dlb_ref)

        def head(hh):
            cols = pl.ds(hh * HEAD, HEAD)
            return chunk(n_blocks - 1 - step, q_ref.at[:, cols], fz_ref.at[:, cols], v_ref.at[:, cols],
                         lb_ref[:, cols], st_ref.at[hh], do_ref.at[:, cols], dq_ref.at[:, cols],
                         dfz_ref.at[:, cols], dv_ref.at[:, cols], dlb_ref.at[:, cols],
                         *[scr.at[hh] for scr in scratch])

        _lockstep([head(hh) for hh in range(HEADS_PER_STEP)])
        finish_ride()

    def blk(col0):
        return pl.BlockSpec((BLOCK, wide), lambda b, h, s: (b * n_blocks + n_blocks - 1 - s, col0 + h))

    any_spec = pl.BlockSpec(memory_space=pl.ANY)
    per_head = pltpu.VMEM((HEADS_PER_STEP, BLOCK, HEAD), F32)
    return _pcall(
        body, name=name, grid=grid,
        in_specs=[blk(0), blk(n_pairs), blk(2 * n_pairs), pl.BlockSpec((1, wide), lambda b, h, s: (0, h)),
                  pl.BlockSpec((None, HEADS_PER_STEP, HEAD, HEAD),
                               lambda b, h, s: (b * n_blocks + n_blocks - 1 - s, h, 0, 0)),
                  blk(0)] + [any_spec] * nr,
        out_specs=(blk(0), blk(0), blk(0), pl.BlockSpec((None, 1, wide), lambda b, h, s: (b, 0, h)))
        + (any_spec,) * nr,
        out_shape=(jax.ShapeDtypeStruct((t, d), BF16),) * 3 + (jax.ShapeDtypeStruct((bsz, 1, d), F32),)
        + _exchange_shapes(ride, ride_modes),
        scratch_shapes=[per_head] * 9 + _exchange_sems(nr),
        compiler_params=_params("arbitrary", "arbitrary", "arbitrary"),
    )(main, main, main, lbrow, states, do, *ride)


SB_GROUP = 3
SB_ROWS = SB_GROUP * BLOCK
SB_DEAD = -104.0


def _sb_tables():
    j = jnp.bitwise_and(_iota2((2 * BLOCK, 2 * BLOCK), 0), BLOCK - 1)
    s = _iota2((2 * BLOCK, 2 * BLOCK), 1)
    ones = s >= BLOCK
    return (_tri(jnp.logical_or(ones, j > s)), _tri(jnp.logical_or(ones, j <= s)),
            _tri(jnp.logical_or(ones, j < s)))


def _sums(x, table, pieces=2):
    hi = x.astype(BF16)
    if pieces == 1:
        r = _dot(hi, table[:BLOCK])
    else:
        lo = (x - hi.astype(F32)).astype(BF16)
        r = _dot(jnp.concatenate([hi, lo], axis=1), table)
    return r[:, :BLOCK], r[:, BLOCK:]


def _sb_logits(q, ks, scale, causal, pad_row):
    z = _dot_nt(q, ks) * scale
    log_keep = -(jnp.maximum(z, 0.0) + jnp.log(1.0 + jnp.exp(-jnp.abs(z))))
    log_beta = z + log_keep
    if causal is not None:
        log_keep = jnp.where(causal, log_keep, 0.0)
    if pad_row is not None:
        log_keep = log_keep * pad_row
    return z, log_beta, log_keep


def _sb_fwd(qkv, bsz, n_blocks, name):
    t, d3 = qkv.shape
    d = d3 // 3
    n_heads = d // HEAD
    lp = n_blocks * BLOCK
    n_groups = n_blocks // SB_GROUP
    assert n_groups * SB_GROUP == n_blocks
    scale = HEAD ** -0.5

    def body(q_ref, k_ref, v_ref, upper_ref, o_ref, tot_ref, stop_ref, c_scr):
        g = pl.program_id(2)
        upper = upper_ref[...]
        causal = _iota2((BLOCK, BLOCK), 1) < _iota2((BLOCK, BLOCK), 0)
        lane = _iota2((1, BLOCK), 1)
        o_ref[...] = jnp.zeros_like(o_ref)
        c_scr[...] = jnp.zeros_like(c_scr)

        def tiles(jobs):
            qrows = [pl.ds(r * BLOCK, BLOCK) for r, _, _, _ in jobs]
            krows = [pl.ds(pl.multiple_of(m * BLOCK, BLOCK), BLOCK) for _, m, _, _ in jobs]
            lg = [_sb_logits(q_ref[qr, :], k_ref[kr, :], scale, causal if dg else None, pad_row)
                  for qr, kr, (_, _, dg, pad_row) in zip(qrows, krows, jobs)]
            sm = [_sums(x[2], upper) for x in lg]
            a_all = []
            for qr, (_, _, dg, _), (_, log_beta, _), (inner, total) in zip(qrows, jobs, lg, sm):
                c = c_scr[qr, :]
                a = jnp.exp(log_beta + c + inner)
                a_all.append(jnp.where(causal, a, 0.0) if dg else a)
                c_scr[qr, :] = c + total
            out = [_dot(a.astype(BF16), v_ref[kr, :]) for a, kr in zip(a_all, krows)]
            for qr, o_part in zip(qrows, out):
                o_ref[qr, :] += o_part

        def pad_row_of(m):
            return jnp.where(jnp.logical_or(m > 0, lane >= N_PAD), 1.0, 0.0)

        base = SB_GROUP * g
        tiles([(r, base + kb, r == kb, pad_row_of(base) if kb == 0 else None)
               for kb in range(SB_GROUP - 1, -1, -1) for r in range(kb, SB_GROUP)])

        def live(least):
            def cond(carry):
                m, c_max = carry
                return jnp.logical_and(m >= least, c_max >= SB_DEAD)
            return cond

        def step_two(carry):
            m, _ = carry
            pad_row = pad_row_of(m - 1)
            tiles([(r, m, False, None) for r in range(SB_GROUP)]
                  + [(r, m - 1, False, pad_row) for r in range(SB_GROUP)])
            return m - 2, jnp.max(c_scr[...])

        def step_one(carry):
            m, _ = carry
            pad_row = pad_row_of(m)
            tiles([(r, m, False, pad_row) for r in range(SB_GROUP)])
            return m - 1, jnp.max(c_scr[...])

        carry = lax.while_loop(live(1), step_two, (base - 1, jnp.max(c_scr[...])))
        m_end, _ = lax.while_loop(live(0), step_one, carry)
        tot_ref[...] = c_scr[...]
        stop_ref[...] = jnp.broadcast_to((m_end + 1).astype(F32), stop_ref.shape)

    qblk = pl.BlockSpec((SB_ROWS, HEAD), lambda b, h, g: (b * n_groups + g, h))
    table = pl.BlockSpec((2 * BLOCK, 2 * BLOCK), lambda b, h, g: (0, 0))
    return _pcall(
        body, name=name, grid=(bsz, n_heads, n_groups),
        in_specs=[qblk, pl.BlockSpec((lp, HEAD), lambda b, h, g: (b, n_heads + h)),
                  pl.BlockSpec((lp, HEAD), lambda b, h, g: (b, 2 * n_heads + h)), table],
        out_specs=(qblk, qblk, pl.BlockSpec((None, 8, 128), lambda b, h, g: ((b * n_heads + h) * n_groups + g, 0, 0))),
        out_shape=(jax.ShapeDtypeStruct((t, d), F32), jax.ShapeDtypeStruct((t, d), F32),
                   jax.ShapeDtypeStruct((bsz * n_heads * n_groups, 8, 128), F32)),
        scratch_shapes=[pltpu.VMEM((SB_ROWS, HEAD), F32)],
        compiler_params=_params("parallel", "parallel", "arbitrary"),
    )(qkv, qkv, qkv, _sb_tables()[0])


def _sb_bwd(qkv, do, tot, stop, bsz, n_blocks, name):
    t, d3 = qkv.shape
    d = d3 // 3
    n_heads = d // HEAD
    lp = n_blocks * BLOCK
    n_groups = n_blocks // SB_GROUP
    scale = HEAD ** -0.5

    def body(q_ref, k_ref, v_ref, do_ref, tot_ref, stop_ref, incl_ref, excl_ref, dq_ref, dk_ref, dv_ref,
             dk_acc, dv_acc, dq_acc, p_scr, e_scr, dob_scr):
        g = pl.program_id(2)

        @pl.when(g == 0)
        def _():
            dk_acc[...] = jnp.zeros_like(dk_acc)
            dv_acc[...] = jnp.zeros_like(dv_acc)

        incl, excl = incl_ref[...], excl_ref[...]
        causal = _iota2((BLOCK, BLOCK), 1) < _iota2((BLOCK, BLOCK), 0)
        lane = _iota2((1, BLOCK), 1)
        dob_scr[...] = do_ref[...].astype(BF16)
        dq_acc[...] = jnp.zeros_like(dq_acc)
        p_scr[...] = jnp.zeros_like(p_scr)
        e_scr[...] = jnp.zeros_like(e_scr)
        first = jnp.clip(jnp.max(stop_ref[...]).astype(jnp.int32), 0, SB_GROUP * g)

        def tiles(jobs):
            n_jobs = range(len(jobs))
            qrows = [pl.ds(r * BLOCK, BLOCK) for r, _, _, _ in jobs]
            krows = [pl.ds(pl.multiple_of(m * BLOCK, BLOCK), BLOCK) for _, m, _, _ in jobs]
            diag = [dg for _, _, dg, _ in jobs]
            lg = [_sb_logits(q_ref[qrows[i], :], k_ref[krows[i], :], scale, causal if diag[i] else None, jobs[i][3])
                  for i in n_jobs]
            d_a = [_dot_nt(dob_scr[qrows[i], :], v_ref[krows[i], :]) for i in n_jobs]
            sm = [_sums(lg[i][2], incl) for i in n_jobs]
            a_all = []
            for i in n_jobs:
                p = p_scr[qrows[i], :]
                a = jnp.exp(lg[i][1] + (tot_ref[qrows[i], :] - p - sm[i][0]))
                a_all.append(jnp.where(causal, a, 0.0) if diag[i] else a)
                p_scr[qrows[i], :] = p + sm[i][1]
            gr = [d_a[i] * a_all[i] for i in n_jobs]
            dv_part = [_dot_tn(a_all[i].astype(BF16), dob_scr[qrows[i], :]) for i in n_jobs]
            gs = [_sums(gr[i], excl, pieces=1) for i in n_jobs]
            dz_all = []
            for i in n_jobs:
                e = e_scr[qrows[i], :]
                dz = gr[i] - (gr[i] + e + gs[i][0]) * jnp.exp(lg[i][1])
                if diag[i]:
                    dz = jnp.where(causal, dz, 0.0)
                dz_all.append((dz * scale).astype(BF16))
                e_scr[qrows[i], :] = e + gs[i][1]
            dk_part = [_dot_tn(dz_all[i], q_ref[qrows[i], :]) for i in n_jobs]
            dq_part = [_dot(dz_all[i], k_ref[krows[i], :]) for i in n_jobs]
            for i in n_jobs:
                dv_acc[krows[i], :] += dv_part[i]
                dk_acc[krows[i], :] += dk_part[i]
                dq_acc[qrows[i], :] += dq_part[i]

        def pad_row_of(m):
            return jnp.where(jnp.logical_or(m > 0, lane >= N_PAD), 1.0, 0.0)

        def step_two(carry):
            m = carry
            pad_row = pad_row_of(m)
            m_next = m + 1
            tiles([(r, m, False, pad_row) for r in range(SB_GROUP)]
                  + [(r, m_next, False, None) for r in range(SB_GROUP)])
            return m + 2

        def step_one(carry):
            m = carry
            pad_row = pad_row_of(m)
            tiles([(r, m, False, pad_row) for r in range(SB_GROUP)])
            return m + 1

        base = SB_GROUP * g
        m = lax.while_loop(lambda m: m + 1 < base, step_two, first)
        lax.while_loop(lambda m: m < base, step_one, m)
        own = [base + kb for kb in range(SB_GROUP)]
        tiles([(r, own[kb], r == kb, pad_row_of(base) if kb == 0 else None)
               for kb in range(SB_GROUP) for r in range(kb, SB_GROUP)])
        dq_ref[...] = dq_acc[...].astype(BF16)

        @pl.when(g == n_groups - 1)
        def _():
            dk_ref[...] = dk_acc[...].astype(BF16)
            dv_ref[...] = dv_acc[...].astype(BF16)

    qblk = pl.BlockSpec((SB_ROWS, HEAD), lambda b, h, g: (b * n_groups + g, h))
    kblk = pl.BlockSpec((lp, HEAD), lambda b, h, g: (b, n_heads + h))
    vblk = pl.BlockSpec((lp, HEAD), lambda b, h, g: (b, 2 * n_heads + h))
    hblk = pl.BlockSpec((lp, HEAD), lambda b, h, g: (b, h))
    sblk = pl.BlockSpec((None, 8, 128), lambda b, h, g: ((b * n_heads + h) * n_groups + g, 0, 0))
    table = pl.BlockSpec((2 * BLOCK, 2 * BLOCK), lambda b, h, g: (0, 0))
    _, incl, excl = _sb_tables()
    return _pcall(
        body, name=name, grid=(bsz, n_heads, n_groups),
        in_specs=[qblk, kblk, vblk, qblk, qblk, sblk, table, table],
        out_specs=(qblk, hblk, hblk),
        out_shape=(jax.ShapeDtypeStruct((t, d), BF16),) * 3,
        scratch_shapes=[pltpu.VMEM((lp, HEAD), F32), pltpu.VMEM((lp, HEAD), F32)]
        + [pltpu.VMEM((SB_ROWS, HEAD), F32)] * 3 + [pltpu.VMEM((SB_ROWS, HEAD), BF16)],
        compiler_params=_params("parallel", "parallel", "arbitrary"),
    )(qkv, qkv, qkv, do, tot, stop, incl, excl)


def _adamw(w, g, m, v):
    m = ADAM_B1 * m + (1.0 - ADAM_B1) * g
    v = ADAM_B2 * v + (1.0 - ADAM_B2) * (g * g)
    m_hat = m / (1.0 - ADAM_B1 ** ADAM_STEP)
    v_hat = v / (1.0 - ADAM_B2 ** ADAM_STEP)
    delta = -ADAM_LR * (m_hat / (jnp.sqrt(v_hat) + ADAM_EPS) + ADAM_WD * w)
    return delta, m, v


def _update_sharded(w, parts, m, v, name):
    r, c = w.shape
    tr = min(r, ROW_TILE)

    def body(w_ref, p_ref, m_ref, v_ref, g_ref, d_ref, nm_ref, nv_ref):
        g = p_ref[0].astype(F32)
        for q in range(1, N_DEV):
            g = g + p_ref[q].astype(F32)
        g_ref[...] = g
        d_ref[...], nm_ref[...], nv_ref[...] = _adamw(w_ref[...], g, m_ref[...], v_ref[...])

    row = pl.BlockSpec((tr, c), lambda i: (i, 0))
    return _pcall(
        body, name=name, grid=(r // tr,),
        in_specs=[row, pl.BlockSpec((N_DEV, tr, c), lambda i: (0, i, 0)), row, row],
        out_specs=(row,) * 4, out_shape=(jax.ShapeDtypeStruct((r, c), F32),) * 4,
        compiler_params=_params("parallel"),
    )(w, parts, m, v)


SMALL_ROWS = 8


def _pack_small(dpre0, dpre1, dpost0, dpost1, dlb, dwon, loss, name):
    d = dpre0.shape[1]
    bsz = dlb.shape[0]

    def body(a0, a1, p0, p1, lb_ref, on_ref, loss_ref, out_ref):
        out_ref[...] = jnp.zeros_like(out_ref)
        out_ref[pl.ds(0, 1), :] = a0[...]
        out_ref[pl.ds(1, 1), :] = a1[...]
        out_ref[pl.ds(2, 1), :] = p0[...]
        out_ref[pl.ds(3, 1), :] = p1[...]
        acc = lb_ref[0]
        for b in range(1, bsz):
            acc = acc + lb_ref[b]
        out_ref[pl.ds(4, 1), :] = acc
        out_ref[pl.ds(5, 1), pl.ds(0, HEAD)] = on_ref[...]
        out_ref[pl.ds(6, 1), pl.ds(0, HEAD)] = loss_ref[pl.ds(0, 1), :]

    return _pcall(body, name=name, out_shape=jax.ShapeDtypeStruct((SMALL_ROWS, d), F32))(
        dpre0, dpre1, dpost0, dpost1, dlb, dwon, loss)


def _update_small(parts, pre, post, lbw, on, moments, name):
    d = pre.shape[1]

    def body(p_ref, pre_ref, post_ref, lbw_ref, on_ref, mpre, mpost, mlb, mon, vpre, vpost, vlb, von,
             loss_ref, *outs):
        def total(r0, nr, width):
            acc = p_ref[0, pl.ds(r0, nr), pl.ds(0, width)]
            for q in range(1, N_DEV):
                acc = acc + p_ref[q, pl.ds(r0, nr), pl.ds(0, width)]
            return acc

        def put(k, w, g, m, v):
            dl, nm, nv = _adamw(w, g, m, v)
            outs[4 * k][...] = g
            outs[4 * k + 1][...] = dl
            outs[4 * k + 2][...] = nm
            outs[4 * k + 3][...] = nv

        put(0, pre_ref[...], total(0, 2, d), mpre[...], vpre[...])
        put(1, post_ref[...], total(2, 2, d), mpost[...], vpost[...])
        a0, a1 = lbw_ref[pl.ds(0, 1), :], lbw_ref[pl.ds(1, 1), :]
        mx = jnp.maximum(a0, a1)
        e0, e1 = jnp.exp(a0 - mx), jnp.exp(a1 - mx)
        p0 = e0 / (e0 + e1)
        g0 = total(4, 1, d) * p0 * (1.0 - p0)
        for r, w, g in ((0, a0, g0), (1, a1, -g0)):
            row = pl.ds(r, 1)
            dl, nm, nv = _adamw(w, g, mlb[row, :], vlb[row, :])
            outs[8][row, :] = g
            outs[9][row, :] = dl
            outs[10][row, :] = nm
            outs[11][row, :] = nv
        put(3, on_ref[...], total(5, 1, HEAD), mon[...], von[...])
        loss_ref[...] = jnp.broadcast_to(total(6, 1, HEAD), loss_ref.shape)

    shapes = []
    for w in (pre, post, lbw, on):
        shapes += [jax.ShapeDtypeStruct(w.shape, F32)] * 4
    return _pcall(body, name=name, out_shape=(jax.ShapeDtypeStruct((8, 128), F32), *shapes))(
        parts, pre, post, lbw, on, *moments)


def kernel(x, meta_tokens, pre_norm, post_norm, hgrn_w_in, hgrn_lb, hgrn_out_norm, hgrn_w_out, sb_w_in, sb_w_out, loss_target, m_meta_tokens, m_pre_norm, m_post_norm, m_hgrn_w_in, m_hgrn_lb, m_hgrn_out_norm, m_hgrn_w_out, m_sb_w_in, m_sb_w_out, v_meta_tokens, v_pre_norm, v_post_norm, v_hgrn_w_in, v_hgrn_lb, v_hgrn_out_norm, v_hgrn_w_out, v_sb_w_in, v_sb_w_out):
    bsz, seq, d = x.shape
    n_blocks = seq // BLOCK + 1
    lp = n_blocks * BLOCK
    t = bsz * lp
    s = hgrn_w_in.shape[2]
    dsh = d // N_DEV

    w_in_h, meta_all = _gather_once_per_chip([hgrn_w_in[0].astype(BF16), meta_tokens], "gather_weights")
    meta_full = jnp.transpose(meta_all, (1, 0, 2)).reshape(N_META, d)

    h0 = jnp.concatenate(
        [jnp.zeros((bsz, N_PAD, d), F32), jnp.broadcast_to(meta_full[None], (bsz, N_META, d)), x], axis=1
    ).reshape(t, d)
    lbrow = jnp.cumsum(jax.nn.softmax(hgrn_lb, axis=0), axis=0)[0:1]

    main0, gate0, yn0 = _norm_inproj(h0, pre_norm[0:1], w_in_h, F32, "inproj_hgrn")
    o0, states, w_in_s, w_out_s, w_out_h = _hgrn_fwd(
        main0, lbrow, bsz, n_blocks,
        [sb_w_in[0].astype(BF16), sb_w_out[0].astype(BF16), hgrn_w_out[0].astype(BF16)], ["gather"] * 3, "hgrn_fwd")
    w_out_s = w_out_s.reshape(d, d)
    w_out_h = w_out_h.reshape(d, d)
    h1, y0 = _mix_out(o0, gate0, h0, w_out_h, hgrn_out_norm, post_norm[0:1], True, "mix_out_hgrn")
    main1, gate1, yn1 = _norm_inproj(h1, pre_norm[1:2], w_in_s, BF16, "inproj_sb")
    o1, tot, stop = _sb_fwd(main1, bsz, n_blocks, "sb_fwd")
    h2, y1 = _mix_out(o1, gate1, h1, w_out_s, hgrn_out_norm, post_norm[1:2], False, "mix_out_sb")
    dh2, loss_part = _loss_grad(h2, loss_target, n_blocks, "loss_grad")

    do1, dgate1, dw_out_s, dpost1, _ = _mix_out_bwd(
        dh2, y1, o1, gate1, w_out_s, hgrn_out_norm, post_norm[1:2], False, "mix_out_sb_bwd")
    dq1, dk1, dv1 = _sb_bwd(main1, do1, tot, stop, bsz, n_blocks, "sb_bwd")
    dproj1 = (dq1, dk1, dv1, dgate1)
    dh1, dpre1 = _inproj_bwd_x(dproj1, w_in_s, h1, pre_norm[1:2], dh2, [], [], "inproj_sb_bwd_x")
    dw_in_s = _inproj_bwd_w(yn1, dproj1, s, "inproj_sb_bwd_w")

    do0, dgate0, dw_out_h, dpost0, dwon = _mix_out_bwd(
        dh1, y0, o0, gate0, w_out_h, hgrn_out_norm, post_norm[0:1], True, "mix_out_hgrn_bwd")
    dq0, dfz0, dv0, dlb, p_in_s, p_out_s, p_out_h = _hgrn_bwd(
        main0, lbrow, states, do0, bsz, n_blocks,
        [dw_in_s, dw_out_s.reshape(N_DEV, dsh, d), dw_out_h.reshape(N_DEV, dsh, d)], ["scatter"] * 3, "hgrn_bwd")
    dproj0 = (dq0, dfz0, dv0, dgate0)
    dw_in_h = _inproj_bwd_w(yn0, dproj0, s, "inproj_hgrn_bwd_w")
    dh0, dpre0, p_in_h = _inproj_bwd_x(
        dproj0, w_in_h, h0, pre_norm[0:1], dh1, [dw_in_h], ["scatter"], "inproj_hgrn_bwd_x")

    dh0 = dh0.reshape(bsz, lp, d)
    grad_x = dh0[:, BLOCK:]
    dmeta = jnp.sum(dh0[:, N_PAD:BLOCK], axis=0)
    dmeta = jnp.transpose(dmeta.reshape(N_META, N_DEV, dsh), (1, 0, 2))
    small = _pack_small(dpre0, dpre1, dpost0, dpost1, dlb, dwon, loss_part, "pack_small")

    p_meta, p_small = _exchange([dmeta, small], ["scatter", "gather"], "exchange_grads")

    u_meta = _update_sharded(meta_tokens, p_meta, m_meta_tokens, v_meta_tokens, "update_meta")
    u_in_h = _update_sharded(hgrn_w_in[0], p_in_h, m_hgrn_w_in[0], v_hgrn_w_in[0], "update_hgrn_w_in")
    u_out_h = _update_sharded(hgrn_w_out[0], p_out_h, m_hgrn_w_out[0], v_hgrn_w_out[0], "update_hgrn_w_out")
    u_in_s = _update_sharded(sb_w_in[0], p_in_s, m_sb_w_in[0], v_sb_w_in[0], "update_sb_w_in")
    u_out_s = _update_sharded(sb_w_out[0], p_out_s, m_sb_w_out[0], v_sb_w_out[0], "update_sb_w_out")
    sm = _update_small(p_small, pre_norm, post_norm, hgrn_lb, hgrn_out_norm,
                       (m_pre_norm, m_post_norm, m_hgrn_lb, m_hgrn_out_norm,
                        v_pre_norm, v_post_norm, v_hgrn_lb, v_hgrn_out_norm), "update_small")
    loss = sm[0][0, 0]
    u_pre, u_post, u_lb, u_on = sm[1:5], sm[5:9], sm[9:13], sm[13:17]

    per_w = [u_meta, u_pre, u_post, tuple(a[None] for a in u_in_h), u_lb, u_on,
             tuple(a[None] for a in u_out_h), tuple(a[None] for a in u_in_s), tuple(a[None] for a in u_out_s)]
    outs = [loss, grad_x]
    for k in range(4):
        outs += [u[k] for u in per_w]
    return tuple(outs)
```

```python
import jax
import jax.numpy as jnp
from jax import lax
from jax.experimental import pallas as pl
from jax.experimental.pallas import tpu as pltpu

F32 = jnp.float32
BF16 = jnp.bfloat16

N_DEV = 8
BLOCK = 128
N_META = 16
N_PAD = BLOCK - N_META
HEAD = 128
SUB = 16
N_SUB = BLOCK // SUB
HALF = 8
EPS = 1e-6
ROW_TILE = 3 * BLOCK
UPDATE_ROWS = 256
K_TILES = (1408, 768, 384, 128)
NEG_BIG = -1e30

ADAM_LR = 0.001
ADAM_B1 = 0.9
ADAM_B2 = 0.999
ADAM_EPS = 1e-08
ADAM_WD = 0.01
ADAM_STEP = 10

VMEM_LIMIT = 56 * 1024 * 1024


def _pcall(body, **kw):
    return pl.pallas_call(body, **kw)


def _params(*sem):
    return pltpu.CompilerParams(dimension_semantics=sem, vmem_limit_bytes=VMEM_LIMIT)


def _dot(a, b):
    return jnp.dot(a, b, preferred_element_type=F32)


def _dot_nt(a, b):
    return lax.dot_general(a, b, (((1,), (1,)), ((), ())), preferred_element_type=F32)


def _dot_tn(a, b):
    return lax.dot_general(a, b, (((0,), (0,)), ((), ())), preferred_element_type=F32)


def _split(x, pieces):
    out = []
    for _ in range(pieces):
        p = x.astype(BF16)
        out.append(p)
        x = x - p.astype(F32)
    return out


def _tri_right(x, tri, pieces=2):
    return sum(_dot(p, tri) for p in _split(x, pieces))


def _tri_left(tri, x, pieces=3):
    return sum(_dot(tri, p) for p in _split(x, pieces))


def _iota2(shape, dim):
    return lax.broadcasted_iota(jnp.int32, shape, dim)


def _tri(cond):
    return jnp.where(cond, 1.0, 0.0).astype(BF16)


def _sig_pair(x):
    e = jnp.exp(-jnp.abs(x))
    r = 1.0 / (1.0 + e)
    er = e * r
    pos = x >= 0
    return jnp.where(pos, r, er), jnp.where(pos, er, r)


def _expand(rows):
    return jnp.concatenate([jnp.broadcast_to(r, (SUB, HEAD)) for r in rows], axis=0)


def _exchange_shapes(arrays, modes):
    return tuple(jax.ShapeDtypeStruct((N_DEV,) + tuple(a.shape[1:] if m == "scatter" else a.shape), a.dtype)
                 for a, m in zip(arrays, modes))


def _exchange_sems(n):
    if n == 0:
        return []
    return [pltpu.SemaphoreType.DMA((n, N_DEV - 1)), pltpu.SemaphoreType.DMA((n, N_DEV - 1)),
            pltpu.SemaphoreType.DMA((n,))]


def _exchange_copies(ins, outs, modes, send_sems, recv_sems, local_sems):
    mx, my, mc = lax.axis_index("x"), lax.axis_index("y"), lax.axis_index("c")
    me = 4 * mx + 2 * my + mc

    def src(i, slot):
        return ins[i].at[slot] if modes[i] == "scatter" else ins[i]

    def peer_of(mask):
        px = 1 - mx if mask & 4 else mx
        py = 1 - my if mask & 2 else my
        pc = 1 - mc if mask & 1 else mc
        return px, py, pc

    def copy(i, mask, dst_slot):
        px, py, pc = peer_of(mask)
        return pltpu.make_async_remote_copy(
            src_ref=src(i, 4 * px + 2 * py + pc), dst_ref=outs[i].at[dst_slot],
            send_sem=send_sems.at[i, mask - 1], recv_sem=recv_sems.at[i, mask - 1],
            device_id=(px, py, pc), device_id_type=pl.DeviceIdType.MESH)

    n = len(ins)
    sends = [copy(i, mask, me) for mask in range(1, N_DEV) for i in range(n)]
    own = [pltpu.make_async_copy(src(i, me), outs[i].at[me], local_sems.at[i]) for i in range(n)]
    arrivals = []
    for mask in range(1, N_DEV):
        px, py, pc = peer_of(mask)
        arrivals += [copy(i, mask, 4 * px + 2 * py + pc) for i in range(n)]

    def start():
        for cp in sends + own:
            cp.start()

    def wait():
        for cp in arrivals:
            cp.wait_recv()
        for cp in sends:
            cp.wait_send()
        for cp in own:
            cp.wait()

    return start, wait


def _gather_once_per_chip(arrays, name):
    n = len(arrays)

    def body(*refs):
        ins, outs = refs[:n], refs[n:2 * n]
        send_sems, recv_sems, local_sems = refs[2 * n:]
        mx, my, mc = lax.axis_index("x"), lax.axis_index("y"), lax.axis_index("c")
        me, sibling = (mx, my, mc), (mx, my, 1 - mc)
        chips = [(1 - mx, my), (mx, 1 - my), (1 - mx, 1 - my)]

        def slot(px, py, pc):
            return 4 * px + 2 * py + pc

        def copy(i, k, block, to, src=None):
            return pltpu.make_async_remote_copy(
                src_ref=outs[i].at[slot(*block)] if src is None else src, dst_ref=outs[i].at[slot(*block)],
                send_sem=send_sems.at[i, k], recv_sem=recv_sems.at[i, k],
                device_id=to, device_id_type=pl.DeviceIdType.MESH)

        own = [pltpu.make_async_copy(ins[i], outs[i].at[slot(*me)], local_sems.at[i]) for i in range(n)]
        first = [copy(i, 0, me, sibling, src=ins[i]) for i in range(n)]
        first += [copy(i, 1 + j, me, (*chip, mc), src=ins[i]) for j, chip in enumerate(chips) for i in range(n)]
        for cp in own + first:
            cp.start()
        passed = []
        for j, chip in enumerate(chips):
            for i in range(n):
                copy(i, 1 + j, (*chip, mc), me).wait_recv()
                cp = copy(i, 4 + j, (*chip, mc), sibling)
                cp.start()
                passed.append(cp)
        for i in range(n):
            copy(i, 0, sibling, me).wait_recv()
            for j, chip in enumerate(chips):
                copy(i, 4 + j, (*chip, 1 - mc), me).wait_recv()
        for cp in first + passed:
            cp.wait_send()
        for cp in own:
            cp.wait()

    any_spec = pl.BlockSpec(memory_space=pl.ANY)
    return _pcall(
        body, name=name, out_shape=_exchange_shapes(arrays, ["gather"] * n),
        in_specs=[any_spec] * n, out_specs=tuple([any_spec] * n),
        scratch_shapes=_exchange_sems(n),
    )(*arrays)


def _exchange(arrays, modes, name):
    n = len(arrays)

    def body(*refs):
        start, wait = _exchange_copies(refs[:n], refs[n:2 * n], modes, *refs[2 * n:])
        start()
        wait()

    any_spec = pl.BlockSpec(memory_space=pl.ANY)
    return _pcall(
        body, name=name, out_shape=_exchange_shapes(arrays, modes),
        in_specs=[any_spec] * n, out_specs=tuple([any_spec] * n),
        scratch_shapes=_exchange_sems(n),
    )(*arrays)


def _tile_blocks_of_x(seq, d):
    assert ROW_TILE == 3 * BLOCK and (seq + BLOCK) % ROW_TILE == 0
    per_seq = (seq + BLOCK) // ROW_TILE

    def spec(k):
        return pl.BlockSpec((None, BLOCK, d),
                            lambda i: (i // per_seq, jnp.maximum(3 * (i % per_seq) - 1 + k, 0), 0))

    return [spec(0), spec(1), spec(2)]


def _norm_inproj(h, wnorm, w_all, main_dtype, name, from_x=None):
    p_n, d, s = w_all.shape
    n_main = 3 * d // s
    tm = ROW_TILE
    if from_x is None:
        t = h.shape[0]
        lead, lead_specs = [h], [pl.BlockSpec((tm, d), lambda i: (i, 0))]
    else:
        x_in, meta = from_x
        t = x_in.shape[0] * (x_in.shape[1] + BLOCK)
        tiles_per_seq = (x_in.shape[1] + BLOCK) // tm
        lead = [x_in, x_in, x_in, meta]
        lead_specs = _tile_blocks_of_x(x_in.shape[1], d) + [pl.BlockSpec((N_META, d), lambda i: (0, 0))]
    n_lead = len(lead)

    def body(*refs):
        wn_ref, w_ref = refs[n_lead:n_lead + 2]
        main_ref, gate_ref, ynt_ref = refs[n_lead + 2:n_lead + 5]
        if from_x is None:
            x = refs[0][...]
        else:
            first_tile = pl.program_id(0) % tiles_per_seq == 0
            meta_block = jnp.concatenate([jnp.zeros((N_PAD, d), F32), refs[3][...]], axis=0)
            x = jnp.concatenate([jnp.where(first_tile, meta_block, refs[0][...]), refs[1][...], refs[2][...]], axis=0)
            refs[n_lead + 5][...] = x
        y = x * lax.rsqrt(jnp.mean(x * x, axis=-1, keepdims=True) + EPS) * wn_ref[...]
        yb = y.astype(BF16)
        ynt_ref[...] = y.T.astype(BF16)
        for p in range(p_n):
            r = _dot(yb, w_ref[p])
            if p < n_main:
                main_ref[:, p * s:(p + 1) * s] = r.astype(main_dtype)
            else:
                gate_ref[:, (p - n_main) * s:(p - n_main + 1) * s] = r

    row = pl.BlockSpec((tm, d), lambda i: (i, 0))
    extra_specs, extra_shapes = ((), ()) if from_x is None else ((row,), (jax.ShapeDtypeStruct((t, d), F32),))
    return _pcall(
        body, name=name, grid=(t // tm,),
        in_specs=lead_specs + [pl.BlockSpec((1, d), lambda i: (0, 0)), pl.BlockSpec((p_n, d, s), lambda i: (0, 0, 0))],
        out_specs=(pl.BlockSpec((tm, 3 * d), lambda i: (i, 0)), row, pl.BlockSpec((d, tm), lambda i: (0, i)))
        + extra_specs,
        out_shape=(jax.ShapeDtypeStruct((t, 3 * d), main_dtype), jax.ShapeDtypeStruct((t, d), F32),
                   jax.ShapeDtypeStruct((d, t), BF16)) + extra_shapes,
        compiler_params=_params("parallel"),
    )(*lead, wnorm, w_all)


def _mix_out(o, gate, h_in, w_out, w_on, w_post, head_norm, name, target=None):
    t, d = o.shape
    n_heads = d // HEAD
    tm = ROW_TILE
    with_loss = target is not None
    if with_loss:
        tiles_per_seq = (target.shape[1] + BLOCK) // tm

    def body(o_ref, g_ref, h_ref, w_ref, won_ref, wp_ref, *rest):
        if with_loss:
            t0_ref, t1_ref, t2_ref, hout_ref, y_ref, loss_ref, u_scr = rest
        else:
            hout_ref, y_ref, u_scr = rest
        for hh in range(n_heads):
            cs = slice(hh * HEAD, (hh + 1) * HEAD)
            oh = o_ref[:, cs]
            gt = g_ref[:, cs]
            if head_norm:
                oh = oh * lax.rsqrt(jnp.mean(oh * oh, axis=-1, keepdims=True) + EPS) * won_ref[...]
            u_scr[:, cs] = (oh * (gt * jax.nn.sigmoid(gt))).astype(BF16)
        y = _dot(u_scr[...], w_ref[...])
        y_ref[...] = y
        r = y * lax.rsqrt(jnp.mean(y * y, axis=-1, keepdims=True) + EPS) * wp_ref[...]
        h_out = h_ref[...] + r
        if not with_loss:
            hout_ref[...] = h_out
            return
        i = pl.program_id(0)

        @pl.when(i == 0)
        def _():
            loss_ref[...] = jnp.zeros_like(loss_ref)

        tgt = jnp.concatenate([t0_ref[...], t1_ref[...], t2_ref[...]], axis=0)
        real = jnp.logical_or(i % tiles_per_seq > 0, _iota2((tm, d), 0) >= BLOCK)
        err = jnp.where(real, h_out - tgt, 0.0)
        hout_ref[...] = err * (1.0 / d)
        part = jnp.sum(jnp.sum(err * err, axis=-1, keepdims=True), axis=0, keepdims=True)
        loss_ref[...] += part * (0.5 / d)

    row = pl.BlockSpec((tm, d), lambda i: (i, 0))
    in_specs = [row, row, row, pl.BlockSpec((d, d), lambda i: (0, 0)),
                pl.BlockSpec((1, HEAD), lambda i: (0, 0)), pl.BlockSpec((1, d), lambda i: (0, 0))]
    out_specs, out_shape = (row, row), (jax.ShapeDtypeStruct((t, d), F32), jax.ShapeDtypeStruct((t, d), F32))
    args = (o, gate, h_in, w_out, w_on, w_post)
    if with_loss:
        in_specs += _tile_blocks_of_x(target.shape[1], d)
        out_specs += (pl.BlockSpec((8, 128), lambda i: (0, 0)),)
        out_shape += (jax.ShapeDtypeStruct((8, 128), F32),)
        args += (target, target, target)
    return _pcall(
        body, name=name, grid=(t // tm,), in_specs=in_specs, out_specs=out_specs, out_shape=out_shape,
        scratch_shapes=[pltpu.VMEM((tm, d), BF16)],
        compiler_params=_params("arbitrary" if with_loss else "parallel"),
    )(*args)


def _mix_out_bwd(dh, y, o, gate, w_out, w_on, w_post, head_norm, name):
    t, d = o.shape
    n_heads = d // HEAD
    tm = ROW_TILE
    last = t // tm - 1

    def body(dh_ref, y_ref, o_ref, g_ref, w_ref, won_ref, wp_ref,
             do_ref, dg_ref, dw_ref, dwp_ref, dwon_ref, u_scr, acc):
        i = pl.program_id(0)

        @pl.when(i == 0)
        def _():
            acc[...] = jnp.zeros_like(acc)
            dwp_ref[...] = jnp.zeros_like(dwp_ref)
            dwon_ref[...] = jnp.zeros_like(dwon_ref)

        yv = y_ref[...]
        rs = lax.rsqrt(jnp.mean(yv * yv, axis=-1, keepdims=True) + EPS)
        yh = yv * rs
        dr = dh_ref[...]
        dwp_ref[...] += jnp.sum(dr * yh, axis=0, keepdims=True)
        wd = dr * wp_ref[...]
        dy = rs * (wd - yh * jnp.mean(wd * yh, axis=-1, keepdims=True))
        dyb = dy.astype(BF16)
        du = _dot_nt(dyb, w_ref[...])
        for hh in range(n_heads):
            cs = slice(hh * HEAD, (hh + 1) * HEAD)
            oh = o_ref[:, cs]
            gt = g_ref[:, cs]
            sg = jax.nn.sigmoid(gt)
            sl = gt * sg
            duh = du[:, cs]
            if head_norm:
                rsh = lax.rsqrt(jnp.mean(oh * oh, axis=-1, keepdims=True) + EPS)
                ohat = oh * rsh
                on = ohat * won_ref[...]
            else:
                on = oh
            u_scr[:, cs] = (on * sl).astype(BF16)
            don = duh * sl
            dg_ref[:, cs] = (duh * on * (sg * (1.0 + gt * (1.0 - sg)))).astype(BF16)
            if head_norm:
                dwon_ref[...] += jnp.sum(don * ohat, axis=0, keepdims=True)
                wdn = don * won_ref[...]
                do_ref[:, cs] = rsh * (wdn - ohat * jnp.mean(wdn * ohat, axis=-1, keepdims=True))
            else:
                do_ref[:, cs] = don
        acc[...] += _dot_tn(u_scr[...], dyb)

        @pl.when(i == last)
        def _():
            dw_ref[...] = acc[...].astype(BF16)

    row = pl.BlockSpec((tm, d), lambda i: (i, 0))
    return _pcall(
        body, name=name, grid=(t // tm,),
        in_specs=[row, row, row, row, pl.BlockSpec((d, d), lambda i: (0, 0)),
                  pl.BlockSpec((1, HEAD), lambda i: (0, 0)), pl.BlockSpec((1, d), lambda i: (0, 0))],
        out_specs=(row, row, pl.BlockSpec((d, d), lambda i: (0, 0)), pl.BlockSpec((1, d), lambda i: (0, 0)),
                   pl.BlockSpec((1, HEAD), lambda i: (0, 0))),
        out_shape=(jax.ShapeDtypeStruct((t, d), F32), jax.ShapeDtypeStruct((t, d), BF16),
                   jax.ShapeDtypeStruct((d, d), BF16), jax.ShapeDtypeStruct((1, d), F32),
                   jax.ShapeDtypeStruct((1, HEAD), F32)),
        scratch_shapes=[pltpu.VMEM((tm, d), BF16), pltpu.VMEM((d, d), F32)],
        compiler_params=_params("arbitrary"),
    )(dh, y, o, gate, w_out, w_on, w_post)


def _inproj_bwd_x(dparts, w_all, h, wnorm, dres, ride, ride_modes, name):
    t, d = h.shape
    p_n, _, s = w_all.shape
    per = d // s
    tm = ROW_TILE
    nr = len(ride)
    grid = (t // tm,)

    def body(*refs):
        d0_ref, d1_ref, d2_ref, d3_ref, w_ref, h_ref, wn_ref, dres_ref = refs[:8]
        dh_ref, dwn_ref = refs[8 + nr:10 + nr]
        finish_ride = _ride_along(refs[8:8 + nr], refs[10 + nr:10 + 2 * nr], ride_modes, refs[10 + 2 * nr:], grid)
        i = pl.program_id(0)

        @pl.when(i == 0)
        def _():
            dwn_ref[...] = jnp.zeros_like(dwn_ref)

        pieces = (d0_ref, d1_ref, d2_ref, d3_ref)
        dyn = jnp.zeros((tm, d), F32)
        for p in range(p_n):
            blk = pieces[p // per][:, (p % per) * s:(p % per + 1) * s]
            dyn = dyn + _dot_nt(blk, w_ref[p])
        x = h_ref[...]
        rs = lax.rsqrt(jnp.mean(x * x, axis=-1, keepdims=True) + EPS)
        xh = x * rs
        dwn_ref[...] += jnp.sum(dyn * xh, axis=0, keepdims=True)
        wd = dyn * wn_ref[...]
        dh_ref[...] = dres_ref[...] + rs * (wd - xh * jnp.mean(wd * xh, axis=-1, keepdims=True))
        finish_ride()

    row = pl.BlockSpec((tm, d), lambda i: (i, 0))
    any_spec = pl.BlockSpec(memory_space=pl.ANY)
    return _pcall(
        body, name=name, grid=grid,
        in_specs=[row, row, row, row, pl.BlockSpec((p_n, d, s), lambda i: (0, 0, 0)),
                  row, pl.BlockSpec((1, d), lambda i: (0, 0)), row] + [any_spec] * nr,
        out_specs=(row, pl.BlockSpec((1, d), lambda i: (0, 0))) + (any_spec,) * nr,
        out_shape=(jax.ShapeDtypeStruct((t, d), F32), jax.ShapeDtypeStruct((1, d), F32))
        + _exchange_shapes(ride, ride_modes),
        scratch_shapes=_exchange_sems(nr),
        compiler_params=_params("arbitrary"),
    )(*dparts, w_all, h, wnorm, dres, *ride)


def _inproj_bwd_w(ynt, dparts, s, name):
    d, t = ynt.shape
    per = d // s
    n_sh = len(dparts) * per
    tk = next(c for c in K_TILES if t % c == 0)
    last = t // tk - 1

    def body(ynt_ref, d0_ref, d1_ref, d2_ref, d3_ref, dw_ref, acc):
        p, i = pl.program_id(0), pl.program_id(1)

        @pl.when(i == 0)
        def _():
            acc[...] = jnp.zeros_like(acc)

        for a, piece in enumerate((d0_ref, d1_ref, d2_ref, d3_ref)):
            @pl.when(p // per == a)
            def _():
                acc[...] += _dot(ynt_ref[...], piece[...])

        @pl.when(i == last)
        def _():
            dw_ref[...] = acc[...].astype(BF16)

    def piece_spec(a):
        return pl.BlockSpec((tk, s), lambda p, i: (jnp.where(p // per == a, i, 0),
                                                   jnp.where(p // per == a, p % per, 0)))

    return _pcall(
        body, name=name, grid=(n_sh, t // tk),
        in_specs=[pl.BlockSpec((d, tk), lambda p, i: (0, i))] + [piece_spec(a) for a in range(4)],
        out_specs=pl.BlockSpec((None, d, s), lambda p, i: (p, 0, 0)),
        out_shape=jax.ShapeDtypeStruct((n_sh, d, s), BF16),
        scratch_shapes=[pltpu.VMEM((d, s), F32)],
        compiler_params=_params("parallel", "arbitrary"),
    )(ynt, *dparts)


def _hgrn_common(q_ref, fz_ref, lb, b_scr, k_scr):
    fz = fz_ref[...]
    sig, nsig = _sig_pair(fz)
    f = lb + (1.0 - lb) * sig
    k = (1.0 - lb) * nsig
    rr, cc = _iota2((BLOCK, BLOCK), 0), _iota2((BLOCK, BLOCK), 1)
    b = _tri_left(_tri(cc <= rr), jnp.log(f))
    b_scr[...] = b
    k_scr[...] = k
    bend = [b_scr[pl.ds(SUB * j + SUB - 1, 1), :] for j in range(N_SUB)]
    bref = [jnp.zeros((1, HEAD), F32)] + bend[:-1]
    refrow, bendrow = _expand(bref), _expand(bend)
    e_q = jnp.exp(b - refrow)
    e_k = jnp.exp(bendrow - b)
    qt = q_ref[...] * e_q
    kh = k * e_k
    bl = bend[-1]
    return dict(sig=sig, nsig=nsig, f=f, k=k, b=b, bend=bend, bref=bref, refrow=refrow,
                e_q=e_q, e_k=e_k, qt=qt, kh=kh, bl=bl)


HEADS_PER_STEP = 4


def _lockstep(chunks):
    live = list(chunks)
    while live:
        still = []
        for gen in live:
            try:
                next(gen)
                still.append(gen)
            except StopIteration:
                pass
        live = still


def _ride_along(ride_ins, ride_outs, modes, sems, grid):
    if not ride_ins:
        return lambda: None
    ids = [pl.program_id(a) for a in range(len(grid))]
    start, wait = _exchange_copies(ride_ins, ride_outs, modes, *sems)
    first, last = ids[0] == 0, ids[0] == grid[0] - 1
    for a in range(1, len(grid)):
        first = jnp.logical_and(first, ids[a] == 0)
        last = jnp.logical_and(last, ids[a] == grid[a] - 1)
    pl.when(first)(start)
    return lambda: pl.when(last)(wait)


def _hgrn_fwd(main, lbrow, bsz, n_blocks, ride, ride_modes, name):
    t, d3 = main.shape
    d = d3 // 3
    n_heads = d // HEAD
    n_pairs = n_heads // HEADS_PER_STEP
    wide = HEADS_PER_STEP * HEAD
    nr = len(ride)
    grid = (bsz, n_pairs, n_blocks)

    def chunk(q_ref, fz_ref, v_ref, lb, o_ref, st_ref, s_scr, b_scr, k_scr, o_acc):
        c = _hgrn_common(q_ref, fz_ref, lb, b_scr, k_scr)
        yield
        s_t = s_scr[...]
        st_ref[...] = s_t
        vb = v_ref[...].astype(BF16)
        q_state = c["qt"] * _expand([jnp.exp(r) for r in c["bref"]])
        o_state = _dot_nt(q_state.astype(BF16), s_t.astype(BF16))
        js = range(N_SUB - 1)
        khb = c["kh"].astype(BF16)
        lhs = [(c["qt"][SUB * (j + 1):] * jnp.exp(c["refrow"][SUB * (j + 1):] - c["bend"][j])).astype(BF16)
               for j in js]
        yield
        a_js = [_dot_nt(lhs[j], khb[SUB * j:SUB * (j + 1)]) for j in js]
        k_state = c["kh"] * _expand([jnp.exp(c["bl"] - r) for r in c["bend"]])
        s_scr[...] = s_t * jnp.exp(c["bl"]) + _dot_tn(vb, k_state.astype(BF16))
        yield
        o_js = [_dot(a_js[j].astype(BF16), vb[SUB * j:SUB * (j + 1)]) for j in js]
        yield
        o_acc[...] = o_state
        for j in js:
            o_acc[SUB * (j + 1):, :] += o_js[j]
        t_loc = _iota2((HALF, HEAD), 0)
        for i in range(N_SUB):
            r0 = SUB * i
            q_h = [q_ref[pl.ds(r0 + HALF * u, HALF), :] for u in range(2)]
            b_h = [b_scr[pl.ds(r0 + HALF * u, HALF), :] for u in range(2)]
            o_h = [jnp.zeros((HALF, HEAD), F32) for _ in range(2)]
            for s in range(SUB):
                brow = b_scr[pl.ds(r0 + s, 1), :]
                krow = k_scr[pl.ds(r0 + s, 1), :]
                vrow = v_ref[pl.ds(r0 + s, 1), :]
                for u in range(s // HALF, 2):
                    diff = b_h[u] - brow
                    if u == s // HALF:
                        diff = jnp.where(t_loc >= s - HALF * u, diff, NEG_BIG)
                    col = jnp.sum(q_h[u] * krow * jnp.exp(diff), axis=-1, keepdims=True)
                    o_h[u] = o_h[u] + col * vrow
            for u in range(2):
                o_acc[pl.ds(r0 + HALF * u, HALF), :] += o_h[u]
            yield
        o_ref[...] = o_acc[...]

    def body(*refs):
        q_ref, fz_ref, v_ref, lb_ref = refs[:4]
        o_ref, st_ref = refs[4 + nr:6 + nr]
        s_scr, b_scr, k_scr, o_acc = refs[6 + 2 * nr:10 + 2 * nr]
        finish_ride = _ride_along(refs[4:4 + nr], refs[6 + nr:6 + 2 * nr], ride_modes, refs[10 + 2 * nr:], grid)

        @pl.when(pl.program_id(2) == 0)
        def _():
            s_scr[...] = jnp.zeros_like(s_scr)

        def head(hh):
            cols = pl.ds(hh * HEAD, HEAD)
            return chunk(q_ref.at[:, cols], fz_ref.at[:, cols], v_ref.at[:, cols], lb_ref[:, cols],
                         o_ref.at[:, cols], st_ref.at[hh], s_scr.at[hh], b_scr.at[hh], k_scr.at[hh], o_acc.at[hh])

        _lockstep([head(hh) for hh in range(HEADS_PER_STEP)])
        finish_ride()

    def blk(col0):
        return pl.BlockSpec((BLOCK, wide), lambda b, h, n: (b * n_blocks + n, col0 + h))

    any_spec = pl.BlockSpec(memory_space=pl.ANY)
    per_head = pltpu.VMEM((HEADS_PER_STEP, BLOCK, HEAD), F32)
    return _pcall(
        body, name=name, grid=grid,
        in_specs=[blk(0), blk(n_pairs), blk(2 * n_pairs), pl.BlockSpec((1, wide), lambda b, h, n: (0, h))]
        + [any_spec] * nr,
        out_specs=(blk(0), pl.BlockSpec((None, HEADS_PER_STEP, HEAD, HEAD), lambda b, h, n: (b * n_blocks + n, h, 0, 0)))
        + (any_spec,) * nr,
        out_shape=(jax.ShapeDtypeStruct((t, d), F32),
                   jax.ShapeDtypeStruct((bsz * n_blocks, n_heads, HEAD, HEAD), F32))
        + _exchange_shapes(ride, ride_modes),
        scratch_shapes=[per_head] * 4 + _exchange_sems(nr),
        compiler_params=_params("arbitrary", "arbitrary", "arbitrary"),
    )(main, main, main, lbrow, *ride)


def _hgrn_bwd(main, lbrow, states, do, bsz, n_blocks, ride, ride_modes, name):
    t, d3 = main.shape
    d = d3 // 3
    n_pairs = d // HEAD // HEADS_PER_STEP
    wide = HEADS_PER_STEP * HEAD
    nr = len(ride)
    grid = (bsz, n_pairs, n_blocks)

    def chunk(n, q_ref, fz_ref, v_ref, lb, st_ref, do_ref, dq_ref, dfz_ref, dv_ref, dlb_ref,
              ds_scr, b_scr, k_scr, dqt_acc, dkh_acc, dv_acc, dqd_acc, dkd_acc, ad_scr):
        c = _hgrn_common(q_ref, fz_ref, lb, b_scr, k_scr)
        yield
        q, k = q_ref[...], c["k"]
        vb = v_ref[...].astype(BF16)
        dob = do_ref[...].astype(BF16)
        s0_t = st_ref[...]
        ds1_t = ds_scr[...]
        e_ref = _expand([jnp.exp(r) for r in c["bref"]])
        e_end = _expand([jnp.exp(c["bl"] - r) for r in c["bend"]])
        e_bl = jnp.exp(c["bl"])
        q_state = c["qt"] * e_ref
        k_state = c["kh"] * e_end
        dq_state = _dot(dob, s0_t.astype(BF16))
        dk_state = _dot(vb, ds1_t.astype(BF16))
        dv_state = _dot_nt(k_state.astype(BF16), ds1_t.astype(BF16))
        ds_scr[...] = ds1_t * e_bl + _dot_tn(dob, q_state.astype(BF16))
        js = range(N_SUB - 1)
        lo = [slice(SUB * j, SUB * (j + 1)) for j in js]
        khb = c["kh"].astype(BF16)
        dj = [jnp.exp(c["refrow"][SUB * (j + 1):] - c["bend"][j]) for j in js]
        lhs = [(c["qt"][SUB * (j + 1):] * dj[j]).astype(BF16) for j in js]
        yield
        a_js = [_dot_nt(lhs[j], khb[lo[j]]) for j in js]
        da_js = [_dot_nt(dob[SUB * (j + 1):], vb[lo[j]]).astype(BF16) for j in js]
        dqt_acc[...] = dq_state * e_ref
        dkh_acc[...] = dk_state * e_end
        dv_acc[...] = dv_state
        dbl = (jnp.sum(s0_t * ds1_t, axis=0, keepdims=True) * e_bl
               + jnp.sum(k_state * dk_state, axis=0, keepdims=True))
        yield
        dv_js = [_dot_tn(a_js[j].astype(BF16), dob[SUB * (j + 1):]) for j in js]
        dq_js = [_dot(da_js[j], khb[lo[j]]) * dj[j] for j in js]
        dk_js = [_dot_tn(da_js[j], lhs[j]) for j in js]
        yield
        for j in js:
            dv_acc[lo[j], :] += dv_js[j]
            dqt_acc[SUB * (j + 1):, :] += dq_js[j]
            dkh_acc[lo[j], :] += dk_js[j]
        t_loc = _iota2((HALF, HEAD), 0)
        lane = _iota2((HALF, HEAD), 1)
        for i in range(N_SUB):
            r0 = SUB * i
            q_h = [q_ref[pl.ds(r0 + HALF * u, HALF), :] for u in range(2)]
            b_h = [b_scr[pl.ds(r0 + HALF * u, HALF), :] for u in range(2)]
            do_h = [do_ref[pl.ds(r0 + HALF * u, HALF), :] for u in range(2)]
            zero = jnp.zeros((HALF, HEAD), F32)
            dq_h, dk_h, a_h = [zero, zero], [zero, zero], [zero, zero]
            for s in range(SUB):
                brow = b_scr[pl.ds(r0 + s, 1), :]
                krow = k_scr[pl.ds(r0 + s, 1), :]
                vrow = v_ref[pl.ds(r0 + s, 1), :]
                dk_row = jnp.zeros((1, HEAD), F32)
                for u in range(s // HALF, 2):
                    diff = b_h[u] - brow
                    if u == s // HALF:
                        diff = jnp.where(t_loc >= s - HALF * u, diff, NEG_BIG)
                    w = jnp.exp(diff)
                    qw = q_h[u] * w
                    a_col = jnp.sum(qw * krow, axis=-1, keepdims=True)
                    da_col = jnp.sum(do_h[u] * vrow, axis=-1, keepdims=True)
                    a_h[u] = jnp.where(lane == r0 + s, a_col, a_h[u])
                    dq_h[u] = dq_h[u] + da_col * (w * krow)
                    dk_row = dk_row + jnp.sum(da_col * qw, axis=0, keepdims=True)
                us = s // HALF
                dk_h[us] = jnp.where(t_loc == s - HALF * us, dk_row, dk_h[us])
            for u in range(2):
                rows = pl.ds(r0 + HALF * u, HALF)
                dqd_acc[rows, :] = dq_h[u]
                dkd_acc[rows, :] = dk_h[u]
                ad_scr[rows, :] = a_h[u]
            yield
        dv_in = _dot_tn(ad_scr[...].astype(BF16), dob)
        dq = dqt_acc[...] * c["e_q"] + dqd_acc[...]
        dk = dkh_acc[...] * c["e_k"] + dkd_acc[...]
        rr, cc = _iota2((BLOCK, BLOCK), 0), _iota2((BLOCK, BLOCK), 1)
        db = q * dq - k * dk + jnp.where(_iota2((BLOCK, HEAD), 0) == BLOCK - 1, dbl, 0.0)
        yield
        dg = _tri_left(_tri(cc >= rr), db)
        yield
        real = jnp.logical_or(n > 0, _iota2((BLOCK, HEAD), 0) >= N_PAD)
        df = jnp.where(real, dg / c["f"] - dk, 0.0)
        dq_ref[...] = dq.astype(BF16)
        dv_ref[...] = (dv_acc[...] + dv_in).astype(BF16)
        dfz_ref[...] = (df * (1.0 - lb) * c["sig"] * c["nsig"]).astype(BF16)
        dlb_ref[...] += jnp.sum(df * c["nsig"], axis=0, keepdims=True)

    def body(*refs):
        q_ref, fz_ref, v_ref, lb_ref, st_ref, do_ref = refs[:6]
        dq_ref, dfz_ref, dv_ref, dlb_ref = refs[6 + nr:10 + nr]
        scratch = refs[10 + 2 * nr:19 + 2 * nr]
        finish_ride = _ride_along(refs[6:6 + nr], refs[10 + nr:10 + 2 * nr], ride_modes, refs[19 + 2 * nr:], grid)
        step = pl.program_id(2)

        @pl.when(step == 0)
        def _():
            scratch[0][...] = jnp.zeros_like(scratch[0])
            dlb_ref[...] = jnp.zeros_like(dlb_ref)

        def head(hh):
            cols = pl.ds(hh * HEAD, HEAD)
            return chunk(n_blocks - 1 - step, q_ref.at[:, cols], fz_ref.at[:, cols], v_ref.at[:, cols],
                         lb_ref[:, cols], st_ref.at[hh], do_ref.at[:, cols], dq_ref.at[:, cols],
                         dfz_ref.at[:, cols], dv_ref.at[:, cols], dlb_ref.at[:, cols],
                         *[scr.at[hh] for scr in scratch])

        _lockstep([head(hh) for hh in range(HEADS_PER_STEP)])
        finish_ride()

    def blk(col0):
        return pl.BlockSpec((BLOCK, wide), lambda b, h, s: (b * n_blocks + n_blocks - 1 - s, col0 + h))

    any_spec = pl.BlockSpec(memory_space=pl.ANY)
    per_head = pltpu.VMEM((HEADS_PER_STEP, BLOCK, HEAD), F32)
    return _pcall(
        body, name=name, grid=grid,
        in_specs=[blk(0), blk(n_pairs), blk(2 * n_pairs), pl.BlockSpec((1, wide), lambda b, h, s: (0, h)),
                  pl.BlockSpec((None, HEADS_PER_STEP, HEAD, HEAD),
                               lambda b, h, s: (b * n_blocks + n_blocks - 1 - s, h, 0, 0)),
                  blk(0)] + [any_spec] * nr,
        out_specs=(blk(0), blk(0), blk(0), pl.BlockSpec((None, 1, wide), lambda b, h, s: (b, 0, h)))
        + (any_spec,) * nr,
        out_shape=(jax.ShapeDtypeStruct((t, d), BF16),) * 3 + (jax.ShapeDtypeStruct((bsz, 1, d), F32),)
        + _exchange_shapes(ride, ride_modes),
        scratch_shapes=[per_head] * 9 + _exchange_sems(nr),
        compiler_params=_params("arbitrary", "arbitrary", "arbitrary"),
    )(main, main, main, lbrow, states, do, *ride)


SB_GROUP = 3
SB_ROWS = SB_GROUP * BLOCK
SB_DEAD = -104.0
SB_BWD_BLOCKS = (3, 2, 1)


def _sb_tables():
    j = jnp.bitwise_and(_iota2((2 * BLOCK, 2 * BLOCK), 0), BLOCK - 1)
    s = _iota2((2 * BLOCK, 2 * BLOCK), 1)
    ones = s >= BLOCK
    return (_tri(jnp.logical_or(ones, j > s)), _tri(jnp.logical_or(ones, j <= s)),
            _tri(jnp.logical_or(ones, j < s)))


def _sums(x, table, pieces=2):
    hi = x.astype(BF16)
    if pieces == 1:
        r = _dot(hi, table[:BLOCK])
    else:
        lo = (x - hi.astype(F32)).astype(BF16)
        r = _dot(jnp.concatenate([hi, lo], axis=1), table)
    return r[:, :BLOCK], r[:, BLOCK:]


def _sb_logits(q, ks, scale, causal, pad_row):
    z = _dot_nt(q, ks) * scale
    log_keep = -(jnp.maximum(z, 0.0) + jnp.log(1.0 + jnp.exp(-jnp.abs(z))))
    log_beta = z + log_keep
    if causal is not None:
        log_keep = jnp.where(causal, log_keep, 0.0)
    if pad_row is not None:
        log_keep = log_keep * pad_row
    return z, log_beta, log_keep


def _sb_fwd(qkv, bsz, n_blocks, name):
    t, d3 = qkv.shape
    d = d3 // 3
    n_heads = d // HEAD
    lp = n_blocks * BLOCK
    n_groups = n_blocks // SB_GROUP
    assert n_groups * SB_GROUP == n_blocks
    scale = HEAD ** -0.5

    def body(q_ref, k_ref, v_ref, upper_ref, o_ref, tot_ref, stop_ref, c_scr):
        g = pl.program_id(2)
        upper = upper_ref[...]
        causal = _iota2((BLOCK, BLOCK), 1) < _iota2((BLOCK, BLOCK), 0)
        lane = _iota2((1, BLOCK), 1)
        o_ref[...] = jnp.zeros_like(o_ref)
        c_scr[...] = jnp.zeros_like(c_scr)

        def tiles(jobs):
            qrows = [pl.ds(r * BLOCK, BLOCK) for r, _, _, _ in jobs]
            krows = [pl.ds(pl.multiple_of(m * BLOCK, BLOCK), BLOCK) for _, m, _, _ in jobs]
            lg = [_sb_logits(q_ref[qr, :], k_ref[kr, :], scale, causal if dg else None, pad_row)
                  for qr, kr, (_, _, dg, pad_row) in zip(qrows, krows, jobs)]
            sm = [_sums(x[2], upper) for x in lg]
            a_all = []
            for qr, (_, _, dg, _), (_, log_beta, _), (inner, total) in zip(qrows, jobs, lg, sm):
                c = c_scr[qr, :]
                a = jnp.exp(log_beta + c + inner)
                a_all.append(jnp.where(causal, a, 0.0) if dg else a)
                c_scr[qr, :] = c + total
            out = [_dot(a.astype(BF16), v_ref[kr, :]) for a, kr in zip(a_all, krows)]
            for qr, o_part in zip(qrows, out):
                o_ref[qr, :] += o_part

        def pad_row_of(m):
            return jnp.where(jnp.logical_or(m > 0, lane >= N_PAD), 1.0, 0.0)

        base = SB_GROUP * g
        tiles([(r, base + kb, r == kb, pad_row_of(base) if kb == 0 else None)
               for kb in range(SB_GROUP - 1, -1, -1) for r in range(kb, SB_GROUP)])

        def live(least):
            def cond(carry):
                m, c_max = carry
                return jnp.logical_and(m >= least, c_max >= SB_DEAD)
            return cond

        def step_two(carry):
            m, _ = carry
            pad_row = pad_row_of(m - 1)
            tiles([(r, m, False, None) for r in range(SB_GROUP)]
                  + [(r, m - 1, False, pad_row) for r in range(SB_GROUP)])
            return m - 2, jnp.max(c_scr[...])

        def step_one(carry):
            m, _ = carry
            pad_row = pad_row_of(m)
            tiles([(r, m, False, pad_row) for r in range(SB_GROUP)])
            return m - 1, jnp.max(c_scr[...])

        carry = lax.while_loop(live(1), step_two, (base - 1, jnp.max(c_scr[...])))
        m_end, _ = lax.while_loop(live(0), step_one, carry)
        tot_ref[...] = c_scr[...]
        stop_ref[...] = jnp.broadcast_to((m_end + 1).astype(F32), stop_ref.shape)

    qblk = pl.BlockSpec((SB_ROWS, HEAD), lambda b, h, g: (b * n_groups + g, h))
    table = pl.BlockSpec((2 * BLOCK, 2 * BLOCK), lambda b, h, g: (0, 0))
    return _pcall(
        body, name=name, grid=(bsz, n_heads, n_groups),
        in_specs=[qblk, pl.BlockSpec((lp, HEAD), lambda b, h, g: (b, n_heads + h)),
                  pl.BlockSpec((lp, HEAD), lambda b, h, g: (b, 2 * n_heads + h)), table],
        out_specs=(qblk, qblk, pl.BlockSpec((None, 8, 128), lambda b, h, g: ((b * n_heads + h) * n_groups + g, 0, 0))),
        out_shape=(jax.ShapeDtypeStruct((t, d), F32), jax.ShapeDtypeStruct((t, d), F32),
                   jax.ShapeDtypeStruct((bsz * n_heads * n_groups, 8, 128), F32)),
        scratch_shapes=[pltpu.VMEM((SB_ROWS, HEAD), F32)],
        compiler_params=_params("parallel", "parallel", "arbitrary"),
    )(qkv, qkv, qkv, _sb_tables()[0])


def _sb_bwd(qkv, do, tot, stop, bsz, n_blocks, name):
    t, d3 = qkv.shape
    d = d3 // 3
    n_heads = d // HEAD
    lp = n_blocks * BLOCK
    n_groups = n_blocks // SB_GROUP
    scale = HEAD ** -0.5

    def body(q_ref, k_ref, v_ref, do_ref, tot_ref, stop_ref, incl_ref, excl_ref, dq_ref, dk_ref, dv_ref,
             dk_acc, dv_acc, dq_acc, p_scr, e_scr, dob_scr):
        g = pl.program_id(2)

        @pl.when(g == 0)
        def _():
            dk_acc[...] = jnp.zeros_like(dk_acc)
            dv_acc[...] = jnp.zeros_like(dv_acc)

        incl, excl = incl_ref[...], excl_ref[...]
        causal = _iota2((BLOCK, BLOCK), 1) < _iota2((BLOCK, BLOCK), 0)
        lane = _iota2((1, BLOCK), 1)
        dob_scr[...] = do_ref[...].astype(BF16)
        dq_acc[...] = jnp.zeros_like(dq_acc)
        p_scr[...] = jnp.zeros_like(p_scr)
        e_scr[...] = jnp.zeros_like(e_scr)
        first = jnp.clip(jnp.max(stop_ref[...]).astype(jnp.int32), 0, SB_GROUP * g)

        def tiles(jobs):
            n_jobs = range(len(jobs))
            qrows = [pl.ds(r * BLOCK, BLOCK) for r, _, _, _ in jobs]
            krows = [pl.ds(pl.multiple_of(m * BLOCK, BLOCK), BLOCK) for _, m, _, _ in jobs]
            diag = [dg for _, _, dg, _ in jobs]
            lg = [_sb_logits(q_ref[qrows[i], :], k_ref[krows[i], :], scale, causal if diag[i] else None, jobs[i][3])
                  for i in n_jobs]
            d_a = [_dot_nt(dob_scr[qrows[i], :], v_ref[krows[i], :]) for i in n_jobs]
            sm = [_sums(lg[i][2], incl) for i in n_jobs]
            a_all = []
            for i in n_jobs:
                p = p_scr[qrows[i], :]
                a = jnp.exp(lg[i][1] + (tot_ref[qrows[i], :] - p - sm[i][0]))
                a_all.append(jnp.where(causal, a, 0.0) if diag[i] else a)
                p_scr[qrows[i], :] = p + sm[i][1]
            gr = [d_a[i] * a_all[i] for i in n_jobs]
            dv_part = [_dot_tn(a_all[i].astype(BF16), dob_scr[qrows[i], :]) for i in n_jobs]
            gs = [_sums(gr[i], excl, pieces=1) for i in n_jobs]
            dz_all = []
            for i in n_jobs:
                e = e_scr[qrows[i], :]
                dz = gr[i] - (gr[i] + e + gs[i][0]) * jnp.exp(lg[i][1])
                if diag[i]:
                    dz = jnp.where(causal, dz, 0.0)
                dz_all.append((dz * scale).astype(BF16))
                e_scr[qrows[i], :] = e + gs[i][1]
            dk_part = [_dot_tn(dz_all[i], q_ref[qrows[i], :]) for i in n_jobs]
            dq_part = [_dot(dz_all[i], k_ref[krows[i], :]) for i in n_jobs]
            for i in n_jobs:
                dv_acc[krows[i], :] += dv_part[i]
                dk_acc[krows[i], :] += dk_part[i]
                dq_acc[qrows[i], :] += dq_part[i]

        def pad_row_of(m):
            return jnp.where(jnp.logical_or(m > 0, lane >= N_PAD), 1.0, 0.0)

        def step_of(n_key_blocks):
            def step(m):
                pad_row = pad_row_of(m)
                jobs = [(r, m, False, pad_row) for r in range(SB_GROUP)]
                for ahead in range(1, n_key_blocks):
                    jobs += [(r, m + ahead, False, None) for r in range(SB_GROUP)]
                tiles(jobs)
                return m + n_key_blocks
            return step

        base = SB_GROUP * g
        m = first
        for n_key_blocks in SB_BWD_BLOCKS:
            m = lax.while_loop(lambda m, nb=n_key_blocks: m + nb - 1 < base, step_of(n_key_blocks), m)
        own = [base + kb for kb in range(SB_GROUP)]
        tiles([(r, own[kb], r == kb, pad_row_of(base) if kb == 0 else None)
               for kb in range(SB_GROUP) for r in range(kb, SB_GROUP)])
        dq_ref[...] = dq_acc[...].astype(BF16)

        @pl.when(g == n_groups - 1)
        def _():
            dk_ref[...] = dk_acc[...].astype(BF16)
            dv_ref[...] = dv_acc[...].astype(BF16)

    qblk = pl.BlockSpec((SB_ROWS, HEAD), lambda b, h, g: (b * n_groups + g, h))
    kblk = pl.BlockSpec((lp, HEAD), lambda b, h, g: (b, n_heads + h))
    vblk = pl.BlockSpec((lp, HEAD), lambda b, h, g: (b, 2 * n_heads + h))
    hblk = pl.BlockSpec((lp, HEAD), lambda b, h, g: (b, h))
    sblk = pl.BlockSpec((None, 8, 128), lambda b, h, g: ((b * n_heads + h) * n_groups + g, 0, 0))
    table = pl.BlockSpec((2 * BLOCK, 2 * BLOCK), lambda b, h, g: (0, 0))
    _, incl, excl = _sb_tables()
    return _pcall(
        body, name=name, grid=(bsz, n_heads, n_groups),
        in_specs=[qblk, kblk, vblk, qblk, qblk, sblk, table, table],
        out_specs=(qblk, hblk, hblk),
        out_shape=(jax.ShapeDtypeStruct((t, d), BF16),) * 3,
        scratch_shapes=[pltpu.VMEM((lp, HEAD), F32), pltpu.VMEM((lp, HEAD), F32)]
        + [pltpu.VMEM((SB_ROWS, HEAD), F32)] * 3 + [pltpu.VMEM((SB_ROWS, HEAD), BF16)],
        compiler_params=_params("parallel", "parallel", "arbitrary"),
    )(qkv, qkv, qkv, do, tot, stop, incl, excl)


def _adamw(w, g, m, v):
    m = ADAM_B1 * m + (1.0 - ADAM_B1) * g
    v = ADAM_B2 * v + (1.0 - ADAM_B2) * (g * g)
    m_hat = m / (1.0 - ADAM_B1 ** ADAM_STEP)
    v_hat = v / (1.0 - ADAM_B2 ** ADAM_STEP)
    delta = -ADAM_LR * (m_hat / (jnp.sqrt(v_hat) + ADAM_EPS) + ADAM_WD * w)
    return delta, m, v


def _update_sharded(w, parts, m, v, name):
    r, c = w.shape
    tr = UPDATE_ROWS if r % UPDATE_ROWS == 0 else r

    def body(w_ref, p_ref, m_ref, v_ref, g_ref, d_ref, nm_ref, nv_ref):
        g = p_ref[0].astype(F32)
        for q in range(1, N_DEV):
            g = g + p_ref[q].astype(F32)
        g_ref[...] = g
        d_ref[...], nm_ref[...], nv_ref[...] = _adamw(w_ref[...], g, m_ref[...], v_ref[...])

    row = pl.BlockSpec((tr, c), lambda i: (i, 0))
    return _pcall(
        body, name=name, grid=(r // tr,),
        in_specs=[row, pl.BlockSpec((N_DEV, tr, c), lambda i: (0, i, 0)), row, row],
        out_specs=(row,) * 4, out_shape=(jax.ShapeDtypeStruct((r, c), F32),) * 4,
        compiler_params=_params("parallel"),
    )(w, parts, m, v)


SMALL_ROWS = 8


def _pack_small(dpre0, dpre1, dpost0, dpost1, dlb, dwon, loss, name):
    d = dpre0.shape[1]
    bsz = dlb.shape[0]

    def body(a0, a1, p0, p1, lb_ref, on_ref, loss_ref, out_ref):
        out_ref[...] = jnp.zeros_like(out_ref)
        out_ref[pl.ds(0, 1), :] = a0[...]
        out_ref[pl.ds(1, 1), :] = a1[...]
        out_ref[pl.ds(2, 1), :] = p0[...]
        out_ref[pl.ds(3, 1), :] = p1[...]
        acc = lb_ref[0]
        for b in range(1, bsz):
            acc = acc + lb_ref[b]
        out_ref[pl.ds(4, 1), :] = acc
        out_ref[pl.ds(5, 1), pl.ds(0, HEAD)] = on_ref[...]
        out_ref[pl.ds(6, 1), pl.ds(0, HEAD)] = loss_ref[pl.ds(0, 1), :]

    return _pcall(body, name=name, out_shape=jax.ShapeDtypeStruct((SMALL_ROWS, d), F32))(
        dpre0, dpre1, dpost0, dpost1, dlb, dwon, loss)


def _update_small(parts, pre, post, lbw, on, moments, name):
    d = pre.shape[1]

    def body(p_ref, pre_ref, post_ref, lbw_ref, on_ref, mpre, mpost, mlb, mon, vpre, vpost, vlb, von,
             loss_ref, *outs):
        def total(r0, nr, width):
            acc = p_ref[0, pl.ds(r0, nr), pl.ds(0, width)]
            for q in range(1, N_DEV):
                acc = acc + p_ref[q, pl.ds(r0, nr), pl.ds(0, width)]
            return acc

        def put(k, w, g, m, v):
            dl, nm, nv = _adamw(w, g, m, v)
            outs[4 * k][...] = g
            outs[4 * k + 1][...] = dl
            outs[4 * k + 2][...] = nm
            outs[4 * k + 3][...] = nv

        put(0, pre_ref[...], total(0, 2, d), mpre[...], vpre[...])
        put(1, post_ref[...], total(2, 2, d), mpost[...], vpost[...])
        a0, a1 = lbw_ref[pl.ds(0, 1), :], lbw_ref[pl.ds(1, 1), :]
        mx = jnp.maximum(a0, a1)
        e0, e1 = jnp.exp(a0 - mx), jnp.exp(a1 - mx)
        p0 = e0 / (e0 + e1)
        g0 = total(4, 1, d) * p0 * (1.0 - p0)
        for r, w, g in ((0, a0, g0), (1, a1, -g0)):
            row = pl.ds(r, 1)
            dl, nm, nv = _adamw(w, g, mlb[row, :], vlb[row, :])
            outs[8][row, :] = g
            outs[9][row, :] = dl
            outs[10][row, :] = nm
            outs[11][row, :] = nv
        put(3, on_ref[...], total(5, 1, HEAD), mon[...], von[...])
        loss_ref[...] = jnp.broadcast_to(total(6, 1, HEAD), loss_ref.shape)

    shapes = []
    for w in (pre, post, lbw, on):
        shapes += [jax.ShapeDtypeStruct(w.shape, F32)] * 4
    return _pcall(body, name=name, out_shape=(jax.ShapeDtypeStruct((8, 128), F32), *shapes))(
        parts, pre, post, lbw, on, *moments)


def kernel(x, meta_tokens, pre_norm, post_norm, hgrn_w_in, hgrn_lb, hgrn_out_norm, hgrn_w_out, sb_w_in, sb_w_out, loss_target, m_meta_tokens, m_pre_norm, m_post_norm, m_hgrn_w_in, m_hgrn_lb, m_hgrn_out_norm, m_hgrn_w_out, m_sb_w_in, m_sb_w_out, v_meta_tokens, v_pre_norm, v_post_norm, v_hgrn_w_in, v_hgrn_lb, v_hgrn_out_norm, v_hgrn_w_out, v_sb_w_in, v_sb_w_out):
    bsz, seq, d = x.shape
    n_blocks = seq // BLOCK + 1
    lp = n_blocks * BLOCK
    t = bsz * lp
    s = hgrn_w_in.shape[2]
    dsh = d // N_DEV

    w_in_h, meta_all = _gather_once_per_chip([hgrn_w_in[0].astype(BF16), meta_tokens], "gather_weights")
    meta_full = jnp.transpose(meta_all, (1, 0, 2)).reshape(N_META, d)

    lbrow = jnp.cumsum(jax.nn.softmax(hgrn_lb, axis=0), axis=0)[0:1]

    main0, gate0, yn0, h0 = _norm_inproj(None, pre_norm[0:1], w_in_h, F32, "inproj_hgrn", from_x=(x, meta_full))
    o0, states, w_in_s, w_out_s, w_out_h = _hgrn_fwd(
        main0, lbrow, bsz, n_blocks,
        [sb_w_in[0].astype(BF16), sb_w_out[0].astype(BF16), hgrn_w_out[0].astype(BF16)], ["gather"] * 3, "hgrn_fwd")
    w_out_s = w_out_s.reshape(d, d)
    w_out_h = w_out_h.reshape(d, d)
    h1, y0 = _mix_out(o0, gate0, h0, w_out_h, hgrn_out_norm, post_norm[0:1], True, "mix_out_hgrn")
    main1, gate1, yn1 = _norm_inproj(h1, pre_norm[1:2], w_in_s, BF16, "inproj_sb")
    o1, tot, stop = _sb_fwd(main1, bsz, n_blocks, "sb_fwd")
    dh2, y1, loss_part = _mix_out(o1, gate1, h1, w_out_s, hgrn_out_norm, post_norm[1:2], False, "mix_out_sb",
                                  target=loss_target)

    do1, dgate1, dw_out_s, dpost1, _ = _mix_out_bwd(
        dh2, y1, o1, gate1, w_out_s, hgrn_out_norm, post_norm[1:2], False, "mix_out_sb_bwd")
    dq1, dk1, dv1 = _sb_bwd(main1, do1, tot, stop, bsz, n_blocks, "sb_bwd")
    dproj1 = (dq1, dk1, dv1, dgate1)
    dh1, dpre1 = _inproj_bwd_x(dproj1, w_in_s, h1, pre_norm[1:2], dh2, [], [], "inproj_sb_bwd_x")
    dw_in_s = _inproj_bwd_w(yn1, dproj1, s, "inproj_sb_bwd_w")

    do0, dgate0, dw_out_h, dpost0, dwon = _mix_out_bwd(
        dh1, y0, o0, gate0, w_out_h, hgrn_out_norm, post_norm[0:1], True, "mix_out_hgrn_bwd")
    dq0, dfz0, dv0, dlb, p_in_s, p_out_s, p_out_h = _hgrn_bwd(
        main0, lbrow, states, do0, bsz, n_blocks,
        [dw_in_s, dw_out_s.reshape(N_DEV, dsh, d), dw_out_h.reshape(N_DEV, dsh, d)], ["scatter"] * 3, "hgrn_bwd")
    dproj0 = (dq0, dfz0, dv0, dgate0)
    dw_in_h = _inproj_bwd_w(yn0, dproj0, s, "inproj_hgrn_bwd_w")
    dh0, dpre0, p_in_h = _inproj_bwd_x(
        dproj0, w_in_h, h0, pre_norm[0:1], dh1, [dw_in_h], ["scatter"], "inproj_hgrn_bwd_x")

    dh0 = dh0.reshape(bsz, lp, d)
    grad_x = dh0[:, BLOCK:]
    dmeta = jnp.sum(dh0[:, N_PAD:BLOCK], axis=0)
    dmeta = jnp.transpose(dmeta.reshape(N_META, N_DEV, dsh), (1, 0, 2))
    small = _pack_small(dpre0, dpre1, dpost0, dpost1, dlb, dwon, loss_part, "pack_small")

    p_meta, p_small = _exchange([dmeta, small], ["scatter", "gather"], "exchange_grads")

    u_meta = _update_sharded(meta_tokens, p_meta, m_meta_tokens, v_meta_tokens, "update_meta")
    u_in_h = _update_sharded(hgrn_w_in[0], p_in_h, m_hgrn_w_in[0], v_hgrn_w_in[0], "update_hgrn_w_in")
    u_out_h = _update_sharded(hgrn_w_out[0], p_out_h, m_hgrn_w_out[0], v_hgrn_w_out[0], "update_hgrn_w_out")
    u_in_s = _update_sharded(sb_w_in[0], p_in_s, m_sb_w_in[0], v_sb_w_in[0], "update_sb_w_in")
    u_out_s = _update_sharded(sb_w_out[0], p_out_s, m_sb_w_out[0], v_sb_w_out[0], "update_sb_w_out")
    sm = _update_small(p_small, pre_norm, post_norm, hgrn_lb, hgrn_out_norm,
                       (m_pre_norm, m_post_norm, m_hgrn_lb, m_hgrn_out_norm,
                        v_pre_norm, v_post_norm, v_hgrn_lb, v_hgrn_out_norm), "update_small")
    loss = sm[0][0, 0]
    u_pre, u_post, u_lb, u_on = sm[1:5], sm[5:9], sm[9:13], sm[13:17]

    per_w = [u_meta, u_pre, u_post, tuple(a[None] for a in u_in_h), u_lb, u_on,
             tuple(a[None] for a in u_out_h), tuple(a[None] for a in u_in_s), tuple(a[None] for a in u_out_s)]
    outs = [loss, grad_x]
    for k in range(4):
        outs += [u[k] for u in per_w]
    return tuple(outs)
```

```python
import jax
import jax.numpy as jnp
from jax import lax
from jax.experimental import pallas as pl
from jax.experimental.pallas import tpu as pltpu

F32 = jnp.float32
BF16 = jnp.bfloat16

N_DEV = 8
BLOCK = 128
N_META = 16
N_PAD = BLOCK - N_META
HEAD = 128
SUB = 16
N_SUB = BLOCK // SUB
HALF = 8
EPS = 1e-6
ROW_TILE = 3 * BLOCK
UPDATE_ROWS = 256
K_TILES = (1408, 768, 384, 128)
NEG_BIG = -1e30

ADAM_LR = 0.001
ADAM_B1 = 0.9
ADAM_B2 = 0.999
ADAM_EPS = 1e-08
ADAM_WD = 0.01
ADAM_STEP = 10

VMEM_LIMIT = 56 * 1024 * 1024


def _pcall(body, **kw):
    return pl.pallas_call(body, **kw)


def _params(*sem):
    return pltpu.CompilerParams(dimension_semantics=sem, vmem_limit_bytes=VMEM_LIMIT)


def _dot(a, b):
    return jnp.dot(a, b, preferred_element_type=F32)


def _dot_nt(a, b):
    return lax.dot_general(a, b, (((1,), (1,)), ((), ())), preferred_element_type=F32)


def _dot_tn(a, b):
    return lax.dot_general(a, b, (((0,), (0,)), ((), ())), preferred_element_type=F32)


def _split(x, pieces):
    out = []
    for _ in range(pieces):
        p = x.astype(BF16)
        out.append(p)
        x = x - p.astype(F32)
    return out


def _tri_right(x, tri, pieces=2):
    return sum(_dot(p, tri) for p in _split(x, pieces))


def _tri_left(tri, x, pieces=3):
    return sum(_dot(tri, p) for p in _split(x, pieces))


def _iota2(shape, dim):
    return lax.broadcasted_iota(jnp.int32, shape, dim)


def _tri(cond):
    return jnp.where(cond, 1.0, 0.0).astype(BF16)


def _sig_pair(x):
    e = jnp.exp(-jnp.abs(x))
    r = 1.0 / (1.0 + e)
    er = e * r
    pos = x >= 0
    return jnp.where(pos, r, er), jnp.where(pos, er, r)


def _expand(rows):
    return jnp.concatenate([jnp.broadcast_to(r, (SUB, HEAD)) for r in rows], axis=0)


def _exchange_shapes(arrays, modes):
    return tuple(jax.ShapeDtypeStruct((N_DEV,) + tuple(a.shape[1:] if m == "scatter" else a.shape), a.dtype)
                 for a, m in zip(arrays, modes))


def _exchange_sems(n):
    if n == 0:
        return []
    return [pltpu.SemaphoreType.DMA((n, N_DEV - 1)), pltpu.SemaphoreType.DMA((n, N_DEV - 1)),
            pltpu.SemaphoreType.DMA((n,))]


def _exchange_copies(ins, outs, modes, send_sems, recv_sems, local_sems):
    mx, my, mc = lax.axis_index("x"), lax.axis_index("y"), lax.axis_index("c")
    me = 4 * mx + 2 * my + mc

    def src(i, slot):
        return ins[i].at[slot] if modes[i] == "scatter" else ins[i]

    def peer_of(mask):
        px = 1 - mx if mask & 4 else mx
        py = 1 - my if mask & 2 else my
        pc = 1 - mc if mask & 1 else mc
        return px, py, pc

    def copy(i, mask, dst_slot):
        px, py, pc = peer_of(mask)
        return pltpu.make_async_remote_copy(
            src_ref=src(i, 4 * px + 2 * py + pc), dst_ref=outs[i].at[dst_slot],
            send_sem=send_sems.at[i, mask - 1], recv_sem=recv_sems.at[i, mask - 1],
            device_id=(px, py, pc), device_id_type=pl.DeviceIdType.MESH)

    n = len(ins)
    sends = [copy(i, mask, me) for mask in range(1, N_DEV) for i in range(n)]
    own = [pltpu.make_async_copy(src(i, me), outs[i].at[me], local_sems.at[i]) for i in range(n)]
    arrivals = []
    for mask in range(1, N_DEV):
        px, py, pc = peer_of(mask)
        arrivals += [copy(i, mask, 4 * px + 2 * py + pc) for i in range(n)]

    def start():
        for cp in sends + own:
            cp.start()

    def wait():
        for cp in arrivals:
            cp.wait_recv()
        for cp in sends:
            cp.wait_send()
        for cp in own:
            cp.wait()

    return start, wait


def _gather_once_per_chip(arrays, name):
    n = len(arrays)

    def body(*refs):
        ins, outs = refs[:n], refs[n:2 * n]
        send_sems, recv_sems, local_sems = refs[2 * n:]
        mx, my, mc = lax.axis_index("x"), lax.axis_index("y"), lax.axis_index("c")
        me, sibling = (mx, my, mc), (mx, my, 1 - mc)
        chips = [(1 - mx, my), (mx, 1 - my), (1 - mx, 1 - my)]

        def slot(px, py, pc):
            return 4 * px + 2 * py + pc

        def copy(i, k, block, to, src=None):
            return pltpu.make_async_remote_copy(
                src_ref=outs[i].at[slot(*block)] if src is None else src, dst_ref=outs[i].at[slot(*block)],
                send_sem=send_sems.at[i, k], recv_sem=recv_sems.at[i, k],
                device_id=to, device_id_type=pl.DeviceIdType.MESH)

        own = [pltpu.make_async_copy(ins[i], outs[i].at[slot(*me)], local_sems.at[i]) for i in range(n)]
        first = [copy(i, 0, me, sibling, src=ins[i]) for i in range(n)]
        first += [copy(i, 1 + j, me, (*chip, mc), src=ins[i]) for j, chip in enumerate(chips) for i in range(n)]
        for cp in own + first:
            cp.start()
        passed = []
        for j, chip in enumerate(chips):
            for i in range(n):
                copy(i, 1 + j, (*chip, mc), me).wait_recv()
                cp = copy(i, 4 + j, (*chip, mc), sibling)
                cp.start()
                passed.append(cp)
        for i in range(n):
            copy(i, 0, sibling, me).wait_recv()
            for j, chip in enumerate(chips):
                copy(i, 4 + j, (*chip, 1 - mc), me).wait_recv()
        for cp in first + passed:
            cp.wait_send()
        for cp in own:
            cp.wait()

    any_spec = pl.BlockSpec(memory_space=pl.ANY)
    return _pcall(
        body, name=name, out_shape=_exchange_shapes(arrays, ["gather"] * n),
        in_specs=[any_spec] * n, out_specs=tuple([any_spec] * n),
        scratch_shapes=_exchange_sems(n),
    )(*arrays)


def _exchange(arrays, modes, name):
    n = len(arrays)

    def body(*refs):
        start, wait = _exchange_copies(refs[:n], refs[n:2 * n], modes, *refs[2 * n:])
        start()
        wait()

    any_spec = pl.BlockSpec(memory_space=pl.ANY)
    return _pcall(
        body, name=name, out_shape=_exchange_shapes(arrays, modes),
        in_specs=[any_spec] * n, out_specs=tuple([any_spec] * n),
        scratch_shapes=_exchange_sems(n),
    )(*arrays)


def _tile_blocks_of_x(seq, d):
    assert ROW_TILE == 3 * BLOCK and (seq + BLOCK) % ROW_TILE == 0
    per_seq = (seq + BLOCK) // ROW_TILE

    def spec(k):
        return pl.BlockSpec((None, BLOCK, d),
                            lambda i: (i // per_seq, jnp.maximum(3 * (i % per_seq) - 1 + k, 0), 0))

    return [spec(0), spec(1), spec(2)]


def _norm_inproj(h, wnorm, w_all, main_dtype, name, from_x=None):
    p_n, d, s = w_all.shape
    n_main = 3 * d // s
    tm = ROW_TILE
    if from_x is None:
        t = h.shape[0]
        lead, lead_specs = [h], [pl.BlockSpec((tm, d), lambda i: (i, 0))]
    else:
        x_in, meta = from_x
        t = x_in.shape[0] * (x_in.shape[1] + BLOCK)
        tiles_per_seq = (x_in.shape[1] + BLOCK) // tm
        lead = [x_in, x_in, x_in, meta]
        lead_specs = _tile_blocks_of_x(x_in.shape[1], d) + [pl.BlockSpec((N_META, d), lambda i: (0, 0))]
    n_lead = len(lead)

    def body(*refs):
        wn_ref, w_ref = refs[n_lead:n_lead + 2]
        main_ref, gate_ref, ynt_ref = refs[n_lead + 2:n_lead + 5]
        if from_x is None:
            x = refs[0][...]
        else:
            first_tile = pl.program_id(0) % tiles_per_seq == 0
            meta_block = jnp.concatenate([jnp.zeros((N_PAD, d), F32), refs[3][...]], axis=0)
            x = jnp.concatenate([jnp.where(first_tile, meta_block, refs[0][...]), refs[1][...], refs[2][...]], axis=0)
            refs[n_lead + 5][...] = x
        y = x * lax.rsqrt(jnp.mean(x * x, axis=-1, keepdims=True) + EPS) * wn_ref[...]
        yb = y.astype(BF16)
        ynt_ref[...] = y.T.astype(BF16)
        for p in range(p_n):
            r = _dot(yb, w_ref[p])
            if p < n_main:
                main_ref[:, p * s:(p + 1) * s] = r.astype(main_dtype)
            else:
                gate_ref[:, (p - n_main) * s:(p - n_main + 1) * s] = r

    row = pl.BlockSpec((tm, d), lambda i: (i, 0))
    extra_specs, extra_shapes = ((), ()) if from_x is None else ((row,), (jax.ShapeDtypeStruct((t, d), F32),))
    return _pcall(
        body, name=name, grid=(t // tm,),
        in_specs=lead_specs + [pl.BlockSpec((1, d), lambda i: (0, 0)), pl.BlockSpec((p_n, d, s), lambda i: (0, 0, 0))],
        out_specs=(pl.BlockSpec((tm, 3 * d), lambda i: (i, 0)), row, pl.BlockSpec((d, tm), lambda i: (0, i)))
        + extra_specs,
        out_shape=(jax.ShapeDtypeStruct((t, 3 * d), main_dtype), jax.ShapeDtypeStruct((t, d), F32),
                   jax.ShapeDtypeStruct((d, t), BF16)) + extra_shapes,
        compiler_params=_params("parallel"),
    )(*lead, wnorm, w_all)


def _mix_out(o, gate, h_in, w_out, w_on, w_post, head_norm, name, target=None):
    t, d = o.shape
    n_heads = d // HEAD
    tm = ROW_TILE
    with_loss = target is not None
    if with_loss:
        tiles_per_seq = (target.shape[1] + BLOCK) // tm

    def body(o_ref, g_ref, h_ref, w_ref, won_ref, wp_ref, *rest):
        if with_loss:
            t0_ref, t1_ref, t2_ref, hout_ref, y_ref, loss_ref, u_scr = rest
        else:
            hout_ref, y_ref, u_scr = rest
        for hh in range(n_heads):
            cs = slice(hh * HEAD, (hh + 1) * HEAD)
            oh = o_ref[:, cs]
            gt = g_ref[:, cs]
            if head_norm:
                oh = oh * lax.rsqrt(jnp.mean(oh * oh, axis=-1, keepdims=True) + EPS) * won_ref[...]
            u_scr[:, cs] = (oh * (gt * jax.nn.sigmoid(gt))).astype(BF16)
        y = _dot(u_scr[...], w_ref[...])
        y_ref[...] = y
        r = y * lax.rsqrt(jnp.mean(y * y, axis=-1, keepdims=True) + EPS) * wp_ref[...]
        h_out = h_ref[...] + r
        if not with_loss:
            hout_ref[...] = h_out
            return
        i = pl.program_id(0)

        @pl.when(i == 0)
        def _():
            loss_ref[...] = jnp.zeros_like(loss_ref)

        tgt = jnp.concatenate([t0_ref[...], t1_ref[...], t2_ref[...]], axis=0)
        real = jnp.logical_or(i % tiles_per_seq > 0, _iota2((tm, d), 0) >= BLOCK)
        err = jnp.where(real, h_out - tgt, 0.0)
        hout_ref[...] = err * (1.0 / d)
        part = jnp.sum(jnp.sum(err * err, axis=-1, keepdims=True), axis=0, keepdims=True)
        loss_ref[...] += part * (0.5 / d)

    row = pl.BlockSpec((tm, d), lambda i: (i, 0))
    in_specs = [row, row, row, pl.BlockSpec((d, d), lambda i: (0, 0)),
                pl.BlockSpec((1, HEAD), lambda i: (0, 0)), pl.BlockSpec((1, d), lambda i: (0, 0))]
    out_specs, out_shape = (row, row), (jax.ShapeDtypeStruct((t, d), F32), jax.ShapeDtypeStruct((t, d), F32))
    args = (o, gate, h_in, w_out, w_on, w_post)
    if with_loss:
        in_specs += _tile_blocks_of_x(target.shape[1], d)
        out_specs += (pl.BlockSpec((8, 128), lambda i: (0, 0)),)
        out_shape += (jax.ShapeDtypeStruct((8, 128), F32),)
        args += (target, target, target)
    return _pcall(
        body, name=name, grid=(t // tm,), in_specs=in_specs, out_specs=out_specs, out_shape=out_shape,
        scratch_shapes=[pltpu.VMEM((tm, d), BF16)],
        compiler_params=_params("arbitrary" if with_loss else "parallel"),
    )(*args)


def _mix_out_bwd(dh, y, o, gate, w_out, w_on, w_post, head_norm, name):
    t, d = o.shape
    n_heads = d // HEAD
    tm = ROW_TILE
    last = t // tm - 1

    def body(dh_ref, y_ref, o_ref, g_ref, w_ref, won_ref, wp_ref,
             do_ref, dg_ref, dw_ref, dwp_ref, dwon_ref, u_scr, acc):
        i = pl.program_id(0)

        @pl.when(i == 0)
        def _():
            acc[...] = jnp.zeros_like(acc)
            dwp_ref[...] = jnp.zeros_like(dwp_ref)
            dwon_ref[...] = jnp.zeros_like(dwon_ref)

        yv = y_ref[...]
        rs = lax.rsqrt(jnp.mean(yv * yv, axis=-1, keepdims=True) + EPS)
        yh = yv * rs
        dr = dh_ref[...]
        dwp_ref[...] += jnp.sum(dr * yh, axis=0, keepdims=True)
        wd = dr * wp_ref[...]
        dy = rs * (wd - yh * jnp.mean(wd * yh, axis=-1, keepdims=True))
        dyb = dy.astype(BF16)
        du = _dot_nt(dyb, w_ref[...])
        for hh in range(n_heads):
            cs = slice(hh * HEAD, (hh + 1) * HEAD)
            oh = o_ref[:, cs]
            gt = g_ref[:, cs]
            sg = jax.nn.sigmoid(gt)
            sl = gt * sg
            duh = du[:, cs]
            if head_norm:
                rsh = lax.rsqrt(jnp.mean(oh * oh, axis=-1, keepdims=True) + EPS)
                ohat = oh * rsh
                on = ohat * won_ref[...]
            else:
                on = oh
            u_scr[:, cs] = (on * sl).astype(BF16)
            don = duh * sl
            dg_ref[:, cs] = (duh * on * (sg * (1.0 + gt * (1.0 - sg)))).astype(BF16)
            if head_norm:
                dwon_ref[...] += jnp.sum(don * ohat, axis=0, keepdims=True)
                wdn = don * won_ref[...]
                do_ref[:, cs] = rsh * (wdn - ohat * jnp.mean(wdn * ohat, axis=-1, keepdims=True))
            else:
                do_ref[:, cs] = don
        acc[...] += _dot_tn(u_scr[...], dyb)

        @pl.when(i == last)
        def _():
            dw_ref[...] = acc[...].astype(BF16)

    row = pl.BlockSpec((tm, d), lambda i: (i, 0))
    return _pcall(
        body, name=name, grid=(t // tm,),
        in_specs=[row, row, row, row, pl.BlockSpec((d, d), lambda i: (0, 0)),
                  pl.BlockSpec((1, HEAD), lambda i: (0, 0)), pl.BlockSpec((1, d), lambda i: (0, 0))],
        out_specs=(row, row, pl.BlockSpec((d, d), lambda i: (0, 0)), pl.BlockSpec((1, d), lambda i: (0, 0)),
                   pl.BlockSpec((1, HEAD), lambda i: (0, 0))),
        out_shape=(jax.ShapeDtypeStruct((t, d), F32), jax.ShapeDtypeStruct((t, d), BF16),
                   jax.ShapeDtypeStruct((d, d), BF16), jax.ShapeDtypeStruct((1, d), F32),
                   jax.ShapeDtypeStruct((1, HEAD), F32)),
        scratch_shapes=[pltpu.VMEM((tm, d), BF16), pltpu.VMEM((d, d), F32)],
        compiler_params=_params("arbitrary"),
    )(dh, y, o, gate, w_out, w_on, w_post)


def _inproj_bwd_x(dparts, w_all, h, wnorm, dres, ride, ride_modes, name):
    t, d = h.shape
    p_n, _, s = w_all.shape
    per = d // s
    tm = ROW_TILE
    nr = len(ride)
    grid = (t // tm,)

    def body(*refs):
        d0_ref, d1_ref, d2_ref, d3_ref, w_ref, h_ref, wn_ref, dres_ref = refs[:8]
        dh_ref, dwn_ref = refs[8 + nr:10 + nr]
        finish_ride = _ride_along(refs[8:8 + nr], refs[10 + nr:10 + 2 * nr], ride_modes, refs[10 + 2 * nr:], grid)
        i = pl.program_id(0)

        @pl.when(i == 0)
        def _():
            dwn_ref[...] = jnp.zeros_like(dwn_ref)

        pieces = (d0_ref, d1_ref, d2_ref, d3_ref)
        dyn = jnp.zeros((tm, d), F32)
        for p in range(p_n):
            blk = pieces[p // per][:, (p % per) * s:(p % per + 1) * s]
            dyn = dyn + _dot_nt(blk, w_ref[p])
        x = h_ref[...]
        rs = lax.rsqrt(jnp.mean(x * x, axis=-1, keepdims=True) + EPS)
        xh = x * rs
        dwn_ref[...] += jnp.sum(dyn * xh, axis=0, keepdims=True)
        wd = dyn * wn_ref[...]
        dh_ref[...] = dres_ref[...] + rs * (wd - xh * jnp.mean(wd * xh, axis=-1, keepdims=True))
        finish_ride()

    row = pl.BlockSpec((tm, d), lambda i: (i, 0))
    any_spec = pl.BlockSpec(memory_space=pl.ANY)
    return _pcall(
        body, name=name, grid=grid,
        in_specs=[row, row, row, row, pl.BlockSpec((p_n, d, s), lambda i: (0, 0, 0)),
                  row, pl.BlockSpec((1, d), lambda i: (0, 0)), row] + [any_spec] * nr,
        out_specs=(row, pl.BlockSpec((1, d), lambda i: (0, 0))) + (any_spec,) * nr,
        out_shape=(jax.ShapeDtypeStruct((t, d), F32), jax.ShapeDtypeStruct((1, d), F32))
        + _exchange_shapes(ride, ride_modes),
        scratch_shapes=_exchange_sems(nr),
        compiler_params=_params("arbitrary"),
    )(*dparts, w_all, h, wnorm, dres, *ride)


def _inproj_bwd_w(ynt, dparts, s, name):
    d, t = ynt.shape
    per = d // s
    n_sh = len(dparts) * per
    tk = next(c for c in K_TILES if t % c == 0)
    last = t // tk - 1

    def body(ynt_ref, d0_ref, d1_ref, d2_ref, d3_ref, dw_ref, acc):
        p, i = pl.program_id(0), pl.program_id(1)

        @pl.when(i == 0)
        def _():
            acc[...] = jnp.zeros_like(acc)

        for a, piece in enumerate((d0_ref, d1_ref, d2_ref, d3_ref)):
            @pl.when(p // per == a)
            def _():
                acc[...] += _dot(ynt_ref[...], piece[...])

        @pl.when(i == last)
        def _():
            dw_ref[...] = acc[...].astype(BF16)

    def piece_spec(a):
        return pl.BlockSpec((tk, s), lambda p, i: (jnp.where(p // per == a, i, 0),
                                                   jnp.where(p // per == a, p % per, 0)))

    return _pcall(
        body, name=name, grid=(n_sh, t // tk),
        in_specs=[pl.BlockSpec((d, tk), lambda p, i: (0, i))] + [piece_spec(a) for a in range(4)],
        out_specs=pl.BlockSpec((None, d, s), lambda p, i: (p, 0, 0)),
        out_shape=jax.ShapeDtypeStruct((n_sh, d, s), BF16),
        scratch_shapes=[pltpu.VMEM((d, s), F32)],
        compiler_params=_params("parallel", "arbitrary"),
    )(ynt, *dparts)


def _hgrn_common(q_ref, fz_ref, lb, b_scr, k_scr):
    fz = fz_ref[...]
    sig, nsig = _sig_pair(fz)
    f = lb + (1.0 - lb) * sig
    k = (1.0 - lb) * nsig
    rr, cc = _iota2((BLOCK, BLOCK), 0), _iota2((BLOCK, BLOCK), 1)
    b = _tri_left(_tri(cc <= rr), jnp.log(f))
    b_scr[...] = b
    k_scr[...] = k
    bend = [b_scr[pl.ds(SUB * j + SUB - 1, 1), :] for j in range(N_SUB)]
    bref = [jnp.zeros((1, HEAD), F32)] + bend[:-1]
    refrow, bendrow = _expand(bref), _expand(bend)
    e_q = jnp.exp(b - refrow)
    e_k = jnp.exp(bendrow - b)
    qt = q_ref[...] * e_q
    kh = k * e_k
    bl = bend[-1]
    return dict(sig=sig, nsig=nsig, f=f, k=k, b=b, bend=bend, bref=bref, refrow=refrow,
                e_q=e_q, e_k=e_k, qt=qt, kh=kh, bl=bl)


HEADS_PER_STEP = 8


def _lockstep(chunks):
    live = list(chunks)
    while live:
        still = []
        for gen in live:
            try:
                next(gen)
                still.append(gen)
            except StopIteration:
                pass
        live = still


def _ride_along(ride_ins, ride_outs, modes, sems, grid):
    if not ride_ins:
        return lambda: None
    ids = [pl.program_id(a) for a in range(len(grid))]
    start, wait = _exchange_copies(ride_ins, ride_outs, modes, *sems)
    first, last = ids[0] == 0, ids[0] == grid[0] - 1
    for a in range(1, len(grid)):
        first = jnp.logical_and(first, ids[a] == 0)
        last = jnp.logical_and(last, ids[a] == grid[a] - 1)
    pl.when(first)(start)
    return lambda: pl.when(last)(wait)


def _hgrn_fwd(main, lbrow, bsz, n_blocks, ride, ride_modes, name):
    t, d3 = main.shape
    d = d3 // 3
    n_heads = d // HEAD
    n_pairs = n_heads // HEADS_PER_STEP
    wide = HEADS_PER_STEP * HEAD
    nr = len(ride)
    grid = (bsz, n_pairs, n_blocks)

    def chunk(q_ref, fz_ref, v_ref, lb, o_ref, st_ref, s_scr, b_scr, k_scr, o_acc):
        c = _hgrn_common(q_ref, fz_ref, lb, b_scr, k_scr)
        yield
        s_t = s_scr[...]
        st_ref[...] = s_t
        vb = v_ref[...].astype(BF16)
        q_state = c["qt"] * _expand([jnp.exp(r) for r in c["bref"]])
        o_state = _dot_nt(q_state.astype(BF16), s_t.astype(BF16))
        js = range(N_SUB - 1)
        khb = c["kh"].astype(BF16)
        lhs = [(c["qt"][SUB * (j + 1):] * jnp.exp(c["refrow"][SUB * (j + 1):] - c["bend"][j])).astype(BF16)
               for j in js]
        yield
        a_js = [_dot_nt(lhs[j], khb[SUB * j:SUB * (j + 1)]) for j in js]
        k_state = c["kh"] * _expand([jnp.exp(c["bl"] - r) for r in c["bend"]])
        s_scr[...] = s_t * jnp.exp(c["bl"]) + _dot_tn(vb, k_state.astype(BF16))
        yield
        o_js = [_dot(a_js[j].astype(BF16), vb[SUB * j:SUB * (j + 1)]) for j in js]
        yield
        o_acc[...] = o_state
        for j in js:
            o_acc[SUB * (j + 1):, :] += o_js[j]
        t_loc = _iota2((HALF, HEAD), 0)
        for i in range(N_SUB):
            r0 = SUB * i
            q_h = [q_ref[pl.ds(r0 + HALF * u, HALF), :] for u in range(2)]
            b_h = [b_scr[pl.ds(r0 + HALF * u, HALF), :] for u in range(2)]
            o_h = [jnp.zeros((HALF, HEAD), F32) for _ in range(2)]
            for s in range(SUB):
                brow = b_scr[pl.ds(r0 + s, 1), :]
                krow = k_scr[pl.ds(r0 + s, 1), :]
                vrow = v_ref[pl.ds(r0 + s, 1), :]
                for u in range(s // HALF, 2):
                    diff = b_h[u] - brow
                    if u == s // HALF:
                        diff = jnp.where(t_loc >= s - HALF * u, diff, NEG_BIG)
                    col = jnp.sum(q_h[u] * krow * jnp.exp(diff), axis=-1, keepdims=True)
                    o_h[u] = o_h[u] + col * vrow
            for u in range(2):
                o_acc[pl.ds(r0 + HALF * u, HALF), :] += o_h[u]
            yield
        o_ref[...] = o_acc[...]

    def body(*refs):
        q_ref, fz_ref, v_ref, lb_ref = refs[:4]
        o_ref, st_ref = refs[4 + nr:6 + nr]
        s_scr, b_scr, k_scr, o_acc = refs[6 + 2 * nr:10 + 2 * nr]
        finish_ride = _ride_along(refs[4:4 + nr], refs[6 + nr:6 + 2 * nr], ride_modes, refs[10 + 2 * nr:], grid)

        @pl.when(pl.program_id(2) == 0)
        def _():
            s_scr[...] = jnp.zeros_like(s_scr)

        def head(hh):
            cols = pl.ds(hh * HEAD, HEAD)
            return chunk(q_ref.at[:, cols], fz_ref.at[:, cols], v_ref.at[:, cols], lb_ref[:, cols],
                         o_ref.at[:, cols], st_ref.at[hh], s_scr.at[hh], b_scr.at[hh], k_scr.at[hh], o_acc.at[hh])

        _lockstep([head(hh) for hh in range(HEADS_PER_STEP)])
        finish_ride()

    def blk(col0):
        return pl.BlockSpec((BLOCK, wide), lambda b, h, n: (b * n_blocks + n, col0 + h))

    any_spec = pl.BlockSpec(memory_space=pl.ANY)
    per_head = pltpu.VMEM((HEADS_PER_STEP, BLOCK, HEAD), F32)
    return _pcall(
        body, name=name, grid=grid,
        in_specs=[blk(0), blk(n_pairs), blk(2 * n_pairs), pl.BlockSpec((1, wide), lambda b, h, n: (0, h))]
        + [any_spec] * nr,
        out_specs=(blk(0), pl.BlockSpec((None, HEADS_PER_STEP, HEAD, HEAD), lambda b, h, n: (b * n_blocks + n, h, 0, 0)))
        + (any_spec,) * nr,
        out_shape=(jax.ShapeDtypeStruct((t, d), F32),
                   jax.ShapeDtypeStruct((bsz * n_blocks, n_heads, HEAD, HEAD), F32))
        + _exchange_shapes(ride, ride_modes),
        scratch_shapes=[per_head] * 4 + _exchange_sems(nr),
        compiler_params=_params("arbitrary", "arbitrary", "arbitrary"),
    )(main, main, main, lbrow, *ride)


def _hgrn_bwd(main, lbrow, states, do, bsz, n_blocks, ride, ride_modes, name):
    t, d3 = main.shape
    d = d3 // 3
    n_pairs = d // HEAD // HEADS_PER_STEP
    wide = HEADS_PER_STEP * HEAD
    nr = len(ride)
    grid = (bsz, n_pairs, n_blocks)

    def chunk(n, q_ref, fz_ref, v_ref, lb, st_ref, do_ref, dq_ref, dfz_ref, dv_ref, dlb_ref,
              ds_scr, b_scr, k_scr, dqt_acc, dkh_acc, dv_acc, dqd_acc, dkd_acc, ad_scr):
        c = _hgrn_common(q_ref, fz_ref, lb, b_scr, k_scr)
        yield
        q, k = q_ref[...], c["k"]
        vb = v_ref[...].astype(BF16)
        dob = do_ref[...].astype(BF16)
        s0_t = st_ref[...]
        ds1_t = ds_scr[...]
        e_ref = _expand([jnp.exp(r) for r in c["bref"]])
        e_end = _expand([jnp.exp(c["bl"] - r) for r in c["bend"]])
        e_bl = jnp.exp(c["bl"])
        q_state = c["qt"] * e_ref
        k_state = c["kh"] * e_end
        dq_state = _dot(dob, s0_t.astype(BF16))
        dk_state = _dot(vb, ds1_t.astype(BF16))
        dv_state = _dot_nt(k_state.astype(BF16), ds1_t.astype(BF16))
        ds_scr[...] = ds1_t * e_bl + _dot_tn(dob, q_state.astype(BF16))
        js = range(N_SUB - 1)
        lo = [slice(SUB * j, SUB * (j + 1)) for j in js]
        khb = c["kh"].astype(BF16)
        dj = [jnp.exp(c["refrow"][SUB * (j + 1):] - c["bend"][j]) for j in js]
        lhs = [(c["qt"][SUB * (j + 1):] * dj[j]).astype(BF16) for j in js]
        yield
        a_js = [_dot_nt(lhs[j], khb[lo[j]]) for j in js]
        da_js = [_dot_nt(dob[SUB * (j + 1):], vb[lo[j]]).astype(BF16) for j in js]
        dqt_acc[...] = dq_state * e_ref
        dkh_acc[...] = dk_state * e_end
        dv_acc[...] = dv_state
        dbl = (jnp.sum(s0_t * ds1_t, axis=0, keepdims=True) * e_bl
               + jnp.sum(k_state * dk_state, axis=0, keepdims=True))
        yield
        dv_js = [_dot_tn(a_js[j].astype(BF16), dob[SUB * (j + 1):]) for j in js]
        dq_js = [_dot(da_js[j], khb[lo[j]]) * dj[j] for j in js]
        dk_js = [_dot_tn(da_js[j], lhs[j]) for j in js]
        yield
        for j in js:
            dv_acc[lo[j], :] += dv_js[j]
            dqt_acc[SUB * (j + 1):, :] += dq_js[j]
            dkh_acc[lo[j], :] += dk_js[j]
        t_loc = _iota2((HALF, HEAD), 0)
        lane = _iota2((HALF, HEAD), 1)
        for i in range(N_SUB):
            r0 = SUB * i
            q_h = [q_ref[pl.ds(r0 + HALF * u, HALF), :] for u in range(2)]
            b_h = [b_scr[pl.ds(r0 + HALF * u, HALF), :] for u in range(2)]
            do_h = [do_ref[pl.ds(r0 + HALF * u, HALF), :] for u in range(2)]
            zero = jnp.zeros((HALF, HEAD), F32)
            dq_h, dk_h, a_h = [zero, zero], [zero, zero], [zero, zero]
            for s in range(SUB):
                brow = b_scr[pl.ds(r0 + s, 1), :]
                krow = k_scr[pl.ds(r0 + s, 1), :]
                vrow = v_ref[pl.ds(r0 + s, 1), :]
                dk_row = jnp.zeros((1, HEAD), F32)
                for u in range(s // HALF, 2):
                    diff = b_h[u] - brow
                    if u == s // HALF:
                        diff = jnp.where(t_loc >= s - HALF * u, diff, NEG_BIG)
                    w = jnp.exp(diff)
                    qw = q_h[u] * w
                    a_col = jnp.sum(qw * krow, axis=-1, keepdims=True)
                    da_col = jnp.sum(do_h[u] * vrow, axis=-1, keepdims=True)
                    a_h[u] = jnp.where(lane == r0 + s, a_col, a_h[u])
                    dq_h[u] = dq_h[u] + da_col * (w * krow)
                    dk_row = dk_row + jnp.sum(da_col * qw, axis=0, keepdims=True)
                us = s // HALF
                dk_h[us] = jnp.where(t_loc == s - HALF * us, dk_row, dk_h[us])
            for u in range(2):
                rows = pl.ds(r0 + HALF * u, HALF)
                dqd_acc[rows, :] = dq_h[u]
                dkd_acc[rows, :] = dk_h[u]
                ad_scr[rows, :] = a_h[u]
            yield
        dv_in = _dot_tn(ad_scr[...].astype(BF16), dob)
        dq = dqt_acc[...] * c["e_q"] + dqd_acc[...]
        dk = dkh_acc[...] * c["e_k"] + dkd_acc[...]
        rr, cc = _iota2((BLOCK, BLOCK), 0), _iota2((BLOCK, BLOCK), 1)
        db = q * dq - k * dk + jnp.where(_iota2((BLOCK, HEAD), 0) == BLOCK - 1, dbl, 0.0)
        yield
        dg = _tri_left(_tri(cc >= rr), db)
        yield
        real = jnp.logical_or(n > 0, _iota2((BLOCK, HEAD), 0) >= N_PAD)
        df = jnp.where(real, dg / c["f"] - dk, 0.0)
        dq_ref[...] = dq.astype(BF16)
        dv_ref[...] = (dv_acc[...] + dv_in).astype(BF16)
        dfz_ref[...] = (df * (1.0 - lb) * c["sig"] * c["nsig"]).astype(BF16)
        dlb_ref[...] += jnp.sum(df * c["nsig"], axis=0, keepdims=True)

    def body(*refs):
        q_ref, fz_ref, v_ref, lb_ref, st_ref, do_ref = refs[:6]
        dq_ref, dfz_ref, dv_ref, dlb_ref = refs[6 + nr:10 + nr]
        scratch = refs[10 + 2 * nr:19 + 2 * nr]
        finish_ride = _ride_along(refs[6:6 + nr], refs[10 + nr:10 + 2 * nr], ride_modes, refs[19 + 2 * nr:], grid)
        step = pl.program_id(2)

        @pl.when(step == 0)
        def _():
            scratch[0][...] = jnp.zeros_like(scratch[0])
            dlb_ref[...] = jnp.zeros_like(dlb_ref)

        def head(hh):
            cols = pl.ds(hh * HEAD, HEAD)
            return chunk(n_blocks - 1 - step, q_ref.at[:, cols], fz_ref.at[:, cols], v_ref.at[:, cols],
                         lb_ref[:, cols], st_ref.at[hh], do_ref.at[:, cols], dq_ref.at[:, cols],
                         dfz_ref.at[:, cols], dv_ref.at[:, cols], dlb_ref.at[:, cols],
                         *[scr.at[hh] for scr in scratch])

        _lockstep([head(hh) for hh in range(HEADS_PER_STEP)])
        finish_ride()

    def blk(col0):
        return pl.BlockSpec((BLOCK, wide), lambda b, h, s: (b * n_blocks + n_blocks - 1 - s, col0 + h))

    any_spec = pl.BlockSpec(memory_space=pl.ANY)
    per_head = pltpu.VMEM((HEADS_PER_STEP, BLOCK, HEAD), F32)
    return _pcall(
        body, name=name, grid=grid,
        in_specs=[blk(0), blk(n_pairs), blk(2 * n_pairs), pl.BlockSpec((1, wide), lambda b, h, s: (0, h)),
                  pl.BlockSpec((None, HEADS_PER_STEP, HEAD, HEAD),
                               lambda b, h, s: (b * n_blocks + n_blocks - 1 - s, h, 0, 0)),
                  blk(0)] + [any_spec] * nr,
        out_specs=(blk(0), blk(0), blk(0), pl.BlockSpec((None, 1, wide), lambda b, h, s: (b, 0, h)))
        + (any_spec,) * nr,
        out_shape=(jax.ShapeDtypeStruct((t, d), BF16),) * 3 + (jax.ShapeDtypeStruct((bsz, 1, d), F32),)
        + _exchange_shapes(ride, ride_modes),
        scratch_shapes=[per_head] * 9 + _exchange_sems(nr),
        compiler_params=_params("arbitrary", "arbitrary", "arbitrary"),
    )(main, main, main, lbrow, states, do, *ride)


SB_GROUP = 3
SB_ROWS = SB_GROUP * BLOCK
SB_DEAD = -104.0
SB_NEAR = ((0, 0), (1, 0), (0, 1))


def _sb_tables():
    j = jnp.bitwise_and(_iota2((2 * BLOCK, 2 * BLOCK), 0), BLOCK - 1)
    s = _iota2((2 * BLOCK, 2 * BLOCK), 1)
    ones = s >= BLOCK
    return (_tri(jnp.logical_or(ones, j > s)), _tri(jnp.logical_or(ones, j <= s)),
            _tri(jnp.logical_or(ones, j < s)))


def _sums(x, table, pieces=2):
    hi = x.astype(BF16)
    if pieces == 1:
        r = _dot(hi, table[:BLOCK])
    else:
        lo = (x - hi.astype(F32)).astype(BF16)
        r = _dot(jnp.concatenate([hi, lo], axis=1), table)
    return r[:, :BLOCK], r[:, BLOCK:]


def _sb_logits(q, ks, scale, causal, pad_row):
    z = _dot_nt(q, ks) * scale
    log_keep = -(jnp.maximum(z, 0.0) + jnp.log(1.0 + jnp.exp(-jnp.abs(z))))
    log_beta = z + log_keep
    if causal is not None:
        log_keep = jnp.where(causal, log_keep, 0.0)
    if pad_row is not None:
        log_keep = log_keep * pad_row
    return z, log_beta, log_keep


def _sb_fwd(qkv, bsz, n_blocks, name):
    t, d3 = qkv.shape
    d = d3 // 3
    n_heads = d // HEAD
    lp = n_blocks * BLOCK
    n_groups = n_blocks // SB_GROUP
    assert n_groups * SB_GROUP == n_blocks
    scale = HEAD ** -0.5

    def body(q_ref, k_ref, v_ref, upper_ref, o_ref, tot_ref, stop_ref, c_scr):
        g = pl.program_id(2)
        upper = upper_ref[...]
        causal = _iota2((BLOCK, BLOCK), 1) < _iota2((BLOCK, BLOCK), 0)
        lane = _iota2((1, BLOCK), 1)
        o_ref[...] = jnp.zeros_like(o_ref)
        c_scr[...] = jnp.zeros_like(c_scr)

        def tiles(jobs):
            qrows = [pl.ds(r * BLOCK, BLOCK) for r, _, _, _ in jobs]
            krows = [pl.ds(pl.multiple_of(m * BLOCK, BLOCK), BLOCK) for _, m, _, _ in jobs]
            lg = [_sb_logits(q_ref[qr, :], k_ref[kr, :], scale, causal if dg else None, pad_row)
                  for qr, kr, (_, _, dg, pad_row) in zip(qrows, krows, jobs)]
            sm = [_sums(x[2], upper) for x in lg]
            a_all = []
            for qr, (_, _, dg, _), (_, log_beta, _), (inner, total) in zip(qrows, jobs, lg, sm):
                c = c_scr[qr, :]
                a = jnp.exp(log_beta + c + inner)
                a_all.append(jnp.where(causal, a, 0.0) if dg else a)
                c_scr[qr, :] = c + total
            out = [_dot(a.astype(BF16), v_ref[kr, :]) for a, kr in zip(a_all, krows)]
            for qr, o_part in zip(qrows, out):
                o_ref[qr, :] += o_part

        def pad_row_of(m):
            return jnp.where(jnp.logical_or(m > 0, lane >= N_PAD), 1.0, 0.0)

        base = SB_GROUP * g
        own = [(r, base + kb, r == kb, pad_row_of(base) if kb == 0 else None)
               for kb in range(SB_GROUP - 1, -1, -1) for r in range(kb, SB_GROUP)]
        pl.when(g == 0)(lambda: tiles(own))
        pl.when(g > 0)(lambda: tiles(own + [(r, base - 1 - back, False, None) for r, back in SB_NEAR]))

        stop_ref[...] = jnp.zeros_like(stop_ref)
        for r in range(SB_GROUP):
            qrows = pl.ds(r * BLOCK, BLOCK)

            def live(carry):
                m, c_max = carry
                return jnp.logical_and(m >= 0, c_max >= SB_DEAD)

            def step(carry, r=r, qrows=qrows):
                m, _ = carry
                tiles([(r, m, False, pad_row_of(m))])
                return m - 1, jnp.max(c_scr[qrows, :])

            lowest = jnp.maximum(base - (SB_GROUP - 1 - r), 0)
            m_end, _ = lax.while_loop(live, step, (lowest - 1, jnp.max(c_scr[qrows, :])))
            stop_ref[pl.ds(r, 1), :] = jnp.broadcast_to((m_end + 1).astype(F32), (1, 128))
        tot_ref[...] = c_scr[...]

    qblk = pl.BlockSpec((SB_ROWS, HEAD), lambda b, h, g: (b * n_groups + g, h))
    table = pl.BlockSpec((2 * BLOCK, 2 * BLOCK), lambda b, h, g: (0, 0))
    return _pcall(
        body, name=name, grid=(bsz, n_heads, n_groups),
        in_specs=[qblk, pl.BlockSpec((lp, HEAD), lambda b, h, g: (b, n_heads + h)),
                  pl.BlockSpec((lp, HEAD), lambda b, h, g: (b, 2 * n_heads + h)), table],
        out_specs=(qblk, qblk, pl.BlockSpec((None, 8, 128), lambda b, h, g: ((b * n_heads + h) * n_groups + g, 0, 0))),
        out_shape=(jax.ShapeDtypeStruct((t, d), F32), jax.ShapeDtypeStruct((t, d), F32),
                   jax.ShapeDtypeStruct((bsz * n_heads * n_groups, 8, 128), F32)),
        scratch_shapes=[pltpu.VMEM((SB_ROWS, HEAD), F32)],
        compiler_params=_params("parallel", "parallel", "arbitrary"),
    )(qkv, qkv, qkv, _sb_tables()[0])


def _sb_bwd(qkv, do, tot, stop, bsz, n_blocks, name):
    t, d3 = qkv.shape
    d = d3 // 3
    n_heads = d // HEAD
    lp = n_blocks * BLOCK
    n_groups = n_blocks // SB_GROUP
    scale = HEAD ** -0.5

    def body(q_ref, k_ref, v_ref, do_ref, tot_ref, stop_ref, incl_ref, excl_ref, dq_ref, dk_ref, dv_ref,
             dk_acc, dv_acc, dq_acc, p_scr, e_scr, dob_scr):
        g = pl.program_id(2)

        @pl.when(g == 0)
        def _():
            dk_acc[...] = jnp.zeros_like(dk_acc)
            dv_acc[...] = jnp.zeros_like(dv_acc)

        incl, excl = incl_ref[...], excl_ref[...]
        causal = _iota2((BLOCK, BLOCK), 1) < _iota2((BLOCK, BLOCK), 0)
        lane = _iota2((1, BLOCK), 1)
        dob_scr[...] = do_ref[...].astype(BF16)
        dq_acc[...] = jnp.zeros_like(dq_acc)
        p_scr[...] = jnp.zeros_like(p_scr)
        e_scr[...] = jnp.zeros_like(e_scr)

        def tiles(jobs):
            n_jobs = range(len(jobs))
            qrows = [pl.ds(r * BLOCK, BLOCK) for r, _, _, _ in jobs]
            krows = [pl.ds(pl.multiple_of(m * BLOCK, BLOCK), BLOCK) for _, m, _, _ in jobs]
            diag = [dg for _, _, dg, _ in jobs]
            lg = [_sb_logits(q_ref[qrows[i], :], k_ref[krows[i], :], scale, causal if diag[i] else None, jobs[i][3])
                  for i in n_jobs]
            d_a = [_dot_nt(dob_scr[qrows[i], :], v_ref[krows[i], :]) for i in n_jobs]
            sm = [_sums(lg[i][2], incl) for i in n_jobs]
            a_all = []
            for i in n_jobs:
                p = p_scr[qrows[i], :]
                a = jnp.exp(lg[i][1] + (tot_ref[qrows[i], :] - p - sm[i][0]))
                a_all.append(jnp.where(causal, a, 0.0) if diag[i] else a)
                p_scr[qrows[i], :] = p + sm[i][1]
            gr = [d_a[i] * a_all[i] for i in n_jobs]
            dv_part = [_dot_tn(a_all[i].astype(BF16), dob_scr[qrows[i], :]) for i in n_jobs]
            gs = [_sums(gr[i], excl, pieces=1) for i in n_jobs]
            dz_all = []
            for i in n_jobs:
                e = e_scr[qrows[i], :]
                dz = gr[i] - (gr[i] + e + gs[i][0]) * jnp.exp(lg[i][1])
                if diag[i]:
                    dz = jnp.where(causal, dz, 0.0)
                dz_all.append((dz * scale).astype(BF16))
                e_scr[qrows[i], :] = e + gs[i][1]
            dk_part = [_dot_tn(dz_all[i], q_ref[qrows[i], :]) for i in n_jobs]
            dq_part = [_dot(dz_all[i], k_ref[krows[i], :]) for i in n_jobs]
            for i in n_jobs:
                dv_acc[krows[i], :] += dv_part[i]
                dk_acc[krows[i], :] += dk_part[i]
                dq_acc[qrows[i], :] += dq_part[i]

        def pad_row_of(m):
            return jnp.where(jnp.logical_or(m > 0, lane >= N_PAD), 1.0, 0.0)

        base = SB_GROUP * g
        for r in range(SB_GROUP):
            lowest = jnp.maximum(base - (SB_GROUP - 1 - r), 0)
            first = jnp.clip(jnp.max(stop_ref[pl.ds(r, 1), :]).astype(jnp.int32), 0, lowest)

            def step(m, r=r):
                tiles([(r, m, False, pad_row_of(m))])
                return m + 1

            lax.while_loop(lambda m, lowest=lowest: m < lowest, step, first)
        own = [(r, base + kb, r == kb, pad_row_of(base) if kb == 0 else None)
               for kb in range(SB_GROUP) for r in range(kb, SB_GROUP)]
        pl.when(g == 0)(lambda: tiles(own))
        pl.when(g > 0)(lambda: tiles([(r, base - 1 - back, False, None) for r, back in reversed(SB_NEAR)] + own))
        dq_ref[...] = dq_acc[...].astype(BF16)

        @pl.when(g == n_groups - 1)
        def _():
            dk_ref[...] = dk_acc[...].astype(BF16)
            dv_ref[...] = dv_acc[...].astype(BF16)

    qblk = pl.BlockSpec((SB_ROWS, HEAD), lambda b, h, g: (b * n_groups + g, h))
    kblk = pl.BlockSpec((lp, HEAD), lambda b, h, g: (b, n_heads + h))
    vblk = pl.BlockSpec((lp, HEAD), lambda b, h, g: (b, 2 * n_heads + h))
    hblk = pl.BlockSpec((lp, HEAD), lambda b, h, g: (b, h))
    sblk = pl.BlockSpec((None, 8, 128), lambda b, h, g: ((b * n_heads + h) * n_groups + g, 0, 0))
    table = pl.BlockSpec((2 * BLOCK, 2 * BLOCK), lambda b, h, g: (0, 0))
    _, incl, excl = _sb_tables()
    return _pcall(
        body, name=name, grid=(bsz, n_heads, n_groups),
        in_specs=[qblk, kblk, vblk, qblk, qblk, sblk, table, table],
        out_specs=(qblk, hblk, hblk),
        out_shape=(jax.ShapeDtypeStruct((t, d), BF16),) * 3,
        scratch_shapes=[pltpu.VMEM((lp, HEAD), F32), pltpu.VMEM((lp, HEAD), F32)]
        + [pltpu.VMEM((SB_ROWS, HEAD), F32)] * 3 + [pltpu.VMEM((SB_ROWS, HEAD), BF16)],
        compiler_params=_params("parallel", "parallel", "arbitrary"),
    )(qkv, qkv, qkv, do, tot, stop, incl, excl)


def _adamw(w, g, m, v):
    m = ADAM_B1 * m + (1.0 - ADAM_B1) * g
    v = ADAM_B2 * v + (1.0 - ADAM_B2) * (g * g)
    m_hat = m / (1.0 - ADAM_B1 ** ADAM_STEP)
    v_hat = v / (1.0 - ADAM_B2 ** ADAM_STEP)
    delta = -ADAM_LR * (m_hat / (jnp.sqrt(v_hat) + ADAM_EPS) + ADAM_WD * w)
    return delta, m, v


def _update_sharded(w, parts, m, v, name):
    r, c = w.shape
    tr = UPDATE_ROWS if r % UPDATE_ROWS == 0 else r

    def body(w_ref, p_ref, m_ref, v_ref, g_ref, d_ref, nm_ref, nv_ref):
        g = p_ref[0].astype(F32)
        for q in range(1, N_DEV):
            g = g + p_ref[q].astype(F32)
        g_ref[...] = g
        d_ref[...], nm_ref[...], nv_ref[...] = _adamw(w_ref[...], g, m_ref[...], v_ref[...])

    row = pl.BlockSpec((tr, c), lambda i: (i, 0))
    return _pcall(
        body, name=name, grid=(r // tr,),
        in_specs=[row, pl.BlockSpec((N_DEV, tr, c), lambda i: (0, i, 0)), row, row],
        out_specs=(row,) * 4, out_shape=(jax.ShapeDtypeStruct((r, c), F32),) * 4,
        compiler_params=_params("parallel"),
    )(w, parts, m, v)


SMALL_ROWS = 8


def _pack_small(dpre0, dpre1, dpost0, dpost1, dlb, dwon, loss, name):
    d = dpre0.shape[1]
    bsz = dlb.shape[0]

    def body(a0, a1, p0, p1, lb_ref, on_ref, loss_ref, out_ref):
        out_ref[...] = jnp.zeros_like(out_ref)
        out_ref[pl.ds(0, 1), :] = a0[...]
        out_ref[pl.ds(1, 1), :] = a1[...]
        out_ref[pl.ds(2, 1), :] = p0[...]
        out_ref[pl.ds(3, 1), :] = p1[...]
        acc = lb_ref[0]
        for b in range(1, bsz):
            acc = acc + lb_ref[b]
        out_ref[pl.ds(4, 1), :] = acc
        out_ref[pl.ds(5, 1), pl.ds(0, HEAD)] = on_ref[...]
        out_ref[pl.ds(6, 1), pl.ds(0, HEAD)] = loss_ref[pl.ds(0, 1), :]

    return _pcall(body, name=name, out_shape=jax.ShapeDtypeStruct((SMALL_ROWS, d), F32))(
        dpre0, dpre1, dpost0, dpost1, dlb, dwon, loss)


def _update_small(parts, pre, post, lbw, on, moments, name):
    d = pre.shape[1]

    def body(p_ref, pre_ref, post_ref, lbw_ref, on_ref, mpre, mpost, mlb, mon, vpre, vpost, vlb, von,
             loss_ref, *outs):
        def total(r0, nr, width):
            acc = p_ref[0, pl.ds(r0, nr), pl.ds(0, width)]
            for q in range(1, N_DEV):
                acc = acc + p_ref[q, pl.ds(r0, nr), pl.ds(0, width)]
            return acc

        def put(k, w, g, m, v):
            dl, nm, nv = _adamw(w, g, m, v)
            outs[4 * k][...] = g
            outs[4 * k + 1][...] = dl
            outs[4 * k + 2][...] = nm
            outs[4 * k + 3][...] = nv

        put(0, pre_ref[...], total(0, 2, d), mpre[...], vpre[...])
        put(1, post_ref[...], total(2, 2, d), mpost[...], vpost[...])
        a0, a1 = lbw_ref[pl.ds(0, 1), :], lbw_ref[pl.ds(1, 1), :]
        mx = jnp.maximum(a0, a1)
        e0, e1 = jnp.exp(a0 - mx), jnp.exp(a1 - mx)
        p0 = e0 / (e0 + e1)
        g0 = total(4, 1, d) * p0 * (1.0 - p0)
        for r, w, g in ((0, a0, g0), (1, a1, -g0)):
            row = pl.ds(r, 1)
            dl, nm, nv = _adamw(w, g, mlb[row, :], vlb[row, :])
            outs[8][row, :] = g
            outs[9][row, :] = dl
            outs[10][row, :] = nm
            outs[11][row, :] = nv
        put(3, on_ref[...], total(5, 1, HEAD), mon[...], von[...])
        loss_ref[...] = jnp.broadcast_to(total(6, 1, HEAD), loss_ref.shape)

    shapes = []
    for w in (pre, post, lbw, on):
        shapes += [jax.ShapeDtypeStruct(w.shape, F32)] * 4
    return _pcall(body, name=name, out_shape=(jax.ShapeDtypeStruct((8, 128), F32), *shapes))(
        parts, pre, post, lbw, on, *moments)


def kernel(x, meta_tokens, pre_norm, post_norm, hgrn_w_in, hgrn_lb, hgrn_out_norm, hgrn_w_out, sb_w_in, sb_w_out, loss_target, m_meta_tokens, m_pre_norm, m_post_norm, m_hgrn_w_in, m_hgrn_lb, m_hgrn_out_norm, m_hgrn_w_out, m_sb_w_in, m_sb_w_out, v_meta_tokens, v_pre_norm, v_post_norm, v_hgrn_w_in, v_hgrn_lb, v_hgrn_out_norm, v_hgrn_w_out, v_sb_w_in, v_sb_w_out):
    bsz, seq, d = x.shape
    n_blocks = seq // BLOCK + 1
    lp = n_blocks * BLOCK
    t = bsz * lp
    s = hgrn_w_in.shape[2]
    dsh = d // N_DEV

    w_in_h, meta_all = _gather_once_per_chip([hgrn_w_in[0].astype(BF16), meta_tokens], "gather_weights")
    meta_full = jnp.transpose(meta_all, (1, 0, 2)).reshape(N_META, d)

    lbrow = jnp.cumsum(jax.nn.softmax(hgrn_lb, axis=0), axis=0)[0:1]

    main0, gate0, yn0, h0 = _norm_inproj(None, pre_norm[0:1], w_in_h, F32, "inproj_hgrn", from_x=(x, meta_full))
    o0, states, w_in_s, w_out_s, w_out_h = _hgrn_fwd(
        main0, lbrow, bsz, n_blocks,
        [sb_w_in[0].astype(BF16), sb_w_out[0].astype(BF16), hgrn_w_out[0].astype(BF16)], ["gather"] * 3, "hgrn_fwd")
    w_out_s = w_out_s.reshape(d, d)
    w_out_h = w_out_h.reshape(d, d)
    h1, y0 = _mix_out(o0, gate0, h0, w_out_h, hgrn_out_norm, post_norm[0:1], True, "mix_out_hgrn")
    main1, gate1, yn1 = _norm_inproj(h1, pre_norm[1:2], w_in_s, BF16, "inproj_sb")
    o1, tot, stop = _sb_fwd(main1, bsz, n_blocks, "sb_fwd")
    dh2, y1, loss_part = _mix_out(o1, gate1, h1, w_out_s, hgrn_out_norm, post_norm[1:2], False, "mix_out_sb",
                                  target=loss_target)

    do1, dgate1, dw_out_s, dpost1, _ = _mix_out_bwd(
        dh2, y1, o1, gate1, w_out_s, hgrn_out_norm, post_norm[1:2], False, "mix_out_sb_bwd")
    dq1, dk1, dv1 = _sb_bwd(main1, do1, tot, stop, bsz, n_blocks, "sb_bwd")
    dproj1 = (dq1, dk1, dv1, dgate1)
    dh1, dpre1 = _inproj_bwd_x(dproj1, w_in_s, h1, pre_norm[1:2], dh2, [], [], "inproj_sb_bwd_x")
    dw_in_s = _inproj_bwd_w(yn1, dproj1, s, "inproj_sb_bwd_w")

    do0, dgate0, dw_out_h, dpost0, dwon = _mix_out_bwd(
        dh1, y0, o0, gate0, w_out_h, hgrn_out_norm, post_norm[0:1], True, "mix_out_hgrn_bwd")
    dq0, dfz0, dv0, dlb, p_in_s, p_out_s, p_out_h = _hgrn_bwd(
        main0, lbrow, states, do0, bsz, n_blocks,
        [dw_in_s, dw_out_s.reshape(N_DEV, dsh, d), dw_out_h.reshape(N_DEV, dsh, d)], ["scatter"] * 3, "hgrn_bwd")
    dproj0 = (dq0, dfz0, dv0, dgate0)
    dw_in_h = _inproj_bwd_w(yn0, dproj0, s, "inproj_hgrn_bwd_w")
    dh0, dpre0, p_in_h = _inproj_bwd_x(
        dproj0, w_in_h, h0, pre_norm[0:1], dh1, [dw_in_h], ["scatter"], "inproj_hgrn_bwd_x")

    dh0 = dh0.reshape(bsz, lp, d)
    grad_x = dh0[:, BLOCK:]
    dmeta = jnp.sum(dh0[:, N_PAD:BLOCK], axis=0)
    dmeta = jnp.transpose(dmeta.reshape(N_META, N_DEV, dsh), (1, 0, 2))
    small = _pack_small(dpre0, dpre1, dpost0, dpost1, dlb, dwon, loss_part, "pack_small")

    p_meta, p_small = _exchange([dmeta, small], ["scatter", "gather"], "exchange_grads")

    u_meta = _update_sharded(meta_tokens, p_meta, m_meta_tokens, v_meta_tokens, "update_meta")
    u_in_h = _update_sharded(hgrn_w_in[0], p_in_h, m_hgrn_w_in[0], v_hgrn_w_in[0], "update_hgrn_w_in")
    u_out_h = _update_sharded(hgrn_w_out[0], p_out_h, m_hgrn_w_out[0], v_hgrn_w_out[0], "update_hgrn_w_out")
    u_in_s = _update_sharded(sb_w_in[0], p_in_s, m_sb_w_in[0], v_sb_w_in[0], "update_sb_w_in")
    u_out_s = _update_sharded(sb_w_out[0], p_out_s, m_sb_w_out[0], v_sb_w_out[0], "update_sb_w_out")
    sm = _update_small(p_small, pre_norm, post_norm, hgrn_lb, hgrn_out_norm,
                       (m_pre_norm, m_post_norm, m_hgrn_lb, m_hgrn_out_norm,
                        v_pre_norm, v_post_norm, v_hgrn_lb, v_hgrn_out_norm), "update_small")
    loss = sm[0][0, 0]
    u_pre, u_post, u_lb, u_on = sm[1:5], sm[5:9], sm[9:13], sm[13:17]

    per_w = [u_meta, u_pre, u_post, tuple(a[None] for a in u_in_h), u_lb, u_on,
             tuple(a[None] for a in u_out_h), tuple(a[None] for a in u_in_s), tuple(a[None] for a in u_out_s)]
    outs = [loss, grad_x]
    for k in range(4):
        outs += [u[k] for u in per_w]
    return tuple(outs)
```

```python
import jax
import jax.numpy as jnp
from jax import lax
from jax.experimental import pallas as pl
from jax.experimental.pallas import tpu as pltpu

F32 = jnp.float32
BF16 = jnp.bfloat16

N_DEV = 8
BLOCK = 128
N_META = 16
N_PAD = BLOCK - N_META
HEAD = 128
SUB = 16
N_SUB = BLOCK // SUB
HALF = 8
EPS = 1e-6
ROW_TILE = 3 * BLOCK
UPDATE_ROWS = 256
K_TILES = (1408, 768, 384, 128)
NEG_BIG = -1e30

ADAM_LR = 0.001
ADAM_B1 = 0.9
ADAM_B2 = 0.999
ADAM_EPS = 1e-08
ADAM_WD = 0.01
ADAM_STEP = 10

VMEM_LIMIT = 56 * 1024 * 1024


def _pcall(body, **kw):
    return pl.pallas_call(body, **kw)


def _params(*sem):
    return pltpu.CompilerParams(dimension_semantics=sem, vmem_limit_bytes=VMEM_LIMIT)


def _dot(a, b):
    return jnp.dot(a, b, preferred_element_type=F32)


def _dot_nt(a, b):
    return lax.dot_general(a, b, (((1,), (1,)), ((), ())), preferred_element_type=F32)


def _dot_tn(a, b):
    return lax.dot_general(a, b, (((0,), (0,)), ((), ())), preferred_element_type=F32)


def _split(x, pieces):
    out = []
    for _ in range(pieces):
        p = x.astype(BF16)
        out.append(p)
        x = x - p.astype(F32)
    return out


def _tri_left(tri, x, pieces=3):
    return sum(_dot(tri, p) for p in _split(x, pieces))


def _iota2(shape, dim):
    return lax.broadcasted_iota(jnp.int32, shape, dim)


def _tri(cond):
    return jnp.where(cond, 1.0, 0.0).astype(BF16)


def _sig_pair(x):
    e = jnp.exp(-jnp.abs(x))
    r = 1.0 / (1.0 + e)
    er = e * r
    pos = x >= 0
    return jnp.where(pos, r, er), jnp.where(pos, er, r)


def _expand(rows):
    return jnp.concatenate([jnp.broadcast_to(r, (SUB, HEAD)) for r in rows], axis=0)


def _exchange_shapes(arrays, modes):
    return tuple(jax.ShapeDtypeStruct((N_DEV,) + tuple(a.shape[1:] if m == "scatter" else a.shape), a.dtype)
                 for a, m in zip(arrays, modes))


def _exchange_sems(n):
    if n == 0:
        return []
    return [pltpu.SemaphoreType.DMA((n, N_DEV - 1)), pltpu.SemaphoreType.DMA((n, N_DEV - 1)),
            pltpu.SemaphoreType.DMA((n,))]


def _exchange_copies(ins, outs, modes, send_sems, recv_sems, local_sems):
    mx, my, mc = lax.axis_index("x"), lax.axis_index("y"), lax.axis_index("c")
    me = 4 * mx + 2 * my + mc

    def src(i, slot):
        return ins[i].at[slot] if modes[i] == "scatter" else ins[i]

    def peer_of(mask):
        px = 1 - mx if mask & 4 else mx
        py = 1 - my if mask & 2 else my
        pc = 1 - mc if mask & 1 else mc
        return px, py, pc

    def copy(i, mask, dst_slot):
        px, py, pc = peer_of(mask)
        return pltpu.make_async_remote_copy(
            src_ref=src(i, 4 * px + 2 * py + pc), dst_ref=outs[i].at[dst_slot],
            send_sem=send_sems.at[i, mask - 1], recv_sem=recv_sems.at[i, mask - 1],
            device_id=(px, py, pc), device_id_type=pl.DeviceIdType.MESH)

    n = len(ins)
    sends = [copy(i, mask, me) for mask in range(1, N_DEV) for i in range(n)]
    own = [pltpu.make_async_copy(src(i, me), outs[i].at[me], local_sems.at[i]) for i in range(n)]
    arrivals = []
    for mask in range(1, N_DEV):
        px, py, pc = peer_of(mask)
        arrivals += [copy(i, mask, 4 * px + 2 * py + pc) for i in range(n)]

    def start():
        for cp in sends + own:
            cp.start()

    def wait():
        for cp in arrivals:
            cp.wait_recv()
        for cp in sends:
            cp.wait_send()
        for cp in own:
            cp.wait()

    return start, wait


def _gather_once_per_chip(arrays, name):
    n = len(arrays)

    def body(*refs):
        ins, outs = refs[:n], refs[n:2 * n]
        send_sems, recv_sems, local_sems = refs[2 * n:]
        mx, my, mc = lax.axis_index("x"), lax.axis_index("y"), lax.axis_index("c")
        me, sibling = (mx, my, mc), (mx, my, 1 - mc)
        chips = [(1 - mx, my), (mx, 1 - my), (1 - mx, 1 - my)]

        def slot(px, py, pc):
            return 4 * px + 2 * py + pc

        def copy(i, k, block, to, src=None):
            return pltpu.make_async_remote_copy(
                src_ref=outs[i].at[slot(*block)] if src is None else src, dst_ref=outs[i].at[slot(*block)],
                send_sem=send_sems.at[i, k], recv_sem=recv_sems.at[i, k],
                device_id=to, device_id_type=pl.DeviceIdType.MESH)

        own = [pltpu.make_async_copy(ins[i], outs[i].at[slot(*me)], local_sems.at[i]) for i in range(n)]
        first = [copy(i, 0, me, sibling, src=ins[i]) for i in range(n)]
        first += [copy(i, 1 + j, me, (*chip, mc), src=ins[i]) for j, chip in enumerate(chips) for i in range(n)]
        for cp in own + first:
            cp.start()
        passed = []
        for j, chip in enumerate(chips):
            for i in range(n):
                copy(i, 1 + j, (*chip, mc), me).wait_recv()
                cp = copy(i, 4 + j, (*chip, mc), sibling)
                cp.start()
                passed.append(cp)
        for i in range(n):
            copy(i, 0, sibling, me).wait_recv()
            for j, chip in enumerate(chips):
                copy(i, 4 + j, (*chip, 1 - mc), me).wait_recv()
        for cp in first + passed:
            cp.wait_send()
        for cp in own:
            cp.wait()

    any_spec = pl.BlockSpec(memory_space=pl.ANY)
    return _pcall(
        body, name=name, out_shape=_exchange_shapes(arrays, ["gather"] * n),
        in_specs=[any_spec] * n, out_specs=tuple([any_spec] * n),
        scratch_shapes=_exchange_sems(n),
    )(*arrays)


def _exchange(arrays, modes, name):
    n = len(arrays)

    def body(*refs):
        start, wait = _exchange_copies(refs[:n], refs[n:2 * n], modes, *refs[2 * n:])
        start()
        wait()

    any_spec = pl.BlockSpec(memory_space=pl.ANY)
    return _pcall(
        body, name=name, out_shape=_exchange_shapes(arrays, modes),
        in_specs=[any_spec] * n, out_specs=tuple([any_spec] * n),
        scratch_shapes=_exchange_sems(n),
    )(*arrays)


def _tile_blocks_of_x(seq, d):
    assert ROW_TILE == 3 * BLOCK and (seq + BLOCK) % ROW_TILE == 0
    per_seq = (seq + BLOCK) // ROW_TILE

    def spec(k):
        return pl.BlockSpec((None, BLOCK, d),
                            lambda i: (i // per_seq, jnp.maximum(3 * (i % per_seq) - 1 + k, 0), 0))

    return [spec(0), spec(1), spec(2)]


def _norm_inproj(h, wnorm, w_all, main_dtype, name, from_x=None):
    p_n, d, s = w_all.shape
    n_main = 3 * d // s
    tm = ROW_TILE
    if from_x is None:
        t = h.shape[0]
        lead, lead_specs = [h], [pl.BlockSpec((tm, d), lambda i: (i, 0))]
    else:
        x_in, meta = from_x
        t = x_in.shape[0] * (x_in.shape[1] + BLOCK)
        tiles_per_seq = (x_in.shape[1] + BLOCK) // tm
        lead = [x_in, x_in, x_in, meta]
        lead_specs = _tile_blocks_of_x(x_in.shape[1], d) + [pl.BlockSpec((N_META, d), lambda i: (0, 0))]
    n_lead = len(lead)

    def body(*refs):
        wn_ref, w_ref = refs[n_lead:n_lead + 2]
        main_ref, gate_ref, ynt_ref = refs[n_lead + 2:n_lead + 5]
        if from_x is None:
            x = refs[0][...]
        else:
            first_tile = pl.program_id(0) % tiles_per_seq == 0
            meta_block = jnp.concatenate([jnp.zeros((N_PAD, d), F32), refs[3][...]], axis=0)
            x = jnp.concatenate([jnp.where(first_tile, meta_block, refs[0][...]), refs[1][...], refs[2][...]], axis=0)
            refs[n_lead + 5][...] = x
        y = x * lax.rsqrt(jnp.mean(x * x, axis=-1, keepdims=True) + EPS) * wn_ref[...]
        yb = y.astype(BF16)
        ynt_ref[...] = y.T.astype(BF16)
        for p in range(p_n):
            r = _dot(yb, w_ref[p])
            if p < n_main:
                main_ref[:, p * s:(p + 1) * s] = r.astype(main_dtype)
            else:
                gate_ref[:, (p - n_main) * s:(p - n_main + 1) * s] = r

    row = pl.BlockSpec((tm, d), lambda i: (i, 0))
    extra_specs, extra_shapes = ((), ()) if from_x is None else ((row,), (jax.ShapeDtypeStruct((t, d), F32),))
    return _pcall(
        body, name=name, grid=(t // tm,),
        in_specs=lead_specs + [pl.BlockSpec((1, d), lambda i: (0, 0)), pl.BlockSpec((p_n, d, s), lambda i: (0, 0, 0))],
        out_specs=(pl.BlockSpec((tm, 3 * d), lambda i: (i, 0)), row, pl.BlockSpec((d, tm), lambda i: (0, i)))
        + extra_specs,
        out_shape=(jax.ShapeDtypeStruct((t, 3 * d), main_dtype), jax.ShapeDtypeStruct((t, d), F32),
                   jax.ShapeDtypeStruct((d, t), BF16)) + extra_shapes,
        compiler_params=_params("parallel"),
    )(*lead, wnorm, w_all)


def _mix_out(o, gate, h_in, w_out, w_on, w_post, head_norm, name, target=None):
    t, d = o.shape
    n_heads = d // HEAD
    tm = ROW_TILE
    with_loss = target is not None
    if with_loss:
        tiles_per_seq = (target.shape[1] + BLOCK) // tm

    def body(o_ref, g_ref, h_ref, w_ref, won_ref, wp_ref, *rest):
        if with_loss:
            t0_ref, t1_ref, t2_ref, hout_ref, y_ref, loss_ref, u_scr = rest
        else:
            hout_ref, y_ref, u_scr = rest
        for hh in range(n_heads):
            cs = slice(hh * HEAD, (hh + 1) * HEAD)
            oh = o_ref[:, cs]
            gt = g_ref[:, cs]
            if head_norm:
                oh = oh * lax.rsqrt(jnp.mean(oh * oh, axis=-1, keepdims=True) + EPS) * won_ref[...]
            u_scr[:, cs] = (oh * (gt * jax.nn.sigmoid(gt))).astype(BF16)
        y = _dot(u_scr[...], w_ref[...])
        y_ref[...] = y
        r = y * lax.rsqrt(jnp.mean(y * y, axis=-1, keepdims=True) + EPS) * wp_ref[...]
        h_out = h_ref[...] + r
        if not with_loss:
            hout_ref[...] = h_out
            return
        i = pl.program_id(0)

        @pl.when(i == 0)
        def _():
            loss_ref[...] = jnp.zeros_like(loss_ref)

        tgt = jnp.concatenate([t0_ref[...], t1_ref[...], t2_ref[...]], axis=0)
        real = jnp.logical_or(i % tiles_per_seq > 0, _iota2((tm, d), 0) >= BLOCK)
        err = jnp.where(real, h_out - tgt, 0.0)
        hout_ref[...] = err * (1.0 / d)
        part = jnp.sum(jnp.sum(err * err, axis=-1, keepdims=True), axis=0, keepdims=True)
        loss_ref[...] += part * (0.5 / d)

    row = pl.BlockSpec((tm, d), lambda i: (i, 0))
    in_specs = [row, row, row, pl.BlockSpec((d, d), lambda i: (0, 0)),
                pl.BlockSpec((1, HEAD), lambda i: (0, 0)), pl.BlockSpec((1, d), lambda i: (0, 0))]
    out_specs, out_shape = (row, row), (jax.ShapeDtypeStruct((t, d), F32), jax.ShapeDtypeStruct((t, d), F32))
    args = (o, gate, h_in, w_out, w_on, w_post)
    if with_loss:
        in_specs += _tile_blocks_of_x(target.shape[1], d)
        out_specs += (pl.BlockSpec((8, 128), lambda i: (0, 0)),)
        out_shape += (jax.ShapeDtypeStruct((8, 128), F32),)
        args += (target, target, target)
    return _pcall(
        body, name=name, grid=(t // tm,), in_specs=in_specs, out_specs=out_specs, out_shape=out_shape,
        scratch_shapes=[pltpu.VMEM((tm, d), BF16)],
        compiler_params=_params("arbitrary" if with_loss else "parallel"),
    )(*args)


def _mix_out_bwd(dh, y, o, gate, w_out, w_on, w_post, head_norm, name):
    t, d = o.shape
    n_heads = d // HEAD
    tm = ROW_TILE
    last = t // tm - 1

    def body(dh_ref, y_ref, o_ref, g_ref, w_ref, won_ref, wp_ref,
             do_ref, dg_ref, dw_ref, dwp_ref, dwon_ref, u_scr, acc):
        i = pl.program_id(0)

        @pl.when(i == 0)
        def _():
            acc[...] = jnp.zeros_like(acc)
            dwp_ref[...] = jnp.zeros_like(dwp_ref)
            dwon_ref[...] = jnp.zeros_like(dwon_ref)

        yv = y_ref[...]
        rs = lax.rsqrt(jnp.mean(yv * yv, axis=-1, keepdims=True) + EPS)
        yh = yv * rs
        dr = dh_ref[...]
        dwp_ref[...] += jnp.sum(dr * yh, axis=0, keepdims=True)
        wd = dr * wp_ref[...]
        dy = rs * (wd - yh * jnp.mean(wd * yh, axis=-1, keepdims=True))
        dyb = dy.astype(BF16)
        du = _dot_nt(dyb, w_ref[...])
        for hh in range(n_heads):
            cs = slice(hh * HEAD, (hh + 1) * HEAD)
            oh = o_ref[:, cs]
            gt = g_ref[:, cs]
            sg = jax.nn.sigmoid(gt)
            sl = gt * sg
            duh = du[:, cs]
            if head_norm:
                rsh = lax.rsqrt(jnp.mean(oh * oh, axis=-1, keepdims=True) + EPS)
                ohat = oh * rsh
                on = ohat * won_ref[...]
            else:
                on = oh
            u_scr[:, cs] = (on * sl).astype(BF16)
            don = duh * sl
            dg_ref[:, cs] = (duh * on * (sg * (1.0 + gt * (1.0 - sg)))).astype(BF16)
            if head_norm:
                dwon_ref[...] += jnp.sum(don * ohat, axis=0, keepdims=True)
                wdn = don * won_ref[...]
                do_ref[:, cs] = rsh * (wdn - ohat * jnp.mean(wdn * ohat, axis=-1, keepdims=True))
            else:
                do_ref[:, cs] = don
        acc[...] += _dot_tn(u_scr[...], dyb)

        @pl.when(i == last)
        def _():
            dw_ref[...] = acc[...].astype(BF16)

    row = pl.BlockSpec((tm, d), lambda i: (i, 0))
    return _pcall(
        body, name=name, grid=(t // tm,),
        in_specs=[row, row, row, row, pl.BlockSpec((d, d), lambda i: (0, 0)),
                  pl.BlockSpec((1, HEAD), lambda i: (0, 0)), pl.BlockSpec((1, d), lambda i: (0, 0))],
        out_specs=(row, row, pl.BlockSpec((d, d), lambda i: (0, 0)), pl.BlockSpec((1, d), lambda i: (0, 0)),
                   pl.BlockSpec((1, HEAD), lambda i: (0, 0))),
        out_shape=(jax.ShapeDtypeStruct((t, d), F32), jax.ShapeDtypeStruct((t, d), BF16),
                   jax.ShapeDtypeStruct((d, d), BF16), jax.ShapeDtypeStruct((1, d), F32),
                   jax.ShapeDtypeStruct((1, HEAD), F32)),
        scratch_shapes=[pltpu.VMEM((tm, d), BF16), pltpu.VMEM((d, d), F32)],
        compiler_params=_params("arbitrary"),
    )(dh, y, o, gate, w_out, w_on, w_post)


def _inproj_bwd_x(dparts, w_all, h, wnorm, dres, ride, ride_modes, name):
    t, d = h.shape
    p_n, _, s = w_all.shape
    per = d // s
    tm = ROW_TILE
    nr = len(ride)
    grid = (t // tm,)

    def body(*refs):
        d0_ref, d1_ref, d2_ref, d3_ref, w_ref, h_ref, wn_ref, dres_ref = refs[:8]
        dh_ref, dwn_ref = refs[8 + nr:10 + nr]
        finish_ride = _ride_along(refs[8:8 + nr], refs[10 + nr:10 + 2 * nr], ride_modes, refs[10 + 2 * nr:], grid)
        i = pl.program_id(0)

        @pl.when(i == 0)
        def _():
            dwn_ref[...] = jnp.zeros_like(dwn_ref)

        pieces = (d0_ref, d1_ref, d2_ref, d3_ref)
        dyn = jnp.zeros((tm, d), F32)
        for p in range(p_n):
            blk = pieces[p // per][:, (p % per) * s:(p % per + 1) * s]
            dyn = dyn + _dot_nt(blk, w_ref[p])
        x = h_ref[...]
        rs = lax.rsqrt(jnp.mean(x * x, axis=-1, keepdims=True) + EPS)
        xh = x * rs
        dwn_ref[...] += jnp.sum(dyn * xh, axis=0, keepdims=True)
        wd = dyn * wn_ref[...]
        dh_ref[...] = dres_ref[...] + rs * (wd - xh * jnp.mean(wd * xh, axis=-1, keepdims=True))
        finish_ride()

    row = pl.BlockSpec((tm, d), lambda i: (i, 0))
    any_spec = pl.BlockSpec(memory_space=pl.ANY)
    return _pcall(
        body, name=name, grid=grid,
        in_specs=[row, row, row, row, pl.BlockSpec((p_n, d, s), lambda i: (0, 0, 0)),
                  row, pl.BlockSpec((1, d), lambda i: (0, 0)), row] + [any_spec] * nr,
        out_specs=(row, pl.BlockSpec((1, d), lambda i: (0, 0))) + (any_spec,) * nr,
        out_shape=(jax.ShapeDtypeStruct((t, d), F32), jax.ShapeDtypeStruct((1, d), F32))
        + _exchange_shapes(ride, ride_modes),
        scratch_shapes=_exchange_sems(nr),
        compiler_params=_params("arbitrary"),
    )(*dparts, w_all, h, wnorm, dres, *ride)


def _inproj_bwd_w(ynt, dparts, s, name):
    d, t = ynt.shape
    per = d // s
    n_sh = len(dparts) * per
    tk = next(c for c in K_TILES if t % c == 0)
    last = t // tk - 1

    def body(ynt_ref, d0_ref, d1_ref, d2_ref, d3_ref, dw_ref, acc):
        p, i = pl.program_id(0), pl.program_id(1)

        @pl.when(i == 0)
        def _():
            acc[...] = jnp.zeros_like(acc)

        for a, piece in enumerate((d0_ref, d1_ref, d2_ref, d3_ref)):
            @pl.when(p // per == a)
            def _():
                acc[...] += _dot(ynt_ref[...], piece[...])

        @pl.when(i == last)
        def _():
            dw_ref[...] = acc[...].astype(BF16)

    def piece_spec(a):
        return pl.BlockSpec((tk, s), lambda p, i: (jnp.where(p // per == a, i, 0),
                                                   jnp.where(p // per == a, p % per, 0)))

    return _pcall(
        body, name=name, grid=(n_sh, t // tk),
        in_specs=[pl.BlockSpec((d, tk), lambda p, i: (0, i))] + [piece_spec(a) for a in range(4)],
        out_specs=pl.BlockSpec((None, d, s), lambda p, i: (p, 0, 0)),
        out_shape=jax.ShapeDtypeStruct((n_sh, d, s), BF16),
        scratch_shapes=[pltpu.VMEM((d, s), F32)],
        compiler_params=_params("parallel", "arbitrary"),
    )(ynt, *dparts)


def _hgrn_common(q_ref, fz_ref, lb, b_scr, k_scr):
    fz = fz_ref[...]
    sig, nsig = _sig_pair(fz)
    f = lb + (1.0 - lb) * sig
    k = (1.0 - lb) * nsig
    rr, cc = _iota2((BLOCK, BLOCK), 0), _iota2((BLOCK, BLOCK), 1)
    b = _tri_left(_tri(cc <= rr), jnp.log(f))
    b_scr[...] = b
    k_scr[...] = k
    bend = [b_scr[pl.ds(SUB * j + SUB - 1, 1), :] for j in range(N_SUB)]
    bref = [jnp.zeros((1, HEAD), F32)] + bend[:-1]
    refrow, bendrow = _expand(bref), _expand(bend)
    e_q = jnp.exp(b - refrow)
    e_k = jnp.exp(bendrow - b)
    qt = q_ref[...] * e_q
    kh = k * e_k
    bl = bend[-1]
    return dict(sig=sig, nsig=nsig, f=f, k=k, b=b, bend=bend, bref=bref, refrow=refrow,
                e_q=e_q, e_k=e_k, qt=qt, kh=kh, bl=bl)


HEADS_PER_STEP = 8


def _lockstep(chunks):
    live = list(chunks)
    while live:
        still = []
        for gen in live:
            try:
                next(gen)
                still.append(gen)
            except StopIteration:
                pass
        live = still


def _ride_along(ride_ins, ride_outs, modes, sems, grid):
    if not ride_ins:
        return lambda: None
    ids = [pl.program_id(a) for a in range(len(grid))]
    start, wait = _exchange_copies(ride_ins, ride_outs, modes, *sems)
    first, last = ids[0] == 0, ids[0] == grid[0] - 1
    for a in range(1, len(grid)):
        first = jnp.logical_and(first, ids[a] == 0)
        last = jnp.logical_and(last, ids[a] == grid[a] - 1)
    pl.when(first)(start)
    return lambda: pl.when(last)(wait)


def _hgrn_fwd(main, lbrow, bsz, n_blocks, ride, ride_modes, name):
    t, d3 = main.shape
    d = d3 // 3
    n_heads = d // HEAD
    n_pairs = n_heads // HEADS_PER_STEP
    wide = HEADS_PER_STEP * HEAD
    nr = len(ride)
    grid = (bsz, n_pairs, n_blocks)

    def chunk(q_ref, fz_ref, v_ref, lb, o_ref, st_ref, s_scr, b_scr, k_scr, o_acc):
        c = _hgrn_common(q_ref, fz_ref, lb, b_scr, k_scr)
        yield
        s_t = s_scr[...]
        st_ref[...] = s_t
        vb = v_ref[...].astype(BF16)
        q_state = c["qt"] * _expand([jnp.exp(r) for r in c["bref"]])
        o_state = _dot_nt(q_state.astype(BF16), s_t.astype(BF16))
        js = range(N_SUB - 1)
        khb = c["kh"].astype(BF16)
        lhs = [(c["qt"][SUB * (j + 1):] * jnp.exp(c["refrow"][SUB * (j + 1):] - c["bend"][j])).astype(BF16)
               for j in js]
        yield
        a_js = [_dot_nt(lhs[j], khb[SUB * j:SUB * (j + 1)]) for j in js]
        k_state = c["kh"] * _expand([jnp.exp(c["bl"] - r) for r in c["bend"]])
        s_scr[...] = s_t * jnp.exp(c["bl"]) + _dot_tn(vb, k_state.astype(BF16))
        yield
        o_js = [_dot(a_js[j].astype(BF16), vb[SUB * j:SUB * (j + 1)]) for j in js]
        yield
        o_acc[...] = o_state
        for j in js:
            o_acc[SUB * (j + 1):, :] += o_js[j]
        t_loc = _iota2((HALF, HEAD), 0)
        for i in range(N_SUB):
            r0 = SUB * i
            q_h = [q_ref[pl.ds(r0 + HALF * u, HALF), :] for u in range(2)]
            b_h = [b_scr[pl.ds(r0 + HALF * u, HALF), :] for u in range(2)]
            o_h = [jnp.zeros((HALF, HEAD), F32) for _ in range(2)]
            for s in range(SUB):
                brow = b_scr[pl.ds(r0 + s, 1), :]
                krow = k_scr[pl.ds(r0 + s, 1), :]
                vrow = v_ref[pl.ds(r0 + s, 1), :]
                for u in range(s // HALF, 2):
                    diff = b_h[u] - brow
                    if u == s // HALF:
                        diff = jnp.where(t_loc >= s - HALF * u, diff, NEG_BIG)
                    col = jnp.sum(q_h[u] * krow * jnp.exp(diff), axis=-1, keepdims=True)
                    o_h[u] = o_h[u] + col * vrow
            for u in range(2):
                o_acc[pl.ds(r0 + HALF * u, HALF), :] += o_h[u]
            yield
        o_ref[...] = o_acc[...]

    def body(*refs):
        q_ref, fz_ref, v_ref, lb_ref = refs[:4]
        o_ref, st_ref = refs[4 + nr:6 + nr]
        s_scr, b_scr, k_scr, o_acc = refs[6 + 2 * nr:10 + 2 * nr]
        finish_ride = _ride_along(refs[4:4 + nr], refs[6 + nr:6 + 2 * nr], ride_modes, refs[10 + 2 * nr:], grid)

        @pl.when(pl.program_id(2) == 0)
        def _():
            s_scr[...] = jnp.zeros_like(s_scr)

        def head(hh):
            cols = pl.ds(hh * HEAD, HEAD)
            return chunk(q_ref.at[:, cols], fz_ref.at[:, cols], v_ref.at[:, cols], lb_ref[:, cols],
                         o_ref.at[:, cols], st_ref.at[hh], s_scr.at[hh], b_scr.at[hh], k_scr.at[hh], o_acc.at[hh])

        _lockstep([head(hh) for hh in range(HEADS_PER_STEP)])
        finish_ride()

    def blk(col0):
        return pl.BlockSpec((BLOCK, wide), lambda b, h, n: (b * n_blocks + n, col0 + h))

    any_spec = pl.BlockSpec(memory_space=pl.ANY)
    per_head = pltpu.VMEM((HEADS_PER_STEP, BLOCK, HEAD), F32)
    return _pcall(
        body, name=name, grid=grid,
        in_specs=[blk(0), blk(n_pairs), blk(2 * n_pairs), pl.BlockSpec((1, wide), lambda b, h, n: (0, h))]
        + [any_spec] * nr,
        out_specs=(blk(0), pl.BlockSpec((None, HEADS_PER_STEP, HEAD, HEAD), lambda b, h, n: (b * n_blocks + n, h, 0, 0)))
        + (any_spec,) * nr,
        out_shape=(jax.ShapeDtypeStruct((t, d), F32),
                   jax.ShapeDtypeStruct((bsz * n_blocks, n_heads, HEAD, HEAD), F32))
        + _exchange_shapes(ride, ride_modes),
        scratch_shapes=[per_head] * 4 + _exchange_sems(nr),
        compiler_params=_params("arbitrary", "arbitrary", "arbitrary"),
    )(main, main, main, lbrow, *ride)


def _hgrn_bwd(main, lbrow, states, do, bsz, n_blocks, ride, ride_modes, name):
    t, d3 = main.shape
    d = d3 // 3
    n_pairs = d // HEAD // HEADS_PER_STEP
    wide = HEADS_PER_STEP * HEAD
    nr = len(ride)
    grid = (bsz, n_pairs, n_blocks)

    def chunk(n, q_ref, fz_ref, v_ref, lb, st_ref, do_ref, dq_ref, dfz_ref, dv_ref, dlb_ref,
              ds_scr, b_scr, k_scr, dqt_acc, dkh_acc, dv_acc, dqd_acc, dkd_acc, ad_scr):
        c = _hgrn_common(q_ref, fz_ref, lb, b_scr, k_scr)
        yield
        q, k = q_ref[...], c["k"]
        vb = v_ref[...].astype(BF16)
        dob = do_ref[...].astype(BF16)
        s0_t = st_ref[...]
        ds1_t = ds_scr[...]
        e_ref = _expand([jnp.exp(r) for r in c["bref"]])
        e_end = _expand([jnp.exp(c["bl"] - r) for r in c["bend"]])
        e_bl = jnp.exp(c["bl"])
        q_state = c["qt"] * e_ref
        k_state = c["kh"] * e_end
        dq_state = _dot(dob, s0_t.astype(BF16))
        dk_state = _dot(vb, ds1_t.astype(BF16))
        dv_state = _dot_nt(k_state.astype(BF16), ds1_t.astype(BF16))
        ds_scr[...] = ds1_t * e_bl + _dot_tn(dob, q_state.astype(BF16))
        js = range(N_SUB - 1)
        lo = [slice(SUB * j, SUB * (j + 1)) for j in js]
        khb = c["kh"].astype(BF16)
        dj = [jnp.exp(c["refrow"][SUB * (j + 1):] - c["bend"][j]) for j in js]
        lhs = [(c["qt"][SUB * (j + 1):] * dj[j]).astype(BF16) for j in js]
        yield
        a_js = [_dot_nt(lhs[j], khb[lo[j]]) for j in js]
        da_js = [_dot_nt(dob[SUB * (j + 1):], vb[lo[j]]).astype(BF16) for j in js]
        dqt_acc[...] = dq_state * e_ref
        dkh_acc[...] = dk_state * e_end
        dv_acc[...] = dv_state
        dbl = (jnp.sum(s0_t * ds1_t, axis=0, keepdims=True) * e_bl
               + jnp.sum(k_state * dk_state, axis=0, keepdims=True))
        yield
        dv_js = [_dot_tn(a_js[j].astype(BF16), dob[SUB * (j + 1):]) for j in js]
        dq_js = [_dot(da_js[j], khb[lo[j]]) * dj[j] for j in js]
        dk_js = [_dot_tn(da_js[j], lhs[j]) for j in js]
        yield
        for j in js:
            dv_acc[lo[j], :] += dv_js[j]
            dqt_acc[SUB * (j + 1):, :] += dq_js[j]
            dkh_acc[lo[j], :] += dk_js[j]
        t_loc = _iota2((HALF, HEAD), 0)
        lane = _iota2((HALF, HEAD), 1)
        for i in range(N_SUB):
            r0 = SUB * i
            q_h = [q_ref[pl.ds(r0 + HALF * u, HALF), :] for u in range(2)]
            b_h = [b_scr[pl.ds(r0 + HALF * u, HALF), :] for u in range(2)]
            do_h = [do_ref[pl.ds(r0 + HALF * u, HALF), :] for u in range(2)]
            zero = jnp.zeros((HALF, HEAD), F32)
            dq_h, dk_h, a_h = [zero, zero], [zero, zero], [zero, zero]
            for s in range(SUB):
                brow = b_scr[pl.ds(r0 + s, 1), :]
                krow = k_scr[pl.ds(r0 + s, 1), :]
                vrow = v_ref[pl.ds(r0 + s, 1), :]
                dk_row = jnp.zeros((1, HEAD), F32)
                for u in range(s // HALF, 2):
                    diff = b_h[u] - brow
                    if u == s // HALF:
                        diff = jnp.where(t_loc >= s - HALF * u, diff, NEG_BIG)
                    w = jnp.exp(diff)
                    qw = q_h[u] * w
                    a_col = jnp.sum(qw * krow, axis=-1, keepdims=True)
                    da_col = jnp.sum(do_h[u] * vrow, axis=-1, keepdims=True)
                    a_h[u] = jnp.where(lane == r0 + s, a_col, a_h[u])
                    dq_h[u] = dq_h[u] + da_col * (w * krow)
                    dk_row = dk_row + jnp.sum(da_col * qw, axis=0, keepdims=True)
                us = s // HALF
                dk_h[us] = jnp.where(t_loc == s - HALF * us, dk_row, dk_h[us])
            for u in range(2):
                rows = pl.ds(r0 + HALF * u, HALF)
                dqd_acc[rows, :] = dq_h[u]
                dkd_acc[rows, :] = dk_h[u]
                ad_scr[rows, :] = a_h[u]
            yield
        dv_in = _dot_tn(ad_scr[...].astype(BF16), dob)
        dq = dqt_acc[...] * c["e_q"] + dqd_acc[...]
        dk = dkh_acc[...] * c["e_k"] + dkd_acc[...]
        rr, cc = _iota2((BLOCK, BLOCK), 0), _iota2((BLOCK, BLOCK), 1)
        db = q * dq - k * dk + jnp.where(_iota2((BLOCK, HEAD), 0) == BLOCK - 1, dbl, 0.0)
        yield
        dg = _tri_left(_tri(cc >= rr), db)
        yield
        real = jnp.logical_or(n > 0, _iota2((BLOCK, HEAD), 0) >= N_PAD)
        df = jnp.where(real, dg / c["f"] - dk, 0.0)
        dq_ref[...] = dq.astype(BF16)
        dv_ref[...] = (dv_acc[...] + dv_in).astype(BF16)
        dfz_ref[...] = (df * (1.0 - lb) * c["sig"] * c["nsig"]).astype(BF16)
        dlb_ref[...] += jnp.sum(df * c["nsig"], axis=0, keepdims=True)

    def body(*refs):
        q_ref, fz_ref, v_ref, lb_ref, st_ref, do_ref = refs[:6]
        dq_ref, dfz_ref, dv_ref, dlb_ref = refs[6 + nr:10 + nr]
        scratch = refs[10 + 2 * nr:19 + 2 * nr]
        finish_ride = _ride_along(refs[6:6 + nr], refs[10 + nr:10 + 2 * nr], ride_modes, refs[19 + 2 * nr:], grid)
        step = pl.program_id(2)

        @pl.when(step == 0)
        def _():
            scratch[0][...] = jnp.zeros_like(scratch[0])
            dlb_ref[...] = jnp.zeros_like(dlb_ref)

        def head(hh):
            cols = pl.ds(hh * HEAD, HEAD)
            return chunk(n_blocks - 1 - step, q_ref.at[:, cols], fz_ref.at[:, cols], v_ref.at[:, cols],
                         lb_ref[:, cols], st_ref.at[hh], do_ref.at[:, cols], dq_ref.at[:, cols],
                         dfz_ref.at[:, cols], dv_ref.at[:, cols], dlb_ref.at[:, cols],
                         *[scr.at[hh] for scr in scratch])

        _lockstep([head(hh) for hh in range(HEADS_PER_STEP)])
        finish_ride()

    def blk(col0):
        return pl.BlockSpec((BLOCK, wide), lambda b, h, s: (b * n_blocks + n_blocks - 1 - s, col0 + h))

    any_spec = pl.BlockSpec(memory_space=pl.ANY)
    per_head = pltpu.VMEM((HEADS_PER_STEP, BLOCK, HEAD), F32)
    return _pcall(
        body, name=name, grid=grid,
        in_specs=[blk(0), blk(n_pairs), blk(2 * n_pairs), pl.BlockSpec((1, wide), lambda b, h, s: (0, h)),
                  pl.BlockSpec((None, HEADS_PER_STEP, HEAD, HEAD),
                               lambda b, h, s: (b * n_blocks + n_blocks - 1 - s, h, 0, 0)),
                  blk(0)] + [any_spec] * nr,
        out_specs=(blk(0), blk(0), blk(0), pl.BlockSpec((None, 1, wide), lambda b, h, s: (b, 0, h)))
        + (any_spec,) * nr,
        out_shape=(jax.ShapeDtypeStruct((t, d), BF16),) * 3 + (jax.ShapeDtypeStruct((bsz, 1, d), F32),)
        + _exchange_shapes(ride, ride_modes),
        scratch_shapes=[per_head] * 9 + _exchange_sems(nr),
        compiler_params=_params("arbitrary", "arbitrary", "arbitrary"),
    )(main, main, main, lbrow, states, do, *ride)


SB_GROUP = 3
SB_ROWS = SB_GROUP * BLOCK
SB_DEAD = -104.0
SB_NEAR = ((0, 0), (1, 0), (0, 1))


def _sb_tables():
    j = jnp.bitwise_and(_iota2((2 * BLOCK, 2 * BLOCK), 0), BLOCK - 1)
    s = _iota2((2 * BLOCK, 2 * BLOCK), 1)
    ones = s >= BLOCK
    return (_tri(jnp.logical_or(ones, j > s)), _tri(jnp.logical_or(ones, j <= s)),
            _tri(jnp.logical_or(ones, j < s)))


def _sums(x, table, pieces=2):
    hi = x.astype(BF16)
    if pieces == 1:
        r = _dot(hi, table[:BLOCK])
    else:
        lo = (x - hi.astype(F32)).astype(BF16)
        r = _dot(jnp.concatenate([hi, lo], axis=1), table)
    return r[:, :BLOCK], r[:, BLOCK:]


def _sb_logits(q, ks, scale, causal, pad_row):
    z = _dot_nt(q, ks) * scale
    log_keep = -(jnp.maximum(z, 0.0) + jnp.log(1.0 + jnp.exp(-jnp.abs(z))))
    log_beta = z + log_keep
    if causal is not None:
        log_keep = jnp.where(causal, log_keep, 0.0)
    if pad_row is not None:
        log_keep = log_keep * pad_row
    return z, log_beta, log_keep


def _sb_fwd(qkv, bsz, n_blocks, name):
    t, d3 = qkv.shape
    d = d3 // 3
    n_heads = d // HEAD
    lp = n_blocks * BLOCK
    n_groups = n_blocks // SB_GROUP
    assert n_groups * SB_GROUP == n_blocks
    scale = HEAD ** -0.5

    def body(q_ref, k_ref, v_ref, upper_ref, o_ref, tot_ref, stop_ref, c_scr):
        g = pl.program_id(2)
        upper = upper_ref[...]
        causal = _iota2((BLOCK, BLOCK), 1) < _iota2((BLOCK, BLOCK), 0)
        lane = _iota2((1, BLOCK), 1)
        o_ref[...] = jnp.zeros_like(o_ref)
        c_scr[...] = jnp.zeros_like(c_scr)

        def tiles(jobs):
            qrows = [pl.ds(r * BLOCK, BLOCK) for r, _, _, _ in jobs]
            krows = [pl.ds(pl.multiple_of(m * BLOCK, BLOCK), BLOCK) for _, m, _, _ in jobs]
            lg = [_sb_logits(q_ref[qr, :], k_ref[kr, :], scale, causal if dg else None, pad_row)
                  for qr, kr, (_, _, dg, pad_row) in zip(qrows, krows, jobs)]
            sm = [_sums(x[2], upper) for x in lg]
            a_all = []
            for qr, (_, _, dg, _), (_, log_beta, _), (inner, total) in zip(qrows, jobs, lg, sm):
                c = c_scr[qr, :]
                a = jnp.exp(log_beta + c + inner)
                a_all.append(jnp.where(causal, a, 0.0) if dg else a)
                c_scr[qr, :] = c + total
            out = [_dot(a.astype(BF16), v_ref[kr, :]) for a, kr in zip(a_all, krows)]
            for qr, o_part in zip(qrows, out):
                o_ref[qr, :] += o_part

        def pad_row_of(m):
            return jnp.where(jnp.logical_or(m > 0, lane >= N_PAD), 1.0, 0.0)

        base = SB_GROUP * g
        own = [(r, base + kb, r == kb, pad_row_of(base) if kb == 0 else None)
               for kb in range(SB_GROUP - 1, -1, -1) for r in range(kb, SB_GROUP)]
        pl.when(g == 0)(lambda: tiles(own))
        pl.when(g > 0)(lambda: tiles(own + [(r, base - 1 - back, False, None) for r, back in SB_NEAR]))

        stop_ref[...] = jnp.zeros_like(stop_ref)
        reach = [jnp.max(c_scr[pl.ds(r * BLOCK, BLOCK), :]) for r in range(SB_GROUP)]
        for r in range(SB_GROUP):
            qrows = pl.ds(r * BLOCK, BLOCK)

            def live(carry):
                m, c_max = carry
                return jnp.logical_and(m >= 0, c_max >= SB_DEAD)

            def step(carry, r=r, qrows=qrows):
                m, _ = carry
                tiles([(r, m, False, pad_row_of(m))])
                return m - 1, jnp.max(c_scr[qrows, :])

            lowest = jnp.maximum(base - (SB_GROUP - 1 - r), 0)
            m_end, _ = lax.while_loop(live, step, (lowest - 1, reach[r]))
            stop_ref[pl.ds(r, 1), :] = jnp.broadcast_to((m_end + 1).astype(F32), (1, 128))
        tot_ref[...] = c_scr[...]

    qblk = pl.BlockSpec((SB_ROWS, HEAD), lambda b, h, g: (b * n_groups + g, h))
    table = pl.BlockSpec((2 * BLOCK, 2 * BLOCK), lambda b, h, g: (0, 0))
    return _pcall(
        body, name=name, grid=(bsz, n_heads, n_groups),
        in_specs=[qblk, pl.BlockSpec((lp, HEAD), lambda b, h, g: (b, n_heads + h)),
                  pl.BlockSpec((lp, HEAD), lambda b, h, g: (b, 2 * n_heads + h)), table],
        out_specs=(qblk, qblk, pl.BlockSpec((None, 8, 128), lambda b, h, g: ((b * n_heads + h) * n_groups + g, 0, 0))),
        out_shape=(jax.ShapeDtypeStruct((t, d), F32), jax.ShapeDtypeStruct((t, d), F32),
                   jax.ShapeDtypeStruct((bsz * n_heads * n_groups, 8, 128), F32)),
        scratch_shapes=[pltpu.VMEM((SB_ROWS, HEAD), F32)],
        compiler_params=_params("parallel", "parallel", "arbitrary"),
    )(qkv, qkv, qkv, _sb_tables()[0])


def _sb_bwd(qkv, do, tot, stop, bsz, n_blocks, name):
    t, d3 = qkv.shape
    d = d3 // 3
    n_heads = d // HEAD
    lp = n_blocks * BLOCK
    n_groups = n_blocks // SB_GROUP
    scale = HEAD ** -0.5

    def body(q_ref, k_ref, v_ref, do_ref, tot_ref, stop_ref, incl_ref, excl_ref, dq_ref, dk_ref, dv_ref,
             dk_acc, dv_acc, dq_acc, p_scr, e_scr, dob_scr):
        g = pl.program_id(2)

        @pl.when(g == 0)
        def _():
            dk_acc[...] = jnp.zeros_like(dk_acc)
            dv_acc[...] = jnp.zeros_like(dv_acc)

        incl, excl = incl_ref[...], excl_ref[...]
        causal = _iota2((BLOCK, BLOCK), 1) < _iota2((BLOCK, BLOCK), 0)
        lane = _iota2((1, BLOCK), 1)
        dob_scr[...] = do_ref[...].astype(BF16)
        dq_acc[...] = jnp.zeros_like(dq_acc)
        p_scr[...] = jnp.zeros_like(p_scr)
        e_scr[...] = jnp.zeros_like(e_scr)

        def tiles(jobs):
            n_jobs = range(len(jobs))
            qrows = [pl.ds(r * BLOCK, BLOCK) for r, _, _, _ in jobs]
            krows = [pl.ds(pl.multiple_of(m * BLOCK, BLOCK), BLOCK) for _, m, _, _ in jobs]
            diag = [dg for _, _, dg, _ in jobs]
            lg = [_sb_logits(q_ref[qrows[i], :], k_ref[krows[i], :], scale, causal if diag[i] else None, jobs[i][3])
                  for i in n_jobs]
            d_a = [_dot_nt(dob_scr[qrows[i], :], v_ref[krows[i], :]) for i in n_jobs]
            sm = [_sums(lg[i][2], incl) for i in n_jobs]
            a_all = []
            for i in n_jobs:
                p = p_scr[qrows[i], :]
                a = jnp.exp(lg[i][1] + (tot_ref[qrows[i], :] - p - sm[i][0]))
                a_all.append(jnp.where(causal, a, 0.0) if diag[i] else a)
                p_scr[qrows[i], :] = p + sm[i][1]
            gr = [d_a[i] * a_all[i] for i in n_jobs]
            dv_part = [_dot_tn(a_all[i].astype(BF16), dob_scr[qrows[i], :]) for i in n_jobs]
            gs = [_sums(gr[i], excl, pieces=1) for i in n_jobs]
            dz_all = []
            for i in n_jobs:
                e = e_scr[qrows[i], :]
                dz = gr[i] - (gr[i] + e + gs[i][0]) * jnp.exp(lg[i][1])
                if diag[i]:
                    dz = jnp.where(causal, dz, 0.0)
                dz_all.append((dz * scale).astype(BF16))
                e_scr[qrows[i], :] = e + gs[i][1]
            dk_part = [_dot_tn(dz_all[i], q_ref[qrows[i], :]) for i in n_jobs]
            dq_part = [_dot(dz_all[i], k_ref[krows[i], :]) for i in n_jobs]
            for i in n_jobs:
                dv_acc[krows[i], :] += dv_part[i]
                dk_acc[krows[i], :] += dk_part[i]
                dq_acc[qrows[i], :] += dq_part[i]

        def pad_row_of(m):
            return jnp.where(jnp.logical_or(m > 0, lane >= N_PAD), 1.0, 0.0)

        base = SB_GROUP * g
        stops = [jnp.max(stop_ref[pl.ds(r, 1), :]).astype(jnp.int32) for r in range(SB_GROUP)]
        for r in range(SB_GROUP):
            lowest = jnp.maximum(base - (SB_GROUP - 1 - r), 0)

            def step(m, r=r):
                tiles([(r, m, False, pad_row_of(m))])
                return m + 1

            lax.while_loop(lambda m, lowest=lowest: m < lowest, step, jnp.clip(stops[r], 0, lowest))
        own = [(r, base + kb, r == kb, pad_row_of(base) if kb == 0 else None)
               for kb in range(SB_GROUP) for r in range(kb, SB_GROUP)]
        pl.when(g == 0)(lambda: tiles(own))
        pl.when(g > 0)(lambda: tiles([(r, base - 1 - back, False, None) for r, back in reversed(SB_NEAR)] + own))
        dq_ref[...] = dq_acc[...].astype(BF16)

        @pl.when(g == n_groups - 1)
        def _():
            dk_ref[...] = dk_acc[...].astype(BF16)
            dv_ref[...] = dv_acc[...].astype(BF16)

    qblk = pl.BlockSpec((SB_ROWS, HEAD), lambda b, h, g: (b * n_groups + g, h))
    kblk = pl.BlockSpec((lp, HEAD), lambda b, h, g: (b, n_heads + h))
    vblk = pl.BlockSpec((lp, HEAD), lambda b, h, g: (b, 2 * n_heads + h))
    hblk = pl.BlockSpec((lp, HEAD), lambda b, h, g: (b, h))
    sblk = pl.BlockSpec((None, 8, 128), lambda b, h, g: ((b * n_heads + h) * n_groups + g, 0, 0))
    table = pl.BlockSpec((2 * BLOCK, 2 * BLOCK), lambda b, h, g: (0, 0))
    _, incl, excl = _sb_tables()
    return _pcall(
        body, name=name, grid=(bsz, n_heads, n_groups),
        in_specs=[qblk, kblk, vblk, qblk, qblk, sblk, table, table],
        out_specs=(qblk, hblk, hblk),
        out_shape=(jax.ShapeDtypeStruct((t, d), BF16),) * 3,
        scratch_shapes=[pltpu.VMEM((lp, HEAD), F32), pltpu.VMEM((lp, HEAD), F32)]
        + [pltpu.VMEM((SB_ROWS, HEAD), F32)] * 3 + [pltpu.VMEM((SB_ROWS, HEAD), BF16)],
        compiler_params=_params("parallel", "parallel", "arbitrary"),
    )(qkv, qkv, qkv, do, tot, stop, incl, excl)


def _adamw(w, g, m, v):
    m = ADAM_B1 * m + (1.0 - ADAM_B1) * g
    v = ADAM_B2 * v + (1.0 - ADAM_B2) * (g * g)
    m_hat = m / (1.0 - ADAM_B1 ** ADAM_STEP)
    v_hat = v / (1.0 - ADAM_B2 ** ADAM_STEP)
    delta = -ADAM_LR * (m_hat / (jnp.sqrt(v_hat) + ADAM_EPS) + ADAM_WD * w)
    return delta, m, v


def _update_sharded(w, parts, m, v, name):
    r, c = w.shape
    tr = UPDATE_ROWS if r % UPDATE_ROWS == 0 else r

    def body(w_ref, p_ref, m_ref, v_ref, g_ref, d_ref, nm_ref, nv_ref):
        g = p_ref[0].astype(F32)
        for q in range(1, N_DEV):
            g = g + p_ref[q].astype(F32)
        g_ref[...] = g
        d_ref[...], nm_ref[...], nv_ref[...] = _adamw(w_ref[...], g, m_ref[...], v_ref[...])

    row = pl.BlockSpec((tr, c), lambda i: (i, 0))
    return _pcall(
        body, name=name, grid=(r // tr,),
        in_specs=[row, pl.BlockSpec((N_DEV, tr, c), lambda i: (0, i, 0)), row, row],
        out_specs=(row,) * 4, out_shape=(jax.ShapeDtypeStruct((r, c), F32),) * 4,
        compiler_params=_params("parallel"),
    )(w, parts, m, v)


SMALL_ROWS = 8


def _pack_small(dpre0, dpre1, dpost0, dpost1, dlb, dwon, loss, name):
    d = dpre0.shape[1]
    bsz = dlb.shape[0]

    def body(a0, a1, p0, p1, lb_ref, on_ref, loss_ref, out_ref):
        out_ref[...] = jnp.zeros_like(out_ref)
        out_ref[pl.ds(0, 1), :] = a0[...]
        out_ref[pl.ds(1, 1), :] = a1[...]
        out_ref[pl.ds(2, 1), :] = p0[...]
        out_ref[pl.ds(3, 1), :] = p1[...]
        acc = lb_ref[0]
        for b in range(1, bsz):
            acc = acc + lb_ref[b]
        out_ref[pl.ds(4, 1), :] = acc
        out_ref[pl.ds(5, 1), pl.ds(0, HEAD)] = on_ref[...]
        out_ref[pl.ds(6, 1), pl.ds(0, HEAD)] = loss_ref[pl.ds(0, 1), :]

    return _pcall(body, name=name, out_shape=jax.ShapeDtypeStruct((SMALL_ROWS, d), F32))(
        dpre0, dpre1, dpost0, dpost1, dlb, dwon, loss)


def _update_small(parts, pre, post, lbw, on, moments, name):
    d = pre.shape[1]

    def body(p_ref, pre_ref, post_ref, lbw_ref, on_ref, mpre, mpost, mlb, mon, vpre, vpost, vlb, von,
             loss_ref, *outs):
        def total(r0, nr, width):
            acc = p_ref[0, pl.ds(r0, nr), pl.ds(0, width)]
            for q in range(1, N_DEV):
                acc = acc + p_ref[q, pl.ds(r0, nr), pl.ds(0, width)]
            return acc

        def put(k, w, g, m, v):
            dl, nm, nv = _adamw(w, g, m, v)
            outs[4 * k][...] = g
            outs[4 * k + 1][...] = dl
            outs[4 * k + 2][...] = nm
            outs[4 * k + 3][...] = nv

        put(0, pre_ref[...], total(0, 2, d), mpre[...], vpre[...])
        put(1, post_ref[...], total(2, 2, d), mpost[...], vpost[...])
        a0, a1 = lbw_ref[pl.ds(0, 1), :], lbw_ref[pl.ds(1, 1), :]
        mx = jnp.maximum(a0, a1)
        e0, e1 = jnp.exp(a0 - mx), jnp.exp(a1 - mx)
        p0 = e0 / (e0 + e1)
        g0 = total(4, 1, d) * p0 * (1.0 - p0)
        for r, w, g in ((0, a0, g0), (1, a1, -g0)):
            row = pl.ds(r, 1)
            dl, nm, nv = _adamw(w, g, mlb[row, :], vlb[row, :])
            outs[8][row, :] = g
            outs[9][row, :] = dl
            outs[10][row, :] = nm
            outs[11][row, :] = nv
        put(3, on_ref[...], total(5, 1, HEAD), mon[...], von[...])
        loss_ref[...] = jnp.broadcast_to(total(6, 1, HEAD), loss_ref.shape)

    shapes = []
    for w in (pre, post, lbw, on):
        shapes += [jax.ShapeDtypeStruct(w.shape, F32)] * 4
    return _pcall(body, name=name, out_shape=(jax.ShapeDtypeStruct((8, 128), F32), *shapes))(
        parts, pre, post, lbw, on, *moments)


def kernel(x, meta_tokens, pre_norm, post_norm, hgrn_w_in, hgrn_lb, hgrn_out_norm, hgrn_w_out, sb_w_in, sb_w_out, loss_target, m_meta_tokens, m_pre_norm, m_post_norm, m_hgrn_w_in, m_hgrn_lb, m_hgrn_out_norm, m_hgrn_w_out, m_sb_w_in, m_sb_w_out, v_meta_tokens, v_pre_norm, v_post_norm, v_hgrn_w_in, v_hgrn_lb, v_hgrn_out_norm, v_hgrn_w_out, v_sb_w_in, v_sb_w_out):
    bsz, seq, d = x.shape
    n_blocks = seq // BLOCK + 1
    lp = n_blocks * BLOCK
    s = hgrn_w_in.shape[2]
    dsh = d // N_DEV

    w_in_h, meta_all = _gather_once_per_chip([hgrn_w_in[0].astype(BF16), meta_tokens], "gather_weights")
    meta_full = jnp.transpose(meta_all, (1, 0, 2)).reshape(N_META, d)

    lbrow = jnp.cumsum(jax.nn.softmax(hgrn_lb, axis=0), axis=0)[0:1]

    main0, gate0, yn0, h0 = _norm_inproj(None, pre_norm[0:1], w_in_h, F32, "inproj_hgrn", from_x=(x, meta_full))
    o0, states, w_in_s, w_out_s, w_out_h = _hgrn_fwd(
        main0, lbrow, bsz, n_blocks,
        [sb_w_in[0].astype(BF16), sb_w_out[0].astype(BF16), hgrn_w_out[0].astype(BF16)], ["gather"] * 3, "hgrn_fwd")
    w_out_s = w_out_s.reshape(d, d)
    w_out_h = w_out_h.reshape(d, d)
    h1, y0 = _mix_out(o0, gate0, h0, w_out_h, hgrn_out_norm, post_norm[0:1], True, "mix_out_hgrn")
    main1, gate1, yn1 = _norm_inproj(h1, pre_norm[1:2], w_in_s, BF16, "inproj_sb")
    o1, tot, stop = _sb_fwd(main1, bsz, n_blocks, "sb_fwd")
    dh2, y1, loss_part = _mix_out(o1, gate1, h1, w_out_s, hgrn_out_norm, post_norm[1:2], False, "mix_out_sb",
                                  target=loss_target)

    do1, dgate1, dw_out_s, dpost1, _ = _mix_out_bwd(
        dh2, y1, o1, gate1, w_out_s, hgrn_out_norm, post_norm[1:2], False, "mix_out_sb_bwd")
    dq1, dk1, dv1 = _sb_bwd(main1, do1, tot, stop, bsz, n_blocks, "sb_bwd")
    dproj1 = (dq1, dk1, dv1, dgate1)
    dh1, dpre1 = _inproj_bwd_x(dproj1, w_in_s, h1, pre_norm[1:2], dh2, [], [], "inproj_sb_bwd_x")
    dw_in_s = _inproj_bwd_w(yn1, dproj1, s, "inproj_sb_bwd_w")

    do0, dgate0, dw_out_h, dpost0, dwon = _mix_out_bwd(
        dh1, y0, o0, gate0, w_out_h, hgrn_out_norm, post_norm[0:1], True, "mix_out_hgrn_bwd")
    dq0, dfz0, dv0, dlb, p_in_s, p_out_s, p_out_h = _hgrn_bwd(
        main0, lbrow, states, do0, bsz, n_blocks,
        [dw_in_s, dw_out_s.reshape(N_DEV, dsh, d), dw_out_h.reshape(N_DEV, dsh, d)], ["scatter"] * 3, "hgrn_bwd")
    dproj0 = (dq0, dfz0, dv0, dgate0)
    dw_in_h = _inproj_bwd_w(yn0, dproj0, s, "inproj_hgrn_bwd_w")
    dh0, dpre0, p_in_h = _inproj_bwd_x(
        dproj0, w_in_h, h0, pre_norm[0:1], dh1, [dw_in_h], ["scatter"], "inproj_hgrn_bwd_x")

    dh0 = dh0.reshape(bsz, lp, d)
    grad_x = dh0[:, BLOCK:]
    dmeta = jnp.sum(dh0[:, N_PAD:BLOCK], axis=0)
    dmeta = jnp.transpose(dmeta.reshape(N_META, N_DEV, dsh), (1, 0, 2))
    small = _pack_small(dpre0, dpre1, dpost0, dpost1, dlb, dwon, loss_part, "pack_small")

    p_meta, p_small = _exchange([dmeta, small], ["scatter", "gather"], "exchange_grads")

    u_meta = _update_sharded(meta_tokens, p_meta, m_meta_tokens, v_meta_tokens, "update_meta")
    u_in_h = _update_sharded(hgrn_w_in[0], p_in_h, m_hgrn_w_in[0], v_hgrn_w_in[0], "update_hgrn_w_in")
    u_out_h = _update_sharded(hgrn_w_out[0], p_out_h, m_hgrn_w_out[0], v_hgrn_w_out[0], "update_hgrn_w_out")
    u_in_s = _update_sharded(sb_w_in[0], p_in_s, m_sb_w_in[0], v_sb_w_in[0], "update_sb_w_in")
    u_out_s = _update_sharded(sb_w_out[0], p_out_s, m_sb_w_out[0], v_sb_w_out[0], "update_sb_w_out")
    sm = _update_small(p_small, pre_norm, post_norm, hgrn_lb, hgrn_out_norm,
                       (m_pre_norm, m_post_norm, m_hgrn_lb, m_hgrn_out_norm,
                        v_pre_norm, v_post_norm, v_hgrn_lb, v_hgrn_out_norm), "update_small")
    loss = sm[0][0, 0]
    u_pre, u_post, u_lb, u_on = sm[1:5], sm[5:9], sm[9:13], sm[13:17]

    per_w = [u_meta, u_pre, u_post, tuple(a[None] for a in u_in_h), u_lb, u_on,
             tuple(a[None] for a in u_out_h), tuple(a[None] for a in u_in_s), tuple(a[None] for a in u_out_s)]
    outs = [loss, grad_x]
    for k in range(4):
        outs += [u[k] for u in per_w]
    return tuple(outs)
```

```python
import jax
import jax.numpy as jnp
from jax import lax
from jax.experimental import pallas as pl
from jax.experimental.pallas import tpu as pltpu

F32 = jnp.float32
BF16 = jnp.bfloat16

N_DEV = 8
BLOCK = 128
N_META = 16
N_PAD = BLOCK - N_META
HEAD = 128
SUB = 16
N_SUB = BLOCK // SUB
HALF = 8
EPS = 1e-6
ROW_TILE = 3 * BLOCK
UPDATE_ROWS = 256
K_TILES = (1408, 768, 384, 128)
NEG_BIG = -1e30

ADAM_LR = 0.001
ADAM_B1 = 0.9
ADAM_B2 = 0.999
ADAM_EPS = 1e-08
ADAM_WD = 0.01
ADAM_STEP = 10

VMEM_LIMIT = 56 * 1024 * 1024


def _pcall(body, **kw):
    return pl.pallas_call(body, **kw)


def _params(*sem):
    return pltpu.CompilerParams(dimension_semantics=sem, vmem_limit_bytes=VMEM_LIMIT)


def _dot(a, b):
    return jnp.dot(a, b, preferred_element_type=F32)


def _dot_nt(a, b):
    return lax.dot_general(a, b, (((1,), (1,)), ((), ())), preferred_element_type=F32)


def _dot_tn(a, b):
    return lax.dot_general(a, b, (((0,), (0,)), ((), ())), preferred_element_type=F32)


def _split(x, pieces):
    out = []
    for _ in range(pieces):
        p = x.astype(BF16)
        out.append(p)
        x = x - p.astype(F32)
    return out


def _tri_left(tri, x, pieces=3):
    return sum(_dot(tri, p) for p in _split(x, pieces))


def _iota2(shape, dim):
    return lax.broadcasted_iota(jnp.int32, shape, dim)


def _tri(cond):
    return jnp.where(cond, 1.0, 0.0).astype(BF16)


def _sig_pair(x):
    e = jnp.exp(-jnp.abs(x))
    r = 1.0 / (1.0 + e)
    er = e * r
    pos = x >= 0
    return jnp.where(pos, r, er), jnp.where(pos, er, r)


def _expand(rows):
    return jnp.concatenate([jnp.broadcast_to(r, (SUB, HEAD)) for r in rows], axis=0)


def _exchange_shapes(arrays, modes):
    return tuple(jax.ShapeDtypeStruct((N_DEV,) + tuple(a.shape[1:] if m == "scatter" else a.shape), a.dtype)
                 for a, m in zip(arrays, modes))


def _exchange_sems(n):
    if n == 0:
        return []
    return [pltpu.SemaphoreType.DMA((n, N_DEV - 1)), pltpu.SemaphoreType.DMA((n, N_DEV - 1)),
            pltpu.SemaphoreType.DMA((n,))]


def _exchange_copies(ins, outs, modes, send_sems, recv_sems, local_sems):
    mx, my, mc = lax.axis_index("x"), lax.axis_index("y"), lax.axis_index("c")
    me = 4 * mx + 2 * my + mc

    def src(i, slot):
        return ins[i].at[slot] if modes[i] == "scatter" else ins[i]

    def peer_of(mask):
        px = 1 - mx if mask & 4 else mx
        py = 1 - my if mask & 2 else my
        pc = 1 - mc if mask & 1 else mc
        return px, py, pc

    def copy(i, mask, dst_slot):
        px, py, pc = peer_of(mask)
        return pltpu.make_async_remote_copy(
            src_ref=src(i, 4 * px + 2 * py + pc), dst_ref=outs[i].at[dst_slot],
            send_sem=send_sems.at[i, mask - 1], recv_sem=recv_sems.at[i, mask - 1],
            device_id=(px, py, pc), device_id_type=pl.DeviceIdType.MESH)

    n = len(ins)
    sends = [copy(i, mask, me) for mask in range(1, N_DEV) for i in range(n)]
    own = [pltpu.make_async_copy(src(i, me), outs[i].at[me], local_sems.at[i]) for i in range(n)]
    arrivals = []
    for mask in range(1, N_DEV):
        px, py, pc = peer_of(mask)
        arrivals += [copy(i, mask, 4 * px + 2 * py + pc) for i in range(n)]

    def start():
        for cp in sends + own:
            cp.start()

    def wait():
        for cp in arrivals:
            cp.wait_recv()
        for cp in sends:
            cp.wait_send()
        for cp in own:
            cp.wait()

    return start, wait


def _gather_once_per_chip(arrays, name):
    n = len(arrays)

    def body(*refs):
        ins, outs = refs[:n], refs[n:2 * n]
        send_sems, recv_sems, local_sems = refs[2 * n:]
        mx, my, mc = lax.axis_index("x"), lax.axis_index("y"), lax.axis_index("c")
        me, sibling = (mx, my, mc), (mx, my, 1 - mc)
        chips = [(1 - mx, my), (mx, 1 - my), (1 - mx, 1 - my)]

        def slot(px, py, pc):
            return 4 * px + 2 * py + pc

        def copy(i, k, block, to, src=None):
            return pltpu.make_async_remote_copy(
                src_ref=outs[i].at[slot(*block)] if src is None else src, dst_ref=outs[i].at[slot(*block)],
                send_sem=send_sems.at[i, k], recv_sem=recv_sems.at[i, k],
                device_id=to, device_id_type=pl.DeviceIdType.MESH)

        own = [pltpu.make_async_copy(ins[i], outs[i].at[slot(*me)], local_sems.at[i]) for i in range(n)]
        first = [copy(i, 0, me, sibling, src=ins[i]) for i in range(n)]
        first += [copy(i, 1 + j, me, (*chip, mc), src=ins[i]) for j, chip in enumerate(chips) for i in range(n)]
        for cp in own + first:
            cp.start()
        passed = []
        for j, chip in enumerate(chips):
            for i in range(n):
                copy(i, 1 + j, (*chip, mc), me).wait_recv()
                cp = copy(i, 4 + j, (*chip, mc), sibling)
                cp.start()
                passed.append(cp)
        for i in range(n):
            copy(i, 0, sibling, me).wait_recv()
            for j, chip in enumerate(chips):
                copy(i, 4 + j, (*chip, 1 - mc), me).wait_recv()
        for cp in first + passed:
            cp.wait_send()
        for cp in own:
            cp.wait()

    any_spec = pl.BlockSpec(memory_space=pl.ANY)
    return _pcall(
        body, name=name, out_shape=_exchange_shapes(arrays, ["gather"] * n),
        in_specs=[any_spec] * n, out_specs=tuple([any_spec] * n),
        scratch_shapes=_exchange_sems(n),
    )(*arrays)


def _exchange(arrays, modes, name):
    n = len(arrays)

    def body(*refs):
        start, wait = _exchange_copies(refs[:n], refs[n:2 * n], modes, *refs[2 * n:])
        start()
        wait()

    any_spec = pl.BlockSpec(memory_space=pl.ANY)
    return _pcall(
        body, name=name, out_shape=_exchange_shapes(arrays, modes),
        in_specs=[any_spec] * n, out_specs=tuple([any_spec] * n),
        scratch_shapes=_exchange_sems(n),
    )(*arrays)


def _tile_blocks_of_x(seq, d):
    assert ROW_TILE == 3 * BLOCK and (seq + BLOCK) % ROW_TILE == 0
    per_seq = (seq + BLOCK) // ROW_TILE

    def spec(k):
        return pl.BlockSpec((None, BLOCK, d),
                            lambda i: (i // per_seq, jnp.maximum(3 * (i % per_seq) - 1 + k, 0), 0))

    return [spec(0), spec(1), spec(2)]


def _norm_inproj(h, wnorm, w_all, main_dtype, name, from_x=None):
    p_n, d, s = w_all.shape
    n_main = 3 * d // s
    tm = ROW_TILE
    if from_x is None:
        t = h.shape[0]
        lead, lead_specs = [h], [pl.BlockSpec((tm, d), lambda i: (i, 0))]
    else:
        x_in, meta = from_x
        t = x_in.shape[0] * (x_in.shape[1] + BLOCK)
        tiles_per_seq = (x_in.shape[1] + BLOCK) // tm
        lead = [x_in, x_in, x_in, meta]
        lead_specs = _tile_blocks_of_x(x_in.shape[1], d) + [pl.BlockSpec((N_META, d), lambda i: (0, 0))]
    n_lead = len(lead)

    def body(*refs):
        wn_ref, w_ref = refs[n_lead:n_lead + 2]
        main_ref, gate_ref, ynt_ref = refs[n_lead + 2:n_lead + 5]
        if from_x is None:
            x = refs[0][...]
        else:
            first_tile = pl.program_id(0) % tiles_per_seq == 0
            meta_block = jnp.concatenate([jnp.zeros((N_PAD, d), F32), refs[3][...]], axis=0)
            x = jnp.concatenate([jnp.where(first_tile, meta_block, refs[0][...]), refs[1][...], refs[2][...]], axis=0)
            refs[n_lead + 5][...] = x
        y = x * lax.rsqrt(jnp.mean(x * x, axis=-1, keepdims=True) + EPS) * wn_ref[...]
        yb = y.astype(BF16)
        ynt_ref[...] = y.T.astype(BF16)
        for p in range(p_n):
            r = _dot(yb, w_ref[p])
            if p < n_main:
                main_ref[:, p * s:(p + 1) * s] = r.astype(main_dtype)
            else:
                gate_ref[:, (p - n_main) * s:(p - n_main + 1) * s] = r

    row = pl.BlockSpec((tm, d), lambda i: (i, 0))
    extra_specs, extra_shapes = ((), ()) if from_x is None else ((row,), (jax.ShapeDtypeStruct((t, d), F32),))
    return _pcall(
        body, name=name, grid=(t // tm,),
        in_specs=lead_specs + [pl.BlockSpec((1, d), lambda i: (0, 0)), pl.BlockSpec((p_n, d, s), lambda i: (0, 0, 0))],
        out_specs=(pl.BlockSpec((tm, 3 * d), lambda i: (i, 0)), row, pl.BlockSpec((d, tm), lambda i: (0, i)))
        + extra_specs,
        out_shape=(jax.ShapeDtypeStruct((t, 3 * d), main_dtype), jax.ShapeDtypeStruct((t, d), F32),
                   jax.ShapeDtypeStruct((d, t), BF16)) + extra_shapes,
        compiler_params=_params("parallel"),
    )(*lead, wnorm, w_all)


def _mix_out(o, gate, h_in, w_out, w_on, w_post, head_norm, name, target=None):
    t, d = o.shape
    n_heads = d // HEAD
    tm = ROW_TILE
    with_loss = target is not None
    if with_loss:
        tiles_per_seq = (target.shape[1] + BLOCK) // tm

    def body(o_ref, g_ref, h_ref, w_ref, won_ref, wp_ref, *rest):
        if with_loss:
            t0_ref, t1_ref, t2_ref, hout_ref, y_ref, loss_ref, u_scr = rest
        else:
            hout_ref, y_ref, u_scr = rest
        for hh in range(n_heads):
            cs = slice(hh * HEAD, (hh + 1) * HEAD)
            oh = o_ref[:, cs]
            gt = g_ref[:, cs]
            if head_norm:
                oh = oh * lax.rsqrt(jnp.mean(oh * oh, axis=-1, keepdims=True) + EPS) * won_ref[...]
            u_scr[:, cs] = (oh * (gt * jax.nn.sigmoid(gt))).astype(BF16)
        y = _dot(u_scr[...], w_ref[...])
        y_ref[...] = y
        r = y * lax.rsqrt(jnp.mean(y * y, axis=-1, keepdims=True) + EPS) * wp_ref[...]
        h_out = h_ref[...] + r
        if not with_loss:
            hout_ref[...] = h_out
            return
        i = pl.program_id(0)

        @pl.when(i == 0)
        def _():
            loss_ref[...] = jnp.zeros_like(loss_ref)

        tgt = jnp.concatenate([t0_ref[...], t1_ref[...], t2_ref[...]], axis=0)
        real = jnp.logical_or(i % tiles_per_seq > 0, _iota2((tm, d), 0) >= BLOCK)
        err = jnp.where(real, h_out - tgt, 0.0)
        hout_ref[...] = err * (1.0 / d)
        part = jnp.sum(jnp.sum(err * err, axis=-1, keepdims=True), axis=0, keepdims=True)
        loss_ref[...] += part * (0.5 / d)

    row = pl.BlockSpec((tm, d), lambda i: (i, 0))
    in_specs = [row, row, row, pl.BlockSpec((d, d), lambda i: (0, 0)),
                pl.BlockSpec((1, HEAD), lambda i: (0, 0)), pl.BlockSpec((1, d), lambda i: (0, 0))]
    out_specs, out_shape = (row, row), (jax.ShapeDtypeStruct((t, d), F32), jax.ShapeDtypeStruct((t, d), F32))
    args = (o, gate, h_in, w_out, w_on, w_post)
    if with_loss:
        in_specs += _tile_blocks_of_x(target.shape[1], d)
        out_specs += (pl.BlockSpec((8, 128), lambda i: (0, 0)),)
        out_shape += (jax.ShapeDtypeStruct((8, 128), F32),)
        args += (target, target, target)
    return _pcall(
        body, name=name, grid=(t // tm,), in_specs=in_specs, out_specs=out_specs, out_shape=out_shape,
        scratch_shapes=[pltpu.VMEM((tm, d), BF16)],
        compiler_params=_params("arbitrary" if with_loss else "parallel"),
    )(*args)


def _mix_out_bwd(dh, y, o, gate, w_out, w_on, w_post, head_norm, name):
    t, d = o.shape
    n_heads = d // HEAD
    tm = ROW_TILE
    last = t // tm - 1

    def body(dh_ref, y_ref, o_ref, g_ref, w_ref, won_ref, wp_ref,
             do_ref, dg_ref, dw_ref, dwp_ref, dwon_ref, u_scr, acc):
        i = pl.program_id(0)

        @pl.when(i == 0)
        def _():
            acc[...] = jnp.zeros_like(acc)
            dwp_ref[...] = jnp.zeros_like(dwp_ref)
            dwon_ref[...] = jnp.zeros_like(dwon_ref)

        yv = y_ref[...]
        rs = lax.rsqrt(jnp.mean(yv * yv, axis=-1, keepdims=True) + EPS)
        yh = yv * rs
        dr = dh_ref[...]
        dwp_ref[...] += jnp.sum(dr * yh, axis=0, keepdims=True)
        wd = dr * wp_ref[...]
        dy = rs * (wd - yh * jnp.mean(wd * yh, axis=-1, keepdims=True))
        dyb = dy.astype(BF16)
        du = _dot_nt(dyb, w_ref[...])
        for hh in range(n_heads):
            cs = slice(hh * HEAD, (hh + 1) * HEAD)
            oh = o_ref[:, cs]
            gt = g_ref[:, cs]
            sg = jax.nn.sigmoid(gt)
            sl = gt * sg
            duh = du[:, cs]
            if head_norm:
                rsh = lax.rsqrt(jnp.mean(oh * oh, axis=-1, keepdims=True) + EPS)
                ohat = oh * rsh
                on = ohat * won_ref[...]
            else:
                on = oh
            u_scr[:, cs] = (on * sl).astype(BF16)
            don = duh * sl
            dg_ref[:, cs] = (duh * on * (sg * (1.0 + gt * (1.0 - sg)))).astype(BF16)
            if head_norm:
                dwon_ref[...] += jnp.sum(don * ohat, axis=0, keepdims=True)
                wdn = don * won_ref[...]
                do_ref[:, cs] = rsh * (wdn - ohat * jnp.mean(wdn * ohat, axis=-1, keepdims=True))
            else:
                do_ref[:, cs] = don
        acc[...] += _dot_tn(u_scr[...], dyb)

        @pl.when(i == last)
        def _():
            dw_ref[...] = acc[...].astype(BF16)

    row = pl.BlockSpec((tm, d), lambda i: (i, 0))
    return _pcall(
        body, name=name, grid=(t // tm,),
        in_specs=[row, row, row, row, pl.BlockSpec((d, d), lambda i: (0, 0)),
                  pl.BlockSpec((1, HEAD), lambda i: (0, 0)), pl.BlockSpec((1, d), lambda i: (0, 0))],
        out_specs=(row, row, pl.BlockSpec((d, d), lambda i: (0, 0)), pl.BlockSpec((1, d), lambda i: (0, 0)),
                   pl.BlockSpec((1, HEAD), lambda i: (0, 0))),
        out_shape=(jax.ShapeDtypeStruct((t, d), F32), jax.ShapeDtypeStruct((t, d), BF16),
                   jax.ShapeDtypeStruct((d, d), BF16), jax.ShapeDtypeStruct((1, d), F32),
                   jax.ShapeDtypeStruct((1, HEAD), F32)),
        scratch_shapes=[pltpu.VMEM((tm, d), BF16), pltpu.VMEM((d, d), F32)],
        compiler_params=_params("arbitrary"),
    )(dh, y, o, gate, w_out, w_on, w_post)


def _inproj_bwd_x(dparts, w_all, h, wnorm, dres, ride, ride_modes, name):
    t, d = h.shape
    p_n, _, s = w_all.shape
    per = d // s
    tm = ROW_TILE
    nr = len(ride)
    grid = (t // tm,)

    def body(*refs):
        d0_ref, d1_ref, d2_ref, d3_ref, w_ref, h_ref, wn_ref, dres_ref = refs[:8]
        dh_ref, dwn_ref = refs[8 + nr:10 + nr]
        finish_ride = _ride_along(refs[8:8 + nr], refs[10 + nr:10 + 2 * nr], ride_modes, refs[10 + 2 * nr:], grid)
        i = pl.program_id(0)

        @pl.when(i == 0)
        def _():
            dwn_ref[...] = jnp.zeros_like(dwn_ref)

        pieces = (d0_ref, d1_ref, d2_ref, d3_ref)
        dyn = jnp.zeros((tm, d), F32)
        for p in range(p_n):
            blk = pieces[p // per][:, (p % per) * s:(p % per + 1) * s]
            dyn = dyn + _dot_nt(blk, w_ref[p])
        x = h_ref[...]
        rs = lax.rsqrt(jnp.mean(x * x, axis=-1, keepdims=True) + EPS)
        xh = x * rs
        dwn_ref[...] += jnp.sum(dyn * xh, axis=0, keepdims=True)
        wd = dyn * wn_ref[...]
        dh_ref[...] = dres_ref[...] + rs * (wd - xh * jnp.mean(wd * xh, axis=-1, keepdims=True))
        finish_ride()

    row = pl.BlockSpec((tm, d), lambda i: (i, 0))
    any_spec = pl.BlockSpec(memory_space=pl.ANY)
    return _pcall(
        body, name=name, grid=grid,
        in_specs=[row, row, row, row, pl.BlockSpec((p_n, d, s), lambda i: (0, 0, 0)),
                  row, pl.BlockSpec((1, d), lambda i: (0, 0)), row] + [any_spec] * nr,
        out_specs=(row, pl.BlockSpec((1, d), lambda i: (0, 0))) + (any_spec,) * nr,
        out_shape=(jax.ShapeDtypeStruct((t, d), F32), jax.ShapeDtypeStruct((1, d), F32))
        + _exchange_shapes(ride, ride_modes),
        scratch_shapes=_exchange_sems(nr),
        compiler_params=_params("arbitrary"),
    )(*dparts, w_all, h, wnorm, dres, *ride)


def _inproj_bwd_w(ynt, dparts, s, name):
    d, t = ynt.shape
    per = d // s
    n_sh = len(dparts) * per
    tk = next(c for c in K_TILES if t % c == 0)
    last = t // tk - 1

    def body(ynt_ref, d0_ref, d1_ref, d2_ref, d3_ref, dw_ref, acc):
        p, i = pl.program_id(0), pl.program_id(1)

        @pl.when(i == 0)
        def _():
            acc[...] = jnp.zeros_like(acc)

        for a, piece in enumerate((d0_ref, d1_ref, d2_ref, d3_ref)):
            @pl.when(p // per == a)
            def _():
                acc[...] += _dot(ynt_ref[...], piece[...])

        @pl.when(i == last)
        def _():
            dw_ref[...] = acc[...].astype(BF16)

    def piece_spec(a):
        return pl.BlockSpec((tk, s), lambda p, i: (jnp.where(p // per == a, i, 0),
                                                   jnp.where(p // per == a, p % per, 0)))

    return _pcall(
        body, name=name, grid=(n_sh, t // tk),
        in_specs=[pl.BlockSpec((d, tk), lambda p, i: (0, i))] + [piece_spec(a) for a in range(4)],
        out_specs=pl.BlockSpec((None, d, s), lambda p, i: (p, 0, 0)),
        out_shape=jax.ShapeDtypeStruct((n_sh, d, s), BF16),
        scratch_shapes=[pltpu.VMEM((d, s), F32)],
        compiler_params=_params("parallel", "arbitrary"),
    )(ynt, *dparts)


def _hgrn_common(q_ref, fz_ref, lb, b_scr, k_scr):
    fz = fz_ref[...]
    sig, nsig = _sig_pair(fz)
    f = lb + (1.0 - lb) * sig
    k = (1.0 - lb) * nsig
    rr, cc = _iota2((BLOCK, BLOCK), 0), _iota2((BLOCK, BLOCK), 1)
    b = _tri_left(_tri(cc <= rr), jnp.log(f))
    b_scr[...] = b
    k_scr[...] = k
    bend = [b_scr[pl.ds(SUB * j + SUB - 1, 1), :] for j in range(N_SUB)]
    bref = [jnp.zeros((1, HEAD), F32)] + bend[:-1]
    refrow, bendrow = _expand(bref), _expand(bend)
    e_q = jnp.exp(b - refrow)
    e_k = jnp.exp(bendrow - b)
    qt = q_ref[...] * e_q
    kh = k * e_k
    bl = bend[-1]
    return dict(sig=sig, nsig=nsig, f=f, k=k, b=b, bend=bend, bref=bref, refrow=refrow,
                e_q=e_q, e_k=e_k, qt=qt, kh=kh, bl=bl)


HEADS_PER_STEP = 8


def _lockstep(chunks):
    live = list(chunks)
    while live:
        still = []
        for gen in live:
            try:
                next(gen)
                still.append(gen)
            except StopIteration:
                pass
        live = still


def _ride_along(ride_ins, ride_outs, modes, sems, grid):
    if not ride_ins:
        return lambda: None
    ids = [pl.program_id(a) for a in range(len(grid))]
    start, wait = _exchange_copies(ride_ins, ride_outs, modes, *sems)
    first, last = ids[0] == 0, ids[0] == grid[0] - 1
    for a in range(1, len(grid)):
        first = jnp.logical_and(first, ids[a] == 0)
        last = jnp.logical_and(last, ids[a] == grid[a] - 1)
    pl.when(first)(start)
    return lambda: pl.when(last)(wait)


def _hgrn_fwd(main, lbrow, bsz, n_blocks, ride, ride_modes, name):
    t, d3 = main.shape
    d = d3 // 3
    n_heads = d // HEAD
    n_pairs = n_heads // HEADS_PER_STEP
    wide = HEADS_PER_STEP * HEAD
    nr = len(ride)
    grid = (bsz, n_pairs, n_blocks)

    def chunk(q_ref, fz_ref, v_ref, lb, o_ref, st_ref, s_scr, b_scr, k_scr, o_acc):
        c = _hgrn_common(q_ref, fz_ref, lb, b_scr, k_scr)
        yield
        s_t = s_scr[...]
        st_ref[...] = s_t
        vb = v_ref[...].astype(BF16)
        q_state = c["qt"] * _expand([jnp.exp(r) for r in c["bref"]])
        o_state = _dot_nt(q_state.astype(BF16), s_t.astype(BF16))
        js = range(N_SUB - 1)
        khb = c["kh"].astype(BF16)
        lhs = [(c["qt"][SUB * (j + 1):] * jnp.exp(c["refrow"][SUB * (j + 1):] - c["bend"][j])).astype(BF16)
               for j in js]
        yield
        a_js = [_dot_nt(lhs[j], khb[SUB * j:SUB * (j + 1)]) for j in js]
        k_state = c["kh"] * _expand([jnp.exp(c["bl"] - r) for r in c["bend"]])
        s_scr[...] = s_t * jnp.exp(c["bl"]) + _dot_tn(vb, k_state.astype(BF16))
        yield
        o_js = [_dot(a_js[j].astype(BF16), vb[SUB * j:SUB * (j + 1)]) for j in js]
        yield
        o_acc[...] = o_state
        for j in js:
            o_acc[SUB * (j + 1):, :] += o_js[j]
        t_loc = _iota2((HALF, HEAD), 0)
        for i in range(N_SUB):
            r0 = SUB * i
            q_h = [q_ref[pl.ds(r0 + HALF * u, HALF), :] for u in range(2)]
            b_h = [b_scr[pl.ds(r0 + HALF * u, HALF), :] for u in range(2)]
            o_h = [jnp.zeros((HALF, HEAD), F32) for _ in range(2)]
            for s in range(SUB):
                brow = b_scr[pl.ds(r0 + s, 1), :]
                krow = k_scr[pl.ds(r0 + s, 1), :]
                vrow = v_ref[pl.ds(r0 + s, 1), :]
                for u in range(s // HALF, 2):
                    diff = b_h[u] - brow
                    if u == s // HALF:
                        diff = jnp.where(t_loc >= s - HALF * u, diff, NEG_BIG)
                    col = jnp.sum(q_h[u] * krow * jnp.exp(diff), axis=-1, keepdims=True)
                    o_h[u] = o_h[u] + col * vrow
            for u in range(2):
                o_acc[pl.ds(r0 + HALF * u, HALF), :] += o_h[u]
            yield
        o_ref[...] = o_acc[...]

    def body(*refs):
        q_ref, fz_ref, v_ref, lb_ref = refs[:4]
        o_ref, st_ref = refs[4 + nr:6 + nr]
        s_scr, b_scr, k_scr, o_acc = refs[6 + 2 * nr:10 + 2 * nr]
        finish_ride = _ride_along(refs[4:4 + nr], refs[6 + nr:6 + 2 * nr], ride_modes, refs[10 + 2 * nr:], grid)

        @pl.when(pl.program_id(2) == 0)
        def _():
            s_scr[...] = jnp.zeros_like(s_scr)

        def head(hh):
            cols = pl.ds(hh * HEAD, HEAD)
            return chunk(q_ref.at[:, cols], fz_ref.at[:, cols], v_ref.at[:, cols], lb_ref[:, cols],
                         o_ref.at[:, cols], st_ref.at[hh], s_scr.at[hh], b_scr.at[hh], k_scr.at[hh], o_acc.at[hh])

        _lockstep([head(hh) for hh in range(HEADS_PER_STEP)])
        finish_ride()

    def blk(col0):
        return pl.BlockSpec((BLOCK, wide), lambda b, h, n: (b * n_blocks + n, col0 + h))

    any_spec = pl.BlockSpec(memory_space=pl.ANY)
    per_head = pltpu.VMEM((HEADS_PER_STEP, BLOCK, HEAD), F32)
    return _pcall(
        body, name=name, grid=grid,
        in_specs=[blk(0), blk(n_pairs), blk(2 * n_pairs), pl.BlockSpec((1, wide), lambda b, h, n: (0, h))]
        + [any_spec] * nr,
        out_specs=(blk(0), pl.BlockSpec((None, HEADS_PER_STEP, HEAD, HEAD), lambda b, h, n: (b * n_blocks + n, h, 0, 0)))
        + (any_spec,) * nr,
        out_shape=(jax.ShapeDtypeStruct((t, d), F32),
                   jax.ShapeDtypeStruct((bsz * n_blocks, n_heads, HEAD, HEAD), F32))
        + _exchange_shapes(ride, ride_modes),
        scratch_shapes=[per_head] * 4 + _exchange_sems(nr),
        compiler_params=_params("arbitrary", "arbitrary", "arbitrary"),
    )(main, main, main, lbrow, *ride)


def _hgrn_bwd(main, lbrow, states, do, bsz, n_blocks, ride, ride_modes, name):
    t, d3 = main.shape
    d = d3 // 3
    n_pairs = d // HEAD // HEADS_PER_STEP
    wide = HEADS_PER_STEP * HEAD
    nr = len(ride)
    grid = (bsz, n_pairs, n_blocks)

    def chunk(n, q_ref, fz_ref, v_ref, lb, st_ref, do_ref, dq_ref, dfz_ref, dv_ref, dlb_ref,
              ds_scr, b_scr, k_scr, dqt_acc, dkh_acc, dv_acc, dqd_acc, dkd_acc, ad_scr):
        c = _hgrn_common(q_ref, fz_ref, lb, b_scr, k_scr)
        yield
        q, k = q_ref[...], c["k"]
        vb = v_ref[...].astype(BF16)
        dob = do_ref[...].astype(BF16)
        s0_t = st_ref[...]
        ds1_t = ds_scr[...]
        e_ref = _expand([jnp.exp(r) for r in c["bref"]])
        e_end = _expand([jnp.exp(c["bl"] - r) for r in c["bend"]])
        e_bl = jnp.exp(c["bl"])
        q_state = c["qt"] * e_ref
        k_state = c["kh"] * e_end
        dq_state = _dot(dob, s0_t.astype(BF16))
        dk_state = _dot(vb, ds1_t.astype(BF16))
        dv_state = _dot_nt(k_state.astype(BF16), ds1_t.astype(BF16))
        ds_scr[...] = ds1_t * e_bl + _dot_tn(dob, q_state.astype(BF16))
        js = range(N_SUB - 1)
        lo = [slice(SUB * j, SUB * (j + 1)) for j in js]
        khb = c["kh"].astype(BF16)
        dj = [jnp.exp(c["refrow"][SUB * (j + 1):] - c["bend"][j]) for j in js]
        lhs = [(c["qt"][SUB * (j + 1):] * dj[j]).astype(BF16) for j in js]
        yield
        a_js = [_dot_nt(lhs[j], khb[lo[j]]) for j in js]
        da_js = [_dot_nt(dob[SUB * (j + 1):], vb[lo[j]]).astype(BF16) for j in js]
        dqt_acc[...] = dq_state * e_ref
        dkh_acc[...] = dk_state * e_end
        dv_acc[...] = dv_state
        dbl = (jnp.sum(s0_t * ds1_t, axis=0, keepdims=True) * e_bl
               + jnp.sum(k_state * dk_state, axis=0, keepdims=True))
        yield
        dv_js = [_dot_tn(a_js[j].astype(BF16), dob[SUB * (j + 1):]) for j in js]
        dq_js = [_dot(da_js[j], khb[lo[j]]) * dj[j] for j in js]
        dk_js = [_dot_tn(da_js[j], lhs[j]) for j in js]
        yield
        for j in js:
            dv_acc[lo[j], :] += dv_js[j]
            dqt_acc[SUB * (j + 1):, :] += dq_js[j]
            dkh_acc[lo[j], :] += dk_js[j]
        t_loc = _iota2((HALF, HEAD), 0)
        lane = _iota2((HALF, HEAD), 1)
        for i in range(N_SUB):
            r0 = SUB * i
            q_h = [q_ref[pl.ds(r0 + HALF * u, HALF), :] for u in range(2)]
            b_h = [b_scr[pl.ds(r0 + HALF * u, HALF), :] for u in range(2)]
            do_h = [do_ref[pl.ds(r0 + HALF * u, HALF), :] for u in range(2)]
            zero = jnp.zeros((HALF, HEAD), F32)
            dq_h, dk_h, a_h = [zero, zero], [zero, zero], [zero, zero]
            for s in range(SUB):
                brow = b_scr[pl.ds(r0 + s, 1), :]
                krow = k_scr[pl.ds(r0 + s, 1), :]
                vrow = v_ref[pl.ds(r0 + s, 1), :]
                dk_row = jnp.zeros((1, HEAD), F32)
                for u in range(s // HALF, 2):
                    diff = b_h[u] - brow
                    if u == s // HALF:
                        diff = jnp.where(t_loc >= s - HALF * u, diff, NEG_BIG)
                    w = jnp.exp(diff)
                    qw = q_h[u] * w
                    a_col = jnp.sum(qw * krow, axis=-1, keepdims=True)
                    da_col = jnp.sum(do_h[u] * vrow, axis=-1, keepdims=True)
                    a_h[u] = jnp.where(lane == r0 + s, a_col, a_h[u])
                    dq_h[u] = dq_h[u] + da_col * (w * krow)
                    dk_row = dk_row + jnp.sum(da_col * qw, axis=0, keepdims=True)
                us = s // HALF
                dk_h[us] = jnp.where(t_loc == s - HALF * us, dk_row, dk_h[us])
            for u in range(2):
                rows = pl.ds(r0 + HALF * u, HALF)
                dqd_acc[rows, :] = dq_h[u]
                dkd_acc[rows, :] = dk_h[u]
                ad_scr[rows, :] = a_h[u]
            yield
        dv_in = _dot_tn(ad_scr[...].astype(BF16), dob)
        dq = dqt_acc[...] * c["e_q"] + dqd_acc[...]
        dk = dkh_acc[...] * c["e_k"] + dkd_acc[...]
        rr, cc = _iota2((BLOCK, BLOCK), 0), _iota2((BLOCK, BLOCK), 1)
        db = q * dq - k * dk + jnp.where(_iota2((BLOCK, HEAD), 0) == BLOCK - 1, dbl, 0.0)
        yield
        dg = _tri_left(_tri(cc >= rr), db)
        yield
        real = jnp.logical_or(n > 0, _iota2((BLOCK, HEAD), 0) >= N_PAD)
        df = jnp.where(real, dg / c["f"] - dk, 0.0)
        dq_ref[...] = dq.astype(BF16)
        dv_ref[...] = (dv_acc[...] + dv_in).astype(BF16)
        dfz_ref[...] = (df * (1.0 - lb) * c["sig"] * c["nsig"]).astype(BF16)
        dlb_ref[...] += jnp.sum(df * c["nsig"], axis=0, keepdims=True)

    def body(*refs):
        q_ref, fz_ref, v_ref, lb_ref, st_ref, do_ref = refs[:6]
        dq_ref, dfz_ref, dv_ref, dlb_ref = refs[6 + nr:10 + nr]
        scratch = refs[10 + 2 * nr:19 + 2 * nr]
        finish_ride = _ride_along(refs[6:6 + nr], refs[10 + nr:10 + 2 * nr], ride_modes, refs[19 + 2 * nr:], grid)
        step = pl.program_id(2)

        @pl.when(step == 0)
        def _():
            scratch[0][...] = jnp.zeros_like(scratch[0])
            dlb_ref[...] = jnp.zeros_like(dlb_ref)

        def head(hh):
            cols = pl.ds(hh * HEAD, HEAD)
            return chunk(n_blocks - 1 - step, q_ref.at[:, cols], fz_ref.at[:, cols], v_ref.at[:, cols],
                         lb_ref[:, cols], st_ref.at[hh], do_ref.at[:, cols], dq_ref.at[:, cols],
                         dfz_ref.at[:, cols], dv_ref.at[:, cols], dlb_ref.at[:, cols],
                         *[scr.at[hh] for scr in scratch])

        _lockstep([head(hh) for hh in range(HEADS_PER_STEP)])
        finish_ride()

    def blk(col0):
        return pl.BlockSpec((BLOCK, wide), lambda b, h, s: (b * n_blocks + n_blocks - 1 - s, col0 + h))

    any_spec = pl.BlockSpec(memory_space=pl.ANY)
    per_head = pltpu.VMEM((HEADS_PER_STEP, BLOCK, HEAD), F32)
    return _pcall(
        body, name=name, grid=grid,
        in_specs=[blk(0), blk(n_pairs), blk(2 * n_pairs), pl.BlockSpec((1, wide), lambda b, h, s: (0, h)),
                  pl.BlockSpec((None, HEADS_PER_STEP, HEAD, HEAD),
                               lambda b, h, s: (b * n_blocks + n_blocks - 1 - s, h, 0, 0)),
                  blk(0)] + [any_spec] * nr,
        out_specs=(blk(0), blk(0), blk(0), pl.BlockSpec((None, 1, wide), lambda b, h, s: (b, 0, h)))
        + (any_spec,) * nr,
        out_shape=(jax.ShapeDtypeStruct((t, d), BF16),) * 3 + (jax.ShapeDtypeStruct((bsz, 1, d), F32),)
        + _exchange_shapes(ride, ride_modes),
        scratch_shapes=[per_head] * 9 + _exchange_sems(nr),
        compiler_params=_params("arbitrary", "arbitrary", "arbitrary"),
    )(main, main, main, lbrow, states, do, *ride)


SB_GROUP = 3
SB_ROWS = SB_GROUP * BLOCK
SB_DEAD = -104.0
SB_NEAR = ((0, 0), (1, 0), (0, 1))
SB_HEADS = 2


def _sb_tables():
    j = jnp.bitwise_and(_iota2((2 * BLOCK, 2 * BLOCK), 0), BLOCK - 1)
    s = _iota2((2 * BLOCK, 2 * BLOCK), 1)
    ones = s >= BLOCK
    return (_tri(jnp.logical_or(ones, j > s)), _tri(jnp.logical_or(ones, j <= s)),
            _tri(jnp.logical_or(ones, j < s)))


def _sums(x, table, pieces=2):
    hi = x.astype(BF16)
    if pieces == 1:
        r = _dot(hi, table[:BLOCK])
    else:
        lo = (x - hi.astype(F32)).astype(BF16)
        r = _dot(jnp.concatenate([hi, lo], axis=1), table)
    return r[:, :BLOCK], r[:, BLOCK:]


def _sb_logits(q, ks, scale, causal, pad_row):
    z = _dot_nt(q, ks) * scale
    log_keep = -(jnp.maximum(z, 0.0) + jnp.log(1.0 + jnp.exp(-jnp.abs(z))))
    log_beta = z + log_keep
    if causal is not None:
        log_keep = jnp.where(causal, log_keep, 0.0)
    if pad_row is not None:
        log_keep = log_keep * pad_row
    return z, log_beta, log_keep


def _sb_fwd(qkv, bsz, n_blocks, name):
    t, d3 = qkv.shape
    d = d3 // 3
    n_heads = d // HEAD
    lp = n_blocks * BLOCK
    n_groups = n_blocks // SB_GROUP
    assert n_groups * SB_GROUP == n_blocks
    scale = HEAD ** -0.5

    n_pairs = n_heads // SB_HEADS
    wide = SB_HEADS * HEAD

    def body(q_ref, k_ref, v_ref, upper_ref, o_ref, tot_ref, stop_ref, c_scr):
        g = pl.program_id(2)
        upper = upper_ref[...]
        causal = _iota2((BLOCK, BLOCK), 1) < _iota2((BLOCK, BLOCK), 0)
        lane = _iota2((1, BLOCK), 1)
        o_ref[...] = jnp.zeros_like(o_ref)
        c_scr[...] = jnp.zeros_like(c_scr)

        def tiles(jobs):
            cols = [pl.ds(hh * HEAD, HEAD) for hh, _, _, _, _ in jobs]
            qrows = [pl.ds(r * BLOCK, BLOCK) for _, r, _, _, _ in jobs]
            krows = [pl.ds(pl.multiple_of(m * BLOCK, BLOCK), BLOCK) for _, _, m, _, _ in jobs]
            n_jobs = range(len(jobs))
            lg = [_sb_logits(q_ref[qrows[i], cols[i]], k_ref[krows[i], cols[i]], scale,
                             causal if jobs[i][3] else None, jobs[i][4]) for i in n_jobs]
            sm = [_sums(x[2], upper) for x in lg]
            a_all = []
            for i in n_jobs:
                c = c_scr[qrows[i], cols[i]]
                a = jnp.exp(lg[i][1] + c + sm[i][0])
                a_all.append(jnp.where(causal, a, 0.0) if jobs[i][3] else a)
                c_scr[qrows[i], cols[i]] = c + sm[i][1]
            out = [_dot(a_all[i].astype(BF16), v_ref[krows[i], cols[i]]) for i in n_jobs]
            for i in n_jobs:
                o_ref[qrows[i], cols[i]] += out[i]

        def pad_row_of(m):
            return jnp.where(jnp.logical_or(m > 0, lane >= N_PAD), 1.0, 0.0)

        base = SB_GROUP * g
        heads = range(SB_HEADS)
        own = [(hh, r, base + kb, r == kb, pad_row_of(base) if kb == 0 else None)
               for hh in heads for kb in range(SB_GROUP - 1, -1, -1) for r in range(kb, SB_GROUP)]
        near = [(hh, r, base - 1 - back, False, None) for hh in heads for r, back in SB_NEAR]
        pl.when(g == 0)(lambda: tiles(own))
        pl.when(g > 0)(lambda: tiles(own + near))

        stop_ref[...] = jnp.zeros_like(stop_ref)
        mine = [(hh, r) for hh in heads for r in range(SB_GROUP)]
        reach = [jnp.max(c_scr[pl.ds(r * BLOCK, BLOCK), pl.ds(hh * HEAD, HEAD)]) for hh, r in mine]
        for (hh, r), reach_now in zip(mine, reach):
            def live(carry):
                m, c_max = carry
                return jnp.logical_and(m >= 0, c_max >= SB_DEAD)

            def step(carry, hh=hh, r=r):
                m, _ = carry
                tiles([(hh, r, m, False, pad_row_of(m))])
                return m - 1, jnp.max(c_scr[pl.ds(r * BLOCK, BLOCK), pl.ds(hh * HEAD, HEAD)])

            lowest = jnp.maximum(base - (SB_GROUP - 1 - r), 0)
            m_end, _ = lax.while_loop(live, step, (lowest - 1, reach_now))
            stop_ref[pl.ds(SB_GROUP * hh + r, 1), :] = jnp.broadcast_to((m_end + 1).astype(F32), (1, 128))
        tot_ref[...] = c_scr[...]

    qblk = pl.BlockSpec((SB_ROWS, wide), lambda b, h, g: (b * n_groups + g, h))
    table = pl.BlockSpec((2 * BLOCK, 2 * BLOCK), lambda b, h, g: (0, 0))
    return _pcall(
        body, name=name, grid=(bsz, n_pairs, n_groups),
        in_specs=[qblk, pl.BlockSpec((lp, wide), lambda b, h, g: (b, n_pairs + h)),
                  pl.BlockSpec((lp, wide), lambda b, h, g: (b, 2 * n_pairs + h)), table],
        out_specs=(qblk, qblk, pl.BlockSpec((None, 8, 128), lambda b, h, g: ((b * n_pairs + h) * n_groups + g, 0, 0))),
        out_shape=(jax.ShapeDtypeStruct((t, d), F32), jax.ShapeDtypeStruct((t, d), F32),
                   jax.ShapeDtypeStruct((bsz * n_pairs * n_groups, 8, 128), F32)),
        scratch_shapes=[pltpu.VMEM((SB_ROWS, wide), F32)],
        compiler_params=_params("parallel", "parallel", "arbitrary"),
    )(qkv, qkv, qkv, _sb_tables()[0])


def _sb_bwd(qkv, do, tot, stop, bsz, n_blocks, name):
    t, d3 = qkv.shape
    d = d3 // 3
    n_heads = d // HEAD
    lp = n_blocks * BLOCK
    n_groups = n_blocks // SB_GROUP
    scale = HEAD ** -0.5

    def body(q_ref, k_ref, v_ref, do_ref, tot_ref, stop_ref, incl_ref, excl_ref, dq_ref, dk_ref, dv_ref,
             dk_acc, dv_acc, dq_acc, p_scr, e_scr, dob_scr):
        g = pl.program_id(2)

        @pl.when(g == 0)
        def _():
            dk_acc[...] = jnp.zeros_like(dk_acc)
            dv_acc[...] = jnp.zeros_like(dv_acc)

        incl, excl = incl_ref[...], excl_ref[...]
        causal = _iota2((BLOCK, BLOCK), 1) < _iota2((BLOCK, BLOCK), 0)
        lane = _iota2((1, BLOCK), 1)
        dob_scr[...] = do_ref[...].astype(BF16)
        dq_acc[...] = jnp.zeros_like(dq_acc)
        p_scr[...] = jnp.zeros_like(p_scr)
        e_scr[...] = jnp.zeros_like(e_scr)

        def tiles(jobs):
            n_jobs = range(len(jobs))
            cols = [pl.ds(hh * HEAD, HEAD) for hh, _, _, _, _ in jobs]
            qrows = [pl.ds(r * BLOCK, BLOCK) for _, r, _, _, _ in jobs]
            krows = [pl.ds(pl.multiple_of(m * BLOCK, BLOCK), BLOCK) for _, _, m, _, _ in jobs]
            diag = [dg for _, _, _, dg, _ in jobs]
            lg = [_sb_logits(q_ref[qrows[i], cols[i]], k_ref[krows[i], cols[i]], scale,
                             causal if diag[i] else None, jobs[i][4]) for i in n_jobs]
            d_a = [_dot_nt(dob_scr[qrows[i], cols[i]], v_ref[krows[i], cols[i]]) for i in n_jobs]
            sm = [_sums(lg[i][2], incl) for i in n_jobs]
            a_all = []
            for i in n_jobs:
                p = p_scr[qrows[i], cols[i]]
                a = jnp.exp(lg[i][1] + (tot_ref[qrows[i], cols[i]] - p - sm[i][0]))
                a_all.append(jnp.where(causal, a, 0.0) if diag[i] else a)
                p_scr[qrows[i], cols[i]] = p + sm[i][1]
            gr = [d_a[i] * a_all[i] for i in n_jobs]
            dv_part = [_dot_tn(a_all[i].astype(BF16), dob_scr[qrows[i], cols[i]]) for i in n_jobs]
            gs = [_sums(gr[i], excl, pieces=1) for i in n_jobs]
            dz_all = []
            for i in n_jobs:
                e = e_scr[qrows[i], cols[i]]
                dz = gr[i] - (gr[i] + e + gs[i][0]) * jnp.exp(lg[i][1])
                if diag[i]:
                    dz = jnp.where(causal, dz, 0.0)
                dz_all.append((dz * scale).astype(BF16))
                e_scr[qrows[i], cols[i]] = e + gs[i][1]
            dk_part = [_dot_tn(dz_all[i], q_ref[qrows[i], cols[i]]) for i in n_jobs]
            dq_part = [_dot(dz_all[i], k_ref[krows[i], cols[i]]) for i in n_jobs]
            for i in n_jobs:
                dv_acc[krows[i], cols[i]] += dv_part[i]
                dk_acc[krows[i], cols[i]] += dk_part[i]
                dq_acc[qrows[i], cols[i]] += dq_part[i]

        def pad_row_of(m):
            return jnp.where(jnp.logical_or(m > 0, lane >= N_PAD), 1.0, 0.0)

        base = SB_GROUP * g
        heads = range(SB_HEADS)
        mine = [(hh, r) for hh in heads for r in range(SB_GROUP)]
        stops = [jnp.max(stop_ref[pl.ds(SB_GROUP * hh + r, 1), :]).astype(jnp.int32) for hh, r in mine]
        for (hh, r), stop_now in zip(mine, stops):
            lowest = jnp.maximum(base - (SB_GROUP - 1 - r), 0)

            def step(m, hh=hh, r=r):
                tiles([(hh, r, m, False, pad_row_of(m))])
                return m + 1

            lax.while_loop(lambda m, lowest=lowest: m < lowest, step, jnp.clip(stop_now, 0, lowest))
        own = [(hh, r, base + kb, r == kb, pad_row_of(base) if kb == 0 else None)
               for hh in heads for kb in range(SB_GROUP) for r in range(kb, SB_GROUP)]
        near = [(hh, r, base - 1 - back, False, None) for hh in heads for r, back in reversed(SB_NEAR)]
        pl.when(g == 0)(lambda: tiles(own))
        pl.when(g > 0)(lambda: tiles(near + own))
        dq_ref[...] = dq_acc[...].astype(BF16)

        @pl.when(g == n_groups - 1)
        def _():
            dk_ref[...] = dk_acc[...].astype(BF16)
            dv_ref[...] = dv_acc[...].astype(BF16)

    n_pairs = n_heads // SB_HEADS
    wide = SB_HEADS * HEAD
    qblk = pl.BlockSpec((SB_ROWS, wide), lambda b, h, g: (b * n_groups + g, h))
    kblk = pl.BlockSpec((lp, wide), lambda b, h, g: (b, n_pairs + h))
    vblk = pl.BlockSpec((lp, wide), lambda b, h, g: (b, 2 * n_pairs + h))
    hblk = pl.BlockSpec((lp, wide), lambda b, h, g: (b, h))
    sblk = pl.BlockSpec((None, 8, 128), lambda b, h, g: ((b * n_pairs + h) * n_groups + g, 0, 0))
    table = pl.BlockSpec((2 * BLOCK, 2 * BLOCK), lambda b, h, g: (0, 0))
    _, incl, excl = _sb_tables()
    return _pcall(
        body, name=name, grid=(bsz, n_pairs, n_groups),
        in_specs=[qblk, kblk, vblk, qblk, qblk, sblk, table, table],
        out_specs=(qblk, hblk, hblk),
        out_shape=(jax.ShapeDtypeStruct((t, d), BF16),) * 3,
        scratch_shapes=[pltpu.VMEM((lp, wide), F32), pltpu.VMEM((lp, wide), F32)]
        + [pltpu.VMEM((SB_ROWS, wide), F32)] * 3 + [pltpu.VMEM((SB_ROWS, wide), BF16)],
        compiler_params=_params("parallel", "parallel", "arbitrary"),
    )(qkv, qkv, qkv, do, tot, stop, incl, excl)


def _adamw(w, g, m, v):
    m = ADAM_B1 * m + (1.0 - ADAM_B1) * g
    v = ADAM_B2 * v + (1.0 - ADAM_B2) * (g * g)
    m_hat = m / (1.0 - ADAM_B1 ** ADAM_STEP)
    v_hat = v / (1.0 - ADAM_B2 ** ADAM_STEP)
    delta = -ADAM_LR * (m_hat / (jnp.sqrt(v_hat) + ADAM_EPS) + ADAM_WD * w)
    return delta, m, v


def _update_sharded(w, parts, m, v, name):
    r, c = w.shape
    tr = UPDATE_ROWS if r % UPDATE_ROWS == 0 else r

    def body(w_ref, p_ref, m_ref, v_ref, g_ref, d_ref, nm_ref, nv_ref):
        g = p_ref[0].astype(F32)
        for q in range(1, N_DEV):
            g = g + p_ref[q].astype(F32)
        g_ref[...] = g
        d_ref[...], nm_ref[...], nv_ref[...] = _adamw(w_ref[...], g, m_ref[...], v_ref[...])

    row = pl.BlockSpec((tr, c), lambda i: (i, 0))
    return _pcall(
        body, name=name, grid=(r // tr,),
        in_specs=[row, pl.BlockSpec((N_DEV, tr, c), lambda i: (0, i, 0)), row, row],
        out_specs=(row,) * 4, out_shape=(jax.ShapeDtypeStruct((r, c), F32),) * 4,
        compiler_params=_params("parallel"),
    )(w, parts, m, v)


SMALL_ROWS = 8


def _pack_small(dpre0, dpre1, dpost0, dpost1, dlb, dwon, loss, name):
    d = dpre0.shape[1]
    bsz = dlb.shape[0]

    def body(a0, a1, p0, p1, lb_ref, on_ref, loss_ref, out_ref):
        out_ref[...] = jnp.zeros_like(out_ref)
        out_ref[pl.ds(0, 1), :] = a0[...]
        out_ref[pl.ds(1, 1), :] = a1[...]
        out_ref[pl.ds(2, 1), :] = p0[...]
        out_ref[pl.ds(3, 1), :] = p1[...]
        acc = lb_ref[0]
        for b in range(1, bsz):
            acc = acc + lb_ref[b]
        out_ref[pl.ds(4, 1), :] = acc
        out_ref[pl.ds(5, 1), pl.ds(0, HEAD)] = on_ref[...]
        out_ref[pl.ds(6, 1), pl.ds(0, HEAD)] = loss_ref[pl.ds(0, 1), :]

    return _pcall(body, name=name, out_shape=jax.ShapeDtypeStruct((SMALL_ROWS, d), F32))(
        dpre0, dpre1, dpost0, dpost1, dlb, dwon, loss)


def _update_small(parts, pre, post, lbw, on, moments, name):
    d = pre.shape[1]

    def body(p_ref, pre_ref, post_ref, lbw_ref, on_ref, mpre, mpost, mlb, mon, vpre, vpost, vlb, von,
             loss_ref, *outs):
        def total(r0, nr, width):
            acc = p_ref[0, pl.ds(r0, nr), pl.ds(0, width)]
            for q in range(1, N_DEV):
                acc = acc + p_ref[q, pl.ds(r0, nr), pl.ds(0, width)]
            return acc

        def put(k, w, g, m, v):
            dl, nm, nv = _adamw(w, g, m, v)
            outs[4 * k][...] = g
            outs[4 * k + 1][...] = dl
            outs[4 * k + 2][...] = nm
            outs[4 * k + 3][...] = nv

        put(0, pre_ref[...], total(0, 2, d), mpre[...], vpre[...])
        put(1, post_ref[...], total(2, 2, d), mpost[...], vpost[...])
        a0, a1 = lbw_ref[pl.ds(0, 1), :], lbw_ref[pl.ds(1, 1), :]
        mx = jnp.maximum(a0, a1)
        e0, e1 = jnp.exp(a0 - mx), jnp.exp(a1 - mx)
        p0 = e0 / (e0 + e1)
        g0 = total(4, 1, d) * p0 * (1.0 - p0)
        for r, w, g in ((0, a0, g0), (1, a1, -g0)):
            row = pl.ds(r, 1)
            dl, nm, nv = _adamw(w, g, mlb[row, :], vlb[row, :])
            outs[8][row, :] = g
            outs[9][row, :] = dl
            outs[10][row, :] = nm
            outs[11][row, :] = nv
        put(3, on_ref[...], total(5, 1, HEAD), mon[...], von[...])
        loss_ref[...] = jnp.broadcast_to(total(6, 1, HEAD), loss_ref.shape)

    shapes = []
    for w in (pre, post, lbw, on):
        shapes += [jax.ShapeDtypeStruct(w.shape, F32)] * 4
    return _pcall(body, name=name, out_shape=(jax.ShapeDtypeStruct((8, 128), F32), *shapes))(
        parts, pre, post, lbw, on, *moments)


def kernel(x, meta_tokens, pre_norm, post_norm, hgrn_w_in, hgrn_lb, hgrn_out_norm, hgrn_w_out, sb_w_in, sb_w_out, loss_target, m_meta_tokens, m_pre_norm, m_post_norm, m_hgrn_w_in, m_hgrn_lb, m_hgrn_out_norm, m_hgrn_w_out, m_sb_w_in, m_sb_w_out, v_meta_tokens, v_pre_norm, v_post_norm, v_hgrn_w_in, v_hgrn_lb, v_hgrn_out_norm, v_hgrn_w_out, v_sb_w_in, v_sb_w_out):
    bsz, seq, d = x.shape
    n_blocks = seq // BLOCK + 1
    lp = n_blocks * BLOCK
    s = hgrn_w_in.shape[2]
    dsh = d // N_DEV

    w_in_h, meta_all = _gather_once_per_chip([hgrn_w_in[0].astype(BF16), meta_tokens], "gather_weights")
    meta_full = jnp.transpose(meta_all, (1, 0, 2)).reshape(N_META, d)

    lbrow = jnp.cumsum(jax.nn.softmax(hgrn_lb, axis=0), axis=0)[0:1]

    main0, gate0, yn0, h0 = _norm_inproj(None, pre_norm[0:1], w_in_h, F32, "inproj_hgrn", from_x=(x, meta_full))
    o0, states, w_in_s, w_out_s, w_out_h = _hgrn_fwd(
        main0, lbrow, bsz, n_blocks,
        [sb_w_in[0].astype(BF16), sb_w_out[0].astype(BF16), hgrn_w_out[0].astype(BF16)], ["gather"] * 3, "hgrn_fwd")
    w_out_s = w_out_s.reshape(d, d)
    w_out_h = w_out_h.reshape(d, d)
    h1, y0 = _mix_out(o0, gate0, h0, w_out_h, hgrn_out_norm, post_norm[0:1], True, "mix_out_hgrn")
    main1, gate1, yn1 = _norm_inproj(h1, pre_norm[1:2], w_in_s, BF16, "inproj_sb")
    o1, tot, stop = _sb_fwd(main1, bsz, n_blocks, "sb_fwd")
    dh2, y1, loss_part = _mix_out(o1, gate1, h1, w_out_s, hgrn_out_norm, post_norm[1:2], False, "mix_out_sb",
                                  target=loss_target)

    do1, dgate1, dw_out_s, dpost1, _ = _mix_out_bwd(
        dh2, y1, o1, gate1, w_out_s, hgrn_out_norm, post_norm[1:2], False, "mix_out_sb_bwd")
    dq1, dk1, dv1 = _sb_bwd(main1, do1, tot, stop, bsz, n_blocks, "sb_bwd")
    dproj1 = (dq1, dk1, dv1, dgate1)
    dh1, dpre1 = _inproj_bwd_x(dproj1, w_in_s, h1, pre_norm[1:2], dh2, [], [], "inproj_sb_bwd_x")
    dw_in_s = _inproj_bwd_w(yn1, dproj1, s, "inproj_sb_bwd_w")

    do0, dgate0, dw_out_h, dpost0, dwon = _mix_out_bwd(
        dh1, y0, o0, gate0, w_out_h, hgrn_out_norm, post_norm[0:1], True, "mix_out_hgrn_bwd")
    dq0, dfz0, dv0, dlb, p_in_s, p_out_s, p_out_h = _hgrn_bwd(
        main0, lbrow, states, do0, bsz, n_blocks,
        [dw_in_s, dw_out_s.reshape(N_DEV, dsh, d), dw_out_h.reshape(N_DEV, dsh, d)], ["scatter"] * 3, "hgrn_bwd")
    dproj0 = (dq0, dfz0, dv0, dgate0)
    dw_in_h = _inproj_bwd_w(yn0, dproj0, s, "inproj_hgrn_bwd_w")
    dh0, dpre0, p_in_h = _inproj_bwd_x(
        dproj0, w_in_h, h0, pre_norm[0:1], dh1, [dw_in_h], ["scatter"], "inproj_hgrn_bwd_x")

    dh0 = dh0.reshape(bsz, lp, d)
    grad_x = dh0[:, BLOCK:]
    dmeta = jnp.sum(dh0[:, N_PAD:BLOCK], axis=0)
    dmeta = jnp.transpose(dmeta.reshape(N_META, N_DEV, dsh), (1, 0, 2))
    small = _pack_small(dpre0, dpre1, dpost0, dpost1, dlb, dwon, loss_part, "pack_small")

    p_meta, p_small = _exchange([dmeta, small], ["scatter", "gather"], "exchange_grads")

    u_meta = _update_sharded(meta_tokens, p_meta, m_meta_tokens, v_meta_tokens, "update_meta")
    u_in_h = _update_sharded(hgrn_w_in[0], p_in_h, m_hgrn_w_in[0], v_hgrn_w_in[0], "update_hgrn_w_in")
    u_out_h = _update_sharded(hgrn_w_out[0], p_out_h, m_hgrn_w_out[0], v_hgrn_w_out[0], "update_hgrn_w_out")
    u_in_s = _update_sharded(sb_w_in[0], p_in_s, m_sb_w_in[0], v_sb_w_in[0], "update_sb_w_in")
    u_out_s = _update_sharded(sb_w_out[0], p_out_s, m_sb_w_out[0], v_sb_w_out[0], "update_sb_w_out")
    sm = _update_small(p_small, pre_norm, post_norm, hgrn_lb, hgrn_out_norm,
                       (m_pre_norm, m_post_norm, m_hgrn_lb, m_hgrn_out_norm,
                        v_pre_norm, v_post_norm, v_hgrn_lb, v_hgrn_out_norm), "update_small")
    loss = sm[0][0, 0]
    u_pre, u_post, u_lb, u_on = sm[1:5], sm[5:9], sm[9:13], sm[13:17]

    per_w = [u_meta, u_pre, u_post, tuple(a[None] for a in u_in_h), u_lb, u_on,
             tuple(a[None] for a in u_out_h), tuple(a[None] for a in u_in_s), tuple(a[None] for a in u_out_s)]
    outs = [loss, grad_x]
    for k in range(4):
        outs += [u[k] for u in per_w]
    return tuple(outs)
```

```python
import jax
import jax.numpy as jnp
from jax import lax
from jax.experimental import pallas as pl
from jax.experimental.pallas import tpu as pltpu

F32 = jnp.float32
BF16 = jnp.bfloat16

N_DEV = 8
BLOCK = 128
N_META = 16
N_PAD = BLOCK - N_META
HEAD = 128
SUB = 16
N_SUB = BLOCK // SUB
HALF = 8
EPS = 1e-6
ROW_TILE = 3 * BLOCK
UPDATE_ROWS = 256
K_TILES = (2816, 1408, 768, 384, 128)
NEG_BIG = -1e30

ADAM_LR = 0.001
ADAM_B1 = 0.9
ADAM_B2 = 0.999
ADAM_EPS = 1e-08
ADAM_WD = 0.01
ADAM_STEP = 10

VMEM_LIMIT = 56 * 1024 * 1024


def _pcall(body, **kw):
    return pl.pallas_call(body, **kw)


def _params(*sem):
    return pltpu.CompilerParams(dimension_semantics=sem, vmem_limit_bytes=VMEM_LIMIT)


def _dot(a, b):
    return jnp.dot(a, b, preferred_element_type=F32)


def _dot_nt(a, b):
    return lax.dot_general(a, b, (((1,), (1,)), ((), ())), preferred_element_type=F32)


def _dot_tn(a, b):
    return lax.dot_general(a, b, (((0,), (0,)), ((), ())), preferred_element_type=F32)


def _split(x, pieces):
    out = []
    for _ in range(pieces):
        p = x.astype(BF16)
        out.append(p)
        x = x - p.astype(F32)
    return out


def _tri_left(tri, x, pieces=3):
    return sum(_dot(tri, p) for p in _split(x, pieces))


def _iota2(shape, dim):
    return lax.broadcasted_iota(jnp.int32, shape, dim)


def _tri(cond):
    return jnp.where(cond, 1.0, 0.0).astype(BF16)


def _sig_pair(x):
    e = jnp.exp(-jnp.abs(x))
    r = 1.0 / (1.0 + e)
    er = e * r
    pos = x >= 0
    return jnp.where(pos, r, er), jnp.where(pos, er, r)


def _expand(rows):
    return jnp.concatenate([jnp.broadcast_to(r, (SUB, HEAD)) for r in rows], axis=0)


def _exchange_shapes(arrays, modes):
    return tuple(jax.ShapeDtypeStruct((N_DEV,) + tuple(a.shape[1:] if m == "scatter" else a.shape), a.dtype)
                 for a, m in zip(arrays, modes))


def _exchange_sems(n):
    if n == 0:
        return []
    return [pltpu.SemaphoreType.DMA((n, N_DEV - 1)), pltpu.SemaphoreType.DMA((n, N_DEV - 1)),
            pltpu.SemaphoreType.DMA((n,))]


def _exchange_copies(ins, outs, modes, send_sems, recv_sems, local_sems):
    mx, my, mc = lax.axis_index("x"), lax.axis_index("y"), lax.axis_index("c")
    me = 4 * mx + 2 * my + mc

    def src(i, slot):
        return ins[i].at[slot] if modes[i] == "scatter" else ins[i]

    def peer_of(mask):
        px = 1 - mx if mask & 4 else mx
        py = 1 - my if mask & 2 else my
        pc = 1 - mc if mask & 1 else mc
        return px, py, pc

    def copy(i, mask, dst_slot):
        px, py, pc = peer_of(mask)
        return pltpu.make_async_remote_copy(
            src_ref=src(i, 4 * px + 2 * py + pc), dst_ref=outs[i].at[dst_slot],
            send_sem=send_sems.at[i, mask - 1], recv_sem=recv_sems.at[i, mask - 1],
            device_id=(px, py, pc), device_id_type=pl.DeviceIdType.MESH)

    n = len(ins)
    sends = [copy(i, mask, me) for mask in range(1, N_DEV) for i in range(n)]
    own = [pltpu.make_async_copy(src(i, me), outs[i].at[me], local_sems.at[i]) for i in range(n)]
    arrivals = []
    for mask in range(1, N_DEV):
        px, py, pc = peer_of(mask)
        arrivals += [copy(i, mask, 4 * px + 2 * py + pc) for i in range(n)]

    def start():
        for cp in sends + own:
            cp.start()

    def wait():
        for cp in arrivals:
            cp.wait_recv()
        for cp in sends:
            cp.wait_send()
        for cp in own:
            cp.wait()

    return start, wait


def _gather_once_per_chip(arrays, name):
    n = len(arrays)

    def body(*refs):
        ins, outs = refs[:n], refs[n:2 * n]
        send_sems, recv_sems, local_sems = refs[2 * n:]
        mx, my, mc = lax.axis_index("x"), lax.axis_index("y"), lax.axis_index("c")
        me, sibling = (mx, my, mc), (mx, my, 1 - mc)
        chips = [(1 - mx, my), (mx, 1 - my), (1 - mx, 1 - my)]

        def slot(px, py, pc):
            return 4 * px + 2 * py + pc

        def copy(i, k, block, to, src=None):
            return pltpu.make_async_remote_copy(
                src_ref=outs[i].at[slot(*block)] if src is None else src, dst_ref=outs[i].at[slot(*block)],
                send_sem=send_sems.at[i, k], recv_sem=recv_sems.at[i, k],
                device_id=to, device_id_type=pl.DeviceIdType.MESH)

        own = [pltpu.make_async_copy(ins[i], outs[i].at[slot(*me)], local_sems.at[i]) for i in range(n)]
        first = [copy(i, 0, me, sibling, src=ins[i]) for i in range(n)]
        first += [copy(i, 1 + j, me, (*chip, mc), src=ins[i]) for j, chip in enumerate(chips) for i in range(n)]
        for cp in own + first:
            cp.start()
        passed = []
        for j, chip in enumerate(chips):
            for i in range(n):
                copy(i, 1 + j, (*chip, mc), me).wait_recv()
                cp = copy(i, 4 + j, (*chip, mc), sibling)
                cp.start()
                passed.append(cp)
        for i in range(n):
            copy(i, 0, sibling, me).wait_recv()
            for j, chip in enumerate(chips):
                copy(i, 4 + j, (*chip, 1 - mc), me).wait_recv()
        for cp in first + passed:
            cp.wait_send()
        for cp in own:
            cp.wait()

    any_spec = pl.BlockSpec(memory_space=pl.ANY)
    return _pcall(
        body, name=name, out_shape=_exchange_shapes(arrays, ["gather"] * n),
        in_specs=[any_spec] * n, out_specs=tuple([any_spec] * n),
        scratch_shapes=_exchange_sems(n),
    )(*arrays)


def _exchange(arrays, modes, name):
    n = len(arrays)

    def body(*refs):
        start, wait = _exchange_copies(refs[:n], refs[n:2 * n], modes, *refs[2 * n:])
        start()
        wait()

    any_spec = pl.BlockSpec(memory_space=pl.ANY)
    return _pcall(
        body, name=name, out_shape=_exchange_shapes(arrays, modes),
        in_specs=[any_spec] * n, out_specs=tuple([any_spec] * n),
        scratch_shapes=_exchange_sems(n),
    )(*arrays)


def _tile_blocks_of_x(seq, d):
    assert ROW_TILE == 3 * BLOCK and (seq + BLOCK) % ROW_TILE == 0
    per_seq = (seq + BLOCK) // ROW_TILE

    def spec(k):
        return pl.BlockSpec((None, BLOCK, d),
                            lambda i: (i // per_seq, jnp.maximum(3 * (i % per_seq) - 1 + k, 0), 0))

    return [spec(0), spec(1), spec(2)]


def _norm_inproj(h, wnorm, w_all, main_dtype, name, from_x=None):
    p_n, d, s = w_all.shape
    n_main = 3 * d // s
    tm = ROW_TILE
    if from_x is None:
        t = h.shape[0]
        lead, lead_specs = [h], [pl.BlockSpec((tm, d), lambda i: (i, 0))]
    else:
        x_in, meta = from_x
        t = x_in.shape[0] * (x_in.shape[1] + BLOCK)
        tiles_per_seq = (x_in.shape[1] + BLOCK) // tm
        lead = [x_in, x_in, x_in, meta]
        lead_specs = _tile_blocks_of_x(x_in.shape[1], d) + [pl.BlockSpec((N_META, d), lambda i: (0, 0))]
    n_lead = len(lead)

    def body(*refs):
        wn_ref, w_ref = refs[n_lead:n_lead + 2]
        main_ref, gate_ref, ynt_ref = refs[n_lead + 2:n_lead + 5]
        if from_x is None:
            x = refs[0][...]
        else:
            first_tile = pl.program_id(0) % tiles_per_seq == 0
            meta_block = jnp.concatenate([jnp.zeros((N_PAD, d), F32), refs[3][...]], axis=0)
            x = jnp.concatenate([jnp.where(first_tile, meta_block, refs[0][...]), refs[1][...], refs[2][...]], axis=0)
            refs[n_lead + 5][...] = x
        y = x * lax.rsqrt(jnp.mean(x * x, axis=-1, keepdims=True) + EPS) * wn_ref[...]
        yb = y.astype(BF16)
        ynt_ref[...] = y.T.astype(BF16)
        for p in range(p_n):
            r = _dot(yb, w_ref[p])
            if p < n_main:
                main_ref[:, p * s:(p + 1) * s] = r.astype(main_dtype)
            else:
                gate_ref[:, (p - n_main) * s:(p - n_main + 1) * s] = r

    row = pl.BlockSpec((tm, d), lambda i: (i, 0))
    extra_specs, extra_shapes = ((), ()) if from_x is None else ((row,), (jax.ShapeDtypeStruct((t, d), F32),))
    return _pcall(
        body, name=name, grid=(t // tm,),
        in_specs=lead_specs + [pl.BlockSpec((1, d), lambda i: (0, 0)), pl.BlockSpec((p_n, d, s), lambda i: (0, 0, 0))],
        out_specs=(pl.BlockSpec((tm, 3 * d), lambda i: (i, 0)), row, pl.BlockSpec((d, tm), lambda i: (0, i)))
        + extra_specs,
        out_shape=(jax.ShapeDtypeStruct((t, 3 * d), main_dtype), jax.ShapeDtypeStruct((t, d), F32),
                   jax.ShapeDtypeStruct((d, t), BF16)) + extra_shapes,
        compiler_params=_params("parallel"),
    )(*lead, wnorm, w_all)


def _mix_out(o, gate, h_in, w_out, w_on, w_post, head_norm, name, target=None):
    t, d = o.shape
    n_heads = d // HEAD
    tm = ROW_TILE
    with_loss = target is not None
    if with_loss:
        tiles_per_seq = (target.shape[1] + BLOCK) // tm

    def body(o_ref, g_ref, h_ref, w_ref, won_ref, wp_ref, *rest):
        if with_loss:
            t0_ref, t1_ref, t2_ref, hout_ref, y_ref, loss_ref, u_scr = rest
        else:
            hout_ref, y_ref, u_scr = rest
        for hh in range(n_heads):
            cs = slice(hh * HEAD, (hh + 1) * HEAD)
            oh = o_ref[:, cs]
            gt = g_ref[:, cs]
            if head_norm:
                oh = oh * lax.rsqrt(jnp.mean(oh * oh, axis=-1, keepdims=True) + EPS) * won_ref[...]
            u_scr[:, cs] = (oh * (gt * jax.nn.sigmoid(gt))).astype(BF16)
        y = _dot(u_scr[...], w_ref[...])
        y_ref[...] = y
        r = y * lax.rsqrt(jnp.mean(y * y, axis=-1, keepdims=True) + EPS) * wp_ref[...]
        h_out = h_ref[...] + r
        if not with_loss:
            hout_ref[...] = h_out
            return
        i = pl.program_id(0)

        @pl.when(i == 0)
        def _():
            loss_ref[...] = jnp.zeros_like(loss_ref)

        tgt = jnp.concatenate([t0_ref[...], t1_ref[...], t2_ref[...]], axis=0)
        real = jnp.logical_or(i % tiles_per_seq > 0, _iota2((tm, d), 0) >= BLOCK)
        err = jnp.where(real, h_out - tgt, 0.0)
        hout_ref[...] = err * (1.0 / d)
        part = jnp.sum(jnp.sum(err * err, axis=-1, keepdims=True), axis=0, keepdims=True)
        loss_ref[...] += part * (0.5 / d)

    row = pl.BlockSpec((tm, d), lambda i: (i, 0))
    in_specs = [row, row, row, pl.BlockSpec((d, d), lambda i: (0, 0)),
                pl.BlockSpec((1, HEAD), lambda i: (0, 0)), pl.BlockSpec((1, d), lambda i: (0, 0))]
    out_specs, out_shape = (row, row), (jax.ShapeDtypeStruct((t, d), F32), jax.ShapeDtypeStruct((t, d), F32))
    args = (o, gate, h_in, w_out, w_on, w_post)
    if with_loss:
        in_specs += _tile_blocks_of_x(target.shape[1], d)
        out_specs += (pl.BlockSpec((8, 128), lambda i: (0, 0)),)
        out_shape += (jax.ShapeDtypeStruct((8, 128), F32),)
        args += (target, target, target)
    return _pcall(
        body, name=name, grid=(t // tm,), in_specs=in_specs, out_specs=out_specs, out_shape=out_shape,
        scratch_shapes=[pltpu.VMEM((tm, d), BF16)],
        compiler_params=_params("arbitrary" if with_loss else "parallel"),
    )(*args)


def _mix_out_bwd(dh, y, o, gate, w_out, w_on, w_post, head_norm, name):
    t, d = o.shape
    n_heads = d // HEAD
    tm = ROW_TILE
    last = t // tm - 1

    def body(dh_ref, y_ref, o_ref, g_ref, w_ref, won_ref, wp_ref,
             do_ref, dg_ref, dw_ref, dwp_ref, dwon_ref, u_scr, acc):
        i = pl.program_id(0)

        @pl.when(i == 0)
        def _():
            acc[...] = jnp.zeros_like(acc)
            dwp_ref[...] = jnp.zeros_like(dwp_ref)
            dwon_ref[...] = jnp.zeros_like(dwon_ref)

        yv = y_ref[...]
        rs = lax.rsqrt(jnp.mean(yv * yv, axis=-1, keepdims=True) + EPS)
        yh = yv * rs
        dr = dh_ref[...]
        dwp_ref[...] += jnp.sum(dr * yh, axis=0, keepdims=True)
        wd = dr * wp_ref[...]
        dy = rs * (wd - yh * jnp.mean(wd * yh, axis=-1, keepdims=True))
        dyb = dy.astype(BF16)
        du = _dot_nt(dyb, w_ref[...])
        for hh in range(n_heads):
            cs = slice(hh * HEAD, (hh + 1) * HEAD)
            oh = o_ref[:, cs]
            gt = g_ref[:, cs]
            sg = jax.nn.sigmoid(gt)
            sl = gt * sg
            duh = du[:, cs]
            if head_norm:
                rsh = lax.rsqrt(jnp.mean(oh * oh, axis=-1, keepdims=True) + EPS)
                ohat = oh * rsh
                on = ohat * won_ref[...]
            else:
                on = oh
            u_scr[:, cs] = (on * sl).astype(BF16)
            don = duh * sl
            dg_ref[:, cs] = (duh * on * (sg * (1.0 + gt * (1.0 - sg)))).astype(BF16)
            if head_norm:
                dwon_ref[...] += jnp.sum(don * ohat, axis=0, keepdims=True)
                wdn = don * won_ref[...]
                do_ref[:, cs] = rsh * (wdn - ohat * jnp.mean(wdn * ohat, axis=-1, keepdims=True))
            else:
                do_ref[:, cs] = don
        acc[...] += _dot_tn(u_scr[...], dyb)

        @pl.when(i == last)
        def _():
            dw_ref[...] = acc[...].astype(BF16)

    row = pl.BlockSpec((tm, d), lambda i: (i, 0))
    return _pcall(
        body, name=name, grid=(t // tm,),
        in_specs=[row, row, row, row, pl.BlockSpec((d, d), lambda i: (0, 0)),
                  pl.BlockSpec((1, HEAD), lambda i: (0, 0)), pl.BlockSpec((1, d), lambda i: (0, 0))],
        out_specs=(row, row, pl.BlockSpec((d, d), lambda i: (0, 0)), pl.BlockSpec((1, d), lambda i: (0, 0)),
                   pl.BlockSpec((1, HEAD), lambda i: (0, 0))),
        out_shape=(jax.ShapeDtypeStruct((t, d), F32), jax.ShapeDtypeStruct((t, d), BF16),
                   jax.ShapeDtypeStruct((d, d), BF16), jax.ShapeDtypeStruct((1, d), F32),
                   jax.ShapeDtypeStruct((1, HEAD), F32)),
        scratch_shapes=[pltpu.VMEM((tm, d), BF16), pltpu.VMEM((d, d), F32)],
        compiler_params=_params("arbitrary"),
    )(dh, y, o, gate, w_out, w_on, w_post)


def _inproj_bwd_x(dparts, w_all, h, wnorm, dres, ride, ride_modes, name):
    t, d = h.shape
    p_n, _, s = w_all.shape
    per = d // s
    tm = ROW_TILE
    nr = len(ride)
    grid = (t // tm,)

    def body(*refs):
        d0_ref, d1_ref, d2_ref, d3_ref, w_ref, h_ref, wn_ref, dres_ref = refs[:8]
        dh_ref, dwn_ref = refs[8 + nr:10 + nr]
        finish_ride = _ride_along(refs[8:8 + nr], refs[10 + nr:10 + 2 * nr], ride_modes, refs[10 + 2 * nr:], grid)
        i = pl.program_id(0)

        @pl.when(i == 0)
        def _():
            dwn_ref[...] = jnp.zeros_like(dwn_ref)

        pieces = (d0_ref, d1_ref, d2_ref, d3_ref)
        dyn = jnp.zeros((tm, d), F32)
        for p in range(p_n):
            blk = pieces[p // per][:, (p % per) * s:(p % per + 1) * s]
            dyn = dyn + _dot_nt(blk, w_ref[p])
        x = h_ref[...]
        rs = lax.rsqrt(jnp.mean(x * x, axis=-1, keepdims=True) + EPS)
        xh = x * rs
        dwn_ref[...] += jnp.sum(dyn * xh, axis=0, keepdims=True)
        wd = dyn * wn_ref[...]
        dh_ref[...] = dres_ref[...] + rs * (wd - xh * jnp.mean(wd * xh, axis=-1, keepdims=True))
        finish_ride()

    row = pl.BlockSpec((tm, d), lambda i: (i, 0))
    any_spec = pl.BlockSpec(memory_space=pl.ANY)
    return _pcall(
        body, name=name, grid=grid,
        in_specs=[row, row, row, row, pl.BlockSpec((p_n, d, s), lambda i: (0, 0, 0)),
                  row, pl.BlockSpec((1, d), lambda i: (0, 0)), row] + [any_spec] * nr,
        out_specs=(row, pl.BlockSpec((1, d), lambda i: (0, 0))) + (any_spec,) * nr,
        out_shape=(jax.ShapeDtypeStruct((t, d), F32), jax.ShapeDtypeStruct((1, d), F32))
        + _exchange_shapes(ride, ride_modes),
        scratch_shapes=_exchange_sems(nr),
        compiler_params=_params("arbitrary"),
    )(*dparts, w_all, h, wnorm, dres, *ride)


def _inproj_bwd_w(ynt, dparts, s, name):
    d, t = ynt.shape
    per = d // s
    n_sh = len(dparts) * per
    tk = next(c for c in K_TILES if t % c == 0)
    last = t // tk - 1

    def body(ynt_ref, d0_ref, d1_ref, d2_ref, d3_ref, dw_ref, acc):
        p, i = pl.program_id(0), pl.program_id(1)

        @pl.when(i == 0)
        def _():
            acc[...] = jnp.zeros_like(acc)

        for a, piece in enumerate((d0_ref, d1_ref, d2_ref, d3_ref)):
            @pl.when(p // per == a)
            def _():
                acc[...] += _dot(ynt_ref[...], piece[...])

        @pl.when(i == last)
        def _():
            dw_ref[...] = acc[...].astype(BF16)

    def piece_spec(a):
        return pl.BlockSpec((tk, s), lambda p, i: (jnp.where(p // per == a, i, 0),
                                                   jnp.where(p // per == a, p % per, 0)))

    return _pcall(
        body, name=name, grid=(n_sh, t // tk),
        in_specs=[pl.BlockSpec((d, tk), lambda p, i: (0, i))] + [piece_spec(a) for a in range(4)],
        out_specs=pl.BlockSpec((None, d, s), lambda p, i: (p, 0, 0)),
        out_shape=jax.ShapeDtypeStruct((n_sh, d, s), BF16),
        scratch_shapes=[pltpu.VMEM((d, s), F32)],
        compiler_params=_params("parallel", "arbitrary"),
    )(ynt, *dparts)


def _hgrn_common(q_ref, fz_ref, lb, b_scr, k_scr):
    fz = fz_ref[...]
    sig, nsig = _sig_pair(fz)
    f = lb + (1.0 - lb) * sig
    k = (1.0 - lb) * nsig
    rr, cc = _iota2((BLOCK, BLOCK), 0), _iota2((BLOCK, BLOCK), 1)
    b = _tri_left(_tri(cc <= rr), jnp.log(f))
    b_scr[...] = b
    k_scr[...] = k
    bend = [b_scr[pl.ds(SUB * j + SUB - 1, 1), :] for j in range(N_SUB)]
    bref = [jnp.zeros((1, HEAD), F32)] + bend[:-1]
    refrow, bendrow = _expand(bref), _expand(bend)
    e_q = jnp.exp(b - refrow)
    e_k = jnp.exp(bendrow - b)
    qt = q_ref[...] * e_q
    kh = k * e_k
    bl = bend[-1]
    return dict(sig=sig, nsig=nsig, f=f, k=k, b=b, bend=bend, bref=bref, refrow=refrow,
                e_q=e_q, e_k=e_k, qt=qt, kh=kh, bl=bl)


HEADS_PER_STEP = 8


def _lockstep(chunks):
    live = list(chunks)
    while live:
        still = []
        for gen in live:
            try:
                next(gen)
                still.append(gen)
            except StopIteration:
                pass
        live = still


def _ride_along(ride_ins, ride_outs, modes, sems, grid):
    if not ride_ins:
        return lambda: None
    ids = [pl.program_id(a) for a in range(len(grid))]
    start, wait = _exchange_copies(ride_ins, ride_outs, modes, *sems)
    first, last = ids[0] == 0, ids[0] == grid[0] - 1
    for a in range(1, len(grid)):
        first = jnp.logical_and(first, ids[a] == 0)
        last = jnp.logical_and(last, ids[a] == grid[a] - 1)
    pl.when(first)(start)
    return lambda: pl.when(last)(wait)


def _hgrn_fwd(main, lbrow, bsz, n_blocks, ride, ride_modes, name):
    t, d3 = main.shape
    d = d3 // 3
    n_heads = d // HEAD
    n_pairs = n_heads // HEADS_PER_STEP
    wide = HEADS_PER_STEP * HEAD
    nr = len(ride)
    grid = (bsz, n_pairs, n_blocks)

    def chunk(q_ref, fz_ref, v_ref, lb, o_ref, st_ref, s_scr, b_scr, k_scr, o_acc):
        c = _hgrn_common(q_ref, fz_ref, lb, b_scr, k_scr)
        yield
        s_t = s_scr[...]
        st_ref[...] = s_t
        vb = v_ref[...].astype(BF16)
        q_state = c["qt"] * _expand([jnp.exp(r) for r in c["bref"]])
        o_state = _dot_nt(q_state.astype(BF16), s_t.astype(BF16))
        js = range(N_SUB - 1)
        khb = c["kh"].astype(BF16)
        lhs = [(c["qt"][SUB * (j + 1):] * jnp.exp(c["refrow"][SUB * (j + 1):] - c["bend"][j])).astype(BF16)
               for j in js]
        yield
        a_js = [_dot_nt(lhs[j], khb[SUB * j:SUB * (j + 1)]) for j in js]
        k_state = c["kh"] * _expand([jnp.exp(c["bl"] - r) for r in c["bend"]])
        s_scr[...] = s_t * jnp.exp(c["bl"]) + _dot_tn(vb, k_state.astype(BF16))
        yield
        o_js = [_dot(a_js[j].astype(BF16), vb[SUB * j:SUB * (j + 1)]) for j in js]
        yield
        o_acc[...] = o_state
        for j in js:
            o_acc[SUB * (j + 1):, :] += o_js[j]
        not_before = [_iota2((HALF, HEAD), 0) >= row for row in range(HALF)]
        for i in range(N_SUB):
            r0 = SUB * i
            q_h = [q_ref[pl.ds(r0 + HALF * u, HALF), :] for u in range(2)]
            b_h = [b_scr[pl.ds(r0 + HALF * u, HALF), :] for u in range(2)]
            o_h = [jnp.zeros((HALF, HEAD), F32) for _ in range(2)]
            for s in range(SUB):
                brow = b_scr[pl.ds(r0 + s, 1), :]
                krow = k_scr[pl.ds(r0 + s, 1), :]
                vrow = v_ref[pl.ds(r0 + s, 1), :]
                for u in range(s // HALF, 2):
                    diff = b_h[u] - brow
                    if u == s // HALF:
                        diff = jnp.where(not_before[s - HALF * u], diff, NEG_BIG)
                    col = jnp.sum(q_h[u] * krow * jnp.exp(diff), axis=-1, keepdims=True)
                    o_h[u] = o_h[u] + col * vrow
            for u in range(2):
                o_acc[pl.ds(r0 + HALF * u, HALF), :] += o_h[u]
            yield
        o_ref[...] = o_acc[...]

    def body(*refs):
        q_ref, fz_ref, v_ref, lb_ref = refs[:4]
        o_ref, st_ref = refs[4 + nr:6 + nr]
        s_scr, b_scr, k_scr, o_acc = refs[6 + 2 * nr:10 + 2 * nr]
        finish_ride = _ride_along(refs[4:4 + nr], refs[6 + nr:6 + 2 * nr], ride_modes, refs[10 + 2 * nr:], grid)

        @pl.when(pl.program_id(2) == 0)
        def _():
            s_scr[...] = jnp.zeros_like(s_scr)

        def head(hh):
            cols = pl.ds(hh * HEAD, HEAD)
            return chunk(q_ref.at[:, cols], fz_ref.at[:, cols], v_ref.at[:, cols], lb_ref[:, cols],
                         o_ref.at[:, cols], st_ref.at[hh], s_scr.at[hh], b_scr.at[hh], k_scr.at[hh], o_acc.at[hh])

        _lockstep([head(hh) for hh in range(HEADS_PER_STEP)])
        finish_ride()

    def blk(col0):
        return pl.BlockSpec((BLOCK, wide), lambda b, h, n: (b * n_blocks + n, col0 + h))

    any_spec = pl.BlockSpec(memory_space=pl.ANY)
    per_head = pltpu.VMEM((HEADS_PER_STEP, BLOCK, HEAD), F32)
    return _pcall(
        body, name=name, grid=grid,
        in_specs=[blk(0), blk(n_pairs), blk(2 * n_pairs), pl.BlockSpec((1, wide), lambda b, h, n: (0, h))]
        + [any_spec] * nr,
        out_specs=(blk(0), pl.BlockSpec((None, HEADS_PER_STEP, HEAD, HEAD), lambda b, h, n: (b * n_blocks + n, h, 0, 0)))
        + (any_spec,) * nr,
        out_shape=(jax.ShapeDtypeStruct((t, d), F32),
                   jax.ShapeDtypeStruct((bsz * n_blocks, n_heads, HEAD, HEAD), F32))
        + _exchange_shapes(ride, ride_modes),
        scratch_shapes=[per_head] * 4 + _exchange_sems(nr),
        compiler_params=_params("arbitrary", "arbitrary", "arbitrary"),
    )(main, main, main, lbrow, *ride)


def _hgrn_bwd(main, lbrow, states, do, bsz, n_blocks, ride, ride_modes, name):
    t, d3 = main.shape
    d = d3 // 3
    n_pairs = d // HEAD // HEADS_PER_STEP
    wide = HEADS_PER_STEP * HEAD
    nr = len(ride)
    grid = (bsz, n_pairs, n_blocks)

    def chunk(n, q_ref, fz_ref, v_ref, lb, st_ref, do_ref, dq_ref, dfz_ref, dv_ref, dlb_ref,
              ds_scr, b_scr, k_scr, dqt_acc, dkh_acc, dv_acc, dqd_acc, dkd_acc, ad_scr):
        c = _hgrn_common(q_ref, fz_ref, lb, b_scr, k_scr)
        yield
        q, k = q_ref[...], c["k"]
        vb = v_ref[...].astype(BF16)
        dob = do_ref[...].astype(BF16)
        s0_t = st_ref[...]
        ds1_t = ds_scr[...]
        e_ref = _expand([jnp.exp(r) for r in c["bref"]])
        e_end = _expand([jnp.exp(c["bl"] - r) for r in c["bend"]])
        e_bl = jnp.exp(c["bl"])
        q_state = c["qt"] * e_ref
        k_state = c["kh"] * e_end
        dq_state = _dot(dob, s0_t.astype(BF16))
        dk_state = _dot(vb, ds1_t.astype(BF16))
        dv_state = _dot_nt(k_state.astype(BF16), ds1_t.astype(BF16))
        ds_scr[...] = ds1_t * e_bl + _dot_tn(dob, q_state.astype(BF16))
        js = range(N_SUB - 1)
        lo = [slice(SUB * j, SUB * (j + 1)) for j in js]
        khb = c["kh"].astype(BF16)
        dj = [jnp.exp(c["refrow"][SUB * (j + 1):] - c["bend"][j]) for j in js]
        lhs = [(c["qt"][SUB * (j + 1):] * dj[j]).astype(BF16) for j in js]
        yield
        a_js = [_dot_nt(lhs[j], khb[lo[j]]) for j in js]
        da_js = [_dot_nt(dob[SUB * (j + 1):], vb[lo[j]]).astype(BF16) for j in js]
        dqt_acc[...] = dq_state * e_ref
        dkh_acc[...] = dk_state * e_end
        dv_acc[...] = dv_state
        dbl = (jnp.sum(s0_t * ds1_t, axis=0, keepdims=True) * e_bl
               + jnp.sum(k_state * dk_state, axis=0, keepdims=True))
        yield
        dv_js = [_dot_tn(a_js[j].astype(BF16), dob[SUB * (j + 1):]) for j in js]
        dq_js = [_dot(da_js[j], khb[lo[j]]) * dj[j] for j in js]
        dk_js = [_dot_tn(da_js[j], lhs[j]) for j in js]
        yield
        for j in js:
            dv_acc[lo[j], :] += dv_js[j]
            dqt_acc[SUB * (j + 1):, :] += dq_js[j]
            dkh_acc[lo[j], :] += dk_js[j]
        not_before = [_iota2((HALF, HEAD), 0) >= row for row in range(HALF)]
        is_row = [_iota2((HALF, HEAD), 0) == row for row in range(HALF)]
        lane = _iota2((HALF, HEAD), 1)
        for i in range(N_SUB):
            r0 = SUB * i
            q_h = [q_ref[pl.ds(r0 + HALF * u, HALF), :] for u in range(2)]
            b_h = [b_scr[pl.ds(r0 + HALF * u, HALF), :] for u in range(2)]
            do_h = [do_ref[pl.ds(r0 + HALF * u, HALF), :] for u in range(2)]
            zero = jnp.zeros((HALF, HEAD), F32)
            dq_h, dk_h, a_h = [zero, zero], [zero, zero], [zero, zero]
            for s in range(SUB):
                brow = b_scr[pl.ds(r0 + s, 1), :]
                krow = k_scr[pl.ds(r0 + s, 1), :]
                vrow = v_ref[pl.ds(r0 + s, 1), :]
                dk_row = jnp.zeros((1, HEAD), F32)
                is_lane = lane == r0 + s
                for u in range(s // HALF, 2):
                    diff = b_h[u] - brow
                    if u == s // HALF:
                        diff = jnp.where(not_before[s - HALF * u], diff, NEG_BIG)
                    w = jnp.exp(diff)
                    qw = q_h[u] * w
                    a_col = jnp.sum(qw * krow, axis=-1, keepdims=True)
                    da_col = jnp.sum(do_h[u] * vrow, axis=-1, keepdims=True)
                    a_h[u] = jnp.where(is_lane, a_col, a_h[u])
                    dq_h[u] = dq_h[u] + da_col * (w * krow)
                    dk_row = dk_row + jnp.sum(da_col * qw, axis=0, keepdims=True)
                us = s // HALF
                dk_h[us] = jnp.where(is_row[s - HALF * us], dk_row, dk_h[us])
            for u in range(2):
                rows = pl.ds(r0 + HALF * u, HALF)
                dqd_acc[rows, :] = dq_h[u]
                dkd_acc[rows, :] = dk_h[u]
                ad_scr[rows, :] = a_h[u]
            yield
        dv_in = _dot_tn(ad_scr[...].astype(BF16), dob)
        dq = dqt_acc[...] * c["e_q"] + dqd_acc[...]
        dk = dkh_acc[...] * c["e_k"] + dkd_acc[...]
        rr, cc = _iota2((BLOCK, BLOCK), 0), _iota2((BLOCK, BLOCK), 1)
        db = q * dq - k * dk + jnp.where(_iota2((BLOCK, HEAD), 0) == BLOCK - 1, dbl, 0.0)
        yield
        dg = _tri_left(_tri(cc >= rr), db)
        yield
        real = jnp.logical_or(n > 0, _iota2((BLOCK, HEAD), 0) >= N_PAD)
        df = jnp.where(real, dg / c["f"] - dk, 0.0)
        dq_ref[...] = dq.astype(BF16)
        dv_ref[...] = (dv_acc[...] + dv_in).astype(BF16)
        dfz_ref[...] = (df * (1.0 - lb) * c["sig"] * c["nsig"]).astype(BF16)
        dlb_ref[...] += jnp.sum(df * c["nsig"], axis=0, keepdims=True)

    def body(*refs):
        q_ref, fz_ref, v_ref, lb_ref, st_ref, do_ref = refs[:6]
        dq_ref, dfz_ref, dv_ref, dlb_ref = refs[6 + nr:10 + nr]
        scratch = refs[10 + 2 * nr:19 + 2 * nr]
        finish_ride = _ride_along(refs[6:6 + nr], refs[10 + nr:10 + 2 * nr], ride_modes, refs[19 + 2 * nr:], grid)
        step = pl.program_id(2)

        @pl.when(step == 0)
        def _():
            scratch[0][...] = jnp.zeros_like(scratch[0])
            dlb_ref[...] = jnp.zeros_like(dlb_ref)

        def head(hh):
            cols = pl.ds(hh * HEAD, HEAD)
            return chunk(n_blocks - 1 - step, q_ref.at[:, cols], fz_ref.at[:, cols], v_ref.at[:, cols],
                         lb_ref[:, cols], st_ref.at[hh], do_ref.at[:, cols], dq_ref.at[:, cols],
                         dfz_ref.at[:, cols], dv_ref.at[:, cols], dlb_ref.at[:, cols],
                         *[scr.at[hh] for scr in scratch])

        _lockstep([head(hh) for hh in range(HEADS_PER_STEP)])
        finish_ride()

    def blk(col0):
        return pl.BlockSpec((BLOCK, wide), lambda b, h, s: (b * n_blocks + n_blocks - 1 - s, col0 + h))

    any_spec = pl.BlockSpec(memory_space=pl.ANY)
    per_head = pltpu.VMEM((HEADS_PER_STEP, BLOCK, HEAD), F32)
    return _pcall(
        body, name=name, grid=grid,
        in_specs=[blk(0), blk(n_pairs), blk(2 * n_pairs), pl.BlockSpec((1, wide), lambda b, h, s: (0, h)),
                  pl.BlockSpec((None, HEADS_PER_STEP, HEAD, HEAD),
                               lambda b, h, s: (b * n_blocks + n_blocks - 1 - s, h, 0, 0)),
                  blk(0)] + [any_spec] * nr,
        out_specs=(blk(0), blk(0), blk(0), pl.BlockSpec((None, 1, wide), lambda b, h, s: (b, 0, h)))
        + (any_spec,) * nr,
        out_shape=(jax.ShapeDtypeStruct((t, d), BF16),) * 3 + (jax.ShapeDtypeStruct((bsz, 1, d), F32),)
        + _exchange_shapes(ride, ride_modes),
        scratch_shapes=[per_head] * 9 + _exchange_sems(nr),
        compiler_params=_params("arbitrary", "arbitrary", "arbitrary"),
    )(main, main, main, lbrow, states, do, *ride)


SB_GROUP = 3
SB_ROWS = SB_GROUP * BLOCK
SB_DEAD = -104.0
SB_NEAR = ((0, 0), (1, 0), (0, 1))
SB_HEADS = 2


def _sb_tables():
    j = jnp.bitwise_and(_iota2((2 * BLOCK, 2 * BLOCK), 0), BLOCK - 1)
    s = _iota2((2 * BLOCK, 2 * BLOCK), 1)
    ones = s >= BLOCK
    return (_tri(jnp.logical_or(ones, j > s)), _tri(jnp.logical_or(ones, j <= s)),
            _tri(jnp.logical_or(ones, j < s)))


def _sums(x, table, pieces=2):
    hi = x.astype(BF16)
    if pieces == 1:
        r = _dot(hi, table[:BLOCK])
    else:
        lo = (x - hi.astype(F32)).astype(BF16)
        r = _dot(jnp.concatenate([hi, lo], axis=1), table)
    return r[:, :BLOCK], r[:, BLOCK:]


def _sb_logits(q, ks, scale, causal, pad_row):
    z = _dot_nt(q, ks) * scale
    log_keep = -(jnp.maximum(z, 0.0) + jnp.log(1.0 + jnp.exp(-jnp.abs(z))))
    log_beta = z + log_keep
    if causal is not None:
        log_keep = jnp.where(causal, log_keep, 0.0)
    if pad_row is not None:
        log_keep = log_keep * pad_row
    return z, log_beta, log_keep


def _sb_fwd(qkv, bsz, n_blocks, name):
    t, d3 = qkv.shape
    d = d3 // 3
    n_heads = d // HEAD
    lp = n_blocks * BLOCK
    n_groups = n_blocks // SB_GROUP
    assert n_groups * SB_GROUP == n_blocks
    scale = HEAD ** -0.5

    n_pairs = n_heads // SB_HEADS
    wide = SB_HEADS * HEAD

    def body(q_ref, k_ref, v_ref, upper_ref, o_ref, tot_ref, stop_ref, c_scr):
        g = pl.program_id(2)
        upper = upper_ref[...]
        causal = _iota2((BLOCK, BLOCK), 1) < _iota2((BLOCK, BLOCK), 0)
        lane = _iota2((1, BLOCK), 1)
        o_ref[...] = jnp.zeros_like(o_ref)
        c_scr[...] = jnp.zeros_like(c_scr)

        def tiles(jobs):
            cols = [pl.ds(hh * HEAD, HEAD) for hh, _, _, _, _ in jobs]
            qrows = [pl.ds(r * BLOCK, BLOCK) for _, r, _, _, _ in jobs]
            krows = [pl.ds(pl.multiple_of(m * BLOCK, BLOCK), BLOCK) for _, _, m, _, _ in jobs]
            n_jobs = range(len(jobs))
            lg = [_sb_logits(q_ref[qrows[i], cols[i]], k_ref[krows[i], cols[i]], scale,
                             causal if jobs[i][3] else None, jobs[i][4]) for i in n_jobs]
            sm = [_sums(x[2], upper) for x in lg]
            a_all = []
            for i in n_jobs:
                c = c_scr[qrows[i], cols[i]]
                a = jnp.exp(lg[i][1] + c + sm[i][0])
                a_all.append(jnp.where(causal, a, 0.0) if jobs[i][3] else a)
                c_scr[qrows[i], cols[i]] = c + sm[i][1]
            out = [_dot(a_all[i].astype(BF16), v_ref[krows[i], cols[i]]) for i in n_jobs]
            for i in n_jobs:
                o_ref[qrows[i], cols[i]] += out[i]

        def pad_row_of(m):
            return jnp.where(jnp.logical_or(m > 0, lane >= N_PAD), 1.0, 0.0)

        base = SB_GROUP * g
        heads = range(SB_HEADS)
        own = [(hh, r, base + kb, r == kb, pad_row_of(base) if kb == 0 else None)
               for hh in heads for kb in range(SB_GROUP - 1, -1, -1) for r in range(kb, SB_GROUP)]
        near = [(hh, r, base - 1 - back, False, None) for hh in heads for r, back in SB_NEAR]
        pl.when(g == 0)(lambda: tiles(own))
        pl.when(g > 0)(lambda: tiles(own + near))

        stop_ref[...] = jnp.zeros_like(stop_ref)
        mine = [(hh, r) for hh in heads for r in range(SB_GROUP)]
        reach = [jnp.max(c_scr[pl.ds(r * BLOCK, BLOCK), pl.ds(hh * HEAD, HEAD)]) for hh, r in mine]
        for (hh, r), reach_now in zip(mine, reach):
            def live(carry):
                m, c_max = carry
                return jnp.logical_and(m >= 0, c_max >= SB_DEAD)

            def step(carry, hh=hh, r=r):
                m, _ = carry
                tiles([(hh, r, m, False, pad_row_of(m))])
                return m - 1, jnp.max(c_scr[pl.ds(r * BLOCK, BLOCK), pl.ds(hh * HEAD, HEAD)])

            lowest = jnp.maximum(base - (SB_GROUP - 1 - r), 0)
            m_end, _ = lax.while_loop(live, step, (lowest - 1, reach_now))
            stop_ref[pl.ds(SB_GROUP * hh + r, 1), :] = jnp.broadcast_to((m_end + 1).astype(F32), (1, 128))
        tot_ref[...] = c_scr[...]

    qblk = pl.BlockSpec((SB_ROWS, wide), lambda b, h, g: (b * n_groups + g, h))
    table = pl.BlockSpec((2 * BLOCK, 2 * BLOCK), lambda b, h, g: (0, 0))
    return _pcall(
        body, name=name, grid=(bsz, n_pairs, n_groups),
        in_specs=[qblk, pl.BlockSpec((lp, wide), lambda b, h, g: (b, n_pairs + h)),
                  pl.BlockSpec((lp, wide), lambda b, h, g: (b, 2 * n_pairs + h)), table],
        out_specs=(qblk, qblk, pl.BlockSpec((None, 8, 128), lambda b, h, g: ((b * n_pairs + h) * n_groups + g, 0, 0))),
        out_shape=(jax.ShapeDtypeStruct((t, d), F32), jax.ShapeDtypeStruct((t, d), F32),
                   jax.ShapeDtypeStruct((bsz * n_pairs * n_groups, 8, 128), F32)),
        scratch_shapes=[pltpu.VMEM((SB_ROWS, wide), F32)],
        compiler_params=_params("parallel", "parallel", "arbitrary"),
    )(qkv, qkv, qkv, _sb_tables()[0])


def _sb_bwd(qkv, do, tot, stop, bsz, n_blocks, name):
    t, d3 = qkv.shape
    d = d3 // 3
    n_heads = d // HEAD
    lp = n_blocks * BLOCK
    n_groups = n_blocks // SB_GROUP
    scale = HEAD ** -0.5

    def body(q_ref, k_ref, v_ref, do_ref, tot_ref, stop_ref, incl_ref, excl_ref, dq_ref, dk_ref, dv_ref,
             dk_acc, dv_acc, dq_acc, p_scr, e_scr, dob_scr):
        g = pl.program_id(2)

        @pl.when(g == 0)
        def _():
            dk_acc[...] = jnp.zeros_like(dk_acc)
            dv_acc[...] = jnp.zeros_like(dv_acc)

        incl, excl = incl_ref[...], excl_ref[...]
        causal = _iota2((BLOCK, BLOCK), 1) < _iota2((BLOCK, BLOCK), 0)
        lane = _iota2((1, BLOCK), 1)
        dob_scr[...] = do_ref[...].astype(BF16)
        dq_acc[...] = jnp.zeros_like(dq_acc)
        p_scr[...] = jnp.zeros_like(p_scr)
        e_scr[...] = jnp.zeros_like(e_scr)

        def tiles(jobs):
            n_jobs = range(len(jobs))
            cols = [pl.ds(hh * HEAD, HEAD) for hh, _, _, _, _ in jobs]
            qrows = [pl.ds(r * BLOCK, BLOCK) for _, r, _, _, _ in jobs]
            krows = [pl.ds(pl.multiple_of(m * BLOCK, BLOCK), BLOCK) for _, _, m, _, _ in jobs]
            diag = [dg for _, _, _, dg, _ in jobs]
            lg = [_sb_logits(q_ref[qrows[i], cols[i]], k_ref[krows[i], cols[i]], scale,
                             causal if diag[i] else None, jobs[i][4]) for i in n_jobs]
            d_a = [_dot_nt(dob_scr[qrows[i], cols[i]], v_ref[krows[i], cols[i]]) for i in n_jobs]
            sm = [_sums(lg[i][2], incl) for i in n_jobs]
            a_all = []
            for i in n_jobs:
                p = p_scr[qrows[i], cols[i]]
                a = jnp.exp(lg[i][1] + (tot_ref[qrows[i], cols[i]] - p - sm[i][0]))
                a_all.append(jnp.where(causal, a, 0.0) if diag[i] else a)
                p_scr[qrows[i], cols[i]] = p + sm[i][1]
            gr = [d_a[i] * a_all[i] for i in n_jobs]
            dv_part = [_dot_tn(a_all[i].astype(BF16), dob_scr[qrows[i], cols[i]]) for i in n_jobs]
            gs = [_sums(gr[i], excl, pieces=1) for i in n_jobs]
            dz_all = []
            for i in n_jobs:
                e = e_scr[qrows[i], cols[i]]
                dz = gr[i] - (gr[i] + e + gs[i][0]) * jnp.exp(lg[i][1])
                if diag[i]:
                    dz = jnp.where(causal, dz, 0.0)
                dz_all.append((dz * scale).astype(BF16))
                e_scr[qrows[i], cols[i]] = e + gs[i][1]
            dk_part = [_dot_tn(dz_all[i], q_ref[qrows[i], cols[i]]) for i in n_jobs]
            dq_part = [_dot(dz_all[i], k_ref[krows[i], cols[i]]) for i in n_jobs]
            for i in n_jobs:
                dv_acc[krows[i], cols[i]] += dv_part[i]
                dk_acc[krows[i], cols[i]] += dk_part[i]
                dq_acc[qrows[i], cols[i]] += dq_part[i]

        def pad_row_of(m):
            return jnp.where(jnp.logical_or(m > 0, lane >= N_PAD), 1.0, 0.0)

        base = SB_GROUP * g
        heads = range(SB_HEADS)
        mine = [(hh, r) for hh in heads for r in range(SB_GROUP)]
        stops = [jnp.max(stop_ref[pl.ds(SB_GROUP * hh + r, 1), :]).astype(jnp.int32) for hh, r in mine]
        for (hh, r), stop_now in zip(mine, stops):
            lowest = jnp.maximum(base - (SB_GROUP - 1 - r), 0)

            def step(m, hh=hh, r=r):
                tiles([(hh, r, m, False, pad_row_of(m))])
                return m + 1

            lax.while_loop(lambda m, lowest=lowest: m < lowest, step, jnp.clip(stop_now, 0, lowest))
        own = [(hh, r, base + kb, r == kb, pad_row_of(base) if kb == 0 else None)
               for hh in heads for kb in range(SB_GROUP) for r in range(kb, SB_GROUP)]
        near = [(hh, r, base - 1 - back, False, None) for hh in heads for r, back in reversed(SB_NEAR)]
        pl.when(g == 0)(lambda: tiles(own))
        pl.when(g > 0)(lambda: tiles(near + own))
        dq_ref[...] = dq_acc[...].astype(BF16)

        @pl.when(g == n_groups - 1)
        def _():
            dk_ref[...] = dk_acc[...].astype(BF16)
            dv_ref[...] = dv_acc[...].astype(BF16)

    n_pairs = n_heads // SB_HEADS
    wide = SB_HEADS * HEAD
    qblk = pl.BlockSpec((SB_ROWS, wide), lambda b, h, g: (b * n_groups + g, h))
    kblk = pl.BlockSpec((lp, wide), lambda b, h, g: (b, n_pairs + h))
    vblk = pl.BlockSpec((lp, wide), lambda b, h, g: (b, 2 * n_pairs + h))
    hblk = pl.BlockSpec((lp, wide), lambda b, h, g: (b, h))
    sblk = pl.BlockSpec((None, 8, 128), lambda b, h, g: ((b * n_pairs + h) * n_groups + g, 0, 0))
    table = pl.BlockSpec((2 * BLOCK, 2 * BLOCK), lambda b, h, g: (0, 0))
    _, incl, excl = _sb_tables()
    return _pcall(
        body, name=name, grid=(bsz, n_pairs, n_groups),
        in_specs=[qblk, kblk, vblk, qblk, qblk, sblk, table, table],
        out_specs=(qblk, hblk, hblk),
        out_shape=(jax.ShapeDtypeStruct((t, d), BF16),) * 3,
        scratch_shapes=[pltpu.VMEM((lp, wide), F32), pltpu.VMEM((lp, wide), F32)]
        + [pltpu.VMEM((SB_ROWS, wide), F32)] * 3 + [pltpu.VMEM((SB_ROWS, wide), BF16)],
        compiler_params=_params("parallel", "parallel", "arbitrary"),
    )(qkv, qkv, qkv, do, tot, stop, incl, excl)


def _adamw(w, g, m, v):
    m = ADAM_B1 * m + (1.0 - ADAM_B1) * g
    v = ADAM_B2 * v + (1.0 - ADAM_B2) * (g * g)
    m_hat = m / (1.0 - ADAM_B1 ** ADAM_STEP)
    v_hat = v / (1.0 - ADAM_B2 ** ADAM_STEP)
    delta = -ADAM_LR * (m_hat / (jnp.sqrt(v_hat) + ADAM_EPS) + ADAM_WD * w)
    return delta, m, v


def _update_sharded(w, parts, m, v, name):
    r, c = w.shape
    tr = UPDATE_ROWS if r % UPDATE_ROWS == 0 else r

    def body(w_ref, p_ref, m_ref, v_ref, g_ref, d_ref, nm_ref, nv_ref):
        g = p_ref[0].astype(F32)
        for q in range(1, N_DEV):
            g = g + p_ref[q].astype(F32)
        g_ref[...] = g
        d_ref[...], nm_ref[...], nv_ref[...] = _adamw(w_ref[...], g, m_ref[...], v_ref[...])

    row = pl.BlockSpec((tr, c), lambda i: (i, 0))
    return _pcall(
        body, name=name, grid=(r // tr,),
        in_specs=[row, pl.BlockSpec((N_DEV, tr, c), lambda i: (0, i, 0)), row, row],
        out_specs=(row,) * 4, out_shape=(jax.ShapeDtypeStruct((r, c), F32),) * 4,
        compiler_params=_params("parallel"),
    )(w, parts, m, v)


SMALL_ROWS = 8


def _pack_small(dpre0, dpre1, dpost0, dpost1, dlb, dwon, loss, name):
    d = dpre0.shape[1]
    bsz = dlb.shape[0]

    def body(a0, a1, p0, p1, lb_ref, on_ref, loss_ref, out_ref):
        out_ref[...] = jnp.zeros_like(out_ref)
        out_ref[pl.ds(0, 1), :] = a0[...]
        out_ref[pl.ds(1, 1), :] = a1[...]
        out_ref[pl.ds(2, 1), :] = p0[...]
        out_ref[pl.ds(3, 1), :] = p1[...]
        acc = lb_ref[0]
        for b in range(1, bsz):
            acc = acc + lb_ref[b]
        out_ref[pl.ds(4, 1), :] = acc
        out_ref[pl.ds(5, 1), pl.ds(0, HEAD)] = on_ref[...]
        out_ref[pl.ds(6, 1), pl.ds(0, HEAD)] = loss_ref[pl.ds(0, 1), :]

    return _pcall(body, name=name, out_shape=jax.ShapeDtypeStruct((SMALL_ROWS, d), F32))(
        dpre0, dpre1, dpost0, dpost1, dlb, dwon, loss)


def _update_small(parts, pre, post, lbw, on, moments, name):
    d = pre.shape[1]

    def body(p_ref, pre_ref, post_ref, lbw_ref, on_ref, mpre, mpost, mlb, mon, vpre, vpost, vlb, von,
             loss_ref, *outs):
        def total(r0, nr, width):
            acc = p_ref[0, pl.ds(r0, nr), pl.ds(0, width)]
            for q in range(1, N_DEV):
                acc = acc + p_ref[q, pl.ds(r0, nr), pl.ds(0, width)]
            return acc

        def put(k, w, g, m, v):
            dl, nm, nv = _adamw(w, g, m, v)
            outs[4 * k][...] = g
            outs[4 * k + 1][...] = dl
            outs[4 * k + 2][...] = nm
            outs[4 * k + 3][...] = nv

        put(0, pre_ref[...], total(0, 2, d), mpre[...], vpre[...])
        put(1, post_ref[...], total(2, 2, d), mpost[...], vpost[...])
        a0, a1 = lbw_ref[pl.ds(0, 1), :], lbw_ref[pl.ds(1, 1), :]
        mx = jnp.maximum(a0, a1)
        e0, e1 = jnp.exp(a0 - mx), jnp.exp(a1 - mx)
        p0 = e0 / (e0 + e1)
        g0 = total(4, 1, d) * p0 * (1.0 - p0)
        for r, w, g in ((0, a0, g0), (1, a1, -g0)):
            row = pl.ds(r, 1)
            dl, nm, nv = _adamw(w, g, mlb[row, :], vlb[row, :])
            outs[8][row, :] = g
            outs[9][row, :] = dl
            outs[10][row, :] = nm
            outs[11][row, :] = nv
        put(3, on_ref[...], total(5, 1, HEAD), mon[...], von[...])
        loss_ref[...] = jnp.broadcast_to(total(6, 1, HEAD), loss_ref.shape)

    shapes = []
    for w in (pre, post, lbw, on):
        shapes += [jax.ShapeDtypeStruct(w.shape, F32)] * 4
    return _pcall(body, name=name, out_shape=(jax.ShapeDtypeStruct((8, 128), F32), *shapes))(
        parts, pre, post, lbw, on, *moments)


def kernel(x, meta_tokens, pre_norm, post_norm, hgrn_w_in, hgrn_lb, hgrn_out_norm, hgrn_w_out, sb_w_in, sb_w_out, loss_target, m_meta_tokens, m_pre_norm, m_post_norm, m_hgrn_w_in, m_hgrn_lb, m_hgrn_out_norm, m_hgrn_w_out, m_sb_w_in, m_sb_w_out, v_meta_tokens, v_pre_norm, v_post_norm, v_hgrn_w_in, v_hgrn_lb, v_hgrn_out_norm, v_hgrn_w_out, v_sb_w_in, v_sb_w_out):
    bsz, seq, d = x.shape
    n_blocks = seq // BLOCK + 1
    lp = n_blocks * BLOCK
    s = hgrn_w_in.shape[2]
    dsh = d // N_DEV

    w_in_h, meta_all = _gather_once_per_chip([hgrn_w_in[0].astype(BF16), meta_tokens], "gather_weights")
    meta_full = jnp.transpose(meta_all, (1, 0, 2)).reshape(N_META, d)

    lbrow = jnp.cumsum(jax.nn.softmax(hgrn_lb, axis=0), axis=0)[0:1]

    main0, gate0, yn0, h0 = _norm_inproj(None, pre_norm[0:1], w_in_h, F32, "inproj_hgrn", from_x=(x, meta_full))
    o0, states, w_in_s, w_out_s, w_out_h = _hgrn_fwd(
        main0, lbrow, bsz, n_blocks,
        [sb_w_in[0].astype(BF16), sb_w_out[0].astype(BF16), hgrn_w_out[0].astype(BF16)], ["gather"] * 3, "hgrn_fwd")
    w_out_s = w_out_s.reshape(d, d)
    w_out_h = w_out_h.reshape(d, d)
    h1, y0 = _mix_out(o0, gate0, h0, w_out_h, hgrn_out_norm, post_norm[0:1], True, "mix_out_hgrn")
    main1, gate1, yn1 = _norm_inproj(h1, pre_norm[1:2], w_in_s, BF16, "inproj_sb")
    o1, tot, stop = _sb_fwd(main1, bsz, n_blocks, "sb_fwd")
    dh2, y1, loss_part = _mix_out(o1, gate1, h1, w_out_s, hgrn_out_norm, post_norm[1:2], False, "mix_out_sb",
                                  target=loss_target)

    do1, dgate1, dw_out_s, dpost1, _ = _mix_out_bwd(
        dh2, y1, o1, gate1, w_out_s, hgrn_out_norm, post_norm[1:2], False, "mix_out_sb_bwd")
    dq1, dk1, dv1 = _sb_bwd(main1, do1, tot, stop, bsz, n_blocks, "sb_bwd")
    dproj1 = (dq1, dk1, dv1, dgate1)
    dh1, dpre1 = _inproj_bwd_x(dproj1, w_in_s, h1, pre_norm[1:2], dh2, [], [], "inproj_sb_bwd_x")
    dw_in_s = _inproj_bwd_w(yn1, dproj1, s, "inproj_sb_bwd_w")

    do0, dgate0, dw_out_h, dpost0, dwon = _mix_out_bwd(
        dh1, y0, o0, gate0, w_out_h, hgrn_out_norm, post_norm[0:1], True, "mix_out_hgrn_bwd")
    dq0, dfz0, dv0, dlb, p_in_s, p_out_s, p_out_h = _hgrn_bwd(
        main0, lbrow, states, do0, bsz, n_blocks,
        [dw_in_s, dw_out_s.reshape(N_DEV, dsh, d), dw_out_h.reshape(N_DEV, dsh, d)], ["scatter"] * 3, "hgrn_bwd")
    dproj0 = (dq0, dfz0, dv0, dgate0)
    dw_in_h = _inproj_bwd_w(yn0, dproj0, s, "inproj_hgrn_bwd_w")
    dh0, dpre0, p_in_h = _inproj_bwd_x(
        dproj0, w_in_h, h0, pre_norm[0:1], dh1, [dw_in_h], ["scatter"], "inproj_hgrn_bwd_x")

    dh0 = dh0.reshape(bsz, lp, d)
    grad_x = dh0[:, BLOCK:]
    dmeta = jnp.sum(dh0[:, N_PAD:BLOCK], axis=0)
    dmeta = jnp.transpose(dmeta.reshape(N_META, N_DEV, dsh), (1, 0, 2))
    small = _pack_small(dpre0, dpre1, dpost0, dpost1, dlb, dwon, loss_part, "pack_small")

    p_meta, p_small = _exchange([dmeta, small], ["scatter", "gather"], "exchange_grads")

    u_meta = _update_sharded(meta_tokens, p_meta, m_meta_tokens, v_meta_tokens, "update_meta")
    u_in_h = _update_sharded(hgrn_w_in[0], p_in_h, m_hgrn_w_in[0], v_hgrn_w_in[0], "update_hgrn_w_in")
    u_out_h = _update_sharded(hgrn_w_out[0], p_out_h, m_hgrn_w_out[0], v_hgrn_w_out[0], "update_hgrn_w_out")
    u_in_s = _update_sharded(sb_w_in[0], p_in_s, m_sb_w_in[0], v_sb_w_in[0], "update_sb_w_in")
    u_out_s = _update_sharded(sb_w_out[0], p_out_s, m_sb_w_out[0], v_sb_w_out[0], "update_sb_w_out")
    sm = _update_small(p_small, pre_norm, post_norm, hgrn_lb, hgrn_out_norm,
                       (m_pre_norm, m_post_norm, m_hgrn_lb, m_hgrn_out_norm,
                        v_pre_norm, v_post_norm, v_hgrn_lb, v_hgrn_out_norm), "update_small")
    loss = sm[0][0, 0]
    u_pre, u_post, u_lb, u_on = sm[1:5], sm[5:9], sm[9:13], sm[13:17]

    per_w = [u_meta, u_pre, u_post, tuple(a[None] for a in u_in_h), u_lb, u_on,
             tuple(a[None] for a in u_out_h), tuple(a[None] for a in u_in_s), tuple(a[None] for a in u_out_s)]
    outs = [loss, grad_x]
    for k in range(4):
        outs += [u[k] for u in per_w]
    return tuple(outs)
```

```python
import jax
import jax.numpy as jnp
from jax import lax
from jax.experimental import pallas as pl
from jax.experimental.pallas import tpu as pltpu

F32 = jnp.float32
BF16 = jnp.bfloat16

N_DEV = 8
BLOCK = 128
N_META = 16
N_PAD = BLOCK - N_META
HEAD = 128
SUB = 16
N_SUB = BLOCK // SUB
HALF = 8
EPS = 1e-6
ROW_TILE = 3 * BLOCK
UPDATE_ROWS = 256
K_TILES = (2816, 1408, 768, 384, 128)
NEG_BIG = -1e30

ADAM_LR = 0.001
ADAM_B1 = 0.9
ADAM_B2 = 0.999
ADAM_EPS = 1e-08
ADAM_WD = 0.01
ADAM_STEP = 10

VMEM_LIMIT = 56 * 1024 * 1024


def _pcall(body, **kw):
    return pl.pallas_call(body, **kw)


def _params(*sem):
    return pltpu.CompilerParams(dimension_semantics=sem, vmem_limit_bytes=VMEM_LIMIT)


def _dot(a, b):
    return jnp.dot(a, b, preferred_element_type=F32)


def _dot_nt(a, b):
    return lax.dot_general(a, b, (((1,), (1,)), ((), ())), preferred_element_type=F32)


def _dot_tn(a, b):
    return lax.dot_general(a, b, (((0,), (0,)), ((), ())), preferred_element_type=F32)


def _split(x, pieces):
    out = []
    for _ in range(pieces):
        p = x.astype(BF16)
        out.append(p)
        x = x - p.astype(F32)
    return out


def _tri_left(tri, x, pieces=3):
    return sum(_dot(tri, p) for p in _split(x, pieces))


def _iota2(shape, dim):
    return lax.broadcasted_iota(jnp.int32, shape, dim)


def _tri(cond):
    return jnp.where(cond, 1.0, 0.0).astype(BF16)


def _sig_pair(x):
    e = jnp.exp(-jnp.abs(x))
    r = 1.0 / (1.0 + e)
    er = e * r
    pos = x >= 0
    return jnp.where(pos, r, er), jnp.where(pos, er, r)


def _expand(rows):
    return jnp.concatenate([jnp.broadcast_to(r, (SUB, HEAD)) for r in rows], axis=0)


def _exchange_shapes(arrays, modes):
    return tuple(jax.ShapeDtypeStruct((N_DEV,) + tuple(a.shape[1:] if m == "scatter" else a.shape), a.dtype)
                 for a, m in zip(arrays, modes))


def _exchange_sems(n):
    if n == 0:
        return []
    return [pltpu.SemaphoreType.DMA((n, N_DEV - 1)), pltpu.SemaphoreType.DMA((n, N_DEV - 1)),
            pltpu.SemaphoreType.DMA((n,))]


def _exchange_copies(ins, outs, modes, send_sems, recv_sems, local_sems):
    mx, my, mc = lax.axis_index("x"), lax.axis_index("y"), lax.axis_index("c")
    me = 4 * mx + 2 * my + mc

    def src(i, slot):
        return ins[i].at[slot] if modes[i] == "scatter" else ins[i]

    def peer_of(mask):
        px = 1 - mx if mask & 4 else mx
        py = 1 - my if mask & 2 else my
        pc = 1 - mc if mask & 1 else mc
        return px, py, pc

    def copy(i, mask, dst_slot):
        px, py, pc = peer_of(mask)
        return pltpu.make_async_remote_copy(
            src_ref=src(i, 4 * px + 2 * py + pc), dst_ref=outs[i].at[dst_slot],
            send_sem=send_sems.at[i, mask - 1], recv_sem=recv_sems.at[i, mask - 1],
            device_id=(px, py, pc), device_id_type=pl.DeviceIdType.MESH)

    n = len(ins)
    sends = [copy(i, mask, me) for mask in range(1, N_DEV) for i in range(n)]
    own = [pltpu.make_async_copy(src(i, me), outs[i].at[me], local_sems.at[i]) for i in range(n)]
    arrivals = []
    for mask in range(1, N_DEV):
        px, py, pc = peer_of(mask)
        arrivals += [copy(i, mask, 4 * px + 2 * py + pc) for i in range(n)]

    def start():
        for cp in sends + own:
            cp.start()

    def wait():
        for cp in arrivals:
            cp.wait_recv()
        for cp in sends:
            cp.wait_send()
        for cp in own:
            cp.wait()

    return start, wait


def _gather_once_per_chip(arrays, name):
    n = len(arrays)

    def body(*refs):
        ins, outs = refs[:n], refs[n:2 * n]
        send_sems, recv_sems, local_sems = refs[2 * n:]
        mx, my, mc = lax.axis_index("x"), lax.axis_index("y"), lax.axis_index("c")
        me, sibling = (mx, my, mc), (mx, my, 1 - mc)
        chips = [(1 - mx, my), (mx, 1 - my), (1 - mx, 1 - my)]

        def slot(px, py, pc):
            return 4 * px + 2 * py + pc

        def copy(i, k, block, to, src=None):
            return pltpu.make_async_remote_copy(
                src_ref=outs[i].at[slot(*block)] if src is None else src, dst_ref=outs[i].at[slot(*block)],
                send_sem=send_sems.at[i, k], recv_sem=recv_sems.at[i, k],
                device_id=to, device_id_type=pl.DeviceIdType.MESH)

        own = [pltpu.make_async_copy(ins[i], outs[i].at[slot(*me)], local_sems.at[i]) for i in range(n)]
        first = [copy(i, 0, me, sibling, src=ins[i]) for i in range(n)]
        first += [copy(i, 1 + j, me, (*chip, mc), src=ins[i]) for j, chip in enumerate(chips) for i in range(n)]
        for cp in own + first:
            cp.start()
        passed = []
        for j, chip in enumerate(chips):
            for i in range(n):
                copy(i, 1 + j, (*chip, mc), me).wait_recv()
                cp = copy(i, 4 + j, (*chip, mc), sibling)
                cp.start()
                passed.append(cp)
        for i in range(n):
            copy(i, 0, sibling, me).wait_recv()
            for j, chip in enumerate(chips):
                copy(i, 4 + j, (*chip, 1 - mc), me).wait_recv()
        for cp in first + passed:
            cp.wait_send()
        for cp in own:
            cp.wait()

    any_spec = pl.BlockSpec(memory_space=pl.ANY)
    return _pcall(
        body, name=name, out_shape=_exchange_shapes(arrays, ["gather"] * n),
        in_specs=[any_spec] * n, out_specs=tuple([any_spec] * n),
        scratch_shapes=_exchange_sems(n),
    )(*arrays)


def _exchange(arrays, modes, name):
    n = len(arrays)

    def body(*refs):
        start, wait = _exchange_copies(refs[:n], refs[n:2 * n], modes, *refs[2 * n:])
        start()
        wait()

    any_spec = pl.BlockSpec(memory_space=pl.ANY)
    return _pcall(
        body, name=name, out_shape=_exchange_shapes(arrays, modes),
        in_specs=[any_spec] * n, out_specs=tuple([any_spec] * n),
        scratch_shapes=_exchange_sems(n),
    )(*arrays)


def _tile_blocks_of_x(seq, d):
    assert ROW_TILE == 3 * BLOCK and (seq + BLOCK) % ROW_TILE == 0
    per_seq = (seq + BLOCK) // ROW_TILE

    def spec(k):
        return pl.BlockSpec((None, BLOCK, d),
                            lambda i: (i // per_seq, jnp.maximum(3 * (i % per_seq) - 1 + k, 0), 0))

    return [spec(0), spec(1), spec(2)]


def _norm_inproj(h, wnorm, w_all, main_dtype, name, from_x=None):
    p_n, d, s = w_all.shape
    n_main = 3 * d // s
    tm = ROW_TILE
    if from_x is None:
        t = h.shape[0]
        lead, lead_specs = [h], [pl.BlockSpec((tm, d), lambda i: (i, 0))]
    else:
        x_in, meta = from_x
        t = x_in.shape[0] * (x_in.shape[1] + BLOCK)
        tiles_per_seq = (x_in.shape[1] + BLOCK) // tm
        lead = [x_in, x_in, x_in, meta]
        lead_specs = _tile_blocks_of_x(x_in.shape[1], d) + [pl.BlockSpec((N_META, d), lambda i: (0, 0))]
    n_lead = len(lead)

    def body(*refs):
        wn_ref, w_ref = refs[n_lead:n_lead + 2]
        main_ref, gate_ref, ynt_ref = refs[n_lead + 2:n_lead + 5]
        if from_x is None:
            x = refs[0][...]
        else:
            first_tile = pl.program_id(0) % tiles_per_seq == 0
            meta_block = jnp.concatenate([jnp.zeros((N_PAD, d), F32), refs[3][...]], axis=0)
            x = jnp.concatenate([jnp.where(first_tile, meta_block, refs[0][...]), refs[1][...], refs[2][...]], axis=0)
            refs[n_lead + 5][...] = x
        y = x * lax.rsqrt(jnp.mean(x * x, axis=-1, keepdims=True) + EPS) * wn_ref[...]
        yb = y.astype(BF16)
        ynt_ref[...] = y.T.astype(BF16)
        for p in range(p_n):
            r = _dot(yb, w_ref[p])
            if p < n_main:
                main_ref[:, p * s:(p + 1) * s] = r.astype(main_dtype)
            else:
                gate_ref[:, (p - n_main) * s:(p - n_main + 1) * s] = r

    row = pl.BlockSpec((tm, d), lambda i: (i, 0))
    extra_specs, extra_shapes = ((), ()) if from_x is None else ((row,), (jax.ShapeDtypeStruct((t, d), F32),))
    return _pcall(
        body, name=name, grid=(t // tm,),
        in_specs=lead_specs + [pl.BlockSpec((1, d), lambda i: (0, 0)), pl.BlockSpec((p_n, d, s), lambda i: (0, 0, 0))],
        out_specs=(pl.BlockSpec((tm, 3 * d), lambda i: (i, 0)), row, pl.BlockSpec((d, tm), lambda i: (0, i)))
        + extra_specs,
        out_shape=(jax.ShapeDtypeStruct((t, 3 * d), main_dtype), jax.ShapeDtypeStruct((t, d), F32),
                   jax.ShapeDtypeStruct((d, t), BF16)) + extra_shapes,
        compiler_params=_params("parallel"),
    )(*lead, wnorm, w_all)


def _mix_out(o, gate, h_in, w_out, w_on, w_post, head_norm, name, target=None):
    t, d = o.shape
    n_heads = d // HEAD
    tm = ROW_TILE
    with_loss = target is not None
    if with_loss:
        tiles_per_seq = (target.shape[1] + BLOCK) // tm

    def body(o_ref, g_ref, h_ref, w_ref, won_ref, wp_ref, *rest):
        if with_loss:
            t0_ref, t1_ref, t2_ref, hout_ref, y_ref, loss_ref, u_scr = rest
        else:
            hout_ref, y_ref, u_scr = rest
        for hh in range(n_heads):
            cs = slice(hh * HEAD, (hh + 1) * HEAD)
            oh = o_ref[:, cs]
            gt = g_ref[:, cs]
            if head_norm:
                oh = oh * lax.rsqrt(jnp.mean(oh * oh, axis=-1, keepdims=True) + EPS) * won_ref[...]
            u_scr[:, cs] = (oh * (gt * jax.nn.sigmoid(gt))).astype(BF16)
        y = _dot(u_scr[...], w_ref[...])
        y_ref[...] = y
        r = y * lax.rsqrt(jnp.mean(y * y, axis=-1, keepdims=True) + EPS) * wp_ref[...]
        h_out = h_ref[...] + r
        if not with_loss:
            hout_ref[...] = h_out
            return
        i = pl.program_id(0)

        @pl.when(i == 0)
        def _():
            loss_ref[...] = jnp.zeros_like(loss_ref)

        tgt = jnp.concatenate([t0_ref[...], t1_ref[...], t2_ref[...]], axis=0)
        real = jnp.logical_or(i % tiles_per_seq > 0, _iota2((tm, d), 0) >= BLOCK)
        err = jnp.where(real, h_out - tgt, 0.0)
        hout_ref[...] = err * (1.0 / d)
        part = jnp.sum(jnp.sum(err * err, axis=-1, keepdims=True), axis=0, keepdims=True)
        loss_ref[...] += part * (0.5 / d)

    row = pl.BlockSpec((tm, d), lambda i: (i, 0))
    in_specs = [row, row, row, pl.BlockSpec((d, d), lambda i: (0, 0)),
                pl.BlockSpec((1, HEAD), lambda i: (0, 0)), pl.BlockSpec((1, d), lambda i: (0, 0))]
    out_specs, out_shape = (row, row), (jax.ShapeDtypeStruct((t, d), F32), jax.ShapeDtypeStruct((t, d), F32))
    args = (o, gate, h_in, w_out, w_on, w_post)
    if with_loss:
        in_specs += _tile_blocks_of_x(target.shape[1], d)
        out_specs += (pl.BlockSpec((8, 128), lambda i: (0, 0)),)
        out_shape += (jax.ShapeDtypeStruct((8, 128), F32),)
        args += (target, target, target)
    return _pcall(
        body, name=name, grid=(t // tm,), in_specs=in_specs, out_specs=out_specs, out_shape=out_shape,
        scratch_shapes=[pltpu.VMEM((tm, d), BF16)],
        compiler_params=_params("arbitrary" if with_loss else "parallel"),
    )(*args)


def _mix_out_bwd(dh, y, o, gate, w_out, w_on, w_post, head_norm, name):
    t, d = o.shape
    n_heads = d // HEAD
    tm = ROW_TILE
    last = t // tm - 1

    def body(dh_ref, y_ref, o_ref, g_ref, w_ref, won_ref, wp_ref,
             do_ref, dg_ref, dw_ref, dwp_ref, dwon_ref, u_scr, acc):
        i = pl.program_id(0)

        @pl.when(i == 0)
        def _():
            acc[...] = jnp.zeros_like(acc)
            dwp_ref[...] = jnp.zeros_like(dwp_ref)
            dwon_ref[...] = jnp.zeros_like(dwon_ref)

        yv = y_ref[...]
        rs = lax.rsqrt(jnp.mean(yv * yv, axis=-1, keepdims=True) + EPS)
        yh = yv * rs
        dr = dh_ref[...]
        dwp_ref[...] += jnp.sum(dr * yh, axis=0, keepdims=True)
        wd = dr * wp_ref[...]
        dy = rs * (wd - yh * jnp.mean(wd * yh, axis=-1, keepdims=True))
        dyb = dy.astype(BF16)
        du = _dot_nt(dyb, w_ref[...])
        for hh in range(n_heads):
            cs = slice(hh * HEAD, (hh + 1) * HEAD)
            oh = o_ref[:, cs]
            gt = g_ref[:, cs]
            sg = jax.nn.sigmoid(gt)
            sl = gt * sg
            duh = du[:, cs]
            if head_norm:
                rsh = lax.rsqrt(jnp.mean(oh * oh, axis=-1, keepdims=True) + EPS)
                ohat = oh * rsh
                on = ohat * won_ref[...]
            else:
                on = oh
            u_scr[:, cs] = (on * sl).astype(BF16)
            don = duh * sl
            dg_ref[:, cs] = (duh * on * (sg * (1.0 + gt * (1.0 - sg)))).astype(BF16)
            if head_norm:
                dwon_ref[...] += jnp.sum(don * ohat, axis=0, keepdims=True)
                wdn = don * won_ref[...]
                do_ref[:, cs] = rsh * (wdn - ohat * jnp.mean(wdn * ohat, axis=-1, keepdims=True))
            else:
                do_ref[:, cs] = don
        acc[...] += _dot_tn(u_scr[...], dyb)

        @pl.when(i == last)
        def _():
            dw_ref[...] = acc[...].astype(BF16)

    row = pl.BlockSpec((tm, d), lambda i: (i, 0))
    return _pcall(
        body, name=name, grid=(t // tm,),
        in_specs=[row, row, row, row, pl.BlockSpec((d, d), lambda i: (0, 0)),
                  pl.BlockSpec((1, HEAD), lambda i: (0, 0)), pl.BlockSpec((1, d), lambda i: (0, 0))],
        out_specs=(row, row, pl.BlockSpec((d, d), lambda i: (0, 0)), pl.BlockSpec((1, d), lambda i: (0, 0)),
                   pl.BlockSpec((1, HEAD), lambda i: (0, 0))),
        out_shape=(jax.ShapeDtypeStruct((t, d), F32), jax.ShapeDtypeStruct((t, d), BF16),
                   jax.ShapeDtypeStruct((d, d), BF16), jax.ShapeDtypeStruct((1, d), F32),
                   jax.ShapeDtypeStruct((1, HEAD), F32)),
        scratch_shapes=[pltpu.VMEM((tm, d), BF16), pltpu.VMEM((d, d), F32)],
        compiler_params=_params("arbitrary"),
    )(dh, y, o, gate, w_out, w_on, w_post)


def _inproj_bwd_x(dparts, w_all, h, wnorm, dres, ride, ride_modes, name):
    t, d = h.shape
    p_n, _, s = w_all.shape
    per = d // s
    tm = ROW_TILE
    nr = len(ride)
    grid = (t // tm,)

    def body(*refs):
        d0_ref, d1_ref, d2_ref, d3_ref, w_ref, h_ref, wn_ref, dres_ref = refs[:8]
        dh_ref, dwn_ref = refs[8 + nr:10 + nr]
        finish_ride = _ride_along(refs[8:8 + nr], refs[10 + nr:10 + 2 * nr], ride_modes, refs[10 + 2 * nr:], grid)
        i = pl.program_id(0)

        @pl.when(i == 0)
        def _():
            dwn_ref[...] = jnp.zeros_like(dwn_ref)

        pieces = (d0_ref, d1_ref, d2_ref, d3_ref)
        dyn = jnp.zeros((tm, d), F32)
        for p in range(p_n):
            blk = pieces[p // per][:, (p % per) * s:(p % per + 1) * s]
            dyn = dyn + _dot_nt(blk, w_ref[p])
        x = h_ref[...]
        rs = lax.rsqrt(jnp.mean(x * x, axis=-1, keepdims=True) + EPS)
        xh = x * rs
        dwn_ref[...] += jnp.sum(dyn * xh, axis=0, keepdims=True)
        wd = dyn * wn_ref[...]
        dh_ref[...] = dres_ref[...] + rs * (wd - xh * jnp.mean(wd * xh, axis=-1, keepdims=True))
        finish_ride()

    row = pl.BlockSpec((tm, d), lambda i: (i, 0))
    any_spec = pl.BlockSpec(memory_space=pl.ANY)
    return _pcall(
        body, name=name, grid=grid,
        in_specs=[row, row, row, row, pl.BlockSpec((p_n, d, s), lambda i: (0, 0, 0)),
                  row, pl.BlockSpec((1, d), lambda i: (0, 0)), row] + [any_spec] * nr,
        out_specs=(row, pl.BlockSpec((1, d), lambda i: (0, 0))) + (any_spec,) * nr,
        out_shape=(jax.ShapeDtypeStruct((t, d), F32), jax.ShapeDtypeStruct((1, d), F32))
        + _exchange_shapes(ride, ride_modes),
        scratch_shapes=_exchange_sems(nr),
        compiler_params=_params("arbitrary"),
    )(*dparts, w_all, h, wnorm, dres, *ride)


def _inproj_bwd_w(ynt, dparts, s, name):
    d, t = ynt.shape
    per = d // s
    n_sh = len(dparts) * per
    tk = next(c for c in K_TILES if t % c == 0)
    last = t // tk - 1

    def body(ynt_ref, d0_ref, d1_ref, d2_ref, d3_ref, dw_ref, acc):
        p, i = pl.program_id(0), pl.program_id(1)

        @pl.when(i == 0)
        def _():
            acc[...] = jnp.zeros_like(acc)

        for a, piece in enumerate((d0_ref, d1_ref, d2_ref, d3_ref)):
            @pl.when(p // per == a)
            def _():
                acc[...] += _dot(ynt_ref[...], piece[...])

        @pl.when(i == last)
        def _():
            dw_ref[...] = acc[...].astype(BF16)

    def piece_spec(a):
        return pl.BlockSpec((tk, s), lambda p, i: (jnp.where(p // per == a, i, 0),
                                                   jnp.where(p // per == a, p % per, 0)))

    return _pcall(
        body, name=name, grid=(n_sh, t // tk),
        in_specs=[pl.BlockSpec((d, tk), lambda p, i: (0, i))] + [piece_spec(a) for a in range(4)],
        out_specs=pl.BlockSpec((None, d, s), lambda p, i: (p, 0, 0)),
        out_shape=jax.ShapeDtypeStruct((n_sh, d, s), BF16),
        scratch_shapes=[pltpu.VMEM((d, s), F32)],
        compiler_params=_params("parallel", "arbitrary"),
    )(ynt, *dparts)


def _hgrn_gates(fz_ref, lb, b_ref, k_ref):
    sig, nsig = _sig_pair(fz_ref[...])
    f = lb + (1.0 - lb) * sig
    rr, cc = _iota2((BLOCK, BLOCK), 0), _iota2((BLOCK, BLOCK), 1)
    b_ref[...] = _tri_left(_tri(cc <= rr), jnp.log(f))
    k_ref[...] = (1.0 - lb) * nsig


def _hgrn_common(q_ref, b_ref, k_ref):
    b, k = b_ref[...], k_ref[...]
    bend = [b_ref[pl.ds(SUB * j + SUB - 1, 1), :] for j in range(N_SUB)]
    bref = [jnp.zeros((1, HEAD), F32)] + bend[:-1]
    refrow, bendrow = _expand(bref), _expand(bend)
    e_q = jnp.exp(b - refrow)
    e_k = jnp.exp(bendrow - b)
    qt = q_ref[...] * e_q
    kh = k * e_k
    bl = bend[-1]
    return dict(k=k, b=b, bend=bend, bref=bref, refrow=refrow, e_q=e_q, e_k=e_k, qt=qt, kh=kh, bl=bl)


HEADS_PER_STEP = 8


def _lockstep(chunks):
    live = list(chunks)
    while live:
        still = []
        for gen in live:
            try:
                next(gen)
                still.append(gen)
            except StopIteration:
                pass
        live = still


def _ride_along(ride_ins, ride_outs, modes, sems, grid):
    if not ride_ins:
        return lambda: None
    ids = [pl.program_id(a) for a in range(len(grid))]
    start, wait = _exchange_copies(ride_ins, ride_outs, modes, *sems)
    first, last = ids[0] == 0, ids[0] == grid[0] - 1
    for a in range(1, len(grid)):
        first = jnp.logical_and(first, ids[a] == 0)
        last = jnp.logical_and(last, ids[a] == grid[a] - 1)
    pl.when(first)(start)
    return lambda: pl.when(last)(wait)


def _hgrn_fwd(main, lbrow, bsz, n_blocks, ride, ride_modes, name):
    t, d3 = main.shape
    d = d3 // 3
    n_heads = d // HEAD
    n_pairs = n_heads // HEADS_PER_STEP
    wide = HEADS_PER_STEP * HEAD
    nr = len(ride)
    grid = (bsz, n_pairs, n_blocks)

    def chunk(q_ref, fz_ref, v_ref, lb, o_ref, st_ref, b_scr, k_scr, s_scr, o_acc):
        _hgrn_gates(fz_ref, lb, b_scr, k_scr)
        yield
        c = _hgrn_common(q_ref, b_scr, k_scr)
        yield
        s_t = s_scr[...]
        st_ref[...] = s_t
        vb = v_ref[...].astype(BF16)
        q_state = c["qt"] * _expand([jnp.exp(r) for r in c["bref"]])
        o_state = _dot_nt(q_state.astype(BF16), s_t.astype(BF16))
        js = range(N_SUB - 1)
        khb = c["kh"].astype(BF16)
        lhs = [(c["qt"][SUB * (j + 1):] * jnp.exp(c["refrow"][SUB * (j + 1):] - c["bend"][j])).astype(BF16)
               for j in js]
        yield
        a_js = [_dot_nt(lhs[j], khb[SUB * j:SUB * (j + 1)]) for j in js]
        k_state = c["kh"] * _expand([jnp.exp(c["bl"] - r) for r in c["bend"]])
        s_scr[...] = s_t * jnp.exp(c["bl"]) + _dot_tn(vb, k_state.astype(BF16))
        yield
        o_js = [_dot(a_js[j].astype(BF16), vb[SUB * j:SUB * (j + 1)]) for j in js]
        yield
        o_acc[...] = o_state
        for j in js:
            o_acc[SUB * (j + 1):, :] += o_js[j]
        not_before = [_iota2((HALF, HEAD), 0) >= row for row in range(HALF)]
        for i in range(N_SUB):
            r0 = SUB * i
            q_h = [q_ref[pl.ds(r0 + HALF * u, HALF), :] for u in range(2)]
            b_h = [b_scr[pl.ds(r0 + HALF * u, HALF), :] for u in range(2)]
            o_h = [jnp.zeros((HALF, HEAD), F32) for _ in range(2)]
            for s in range(SUB):
                brow = b_scr[pl.ds(r0 + s, 1), :]
                krow = k_scr[pl.ds(r0 + s, 1), :]
                vrow = v_ref[pl.ds(r0 + s, 1), :]
                for u in range(s // HALF, 2):
                    diff = b_h[u] - brow
                    if u == s // HALF:
                        diff = jnp.where(not_before[s - HALF * u], diff, NEG_BIG)
                    col = jnp.sum(q_h[u] * krow * jnp.exp(diff), axis=-1, keepdims=True)
                    o_h[u] = o_h[u] + col * vrow
            for u in range(2):
                o_acc[pl.ds(r0 + HALF * u, HALF), :] += o_h[u]
            yield
        o_ref[...] = o_acc[...]

    def body(*refs):
        q_ref, fz_ref, v_ref, lb_ref = refs[:4]
        o_ref, st_ref, b_out, k_out = refs[4 + nr:8 + nr]
        s_scr, o_acc = refs[8 + 2 * nr:10 + 2 * nr]
        finish_ride = _ride_along(refs[4:4 + nr], refs[8 + nr:8 + 2 * nr], ride_modes, refs[10 + 2 * nr:], grid)

        @pl.when(pl.program_id(2) == 0)
        def _():
            s_scr[...] = jnp.zeros_like(s_scr)

        def head(hh):
            cols = pl.ds(hh * HEAD, HEAD)
            return chunk(q_ref.at[:, cols], fz_ref.at[:, cols], v_ref.at[:, cols], lb_ref[:, cols],
                         o_ref.at[:, cols], st_ref.at[hh], b_out.at[:, cols], k_out.at[:, cols],
                         s_scr.at[hh], o_acc.at[hh])

        _lockstep([head(hh) for hh in range(HEADS_PER_STEP)])
        finish_ride()

    def blk(col0):
        return pl.BlockSpec((BLOCK, wide), lambda b, h, n: (b * n_blocks + n, col0 + h))

    any_spec = pl.BlockSpec(memory_space=pl.ANY)
    per_head = pltpu.VMEM((HEADS_PER_STEP, BLOCK, HEAD), F32)
    return _pcall(
        body, name=name, grid=grid,
        in_specs=[blk(0), blk(n_pairs), blk(2 * n_pairs), pl.BlockSpec((1, wide), lambda b, h, n: (0, h))]
        + [any_spec] * nr,
        out_specs=(blk(0), pl.BlockSpec((None, HEADS_PER_STEP, HEAD, HEAD), lambda b, h, n: (b * n_blocks + n, h, 0, 0)),
                   blk(0), blk(0)) + (any_spec,) * nr,
        out_shape=(jax.ShapeDtypeStruct((t, d), F32),
                   jax.ShapeDtypeStruct((bsz * n_blocks, n_heads, HEAD, HEAD), F32),
                   jax.ShapeDtypeStruct((t, d), F32), jax.ShapeDtypeStruct((t, d), F32))
        + _exchange_shapes(ride, ride_modes),
        scratch_shapes=[per_head] * 2 + _exchange_sems(nr),
        compiler_params=_params("arbitrary", "arbitrary", "arbitrary"),
    )(main, main, main, lbrow, *ride)


def _hgrn_bwd(main, b_all, k_all, lbrow, states, do, bsz, n_blocks, ride, ride_modes, name):
    t, d3 = main.shape
    d = d3 // 3
    n_pairs = d // HEAD // HEADS_PER_STEP
    wide = HEADS_PER_STEP * HEAD
    nr = len(ride)
    grid = (bsz, n_pairs, n_blocks)

    def chunk(n, q_ref, b_scr, k_scr, v_ref, lb, st_ref, do_ref, dq_ref, dfz_ref, dv_ref, dlb_ref,
              ds_scr, dqt_acc, dkh_acc, dv_acc, dqd_acc, dkd_acc, ad_scr):
        c = _hgrn_common(q_ref, b_scr, k_scr)
        yield
        q, k = q_ref[...], c["k"]
        vb = v_ref[...].astype(BF16)
        dob = do_ref[...].astype(BF16)
        s0_t = st_ref[...]
        ds1_t = ds_scr[...]
        e_ref = _expand([jnp.exp(r) for r in c["bref"]])
        e_end = _expand([jnp.exp(c["bl"] - r) for r in c["bend"]])
        e_bl = jnp.exp(c["bl"])
        q_state = c["qt"] * e_ref
        k_state = c["kh"] * e_end
        dq_state = _dot(dob, s0_t.astype(BF16))
        dk_state = _dot(vb, ds1_t.astype(BF16))
        dv_state = _dot_nt(k_state.astype(BF16), ds1_t.astype(BF16))
        ds_scr[...] = ds1_t * e_bl + _dot_tn(dob, q_state.astype(BF16))
        js = range(N_SUB - 1)
        lo = [slice(SUB * j, SUB * (j + 1)) for j in js]
        khb = c["kh"].astype(BF16)
        dj = [jnp.exp(c["refrow"][SUB * (j + 1):] - c["bend"][j]) for j in js]
        lhs = [(c["qt"][SUB * (j + 1):] * dj[j]).astype(BF16) for j in js]
        yield
        a_js = [_dot_nt(lhs[j], khb[lo[j]]) for j in js]
        da_js = [_dot_nt(dob[SUB * (j + 1):], vb[lo[j]]).astype(BF16) for j in js]
        dqt_acc[...] = dq_state * e_ref
        dkh_acc[...] = dk_state * e_end
        dv_acc[...] = dv_state
        dbl = (jnp.sum(s0_t * ds1_t, axis=0, keepdims=True) * e_bl
               + jnp.sum(k_state * dk_state, axis=0, keepdims=True))
        yield
        dv_js = [_dot_tn(a_js[j].astype(BF16), dob[SUB * (j + 1):]) for j in js]
        dq_js = [_dot(da_js[j], khb[lo[j]]) * dj[j] for j in js]
        dk_js = [_dot_tn(da_js[j], lhs[j]) for j in js]
        yield
        for j in js:
            dv_acc[lo[j], :] += dv_js[j]
            dqt_acc[SUB * (j + 1):, :] += dq_js[j]
            dkh_acc[lo[j], :] += dk_js[j]
        not_before = [_iota2((HALF, HEAD), 0) >= row for row in range(HALF)]
        is_row = [_iota2((HALF, HEAD), 0) == row for row in range(HALF)]
        lane = _iota2((HALF, HEAD), 1)
        for i in range(N_SUB):
            r0 = SUB * i
            q_h = [q_ref[pl.ds(r0 + HALF * u, HALF), :] for u in range(2)]
            b_h = [b_scr[pl.ds(r0 + HALF * u, HALF), :] for u in range(2)]
            do_h = [do_ref[pl.ds(r0 + HALF * u, HALF), :] for u in range(2)]
            zero = jnp.zeros((HALF, HEAD), F32)
            dq_h, dk_h, a_h = [zero, zero], [zero, zero], [zero, zero]
            for s in range(SUB):
                brow = b_scr[pl.ds(r0 + s, 1), :]
                krow = k_scr[pl.ds(r0 + s, 1), :]
                vrow = v_ref[pl.ds(r0 + s, 1), :]
                dk_row = jnp.zeros((1, HEAD), F32)
                is_lane = lane == r0 + s
                for u in range(s // HALF, 2):
                    diff = b_h[u] - brow
                    if u == s // HALF:
                        diff = jnp.where(not_before[s - HALF * u], diff, NEG_BIG)
                    w = jnp.exp(diff)
                    qw = q_h[u] * w
                    a_col = jnp.sum(qw * krow, axis=-1, keepdims=True)
                    da_col = jnp.sum(do_h[u] * vrow, axis=-1, keepdims=True)
                    a_h[u] = jnp.where(is_lane, a_col, a_h[u])
                    dq_h[u] = dq_h[u] + da_col * (w * krow)
                    dk_row = dk_row + jnp.sum(da_col * qw, axis=0, keepdims=True)
                us = s // HALF
                dk_h[us] = jnp.where(is_row[s - HALF * us], dk_row, dk_h[us])
            for u in range(2):
                rows = pl.ds(r0 + HALF * u, HALF)
                dqd_acc[rows, :] = dq_h[u]
                dkd_acc[rows, :] = dk_h[u]
                ad_scr[rows, :] = a_h[u]
            yield
        dv_in = _dot_tn(ad_scr[...].astype(BF16), dob)
        dq = dqt_acc[...] * c["e_q"] + dqd_acc[...]
        dk = dkh_acc[...] * c["e_k"] + dkd_acc[...]
        rr, cc = _iota2((BLOCK, BLOCK), 0), _iota2((BLOCK, BLOCK), 1)
        db = q * dq - k * dk + jnp.where(_iota2((BLOCK, HEAD), 0) == BLOCK - 1, dbl, 0.0)
        yield
        dg = _tri_left(_tri(cc >= rr), db)
        yield
        real = jnp.logical_or(n > 0, _iota2((BLOCK, HEAD), 0) >= N_PAD)
        df = jnp.where(real, dg / (1.0 - k) - dk, 0.0)
        nsig = k * (1.0 / (1.0 - lb))
        dq_ref[...] = dq.astype(BF16)
        dv_ref[...] = (dv_acc[...] + dv_in).astype(BF16)
        dfz_ref[...] = (df * k * (1.0 - nsig)).astype(BF16)
        dlb_ref[...] += jnp.sum(df * nsig, axis=0, keepdims=True)

    def body(*refs):
        q_ref, b_ref, k_ref, v_ref, lb_ref, st_ref, do_ref = refs[:7]
        dq_ref, dfz_ref, dv_ref, dlb_ref = refs[7 + nr:11 + nr]
        scratch = refs[11 + 2 * nr:18 + 2 * nr]
        finish_ride = _ride_along(refs[7:7 + nr], refs[11 + nr:11 + 2 * nr], ride_modes, refs[18 + 2 * nr:], grid)
        step = pl.program_id(2)

        @pl.when(step == 0)
        def _():
            scratch[0][...] = jnp.zeros_like(scratch[0])
            dlb_ref[...] = jnp.zeros_like(dlb_ref)

        def head(hh):
            cols = pl.ds(hh * HEAD, HEAD)
            return chunk(n_blocks - 1 - step, q_ref.at[:, cols], b_ref.at[:, cols], k_ref.at[:, cols],
                         v_ref.at[:, cols], lb_ref[:, cols], st_ref.at[hh], do_ref.at[:, cols], dq_ref.at[:, cols],
                         dfz_ref.at[:, cols], dv_ref.at[:, cols], dlb_ref.at[:, cols],
                         *[scr.at[hh] for scr in scratch])

        _lockstep([head(hh) for hh in range(HEADS_PER_STEP)])
        finish_ride()

    def blk(col0):
        return pl.BlockSpec((BLOCK, wide), lambda b, h, s: (b * n_blocks + n_blocks - 1 - s, col0 + h))

    any_spec = pl.BlockSpec(memory_space=pl.ANY)
    per_head = pltpu.VMEM((HEADS_PER_STEP, BLOCK, HEAD), F32)
    return _pcall(
        body, name=name, grid=grid,
        in_specs=[blk(0), blk(0), blk(0), blk(2 * n_pairs), pl.BlockSpec((1, wide), lambda b, h, s: (0, h)),
                  pl.BlockSpec((None, HEADS_PER_STEP, HEAD, HEAD),
                               lambda b, h, s: (b * n_blocks + n_blocks - 1 - s, h, 0, 0)),
                  blk(0)] + [any_spec] * nr,
        out_specs=(blk(0), blk(0), blk(0), pl.BlockSpec((None, 1, wide), lambda b, h, s: (b, 0, h)))
        + (any_spec,) * nr,
        out_shape=(jax.ShapeDtypeStruct((t, d), BF16),) * 3 + (jax.ShapeDtypeStruct((bsz, 1, d), F32),)
        + _exchange_shapes(ride, ride_modes),
        scratch_shapes=[per_head] * 7 + _exchange_sems(nr),
        compiler_params=_params("arbitrary", "arbitrary", "arbitrary"),
    )(main, b_all, k_all, main, lbrow, states, do, *ride)


SB_GROUP = 3
SB_ROWS = SB_GROUP * BLOCK
SB_DEAD = -104.0
SB_NEAR = ((0, 0), (1, 0), (0, 1))
SB_HEADS = 2


def _sb_tables():
    j = jnp.bitwise_and(_iota2((2 * BLOCK, 2 * BLOCK), 0), BLOCK - 1)
    s = _iota2((2 * BLOCK, 2 * BLOCK), 1)
    ones = s >= BLOCK
    return (_tri(jnp.logical_or(ones, j > s)), _tri(jnp.logical_or(ones, j <= s)),
            _tri(jnp.logical_or(ones, j < s)))


def _sums(x, table, pieces=2):
    hi = x.astype(BF16)
    if pieces == 1:
        r = _dot(hi, table[:BLOCK])
    else:
        lo = (x - hi.astype(F32)).astype(BF16)
        r = _dot(jnp.concatenate([hi, lo], axis=1), table)
    return r[:, :BLOCK], r[:, BLOCK:]


def _sb_logits(q, ks, scale, causal, pad_row):
    z = _dot_nt(q, ks) * scale
    log_keep = -(jnp.maximum(z, 0.0) + jnp.log(1.0 + jnp.exp(-jnp.abs(z))))
    log_beta = z + log_keep
    if causal is not None:
        log_keep = jnp.where(causal, log_keep, 0.0)
    if pad_row is not None:
        log_keep = log_keep * pad_row
    return z, log_beta, log_keep


def _sb_fwd(qkv, bsz, n_blocks, name):
    t, d3 = qkv.shape
    d = d3 // 3
    n_heads = d // HEAD
    lp = n_blocks * BLOCK
    n_groups = n_blocks // SB_GROUP
    assert n_groups * SB_GROUP == n_blocks
    scale = HEAD ** -0.5

    n_pairs = n_heads // SB_HEADS
    wide = SB_HEADS * HEAD

    def body(q_ref, k_ref, v_ref, upper_ref, o_ref, tot_ref, stop_ref, c_scr):
        g = pl.program_id(2)
        upper = upper_ref[...]
        causal = _iota2((BLOCK, BLOCK), 1) < _iota2((BLOCK, BLOCK), 0)
        lane = _iota2((1, BLOCK), 1)
        o_ref[...] = jnp.zeros_like(o_ref)
        c_scr[...] = jnp.zeros_like(c_scr)

        def tiles(jobs):
            cols = [pl.ds(hh * HEAD, HEAD) for hh, _, _, _, _ in jobs]
            qrows = [pl.ds(r * BLOCK, BLOCK) for _, r, _, _, _ in jobs]
            krows = [pl.ds(pl.multiple_of(m * BLOCK, BLOCK), BLOCK) for _, _, m, _, _ in jobs]
            n_jobs = range(len(jobs))
            lg = [_sb_logits(q_ref[qrows[i], cols[i]], k_ref[krows[i], cols[i]], scale,
                             causal if jobs[i][3] else None, jobs[i][4]) for i in n_jobs]
            sm = [_sums(x[2], upper) for x in lg]
            a_all = []
            for i in n_jobs:
                c = c_scr[qrows[i], cols[i]]
                a = jnp.exp(lg[i][1] + c + sm[i][0])
                a_all.append(jnp.where(causal, a, 0.0) if jobs[i][3] else a)
                c_scr[qrows[i], cols[i]] = c + sm[i][1]
            out = [_dot(a_all[i].astype(BF16), v_ref[krows[i], cols[i]]) for i in n_jobs]
            for i in n_jobs:
                o_ref[qrows[i], cols[i]] += out[i]

        def pad_row_of(m):
            return jnp.where(jnp.logical_or(m > 0, lane >= N_PAD), 1.0, 0.0)

        base = SB_GROUP * g
        heads = range(SB_HEADS)
        own = [(hh, r, base + kb, r == kb, pad_row_of(base) if kb == 0 else None)
               for hh in heads for kb in range(SB_GROUP - 1, -1, -1) for r in range(kb, SB_GROUP)]
        near = [(hh, r, base - 1 - back, False, None) for hh in heads for r, back in SB_NEAR]
        pl.when(g == 0)(lambda: tiles(own))
        pl.when(g > 0)(lambda: tiles(own + near))

        stop_ref[...] = jnp.zeros_like(stop_ref)
        mine = [(hh, r) for hh in heads for r in range(SB_GROUP)]
        reach = [jnp.max(c_scr[pl.ds(r * BLOCK, BLOCK), pl.ds(hh * HEAD, HEAD)]) for hh, r in mine]
        for (hh, r), reach_now in zip(mine, reach):
            def live(carry):
                m, c_max = carry
                return jnp.logical_and(m >= 0, c_max >= SB_DEAD)

            def step(carry, hh=hh, r=r):
                m, _ = carry
                tiles([(hh, r, m, False, pad_row_of(m))])
                return m - 1, jnp.max(c_scr[pl.ds(r * BLOCK, BLOCK), pl.ds(hh * HEAD, HEAD)])

            lowest = jnp.maximum(base - (SB_GROUP - 1 - r), 0)
            m_end, _ = lax.while_loop(live, step, (lowest - 1, reach_now))
            stop_ref[pl.ds(SB_GROUP * hh + r, 1), :] = jnp.broadcast_to((m_end + 1).astype(F32), (1, 128))
        tot_ref[...] = c_scr[...]

    qblk = pl.BlockSpec((SB_ROWS, wide), lambda b, h, g: (b * n_groups + g, h))
    table = pl.BlockSpec((2 * BLOCK, 2 * BLOCK), lambda b, h, g: (0, 0))
    return _pcall(
        body, name=name, grid=(bsz, n_pairs, n_groups),
        in_specs=[qblk, pl.BlockSpec((lp, wide), lambda b, h, g: (b, n_pairs + h)),
                  pl.BlockSpec((lp, wide), lambda b, h, g: (b, 2 * n_pairs + h)), table],
        out_specs=(qblk, qblk, pl.BlockSpec((None, 8, 128), lambda b, h, g: ((b * n_pairs + h) * n_groups + g, 0, 0))),
        out_shape=(jax.ShapeDtypeStruct((t, d), F32), jax.ShapeDtypeStruct((t, d), F32),
                   jax.ShapeDtypeStruct((bsz * n_pairs * n_groups, 8, 128), F32)),
        scratch_shapes=[pltpu.VMEM((SB_ROWS, wide), F32)],
        compiler_params=_params("parallel", "parallel", "arbitrary"),
    )(qkv, qkv, qkv, _sb_tables()[0])


def _sb_bwd(qkv, do, tot, stop, bsz, n_blocks, name):
    t, d3 = qkv.shape
    d = d3 // 3
    n_heads = d // HEAD
    lp = n_blocks * BLOCK
    n_groups = n_blocks // SB_GROUP
    scale = HEAD ** -0.5

    def body(q_ref, k_ref, v_ref, do_ref, tot_ref, stop_ref, incl_ref, excl_ref, dq_ref, dk_ref, dv_ref,
             dk_acc, dv_acc, dq_acc, p_scr, e_scr, dob_scr):
        g = pl.program_id(2)

        @pl.when(g == 0)
        def _():
            dk_acc[...] = jnp.zeros_like(dk_acc)
            dv_acc[...] = jnp.zeros_like(dv_acc)

        incl, excl = incl_ref[...], excl_ref[...]
        causal = _iota2((BLOCK, BLOCK), 1) < _iota2((BLOCK, BLOCK), 0)
        lane = _iota2((1, BLOCK), 1)
        dob_scr[...] = do_ref[...].astype(BF16)
        dq_acc[...] = jnp.zeros_like(dq_acc)
        p_scr[...] = jnp.zeros_like(p_scr)
        e_scr[...] = jnp.zeros_like(e_scr)

        def tiles(jobs):
            n_jobs = range(len(jobs))
            cols = [pl.ds(hh * HEAD, HEAD) for hh, _, _, _, _ in jobs]
            qrows = [pl.ds(r * BLOCK, BLOCK) for _, r, _, _, _ in jobs]
            krows = [pl.ds(pl.multiple_of(m * BLOCK, BLOCK), BLOCK) for _, _, m, _, _ in jobs]
            diag = [dg for _, _, _, dg, _ in jobs]
            lg = [_sb_logits(q_ref[qrows[i], cols[i]], k_ref[krows[i], cols[i]], scale,
                             causal if diag[i] else None, jobs[i][4]) for i in n_jobs]
            d_a = [_dot_nt(dob_scr[qrows[i], cols[i]], v_ref[krows[i], cols[i]]) for i in n_jobs]
            sm = [_sums(lg[i][2], incl) for i in n_jobs]
            a_all = []
            for i in n_jobs:
                p = p_scr[qrows[i], cols[i]]
                a = jnp.exp(lg[i][1] + (tot_ref[qrows[i], cols[i]] - p - sm[i][0]))
                a_all.append(jnp.where(causal, a, 0.0) if diag[i] else a)
                p_scr[qrows[i], cols[i]] = p + sm[i][1]
            gr = [d_a[i] * a_all[i] for i in n_jobs]
            dv_part = [_dot_tn(a_all[i].astype(BF16), dob_scr[qrows[i], cols[i]]) for i in n_jobs]
            gs = [_sums(gr[i], excl, pieces=1) for i in n_jobs]
            dz_all = []
            for i in n_jobs:
                e = e_scr[qrows[i], cols[i]]
                dz = gr[i] - (gr[i] + e + gs[i][0]) * jnp.exp(lg[i][1])
                if diag[i]:
                    dz = jnp.where(causal, dz, 0.0)
                dz_all.append((dz * scale).astype(BF16))
                e_scr[qrows[i], cols[i]] = e + gs[i][1]
            dk_part = [_dot_tn(dz_all[i], q_ref[qrows[i], cols[i]]) for i in n_jobs]
            dq_part = [_dot(dz_all[i], k_ref[krows[i], cols[i]]) for i in n_jobs]
            for i in n_jobs:
                dv_acc[krows[i], cols[i]] += dv_part[i]
                dk_acc[krows[i], cols[i]] += dk_part[i]
                dq_acc[qrows[i], cols[i]] += dq_part[i]

        def pad_row_of(m):
            return jnp.where(jnp.logical_or(m > 0, lane >= N_PAD), 1.0, 0.0)

        base = SB_GROUP * g
        heads = range(SB_HEADS)
        mine = [(hh, r) for hh in heads for r in range(SB_GROUP)]
        stops = [jnp.max(stop_ref[pl.ds(SB_GROUP * hh + r, 1), :]).astype(jnp.int32) for hh, r in mine]
        for (hh, r), stop_now in zip(mine, stops):
            lowest = jnp.maximum(base - (SB_GROUP - 1 - r), 0)

            def step(m, hh=hh, r=r):
                tiles([(hh, r, m, False, pad_row_of(m))])
                return m + 1

            lax.while_loop(lambda m, lowest=lowest: m < lowest, step, jnp.clip(stop_now, 0, lowest))
        own = [(hh, r, base + kb, r == kb, pad_row_of(base) if kb == 0 else None)
               for hh in heads for kb in range(SB_GROUP) for r in range(kb, SB_GROUP)]
        near = [(hh, r, base - 1 - back, False, None) for hh in heads for r, back in reversed(SB_NEAR)]
        pl.when(g == 0)(lambda: tiles(own))
        pl.when(g > 0)(lambda: tiles(near + own))
        dq_ref[...] = dq_acc[...].astype(BF16)

        @pl.when(g == n_groups - 1)
        def _():
            dk_ref[...] = dk_acc[...].astype(BF16)
            dv_ref[...] = dv_acc[...].astype(BF16)

    n_pairs = n_heads // SB_HEADS
    wide = SB_HEADS * HEAD
    qblk = pl.BlockSpec((SB_ROWS, wide), lambda b, h, g: (b * n_groups + g, h))
    kblk = pl.BlockSpec((lp, wide), lambda b, h, g: (b, n_pairs + h))
    vblk = pl.BlockSpec((lp, wide), lambda b, h, g: (b, 2 * n_pairs + h))
    hblk = pl.BlockSpec((lp, wide), lambda b, h, g: (b, h))
    sblk = pl.BlockSpec((None, 8, 128), lambda b, h, g: ((b * n_pairs + h) * n_groups + g, 0, 0))
    table = pl.BlockSpec((2 * BLOCK, 2 * BLOCK), lambda b, h, g: (0, 0))
    _, incl, excl = _sb_tables()
    return _pcall(
        body, name=name, grid=(bsz, n_pairs, n_groups),
        in_specs=[qblk, kblk, vblk, qblk, qblk, sblk, table, table],
        out_specs=(qblk, hblk, hblk),
        out_shape=(jax.ShapeDtypeStruct((t, d), BF16),) * 3,
        scratch_shapes=[pltpu.VMEM((lp, wide), F32), pltpu.VMEM((lp, wide), F32)]
        + [pltpu.VMEM((SB_ROWS, wide), F32)] * 3 + [pltpu.VMEM((SB_ROWS, wide), BF16)],
        compiler_params=_params("parallel", "parallel", "arbitrary"),
    )(qkv, qkv, qkv, do, tot, stop, incl, excl)


def _adamw(w, g, m, v):
    m = ADAM_B1 * m + (1.0 - ADAM_B1) * g
    v = ADAM_B2 * v + (1.0 - ADAM_B2) * (g * g)
    m_hat = m / (1.0 - ADAM_B1 ** ADAM_STEP)
    v_hat = v / (1.0 - ADAM_B2 ** ADAM_STEP)
    delta = -ADAM_LR * (m_hat / (jnp.sqrt(v_hat) + ADAM_EPS) + ADAM_WD * w)
    return delta, m, v


def _update_sharded(w, parts, m, v, name):
    r, c = w.shape
    tr = UPDATE_ROWS if r % UPDATE_ROWS == 0 else r

    def body(w_ref, p_ref, m_ref, v_ref, g_ref, d_ref, nm_ref, nv_ref):
        g = p_ref[0].astype(F32)
        for q in range(1, N_DEV):
            g = g + p_ref[q].astype(F32)
        g_ref[...] = g
        d_ref[...], nm_ref[...], nv_ref[...] = _adamw(w_ref[...], g, m_ref[...], v_ref[...])

    row = pl.BlockSpec((tr, c), lambda i: (i, 0))
    return _pcall(
        body, name=name, grid=(r // tr,),
        in_specs=[row, pl.BlockSpec((N_DEV, tr, c), lambda i: (0, i, 0)), row, row],
        out_specs=(row,) * 4, out_shape=(jax.ShapeDtypeStruct((r, c), F32),) * 4,
        compiler_params=_params("parallel"),
    )(w, parts, m, v)


SMALL_ROWS = 8


def _pack_small(dpre0, dpre1, dpost0, dpost1, dlb, dwon, loss, name):
    d = dpre0.shape[1]
    bsz = dlb.shape[0]

    def body(a0, a1, p0, p1, lb_ref, on_ref, loss_ref, out_ref):
        out_ref[...] = jnp.zeros_like(out_ref)
        out_ref[pl.ds(0, 1), :] = a0[...]
        out_ref[pl.ds(1, 1), :] = a1[...]
        out_ref[pl.ds(2, 1), :] = p0[...]
        out_ref[pl.ds(3, 1), :] = p1[...]
        acc = lb_ref[0]
        for b in range(1, bsz):
            acc = acc + lb_ref[b]
        out_ref[pl.ds(4, 1), :] = acc
        out_ref[pl.ds(5, 1), pl.ds(0, HEAD)] = on_ref[...]
        out_ref[pl.ds(6, 1), pl.ds(0, HEAD)] = loss_ref[pl.ds(0, 1), :]

    return _pcall(body, name=name, out_shape=jax.ShapeDtypeStruct((SMALL_ROWS, d), F32))(
        dpre0, dpre1, dpost0, dpost1, dlb, dwon, loss)


def _update_small(parts, pre, post, lbw, on, moments, name):
    d = pre.shape[1]

    def body(p_ref, pre_ref, post_ref, lbw_ref, on_ref, mpre, mpost, mlb, mon, vpre, vpost, vlb, von,
             loss_ref, *outs):
        def total(r0, nr, width):
            acc = p_ref[0, pl.ds(r0, nr), pl.ds(0, width)]
            for q in range(1, N_DEV):
                acc = acc + p_ref[q, pl.ds(r0, nr), pl.ds(0, width)]
            return acc

        def put(k, w, g, m, v):
            dl, nm, nv = _adamw(w, g, m, v)
            outs[4 * k][...] = g
            outs[4 * k + 1][...] = dl
            outs[4 * k + 2][...] = nm
            outs[4 * k + 3][...] = nv

        put(0, pre_ref[...], total(0, 2, d), mpre[...], vpre[...])
        put(1, post_ref[...], total(2, 2, d), mpost[...], vpost[...])
        a0, a1 = lbw_ref[pl.ds(0, 1), :], lbw_ref[pl.ds(1, 1), :]
        mx = jnp.maximum(a0, a1)
        e0, e1 = jnp.exp(a0 - mx), jnp.exp(a1 - mx)
        p0 = e0 / (e0 + e1)
        g0 = total(4, 1, d) * p0 * (1.0 - p0)
        for r, w, g in ((0, a0, g0), (1, a1, -g0)):
            row = pl.ds(r, 1)
            dl, nm, nv = _adamw(w, g, mlb[row, :], vlb[row, :])
            outs[8][row, :] = g
            outs[9][row, :] = dl
            outs[10][row, :] = nm
            outs[11][row, :] = nv
        put(3, on_ref[...], total(5, 1, HEAD), mon[...], von[...])
        loss_ref[...] = jnp.broadcast_to(total(6, 1, HEAD), loss_ref.shape)

    shapes = []
    for w in (pre, post, lbw, on):
        shapes += [jax.ShapeDtypeStruct(w.shape, F32)] * 4
    return _pcall(body, name=name, out_shape=(jax.ShapeDtypeStruct((8, 128), F32), *shapes))(
        parts, pre, post, lbw, on, *moments)


def kernel(x, meta_tokens, pre_norm, post_norm, hgrn_w_in, hgrn_lb, hgrn_out_norm, hgrn_w_out, sb_w_in, sb_w_out, loss_target, m_meta_tokens, m_pre_norm, m_post_norm, m_hgrn_w_in, m_hgrn_lb, m_hgrn_out_norm, m_hgrn_w_out, m_sb_w_in, m_sb_w_out, v_meta_tokens, v_pre_norm, v_post_norm, v_hgrn_w_in, v_hgrn_lb, v_hgrn_out_norm, v_hgrn_w_out, v_sb_w_in, v_sb_w_out):
    bsz, seq, d = x.shape
    n_blocks = seq // BLOCK + 1
    lp = n_blocks * BLOCK
    s = hgrn_w_in.shape[2]
    dsh = d // N_DEV

    w_in_h, meta_all = _gather_once_per_chip([hgrn_w_in[0].astype(BF16), meta_tokens], "gather_weights")
    meta_full = jnp.transpose(meta_all, (1, 0, 2)).reshape(N_META, d)

    lbrow = jnp.cumsum(jax.nn.softmax(hgrn_lb, axis=0), axis=0)[0:1]

    main0, gate0, yn0, h0 = _norm_inproj(None, pre_norm[0:1], w_in_h, F32, "inproj_hgrn", from_x=(x, meta_full))
    o0, states, decay0, k0, w_in_s, w_out_s, w_out_h = _hgrn_fwd(
        main0, lbrow, bsz, n_blocks,
        [sb_w_in[0].astype(BF16), sb_w_out[0].astype(BF16), hgrn_w_out[0].astype(BF16)], ["gather"] * 3, "hgrn_fwd")
    w_out_s = w_out_s.reshape(d, d)
    w_out_h = w_out_h.reshape(d, d)
    h1, y0 = _mix_out(o0, gate0, h0, w_out_h, hgrn_out_norm, post_norm[0:1], True, "mix_out_hgrn")
    main1, gate1, yn1 = _norm_inproj(h1, pre_norm[1:2], w_in_s, BF16, "inproj_sb")
    o1, tot, stop = _sb_fwd(main1, bsz, n_blocks, "sb_fwd")
    dh2, y1, loss_part = _mix_out(o1, gate1, h1, w_out_s, hgrn_out_norm, post_norm[1:2], False, "mix_out_sb",
                                  target=loss_target)

    do1, dgate1, dw_out_s, dpost1, _ = _mix_out_bwd(
        dh2, y1, o1, gate1, w_out_s, hgrn_out_norm, post_norm[1:2], False, "mix_out_sb_bwd")
    dq1, dk1, dv1 = _sb_bwd(main1, do1, tot, stop, bsz, n_blocks, "sb_bwd")
    dproj1 = (dq1, dk1, dv1, dgate1)
    dh1, dpre1 = _inproj_bwd_x(dproj1, w_in_s, h1, pre_norm[1:2], dh2, [], [], "inproj_sb_bwd_x")
    dw_in_s = _inproj_bwd_w(yn1, dproj1, s, "inproj_sb_bwd_w")

    do0, dgate0, dw_out_h, dpost0, dwon = _mix_out_bwd(
        dh1, y0, o0, gate0, w_out_h, hgrn_out_norm, post_norm[0:1], True, "mix_out_hgrn_bwd")
    dq0, dfz0, dv0, dlb, p_in_s, p_out_s, p_out_h = _hgrn_bwd(
        main0, decay0, k0, lbrow, states, do0, bsz, n_blocks,
        [dw_in_s, dw_out_s.reshape(N_DEV, dsh, d), dw_out_h.reshape(N_DEV, dsh, d)], ["scatter"] * 3, "hgrn_bwd")
    dproj0 = (dq0, dfz0, dv0, dgate0)
    dw_in_h = _inproj_bwd_w(yn0, dproj0, s, "inproj_hgrn_bwd_w")
    dh0, dpre0, p_in_h = _inproj_bwd_x(
        dproj0, w_in_h, h0, pre_norm[0:1], dh1, [dw_in_h], ["scatter"], "inproj_hgrn_bwd_x")

    dh0 = dh0.reshape(bsz, lp, d)
    grad_x = dh0[:, BLOCK:]
    dmeta = jnp.sum(dh0[:, N_PAD:BLOCK], axis=0)
    dmeta = jnp.transpose(dmeta.reshape(N_META, N_DEV, dsh), (1, 0, 2))
    small = _pack_small(dpre0, dpre1, dpost0, dpost1, dlb, dwon, loss_part, "pack_small")

    p_meta, p_small = _exchange([dmeta, small], ["scatter", "gather"], "exchange_grads")

    u_meta = _update_sharded(meta_tokens, p_meta, m_meta_tokens, v_meta_tokens, "update_meta")
    u_in_h = _update_sharded(hgrn_w_in[0], p_in_h, m_hgrn_w_in[0], v_hgrn_w_in[0], "update_hgrn_w_in")
    u_out_h = _update_sharded(hgrn_w_out[0], p_out_h, m_hgrn_w_out[0], v_hgrn_w_out[0], "update_hgrn_w_out")
    u_in_s = _update_sharded(sb_w_in[0], p_in_s, m_sb_w_in[0], v_sb_w_in[0], "update_sb_w_in")
    u_out_s = _update_sharded(sb_w_out[0], p_out_s, m_sb_w_out[0], v_sb_w_out[0], "update_sb_w_out")
    sm = _update_small(p_small, pre_norm, post_norm, hgrn_lb, hgrn_out_norm,
                       (m_pre_norm, m_post_norm, m_hgrn_lb, m_hgrn_out_norm,
                        v_pre_norm, v_post_norm, v_hgrn_lb, v_hgrn_out_norm), "update_small")
    loss = sm[0][0, 0]
    u_pre, u_post, u_lb, u_on = sm[1:5], sm[5:9], sm[9:13], sm[13:17]

    per_w = [u_meta, u_pre, u_post, tuple(a[None] for a in u_in_h), u_lb, u_on,
             tuple(a[None] for a in u_out_h), tuple(a[None] for a in u_in_s), tuple(a[None] for a in u_out_s)]
    outs = [loss, grad_x]
    for k in range(4):
        outs += [u[k] for u in per_w]
    return tuple(outs)
```

```python
import jax
import jax.numpy as jnp
from jax import lax
from jax.experimental import pallas as pl
from jax.experimental.pallas import tpu as pltpu

F32 = jnp.float32
BF16 = jnp.bfloat16

N_DEV = 8
BLOCK = 128
N_META = 16
N_PAD = BLOCK - N_META
HEAD = 128
SUB = 16
N_SUB = BLOCK // SUB
HALF = 8
EPS = 1e-6
ROW_TILE = 3 * BLOCK
UPDATE_ROWS = 256
K_TILES = (2816, 1408, 768, 384, 128)
NEG_BIG = -1e30

ADAM_LR = 0.001
ADAM_B1 = 0.9
ADAM_B2 = 0.999
ADAM_EPS = 1e-08
ADAM_WD = 0.01
ADAM_STEP = 10

VMEM_LIMIT = 56 * 1024 * 1024


def _pcall(body, **kw):
    return pl.pallas_call(body, **kw)


def _params(*sem):
    return pltpu.CompilerParams(dimension_semantics=sem, vmem_limit_bytes=VMEM_LIMIT)


def _dot(a, b):
    return jnp.dot(a, b, preferred_element_type=F32)


def _dot_nt(a, b):
    return lax.dot_general(a, b, (((1,), (1,)), ((), ())), preferred_element_type=F32)


def _dot_tn(a, b):
    return lax.dot_general(a, b, (((0,), (0,)), ((), ())), preferred_element_type=F32)


def _split(x, pieces):
    out = []
    for _ in range(pieces):
        p = x.astype(BF16)
        out.append(p)
        x = x - p.astype(F32)
    return out


def _tri_left(tri, x, pieces=3):
    return sum(_dot(tri, p) for p in _split(x, pieces))


def _iota2(shape, dim):
    return lax.broadcasted_iota(jnp.int32, shape, dim)


def _tri(cond):
    return jnp.where(cond, 1.0, 0.0).astype(BF16)


def _sig_pair(x):
    e = jnp.exp(-jnp.abs(x))
    r = 1.0 / (1.0 + e)
    er = e * r
    pos = x >= 0
    return jnp.where(pos, r, er), jnp.where(pos, er, r)


def _expand(rows):
    return jnp.concatenate([jnp.broadcast_to(r, (SUB, HEAD)) for r in rows], axis=0)


def _exchange_shapes(arrays, modes):
    return tuple(jax.ShapeDtypeStruct((N_DEV,) + tuple(a.shape[1:] if m == "scatter" else a.shape), a.dtype)
                 for a, m in zip(arrays, modes))


def _exchange_sems(n):
    if n == 0:
        return []
    return [pltpu.SemaphoreType.DMA((n, N_DEV - 1)), pltpu.SemaphoreType.DMA((n, N_DEV - 1)),
            pltpu.SemaphoreType.DMA((n,))]


def _exchange_copies(ins, outs, modes, send_sems, recv_sems, local_sems):
    mx, my, mc = lax.axis_index("x"), lax.axis_index("y"), lax.axis_index("c")
    me = 4 * mx + 2 * my + mc

    def src(i, slot):
        return ins[i].at[slot] if modes[i] == "scatter" else ins[i]

    def peer_of(mask):
        px = 1 - mx if mask & 4 else mx
        py = 1 - my if mask & 2 else my
        pc = 1 - mc if mask & 1 else mc
        return px, py, pc

    def copy(i, mask, dst_slot):
        px, py, pc = peer_of(mask)
        return pltpu.make_async_remote_copy(
            src_ref=src(i, 4 * px + 2 * py + pc), dst_ref=outs[i].at[dst_slot],
            send_sem=send_sems.at[i, mask - 1], recv_sem=recv_sems.at[i, mask - 1],
            device_id=(px, py, pc), device_id_type=pl.DeviceIdType.MESH)

    n = len(ins)
    sends = [copy(i, mask, me) for mask in range(1, N_DEV) for i in range(n)]
    own = [pltpu.make_async_copy(src(i, me), outs[i].at[me], local_sems.at[i]) for i in range(n)]
    arrivals = []
    for mask in range(1, N_DEV):
        px, py, pc = peer_of(mask)
        arrivals += [copy(i, mask, 4 * px + 2 * py + pc) for i in range(n)]

    def start():
        for cp in sends + own:
            cp.start()

    def wait():
        for cp in arrivals:
            cp.wait_recv()
        for cp in sends:
            cp.wait_send()
        for cp in own:
            cp.wait()

    return start, wait


def _gather_once_per_chip(arrays, name):
    n = len(arrays)

    def body(*refs):
        ins, outs = refs[:n], refs[n:2 * n]
        send_sems, recv_sems, local_sems = refs[2 * n:]
        mx, my, mc = lax.axis_index("x"), lax.axis_index("y"), lax.axis_index("c")
        me, sibling = (mx, my, mc), (mx, my, 1 - mc)
        chips = [(1 - mx, my), (mx, 1 - my), (1 - mx, 1 - my)]

        def slot(px, py, pc):
            return 4 * px + 2 * py + pc

        def copy(i, k, block, to, src=None):
            return pltpu.make_async_remote_copy(
                src_ref=outs[i].at[slot(*block)] if src is None else src, dst_ref=outs[i].at[slot(*block)],
                send_sem=send_sems.at[i, k], recv_sem=recv_sems.at[i, k],
                device_id=to, device_id_type=pl.DeviceIdType.MESH)

        own = [pltpu.make_async_copy(ins[i], outs[i].at[slot(*me)], local_sems.at[i]) for i in range(n)]
        first = [copy(i, 0, me, sibling, src=ins[i]) for i in range(n)]
        first += [copy(i, 1 + j, me, (*chip, mc), src=ins[i]) for j, chip in enumerate(chips) for i in range(n)]
        for cp in own + first:
            cp.start()
        passed = []
        for j, chip in enumerate(chips):
            for i in range(n):
                copy(i, 1 + j, (*chip, mc), me).wait_recv()
                cp = copy(i, 4 + j, (*chip, mc), sibling)
                cp.start()
                passed.append(cp)
        for i in range(n):
            copy(i, 0, sibling, me).wait_recv()
            for j, chip in enumerate(chips):
                copy(i, 4 + j, (*chip, 1 - mc), me).wait_recv()
        for cp in first + passed:
            cp.wait_send()
        for cp in own:
            cp.wait()

    any_spec = pl.BlockSpec(memory_space=pl.ANY)
    return _pcall(
        body, name=name, out_shape=_exchange_shapes(arrays, ["gather"] * n),
        in_specs=[any_spec] * n, out_specs=tuple([any_spec] * n),
        scratch_shapes=_exchange_sems(n),
    )(*arrays)


def _exchange(arrays, modes, name):
    n = len(arrays)

    def body(*refs):
        start, wait = _exchange_copies(refs[:n], refs[n:2 * n], modes, *refs[2 * n:])
        start()
        wait()

    any_spec = pl.BlockSpec(memory_space=pl.ANY)
    return _pcall(
        body, name=name, out_shape=_exchange_shapes(arrays, modes),
        in_specs=[any_spec] * n, out_specs=tuple([any_spec] * n),
        scratch_shapes=_exchange_sems(n),
    )(*arrays)


def _tile_blocks_of_x(seq, d):
    assert ROW_TILE == 3 * BLOCK and (seq + BLOCK) % ROW_TILE == 0
    per_seq = (seq + BLOCK) // ROW_TILE

    def spec(k):
        return pl.BlockSpec((None, BLOCK, d),
                            lambda i: (i // per_seq, jnp.maximum(3 * (i % per_seq) - 1 + k, 0), 0))

    return [spec(0), spec(1), spec(2)]


def _norm_inproj(h, wnorm, w_all, main_dtype, name, from_x=None):
    p_n, d, s = w_all.shape
    n_main = 3 * d // s
    tm = ROW_TILE
    if from_x is None:
        t = h.shape[0]
        lead, lead_specs = [h], [pl.BlockSpec((tm, d), lambda i: (i, 0))]
    else:
        x_in, meta = from_x
        t = x_in.shape[0] * (x_in.shape[1] + BLOCK)
        tiles_per_seq = (x_in.shape[1] + BLOCK) // tm
        lead = [x_in, x_in, x_in, meta]
        lead_specs = _tile_blocks_of_x(x_in.shape[1], d) + [pl.BlockSpec((N_META, d), lambda i: (0, 0))]
    n_lead = len(lead)

    def body(*refs):
        wn_ref, w_ref = refs[n_lead:n_lead + 2]
        main_ref, gate_ref, ynt_ref = refs[n_lead + 2:n_lead + 5]
        if from_x is None:
            x = refs[0][...]
        else:
            first_tile = pl.program_id(0) % tiles_per_seq == 0
            meta_block = jnp.concatenate([jnp.zeros((N_PAD, d), F32), refs[3][...]], axis=0)
            x = jnp.concatenate([jnp.where(first_tile, meta_block, refs[0][...]), refs[1][...], refs[2][...]], axis=0)
            refs[n_lead + 5][...] = x
        y = x * lax.rsqrt(jnp.mean(x * x, axis=-1, keepdims=True) + EPS) * wn_ref[...]
        yb = y.astype(BF16)
        ynt_ref[...] = y.T.astype(BF16)
        for p in range(p_n):
            r = _dot(yb, w_ref[p])
            if p < n_main:
                main_ref[:, p * s:(p + 1) * s] = r.astype(main_dtype)
            else:
                gate_ref[:, (p - n_main) * s:(p - n_main + 1) * s] = r

    row = pl.BlockSpec((tm, d), lambda i: (i, 0))
    extra_specs, extra_shapes = ((), ()) if from_x is None else ((row,), (jax.ShapeDtypeStruct((t, d), F32),))
    return _pcall(
        body, name=name, grid=(t // tm,),
        in_specs=lead_specs + [pl.BlockSpec((1, d), lambda i: (0, 0)), pl.BlockSpec((p_n, d, s), lambda i: (0, 0, 0))],
        out_specs=(pl.BlockSpec((tm, 3 * d), lambda i: (i, 0)), row, pl.BlockSpec((d, tm), lambda i: (0, i)))
        + extra_specs,
        out_shape=(jax.ShapeDtypeStruct((t, 3 * d), main_dtype), jax.ShapeDtypeStruct((t, d), F32),
                   jax.ShapeDtypeStruct((d, t), BF16)) + extra_shapes,
        compiler_params=_params("parallel"),
    )(*lead, wnorm, w_all)


def _mix_out(o, gate, h_in, w_out, w_on, w_post, head_norm, name, target=None):
    t, d = o.shape
    n_heads = d // HEAD
    tm = ROW_TILE
    with_loss = target is not None
    if with_loss:
        tiles_per_seq = (target.shape[1] + BLOCK) // tm

    def body(o_ref, g_ref, h_ref, w_ref, won_ref, wp_ref, *rest):
        if with_loss:
            t0_ref, t1_ref, t2_ref, hout_ref, y_ref, loss_ref, u_scr = rest
        else:
            hout_ref, y_ref, u_scr = rest
        for hh in range(n_heads):
            cs = slice(hh * HEAD, (hh + 1) * HEAD)
            oh = o_ref[:, cs]
            gt = g_ref[:, cs]
            if head_norm:
                oh = oh * lax.rsqrt(jnp.mean(oh * oh, axis=-1, keepdims=True) + EPS) * won_ref[...]
            u_scr[:, cs] = (oh * (gt * jax.nn.sigmoid(gt))).astype(BF16)
        y = _dot(u_scr[...], w_ref[...])
        y_ref[...] = y
        r = y * lax.rsqrt(jnp.mean(y * y, axis=-1, keepdims=True) + EPS) * wp_ref[...]
        h_out = h_ref[...] + r
        if not with_loss:
            hout_ref[...] = h_out
            return
        i = pl.program_id(0)

        @pl.when(i == 0)
        def _():
            loss_ref[...] = jnp.zeros_like(loss_ref)

        tgt = jnp.concatenate([t0_ref[...], t1_ref[...], t2_ref[...]], axis=0)
        real = jnp.logical_or(i % tiles_per_seq > 0, _iota2((tm, d), 0) >= BLOCK)
        err = jnp.where(real, h_out - tgt, 0.0)
        hout_ref[...] = err * (1.0 / d)
        part = jnp.sum(jnp.sum(err * err, axis=-1, keepdims=True), axis=0, keepdims=True)
        loss_ref[...] += part * (0.5 / d)

    row = pl.BlockSpec((tm, d), lambda i: (i, 0))
    in_specs = [row, row, row, pl.BlockSpec((d, d), lambda i: (0, 0)),
                pl.BlockSpec((1, HEAD), lambda i: (0, 0)), pl.BlockSpec((1, d), lambda i: (0, 0))]
    out_specs, out_shape = (row, row), (jax.ShapeDtypeStruct((t, d), F32), jax.ShapeDtypeStruct((t, d), F32))
    args = (o, gate, h_in, w_out, w_on, w_post)
    if with_loss:
        in_specs += _tile_blocks_of_x(target.shape[1], d)
        out_specs += (pl.BlockSpec((8, 128), lambda i: (0, 0)),)
        out_shape += (jax.ShapeDtypeStruct((8, 128), F32),)
        args += (target, target, target)
    return _pcall(
        body, name=name, grid=(t // tm,), in_specs=in_specs, out_specs=out_specs, out_shape=out_shape,
        scratch_shapes=[pltpu.VMEM((tm, d), BF16)],
        compiler_params=_params("arbitrary" if with_loss else "parallel"),
    )(*args)


def _mix_out_bwd(dh, y, o, gate, w_out, w_on, w_post, head_norm, name):
    t, d = o.shape
    n_heads = d // HEAD
    tm = ROW_TILE
    last = t // tm - 1

    def body(dh_ref, y_ref, o_ref, g_ref, w_ref, won_ref, wp_ref,
             do_ref, dg_ref, dw_ref, dwp_ref, dwon_ref, u_scr, acc):
        i = pl.program_id(0)

        @pl.when(i == 0)
        def _():
            acc[...] = jnp.zeros_like(acc)
            dwp_ref[...] = jnp.zeros_like(dwp_ref)
            dwon_ref[...] = jnp.zeros_like(dwon_ref)

        yv = y_ref[...]
        rs = lax.rsqrt(jnp.mean(yv * yv, axis=-1, keepdims=True) + EPS)
        yh = yv * rs
        dr = dh_ref[...]
        dwp_ref[...] += jnp.sum(dr * yh, axis=0, keepdims=True)
        wd = dr * wp_ref[...]
        dy = rs * (wd - yh * jnp.mean(wd * yh, axis=-1, keepdims=True))
        dyb = dy.astype(BF16)
        du = _dot_nt(dyb, w_ref[...])
        for hh in range(n_heads):
            cs = slice(hh * HEAD, (hh + 1) * HEAD)
            oh = o_ref[:, cs]
            gt = g_ref[:, cs]
            sg = jax.nn.sigmoid(gt)
            sl = gt * sg
            duh = du[:, cs]
            if head_norm:
                rsh = lax.rsqrt(jnp.mean(oh * oh, axis=-1, keepdims=True) + EPS)
                ohat = oh * rsh
                on = ohat * won_ref[...]
            else:
                on = oh
            u_scr[:, cs] = (on * sl).astype(BF16)
            don = duh * sl
            dg_ref[:, cs] = (duh * on * (sg * (1.0 + gt * (1.0 - sg)))).astype(BF16)
            if head_norm:
                dwon_ref[...] += jnp.sum(don * ohat, axis=0, keepdims=True)
                wdn = don * won_ref[...]
                do_ref[:, cs] = rsh * (wdn - ohat * jnp.mean(wdn * ohat, axis=-1, keepdims=True))
            else:
                do_ref[:, cs] = don
        acc[...] += _dot_tn(u_scr[...], dyb)

        @pl.when(i == last)
        def _():
            dw_ref[...] = acc[...].astype(BF16)

    row = pl.BlockSpec((tm, d), lambda i: (i, 0))
    return _pcall(
        body, name=name, grid=(t // tm,),
        in_specs=[row, row, row, row, pl.BlockSpec((d, d), lambda i: (0, 0)),
                  pl.BlockSpec((1, HEAD), lambda i: (0, 0)), pl.BlockSpec((1, d), lambda i: (0, 0))],
        out_specs=(row, row, pl.BlockSpec((d, d), lambda i: (0, 0)), pl.BlockSpec((1, d), lambda i: (0, 0)),
                   pl.BlockSpec((1, HEAD), lambda i: (0, 0))),
        out_shape=(jax.ShapeDtypeStruct((t, d), F32), jax.ShapeDtypeStruct((t, d), BF16),
                   jax.ShapeDtypeStruct((d, d), BF16), jax.ShapeDtypeStruct((1, d), F32),
                   jax.ShapeDtypeStruct((1, HEAD), F32)),
        scratch_shapes=[pltpu.VMEM((tm, d), BF16), pltpu.VMEM((d, d), F32)],
        compiler_params=_params("arbitrary"),
    )(dh, y, o, gate, w_out, w_on, w_post)


def _inproj_bwd_x(dparts, w_all, h, wnorm, dres, ride, ride_modes, name):
    t, d = h.shape
    p_n, _, s = w_all.shape
    per = d // s
    tm = ROW_TILE
    nr = len(ride)
    grid = (t // tm,)

    def body(*refs):
        d0_ref, d1_ref, d2_ref, d3_ref, w_ref, h_ref, wn_ref, dres_ref = refs[:8]
        dh_ref, dwn_ref = refs[8 + nr:10 + nr]
        finish_ride = _ride_along(refs[8:8 + nr], refs[10 + nr:10 + 2 * nr], ride_modes, refs[10 + 2 * nr:], grid)
        i = pl.program_id(0)

        @pl.when(i == 0)
        def _():
            dwn_ref[...] = jnp.zeros_like(dwn_ref)

        pieces = (d0_ref, d1_ref, d2_ref, d3_ref)
        dyn = jnp.zeros((tm, d), F32)
        for p in range(p_n):
            blk = pieces[p // per][:, (p % per) * s:(p % per + 1) * s]
            dyn = dyn + _dot_nt(blk, w_ref[p])
        x = h_ref[...]
        rs = lax.rsqrt(jnp.mean(x * x, axis=-1, keepdims=True) + EPS)
        xh = x * rs
        dwn_ref[...] += jnp.sum(dyn * xh, axis=0, keepdims=True)
        wd = dyn * wn_ref[...]
        dh_ref[...] = dres_ref[...] + rs * (wd - xh * jnp.mean(wd * xh, axis=-1, keepdims=True))
        finish_ride()

    row = pl.BlockSpec((tm, d), lambda i: (i, 0))
    any_spec = pl.BlockSpec(memory_space=pl.ANY)
    return _pcall(
        body, name=name, grid=grid,
        in_specs=[row, row, row, row, pl.BlockSpec((p_n, d, s), lambda i: (0, 0, 0)),
                  row, pl.BlockSpec((1, d), lambda i: (0, 0)), row] + [any_spec] * nr,
        out_specs=(row, pl.BlockSpec((1, d), lambda i: (0, 0))) + (any_spec,) * nr,
        out_shape=(jax.ShapeDtypeStruct((t, d), F32), jax.ShapeDtypeStruct((1, d), F32))
        + _exchange_shapes(ride, ride_modes),
        scratch_shapes=_exchange_sems(nr),
        compiler_params=_params("arbitrary"),
    )(*dparts, w_all, h, wnorm, dres, *ride)


def _inproj_bwd_w(ynt, dparts, s, name):
    d, t = ynt.shape
    per = d // s
    n_sh = len(dparts) * per
    tk = next(c for c in K_TILES if t % c == 0)
    last = t // tk - 1

    def body(ynt_ref, d0_ref, d1_ref, d2_ref, d3_ref, dw_ref, acc):
        p, i = pl.program_id(0), pl.program_id(1)

        @pl.when(i == 0)
        def _():
            acc[...] = jnp.zeros_like(acc)

        for a, piece in enumerate((d0_ref, d1_ref, d2_ref, d3_ref)):
            @pl.when(p // per == a)
            def _():
                acc[...] += _dot(ynt_ref[...], piece[...])

        @pl.when(i == last)
        def _():
            dw_ref[...] = acc[...].astype(BF16)

    def piece_spec(a):
        return pl.BlockSpec((tk, s), lambda p, i: (jnp.where(p // per == a, i, 0),
                                                   jnp.where(p // per == a, p % per, 0)))

    return _pcall(
        body, name=name, grid=(n_sh, t // tk),
        in_specs=[pl.BlockSpec((d, tk), lambda p, i: (0, i))] + [piece_spec(a) for a in range(4)],
        out_specs=pl.BlockSpec((None, d, s), lambda p, i: (p, 0, 0)),
        out_shape=jax.ShapeDtypeStruct((n_sh, d, s), BF16),
        scratch_shapes=[pltpu.VMEM((d, s), F32)],
        compiler_params=_params("parallel", "arbitrary"),
    )(ynt, *dparts)


def _hgrn_gates(fz_ref, lb, b_ref, k_ref):
    sig, nsig = _sig_pair(fz_ref[...])
    f = lb + (1.0 - lb) * sig
    rr, cc = _iota2((BLOCK, BLOCK), 0), _iota2((BLOCK, BLOCK), 1)
    b_ref[...] = _tri_left(_tri(cc <= rr), jnp.log(f))
    k_ref[...] = (1.0 - lb) * nsig


def _hgrn_common(q_ref, b_ref, k_ref):
    b, k = b_ref[...], k_ref[...]
    bend = [b_ref[pl.ds(SUB * j + SUB - 1, 1), :] for j in range(N_SUB)]
    bref = [jnp.zeros((1, HEAD), F32)] + bend[:-1]
    refrow, bendrow = _expand(bref), _expand(bend)
    e_q = jnp.exp(b - refrow)
    e_k = jnp.exp(bendrow - b)
    qt = q_ref[...] * e_q
    kh = k * e_k
    bl = bend[-1]
    return dict(k=k, b=b, bend=bend, bref=bref, refrow=refrow, e_q=e_q, e_k=e_k, qt=qt, kh=kh, bl=bl)


HEADS_PER_STEP = 8


def _lockstep(chunks):
    live = list(chunks)
    while live:
        still = []
        for gen in live:
            try:
                next(gen)
                still.append(gen)
            except StopIteration:
                pass
        live = still


def _ride_along(ride_ins, ride_outs, modes, sems, grid):
    if not ride_ins:
        return lambda: None
    ids = [pl.program_id(a) for a in range(len(grid))]
    start, wait = _exchange_copies(ride_ins, ride_outs, modes, *sems)
    first, last = ids[0] == 0, ids[0] == grid[0] - 1
    for a in range(1, len(grid)):
        first = jnp.logical_and(first, ids[a] == 0)
        last = jnp.logical_and(last, ids[a] == grid[a] - 1)
    pl.when(first)(start)
    return lambda: pl.when(last)(wait)


def _hgrn_fwd(main, lbrow, bsz, n_blocks, ride, ride_modes, name):
    t, d3 = main.shape
    d = d3 // 3
    n_heads = d // HEAD
    n_pairs = n_heads // HEADS_PER_STEP
    wide = HEADS_PER_STEP * HEAD
    nr = len(ride)
    grid = (bsz, n_pairs, n_blocks)

    def chunk(q_ref, fz_ref, v_ref, lb, o_ref, st_ref, b_scr, k_scr, s_scr, o_acc):
        _hgrn_gates(fz_ref, lb, b_scr, k_scr)
        yield
        c = _hgrn_common(q_ref, b_scr, k_scr)
        yield
        s_t = s_scr[...]
        st_ref[...] = s_t
        vb = v_ref[...].astype(BF16)
        q_state = c["qt"] * _expand([jnp.exp(r) for r in c["bref"]])
        o_state = _dot_nt(q_state.astype(BF16), s_t.astype(BF16))
        js = range(N_SUB - 1)
        khb = c["kh"].astype(BF16)
        lhs = [(c["qt"][SUB * (j + 1):] * jnp.exp(c["refrow"][SUB * (j + 1):] - c["bend"][j])).astype(BF16)
               for j in js]
        yield
        a_js = [_dot_nt(lhs[j], khb[SUB * j:SUB * (j + 1)]) for j in js]
        k_state = c["kh"] * _expand([jnp.exp(c["bl"] - r) for r in c["bend"]])
        s_scr[...] = s_t * jnp.exp(c["bl"]) + _dot_tn(vb, k_state.astype(BF16))
        yield
        o_js = [_dot(a_js[j].astype(BF16), vb[SUB * j:SUB * (j + 1)]) for j in js]
        yield
        o_acc[...] = o_state
        for j in js:
            o_acc[SUB * (j + 1):, :] += o_js[j]
        not_before = [_iota2((HALF, HEAD), 0) >= row for row in range(HALF)]
        for i in range(N_SUB):
            r0 = SUB * i
            q_h = [q_ref[pl.ds(r0 + HALF * u, HALF), :] for u in range(2)]
            b_h = [b_scr[pl.ds(r0 + HALF * u, HALF), :] for u in range(2)]
            o_h = [jnp.zeros((HALF, HEAD), F32) for _ in range(2)]
            for s in range(SUB):
                brow = b_scr[pl.ds(r0 + s, 1), :]
                krow = k_scr[pl.ds(r0 + s, 1), :]
                vrow = v_ref[pl.ds(r0 + s, 1), :]
                for u in range(s // HALF, 2):
                    diff = b_h[u] - brow
                    if u == s // HALF:
                        diff = jnp.where(not_before[s - HALF * u], diff, NEG_BIG)
                    col = jnp.sum(q_h[u] * krow * jnp.exp(diff), axis=-1, keepdims=True)
                    o_h[u] = o_h[u] + col * vrow
            for u in range(2):
                o_acc[pl.ds(r0 + HALF * u, HALF), :] += o_h[u]
            yield
        o_ref[...] = o_acc[...]

    def body(*refs):
        q_ref, fz_ref, v_ref, lb_ref = refs[:4]
        o_ref, st_ref, b_out, k_out = refs[4 + nr:8 + nr]
        s_scr, o_acc = refs[8 + 2 * nr:10 + 2 * nr]
        finish_ride = _ride_along(refs[4:4 + nr], refs[8 + nr:8 + 2 * nr], ride_modes, refs[10 + 2 * nr:], grid)

        @pl.when(pl.program_id(2) == 0)
        def _():
            s_scr[...] = jnp.zeros_like(s_scr)

        def head(hh):
            cols = pl.ds(hh * HEAD, HEAD)
            return chunk(q_ref.at[:, cols], fz_ref.at[:, cols], v_ref.at[:, cols], lb_ref[:, cols],
                         o_ref.at[:, cols], st_ref.at[hh], b_out.at[:, cols], k_out.at[:, cols],
                         s_scr.at[hh], o_acc.at[hh])

        _lockstep([head(hh) for hh in range(HEADS_PER_STEP)])
        finish_ride()

    def blk(col0):
        return pl.BlockSpec((BLOCK, wide), lambda b, h, n: (b * n_blocks + n, col0 + h))

    any_spec = pl.BlockSpec(memory_space=pl.ANY)
    per_head = pltpu.VMEM((HEADS_PER_STEP, BLOCK, HEAD), F32)
    return _pcall(
        body, name=name, grid=grid,
        in_specs=[blk(0), blk(n_pairs), blk(2 * n_pairs), pl.BlockSpec((1, wide), lambda b, h, n: (0, h))]
        + [any_spec] * nr,
        out_specs=(blk(0), pl.BlockSpec((None, HEADS_PER_STEP, HEAD, HEAD), lambda b, h, n: (b * n_blocks + n, h, 0, 0)),
                   blk(0), blk(0)) + (any_spec,) * nr,
        out_shape=(jax.ShapeDtypeStruct((t, d), F32),
                   jax.ShapeDtypeStruct((bsz * n_blocks, n_heads, HEAD, HEAD), F32),
                   jax.ShapeDtypeStruct((t, d), F32), jax.ShapeDtypeStruct((t, d), F32))
        + _exchange_shapes(ride, ride_modes),
        scratch_shapes=[per_head] * 2 + _exchange_sems(nr),
        compiler_params=_params("arbitrary", "arbitrary", "arbitrary"),
    )(main, main, main, lbrow, *ride)


def _hgrn_bwd(main, b_all, k_all, lbrow, states, do, bsz, n_blocks, ride, ride_modes, name):
    t, d3 = main.shape
    d = d3 // 3
    n_pairs = d // HEAD // HEADS_PER_STEP
    wide = HEADS_PER_STEP * HEAD
    nr = len(ride)
    grid = (bsz, n_pairs, n_blocks)

    def chunk(n, q_ref, b_scr, k_scr, v_ref, lb, st_ref, do_ref, dq_ref, dfz_ref, dv_ref, dlb_ref,
              ds_scr, dqt_acc, dkh_acc, dv_acc, dqd_acc, dkd_acc, ad_scr):
        c = _hgrn_common(q_ref, b_scr, k_scr)
        yield
        q, k = q_ref[...], c["k"]
        vb = v_ref[...].astype(BF16)
        dob = do_ref[...].astype(BF16)
        s0_t = st_ref[...]
        ds1_t = ds_scr[...]
        e_ref = _expand([jnp.exp(r) for r in c["bref"]])
        e_end = _expand([jnp.exp(c["bl"] - r) for r in c["bend"]])
        e_bl = jnp.exp(c["bl"])
        q_state = c["qt"] * e_ref
        k_state = c["kh"] * e_end
        dq_state = _dot(dob, s0_t.astype(BF16))
        dk_state = _dot(vb, ds1_t.astype(BF16))
        dv_state = _dot_nt(k_state.astype(BF16), ds1_t.astype(BF16))
        ds_scr[...] = ds1_t * e_bl + _dot_tn(dob, q_state.astype(BF16))
        js = range(N_SUB - 1)
        lo = [slice(SUB * j, SUB * (j + 1)) for j in js]
        khb = c["kh"].astype(BF16)
        dj = [jnp.exp(c["refrow"][SUB * (j + 1):] - c["bend"][j]) for j in js]
        lhs = [(c["qt"][SUB * (j + 1):] * dj[j]).astype(BF16) for j in js]
        yield
        a_js = [_dot_nt(lhs[j], khb[lo[j]]) for j in js]
        da_js = [_dot_nt(dob[SUB * (j + 1):], vb[lo[j]]).astype(BF16) for j in js]
        dqt_acc[...] = dq_state * e_ref
        dkh_acc[...] = dk_state * e_end
        dv_acc[...] = dv_state
        dbl = (jnp.sum(s0_t * ds1_t, axis=0, keepdims=True) * e_bl
               + jnp.sum(k_state * dk_state, axis=0, keepdims=True))
        yield
        dv_js = [_dot_tn(a_js[j].astype(BF16), dob[SUB * (j + 1):]) for j in js]
        dq_js = [_dot(da_js[j], khb[lo[j]]) * dj[j] for j in js]
        dk_js = [_dot_tn(da_js[j], lhs[j]) for j in js]
        yield
        for j in js:
            dv_acc[lo[j], :] += dv_js[j]
            dqt_acc[SUB * (j + 1):, :] += dq_js[j]
            dkh_acc[lo[j], :] += dk_js[j]
        not_before = [_iota2((HALF, HEAD), 0) >= row for row in range(HALF)]
        is_row = [_iota2((HALF, HEAD), 0) == row for row in range(HALF)]
        lane = _iota2((HALF, HEAD), 1)
        for i in range(N_SUB):
            r0 = SUB * i
            q_h = [q_ref[pl.ds(r0 + HALF * u, HALF), :] for u in range(2)]
            b_h = [b_scr[pl.ds(r0 + HALF * u, HALF), :] for u in range(2)]
            do_h = [do_ref[pl.ds(r0 + HALF * u, HALF), :] for u in range(2)]
            zero = jnp.zeros((HALF, HEAD), F32)
            dq_h, dk_h, a_h = [zero, zero], [zero, zero], [zero, zero]
            for s in range(SUB):
                brow = b_scr[pl.ds(r0 + s, 1), :]
                krow = k_scr[pl.ds(r0 + s, 1), :]
                vrow = v_ref[pl.ds(r0 + s, 1), :]
                dk_row = jnp.zeros((1, HEAD), F32)
                is_lane = lane == r0 + s
                for u in range(s // HALF, 2):
                    diff = b_h[u] - brow
                    if u == s // HALF:
                        diff = jnp.where(not_before[s - HALF * u], diff, NEG_BIG)
                    w = jnp.exp(diff)
                    qw = q_h[u] * w
                    a_col = jnp.sum(qw * krow, axis=-1, keepdims=True)
                    da_col = jnp.sum(do_h[u] * vrow, axis=-1, keepdims=True)
                    a_h[u] = jnp.where(is_lane, a_col, a_h[u])
                    dq_h[u] = dq_h[u] + da_col * (w * krow)
                    dk_row = dk_row + jnp.sum(da_col * qw, axis=0, keepdims=True)
                us = s // HALF
                dk_h[us] = jnp.where(is_row[s - HALF * us], dk_row, dk_h[us])
            for u in range(2):
                rows = pl.ds(r0 + HALF * u, HALF)
                dqd_acc[rows, :] = dq_h[u]
                dkd_acc[rows, :] = dk_h[u]
                ad_scr[rows, :] = a_h[u]
            yield
        dv_in = _dot_tn(ad_scr[...].astype(BF16), dob)
        dq = dqt_acc[...] * c["e_q"] + dqd_acc[...]
        dk = dkh_acc[...] * c["e_k"] + dkd_acc[...]
        rr, cc = _iota2((BLOCK, BLOCK), 0), _iota2((BLOCK, BLOCK), 1)
        db = q * dq - k * dk + jnp.where(_iota2((BLOCK, HEAD), 0) == BLOCK - 1, dbl, 0.0)
        yield
        dg = _tri_left(_tri(cc >= rr), db)
        yield
        real = jnp.logical_or(n > 0, _iota2((BLOCK, HEAD), 0) >= N_PAD)
        df = jnp.where(real, dg / (1.0 - k) - dk, 0.0)
        nsig = k * (1.0 / (1.0 - lb))
        dq_ref[...] = dq.astype(BF16)
        dv_ref[...] = (dv_acc[...] + dv_in).astype(BF16)
        dfz_ref[...] = (df * k * (1.0 - nsig)).astype(BF16)
        dlb_ref[...] += jnp.sum(df * nsig, axis=0, keepdims=True)

    def body(*refs):
        q_ref, b_ref, k_ref, v_ref, lb_ref, st_ref, do_ref = refs[:7]
        dq_ref, dfz_ref, dv_ref, dlb_ref = refs[7 + nr:11 + nr]
        scratch = refs[11 + 2 * nr:18 + 2 * nr]
        finish_ride = _ride_along(refs[7:7 + nr], refs[11 + nr:11 + 2 * nr], ride_modes, refs[18 + 2 * nr:], grid)
        step = pl.program_id(2)

        @pl.when(step == 0)
        def _():
            scratch[0][...] = jnp.zeros_like(scratch[0])
            dlb_ref[...] = jnp.zeros_like(dlb_ref)

        def head(hh):
            cols = pl.ds(hh * HEAD, HEAD)
            return chunk(n_blocks - 1 - step, q_ref.at[:, cols], b_ref.at[:, cols], k_ref.at[:, cols],
                         v_ref.at[:, cols], lb_ref[:, cols], st_ref.at[hh], do_ref.at[:, cols], dq_ref.at[:, cols],
                         dfz_ref.at[:, cols], dv_ref.at[:, cols], dlb_ref.at[:, cols],
                         *[scr.at[hh] for scr in scratch])

        _lockstep([head(hh) for hh in range(HEADS_PER_STEP)])
        finish_ride()

    def blk(col0):
        return pl.BlockSpec((BLOCK, wide), lambda b, h, s: (b * n_blocks + n_blocks - 1 - s, col0 + h))

    any_spec = pl.BlockSpec(memory_space=pl.ANY)
    per_head = pltpu.VMEM((HEADS_PER_STEP, BLOCK, HEAD), F32)
    return _pcall(
        body, name=name, grid=grid,
        in_specs=[blk(0), blk(0), blk(0), blk(2 * n_pairs), pl.BlockSpec((1, wide), lambda b, h, s: (0, h)),
                  pl.BlockSpec((None, HEADS_PER_STEP, HEAD, HEAD),
                               lambda b, h, s: (b * n_blocks + n_blocks - 1 - s, h, 0, 0)),
                  blk(0)] + [any_spec] * nr,
        out_specs=(blk(0), blk(0), blk(0), pl.BlockSpec((None, 1, wide), lambda b, h, s: (b, 0, h)))
        + (any_spec,) * nr,
        out_shape=(jax.ShapeDtypeStruct((t, d), BF16),) * 3 + (jax.ShapeDtypeStruct((bsz, 1, d), F32),)
        + _exchange_shapes(ride, ride_modes),
        scratch_shapes=[per_head] * 7 + _exchange_sems(nr),
        compiler_params=_params("arbitrary", "arbitrary", "arbitrary"),
    )(main, b_all, k_all, main, lbrow, states, do, *ride)


SB_GROUP = 3
SB_ROWS = SB_GROUP * BLOCK
SB_DEAD = -104.0
SB_NEAR = ((0, 0), (1, 0), (0, 1))
SB_HEADS = 2
SB_STATIC_OWN = tuple((hh, r, kb) for hh in range(SB_HEADS) for kb in range(SB_GROUP - 1, -1, -1)
                      for r in range(kb, SB_GROUP))
SB_STATIC_NEAR = tuple((hh, r, -1 - back) for hh in range(SB_HEADS) for r, back in SB_NEAR)


def _sb_tables():
    j = jnp.bitwise_and(_iota2((2 * BLOCK, 2 * BLOCK), 0), BLOCK - 1)
    s = _iota2((2 * BLOCK, 2 * BLOCK), 1)
    ones = s >= BLOCK
    return (_tri(jnp.logical_or(ones, j > s)), _tri(jnp.logical_or(ones, j <= s)),
            _tri(jnp.logical_or(ones, j < s)))


def _sums(x, table, pieces=2):
    hi = x.astype(BF16)
    if pieces == 1:
        r = _dot(hi, table[:BLOCK])
    else:
        lo = (x - hi.astype(F32)).astype(BF16)
        r = _dot(jnp.concatenate([hi, lo], axis=1), table)
    return r[:, :BLOCK], r[:, BLOCK:]


def _sb_logits(q, ks, scale, causal, pad_row):
    z = _dot_nt(q, ks) * scale
    log_keep = -(jnp.maximum(z, 0.0) + jnp.log(1.0 + jnp.exp(-jnp.abs(z))))
    log_beta = z + log_keep
    if causal is not None:
        log_keep = jnp.where(causal, log_keep, 0.0)
    if pad_row is not None:
        log_keep = log_keep * pad_row
    return z, log_beta, log_keep


def _sb_fwd(qkv, bsz, n_blocks, name):
    t, d3 = qkv.shape
    d = d3 // 3
    n_heads = d // HEAD
    lp = n_blocks * BLOCK
    n_groups = n_blocks // SB_GROUP
    assert n_groups * SB_GROUP == n_blocks
    scale = HEAD ** -0.5

    n_pairs = n_heads // SB_HEADS
    wide = SB_HEADS * HEAD

    def body(q_ref, k_ref, v_ref, upper_ref, o_ref, tot_ref, stop_ref, w_keep, lb_keep, c_scr):
        g = pl.program_id(2)
        upper = upper_ref[...]
        causal = _iota2((BLOCK, BLOCK), 1) < _iota2((BLOCK, BLOCK), 0)
        lane = _iota2((1, BLOCK), 1)
        o_ref[...] = jnp.zeros_like(o_ref)
        c_scr[...] = jnp.zeros_like(c_scr)

        def tiles(jobs, keep=False):
            cols = [pl.ds(hh * HEAD, HEAD) for hh, _, _, _, _ in jobs]
            qrows = [pl.ds(r * BLOCK, BLOCK) for _, r, _, _, _ in jobs]
            krows = [pl.ds(pl.multiple_of(m * BLOCK, BLOCK), BLOCK) for _, _, m, _, _ in jobs]
            n_jobs = range(len(jobs))
            lg = [_sb_logits(q_ref[qrows[i], cols[i]], k_ref[krows[i], cols[i]], scale,
                             causal if jobs[i][3] else None, jobs[i][4]) for i in n_jobs]
            sm = [_sums(x[2], upper) for x in lg]
            a_all = []
            for i in n_jobs:
                c = c_scr[qrows[i], cols[i]]
                a = jnp.exp(lg[i][1] + c + sm[i][0])
                a_all.append((jnp.where(causal, a, 0.0) if jobs[i][3] else a).astype(BF16))
                c_scr[qrows[i], cols[i]] = c + sm[i][1]
            out = [_dot(a_all[i], v_ref[krows[i], cols[i]]) for i in n_jobs]
            for i in n_jobs:
                o_ref[qrows[i], cols[i]] += out[i]
                if keep:
                    w_keep[i] = a_all[i]
                    lb_keep[i] = lg[i][1].astype(BF16)

        def pad_row_of(m):
            return jnp.where(jnp.logical_or(m > 0, lane >= N_PAD), 1.0, 0.0)

        base = SB_GROUP * g
        heads = range(SB_HEADS)
        own = [(hh, r, base + rel, r == rel, pad_row_of(base) if rel == 0 else None) for hh, r, rel in SB_STATIC_OWN]
        near = [(hh, r, base + rel, False, None) for hh, r, rel in SB_STATIC_NEAR]
        pl.when(g == 0)(lambda: tiles(own, keep=True))
        pl.when(g > 0)(lambda: tiles(own + near, keep=True))

        stop_ref[...] = jnp.zeros_like(stop_ref)
        mine = [(hh, r) for hh in heads for r in range(SB_GROUP)]
        reach = [jnp.max(c_scr[pl.ds(r * BLOCK, BLOCK), pl.ds(hh * HEAD, HEAD)]) for hh, r in mine]
        for (hh, r), reach_now in zip(mine, reach):
            def live(carry):
                m, c_max = carry
                return jnp.logical_and(m >= 0, c_max >= SB_DEAD)

            def step(carry, hh=hh, r=r):
                m, _ = carry
                tiles([(hh, r, m, False, pad_row_of(m))])
                return m - 1, jnp.max(c_scr[pl.ds(r * BLOCK, BLOCK), pl.ds(hh * HEAD, HEAD)])

            lowest = jnp.maximum(base - (SB_GROUP - 1 - r), 0)
            m_end, _ = lax.while_loop(live, step, (lowest - 1, reach_now))
            stop_ref[pl.ds(SB_GROUP * hh + r, 1), :] = jnp.broadcast_to((m_end + 1).astype(F32), (1, 128))
        tot_ref[...] = c_scr[...]

    qblk = pl.BlockSpec((SB_ROWS, wide), lambda b, h, g: (b * n_groups + g, h))
    table = pl.BlockSpec((2 * BLOCK, 2 * BLOCK), lambda b, h, g: (0, 0))
    n_static = len(SB_STATIC_OWN) + len(SB_STATIC_NEAR)
    kept = pl.BlockSpec((None, n_static, BLOCK, BLOCK), lambda b, h, g: ((b * n_pairs + h) * n_groups + g, 0, 0, 0))
    return _pcall(
        body, name=name, grid=(bsz, n_pairs, n_groups),
        in_specs=[qblk, pl.BlockSpec((lp, wide), lambda b, h, g: (b, n_pairs + h)),
                  pl.BlockSpec((lp, wide), lambda b, h, g: (b, 2 * n_pairs + h)), table],
        out_specs=(qblk, qblk, pl.BlockSpec((None, 8, 128), lambda b, h, g: ((b * n_pairs + h) * n_groups + g, 0, 0)),
                   kept, kept),
        out_shape=(jax.ShapeDtypeStruct((t, d), F32), jax.ShapeDtypeStruct((t, d), F32),
                   jax.ShapeDtypeStruct((bsz * n_pairs * n_groups, 8, 128), F32))
        + (jax.ShapeDtypeStruct((bsz * n_pairs * n_groups, n_static, BLOCK, BLOCK), BF16),) * 2,
        scratch_shapes=[pltpu.VMEM((SB_ROWS, wide), F32)],
        compiler_params=_params("parallel", "parallel", "arbitrary"),
    )(qkv, qkv, qkv, _sb_tables()[0])


def _sb_bwd(qkv, do, tot, stop, w_kept, lb_kept, bsz, n_blocks, name):
    t, d3 = qkv.shape
    d = d3 // 3
    n_heads = d // HEAD
    lp = n_blocks * BLOCK
    n_groups = n_blocks // SB_GROUP
    scale = HEAD ** -0.5

    def body(q_ref, k_ref, v_ref, do_ref, tot_ref, stop_ref, incl_ref, excl_ref, w_kept, lb_kept,
             dq_ref, dk_ref, dv_ref, dk_acc, dv_acc, dq_acc, p_scr, e_scr, dob_scr):
        g = pl.program_id(2)

        @pl.when(g == 0)
        def _():
            dk_acc[...] = jnp.zeros_like(dk_acc)
            dv_acc[...] = jnp.zeros_like(dv_acc)

        incl, excl = incl_ref[...], excl_ref[...]
        causal = _iota2((BLOCK, BLOCK), 1) < _iota2((BLOCK, BLOCK), 0)
        lane = _iota2((1, BLOCK), 1)
        dob_scr[...] = do_ref[...].astype(BF16)
        dq_acc[...] = jnp.zeros_like(dq_acc)
        p_scr[...] = jnp.zeros_like(p_scr)
        e_scr[...] = jnp.zeros_like(e_scr)

        def tiles(jobs, slots=None):
            n_jobs = range(len(jobs))
            cols = [pl.ds(hh * HEAD, HEAD) for hh, _, _, _, _ in jobs]
            qrows = [pl.ds(r * BLOCK, BLOCK) for _, r, _, _, _ in jobs]
            krows = [pl.ds(pl.multiple_of(m * BLOCK, BLOCK), BLOCK) for _, _, m, _, _ in jobs]
            diag = [dg for _, _, _, dg, _ in jobs]
            d_a = [_dot_nt(dob_scr[qrows[i], cols[i]], v_ref[krows[i], cols[i]]) for i in n_jobs]
            if slots is None:
                lg = [_sb_logits(q_ref[qrows[i], cols[i]], k_ref[krows[i], cols[i]], scale,
                                 causal if diag[i] else None, jobs[i][4]) for i in n_jobs]
                log_beta = [x[1] for x in lg]
                sm = [_sums(lg[i][2], incl) for i in n_jobs]
                a_all = []
                for i in n_jobs:
                    p = p_scr[qrows[i], cols[i]]
                    a = jnp.exp(log_beta[i] + (tot_ref[qrows[i], cols[i]] - p - sm[i][0]))
                    a_all.append((jnp.where(causal, a, 0.0) if diag[i] else a).astype(BF16))
                    p_scr[qrows[i], cols[i]] = p + sm[i][1]
            else:
                a_all = [w_kept[s] for s in slots]
                log_beta = [lb_kept[s].astype(F32) for s in slots]
            gr = [d_a[i] * a_all[i].astype(F32) for i in n_jobs]
            dv_part = [_dot_tn(a_all[i], dob_scr[qrows[i], cols[i]]) for i in n_jobs]
            gs = [_sums(gr[i], excl, pieces=1) for i in n_jobs]
            dz_all = []
            for i in n_jobs:
                e = e_scr[qrows[i], cols[i]]
                dz = gr[i] - (gr[i] + e + gs[i][0]) * jnp.exp(log_beta[i])
                if diag[i]:
                    dz = jnp.where(causal, dz, 0.0)
                dz_all.append((dz * scale).astype(BF16))
                e_scr[qrows[i], cols[i]] = e + gs[i][1]
            dk_part = [_dot_tn(dz_all[i], q_ref[qrows[i], cols[i]]) for i in n_jobs]
            dq_part = [_dot(dz_all[i], k_ref[krows[i], cols[i]]) for i in n_jobs]
            for i in n_jobs:
                dv_acc[krows[i], cols[i]] += dv_part[i]
                dk_acc[krows[i], cols[i]] += dk_part[i]
                dq_acc[qrows[i], cols[i]] += dq_part[i]

        def pad_row_of(m):
            return jnp.where(jnp.logical_or(m > 0, lane >= N_PAD), 1.0, 0.0)

        base = SB_GROUP * g
        heads = range(SB_HEADS)
        mine = [(hh, r) for hh in heads for r in range(SB_GROUP)]
        stops = [jnp.max(stop_ref[pl.ds(SB_GROUP * hh + r, 1), :]).astype(jnp.int32) for hh, r in mine]
        for (hh, r), stop_now in zip(mine, stops):
            lowest = jnp.maximum(base - (SB_GROUP - 1 - r), 0)

            def step(m, hh=hh, r=r):
                tiles([(hh, r, m, False, pad_row_of(m))])
                return m + 1

            lax.while_loop(lambda m, lowest=lowest: m < lowest, step, jnp.clip(stop_now, 0, lowest))
        slot_of = {tile: i for i, tile in enumerate(SB_STATIC_OWN + SB_STATIC_NEAR)}
        own = sorted(SB_STATIC_OWN, key=lambda tile: (tile[0], tile[2], tile[1]))
        near = sorted(SB_STATIC_NEAR, key=lambda tile: (tile[0], tile[2], tile[1]))

        def static_batch(order):
            tiles([(hh, r, base + rel, r == rel, None) for hh, r, rel in order], slots=[slot_of[tile] for tile in order])

        pl.when(g == 0)(lambda: static_batch(own))
        pl.when(g > 0)(lambda: static_batch(near + own))
        dq_ref[...] = dq_acc[...].astype(BF16)

        @pl.when(g == n_groups - 1)
        def _():
            dk_ref[...] = dk_acc[...].astype(BF16)
            dv_ref[...] = dv_acc[...].astype(BF16)

    n_pairs = n_heads // SB_HEADS
    wide = SB_HEADS * HEAD
    qblk = pl.BlockSpec((SB_ROWS, wide), lambda b, h, g: (b * n_groups + g, h))
    kblk = pl.BlockSpec((lp, wide), lambda b, h, g: (b, n_pairs + h))
    vblk = pl.BlockSpec((lp, wide), lambda b, h, g: (b, 2 * n_pairs + h))
    hblk = pl.BlockSpec((lp, wide), lambda b, h, g: (b, h))
    sblk = pl.BlockSpec((None, 8, 128), lambda b, h, g: ((b * n_pairs + h) * n_groups + g, 0, 0))
    table = pl.BlockSpec((2 * BLOCK, 2 * BLOCK), lambda b, h, g: (0, 0))
    kept = pl.BlockSpec((None,) + w_kept.shape[1:], lambda b, h, g: ((b * n_pairs + h) * n_groups + g, 0, 0, 0))
    _, incl, excl = _sb_tables()
    return _pcall(
        body, name=name, grid=(bsz, n_pairs, n_groups),
        in_specs=[qblk, kblk, vblk, qblk, qblk, sblk, table, table, kept, kept],
        out_specs=(qblk, hblk, hblk),
        out_shape=(jax.ShapeDtypeStruct((t, d), BF16),) * 3,
        scratch_shapes=[pltpu.VMEM((lp, wide), F32), pltpu.VMEM((lp, wide), F32)]
        + [pltpu.VMEM((SB_ROWS, wide), F32)] * 3 + [pltpu.VMEM((SB_ROWS, wide), BF16)],
        compiler_params=_params("parallel", "parallel", "arbitrary"),
    )(qkv, qkv, qkv, do, tot, stop, incl, excl, w_kept, lb_kept)


def _adamw(w, g, m, v):
    m = ADAM_B1 * m + (1.0 - ADAM_B1) * g
    v = ADAM_B2 * v + (1.0 - ADAM_B2) * (g * g)
    m_hat = m / (1.0 - ADAM_B1 ** ADAM_STEP)
    v_hat = v / (1.0 - ADAM_B2 ** ADAM_STEP)
    delta = -ADAM_LR * (m_hat / (jnp.sqrt(v_hat) + ADAM_EPS) + ADAM_WD * w)
    return delta, m, v


def _update_sharded(w, parts, m, v, name):
    r, c = w.shape
    tr = UPDATE_ROWS if r % UPDATE_ROWS == 0 else r

    def body(w_ref, p_ref, m_ref, v_ref, g_ref, d_ref, nm_ref, nv_ref):
        g = p_ref[0].astype(F32)
        for q in range(1, N_DEV):
            g = g + p_ref[q].astype(F32)
        g_ref[...] = g
        d_ref[...], nm_ref[...], nv_ref[...] = _adamw(w_ref[...], g, m_ref[...], v_ref[...])

    row = pl.BlockSpec((tr, c), lambda i: (i, 0))
    return _pcall(
        body, name=name, grid=(r // tr,),
        in_specs=[row, pl.BlockSpec((N_DEV, tr, c), lambda i: (0, i, 0)), row, row],
        out_specs=(row,) * 4, out_shape=(jax.ShapeDtypeStruct((r, c), F32),) * 4,
        compiler_params=_params("parallel"),
    )(w, parts, m, v)


SMALL_ROWS = 8


def _pack_small(dpre0, dpre1, dpost0, dpost1, dlb, dwon, loss, name):
    d = dpre0.shape[1]
    bsz = dlb.shape[0]

    def body(a0, a1, p0, p1, lb_ref, on_ref, loss_ref, out_ref):
        out_ref[...] = jnp.zeros_like(out_ref)
        out_ref[pl.ds(0, 1), :] = a0[...]
        out_ref[pl.ds(1, 1), :] = a1[...]
        out_ref[pl.ds(2, 1), :] = p0[...]
        out_ref[pl.ds(3, 1), :] = p1[...]
        acc = lb_ref[0]
        for b in range(1, bsz):
            acc = acc + lb_ref[b]
        out_ref[pl.ds(4, 1), :] = acc
        out_ref[pl.ds(5, 1), pl.ds(0, HEAD)] = on_ref[...]
        out_ref[pl.ds(6, 1), pl.ds(0, HEAD)] = loss_ref[pl.ds(0, 1), :]

    return _pcall(body, name=name, out_shape=jax.ShapeDtypeStruct((SMALL_ROWS, d), F32))(
        dpre0, dpre1, dpost0, dpost1, dlb, dwon, loss)


def _update_small(parts, pre, post, lbw, on, moments, name):
    d = pre.shape[1]

    def body(p_ref, pre_ref, post_ref, lbw_ref, on_ref, mpre, mpost, mlb, mon, vpre, vpost, vlb, von,
             loss_ref, *outs):
        def total(r0, nr, width):
            acc = p_ref[0, pl.ds(r0, nr), pl.ds(0, width)]
            for q in range(1, N_DEV):
                acc = acc + p_ref[q, pl.ds(r0, nr), pl.ds(0, width)]
            return acc

        def put(k, w, g, m, v):
            dl, nm, nv = _adamw(w, g, m, v)
            outs[4 * k][...] = g
            outs[4 * k + 1][...] = dl
            outs[4 * k + 2][...] = nm
            outs[4 * k + 3][...] = nv

        put(0, pre_ref[...], total(0, 2, d), mpre[...], vpre[...])
        put(1, post_ref[...], total(2, 2, d), mpost[...], vpost[...])
        a0, a1 = lbw_ref[pl.ds(0, 1), :], lbw_ref[pl.ds(1, 1), :]
        mx = jnp.maximum(a0, a1)
        e0, e1 = jnp.exp(a0 - mx), jnp.exp(a1 - mx)
        p0 = e0 / (e0 + e1)
        g0 = total(4, 1, d) * p0 * (1.0 - p0)
        for r, w, g in ((0, a0, g0), (1, a1, -g0)):
            row = pl.ds(r, 1)
            dl, nm, nv = _adamw(w, g, mlb[row, :], vlb[row, :])
            outs[8][row, :] = g
            outs[9][row, :] = dl
            outs[10][row, :] = nm
            outs[11][row, :] = nv
        put(3, on_ref[...], total(5, 1, HEAD), mon[...], von[...])
        loss_ref[...] = jnp.broadcast_to(total(6, 1, HEAD), loss_ref.shape)

    shapes = []
    for w in (pre, post, lbw, on):
        shapes += [jax.ShapeDtypeStruct(w.shape, F32)] * 4
    return _pcall(body, name=name, out_shape=(jax.ShapeDtypeStruct((8, 128), F32), *shapes))(
        parts, pre, post, lbw, on, *moments)


def kernel(x, meta_tokens, pre_norm, post_norm, hgrn_w_in, hgrn_lb, hgrn_out_norm, hgrn_w_out, sb_w_in, sb_w_out, loss_target, m_meta_tokens, m_pre_norm, m_post_norm, m_hgrn_w_in, m_hgrn_lb, m_hgrn_out_norm, m_hgrn_w_out, m_sb_w_in, m_sb_w_out, v_meta_tokens, v_pre_norm, v_post_norm, v_hgrn_w_in, v_hgrn_lb, v_hgrn_out_norm, v_hgrn_w_out, v_sb_w_in, v_sb_w_out):
    bsz, seq, d = x.shape
    n_blocks = seq // BLOCK + 1
    lp = n_blocks * BLOCK
    s = hgrn_w_in.shape[2]
    dsh = d // N_DEV

    w_in_h, meta_all = _gather_once_per_chip([hgrn_w_in[0].astype(BF16), meta_tokens], "gather_weights")
    meta_full = jnp.transpose(meta_all, (1, 0, 2)).reshape(N_META, d)

    lbrow = jnp.cumsum(jax.nn.softmax(hgrn_lb, axis=0), axis=0)[0:1]

    main0, gate0, yn0, h0 = _norm_inproj(None, pre_norm[0:1], w_in_h, F32, "inproj_hgrn", from_x=(x, meta_full))
    o0, states, decay0, k0, w_in_s, w_out_s, w_out_h = _hgrn_fwd(
        main0, lbrow, bsz, n_blocks,
        [sb_w_in[0].astype(BF16), sb_w_out[0].astype(BF16), hgrn_w_out[0].astype(BF16)], ["gather"] * 3, "hgrn_fwd")
    w_out_s = w_out_s.reshape(d, d)
    w_out_h = w_out_h.reshape(d, d)
    h1, y0 = _mix_out(o0, gate0, h0, w_out_h, hgrn_out_norm, post_norm[0:1], True, "mix_out_hgrn")
    main1, gate1, yn1 = _norm_inproj(h1, pre_norm[1:2], w_in_s, BF16, "inproj_sb")
    o1, tot, stop, w_kept, lb_kept = _sb_fwd(main1, bsz, n_blocks, "sb_fwd")
    dh2, y1, loss_part = _mix_out(o1, gate1, h1, w_out_s, hgrn_out_norm, post_norm[1:2], False, "mix_out_sb",
                                  target=loss_target)

    do1, dgate1, dw_out_s, dpost1, _ = _mix_out_bwd(
        dh2, y1, o1, gate1, w_out_s, hgrn_out_norm, post_norm[1:2], False, "mix_out_sb_bwd")
    dq1, dk1, dv1 = _sb_bwd(main1, do1, tot, stop, w_kept, lb_kept, bsz, n_blocks, "sb_bwd")
    dproj1 = (dq1, dk1, dv1, dgate1)
    dh1, dpre1 = _inproj_bwd_x(dproj1, w_in_s, h1, pre_norm[1:2], dh2, [], [], "inproj_sb_bwd_x")
    dw_in_s = _inproj_bwd_w(yn1, dproj1, s, "inproj_sb_bwd_w")

    do0, dgate0, dw_out_h, dpost0, dwon = _mix_out_bwd(
        dh1, y0, o0, gate0, w_out_h, hgrn_out_norm, post_norm[0:1], True, "mix_out_hgrn_bwd")
    dq0, dfz0, dv0, dlb, p_in_s, p_out_s, p_out_h = _hgrn_bwd(
        main0, decay0, k0, lbrow, states, do0, bsz, n_blocks,
        [dw_in_s, dw_out_s.reshape(N_DEV, dsh, d), dw_out_h.reshape(N_DEV, dsh, d)], ["scatter"] * 3, "hgrn_bwd")
    dproj0 = (dq0, dfz0, dv0, dgate0)
    dw_in_h = _inproj_bwd_w(yn0, dproj0, s, "inproj_hgrn_bwd_w")
    dh0, dpre0, p_in_h = _inproj_bwd_x(
        dproj0, w_in_h, h0, pre_norm[0:1], dh1, [dw_in_h], ["scatter"], "inproj_hgrn_bwd_x")

    dh0 = dh0.reshape(bsz, lp, d)
    grad_x = dh0[:, BLOCK:]
    dmeta = jnp.sum(dh0[:, N_PAD:BLOCK], axis=0)
    dmeta = jnp.transpose(dmeta.reshape(N_META, N_DEV, dsh), (1, 0, 2))
    small = _pack_small(dpre0, dpre1, dpost0, dpost1, dlb, dwon, loss_part, "pack_small")

    p_meta, p_small = _exchange([dmeta, small], ["scatter", "gather"], "exchange_grads")

    u_meta = _update_sharded(meta_tokens, p_meta, m_meta_tokens, v_meta_tokens, "update_meta")
    u_in_h = _update_sharded(hgrn_w_in[0], p_in_h, m_hgrn_w_in[0], v_hgrn_w_in[0], "update_hgrn_w_in")
    u_out_h = _update_sharded(hgrn_w_out[0], p_out_h, m_hgrn_w_out[0], v_hgrn_w_out[0], "update_hgrn_w_out")
    u_in_s = _update_sharded(sb_w_in[0], p_in_s, m_sb_w_in[0], v_sb_w_in[0], "update_sb_w_in")
    u_out_s = _update_sharded(sb_w_out[0], p_out_s, m_sb_w_out[0], v_sb_w_out[0], "update_sb_w_out")
    sm = _update_small(p_small, pre_norm, post_norm, hgrn_lb, hgrn_out_norm,
                       (m_pre_norm, m_post_norm, m_hgrn_lb, m_hgrn_out_norm,
                        v_pre_norm, v_post_norm, v_hgrn_lb, v_hgrn_out_norm), "update_small")
    loss = sm[0][0, 0]
    u_pre, u_post, u_lb, u_on = sm[1:5], sm[5:9], sm[9:13], sm[13:17]

    per_w = [u_meta, u_pre, u_post, tuple(a[None] for a in u_in_h), u_lb, u_on,
             tuple(a[None] for a in u_out_h), tuple(a[None] for a in u_in_s), tuple(a[None] for a in u_out_s)]
    outs = [loss, grad_x]
    for k in range(4):
        outs += [u[k] for u in per_w]
    return tuple(outs)
```

```python
import jax
import jax.numpy as jnp
from jax import lax
from jax.experimental import pallas as pl
from jax.experimental.pallas import tpu as pltpu

F32 = jnp.float32
BF16 = jnp.bfloat16

N_DEV = 8
BLOCK = 128
N_META = 16
N_PAD = BLOCK - N_META
HEAD = 128
SUB = 16
N_SUB = BLOCK // SUB
HALF = 8
N_HALF = SUB // HALF
EPS = 1e-6
ROW_TILE = 3 * BLOCK
UPDATE_ROWS = 256
K_TILES = (2816, 1408, 768, 384, 128)
NEG_BIG = -1e30

ADAM_LR = 0.001
ADAM_B1 = 0.9
ADAM_B2 = 0.999
ADAM_EPS = 1e-08
ADAM_WD = 0.01
ADAM_STEP = 10

VMEM_LIMIT = 56 * 1024 * 1024


def _pcall(body, **kw):
    return pl.pallas_call(body, **kw)


def _params(*sem):
    return pltpu.CompilerParams(dimension_semantics=sem, vmem_limit_bytes=VMEM_LIMIT)


def _dot(a, b):
    return jnp.dot(a, b, preferred_element_type=F32)


def _dot_nt(a, b):
    return lax.dot_general(a, b, (((1,), (1,)), ((), ())), preferred_element_type=F32)


def _dot_tn(a, b):
    return lax.dot_general(a, b, (((0,), (0,)), ((), ())), preferred_element_type=F32)


def _split(x, pieces):
    out = []
    for _ in range(pieces):
        p = x.astype(BF16)
        out.append(p)
        x = x - p.astype(F32)
    return out


def _tri_left(tri, x, pieces=3):
    return sum(_dot(tri, p) for p in _split(x, pieces))


def _iota2(shape, dim):
    return lax.broadcasted_iota(jnp.int32, shape, dim)


def _tri(cond):
    return jnp.where(cond, 1.0, 0.0).astype(BF16)


def _sig_pair(x):
    e = jnp.exp(-jnp.abs(x))
    r = 1.0 / (1.0 + e)
    er = e * r
    pos = x >= 0
    return jnp.where(pos, r, er), jnp.where(pos, er, r)


def _expand(rows):
    return jnp.concatenate([jnp.broadcast_to(r, (SUB, HEAD)) for r in rows], axis=0)


def _exchange_shapes(arrays, modes):
    return tuple(jax.ShapeDtypeStruct((N_DEV,) + tuple(a.shape[1:] if m == "scatter" else a.shape), a.dtype)
                 for a, m in zip(arrays, modes))


def _exchange_sems(n):
    if n == 0:
        return []
    return [pltpu.SemaphoreType.DMA((n, N_DEV - 1)), pltpu.SemaphoreType.DMA((n, N_DEV - 1)),
            pltpu.SemaphoreType.DMA((n,))]


def _exchange_copies(ins, outs, modes, send_sems, recv_sems, local_sems):
    mx, my, mc = lax.axis_index("x"), lax.axis_index("y"), lax.axis_index("c")
    me = 4 * mx + 2 * my + mc

    def src(i, slot):
        return ins[i].at[slot] if modes[i] == "scatter" else ins[i]

    def peer_of(mask):
        px = 1 - mx if mask & 4 else mx
        py = 1 - my if mask & 2 else my
        pc = 1 - mc if mask & 1 else mc
        return px, py, pc

    def copy(i, mask, dst_slot):
        px, py, pc = peer_of(mask)
        return pltpu.make_async_remote_copy(
            src_ref=src(i, 4 * px + 2 * py + pc), dst_ref=outs[i].at[dst_slot],
            send_sem=send_sems.at[i, mask - 1], recv_sem=recv_sems.at[i, mask - 1],
            device_id=(px, py, pc), device_id_type=pl.DeviceIdType.MESH)

    n = len(ins)
    sends = [copy(i, mask, me) for mask in range(1, N_DEV) for i in range(n)]
    own = [pltpu.make_async_copy(src(i, me), outs[i].at[me], local_sems.at[i]) for i in range(n)]
    arrivals = []
    for mask in range(1, N_DEV):
        px, py, pc = peer_of(mask)
        arrivals += [copy(i, mask, 4 * px + 2 * py + pc) for i in range(n)]

    def start():
        for cp in sends + own:
            cp.start()

    def wait():
        for cp in arrivals:
            cp.wait_recv()
        for cp in sends:
            cp.wait_send()
        for cp in own:
            cp.wait()

    return start, wait


def _gather_once_per_chip(arrays, name):
    n = len(arrays)

    def body(*refs):
        ins, outs = refs[:n], refs[n:2 * n]
        send_sems, recv_sems, local_sems = refs[2 * n:]
        mx, my, mc = lax.axis_index("x"), lax.axis_index("y"), lax.axis_index("c")
        me, sibling = (mx, my, mc), (mx, my, 1 - mc)
        chips = [(1 - mx, my), (mx, 1 - my), (1 - mx, 1 - my)]

        def slot(px, py, pc):
            return 4 * px + 2 * py + pc

        def copy(i, k, block, to, src=None):
            return pltpu.make_async_remote_copy(
                src_ref=outs[i].at[slot(*block)] if src is None else src, dst_ref=outs[i].at[slot(*block)],
                send_sem=send_sems.at[i, k], recv_sem=recv_sems.at[i, k],
                device_id=to, device_id_type=pl.DeviceIdType.MESH)

        own = [pltpu.make_async_copy(ins[i], outs[i].at[slot(*me)], local_sems.at[i]) for i in range(n)]
        first = [copy(i, 0, me, sibling, src=ins[i]) for i in range(n)]
        first += [copy(i, 1 + j, me, (*chip, mc), src=ins[i]) for j, chip in enumerate(chips) for i in range(n)]
        for cp in own + first:
            cp.start()
        passed = []
        for j, chip in enumerate(chips):
            for i in range(n):
                copy(i, 1 + j, (*chip, mc), me).wait_recv()
                cp = copy(i, 4 + j, (*chip, mc), sibling)
                cp.start()
                passed.append(cp)
        for i in range(n):
            copy(i, 0, sibling, me).wait_recv()
            for j, chip in enumerate(chips):
                copy(i, 4 + j, (*chip, 1 - mc), me).wait_recv()
        for cp in first + passed:
            cp.wait_send()
        for cp in own:
            cp.wait()

    any_spec = pl.BlockSpec(memory_space=pl.ANY)
    return _pcall(
        body, name=name, out_shape=_exchange_shapes(arrays, ["gather"] * n),
        in_specs=[any_spec] * n, out_specs=tuple([any_spec] * n),
        scratch_shapes=_exchange_sems(n),
    )(*arrays)


def _exchange(arrays, modes, name):
    n = len(arrays)

    def body(*refs):
        start, wait = _exchange_copies(refs[:n], refs[n:2 * n], modes, *refs[2 * n:])
        start()
        wait()

    any_spec = pl.BlockSpec(memory_space=pl.ANY)
    return _pcall(
        body, name=name, out_shape=_exchange_shapes(arrays, modes),
        in_specs=[any_spec] * n, out_specs=tuple([any_spec] * n),
        scratch_shapes=_exchange_sems(n),
    )(*arrays)


def _tile_blocks_of_x(seq, d):
    assert ROW_TILE == 3 * BLOCK and (seq + BLOCK) % ROW_TILE == 0
    per_seq = (seq + BLOCK) // ROW_TILE

    def spec(k):
        return pl.BlockSpec((None, BLOCK, d),
                            lambda i: (i // per_seq, jnp.maximum(3 * (i % per_seq) - 1 + k, 0), 0))

    return [spec(0), spec(1), spec(2)]


def _norm_inproj(h, wnorm, w_all, main_dtype, name, from_x=None):
    p_n, d, s = w_all.shape
    n_main = 3 * d // s
    tm = ROW_TILE
    if from_x is None:
        t = h.shape[0]
        lead, lead_specs = [h], [pl.BlockSpec((tm, d), lambda i: (i, 0))]
    else:
        x_in, meta = from_x
        t = x_in.shape[0] * (x_in.shape[1] + BLOCK)
        tiles_per_seq = (x_in.shape[1] + BLOCK) // tm
        lead = [x_in, x_in, x_in, meta]
        lead_specs = _tile_blocks_of_x(x_in.shape[1], d) + [pl.BlockSpec((N_META, d), lambda i: (0, 0))]
    n_lead = len(lead)

    def body(*refs):
        wn_ref, w_ref = refs[n_lead:n_lead + 2]
        main_ref, gate_ref, ynt_ref = refs[n_lead + 2:n_lead + 5]
        if from_x is None:
            x = refs[0][...]
        else:
            first_tile = pl.program_id(0) % tiles_per_seq == 0
            meta_block = jnp.concatenate([jnp.zeros((N_PAD, d), F32), refs[3][...]], axis=0)
            x = jnp.concatenate([jnp.where(first_tile, meta_block, refs[0][...]), refs[1][...], refs[2][...]], axis=0)
            refs[n_lead + 5][...] = x
        y = x * lax.rsqrt(jnp.mean(x * x, axis=-1, keepdims=True) + EPS) * wn_ref[...]
        yb = y.astype(BF16)
        ynt_ref[...] = y.T.astype(BF16)
        for p in range(p_n):
            r = _dot(yb, w_ref[p])
            if p < n_main:
                main_ref[:, p * s:(p + 1) * s] = r.astype(main_dtype)
            else:
                gate_ref[:, (p - n_main) * s:(p - n_main + 1) * s] = r

    row = pl.BlockSpec((tm, d), lambda i: (i, 0))
    extra_specs, extra_shapes = ((), ()) if from_x is None else ((row,), (jax.ShapeDtypeStruct((t, d), F32),))
    return _pcall(
        body, name=name, grid=(t // tm,),
        in_specs=lead_specs + [pl.BlockSpec((1, d), lambda i: (0, 0)), pl.BlockSpec((p_n, d, s), lambda i: (0, 0, 0))],
        out_specs=(pl.BlockSpec((tm, 3 * d), lambda i: (i, 0)), row, pl.BlockSpec((d, tm), lambda i: (0, i)))
        + extra_specs,
        out_shape=(jax.ShapeDtypeStruct((t, 3 * d), main_dtype), jax.ShapeDtypeStruct((t, d), F32),
                   jax.ShapeDtypeStruct((d, t), BF16)) + extra_shapes,
        compiler_params=_params("parallel"),
    )(*lead, wnorm, w_all)


def _mix_out(o, gate, h_in, w_out, w_on, w_post, head_norm, name, target=None):
    t, d = o.shape
    n_heads = d // HEAD
    tm = ROW_TILE
    with_loss = target is not None
    if with_loss:
        tiles_per_seq = (target.shape[1] + BLOCK) // tm

    def body(o_ref, g_ref, h_ref, w_ref, won_ref, wp_ref, *rest):
        if with_loss:
            t0_ref, t1_ref, t2_ref, hout_ref, y_ref, loss_ref, u_scr = rest
        else:
            hout_ref, y_ref, u_scr = rest
        for hh in range(n_heads):
            cs = slice(hh * HEAD, (hh + 1) * HEAD)
            oh = o_ref[:, cs]
            gt = g_ref[:, cs]
            if head_norm:
                oh = oh * lax.rsqrt(jnp.mean(oh * oh, axis=-1, keepdims=True) + EPS) * won_ref[...]
            u_scr[:, cs] = (oh * (gt * jax.nn.sigmoid(gt))).astype(BF16)
        y = _dot(u_scr[...], w_ref[...])
        y_ref[...] = y
        r = y * lax.rsqrt(jnp.mean(y * y, axis=-1, keepdims=True) + EPS) * wp_ref[...]
        h_out = h_ref[...] + r
        if not with_loss:
            hout_ref[...] = h_out
            return
        i = pl.program_id(0)

        @pl.when(i == 0)
        def _():
            loss_ref[...] = jnp.zeros_like(loss_ref)

        tgt = jnp.concatenate([t0_ref[...], t1_ref[...], t2_ref[...]], axis=0)
        real = jnp.logical_or(i % tiles_per_seq > 0, _iota2((tm, d), 0) >= BLOCK)
        err = jnp.where(real, h_out - tgt, 0.0)
        hout_ref[...] = err * (1.0 / d)
        part = jnp.sum(jnp.sum(err * err, axis=-1, keepdims=True), axis=0, keepdims=True)
        loss_ref[...] += part * (0.5 / d)

    row = pl.BlockSpec((tm, d), lambda i: (i, 0))
    in_specs = [row, row, row, pl.BlockSpec((d, d), lambda i: (0, 0)),
                pl.BlockSpec((1, HEAD), lambda i: (0, 0)), pl.BlockSpec((1, d), lambda i: (0, 0))]
    out_specs, out_shape = (row, row), (jax.ShapeDtypeStruct((t, d), F32), jax.ShapeDtypeStruct((t, d), F32))
    args = (o, gate, h_in, w_out, w_on, w_post)
    if with_loss:
        in_specs += _tile_blocks_of_x(target.shape[1], d)
        out_specs += (pl.BlockSpec((8, 128), lambda i: (0, 0)),)
        out_shape += (jax.ShapeDtypeStruct((8, 128), F32),)
        args += (target, target, target)
    return _pcall(
        body, name=name, grid=(t // tm,), in_specs=in_specs, out_specs=out_specs, out_shape=out_shape,
        scratch_shapes=[pltpu.VMEM((tm, d), BF16)],
        compiler_params=_params("arbitrary" if with_loss else "parallel"),
    )(*args)


def _mix_out_bwd(dh, y, o, gate, w_out, w_on, w_post, head_norm, name):
    t, d = o.shape
    n_heads = d // HEAD
    tm = ROW_TILE
    last = t // tm - 1

    def body(dh_ref, y_ref, o_ref, g_ref, w_ref, won_ref, wp_ref,
             do_ref, dg_ref, dw_ref, dwp_ref, dwon_ref, u_scr, acc):
        i = pl.program_id(0)

        @pl.when(i == 0)
        def _():
            acc[...] = jnp.zeros_like(acc)
            dwp_ref[...] = jnp.zeros_like(dwp_ref)
            dwon_ref[...] = jnp.zeros_like(dwon_ref)

        yv = y_ref[...]
        rs = lax.rsqrt(jnp.mean(yv * yv, axis=-1, keepdims=True) + EPS)
        yh = yv * rs
        dr = dh_ref[...]
        dwp_ref[...] += jnp.sum(dr * yh, axis=0, keepdims=True)
        wd = dr * wp_ref[...]
        dy = rs * (wd - yh * jnp.mean(wd * yh, axis=-1, keepdims=True))
        dyb = dy.astype(BF16)
        du = _dot_nt(dyb, w_ref[...])
        for hh in range(n_heads):
            cs = slice(hh * HEAD, (hh + 1) * HEAD)
            oh = o_ref[:, cs]
            gt = g_ref[:, cs]
            sg = jax.nn.sigmoid(gt)
            sl = gt * sg
            duh = du[:, cs]
            if head_norm:
                rsh = lax.rsqrt(jnp.mean(oh * oh, axis=-1, keepdims=True) + EPS)
                ohat = oh * rsh
                on = ohat * won_ref[...]
            else:
                on = oh
            u_scr[:, cs] = (on * sl).astype(BF16)
            don = duh * sl
            dg_ref[:, cs] = (duh * on * (sg * (1.0 + gt * (1.0 - sg)))).astype(BF16)
            if head_norm:
                dwon_ref[...] += jnp.sum(don * ohat, axis=0, keepdims=True)
                wdn = don * won_ref[...]
                do_ref[:, cs] = rsh * (wdn - ohat * jnp.mean(wdn * ohat, axis=-1, keepdims=True))
            else:
                do_ref[:, cs] = don
        acc[...] += _dot_tn(u_scr[...], dyb)

        @pl.when(i == last)
        def _():
            dw_ref[...] = acc[...].astype(BF16)

    row = pl.BlockSpec((tm, d), lambda i: (i, 0))
    return _pcall(
        body, name=name, grid=(t // tm,),
        in_specs=[row, row, row, row, pl.BlockSpec((d, d), lambda i: (0, 0)),
                  pl.BlockSpec((1, HEAD), lambda i: (0, 0)), pl.BlockSpec((1, d), lambda i: (0, 0))],
        out_specs=(row, row, pl.BlockSpec((d, d), lambda i: (0, 0)), pl.BlockSpec((1, d), lambda i: (0, 0)),
                   pl.BlockSpec((1, HEAD), lambda i: (0, 0))),
        out_shape=(jax.ShapeDtypeStruct((t, d), F32), jax.ShapeDtypeStruct((t, d), BF16),
                   jax.ShapeDtypeStruct((d, d), BF16), jax.ShapeDtypeStruct((1, d), F32),
                   jax.ShapeDtypeStruct((1, HEAD), F32)),
        scratch_shapes=[pltpu.VMEM((tm, d), BF16), pltpu.VMEM((d, d), F32)],
        compiler_params=_params("arbitrary"),
    )(dh, y, o, gate, w_out, w_on, w_post)


def _inproj_bwd_x(dparts, w_all, h, wnorm, dres, ride, ride_modes, name):
    t, d = h.shape
    p_n, _, s = w_all.shape
    per = d // s
    tm = ROW_TILE
    nr = len(ride)
    grid = (t // tm,)

    def body(*refs):
        d0_ref, d1_ref, d2_ref, d3_ref, w_ref, h_ref, wn_ref, dres_ref = refs[:8]
        dh_ref, dwn_ref = refs[8 + nr:10 + nr]
        finish_ride = _ride_along(refs[8:8 + nr], refs[10 + nr:10 + 2 * nr], ride_modes, refs[10 + 2 * nr:], grid)
        i = pl.program_id(0)

        @pl.when(i == 0)
        def _():
            dwn_ref[...] = jnp.zeros_like(dwn_ref)

        pieces = (d0_ref, d1_ref, d2_ref, d3_ref)
        dyn = jnp.zeros((tm, d), F32)
        for p in range(p_n):
            blk = pieces[p // per][:, (p % per) * s:(p % per + 1) * s]
            dyn = dyn + _dot_nt(blk, w_ref[p])
        x = h_ref[...]
        rs = lax.rsqrt(jnp.mean(x * x, axis=-1, keepdims=True) + EPS)
        xh = x * rs
        dwn_ref[...] += jnp.sum(dyn * xh, axis=0, keepdims=True)
        wd = dyn * wn_ref[...]
        dh_ref[...] = dres_ref[...] + rs * (wd - xh * jnp.mean(wd * xh, axis=-1, keepdims=True))
        finish_ride()

    row = pl.BlockSpec((tm, d), lambda i: (i, 0))
    any_spec = pl.BlockSpec(memory_space=pl.ANY)
    return _pcall(
        body, name=name, grid=grid,
        in_specs=[row, row, row, row, pl.BlockSpec((p_n, d, s), lambda i: (0, 0, 0)),
                  row, pl.BlockSpec((1, d), lambda i: (0, 0)), row] + [any_spec] * nr,
        out_specs=(row, pl.BlockSpec((1, d), lambda i: (0, 0))) + (any_spec,) * nr,
        out_shape=(jax.ShapeDtypeStruct((t, d), F32), jax.ShapeDtypeStruct((1, d), F32))
        + _exchange_shapes(ride, ride_modes),
        scratch_shapes=_exchange_sems(nr),
        compiler_params=_params("arbitrary"),
    )(*dparts, w_all, h, wnorm, dres, *ride)


def _inproj_bwd_w(ynt, dparts, s, name):
    d, t = ynt.shape
    per = d // s
    n_sh = len(dparts) * per
    tk = next(c for c in K_TILES if t % c == 0)
    last = t // tk - 1

    def body(ynt_ref, d0_ref, d1_ref, d2_ref, d3_ref, dw_ref, acc):
        p, i = pl.program_id(0), pl.program_id(1)

        @pl.when(i == 0)
        def _():
            acc[...] = jnp.zeros_like(acc)

        for a, piece in enumerate((d0_ref, d1_ref, d2_ref, d3_ref)):
            @pl.when(p // per == a)
            def _():
                acc[...] += _dot(ynt_ref[...], piece[...])

        @pl.when(i == last)
        def _():
            dw_ref[...] = acc[...].astype(BF16)

    def piece_spec(a):
        return pl.BlockSpec((tk, s), lambda p, i: (jnp.where(p // per == a, i, 0),
                                                   jnp.where(p // per == a, p % per, 0)))

    return _pcall(
        body, name=name, grid=(n_sh, t // tk),
        in_specs=[pl.BlockSpec((d, tk), lambda p, i: (0, i))] + [piece_spec(a) for a in range(4)],
        out_specs=pl.BlockSpec((None, d, s), lambda p, i: (p, 0, 0)),
        out_shape=jax.ShapeDtypeStruct((n_sh, d, s), BF16),
        scratch_shapes=[pltpu.VMEM((d, s), F32)],
        compiler_params=_params("parallel", "arbitrary"),
    )(ynt, *dparts)


def _hgrn_gates(fz_ref, lb, b_ref, k_ref):
    sig, nsig = _sig_pair(fz_ref[...])
    f = lb + (1.0 - lb) * sig
    rr, cc = _iota2((BLOCK, BLOCK), 0), _iota2((BLOCK, BLOCK), 1)
    b_ref[...] = _tri_left(_tri(cc <= rr), jnp.log(f))
    k_ref[...] = (1.0 - lb) * nsig


def _hgrn_common(q_ref, b_ref, k_ref):
    b, k = b_ref[...], k_ref[...]
    bend = [b_ref[pl.ds(SUB * j + SUB - 1, 1), :] for j in range(N_SUB)]
    bref = [jnp.zeros((1, HEAD), F32)] + bend[:-1]
    refrow, bendrow = _expand(bref), _expand(bend)
    e_q = jnp.exp(b - refrow)
    e_k = jnp.exp(bendrow - b)
    qt = q_ref[...] * e_q
    kh = k * e_k
    bl = bend[-1]
    return dict(k=k, b=b, bend=bend, bref=bref, refrow=refrow, e_q=e_q, e_k=e_k, qt=qt, kh=kh, bl=bl)


HEADS_PER_STEP = 8


def _lockstep(chunks):
    live = list(chunks)
    while live:
        still = []
        for gen in live:
            try:
                next(gen)
                still.append(gen)
            except StopIteration:
                pass
        live = still


def _ride_along(ride_ins, ride_outs, modes, sems, grid):
    if not ride_ins:
        return lambda: None
    ids = [pl.program_id(a) for a in range(len(grid))]
    start, wait = _exchange_copies(ride_ins, ride_outs, modes, *sems)
    first, last = ids[0] == 0, ids[0] == grid[0] - 1
    for a in range(1, len(grid)):
        first = jnp.logical_and(first, ids[a] == 0)
        last = jnp.logical_and(last, ids[a] == grid[a] - 1)
    pl.when(first)(start)
    return lambda: pl.when(last)(wait)


def _hgrn_fwd(main, lbrow, bsz, n_blocks, ride, ride_modes, name):
    t, d3 = main.shape
    d = d3 // 3
    n_heads = d // HEAD
    n_pairs = n_heads // HEADS_PER_STEP
    wide = HEADS_PER_STEP * HEAD
    nr = len(ride)
    grid = (bsz, n_pairs, n_blocks)

    def chunk(q_ref, fz_ref, v_ref, lb, o_ref, st_ref, b_scr, k_scr, s_scr, o_acc):
        _hgrn_gates(fz_ref, lb, b_scr, k_scr)
        yield
        c = _hgrn_common(q_ref, b_scr, k_scr)
        yield
        s_t = s_scr[...]
        st_ref[...] = s_t
        vb = v_ref[...].astype(BF16)
        q_state = c["qt"] * _expand([jnp.exp(r) for r in c["bref"]])
        o_state = _dot_nt(q_state.astype(BF16), s_t.astype(BF16))
        js = range(N_SUB - 1)
        khb = c["kh"].astype(BF16)
        lhs = [(c["qt"][SUB * (j + 1):] * jnp.exp(c["refrow"][SUB * (j + 1):] - c["bend"][j])).astype(BF16)
               for j in js]
        yield
        a_js = [_dot_nt(lhs[j], khb[SUB * j:SUB * (j + 1)]) for j in js]
        k_state = c["kh"] * _expand([jnp.exp(c["bl"] - r) for r in c["bend"]])
        s_scr[...] = s_t * jnp.exp(c["bl"]) + _dot_tn(vb, k_state.astype(BF16))
        yield
        o_js = [_dot(a_js[j].astype(BF16), vb[SUB * j:SUB * (j + 1)]) for j in js]
        yield
        o_acc[...] = o_state
        for j in js:
            o_acc[SUB * (j + 1):, :] += o_js[j]
        not_before = [_iota2((HALF, HEAD), 0) >= row for row in range(HALF)]
        for i in range(N_SUB):
            r0 = SUB * i
            q_h = [q_ref[pl.ds(r0 + HALF * u, HALF), :] for u in range(N_HALF)]
            b_h = [b_scr[pl.ds(r0 + HALF * u, HALF), :] for u in range(N_HALF)]
            o_h = [jnp.zeros((HALF, HEAD), F32) for _ in range(N_HALF)]
            for s in range(SUB):
                brow = b_scr[pl.ds(r0 + s, 1), :]
                krow = k_scr[pl.ds(r0 + s, 1), :]
                vrow = v_ref[pl.ds(r0 + s, 1), :]
                for u in range(s // HALF, N_HALF):
                    diff = b_h[u] - brow
                    if u == s // HALF:
                        diff = jnp.where(not_before[s - HALF * u], diff, NEG_BIG)
                    col = jnp.sum(q_h[u] * krow * jnp.exp(diff), axis=-1, keepdims=True)
                    o_h[u] = o_h[u] + col * vrow
            for u in range(N_HALF):
                o_acc[pl.ds(r0 + HALF * u, HALF), :] += o_h[u]
            yield
        o_ref[...] = o_acc[...]

    def body(*refs):
        q_ref, fz_ref, v_ref, lb_ref = refs[:4]
        o_ref, st_ref, b_out, k_out = refs[4 + nr:8 + nr]
        s_scr, o_acc = refs[8 + 2 * nr:10 + 2 * nr]
        finish_ride = _ride_along(refs[4:4 + nr], refs[8 + nr:8 + 2 * nr], ride_modes, refs[10 + 2 * nr:], grid)

        @pl.when(pl.program_id(2) == 0)
        def _():
            s_scr[...] = jnp.zeros_like(s_scr)

        def head(hh):
            cols = pl.ds(hh * HEAD, HEAD)
            return chunk(q_ref.at[:, cols], fz_ref.at[:, cols], v_ref.at[:, cols], lb_ref[:, cols],
                         o_ref.at[:, cols], st_ref.at[hh], b_out.at[:, cols], k_out.at[:, cols],
                         s_scr.at[hh], o_acc.at[hh])

        _lockstep([head(hh) for hh in range(HEADS_PER_STEP)])
        finish_ride()

    def blk(col0):
        return pl.BlockSpec((BLOCK, wide), lambda b, h, n: (b * n_blocks + n, col0 + h))

    any_spec = pl.BlockSpec(memory_space=pl.ANY)
    per_head = pltpu.VMEM((HEADS_PER_STEP, BLOCK, HEAD), F32)
    return _pcall(
        body, name=name, grid=grid,
        in_specs=[blk(0), blk(n_pairs), blk(2 * n_pairs), pl.BlockSpec((1, wide), lambda b, h, n: (0, h))]
        + [any_spec] * nr,
        out_specs=(blk(0), pl.BlockSpec((None, HEADS_PER_STEP, HEAD, HEAD), lambda b, h, n: (b * n_blocks + n, h, 0, 0)),
                   blk(0), blk(0)) + (any_spec,) * nr,
        out_shape=(jax.ShapeDtypeStruct((t, d), F32),
                   jax.ShapeDtypeStruct((bsz * n_blocks, n_heads, HEAD, HEAD), F32),
                   jax.ShapeDtypeStruct((t, d), F32), jax.ShapeDtypeStruct((t, d), F32))
        + _exchange_shapes(ride, ride_modes),
        scratch_shapes=[per_head] * 2 + _exchange_sems(nr),
        compiler_params=_params("arbitrary", "arbitrary", "arbitrary"),
    )(main, main, main, lbrow, *ride)


def _hgrn_bwd(main, b_all, k_all, lbrow, states, do, bsz, n_blocks, ride, ride_modes, name):
    t, d3 = main.shape
    d = d3 // 3
    n_pairs = d // HEAD // HEADS_PER_STEP
    wide = HEADS_PER_STEP * HEAD
    nr = len(ride)
    grid = (bsz, n_pairs, n_blocks)

    def chunk(n, q_ref, b_scr, k_scr, v_ref, lb, st_ref, do_ref, dq_ref, dfz_ref, dv_ref, dlb_ref,
              ds_scr, dqt_acc, dkh_acc, dv_acc, dqd_acc, dkd_acc, ad_scr):
        c = _hgrn_common(q_ref, b_scr, k_scr)
        yield
        q, k = q_ref[...], c["k"]
        vb = v_ref[...].astype(BF16)
        dob = do_ref[...].astype(BF16)
        s0_t = st_ref[...]
        ds1_t = ds_scr[...]
        e_ref = _expand([jnp.exp(r) for r in c["bref"]])
        e_end = _expand([jnp.exp(c["bl"] - r) for r in c["bend"]])
        e_bl = jnp.exp(c["bl"])
        q_state = c["qt"] * e_ref
        k_state = c["kh"] * e_end
        dq_state = _dot(dob, s0_t.astype(BF16))
        dk_state = _dot(vb, ds1_t.astype(BF16))
        dv_state = _dot_nt(k_state.astype(BF16), ds1_t.astype(BF16))
        ds_scr[...] = ds1_t * e_bl + _dot_tn(dob, q_state.astype(BF16))
        js = range(N_SUB - 1)
        lo = [slice(SUB * j, SUB * (j + 1)) for j in js]
        khb = c["kh"].astype(BF16)
        dj = [jnp.exp(c["refrow"][SUB * (j + 1):] - c["bend"][j]) for j in js]
        lhs = [(c["qt"][SUB * (j + 1):] * dj[j]).astype(BF16) for j in js]
        yield
        a_js = [_dot_nt(lhs[j], khb[lo[j]]) for j in js]
        da_js = [_dot_nt(dob[SUB * (j + 1):], vb[lo[j]]).astype(BF16) for j in js]
        dqt_acc[...] = dq_state * e_ref
        dkh_acc[...] = dk_state * e_end
        dv_acc[...] = dv_state
        dbl = (jnp.sum(s0_t * ds1_t, axis=0, keepdims=True) * e_bl
               + jnp.sum(k_state * dk_state, axis=0, keepdims=True))
        yield
        dv_js = [_dot_tn(a_js[j].astype(BF16), dob[SUB * (j + 1):]) for j in js]
        dq_js = [_dot(da_js[j], khb[lo[j]]) * dj[j] for j in js]
        dk_js = [_dot_tn(da_js[j], lhs[j]) for j in js]
        yield
        for j in js:
            dv_acc[lo[j], :] += dv_js[j]
            dqt_acc[SUB * (j + 1):, :] += dq_js[j]
            dkh_acc[lo[j], :] += dk_js[j]
        not_before = [_iota2((HALF, HEAD), 0) >= row for row in range(HALF)]
        is_row = [_iota2((HALF, HEAD), 0) == row for row in range(HALF)]
        lane = _iota2((HALF, HEAD), 1)
        for i in range(N_SUB):
            r0 = SUB * i
            q_h = [q_ref[pl.ds(r0 + HALF * u, HALF), :] for u in range(N_HALF)]
            b_h = [b_scr[pl.ds(r0 + HALF * u, HALF), :] for u in range(N_HALF)]
            do_h = [do_ref[pl.ds(r0 + HALF * u, HALF), :] for u in range(N_HALF)]
            zero = jnp.zeros((HALF, HEAD), F32)
            dq_h, dk_h, a_h = [zero] * N_HALF, [zero] * N_HALF, [zero] * N_HALF
            for s in range(SUB):
                brow = b_scr[pl.ds(r0 + s, 1), :]
                krow = k_scr[pl.ds(r0 + s, 1), :]
                vrow = v_ref[pl.ds(r0 + s, 1), :]
                dk_row = jnp.zeros((1, HEAD), F32)
                is_lane = lane == r0 + s
                for u in range(s // HALF, N_HALF):
                    diff = b_h[u] - brow
                    if u == s // HALF:
                        diff = jnp.where(not_before[s - HALF * u], diff, NEG_BIG)
                    w = jnp.exp(diff)
                    qw = q_h[u] * w
                    a_col = jnp.sum(qw * krow, axis=-1, keepdims=True)
                    da_col = jnp.sum(do_h[u] * vrow, axis=-1, keepdims=True)
                    a_h[u] = jnp.where(is_lane, a_col, a_h[u])
                    dq_h[u] = dq_h[u] + da_col * (w * krow)
                    dk_row = dk_row + jnp.sum(da_col * qw, axis=0, keepdims=True)
                us = s // HALF
                dk_h[us] = jnp.where(is_row[s - HALF * us], dk_row, dk_h[us])
            for u in range(N_HALF):
                rows = pl.ds(r0 + HALF * u, HALF)
                dqd_acc[rows, :] = dq_h[u]
                dkd_acc[rows, :] = dk_h[u]
                ad_scr[rows, :] = a_h[u]
            yield
        dv_in = _dot_tn(ad_scr[...].astype(BF16), dob)
        dq = dqt_acc[...] * c["e_q"] + dqd_acc[...]
        dk = dkh_acc[...] * c["e_k"] + dkd_acc[...]
        rr, cc = _iota2((BLOCK, BLOCK), 0), _iota2((BLOCK, BLOCK), 1)
        db = q * dq - k * dk + jnp.where(_iota2((BLOCK, HEAD), 0) == BLOCK - 1, dbl, 0.0)
        yield
        dg = _tri_left(_tri(cc >= rr), db)
        yield
        real = jnp.logical_or(n > 0, _iota2((BLOCK, HEAD), 0) >= N_PAD)
        df = jnp.where(real, dg / (1.0 - k) - dk, 0.0)
        nsig = k * (1.0 / (1.0 - lb))
        dq_ref[...] = dq.astype(BF16)
        dv_ref[...] = (dv_acc[...] + dv_in).astype(BF16)
        dfz_ref[...] = (df * k * (1.0 - nsig)).astype(BF16)
        dlb_ref[...] += jnp.sum(df * nsig, axis=0, keepdims=True)

    def body(*refs):
        q_ref, b_ref, k_ref, v_ref, lb_ref, st_ref, do_ref = refs[:7]
        dq_ref, dfz_ref, dv_ref, dlb_ref = refs[7 + nr:11 + nr]
        scratch = refs[11 + 2 * nr:18 + 2 * nr]
        finish_ride = _ride_along(refs[7:7 + nr], refs[11 + nr:11 + 2 * nr], ride_modes, refs[18 + 2 * nr:], grid)
        step = pl.program_id(2)

        @pl.when(step == 0)
        def _():
            scratch[0][...] = jnp.zeros_like(scratch[0])
            dlb_ref[...] = jnp.zeros_like(dlb_ref)

        def head(hh):
            cols = pl.ds(hh * HEAD, HEAD)
            return chunk(n_blocks - 1 - step, q_ref.at[:, cols], b_ref.at[:, cols], k_ref.at[:, cols],
                         v_ref.at[:, cols], lb_ref[:, cols], st_ref.at[hh], do_ref.at[:, cols], dq_ref.at[:, cols],
                         dfz_ref.at[:, cols], dv_ref.at[:, cols], dlb_ref.at[:, cols],
                         *[scr.at[hh] for scr in scratch])

        _lockstep([head(hh) for hh in range(HEADS_PER_STEP)])
        finish_ride()

    def blk(col0):
        return pl.BlockSpec((BLOCK, wide), lambda b, h, s: (b * n_blocks + n_blocks - 1 - s, col0 + h))

    any_spec = pl.BlockSpec(memory_space=pl.ANY)
    per_head = pltpu.VMEM((HEADS_PER_STEP, BLOCK, HEAD), F32)
    return _pcall(
        body, name=name, grid=grid,
        in_specs=[blk(0), blk(0), blk(0), blk(2 * n_pairs), pl.BlockSpec((1, wide), lambda b, h, s: (0, h)),
                  pl.BlockSpec((None, HEADS_PER_STEP, HEAD, HEAD),
                               lambda b, h, s: (b * n_blocks + n_blocks - 1 - s, h, 0, 0)),
                  blk(0)] + [any_spec] * nr,
        out_specs=(blk(0), blk(0), blk(0), pl.BlockSpec((None, 1, wide), lambda b, h, s: (b, 0, h)))
        + (any_spec,) * nr,
        out_shape=(jax.ShapeDtypeStruct((t, d), BF16),) * 3 + (jax.ShapeDtypeStruct((bsz, 1, d), F32),)
        + _exchange_shapes(ride, ride_modes),
        scratch_shapes=[per_head] * 7 + _exchange_sems(nr),
        compiler_params=_params("arbitrary", "arbitrary", "arbitrary"),
    )(main, b_all, k_all, main, lbrow, states, do, *ride)


SB_GROUP = 3
SB_ROWS = SB_GROUP * BLOCK
SB_DEAD = -104.0
SB_NEAR = ((0, 0), (1, 0), (0, 1))
SB_HEADS = 4
SB_STOP_ROWS = 16
SB_STATIC_OWN = tuple((hh, r, kb) for hh in range(SB_HEADS) for kb in range(SB_GROUP - 1, -1, -1)
                      for r in range(kb, SB_GROUP))
SB_STATIC_NEAR = tuple((hh, r, -1 - back) for hh in range(SB_HEADS) for r, back in SB_NEAR)


def _sb_tables():
    j = jnp.bitwise_and(_iota2((2 * BLOCK, 2 * BLOCK), 0), BLOCK - 1)
    s = _iota2((2 * BLOCK, 2 * BLOCK), 1)
    ones = s >= BLOCK
    return (_tri(jnp.logical_or(ones, j > s)), _tri(jnp.logical_or(ones, j <= s)),
            _tri(jnp.logical_or(ones, j < s)))


def _sums(x, table, pieces=2):
    hi = x.astype(BF16)
    if pieces == 1:
        r = _dot(hi, table[:BLOCK])
    else:
        lo = (x - hi.astype(F32)).astype(BF16)
        r = _dot(jnp.concatenate([hi, lo], axis=1), table)
    return r[:, :BLOCK], r[:, BLOCK:]


def _sb_logits(q, ks, scale, causal, pad_row):
    z = _dot_nt(q, ks) * scale
    log_keep = -(jnp.maximum(z, 0.0) + jnp.log(1.0 + jnp.exp(-jnp.abs(z))))
    log_beta = z + log_keep
    if causal is not None:
        log_keep = jnp.where(causal, log_keep, 0.0)
    if pad_row is not None:
        log_keep = log_keep * pad_row
    return z, log_beta, log_keep


def _sb_fwd(qkv, bsz, n_blocks, name):
    t, d3 = qkv.shape
    d = d3 // 3
    n_heads = d // HEAD
    lp = n_blocks * BLOCK
    n_groups = n_blocks // SB_GROUP
    assert n_groups * SB_GROUP == n_blocks
    scale = HEAD ** -0.5

    n_pairs = n_heads // SB_HEADS
    wide = SB_HEADS * HEAD

    def body(q_ref, k_ref, v_ref, upper_ref, o_ref, tot_ref, stop_ref, w_keep, lb_keep, c_scr):
        g = pl.program_id(2)
        upper = upper_ref[...]
        causal = _iota2((BLOCK, BLOCK), 1) < _iota2((BLOCK, BLOCK), 0)
        lane = _iota2((1, BLOCK), 1)
        o_ref[...] = jnp.zeros_like(o_ref)
        c_scr[...] = jnp.zeros_like(c_scr)

        def tiles(jobs, keep=False):
            cols = [pl.ds(hh * HEAD, HEAD) for hh, _, _, _, _ in jobs]
            qrows = [pl.ds(r * BLOCK, BLOCK) for _, r, _, _, _ in jobs]
            krows = [pl.ds(pl.multiple_of(m * BLOCK, BLOCK), BLOCK) for _, _, m, _, _ in jobs]
            n_jobs = range(len(jobs))
            lg = [_sb_logits(q_ref[qrows[i], cols[i]], k_ref[krows[i], cols[i]], scale,
                             causal if jobs[i][3] else None, jobs[i][4]) for i in n_jobs]
            sm = [_sums(x[2], upper) for x in lg]
            a_all = []
            for i in n_jobs:
                c = c_scr[qrows[i], cols[i]]
                a = jnp.exp(lg[i][1] + c + sm[i][0])
                a_all.append((jnp.where(causal, a, 0.0) if jobs[i][3] else a).astype(BF16))
                c_scr[qrows[i], cols[i]] = c + sm[i][1]
            out = [_dot(a_all[i], v_ref[krows[i], cols[i]]) for i in n_jobs]
            for i in n_jobs:
                o_ref[qrows[i], cols[i]] += out[i]
                if keep:
                    w_keep[i] = a_all[i]
                    lb_keep[i] = lg[i][1].astype(BF16)

        def pad_row_of(m):
            return jnp.where(jnp.logical_or(m > 0, lane >= N_PAD), 1.0, 0.0)

        base = SB_GROUP * g
        heads = range(SB_HEADS)
        own = [(hh, r, base + rel, r == rel, pad_row_of(base) if rel == 0 else None) for hh, r, rel in SB_STATIC_OWN]
        near = [(hh, r, base + rel, False, None) for hh, r, rel in SB_STATIC_NEAR]
        pl.when(g == 0)(lambda: tiles(own, keep=True))
        pl.when(g > 0)(lambda: tiles(own + near, keep=True))

        stop_ref[...] = jnp.zeros_like(stop_ref)
        mine = [(hh, r) for hh in heads for r in range(SB_GROUP)]
        reach = [jnp.max(c_scr[pl.ds(r * BLOCK, BLOCK), pl.ds(hh * HEAD, HEAD)]) for hh, r in mine]
        for (hh, r), reach_now in zip(mine, reach):
            def live(carry):
                m, c_max = carry
                return jnp.logical_and(m >= 0, c_max >= SB_DEAD)

            def step(carry, hh=hh, r=r):
                m, _ = carry
                tiles([(hh, r, m, False, pad_row_of(m))])
                return m - 1, jnp.max(c_scr[pl.ds(r * BLOCK, BLOCK), pl.ds(hh * HEAD, HEAD)])

            lowest = jnp.maximum(base - (SB_GROUP - 1 - r), 0)
            m_end, _ = lax.while_loop(live, step, (lowest - 1, reach_now))
            stop_ref[pl.ds(SB_GROUP * hh + r, 1), :] = jnp.broadcast_to((m_end + 1).astype(F32), (1, 128))
        tot_ref[...] = c_scr[...]

    qblk = pl.BlockSpec((SB_ROWS, wide), lambda b, h, g: (b * n_groups + g, h))
    table = pl.BlockSpec((2 * BLOCK, 2 * BLOCK), lambda b, h, g: (0, 0))
    n_static = len(SB_STATIC_OWN) + len(SB_STATIC_NEAR)
    kept = pl.BlockSpec((None, n_static, BLOCK, BLOCK), lambda b, h, g: ((b * n_pairs + h) * n_groups + g, 0, 0, 0))
    return _pcall(
        body, name=name, grid=(bsz, n_pairs, n_groups),
        in_specs=[qblk, pl.BlockSpec((lp, wide), lambda b, h, g: (b, n_pairs + h)),
                  pl.BlockSpec((lp, wide), lambda b, h, g: (b, 2 * n_pairs + h)), table],
        out_specs=(qblk, qblk,
                   pl.BlockSpec((None, SB_STOP_ROWS, 128), lambda b, h, g: ((b * n_pairs + h) * n_groups + g, 0, 0)),
                   kept, kept),
        out_shape=(jax.ShapeDtypeStruct((t, d), F32), jax.ShapeDtypeStruct((t, d), F32),
                   jax.ShapeDtypeStruct((bsz * n_pairs * n_groups, SB_STOP_ROWS, 128), F32))
        + (jax.ShapeDtypeStruct((bsz * n_pairs * n_groups, n_static, BLOCK, BLOCK), BF16),) * 2,
        scratch_shapes=[pltpu.VMEM((SB_ROWS, wide), F32)],
        compiler_params=_params("parallel", "parallel", "arbitrary"),
    )(qkv, qkv, qkv, _sb_tables()[0])


def _sb_bwd(qkv, do, tot, stop, w_kept, lb_kept, bsz, n_blocks, name):
    t, d3 = qkv.shape
    d = d3 // 3
    n_heads = d // HEAD
    lp = n_blocks * BLOCK
    n_groups = n_blocks // SB_GROUP
    scale = HEAD ** -0.5

    def body(q_ref, k_ref, v_ref, do_ref, tot_ref, stop_ref, incl_ref, excl_ref, w_kept, lb_kept,
             dq_ref, dk_ref, dv_ref, dk_acc, dv_acc, dq_acc, p_scr, e_scr, dob_scr):
        g = pl.program_id(2)

        @pl.when(g == 0)
        def _():
            dk_acc[...] = jnp.zeros_like(dk_acc)
            dv_acc[...] = jnp.zeros_like(dv_acc)

        incl, excl = incl_ref[...], excl_ref[...]
        causal = _iota2((BLOCK, BLOCK), 1) < _iota2((BLOCK, BLOCK), 0)
        lane = _iota2((1, BLOCK), 1)
        dob_scr[...] = do_ref[...].astype(BF16)
        dq_acc[...] = jnp.zeros_like(dq_acc)
        p_scr[...] = jnp.zeros_like(p_scr)
        e_scr[...] = jnp.zeros_like(e_scr)

        def tiles(jobs, slots=None):
            n_jobs = range(len(jobs))
            cols = [pl.ds(hh * HEAD, HEAD) for hh, _, _, _, _ in jobs]
            qrows = [pl.ds(r * BLOCK, BLOCK) for _, r, _, _, _ in jobs]
            krows = [pl.ds(pl.multiple_of(m * BLOCK, BLOCK), BLOCK) for _, _, m, _, _ in jobs]
            diag = [dg for _, _, _, dg, _ in jobs]
            d_a = [_dot_nt(dob_scr[qrows[i], cols[i]], v_ref[krows[i], cols[i]]) for i in n_jobs]
            if slots is None:
                lg = [_sb_logits(q_ref[qrows[i], cols[i]], k_ref[krows[i], cols[i]], scale,
                                 causal if diag[i] else None, jobs[i][4]) for i in n_jobs]
                log_beta = [x[1] for x in lg]
                sm = [_sums(lg[i][2], incl) for i in n_jobs]
                a_all = []
                for i in n_jobs:
                    p = p_scr[qrows[i], cols[i]]
                    a = jnp.exp(log_beta[i] + (tot_ref[qrows[i], cols[i]] - p - sm[i][0]))
                    a_all.append((jnp.where(causal, a, 0.0) if diag[i] else a).astype(BF16))
                    p_scr[qrows[i], cols[i]] = p + sm[i][1]
            else:
                a_all = [w_kept[s] for s in slots]
                log_beta = [lb_kept[s].astype(F32) for s in slots]
            gr = [d_a[i] * a_all[i].astype(F32) for i in n_jobs]
            dv_part = [_dot_tn(a_all[i], dob_scr[qrows[i], cols[i]]) for i in n_jobs]
            gs = [_sums(gr[i], excl, pieces=1) for i in n_jobs]
            dz_all = []
            for i in n_jobs:
                e = e_scr[qrows[i], cols[i]]
                dz = gr[i] - (gr[i] + e + gs[i][0]) * jnp.exp(log_beta[i])
                if diag[i]:
                    dz = jnp.where(causal, dz, 0.0)
                dz_all.append((dz * scale).astype(BF16))
                e_scr[qrows[i], cols[i]] = e + gs[i][1]
            dk_part = [_dot_tn(dz_all[i], q_ref[qrows[i], cols[i]]) for i in n_jobs]
            dq_part = [_dot(dz_all[i], k_ref[krows[i], cols[i]]) for i in n_jobs]
            for i in n_jobs:
                dv_acc[krows[i], cols[i]] += dv_part[i]
                dk_acc[krows[i], cols[i]] += dk_part[i]
                dq_acc[qrows[i], cols[i]] += dq_part[i]

        def pad_row_of(m):
            return jnp.where(jnp.logical_or(m > 0, lane >= N_PAD), 1.0, 0.0)

        base = SB_GROUP * g
        heads = range(SB_HEADS)
        mine = [(hh, r) for hh in heads for r in range(SB_GROUP)]
        stops = [jnp.max(stop_ref[pl.ds(SB_GROUP * hh + r, 1), :]).astype(jnp.int32) for hh, r in mine]
        for (hh, r), stop_now in zip(mine, stops):
            lowest = jnp.maximum(base - (SB_GROUP - 1 - r), 0)

            def step(m, hh=hh, r=r):
                tiles([(hh, r, m, False, pad_row_of(m))])
                return m + 1

            lax.while_loop(lambda m, lowest=lowest: m < lowest, step, jnp.clip(stop_now, 0, lowest))
        slot_of = {tile: i for i, tile in enumerate(SB_STATIC_OWN + SB_STATIC_NEAR)}
        own = sorted(SB_STATIC_OWN, key=lambda tile: (tile[0], tile[2], tile[1]))
        near = sorted(SB_STATIC_NEAR, key=lambda tile: (tile[0], tile[2], tile[1]))

        def static_batch(order):
            tiles([(hh, r, base + rel, r == rel, None) for hh, r, rel in order], slots=[slot_of[tile] for tile in order])

        pl.when(g == 0)(lambda: static_batch(own))
        pl.when(g > 0)(lambda: static_batch(near + own))
        dq_ref[...] = dq_acc[...].astype(BF16)

        @pl.when(g == n_groups - 1)
        def _():
            dk_ref[...] = dk_acc[...].astype(BF16)
            dv_ref[...] = dv_acc[...].astype(BF16)

    n_pairs = n_heads // SB_HEADS
    wide = SB_HEADS * HEAD
    qblk = pl.BlockSpec((SB_ROWS, wide), lambda b, h, g: (b * n_groups + g, h))
    once = pl.Buffered(1)
    kblk = pl.BlockSpec((lp, wide), lambda b, h, g: (b, n_pairs + h), pipeline_mode=once)
    vblk = pl.BlockSpec((lp, wide), lambda b, h, g: (b, 2 * n_pairs + h), pipeline_mode=once)
    hblk = pl.BlockSpec((lp, wide), lambda b, h, g: (b, h), pipeline_mode=once)
    sblk = pl.BlockSpec((None, SB_STOP_ROWS, 128), lambda b, h, g: ((b * n_pairs + h) * n_groups + g, 0, 0))
    table = pl.BlockSpec((2 * BLOCK, 2 * BLOCK), lambda b, h, g: (0, 0))
    kept = pl.BlockSpec((None,) + w_kept.shape[1:], lambda b, h, g: ((b * n_pairs + h) * n_groups + g, 0, 0, 0))
    _, incl, excl = _sb_tables()
    return _pcall(
        body, name=name, grid=(bsz, n_pairs, n_groups),
        in_specs=[qblk, kblk, vblk, qblk, qblk, sblk, table, table, kept, kept],
        out_specs=(qblk, hblk, hblk),
        out_shape=(jax.ShapeDtypeStruct((t, d), BF16),) * 3,
        scratch_shapes=[pltpu.VMEM((lp, wide), F32), pltpu.VMEM((lp, wide), F32)]
        + [pltpu.VMEM((SB_ROWS, wide), F32)] * 3 + [pltpu.VMEM((SB_ROWS, wide), BF16)],
        compiler_params=_params("parallel", "parallel", "arbitrary"),
    )(qkv, qkv, qkv, do, tot, stop, incl, excl, w_kept, lb_kept)


def _adamw(w, g, m, v):
    m = ADAM_B1 * m + (1.0 - ADAM_B1) * g
    v = ADAM_B2 * v + (1.0 - ADAM_B2) * (g * g)
    m_hat = m / (1.0 - ADAM_B1 ** ADAM_STEP)
    v_hat = v / (1.0 - ADAM_B2 ** ADAM_STEP)
    delta = -ADAM_LR * (m_hat / (jnp.sqrt(v_hat) + ADAM_EPS) + ADAM_WD * w)
    return delta, m, v


def _update_sharded(w, parts, m, v, name):
    r, c = w.shape
    tr = UPDATE_ROWS if r % UPDATE_ROWS == 0 else r

    def body(w_ref, p_ref, m_ref, v_ref, g_ref, d_ref, nm_ref, nv_ref):
        g = p_ref[0].astype(F32)
        for q in range(1, N_DEV):
            g = g + p_ref[q].astype(F32)
        g_ref[...] = g
        d_ref[...], nm_ref[...], nv_ref[...] = _adamw(w_ref[...], g, m_ref[...], v_ref[...])

    row = pl.BlockSpec((tr, c), lambda i: (i, 0))
    return _pcall(
        body, name=name, grid=(r // tr,),
        in_specs=[row, pl.BlockSpec((N_DEV, tr, c), lambda i: (0, i, 0)), row, row],
        out_specs=(row,) * 4, out_shape=(jax.ShapeDtypeStruct((r, c), F32),) * 4,
        compiler_params=_params("parallel"),
    )(w, parts, m, v)


SMALL_ROWS = 8


def _pack_small(dpre0, dpre1, dpost0, dpost1, dlb, dwon, loss, name):
    d = dpre0.shape[1]
    bsz = dlb.shape[0]

    def body(a0, a1, p0, p1, lb_ref, on_ref, loss_ref, out_ref):
        out_ref[...] = jnp.zeros_like(out_ref)
        out_ref[pl.ds(0, 1), :] = a0[...]
        out_ref[pl.ds(1, 1), :] = a1[...]
        out_ref[pl.ds(2, 1), :] = p0[...]
        out_ref[pl.ds(3, 1), :] = p1[...]
        acc = lb_ref[0]
        for b in range(1, bsz):
            acc = acc + lb_ref[b]
        out_ref[pl.ds(4, 1), :] = acc
        out_ref[pl.ds(5, 1), pl.ds(0, HEAD)] = on_ref[...]
        out_ref[pl.ds(6, 1), pl.ds(0, HEAD)] = loss_ref[pl.ds(0, 1), :]

    return _pcall(body, name=name, out_shape=jax.ShapeDtypeStruct((SMALL_ROWS, d), F32))(
        dpre0, dpre1, dpost0, dpost1, dlb, dwon, loss)


def _update_small(parts, pre, post, lbw, on, moments, name):
    d = pre.shape[1]

    def body(p_ref, pre_ref, post_ref, lbw_ref, on_ref, mpre, mpost, mlb, mon, vpre, vpost, vlb, von,
             loss_ref, *outs):
        def total(r0, nr, width):
            acc = p_ref[0, pl.ds(r0, nr), pl.ds(0, width)]
            for q in range(1, N_DEV):
                acc = acc + p_ref[q, pl.ds(r0, nr), pl.ds(0, width)]
            return acc

        def put(k, w, g, m, v):
            dl, nm, nv = _adamw(w, g, m, v)
            outs[4 * k][...] = g
            outs[4 * k + 1][...] = dl
            outs[4 * k + 2][...] = nm
            outs[4 * k + 3][...] = nv

        put(0, pre_ref[...], total(0, 2, d), mpre[...], vpre[...])
        put(1, post_ref[...], total(2, 2, d), mpost[...], vpost[...])
        a0, a1 = lbw_ref[pl.ds(0, 1), :], lbw_ref[pl.ds(1, 1), :]
        mx = jnp.maximum(a0, a1)
        e0, e1 = jnp.exp(a0 - mx), jnp.exp(a1 - mx)
        p0 = e0 / (e0 + e1)
        g0 = total(4, 1, d) * p0 * (1.0 - p0)
        for r, w, g in ((0, a0, g0), (1, a1, -g0)):
            row = pl.ds(r, 1)
            dl, nm, nv = _adamw(w, g, mlb[row, :], vlb[row, :])
            outs[8][row, :] = g
            outs[9][row, :] = dl
            outs[10][row, :] = nm
            outs[11][row, :] = nv
        put(3, on_ref[...], total(5, 1, HEAD), mon[...], von[...])
        loss_ref[...] = jnp.broadcast_to(total(6, 1, HEAD), loss_ref.shape)

    shapes = []
    for w in (pre, post, lbw, on):
        shapes += [jax.ShapeDtypeStruct(w.shape, F32)] * 4
    return _pcall(body, name=name, out_shape=(jax.ShapeDtypeStruct((8, 128), F32), *shapes))(
        parts, pre, post, lbw, on, *moments)


def kernel(x, meta_tokens, pre_norm, post_norm, hgrn_w_in, hgrn_lb, hgrn_out_norm, hgrn_w_out, sb_w_in, sb_w_out, loss_target, m_meta_tokens, m_pre_norm, m_post_norm, m_hgrn_w_in, m_hgrn_lb, m_hgrn_out_norm, m_hgrn_w_out, m_sb_w_in, m_sb_w_out, v_meta_tokens, v_pre_norm, v_post_norm, v_hgrn_w_in, v_hgrn_lb, v_hgrn_out_norm, v_hgrn_w_out, v_sb_w_in, v_sb_w_out):
    bsz, seq, d = x.shape
    n_blocks = seq // BLOCK + 1
    lp = n_blocks * BLOCK
    s = hgrn_w_in.shape[2]
    dsh = d // N_DEV

    w_in_h, meta_all = _gather_once_per_chip([hgrn_w_in[0].astype(BF16), meta_tokens], "gather_weights")
    meta_full = jnp.transpose(meta_all, (1, 0, 2)).reshape(N_META, d)

    lbrow = jnp.cumsum(jax.nn.softmax(hgrn_lb, axis=0), axis=0)[0:1]

    main0, gate0, yn0, h0 = _norm_inproj(None, pre_norm[0:1], w_in_h, F32, "inproj_hgrn", from_x=(x, meta_full))
    o0, states, decay0, k0, w_in_s, w_out_s, w_out_h = _hgrn_fwd(
        main0, lbrow, bsz, n_blocks,
        [sb_w_in[0].astype(BF16), sb_w_out[0].astype(BF16), hgrn_w_out[0].astype(BF16)], ["gather"] * 3, "hgrn_fwd")
    w_out_s = w_out_s.reshape(d, d)
    w_out_h = w_out_h.reshape(d, d)
    h1, y0 = _mix_out(o0, gate0, h0, w_out_h, hgrn_out_norm, post_norm[0:1], True, "mix_out_hgrn")
    main1, gate1, yn1 = _norm_inproj(h1, pre_norm[1:2], w_in_s, BF16, "inproj_sb")
    o1, tot, stop, w_kept, lb_kept = _sb_fwd(main1, bsz, n_blocks, "sb_fwd")
    dh2, y1, loss_part = _mix_out(o1, gate1, h1, w_out_s, hgrn_out_norm, post_norm[1:2], False, "mix_out_sb",
                                  target=loss_target)

    do1, dgate1, dw_out_s, dpost1, _ = _mix_out_bwd(
        dh2, y1, o1, gate1, w_out_s, hgrn_out_norm, post_norm[1:2], False, "mix_out_sb_bwd")
    dq1, dk1, dv1 = _sb_bwd(main1, do1, tot, stop, w_kept, lb_kept, bsz, n_blocks, "sb_bwd")
    dproj1 = (dq1, dk1, dv1, dgate1)
    dh1, dpre1 = _inproj_bwd_x(dproj1, w_in_s, h1, pre_norm[1:2], dh2, [], [], "inproj_sb_bwd_x")
    dw_in_s = _inproj_bwd_w(yn1, dproj1, s, "inproj_sb_bwd_w")

    do0, dgate0, dw_out_h, dpost0, dwon = _mix_out_bwd(
        dh1, y0, o0, gate0, w_out_h, hgrn_out_norm, post_norm[0:1], True, "mix_out_hgrn_bwd")
    dq0, dfz0, dv0, dlb, p_in_s, p_out_s, p_out_h = _hgrn_bwd(
        main0, decay0, k0, lbrow, states, do0, bsz, n_blocks,
        [dw_in_s, dw_out_s.reshape(N_DEV, dsh, d), dw_out_h.reshape(N_DEV, dsh, d)], ["scatter"] * 3, "hgrn_bwd")
    dproj0 = (dq0, dfz0, dv0, dgate0)
    dw_in_h = _inproj_bwd_w(yn0, dproj0, s, "inproj_hgrn_bwd_w")
    dh0, dpre0, p_in_h = _inproj_bwd_x(
        dproj0, w_in_h, h0, pre_norm[0:1], dh1, [dw_in_h], ["scatter"], "inproj_hgrn_bwd_x")

    dh0 = dh0.reshape(bsz, lp, d)
    grad_x = dh0[:, BLOCK:]
    dmeta = jnp.sum(dh0[:, N_PAD:BLOCK], axis=0)
    dmeta = jnp.transpose(dmeta.reshape(N_META, N_DEV, dsh), (1, 0, 2))
    small = _pack_small(dpre0, dpre1, dpost0, dpost1, dlb, dwon, loss_part, "pack_small")

    p_meta, p_small = _exchange([dmeta, small], ["scatter", "gather"], "exchange_grads")

    u_meta = _update_sharded(meta_tokens, p_meta, m_meta_tokens, v_meta_tokens, "update_meta")
    u_in_h = _update_sharded(hgrn_w_in[0], p_in_h, m_hgrn_w_in[0], v_hgrn_w_in[0], "update_hgrn_w_in")
    u_out_h = _update_sharded(hgrn_w_out[0], p_out_h, m_hgrn_w_out[0], v_hgrn_w_out[0], "update_hgrn_w_out")
    u_in_s = _update_sharded(sb_w_in[0], p_in_s, m_sb_w_in[0], v_sb_w_in[0], "update_sb_w_in")
    u_out_s = _update_sharded(sb_w_out[0], p_out_s, m_sb_w_out[0], v_sb_w_out[0], "update_sb_w_out")
    sm = _update_small(p_small, pre_norm, post_norm, hgrn_lb, hgrn_out_norm,
                       (m_pre_norm, m_post_norm, m_hgrn_lb, m_hgrn_out_norm,
                        v_pre_norm, v_post_norm, v_hgrn_lb, v_hgrn_out_norm), "update_small")
    loss = sm[0][0, 0]
    u_pre, u_post, u_lb, u_on = sm[1:5], sm[5:9], sm[9:13], sm[13:17]

    per_w = [u_meta, u_pre, u_post, tuple(a[None] for a in u_in_h), u_lb, u_on,
             tuple(a[None] for a in u_out_h), tuple(a[None] for a in u_in_s), tuple(a[None] for a in u_out_s)]
    outs = [loss, grad_x]
    for k in range(4):
        outs += [u[k] for u in per_w]
    return tuple(outs)
```

```python
import jax
import jax.numpy as jnp
from jax import lax
from jax.experimental import pallas as pl
from jax.experimental.pallas import tpu as pltpu

F32 = jnp.float32
BF16 = jnp.bfloat16

N_DEV = 8
BLOCK = 128
N_META = 16
N_PAD = BLOCK - N_META
HEAD = 128
SUB = 16
N_SUB = BLOCK // SUB
HALF = 8
N_HALF = SUB // HALF
EPS = 1e-6
ROW_TILE = 3 * BLOCK
UPDATE_ROWS = 256
K_TILES = (2816, 1408, 768, 384, 128)
NEG_BIG = -1e30

ADAM_LR = 0.001
ADAM_B1 = 0.9
ADAM_B2 = 0.999
ADAM_EPS = 1e-08
ADAM_WD = 0.01
ADAM_STEP = 10

VMEM_LIMIT = 56 * 1024 * 1024


def _pcall(body, **kw):
    return pl.pallas_call(body, **kw)


def _params(*sem):
    return pltpu.CompilerParams(dimension_semantics=sem, vmem_limit_bytes=VMEM_LIMIT)


def _dot(a, b):
    return jnp.dot(a, b, preferred_element_type=F32)


def _dot_nt(a, b):
    return lax.dot_general(a, b, (((1,), (1,)), ((), ())), preferred_element_type=F32)


def _dot_tn(a, b):
    return lax.dot_general(a, b, (((0,), (0,)), ((), ())), preferred_element_type=F32)


def _split(x, pieces):
    out = []
    for _ in range(pieces):
        p = x.astype(BF16)
        out.append(p)
        x = x - p.astype(F32)
    return out


def _tri_left(tri, x, pieces=3):
    return sum(_dot(tri, p) for p in _split(x, pieces))


def _iota2(shape, dim):
    return lax.broadcasted_iota(jnp.int32, shape, dim)


def _tri(cond):
    return jnp.where(cond, 1.0, 0.0).astype(BF16)


def _sig_pair(x):
    e = jnp.exp(-jnp.abs(x))
    r = 1.0 / (1.0 + e)
    er = e * r
    pos = x >= 0
    return jnp.where(pos, r, er), jnp.where(pos, er, r)


def _expand(rows):
    return jnp.concatenate([jnp.broadcast_to(r, (SUB, HEAD)) for r in rows], axis=0)


def _exchange_shapes(arrays, modes):
    return tuple(jax.ShapeDtypeStruct((N_DEV,) + tuple(a.shape[1:] if m == "scatter" else a.shape), a.dtype)
                 for a, m in zip(arrays, modes))


def _exchange_sems(n):
    if n == 0:
        return []
    return [pltpu.SemaphoreType.DMA((n, N_DEV - 1)), pltpu.SemaphoreType.DMA((n, N_DEV - 1)),
            pltpu.SemaphoreType.DMA((n,))]


def _exchange_copies(ins, outs, modes, send_sems, recv_sems, local_sems):
    mx, my, mc = lax.axis_index("x"), lax.axis_index("y"), lax.axis_index("c")
    me = 4 * mx + 2 * my + mc

    def src(i, slot):
        return ins[i].at[slot] if modes[i] == "scatter" else ins[i]

    def peer_of(mask):
        px = 1 - mx if mask & 4 else mx
        py = 1 - my if mask & 2 else my
        pc = 1 - mc if mask & 1 else mc
        return px, py, pc

    def copy(i, mask, dst_slot):
        px, py, pc = peer_of(mask)
        return pltpu.make_async_remote_copy(
            src_ref=src(i, 4 * px + 2 * py + pc), dst_ref=outs[i].at[dst_slot],
            send_sem=send_sems.at[i, mask - 1], recv_sem=recv_sems.at[i, mask - 1],
            device_id=(px, py, pc), device_id_type=pl.DeviceIdType.MESH)

    n = len(ins)
    sends = [copy(i, mask, me) for mask in range(1, N_DEV) for i in range(n)]
    own = [pltpu.make_async_copy(src(i, me), outs[i].at[me], local_sems.at[i]) for i in range(n)]
    arrivals = []
    for mask in range(1, N_DEV):
        px, py, pc = peer_of(mask)
        arrivals += [copy(i, mask, 4 * px + 2 * py + pc) for i in range(n)]

    def start():
        for cp in sends + own:
            cp.start()

    def wait():
        for cp in arrivals:
            cp.wait_recv()
        for cp in sends:
            cp.wait_send()
        for cp in own:
            cp.wait()

    return start, wait


def _gather_once_per_chip(arrays, name):
    n = len(arrays)

    def body(*refs):
        ins, outs = refs[:n], refs[n:2 * n]
        send_sems, recv_sems, local_sems = refs[2 * n:]
        mx, my, mc = lax.axis_index("x"), lax.axis_index("y"), lax.axis_index("c")
        me, sibling = (mx, my, mc), (mx, my, 1 - mc)
        chips = [(1 - mx, my), (mx, 1 - my), (1 - mx, 1 - my)]

        def slot(px, py, pc):
            return 4 * px + 2 * py + pc

        def copy(i, k, block, to, src=None):
            return pltpu.make_async_remote_copy(
                src_ref=outs[i].at[slot(*block)] if src is None else src, dst_ref=outs[i].at[slot(*block)],
                send_sem=send_sems.at[i, k], recv_sem=recv_sems.at[i, k],
                device_id=to, device_id_type=pl.DeviceIdType.MESH)

        own = [pltpu.make_async_copy(ins[i], outs[i].at[slot(*me)], local_sems.at[i]) for i in range(n)]
        first = [copy(i, 0, me, sibling, src=ins[i]) for i in range(n)]
        first += [copy(i, 1 + j, me, (*chip, mc), src=ins[i]) for j, chip in enumerate(chips) for i in range(n)]
        for cp in own + first:
            cp.start()
        passed = []
        for j, chip in enumerate(chips):
            for i in range(n):
                copy(i, 1 + j, (*chip, mc), me).wait_recv()
                cp = copy(i, 4 + j, (*chip, mc), sibling)
                cp.start()
                passed.append(cp)
        for i in range(n):
            copy(i, 0, sibling, me).wait_recv()
            for j, chip in enumerate(chips):
                copy(i, 4 + j, (*chip, 1 - mc), me).wait_recv()
        for cp in first + passed:
            cp.wait_send()
        for cp in own:
            cp.wait()

    any_spec = pl.BlockSpec(memory_space=pl.ANY)
    return _pcall(
        body, name=name, out_shape=_exchange_shapes(arrays, ["gather"] * n),
        in_specs=[any_spec] * n, out_specs=tuple([any_spec] * n),
        scratch_shapes=_exchange_sems(n),
    )(*arrays)


def _exchange(arrays, modes, name):
    n = len(arrays)

    def body(*refs):
        start, wait = _exchange_copies(refs[:n], refs[n:2 * n], modes, *refs[2 * n:])
        start()
        wait()

    any_spec = pl.BlockSpec(memory_space=pl.ANY)
    return _pcall(
        body, name=name, out_shape=_exchange_shapes(arrays, modes),
        in_specs=[any_spec] * n, out_specs=tuple([any_spec] * n),
        scratch_shapes=_exchange_sems(n),
    )(*arrays)


def _tile_blocks_of_x(seq, d):
    assert ROW_TILE == 3 * BLOCK and (seq + BLOCK) % ROW_TILE == 0
    per_seq = (seq + BLOCK) // ROW_TILE

    def spec(k):
        return pl.BlockSpec((None, BLOCK, d),
                            lambda i: (i // per_seq, jnp.maximum(3 * (i % per_seq) - 1 + k, 0), 0))

    return [spec(0), spec(1), spec(2)]


def _norm_inproj(h, wnorm, w_all, main_dtype, name, from_x=None):
    p_n, d, s = w_all.shape
    n_main = 3 * d // s
    tm = ROW_TILE
    if from_x is None:
        t = h.shape[0]
        lead, lead_specs = [h], [pl.BlockSpec((tm, d), lambda i: (i, 0))]
    else:
        x_in, meta = from_x
        t = x_in.shape[0] * (x_in.shape[1] + BLOCK)
        tiles_per_seq = (x_in.shape[1] + BLOCK) // tm
        lead = [x_in, x_in, x_in, meta]
        lead_specs = _tile_blocks_of_x(x_in.shape[1], d) + [pl.BlockSpec((N_META, d), lambda i: (0, 0))]
    n_lead = len(lead)

    def body(*refs):
        wn_ref, w_ref = refs[n_lead:n_lead + 2]
        main_ref, gate_ref, ynt_ref = refs[n_lead + 2:n_lead + 5]
        if from_x is None:
            x = refs[0][...]
        else:
            first_tile = pl.program_id(0) % tiles_per_seq == 0
            meta_block = jnp.concatenate([jnp.zeros((N_PAD, d), F32), refs[3][...]], axis=0)
            x = jnp.concatenate([jnp.where(first_tile, meta_block, refs[0][...]), refs[1][...], refs[2][...]], axis=0)
            refs[n_lead + 5][...] = x
        y = x * lax.rsqrt(jnp.mean(x * x, axis=-1, keepdims=True) + EPS) * wn_ref[...]
        yb = y.astype(BF16)
        ynt_ref[...] = y.T.astype(BF16)
        for p in range(p_n):
            r = _dot(yb, w_ref[p])
            if p < n_main:
                main_ref[:, p * s:(p + 1) * s] = r.astype(main_dtype)
            else:
                gate_ref[:, (p - n_main) * s:(p - n_main + 1) * s] = r

    row = pl.BlockSpec((tm, d), lambda i: (i, 0))
    extra_specs, extra_shapes = ((), ()) if from_x is None else ((row,), (jax.ShapeDtypeStruct((t, d), F32),))
    return _pcall(
        body, name=name, grid=(t // tm,),
        in_specs=lead_specs + [pl.BlockSpec((1, d), lambda i: (0, 0)), pl.BlockSpec((p_n, d, s), lambda i: (0, 0, 0))],
        out_specs=(pl.BlockSpec((tm, 3 * d), lambda i: (i, 0)), row, pl.BlockSpec((d, tm), lambda i: (0, i)))
        + extra_specs,
        out_shape=(jax.ShapeDtypeStruct((t, 3 * d), main_dtype), jax.ShapeDtypeStruct((t, d), F32),
                   jax.ShapeDtypeStruct((d, t), BF16)) + extra_shapes,
        compiler_params=_params("parallel"),
    )(*lead, wnorm, w_all)


def _mix_out(o, gate, h_in, w_out, w_on, w_post, head_norm, name, target=None):
    t, d = o.shape
    n_heads = d // HEAD
    tm = ROW_TILE
    with_loss = target is not None
    if with_loss:
        tiles_per_seq = (target.shape[1] + BLOCK) // tm

    def body(o_ref, g_ref, h_ref, w_ref, won_ref, wp_ref, *rest):
        if with_loss:
            t0_ref, t1_ref, t2_ref, hout_ref, y_ref, loss_ref, u_scr = rest
        else:
            hout_ref, y_ref, u_scr = rest
        for hh in range(n_heads):
            cs = slice(hh * HEAD, (hh + 1) * HEAD)
            oh = o_ref[:, cs]
            gt = g_ref[:, cs]
            if head_norm:
                oh = oh * lax.rsqrt(jnp.mean(oh * oh, axis=-1, keepdims=True) + EPS) * won_ref[...]
            u_scr[:, cs] = (oh * (gt * jax.nn.sigmoid(gt))).astype(BF16)
        y = _dot(u_scr[...], w_ref[...])
        y_ref[...] = y
        r = y * lax.rsqrt(jnp.mean(y * y, axis=-1, keepdims=True) + EPS) * wp_ref[...]
        h_out = h_ref[...] + r
        if not with_loss:
            hout_ref[...] = h_out
            return
        i = pl.program_id(0)

        @pl.when(i == 0)
        def _():
            loss_ref[...] = jnp.zeros_like(loss_ref)

        tgt = jnp.concatenate([t0_ref[...], t1_ref[...], t2_ref[...]], axis=0)
        real = jnp.logical_or(i % tiles_per_seq > 0, _iota2((tm, d), 0) >= BLOCK)
        err = jnp.where(real, h_out - tgt, 0.0)
        hout_ref[...] = err * (1.0 / d)
        part = jnp.sum(jnp.sum(err * err, axis=-1, keepdims=True), axis=0, keepdims=True)
        loss_ref[...] += part * (0.5 / d)

    row = pl.BlockSpec((tm, d), lambda i: (i, 0))
    in_specs = [row, row, row, pl.BlockSpec((d, d), lambda i: (0, 0)),
                pl.BlockSpec((1, HEAD), lambda i: (0, 0)), pl.BlockSpec((1, d), lambda i: (0, 0))]
    out_specs, out_shape = (row, row), (jax.ShapeDtypeStruct((t, d), F32), jax.ShapeDtypeStruct((t, d), F32))
    args = (o, gate, h_in, w_out, w_on, w_post)
    if with_loss:
        in_specs += _tile_blocks_of_x(target.shape[1], d)
        out_specs += (pl.BlockSpec((8, 128), lambda i: (0, 0)),)
        out_shape += (jax.ShapeDtypeStruct((8, 128), F32),)
        args += (target, target, target)
    return _pcall(
        body, name=name, grid=(t // tm,), in_specs=in_specs, out_specs=out_specs, out_shape=out_shape,
        scratch_shapes=[pltpu.VMEM((tm, d), BF16)],
        compiler_params=_params("arbitrary" if with_loss else "parallel"),
    )(*args)


def _mix_out_bwd(dh, y, o, gate, w_out, w_on, w_post, head_norm, name):
    t, d = o.shape
    n_heads = d // HEAD
    tm = ROW_TILE
    last = t // tm - 1

    def body(dh_ref, y_ref, o_ref, g_ref, w_ref, won_ref, wp_ref,
             do_ref, dg_ref, dw_ref, dwp_ref, dwon_ref, u_scr, acc):
        i = pl.program_id(0)

        @pl.when(i == 0)
        def _():
            acc[...] = jnp.zeros_like(acc)
            dwp_ref[...] = jnp.zeros_like(dwp_ref)
            dwon_ref[...] = jnp.zeros_like(dwon_ref)

        yv = y_ref[...]
        rs = lax.rsqrt(jnp.mean(yv * yv, axis=-1, keepdims=True) + EPS)
        yh = yv * rs
        dr = dh_ref[...]
        dwp_ref[...] += jnp.sum(dr * yh, axis=0, keepdims=True)
        wd = dr * wp_ref[...]
        dy = rs * (wd - yh * jnp.mean(wd * yh, axis=-1, keepdims=True))
        dyb = dy.astype(BF16)
        du = _dot_nt(dyb, w_ref[...])
        for hh in range(n_heads):
            cs = slice(hh * HEAD, (hh + 1) * HEAD)
            oh = o_ref[:, cs]
            gt = g_ref[:, cs]
            sg = jax.nn.sigmoid(gt)
            sl = gt * sg
            duh = du[:, cs]
            if head_norm:
                rsh = lax.rsqrt(jnp.mean(oh * oh, axis=-1, keepdims=True) + EPS)
                ohat = oh * rsh
                on = ohat * won_ref[...]
            else:
                on = oh
            u_scr[:, cs] = (on * sl).astype(BF16)
            don = duh * sl
            dg_ref[:, cs] = (duh * on * (sg * (1.0 + gt * (1.0 - sg)))).astype(BF16)
            if head_norm:
                dwon_ref[...] += jnp.sum(don * ohat, axis=0, keepdims=True)
                wdn = don * won_ref[...]
                do_ref[:, cs] = rsh * (wdn - ohat * jnp.mean(wdn * ohat, axis=-1, keepdims=True))
            else:
                do_ref[:, cs] = don
        acc[...] += _dot_tn(u_scr[...], dyb)

        @pl.when(i == last)
        def _():
            dw_ref[...] = acc[...].astype(BF16)

    row = pl.BlockSpec((tm, d), lambda i: (i, 0))
    return _pcall(
        body, name=name, grid=(t // tm,),
        in_specs=[row, row, row, row, pl.BlockSpec((d, d), lambda i: (0, 0)),
                  pl.BlockSpec((1, HEAD), lambda i: (0, 0)), pl.BlockSpec((1, d), lambda i: (0, 0))],
        out_specs=(row, row, pl.BlockSpec((d, d), lambda i: (0, 0)), pl.BlockSpec((1, d), lambda i: (0, 0)),
                   pl.BlockSpec((1, HEAD), lambda i: (0, 0))),
        out_shape=(jax.ShapeDtypeStruct((t, d), F32), jax.ShapeDtypeStruct((t, d), BF16),
                   jax.ShapeDtypeStruct((d, d), BF16), jax.ShapeDtypeStruct((1, d), F32),
                   jax.ShapeDtypeStruct((1, HEAD), F32)),
        scratch_shapes=[pltpu.VMEM((tm, d), BF16), pltpu.VMEM((d, d), F32)],
        compiler_params=_params("arbitrary"),
    )(dh, y, o, gate, w_out, w_on, w_post)


def _inproj_bwd_x(dparts, w_all, h, wnorm, dres, ride, ride_modes, name):
    t, d = h.shape
    p_n, _, s = w_all.shape
    per = d // s
    tm = ROW_TILE
    nr = len(ride)
    grid = (t // tm,)

    def body(*refs):
        d0_ref, d1_ref, d2_ref, d3_ref, w_ref, h_ref, wn_ref, dres_ref = refs[:8]
        dh_ref, dwn_ref = refs[8 + nr:10 + nr]
        finish_ride = _ride_along(refs[8:8 + nr], refs[10 + nr:10 + 2 * nr], ride_modes, refs[10 + 2 * nr:], grid)
        i = pl.program_id(0)

        @pl.when(i == 0)
        def _():
            dwn_ref[...] = jnp.zeros_like(dwn_ref)

        pieces = (d0_ref, d1_ref, d2_ref, d3_ref)
        dyn = jnp.zeros((tm, d), F32)
        for p in range(p_n):
            blk = pieces[p // per][:, (p % per) * s:(p % per + 1) * s]
            dyn = dyn + _dot_nt(blk, w_ref[p])
        x = h_ref[...]
        rs = lax.rsqrt(jnp.mean(x * x, axis=-1, keepdims=True) + EPS)
        xh = x * rs
        dwn_ref[...] += jnp.sum(dyn * xh, axis=0, keepdims=True)
        wd = dyn * wn_ref[...]
        dh_ref[...] = dres_ref[...] + rs * (wd - xh * jnp.mean(wd * xh, axis=-1, keepdims=True))
        finish_ride()

    row = pl.BlockSpec((tm, d), lambda i: (i, 0))
    any_spec = pl.BlockSpec(memory_space=pl.ANY)
    return _pcall(
        body, name=name, grid=grid,
        in_specs=[row, row, row, row, pl.BlockSpec((p_n, d, s), lambda i: (0, 0, 0)),
                  row, pl.BlockSpec((1, d), lambda i: (0, 0)), row] + [any_spec] * nr,
        out_specs=(row, pl.BlockSpec((1, d), lambda i: (0, 0))) + (any_spec,) * nr,
        out_shape=(jax.ShapeDtypeStruct((t, d), F32), jax.ShapeDtypeStruct((1, d), F32))
        + _exchange_shapes(ride, ride_modes),
        scratch_shapes=_exchange_sems(nr),
        compiler_params=_params("arbitrary"),
    )(*dparts, w_all, h, wnorm, dres, *ride)


def _inproj_bwd_w(ynt, dparts, s, name):
    d, t = ynt.shape
    per = d // s
    n_sh = len(dparts) * per
    tk = next(c for c in K_TILES if t % c == 0)
    last = t // tk - 1

    def body(ynt_ref, d0_ref, d1_ref, d2_ref, d3_ref, dw_ref, acc):
        p, i = pl.program_id(0), pl.program_id(1)

        @pl.when(i == 0)
        def _():
            acc[...] = jnp.zeros_like(acc)

        for a, piece in enumerate((d0_ref, d1_ref, d2_ref, d3_ref)):
            @pl.when(p // per == a)
            def _():
                acc[...] += _dot(ynt_ref[...], piece[...])

        @pl.when(i == last)
        def _():
            dw_ref[...] = acc[...].astype(BF16)

    def piece_spec(a):
        return pl.BlockSpec((tk, s), lambda p, i: (jnp.where(p // per == a, i, 0),
                                                   jnp.where(p // per == a, p % per, 0)))

    return _pcall(
        body, name=name, grid=(n_sh, t // tk),
        in_specs=[pl.BlockSpec((d, tk), lambda p, i: (0, i))] + [piece_spec(a) for a in range(4)],
        out_specs=pl.BlockSpec((None, d, s), lambda p, i: (p, 0, 0)),
        out_shape=jax.ShapeDtypeStruct((n_sh, d, s), BF16),
        scratch_shapes=[pltpu.VMEM((d, s), F32)],
        compiler_params=_params("parallel", "arbitrary"),
    )(ynt, *dparts)


def _hgrn_gates(fz_ref, lb, b_ref, k_ref):
    sig, nsig = _sig_pair(fz_ref[...])
    f = lb + (1.0 - lb) * sig
    rr, cc = _iota2((BLOCK, BLOCK), 0), _iota2((BLOCK, BLOCK), 1)
    b_ref[...] = _tri_left(_tri(cc <= rr), jnp.log(f))
    k_ref[...] = (1.0 - lb) * nsig


def _hgrn_common(q_ref, b_ref, k_ref):
    b, k = b_ref[...], k_ref[...]
    bend = [b_ref[pl.ds(SUB * j + SUB - 1, 1), :] for j in range(N_SUB)]
    bref = [jnp.zeros((1, HEAD), F32)] + bend[:-1]
    refrow, bendrow = _expand(bref), _expand(bend)
    e_q = jnp.exp(b - refrow)
    e_k = jnp.exp(bendrow - b)
    qt = q_ref[...] * e_q
    kh = k * e_k
    bl = bend[-1]
    return dict(k=k, b=b, bend=bend, bref=bref, refrow=refrow, e_q=e_q, e_k=e_k, qt=qt, kh=kh, bl=bl)


HEADS_PER_STEP = 8


def _lockstep(chunks):
    live = list(chunks)
    while live:
        still = []
        for gen in live:
            try:
                next(gen)
                still.append(gen)
            except StopIteration:
                pass
        live = still


def _ride_along(ride_ins, ride_outs, modes, sems, grid):
    if not ride_ins:
        return lambda: None
    ids = [pl.program_id(a) for a in range(len(grid))]
    start, wait = _exchange_copies(ride_ins, ride_outs, modes, *sems)
    first, last = ids[0] == 0, ids[0] == grid[0] - 1
    for a in range(1, len(grid)):
        first = jnp.logical_and(first, ids[a] == 0)
        last = jnp.logical_and(last, ids[a] == grid[a] - 1)
    pl.when(first)(start)
    return lambda: pl.when(last)(wait)


def _hgrn_fwd(main, lbrow, bsz, n_blocks, ride, ride_modes, name):
    t, d3 = main.shape
    d = d3 // 3
    n_heads = d // HEAD
    n_pairs = n_heads // HEADS_PER_STEP
    wide = HEADS_PER_STEP * HEAD
    nr = len(ride)
    grid = (bsz, n_pairs, n_blocks)

    def chunk(q_ref, fz_ref, v_ref, lb, o_ref, st_ref, b_scr, k_scr, s_scr, o_acc):
        _hgrn_gates(fz_ref, lb, b_scr, k_scr)
        yield
        c = _hgrn_common(q_ref, b_scr, k_scr)
        yield
        s_t = s_scr[...]
        st_ref[...] = s_t
        vb = v_ref[...].astype(BF16)
        q_state = c["qt"] * _expand([jnp.exp(r) for r in c["bref"]])
        o_state = _dot_nt(q_state.astype(BF16), s_t.astype(BF16))
        js = range(N_SUB - 1)
        khb = c["kh"].astype(BF16)
        lhs = [(c["qt"][SUB * (j + 1):] * jnp.exp(c["refrow"][SUB * (j + 1):] - c["bend"][j])).astype(BF16)
               for j in js]
        yield
        a_js = [_dot_nt(lhs[j], khb[SUB * j:SUB * (j + 1)]) for j in js]
        k_state = c["kh"] * _expand([jnp.exp(c["bl"] - r) for r in c["bend"]])
        s_scr[...] = s_t * jnp.exp(c["bl"]) + _dot_tn(vb, k_state.astype(BF16))
        yield
        o_js = [_dot(a_js[j].astype(BF16), vb[SUB * j:SUB * (j + 1)]) for j in js]
        yield
        o_acc[...] = o_state
        for j in js:
            o_acc[SUB * (j + 1):, :] += o_js[j]
        not_before = [_iota2((HALF, HEAD), 0) >= row for row in range(HALF)]
        for i in range(N_SUB):
            r0 = SUB * i
            q_h = [q_ref[pl.ds(r0 + HALF * u, HALF), :] for u in range(N_HALF)]
            b_h = [b_scr[pl.ds(r0 + HALF * u, HALF), :] for u in range(N_HALF)]
            o_h = [jnp.zeros((HALF, HEAD), F32) for _ in range(N_HALF)]
            for s in range(SUB):
                brow = b_scr[pl.ds(r0 + s, 1), :]
                krow = k_scr[pl.ds(r0 + s, 1), :]
                vrow = v_ref[pl.ds(r0 + s, 1), :]
                for u in range(s // HALF, N_HALF):
                    diff = b_h[u] - brow
                    if u == s // HALF:
                        diff = jnp.where(not_before[s - HALF * u], diff, NEG_BIG)
                    col = jnp.sum(q_h[u] * krow * jnp.exp(diff), axis=-1, keepdims=True)
                    o_h[u] = o_h[u] + col * vrow
            for u in range(N_HALF):
                o_acc[pl.ds(r0 + HALF * u, HALF), :] += o_h[u]
            yield
        o_ref[...] = o_acc[...]

    def body(*refs):
        q_ref, fz_ref, v_ref, lb_ref = refs[:4]
        o_ref, st_ref, b_out, k_out = refs[4 + nr:8 + nr]
        s_scr, o_acc = refs[8 + 2 * nr:10 + 2 * nr]
        finish_ride = _ride_along(refs[4:4 + nr], refs[8 + nr:8 + 2 * nr], ride_modes, refs[10 + 2 * nr:], grid)

        @pl.when(pl.program_id(2) == 0)
        def _():
            s_scr[...] = jnp.zeros_like(s_scr)

        def head(hh):
            cols = pl.ds(hh * HEAD, HEAD)
            return chunk(q_ref.at[:, cols], fz_ref.at[:, cols], v_ref.at[:, cols], lb_ref[:, cols],
                         o_ref.at[:, cols], st_ref.at[hh], b_out.at[:, cols], k_out.at[:, cols],
                         s_scr.at[hh], o_acc.at[hh])

        _lockstep([head(hh) for hh in range(HEADS_PER_STEP)])
        finish_ride()

    def blk(col0):
        return pl.BlockSpec((BLOCK, wide), lambda b, h, n: (b * n_blocks + n, col0 + h))

    any_spec = pl.BlockSpec(memory_space=pl.ANY)
    per_head = pltpu.VMEM((HEADS_PER_STEP, BLOCK, HEAD), F32)
    return _pcall(
        body, name=name, grid=grid,
        in_specs=[blk(0), blk(n_pairs), blk(2 * n_pairs), pl.BlockSpec((1, wide), lambda b, h, n: (0, h))]
        + [any_spec] * nr,
        out_specs=(blk(0), pl.BlockSpec((None, HEADS_PER_STEP, HEAD, HEAD), lambda b, h, n: (b * n_blocks + n, h, 0, 0)),
                   blk(0), blk(0)) + (any_spec,) * nr,
        out_shape=(jax.ShapeDtypeStruct((t, d), F32),
                   jax.ShapeDtypeStruct((bsz * n_blocks, n_heads, HEAD, HEAD), F32),
                   jax.ShapeDtypeStruct((t, d), F32), jax.ShapeDtypeStruct((t, d), F32))
        + _exchange_shapes(ride, ride_modes),
        scratch_shapes=[per_head] * 2 + _exchange_sems(nr),
        compiler_params=_params("arbitrary", "arbitrary", "arbitrary"),
    )(main, main, main, lbrow, *ride)


def _hgrn_bwd(main, b_all, k_all, lbrow, states, do, bsz, n_blocks, ride, ride_modes, name):
    t, d3 = main.shape
    d = d3 // 3
    n_pairs = d // HEAD // HEADS_PER_STEP
    wide = HEADS_PER_STEP * HEAD
    nr = len(ride)
    grid = (bsz, n_pairs, n_blocks)

    def chunk(n, q_ref, b_scr, k_scr, v_ref, lb, st_ref, do_ref, dq_ref, dfz_ref, dv_ref, dlb_ref,
              ds_scr, dqt_acc, dkh_acc, dv_acc, dqd_acc, dkd_acc, ad_scr):
        c = _hgrn_common(q_ref, b_scr, k_scr)
        yield
        q, k = q_ref[...], c["k"]
        vb = v_ref[...].astype(BF16)
        dob = do_ref[...].astype(BF16)
        s0_t = st_ref[...]
        ds1_t = ds_scr[...]
        e_ref = _expand([jnp.exp(r) for r in c["bref"]])
        e_end = _expand([jnp.exp(c["bl"] - r) for r in c["bend"]])
        e_bl = jnp.exp(c["bl"])
        q_state = c["qt"] * e_ref
        k_state = c["kh"] * e_end
        dq_state = _dot(dob, s0_t.astype(BF16))
        dk_state = _dot(vb, ds1_t.astype(BF16))
        dv_state = _dot_nt(k_state.astype(BF16), ds1_t.astype(BF16))
        ds_scr[...] = ds1_t * e_bl + _dot_tn(dob, q_state.astype(BF16))
        js = range(N_SUB - 1)
        lo = [slice(SUB * j, SUB * (j + 1)) for j in js]
        khb = c["kh"].astype(BF16)
        dj = [jnp.exp(c["refrow"][SUB * (j + 1):] - c["bend"][j]) for j in js]
        lhs = [(c["qt"][SUB * (j + 1):] * dj[j]).astype(BF16) for j in js]
        yield
        a_js = [_dot_nt(lhs[j], khb[lo[j]]) for j in js]
        da_js = [_dot_nt(dob[SUB * (j + 1):], vb[lo[j]]).astype(BF16) for j in js]
        dqt_acc[...] = dq_state * e_ref
        dkh_acc[...] = dk_state * e_end
        dv_acc[...] = dv_state
        dbl = (jnp.sum(s0_t * ds1_t, axis=0, keepdims=True) * e_bl
               + jnp.sum(k_state * dk_state, axis=0, keepdims=True))
        yield
        dv_js = [_dot_tn(a_js[j].astype(BF16), dob[SUB * (j + 1):]) for j in js]
        dq_js = [_dot(da_js[j], khb[lo[j]]) * dj[j] for j in js]
        dk_js = [_dot_tn(da_js[j], lhs[j]) for j in js]
        yield
        for j in js:
            dv_acc[lo[j], :] += dv_js[j]
            dqt_acc[SUB * (j + 1):, :] += dq_js[j]
            dkh_acc[lo[j], :] += dk_js[j]
        not_before = [_iota2((HALF, HEAD), 0) >= row for row in range(HALF)]
        is_row = [_iota2((HALF, HEAD), 0) == row for row in range(HALF)]
        lane = _iota2((HALF, HEAD), 1)
        for i in range(N_SUB):
            r0 = SUB * i
            q_h = [q_ref[pl.ds(r0 + HALF * u, HALF), :] for u in range(N_HALF)]
            b_h = [b_scr[pl.ds(r0 + HALF * u, HALF), :] for u in range(N_HALF)]
            do_h = [do_ref[pl.ds(r0 + HALF * u, HALF), :] for u in range(N_HALF)]
            zero = jnp.zeros((HALF, HEAD), F32)
            dq_h, dk_h, a_h = [zero] * N_HALF, [zero] * N_HALF, [zero] * N_HALF
            for s in range(SUB):
                brow = b_scr[pl.ds(r0 + s, 1), :]
                krow = k_scr[pl.ds(r0 + s, 1), :]
                vrow = v_ref[pl.ds(r0 + s, 1), :]
                dk_row = jnp.zeros((1, HEAD), F32)
                is_lane = lane == r0 + s
                for u in range(s // HALF, N_HALF):
                    diff = b_h[u] - brow
                    if u == s // HALF:
                        diff = jnp.where(not_before[s - HALF * u], diff, NEG_BIG)
                    w = jnp.exp(diff)
                    qw = q_h[u] * w
                    a_col = jnp.sum(qw * krow, axis=-1, keepdims=True)
                    da_col = jnp.sum(do_h[u] * vrow, axis=-1, keepdims=True)
                    a_h[u] = jnp.where(is_lane, a_col, a_h[u])
                    dq_h[u] = dq_h[u] + da_col * (w * krow)
                    dk_row = dk_row + jnp.sum(da_col * qw, axis=0, keepdims=True)
                us = s // HALF
                dk_h[us] = jnp.where(is_row[s - HALF * us], dk_row, dk_h[us])
            for u in range(N_HALF):
                rows = pl.ds(r0 + HALF * u, HALF)
                dqd_acc[rows, :] = dq_h[u]
                dkd_acc[rows, :] = dk_h[u]
                ad_scr[rows, :] = a_h[u]
            yield
        dv_in = _dot_tn(ad_scr[...].astype(BF16), dob)
        dq = dqt_acc[...] * c["e_q"] + dqd_acc[...]
        dk = dkh_acc[...] * c["e_k"] + dkd_acc[...]
        rr, cc = _iota2((BLOCK, BLOCK), 0), _iota2((BLOCK, BLOCK), 1)
        db = q * dq - k * dk + jnp.where(_iota2((BLOCK, HEAD), 0) == BLOCK - 1, dbl, 0.0)
        yield
        dg = _tri_left(_tri(cc >= rr), db)
        yield
        real = jnp.logical_or(n > 0, _iota2((BLOCK, HEAD), 0) >= N_PAD)
        df = jnp.where(real, dg / (1.0 - k) - dk, 0.0)
        nsig = k * (1.0 / (1.0 - lb))
        dq_ref[...] = dq.astype(BF16)
        dv_ref[...] = (dv_acc[...] + dv_in).astype(BF16)
        dfz_ref[...] = (df * k * (1.0 - nsig)).astype(BF16)
        dlb_ref[...] += jnp.sum(df * nsig, axis=0, keepdims=True)

    def body(*refs):
        q_ref, b_ref, k_ref, v_ref, lb_ref, st_ref, do_ref = refs[:7]
        dq_ref, dfz_ref, dv_ref, dlb_ref = refs[7 + nr:11 + nr]
        scratch = refs[11 + 2 * nr:18 + 2 * nr]
        finish_ride = _ride_along(refs[7:7 + nr], refs[11 + nr:11 + 2 * nr], ride_modes, refs[18 + 2 * nr:], grid)
        step = pl.program_id(2)

        @pl.when(step == 0)
        def _():
            scratch[0][...] = jnp.zeros_like(scratch[0])
            dlb_ref[...] = jnp.zeros_like(dlb_ref)

        def head(hh):
            cols = pl.ds(hh * HEAD, HEAD)
            return chunk(n_blocks - 1 - step, q_ref.at[:, cols], b_ref.at[:, cols], k_ref.at[:, cols],
                         v_ref.at[:, cols], lb_ref[:, cols], st_ref.at[hh], do_ref.at[:, cols], dq_ref.at[:, cols],
                         dfz_ref.at[:, cols], dv_ref.at[:, cols], dlb_ref.at[:, cols],
                         *[scr.at[hh] for scr in scratch])

        _lockstep([head(hh) for hh in range(HEADS_PER_STEP)])
        finish_ride()

    def blk(col0):
        return pl.BlockSpec((BLOCK, wide), lambda b, h, s: (b * n_blocks + n_blocks - 1 - s, col0 + h))

    any_spec = pl.BlockSpec(memory_space=pl.ANY)
    per_head = pltpu.VMEM((HEADS_PER_STEP, BLOCK, HEAD), F32)
    return _pcall(
        body, name=name, grid=grid,
        in_specs=[blk(0), blk(0), blk(0), blk(2 * n_pairs), pl.BlockSpec((1, wide), lambda b, h, s: (0, h)),
                  pl.BlockSpec((None, HEADS_PER_STEP, HEAD, HEAD),
                               lambda b, h, s: (b * n_blocks + n_blocks - 1 - s, h, 0, 0)),
                  blk(0)] + [any_spec] * nr,
        out_specs=(blk(0), blk(0), blk(0), pl.BlockSpec((None, 1, wide), lambda b, h, s: (b, 0, h)))
        + (any_spec,) * nr,
        out_shape=(jax.ShapeDtypeStruct((t, d), BF16),) * 3 + (jax.ShapeDtypeStruct((bsz, 1, d), F32),)
        + _exchange_shapes(ride, ride_modes),
        scratch_shapes=[per_head] * 7 + _exchange_sems(nr),
        compiler_params=_params("arbitrary", "arbitrary", "arbitrary"),
    )(main, b_all, k_all, main, lbrow, states, do, *ride)


SB_GROUP = 3
SB_ROWS = SB_GROUP * BLOCK
SB_DEAD = -104.0
SB_NEAR = ((0, 0), (1, 0), (0, 1))
SB_HEADS_FWD = 4
SB_HEADS_BWD = 2
SB_STATIC_OWN = tuple((r, kb) for kb in range(SB_GROUP - 1, -1, -1) for r in range(kb, SB_GROUP))
SB_STATIC_NEAR = tuple((r, -1 - back) for r, back in SB_NEAR)
N_STATIC = len(SB_STATIC_OWN) + len(SB_STATIC_NEAR)


def _sb_tables():
    j = jnp.bitwise_and(_iota2((2 * BLOCK, 2 * BLOCK), 0), BLOCK - 1)
    s = _iota2((2 * BLOCK, 2 * BLOCK), 1)
    ones = s >= BLOCK
    return (_tri(jnp.logical_or(ones, j > s)), _tri(jnp.logical_or(ones, j <= s)),
            _tri(jnp.logical_or(ones, j < s)))


def _sums(x, table, pieces=2):
    hi = x.astype(BF16)
    if pieces == 1:
        r = _dot(hi, table[:BLOCK])
    else:
        lo = (x - hi.astype(F32)).astype(BF16)
        r = _dot(jnp.concatenate([hi, lo], axis=1), table)
    return r[:, :BLOCK], r[:, BLOCK:]


def _sb_logits(q, ks, scale, causal, pad_row):
    z = _dot_nt(q, ks) * scale
    log_keep = -(jnp.maximum(z, 0.0) + jnp.log(1.0 + jnp.exp(-jnp.abs(z))))
    log_beta = z + log_keep
    if causal is not None:
        log_keep = jnp.where(causal, log_keep, 0.0)
    if pad_row is not None:
        log_keep = log_keep * pad_row
    return z, log_beta, log_keep


def _sb_fwd(qkv, bsz, n_blocks, name):
    t, d3 = qkv.shape
    d = d3 // 3
    n_heads = d // HEAD
    lp = n_blocks * BLOCK
    n_groups = n_blocks // SB_GROUP
    assert n_groups * SB_GROUP == n_blocks
    scale = HEAD ** -0.5

    n_pairs = n_heads // SB_HEADS_FWD
    wide = SB_HEADS_FWD * HEAD

    def body(q_ref, k_ref, v_ref, upper_ref, o_ref, tot_ref, stop_ref, w_keep, lb_keep, c_scr):
        g = pl.program_id(2)
        upper = upper_ref[...]
        causal = _iota2((BLOCK, BLOCK), 1) < _iota2((BLOCK, BLOCK), 0)
        lane = _iota2((1, BLOCK), 1)
        o_ref[...] = jnp.zeros_like(o_ref)
        c_scr[...] = jnp.zeros_like(c_scr)

        def tiles(jobs, keep=None):
            cols = [pl.ds(hh * HEAD, HEAD) for hh, _, _, _, _ in jobs]
            qrows = [pl.ds(r * BLOCK, BLOCK) for _, r, _, _, _ in jobs]
            krows = [pl.ds(pl.multiple_of(m * BLOCK, BLOCK), BLOCK) for _, _, m, _, _ in jobs]
            n_jobs = range(len(jobs))
            lg = [_sb_logits(q_ref[qrows[i], cols[i]], k_ref[krows[i], cols[i]], scale,
                             causal if jobs[i][3] else None, jobs[i][4]) for i in n_jobs]
            sm = [_sums(x[2], upper) for x in lg]
            a_all = []
            for i in n_jobs:
                c = c_scr[qrows[i], cols[i]]
                a = jnp.exp(lg[i][1] + c + sm[i][0])
                a_all.append((jnp.where(causal, a, 0.0) if jobs[i][3] else a).astype(BF16))
                c_scr[qrows[i], cols[i]] = c + sm[i][1]
            out = [_dot(a_all[i], v_ref[krows[i], cols[i]]) for i in n_jobs]
            for i in n_jobs:
                o_ref[qrows[i], cols[i]] += out[i]
                if keep is not None:
                    w_keep[jobs[i][0], keep[i]] = a_all[i]
                    lb_keep[jobs[i][0], keep[i]] = lg[i][1].astype(BF16)

        def pad_row_of(m):
            return jnp.where(jnp.logical_or(m > 0, lane >= N_PAD), 1.0, 0.0)

        base = SB_GROUP * g
        heads = range(SB_HEADS_FWD)
        own = [(hh, r, base + rel, r == rel, pad_row_of(base) if rel == 0 else None)
               for hh in heads for r, rel in SB_STATIC_OWN]
        near = [(hh, r, base + rel, False, None) for hh in heads for r, rel in SB_STATIC_NEAR]
        own_slots = [i for _ in heads for i in range(len(SB_STATIC_OWN))]
        near_slots = [len(SB_STATIC_OWN) + i for _ in heads for i in range(len(SB_STATIC_NEAR))]
        pl.when(g == 0)(lambda: tiles(own, keep=own_slots))
        pl.when(g > 0)(lambda: tiles(own + near, keep=own_slots + near_slots))

        stop_ref[...] = jnp.zeros_like(stop_ref)
        mine = [(hh, r) for hh in heads for r in range(SB_GROUP)]
        reach = [jnp.max(c_scr[pl.ds(r * BLOCK, BLOCK), pl.ds(hh * HEAD, HEAD)]) for hh, r in mine]
        for (hh, r), reach_now in zip(mine, reach):
            def live(carry):
                m, c_max = carry
                return jnp.logical_and(m >= 0, c_max >= SB_DEAD)

            def step(carry, hh=hh, r=r):
                m, _ = carry
                tiles([(hh, r, m, False, pad_row_of(m))])
                return m - 1, jnp.max(c_scr[pl.ds(r * BLOCK, BLOCK), pl.ds(hh * HEAD, HEAD)])

            lowest = jnp.maximum(base - (SB_GROUP - 1 - r), 0)
            m_end, _ = lax.while_loop(live, step, (lowest - 1, reach_now))
            stop_ref[hh, pl.ds(r, 1), :] = jnp.broadcast_to((m_end + 1).astype(F32), (1, 128))
        tot_ref[...] = c_scr[...]

    qblk = pl.BlockSpec((SB_ROWS, wide), lambda b, h, g: (b * n_groups + g, h))
    table = pl.BlockSpec((2 * BLOCK, 2 * BLOCK), lambda b, h, g: (0, 0))
    stops = pl.BlockSpec((None, None, SB_HEADS_FWD, 8, 128), lambda b, h, g: (b, g, h, 0, 0))
    kept = pl.BlockSpec((None, None, SB_HEADS_FWD, N_STATIC, BLOCK, BLOCK), lambda b, h, g: (b, g, h, 0, 0, 0))
    return _pcall(
        body, name=name, grid=(bsz, n_pairs, n_groups),
        in_specs=[qblk, pl.BlockSpec((lp, wide), lambda b, h, g: (b, n_pairs + h)),
                  pl.BlockSpec((lp, wide), lambda b, h, g: (b, 2 * n_pairs + h)), table],
        out_specs=(qblk, qblk, stops, kept, kept),
        out_shape=(jax.ShapeDtypeStruct((t, d), F32), jax.ShapeDtypeStruct((t, d), F32),
                   jax.ShapeDtypeStruct((bsz, n_groups, n_heads, 8, 128), F32))
        + (jax.ShapeDtypeStruct((bsz, n_groups, n_heads, N_STATIC, BLOCK, BLOCK), BF16),) * 2,
        scratch_shapes=[pltpu.VMEM((SB_ROWS, wide), F32)],
        compiler_params=_params("parallel", "parallel", "arbitrary"),
    )(qkv, qkv, qkv, _sb_tables()[0])


def _sb_bwd(qkv, do, tot, stop, w_kept, lb_kept, bsz, n_blocks, name):
    t, d3 = qkv.shape
    d = d3 // 3
    n_heads = d // HEAD
    lp = n_blocks * BLOCK
    n_groups = n_blocks // SB_GROUP
    scale = HEAD ** -0.5

    def body(q_ref, k_ref, v_ref, do_ref, tot_ref, stop_ref, incl_ref, excl_ref, w_kept, lb_kept,
             dq_ref, dk_ref, dv_ref, dk_acc, dv_acc, dq_acc, p_scr, e_scr, dob_scr):
        g = pl.program_id(2)

        @pl.when(g == 0)
        def _():
            dk_acc[...] = jnp.zeros_like(dk_acc)
            dv_acc[...] = jnp.zeros_like(dv_acc)

        incl, excl = incl_ref[...], excl_ref[...]
        causal = _iota2((BLOCK, BLOCK), 1) < _iota2((BLOCK, BLOCK), 0)
        lane = _iota2((1, BLOCK), 1)
        dob_scr[...] = do_ref[...].astype(BF16)
        dq_acc[...] = jnp.zeros_like(dq_acc)
        p_scr[...] = jnp.zeros_like(p_scr)
        e_scr[...] = jnp.zeros_like(e_scr)

        def tiles(jobs, slots=None):
            n_jobs = range(len(jobs))
            cols = [pl.ds(hh * HEAD, HEAD) for hh, _, _, _, _ in jobs]
            qrows = [pl.ds(r * BLOCK, BLOCK) for _, r, _, _, _ in jobs]
            krows = [pl.ds(pl.multiple_of(m * BLOCK, BLOCK), BLOCK) for _, _, m, _, _ in jobs]
            diag = [dg for _, _, _, dg, _ in jobs]
            d_a = [_dot_nt(dob_scr[qrows[i], cols[i]], v_ref[krows[i], cols[i]]) for i in n_jobs]
            if slots is None:
                lg = [_sb_logits(q_ref[qrows[i], cols[i]], k_ref[krows[i], cols[i]], scale,
                                 causal if diag[i] else None, jobs[i][4]) for i in n_jobs]
                log_beta = [x[1] for x in lg]
                sm = [_sums(lg[i][2], incl) for i in n_jobs]
                a_all = []
                for i in n_jobs:
                    p = p_scr[qrows[i], cols[i]]
                    a = jnp.exp(log_beta[i] + (tot_ref[qrows[i], cols[i]] - p - sm[i][0]))
                    a_all.append((jnp.where(causal, a, 0.0) if diag[i] else a).astype(BF16))
                    p_scr[qrows[i], cols[i]] = p + sm[i][1]
            else:
                a_all = [w_kept[hh, s] for hh, s in slots]
                log_beta = [lb_kept[hh, s].astype(F32) for hh, s in slots]
            gr = [d_a[i] * a_all[i].astype(F32) for i in n_jobs]
            dv_part = [_dot_tn(a_all[i], dob_scr[qrows[i], cols[i]]) for i in n_jobs]
            gs = [_sums(gr[i], excl, pieces=1) for i in n_jobs]
            dz_all = []
            for i in n_jobs:
                e = e_scr[qrows[i], cols[i]]
                dz = gr[i] - (gr[i] + e + gs[i][0]) * jnp.exp(log_beta[i])
                if diag[i]:
                    dz = jnp.where(causal, dz, 0.0)
                dz_all.append((dz * scale).astype(BF16))
                e_scr[qrows[i], cols[i]] = e + gs[i][1]
            dk_part = [_dot_tn(dz_all[i], q_ref[qrows[i], cols[i]]) for i in n_jobs]
            dq_part = [_dot(dz_all[i], k_ref[krows[i], cols[i]]) for i in n_jobs]
            for i in n_jobs:
                dv_acc[krows[i], cols[i]] += dv_part[i]
                dk_acc[krows[i], cols[i]] += dk_part[i]
                dq_acc[qrows[i], cols[i]] += dq_part[i]

        def pad_row_of(m):
            return jnp.where(jnp.logical_or(m > 0, lane >= N_PAD), 1.0, 0.0)

        base = SB_GROUP * g
        heads = range(SB_HEADS_BWD)
        mine = [(hh, r) for hh in heads for r in range(SB_GROUP)]
        stops = [jnp.max(stop_ref[hh, pl.ds(r, 1), :]).astype(jnp.int32) for hh, r in mine]
        for (hh, r), stop_now in zip(mine, stops):
            lowest = jnp.maximum(base - (SB_GROUP - 1 - r), 0)

            def step(m, hh=hh, r=r):
                tiles([(hh, r, m, False, pad_row_of(m))])
                return m + 1

            lax.while_loop(lambda m, lowest=lowest: m < lowest, step, jnp.clip(stop_now, 0, lowest))
        slot_of = {tile: i for i, tile in enumerate(SB_STATIC_OWN + SB_STATIC_NEAR)}
        upwards = lambda tile: (tile[1], tile[0])
        own = [(hh,) + tile for hh in heads for tile in sorted(SB_STATIC_OWN, key=upwards)]
        near = [(hh,) + tile for hh in heads for tile in sorted(SB_STATIC_NEAR, key=upwards)]

        def static_batch(order):
            tiles([(hh, r, base + rel, r == rel, None) for hh, r, rel in order],
                  slots=[(hh, slot_of[(r, rel)]) for hh, r, rel in order])

        pl.when(g == 0)(lambda: static_batch(own))
        pl.when(g > 0)(lambda: static_batch(near + own))
        dq_ref[...] = dq_acc[...].astype(BF16)

        @pl.when(g == n_groups - 1)
        def _():
            dk_ref[...] = dk_acc[...].astype(BF16)
            dv_ref[...] = dv_acc[...].astype(BF16)

    n_pairs = n_heads // SB_HEADS_BWD
    wide = SB_HEADS_BWD * HEAD
    qblk = pl.BlockSpec((SB_ROWS, wide), lambda b, h, g: (b * n_groups + g, h))
    kblk = pl.BlockSpec((lp, wide), lambda b, h, g: (b, n_pairs + h))
    vblk = pl.BlockSpec((lp, wide), lambda b, h, g: (b, 2 * n_pairs + h))
    hblk = pl.BlockSpec((lp, wide), lambda b, h, g: (b, h))
    sblk = pl.BlockSpec((None, None, SB_HEADS_BWD, 8, 128), lambda b, h, g: (b, g, h, 0, 0))
    table = pl.BlockSpec((2 * BLOCK, 2 * BLOCK), lambda b, h, g: (0, 0))
    kept = pl.BlockSpec((None, None, SB_HEADS_BWD, N_STATIC, BLOCK, BLOCK), lambda b, h, g: (b, g, h, 0, 0, 0))
    _, incl, excl = _sb_tables()
    return _pcall(
        body, name=name, grid=(bsz, n_pairs, n_groups),
        in_specs=[qblk, kblk, vblk, qblk, qblk, sblk, table, table, kept, kept],
        out_specs=(qblk, hblk, hblk),
        out_shape=(jax.ShapeDtypeStruct((t, d), BF16),) * 3,
        scratch_shapes=[pltpu.VMEM((lp, wide), F32), pltpu.VMEM((lp, wide), F32)]
        + [pltpu.VMEM((SB_ROWS, wide), F32)] * 3 + [pltpu.VMEM((SB_ROWS, wide), BF16)],
        compiler_params=_params("parallel", "parallel", "arbitrary"),
    )(qkv, qkv, qkv, do, tot, stop, incl, excl, w_kept, lb_kept)


def _adamw(w, g, m, v):
    m = ADAM_B1 * m + (1.0 - ADAM_B1) * g
    v = ADAM_B2 * v + (1.0 - ADAM_B2) * (g * g)
    m_hat = m / (1.0 - ADAM_B1 ** ADAM_STEP)
    v_hat = v / (1.0 - ADAM_B2 ** ADAM_STEP)
    delta = -ADAM_LR * (m_hat / (jnp.sqrt(v_hat) + ADAM_EPS) + ADAM_WD * w)
    return delta, m, v


def _update_sharded(w, parts, m, v, name):
    r, c = w.shape
    tr = UPDATE_ROWS if r % UPDATE_ROWS == 0 else r

    def body(w_ref, p_ref, m_ref, v_ref, g_ref, d_ref, nm_ref, nv_ref):
        g = p_ref[0].astype(F32)
        for q in range(1, N_DEV):
            g = g + p_ref[q].astype(F32)
        g_ref[...] = g
        d_ref[...], nm_ref[...], nv_ref[...] = _adamw(w_ref[...], g, m_ref[...], v_ref[...])

    row = pl.BlockSpec((tr, c), lambda i: (i, 0))
    return _pcall(
        body, name=name, grid=(r // tr,),
        in_specs=[row, pl.BlockSpec((N_DEV, tr, c), lambda i: (0, i, 0)), row, row],
        out_specs=(row,) * 4, out_shape=(jax.ShapeDtypeStruct((r, c), F32),) * 4,
        compiler_params=_params("parallel"),
    )(w, parts, m, v)


SMALL_ROWS = 8


def _pack_small(dpre0, dpre1, dpost0, dpost1, dlb, dwon, loss, name):
    d = dpre0.shape[1]
    bsz = dlb.shape[0]

    def body(a0, a1, p0, p1, lb_ref, on_ref, loss_ref, out_ref):
        out_ref[...] = jnp.zeros_like(out_ref)
        out_ref[pl.ds(0, 1), :] = a0[...]
        out_ref[pl.ds(1, 1), :] = a1[...]
        out_ref[pl.ds(2, 1), :] = p0[...]
        out_ref[pl.ds(3, 1), :] = p1[...]
        acc = lb_ref[0]
        for b in range(1, bsz):
            acc = acc + lb_ref[b]
        out_ref[pl.ds(4, 1), :] = acc
        out_ref[pl.ds(5, 1), pl.ds(0, HEAD)] = on_ref[...]
        out_ref[pl.ds(6, 1), pl.ds(0, HEAD)] = loss_ref[pl.ds(0, 1), :]

    return _pcall(body, name=name, out_shape=jax.ShapeDtypeStruct((SMALL_ROWS, d), F32))(
        dpre0, dpre1, dpost0, dpost1, dlb, dwon, loss)


def _update_small(parts, pre, post, lbw, on, moments, name):
    d = pre.shape[1]

    def body(p_ref, pre_ref, post_ref, lbw_ref, on_ref, mpre, mpost, mlb, mon, vpre, vpost, vlb, von,
             loss_ref, *outs):
        def total(r0, nr, width):
            acc = p_ref[0, pl.ds(r0, nr), pl.ds(0, width)]
            for q in range(1, N_DEV):
                acc = acc + p_ref[q, pl.ds(r0, nr), pl.ds(0, width)]
            return acc

        def put(k, w, g, m, v):
            dl, nm, nv = _adamw(w, g, m, v)
            outs[4 * k][...] = g
            outs[4 * k + 1][...] = dl
            outs[4 * k + 2][...] = nm
            outs[4 * k + 3][...] = nv

        put(0, pre_ref[...], total(0, 2, d), mpre[...], vpre[...])
        put(1, post_ref[...], total(2, 2, d), mpost[...], vpost[...])
        a0, a1 = lbw_ref[pl.ds(0, 1), :], lbw_ref[pl.ds(1, 1), :]
        mx = jnp.maximum(a0, a1)
        e0, e1 = jnp.exp(a0 - mx), jnp.exp(a1 - mx)
        p0 = e0 / (e0 + e1)
        g0 = total(4, 1, d) * p0 * (1.0 - p0)
        for r, w, g in ((0, a0, g0), (1, a1, -g0)):
            row = pl.ds(r, 1)
            dl, nm, nv = _adamw(w, g, mlb[row, :], vlb[row, :])
            outs[8][row, :] = g
            outs[9][row, :] = dl
            outs[10][row, :] = nm
            outs[11][row, :] = nv
        put(3, on_ref[...], total(5, 1, HEAD), mon[...], von[...])
        loss_ref[...] = jnp.broadcast_to(total(6, 1, HEAD), loss_ref.shape)

    shapes = []
    for w in (pre, post, lbw, on):
        shapes += [jax.ShapeDtypeStruct(w.shape, F32)] * 4
    return _pcall(body, name=name, out_shape=(jax.ShapeDtypeStruct((8, 128), F32), *shapes))(
        parts, pre, post, lbw, on, *moments)


def kernel(x, meta_tokens, pre_norm, post_norm, hgrn_w_in, hgrn_lb, hgrn_out_norm, hgrn_w_out, sb_w_in, sb_w_out, loss_target, m_meta_tokens, m_pre_norm, m_post_norm, m_hgrn_w_in, m_hgrn_lb, m_hgrn_out_norm, m_hgrn_w_out, m_sb_w_in, m_sb_w_out, v_meta_tokens, v_pre_norm, v_post_norm, v_hgrn_w_in, v_hgrn_lb, v_hgrn_out_norm, v_hgrn_w_out, v_sb_w_in, v_sb_w_out):
    bsz, seq, d = x.shape
    n_blocks = seq // BLOCK + 1
    lp = n_blocks * BLOCK
    s = hgrn_w_in.shape[2]
    dsh = d // N_DEV

    w_in_h, meta_all = _gather_once_per_chip([hgrn_w_in[0].astype(BF16), meta_tokens], "gather_weights")
    meta_full = jnp.transpose(meta_all, (1, 0, 2)).reshape(N_META, d)

    lbrow = jnp.cumsum(jax.nn.softmax(hgrn_lb, axis=0), axis=0)[0:1]

    main0, gate0, yn0, h0 = _norm_inproj(None, pre_norm[0:1], w_in_h, F32, "inproj_hgrn", from_x=(x, meta_full))
    o0, states, decay0, k0, w_in_s, w_out_s, w_out_h = _hgrn_fwd(
        main0, lbrow, bsz, n_blocks,
        [sb_w_in[0].astype(BF16), sb_w_out[0].astype(BF16), hgrn_w_out[0].astype(BF16)], ["gather"] * 3, "hgrn_fwd")
    w_out_s = w_out_s.reshape(d, d)
    w_out_h = w_out_h.reshape(d, d)
    h1, y0 = _mix_out(o0, gate0, h0, w_out_h, hgrn_out_norm, post_norm[0:1], True, "mix_out_hgrn")
    main1, gate1, yn1 = _norm_inproj(h1, pre_norm[1:2], w_in_s, BF16, "inproj_sb")
    o1, tot, stop, w_kept, lb_kept = _sb_fwd(main1, bsz, n_blocks, "sb_fwd")
    dh2, y1, loss_part = _mix_out(o1, gate1, h1, w_out_s, hgrn_out_norm, post_norm[1:2], False, "mix_out_sb",
                                  target=loss_target)

    do1, dgate1, dw_out_s, dpost1, _ = _mix_out_bwd(
        dh2, y1, o1, gate1, w_out_s, hgrn_out_norm, post_norm[1:2], False, "mix_out_sb_bwd")
    dq1, dk1, dv1 = _sb_bwd(main1, do1, tot, stop, w_kept, lb_kept, bsz, n_blocks, "sb_bwd")
    dproj1 = (dq1, dk1, dv1, dgate1)
    dh1, dpre1 = _inproj_bwd_x(dproj1, w_in_s, h1, pre_norm[1:2], dh2, [], [], "inproj_sb_bwd_x")
    dw_in_s = _inproj_bwd_w(yn1, dproj1, s, "inproj_sb_bwd_w")

    do0, dgate0, dw_out_h, dpost0, dwon = _mix_out_bwd(
        dh1, y0, o0, gate0, w_out_h, hgrn_out_norm, post_norm[0:1], True, "mix_out_hgrn_bwd")
    dq0, dfz0, dv0, dlb, p_in_s, p_out_s, p_out_h = _hgrn_bwd(
        main0, decay0, k0, lbrow, states, do0, bsz, n_blocks,
        [dw_in_s, dw_out_s.reshape(N_DEV, dsh, d), dw_out_h.reshape(N_DEV, dsh, d)], ["scatter"] * 3, "hgrn_bwd")
    dproj0 = (dq0, dfz0, dv0, dgate0)
    dw_in_h = _inproj_bwd_w(yn0, dproj0, s, "inproj_hgrn_bwd_w")
    dh0, dpre0, p_in_h = _inproj_bwd_x(
        dproj0, w_in_h, h0, pre_norm[0:1], dh1, [dw_in_h], ["scatter"], "inproj_hgrn_bwd_x")

    dh0 = dh0.reshape(bsz, lp, d)
    grad_x = dh0[:, BLOCK:]
    dmeta = jnp.sum(dh0[:, N_PAD:BLOCK], axis=0)
    dmeta = jnp.transpose(dmeta.reshape(N_META, N_DEV, dsh), (1, 0, 2))
    small = _pack_small(dpre0, dpre1, dpost0, dpost1, dlb, dwon, loss_part, "pack_small")

    p_meta, p_small = _exchange([dmeta, small], ["scatter", "gather"], "exchange_grads")

    u_meta = _update_sharded(meta_tokens, p_meta, m_meta_tokens, v_meta_tokens, "update_meta")
    u_in_h = _update_sharded(hgrn_w_in[0], p_in_h, m_hgrn_w_in[0], v_hgrn_w_in[0], "update_hgrn_w_in")
    u_out_h = _update_sharded(hgrn_w_out[0], p_out_h, m_hgrn_w_out[0], v_hgrn_w_out[0], "update_hgrn_w_out")
    u_in_s = _update_sharded(sb_w_in[0], p_in_s, m_sb_w_in[0], v_sb_w_in[0], "update_sb_w_in")
    u_out_s = _update_sharded(sb_w_out[0], p_out_s, m_sb_w_out[0], v_sb_w_out[0], "update_sb_w_out")
    sm = _update_small(p_small, pre_norm, post_norm, hgrn_lb, hgrn_out_norm,
                       (m_pre_norm, m_post_norm, m_hgrn_lb, m_hgrn_out_norm,
                        v_pre_norm, v_post_norm, v_hgrn_lb, v_hgrn_out_norm), "update_small")
    loss = sm[0][0, 0]
    u_pre, u_post, u_lb, u_on = sm[1:5], sm[5:9], sm[9:13], sm[13:17]

    per_w = [u_meta, u_pre, u_post, tuple(a[None] for a in u_in_h), u_lb, u_on,
             tuple(a[None] for a in u_out_h), tuple(a[None] for a in u_in_s), tuple(a[None] for a in u_out_s)]
    outs = [loss, grad_x]
    for k in range(4):
        outs += [u[k] for u in per_w]
    return tuple(outs)
```

```python
import jax
import jax.numpy as jnp
from jax import lax
from jax.experimental import pallas as pl
from jax.experimental.pallas import tpu as pltpu

F32 = jnp.float32
BF16 = jnp.bfloat16

N_DEV = 8
BLOCK = 128
N_META = 16
N_PAD = BLOCK - N_META
HEAD = 128
SUB = 16
N_SUB = BLOCK // SUB
HALF = 8
N_HALF = SUB // HALF
EPS = 1e-6
ROW_TILE = 3 * BLOCK
UPDATE_ROWS = 256
K_TILES = (2816, 1408, 768, 384, 128)
NEG_BIG = -1e30

ADAM_LR = 0.001
ADAM_B1 = 0.9
ADAM_B2 = 0.999
ADAM_EPS = 1e-08
ADAM_WD = 0.01
ADAM_STEP = 10

VMEM_LIMIT = 56 * 1024 * 1024


def _pcall(body, **kw):
    return pl.pallas_call(body, **kw)


def _params(*sem):
    return pltpu.CompilerParams(dimension_semantics=sem, vmem_limit_bytes=VMEM_LIMIT)


def _dot(a, b):
    return jnp.dot(a, b, preferred_element_type=F32)


def _dot_nt(a, b):
    return lax.dot_general(a, b, (((1,), (1,)), ((), ())), preferred_element_type=F32)


def _dot_tn(a, b):
    return lax.dot_general(a, b, (((0,), (0,)), ((), ())), preferred_element_type=F32)


def _split(x, pieces):
    out = []
    for _ in range(pieces):
        p = x.astype(BF16)
        out.append(p)
        x = x - p.astype(F32)
    return out


def _tri_left(tri, x, pieces=3):
    return sum(_dot(tri, p) for p in _split(x, pieces))


def _iota2(shape, dim):
    return lax.broadcasted_iota(jnp.int32, shape, dim)


def _tri(cond):
    return jnp.where(cond, 1.0, 0.0).astype(BF16)


def _sig_pair(x):
    e = jnp.exp(-jnp.abs(x))
    r = 1.0 / (1.0 + e)
    er = e * r
    pos = x >= 0
    return jnp.where(pos, r, er), jnp.where(pos, er, r)


def _expand(rows):
    return jnp.concatenate([jnp.broadcast_to(r, (SUB, HEAD)) for r in rows], axis=0)


def _exchange_shapes(arrays, modes):
    return tuple(jax.ShapeDtypeStruct((N_DEV,) + tuple(a.shape[1:] if m == "scatter" else a.shape), a.dtype)
                 for a, m in zip(arrays, modes))


def _exchange_sems(n):
    if n == 0:
        return []
    return [pltpu.SemaphoreType.DMA((n, N_DEV - 1)), pltpu.SemaphoreType.DMA((n, N_DEV - 1)),
            pltpu.SemaphoreType.DMA((n,))]


def _exchange_copies(ins, outs, modes, send_sems, recv_sems, local_sems):
    mx, my, mc = lax.axis_index("x"), lax.axis_index("y"), lax.axis_index("c")
    me = 4 * mx + 2 * my + mc

    def src(i, slot):
        return ins[i].at[slot] if modes[i] == "scatter" else ins[i]

    def peer_of(mask):
        px = 1 - mx if mask & 4 else mx
        py = 1 - my if mask & 2 else my
        pc = 1 - mc if mask & 1 else mc
        return px, py, pc

    def copy(i, mask, dst_slot):
        px, py, pc = peer_of(mask)
        return pltpu.make_async_remote_copy(
            src_ref=src(i, 4 * px + 2 * py + pc), dst_ref=outs[i].at[dst_slot],
            send_sem=send_sems.at[i, mask - 1], recv_sem=recv_sems.at[i, mask - 1],
            device_id=(px, py, pc), device_id_type=pl.DeviceIdType.MESH)

    n = len(ins)
    sends = [copy(i, mask, me) for mask in range(1, N_DEV) for i in range(n)]
    own = [pltpu.make_async_copy(src(i, me), outs[i].at[me], local_sems.at[i]) for i in range(n)]
    arrivals = []
    for mask in range(1, N_DEV):
        px, py, pc = peer_of(mask)
        arrivals += [copy(i, mask, 4 * px + 2 * py + pc) for i in range(n)]

    def start():
        for cp in sends + own:
            cp.start()

    def wait():
        for cp in arrivals:
            cp.wait_recv()
        for cp in sends:
            cp.wait_send()
        for cp in own:
            cp.wait()

    return start, wait


def _gather_once_per_chip(arrays, name):
    n = len(arrays)

    def body(*refs):
        ins, outs = refs[:n], refs[n:2 * n]
        send_sems, recv_sems, local_sems = refs[2 * n:]
        mx, my, mc = lax.axis_index("x"), lax.axis_index("y"), lax.axis_index("c")
        me, sibling = (mx, my, mc), (mx, my, 1 - mc)
        chips = [(1 - mx, my), (mx, 1 - my), (1 - mx, 1 - my)]

        def slot(px, py, pc):
            return 4 * px + 2 * py + pc

        def copy(i, k, block, to, src=None):
            return pltpu.make_async_remote_copy(
                src_ref=outs[i].at[slot(*block)] if src is None else src, dst_ref=outs[i].at[slot(*block)],
                send_sem=send_sems.at[i, k], recv_sem=recv_sems.at[i, k],
                device_id=to, device_id_type=pl.DeviceIdType.MESH)

        own = [pltpu.make_async_copy(ins[i], outs[i].at[slot(*me)], local_sems.at[i]) for i in range(n)]
        first = [copy(i, 0, me, sibling, src=ins[i]) for i in range(n)]
        first += [copy(i, 1 + j, me, (*chip, mc), src=ins[i]) for j, chip in enumerate(chips) for i in range(n)]
        for cp in own + first:
            cp.start()
        passed = []
        for j, chip in enumerate(chips):
            for i in range(n):
                copy(i, 1 + j, (*chip, mc), me).wait_recv()
                cp = copy(i, 4 + j, (*chip, mc), sibling)
                cp.start()
                passed.append(cp)
        for i in range(n):
            copy(i, 0, sibling, me).wait_recv()
            for j, chip in enumerate(chips):
                copy(i, 4 + j, (*chip, 1 - mc), me).wait_recv()
        for cp in first + passed:
            cp.wait_send()
        for cp in own:
            cp.wait()

    any_spec = pl.BlockSpec(memory_space=pl.ANY)
    return _pcall(
        body, name=name, out_shape=_exchange_shapes(arrays, ["gather"] * n),
        in_specs=[any_spec] * n, out_specs=tuple([any_spec] * n),
        scratch_shapes=_exchange_sems(n),
    )(*arrays)


def _exchange(arrays, modes, name):
    n = len(arrays)

    def body(*refs):
        start, wait = _exchange_copies(refs[:n], refs[n:2 * n], modes, *refs[2 * n:])
        start()
        wait()

    any_spec = pl.BlockSpec(memory_space=pl.ANY)
    return _pcall(
        body, name=name, out_shape=_exchange_shapes(arrays, modes),
        in_specs=[any_spec] * n, out_specs=tuple([any_spec] * n),
        scratch_shapes=_exchange_sems(n),
    )(*arrays)


def _tile_blocks_of_x(seq, d):
    assert ROW_TILE == 3 * BLOCK and (seq + BLOCK) % ROW_TILE == 0
    per_seq = (seq + BLOCK) // ROW_TILE

    def spec(k):
        return pl.BlockSpec((None, BLOCK, d),
                            lambda i: (i // per_seq, jnp.maximum(3 * (i % per_seq) - 1 + k, 0), 0))

    return [spec(0), spec(1), spec(2)]


def _norm_inproj(h, wnorm, w_all, main_dtype, name, from_x=None):
    p_n, d, s = w_all.shape
    n_main = 3 * d // s
    tm = ROW_TILE
    if from_x is None:
        t = h.shape[0]
        lead, lead_specs = [h], [pl.BlockSpec((tm, d), lambda i: (i, 0))]
    else:
        x_in, meta = from_x
        t = x_in.shape[0] * (x_in.shape[1] + BLOCK)
        tiles_per_seq = (x_in.shape[1] + BLOCK) // tm
        lead = [x_in, x_in, x_in, meta]
        lead_specs = _tile_blocks_of_x(x_in.shape[1], d) + [pl.BlockSpec((N_META, d), lambda i: (0, 0))]
    n_lead = len(lead)

    def body(*refs):
        wn_ref, w_ref = refs[n_lead:n_lead + 2]
        main_ref, gate_ref, ynt_ref = refs[n_lead + 2:n_lead + 5]
        if from_x is None:
            x = refs[0][...]
        else:
            first_tile = pl.program_id(0) % tiles_per_seq == 0
            meta_block = jnp.concatenate([jnp.zeros((N_PAD, d), F32), refs[3][...]], axis=0)
            x = jnp.concatenate([jnp.where(first_tile, meta_block, refs[0][...]), refs[1][...], refs[2][...]], axis=0)
            refs[n_lead + 5][...] = x
        y = x * lax.rsqrt(jnp.mean(x * x, axis=-1, keepdims=True) + EPS) * wn_ref[...]
        yb = y.astype(BF16)
        ynt_ref[...] = y.T.astype(BF16)
        for p in range(p_n):
            r = _dot(yb, w_ref[p])
            if p < n_main:
                main_ref[:, p * s:(p + 1) * s] = r.astype(main_dtype)
            else:
                gate_ref[:, (p - n_main) * s:(p - n_main + 1) * s] = r

    row = pl.BlockSpec((tm, d), lambda i: (i, 0))
    extra_specs, extra_shapes = ((), ()) if from_x is None else ((row,), (jax.ShapeDtypeStruct((t, d), F32),))
    return _pcall(
        body, name=name, grid=(t // tm,),
        in_specs=lead_specs + [pl.BlockSpec((1, d), lambda i: (0, 0)), pl.BlockSpec((p_n, d, s), lambda i: (0, 0, 0))],
        out_specs=(pl.BlockSpec((tm, 3 * d), lambda i: (i, 0)), row, pl.BlockSpec((d, tm), lambda i: (0, i)))
        + extra_specs,
        out_shape=(jax.ShapeDtypeStruct((t, 3 * d), main_dtype), jax.ShapeDtypeStruct((t, d), F32),
                   jax.ShapeDtypeStruct((d, t), BF16)) + extra_shapes,
        compiler_params=_params("parallel"),
    )(*lead, wnorm, w_all)


def _mix_out(o, gate, h_in, w_out, w_on, w_post, head_norm, name, target=None):
    t, d = o.shape
    n_heads = d // HEAD
    tm = ROW_TILE
    with_loss = target is not None
    if with_loss:
        tiles_per_seq = (target.shape[1] + BLOCK) // tm

    def body(o_ref, g_ref, h_ref, w_ref, won_ref, wp_ref, *rest):
        if with_loss:
            t0_ref, t1_ref, t2_ref, hout_ref, y_ref, loss_ref, u_scr = rest
        else:
            hout_ref, y_ref, u_scr = rest
        for hh in range(n_heads):
            cs = slice(hh * HEAD, (hh + 1) * HEAD)
            oh = o_ref[:, cs]
            gt = g_ref[:, cs]
            if head_norm:
                oh = oh * lax.rsqrt(jnp.mean(oh * oh, axis=-1, keepdims=True) + EPS) * won_ref[...]
            u_scr[:, cs] = (oh * (gt * jax.nn.sigmoid(gt))).astype(BF16)
        y = _dot(u_scr[...], w_ref[...])
        y_ref[...] = y
        r = y * lax.rsqrt(jnp.mean(y * y, axis=-1, keepdims=True) + EPS) * wp_ref[...]
        h_out = h_ref[...] + r
        if not with_loss:
            hout_ref[...] = h_out
            return
        i = pl.program_id(0)

        @pl.when(i == 0)
        def _():
            loss_ref[...] = jnp.zeros_like(loss_ref)

        tgt = jnp.concatenate([t0_ref[...], t1_ref[...], t2_ref[...]], axis=0)
        real = jnp.logical_or(i % tiles_per_seq > 0, _iota2((tm, d), 0) >= BLOCK)
        err = jnp.where(real, h_out - tgt, 0.0)
        hout_ref[...] = err * (1.0 / d)
        part = jnp.sum(jnp.sum(err * err, axis=-1, keepdims=True), axis=0, keepdims=True)
        loss_ref[...] += part * (0.5 / d)

    row = pl.BlockSpec((tm, d), lambda i: (i, 0))
    in_specs = [row, row, row, pl.BlockSpec((d, d), lambda i: (0, 0)),
                pl.BlockSpec((1, HEAD), lambda i: (0, 0)), pl.BlockSpec((1, d), lambda i: (0, 0))]
    out_specs, out_shape = (row, row), (jax.ShapeDtypeStruct((t, d), F32), jax.ShapeDtypeStruct((t, d), F32))
    args = (o, gate, h_in, w_out, w_on, w_post)
    if with_loss:
        in_specs += _tile_blocks_of_x(target.shape[1], d)
        out_specs += (pl.BlockSpec((8, 128), lambda i: (0, 0)),)
        out_shape += (jax.ShapeDtypeStruct((8, 128), F32),)
        args += (target, target, target)
    return _pcall(
        body, name=name, grid=(t // tm,), in_specs=in_specs, out_specs=out_specs, out_shape=out_shape,
        scratch_shapes=[pltpu.VMEM((tm, d), BF16)],
        compiler_params=_params("arbitrary" if with_loss else "parallel"),
    )(*args)


def _mix_out_bwd(dh, y, o, gate, w_out, w_on, w_post, head_norm, name):
    t, d = o.shape
    n_heads = d // HEAD
    tm = ROW_TILE
    last = t // tm - 1

    def body(dh_ref, y_ref, o_ref, g_ref, w_ref, won_ref, wp_ref,
             do_ref, dg_ref, dw_ref, dwp_ref, dwon_ref, u_scr, acc):
        i = pl.program_id(0)

        @pl.when(i == 0)
        def _():
            acc[...] = jnp.zeros_like(acc)
            dwp_ref[...] = jnp.zeros_like(dwp_ref)
            dwon_ref[...] = jnp.zeros_like(dwon_ref)

        yv = y_ref[...]
        rs = lax.rsqrt(jnp.mean(yv * yv, axis=-1, keepdims=True) + EPS)
        yh = yv * rs
        dr = dh_ref[...]
        dwp_ref[...] += jnp.sum(dr * yh, axis=0, keepdims=True)
        wd = dr * wp_ref[...]
        dy = rs * (wd - yh * jnp.mean(wd * yh, axis=-1, keepdims=True))
        dyb = dy.astype(BF16)
        du = _dot_nt(dyb, w_ref[...])
        for hh in range(n_heads):
            cs = slice(hh * HEAD, (hh + 1) * HEAD)
            oh = o_ref[:, cs]
            gt = g_ref[:, cs]
            sg = jax.nn.sigmoid(gt)
            sl = gt * sg
            duh = du[:, cs]
            if head_norm:
                rsh = lax.rsqrt(jnp.mean(oh * oh, axis=-1, keepdims=True) + EPS)
                ohat = oh * rsh
                on = ohat * won_ref[...]
            else:
                on = oh
            u_scr[:, cs] = (on * sl).astype(BF16)
            don = duh * sl
            dg_ref[:, cs] = (duh * on * (sg * (1.0 + gt * (1.0 - sg)))).astype(BF16)
            if head_norm:
                dwon_ref[...] += jnp.sum(don * ohat, axis=0, keepdims=True)
                wdn = don * won_ref[...]
                do_ref[:, cs] = rsh * (wdn - ohat * jnp.mean(wdn * ohat, axis=-1, keepdims=True))
            else:
                do_ref[:, cs] = don
        acc[...] += _dot_tn(u_scr[...], dyb)

        @pl.when(i == last)
        def _():
            dw_ref[...] = acc[...].astype(BF16)

    row = pl.BlockSpec((tm, d), lambda i: (i, 0))
    return _pcall(
        body, name=name, grid=(t // tm,),
        in_specs=[row, row, row, row, pl.BlockSpec((d, d), lambda i: (0, 0)),
                  pl.BlockSpec((1, HEAD), lambda i: (0, 0)), pl.BlockSpec((1, d), lambda i: (0, 0))],
        out_specs=(row, row, pl.BlockSpec((d, d), lambda i: (0, 0)), pl.BlockSpec((1, d), lambda i: (0, 0)),
                   pl.BlockSpec((1, HEAD), lambda i: (0, 0))),
        out_shape=(jax.ShapeDtypeStruct((t, d), F32), jax.ShapeDtypeStruct((t, d), BF16),
                   jax.ShapeDtypeStruct((d, d), BF16), jax.ShapeDtypeStruct((1, d), F32),
                   jax.ShapeDtypeStruct((1, HEAD), F32)),
        scratch_shapes=[pltpu.VMEM((tm, d), BF16), pltpu.VMEM((d, d), F32)],
        compiler_params=_params("arbitrary"),
    )(dh, y, o, gate, w_out, w_on, w_post)


def _inproj_bwd_x(dparts, w_all, h, wnorm, dres, ride, ride_modes, name):
    t, d = h.shape
    p_n, _, s = w_all.shape
    per = d // s
    tm = ROW_TILE
    nr = len(ride)
    grid = (t // tm,)

    def body(*refs):
        d0_ref, d1_ref, d2_ref, d3_ref, w_ref, h_ref, wn_ref, dres_ref = refs[:8]
        dh_ref, dwn_ref = refs[8 + nr:10 + nr]
        finish_ride = _ride_along(refs[8:8 + nr], refs[10 + nr:10 + 2 * nr], ride_modes, refs[10 + 2 * nr:], grid)
        i = pl.program_id(0)

        @pl.when(i == 0)
        def _():
            dwn_ref[...] = jnp.zeros_like(dwn_ref)

        pieces = (d0_ref, d1_ref, d2_ref, d3_ref)
        dyn = jnp.zeros((tm, d), F32)
        for p in range(p_n):
            blk = pieces[p // per][:, (p % per) * s:(p % per + 1) * s]
            dyn = dyn + _dot_nt(blk, w_ref[p])
        x = h_ref[...]
        rs = lax.rsqrt(jnp.mean(x * x, axis=-1, keepdims=True) + EPS)
        xh = x * rs
        dwn_ref[...] += jnp.sum(dyn * xh, axis=0, keepdims=True)
        wd = dyn * wn_ref[...]
        dh_ref[...] = dres_ref[...] + rs * (wd - xh * jnp.mean(wd * xh, axis=-1, keepdims=True))
        finish_ride()

    row = pl.BlockSpec((tm, d), lambda i: (i, 0))
    any_spec = pl.BlockSpec(memory_space=pl.ANY)
    return _pcall(
        body, name=name, grid=grid,
        in_specs=[row, row, row, row, pl.BlockSpec((p_n, d, s), lambda i: (0, 0, 0)),
                  row, pl.BlockSpec((1, d), lambda i: (0, 0)), row] + [any_spec] * nr,
        out_specs=(row, pl.BlockSpec((1, d), lambda i: (0, 0))) + (any_spec,) * nr,
        out_shape=(jax.ShapeDtypeStruct((t, d), F32), jax.ShapeDtypeStruct((1, d), F32))
        + _exchange_shapes(ride, ride_modes),
        scratch_shapes=_exchange_sems(nr),
        compiler_params=_params("arbitrary"),
    )(*dparts, w_all, h, wnorm, dres, *ride)


def _inproj_bwd_w(ynt, dparts, s, name):
    d, t = ynt.shape
    per = d // s
    n_sh = len(dparts) * per
    tk = next(c for c in K_TILES if t % c == 0)
    last = t // tk - 1

    def body(ynt_ref, d0_ref, d1_ref, d2_ref, d3_ref, dw_ref, acc):
        p, i = pl.program_id(0), pl.program_id(1)

        @pl.when(i == 0)
        def _():
            acc[...] = jnp.zeros_like(acc)

        for a, piece in enumerate((d0_ref, d1_ref, d2_ref, d3_ref)):
            @pl.when(p // per == a)
            def _():
                acc[...] += _dot(ynt_ref[...], piece[...])

        @pl.when(i == last)
        def _():
            dw_ref[...] = acc[...].astype(BF16)

    def piece_spec(a):
        return pl.BlockSpec((tk, s), lambda p, i: (jnp.where(p // per == a, i, 0),
                                                   jnp.where(p // per == a, p % per, 0)))

    return _pcall(
        body, name=name, grid=(n_sh, t // tk),
        in_specs=[pl.BlockSpec((d, tk), lambda p, i: (0, i))] + [piece_spec(a) for a in range(4)],
        out_specs=pl.BlockSpec((None, d, s), lambda p, i: (p, 0, 0)),
        out_shape=jax.ShapeDtypeStruct((n_sh, d, s), BF16),
        scratch_shapes=[pltpu.VMEM((d, s), F32)],
        compiler_params=_params("parallel", "arbitrary"),
    )(ynt, *dparts)


def _hgrn_gates(fz_ref, lb, b_ref, k_ref):
    sig, nsig = _sig_pair(fz_ref[...])
    f = lb + (1.0 - lb) * sig
    rr, cc = _iota2((BLOCK, BLOCK), 0), _iota2((BLOCK, BLOCK), 1)
    b_ref[...] = _tri_left(_tri(cc <= rr), jnp.log(f))
    k_ref[...] = (1.0 - lb) * nsig


def _hgrn_common(q_ref, b_ref, k_ref):
    b, k = b_ref[...], k_ref[...]
    bend = [b_ref[pl.ds(SUB * j + SUB - 1, 1), :] for j in range(N_SUB)]
    bref = [jnp.zeros((1, HEAD), F32)] + bend[:-1]
    refrow, bendrow = _expand(bref), _expand(bend)
    e_q = jnp.exp(b - refrow)
    e_k = jnp.exp(bendrow - b)
    qt = q_ref[...] * e_q
    kh = k * e_k
    bl = bend[-1]
    return dict(k=k, b=b, bend=bend, bref=bref, refrow=refrow, e_q=e_q, e_k=e_k, qt=qt, kh=kh, bl=bl)


HEADS_PER_STEP = 8


def _lockstep(chunks):
    live = list(chunks)
    while live:
        still = []
        for gen in live:
            try:
                next(gen)
                still.append(gen)
            except StopIteration:
                pass
        live = still


def _ride_along(ride_ins, ride_outs, modes, sems, grid):
    if not ride_ins:
        return lambda: None
    ids = [pl.program_id(a) for a in range(len(grid))]
    start, wait = _exchange_copies(ride_ins, ride_outs, modes, *sems)
    first, last = ids[0] == 0, ids[0] == grid[0] - 1
    for a in range(1, len(grid)):
        first = jnp.logical_and(first, ids[a] == 0)
        last = jnp.logical_and(last, ids[a] == grid[a] - 1)
    pl.when(first)(start)
    return lambda: pl.when(last)(wait)


def _hgrn_fwd(main, lbrow, bsz, n_blocks, ride, ride_modes, name):
    t, d3 = main.shape
    d = d3 // 3
    n_heads = d // HEAD
    n_pairs = n_heads // HEADS_PER_STEP
    wide = HEADS_PER_STEP * HEAD
    nr = len(ride)
    grid = (bsz, n_pairs, n_blocks)

    def chunk(q_ref, fz_ref, v_ref, lb, o_ref, st_ref, b_scr, k_scr, ad_ref, s_scr, o_acc, ad_scr):
        _hgrn_gates(fz_ref, lb, b_scr, k_scr)
        yield
        c = _hgrn_common(q_ref, b_scr, k_scr)
        yield
        s_t = s_scr[...]
        st_ref[...] = s_t
        vb = v_ref[...].astype(BF16)
        q_state = c["qt"] * _expand([jnp.exp(r) for r in c["bref"]])
        o_state = _dot_nt(q_state.astype(BF16), s_t.astype(BF16))
        js = range(N_SUB - 1)
        khb = c["kh"].astype(BF16)
        lhs = [(c["qt"][SUB * (j + 1):] * jnp.exp(c["refrow"][SUB * (j + 1):] - c["bend"][j])).astype(BF16)
               for j in js]
        yield
        a_js = [_dot_nt(lhs[j], khb[SUB * j:SUB * (j + 1)]) for j in js]
        k_state = c["kh"] * _expand([jnp.exp(c["bl"] - r) for r in c["bend"]])
        s_scr[...] = s_t * jnp.exp(c["bl"]) + _dot_tn(vb, k_state.astype(BF16))
        yield
        o_js = [_dot(a_js[j].astype(BF16), vb[SUB * j:SUB * (j + 1)]) for j in js]
        yield
        o_acc[...] = o_state
        for j in js:
            o_acc[SUB * (j + 1):, :] += o_js[j]
        not_before = [_iota2((HALF, HEAD), 0) >= row for row in range(HALF)]
        lane = _iota2((HALF, HEAD), 1)
        for i in range(N_SUB):
            r0 = SUB * i
            q_h = [q_ref[pl.ds(r0 + HALF * u, HALF), :] for u in range(N_HALF)]
            b_h = [b_scr[pl.ds(r0 + HALF * u, HALF), :] for u in range(N_HALF)]
            a_h = [jnp.zeros((HALF, HEAD), F32) for _ in range(N_HALF)]
            for s in range(SUB):
                brow = b_scr[pl.ds(r0 + s, 1), :]
                krow = k_scr[pl.ds(r0 + s, 1), :]
                is_lane = lane == r0 + s
                for u in range(s // HALF, N_HALF):
                    diff = b_h[u] - brow
                    if u == s // HALF:
                        diff = jnp.where(not_before[s - HALF * u], diff, NEG_BIG)
                    col = jnp.sum(q_h[u] * krow * jnp.exp(diff), axis=-1, keepdims=True)
                    a_h[u] = jnp.where(is_lane, col, a_h[u])
            for u in range(N_HALF):
                ad_scr[pl.ds(r0 + HALF * u, HALF), :] = a_h[u]
            yield
        a_diag = ad_scr[...].astype(BF16)
        ad_ref[...] = a_diag
        o_ref[...] = o_acc[...] + _dot(a_diag, vb)

    def body(*refs):
        q_ref, fz_ref, v_ref, lb_ref = refs[:4]
        o_ref, st_ref, b_out, k_out, ad_out = refs[4 + nr:9 + nr]
        s_scr, o_acc, ad_scr = refs[9 + 2 * nr:12 + 2 * nr]
        finish_ride = _ride_along(refs[4:4 + nr], refs[9 + nr:9 + 2 * nr], ride_modes, refs[12 + 2 * nr:], grid)

        @pl.when(pl.program_id(2) == 0)
        def _():
            s_scr[...] = jnp.zeros_like(s_scr)

        def head(hh):
            cols = pl.ds(hh * HEAD, HEAD)
            return chunk(q_ref.at[:, cols], fz_ref.at[:, cols], v_ref.at[:, cols], lb_ref[:, cols],
                         o_ref.at[:, cols], st_ref.at[hh], b_out.at[:, cols], k_out.at[:, cols],
                         ad_out.at[:, cols], s_scr.at[hh], o_acc.at[hh], ad_scr.at[hh])

        _lockstep([head(hh) for hh in range(HEADS_PER_STEP)])
        finish_ride()

    def blk(col0):
        return pl.BlockSpec((BLOCK, wide), lambda b, h, n: (b * n_blocks + n, col0 + h))

    any_spec = pl.BlockSpec(memory_space=pl.ANY)
    per_head = pltpu.VMEM((HEADS_PER_STEP, BLOCK, HEAD), F32)
    return _pcall(
        body, name=name, grid=grid,
        in_specs=[blk(0), blk(n_pairs), blk(2 * n_pairs), pl.BlockSpec((1, wide), lambda b, h, n: (0, h))]
        + [any_spec] * nr,
        out_specs=(blk(0), pl.BlockSpec((None, HEADS_PER_STEP, HEAD, HEAD), lambda b, h, n: (b * n_blocks + n, h, 0, 0)),
                   blk(0), blk(0), blk(0)) + (any_spec,) * nr,
        out_shape=(jax.ShapeDtypeStruct((t, d), F32),
                   jax.ShapeDtypeStruct((bsz * n_blocks, n_heads, HEAD, HEAD), F32),
                   jax.ShapeDtypeStruct((t, d), F32), jax.ShapeDtypeStruct((t, d), F32),
                   jax.ShapeDtypeStruct((t, d), BF16))
        + _exchange_shapes(ride, ride_modes),
        scratch_shapes=[per_head] * 3 + _exchange_sems(nr),
        compiler_params=_params("arbitrary", "arbitrary", "arbitrary"),
    )(main, main, main, lbrow, *ride)


def _hgrn_bwd(main, b_all, k_all, a_diag, lbrow, states, do, bsz, n_blocks, ride, ride_modes, name):
    t, d3 = main.shape
    d = d3 // 3
    n_pairs = d // HEAD // HEADS_PER_STEP
    wide = HEADS_PER_STEP * HEAD
    nr = len(ride)
    grid = (bsz, n_pairs, n_blocks)

    def chunk(n, q_ref, b_scr, k_scr, ad_ref, v_ref, lb, st_ref, do_ref, dq_ref, dfz_ref, dv_ref, dlb_ref,
              ds_scr, dqt_acc, dkh_acc, dv_acc, dqd_acc, dkd_acc):
        c = _hgrn_common(q_ref, b_scr, k_scr)
        yield
        q, k = q_ref[...], c["k"]
        vb = v_ref[...].astype(BF16)
        dob = do_ref[...].astype(BF16)
        s0_t = st_ref[...]
        ds1_t = ds_scr[...]
        e_ref = _expand([jnp.exp(r) for r in c["bref"]])
        e_end = _expand([jnp.exp(c["bl"] - r) for r in c["bend"]])
        e_bl = jnp.exp(c["bl"])
        q_state = c["qt"] * e_ref
        k_state = c["kh"] * e_end
        dq_state = _dot(dob, s0_t.astype(BF16))
        dk_state = _dot(vb, ds1_t.astype(BF16))
        dv_state = _dot_nt(k_state.astype(BF16), ds1_t.astype(BF16))
        ds_scr[...] = ds1_t * e_bl + _dot_tn(dob, q_state.astype(BF16))
        js = range(N_SUB - 1)
        lo = [slice(SUB * j, SUB * (j + 1)) for j in js]
        khb = c["kh"].astype(BF16)
        dj = [jnp.exp(c["refrow"][SUB * (j + 1):] - c["bend"][j]) for j in js]
        lhs = [(c["qt"][SUB * (j + 1):] * dj[j]).astype(BF16) for j in js]
        yield
        a_js = [_dot_nt(lhs[j], khb[lo[j]]) for j in js]
        da_js = [_dot_nt(dob[SUB * (j + 1):], vb[lo[j]]).astype(BF16) for j in js]
        dqt_acc[...] = dq_state * e_ref
        dkh_acc[...] = dk_state * e_end
        dv_acc[...] = dv_state
        dbl = (jnp.sum(s0_t * ds1_t, axis=0, keepdims=True) * e_bl
               + jnp.sum(k_state * dk_state, axis=0, keepdims=True))
        yield
        dv_js = [_dot_tn(a_js[j].astype(BF16), dob[SUB * (j + 1):]) for j in js]
        dq_js = [_dot(da_js[j], khb[lo[j]]) * dj[j] for j in js]
        dk_js = [_dot_tn(da_js[j], lhs[j]) for j in js]
        yield
        for j in js:
            dv_acc[lo[j], :] += dv_js[j]
            dqt_acc[SUB * (j + 1):, :] += dq_js[j]
            dkh_acc[lo[j], :] += dk_js[j]
        not_before = [_iota2((HALF, HEAD), 0) >= row for row in range(HALF)]
        is_row = [_iota2((HALF, HEAD), 0) == row for row in range(HALF)]
        for i in range(N_SUB):
            r0 = SUB * i
            q_h = [q_ref[pl.ds(r0 + HALF * u, HALF), :] for u in range(N_HALF)]
            b_h = [b_scr[pl.ds(r0 + HALF * u, HALF), :] for u in range(N_HALF)]
            do_h = [do_ref[pl.ds(r0 + HALF * u, HALF), :] for u in range(N_HALF)]
            zero = jnp.zeros((HALF, HEAD), F32)
            dq_h, dk_h = [zero] * N_HALF, [zero] * N_HALF
            for s in range(SUB):
                brow = b_scr[pl.ds(r0 + s, 1), :]
                krow = k_scr[pl.ds(r0 + s, 1), :]
                vrow = v_ref[pl.ds(r0 + s, 1), :]
                dk_row = jnp.zeros((1, HEAD), F32)
                for u in range(s // HALF, N_HALF):
                    diff = b_h[u] - brow
                    if u == s // HALF:
                        diff = jnp.where(not_before[s - HALF * u], diff, NEG_BIG)
                    w = jnp.exp(diff)
                    qw = q_h[u] * w
                    da_col = jnp.sum(do_h[u] * vrow, axis=-1, keepdims=True)
                    dq_h[u] = dq_h[u] + da_col * (w * krow)
                    dk_row = dk_row + jnp.sum(da_col * qw, axis=0, keepdims=True)
                us = s // HALF
                dk_h[us] = jnp.where(is_row[s - HALF * us], dk_row, dk_h[us])
            for u in range(N_HALF):
                rows = pl.ds(r0 + HALF * u, HALF)
                dqd_acc[rows, :] = dq_h[u]
                dkd_acc[rows, :] = dk_h[u]
            yield
        dv_in = _dot_tn(ad_ref[...], dob)
        dq = dqt_acc[...] * c["e_q"] + dqd_acc[...]
        dk = dkh_acc[...] * c["e_k"] + dkd_acc[...]
        rr, cc = _iota2((BLOCK, BLOCK), 0), _iota2((BLOCK, BLOCK), 1)
        db = q * dq - k * dk + jnp.where(_iota2((BLOCK, HEAD), 0) == BLOCK - 1, dbl, 0.0)
        yield
        dg = _tri_left(_tri(cc >= rr), db)
        yield
        real = jnp.logical_or(n > 0, _iota2((BLOCK, HEAD), 0) >= N_PAD)
        df = jnp.where(real, dg / (1.0 - k) - dk, 0.0)
        nsig = k * (1.0 / (1.0 - lb))
        dq_ref[...] = dq.astype(BF16)
        dv_ref[...] = (dv_acc[...] + dv_in).astype(BF16)
        dfz_ref[...] = (df * k * (1.0 - nsig)).astype(BF16)
        dlb_ref[...] += jnp.sum(df * nsig, axis=0, keepdims=True)

    def body(*refs):
        q_ref, b_ref, k_ref, ad_ref, v_ref, lb_ref, st_ref, do_ref = refs[:8]
        dq_ref, dfz_ref, dv_ref, dlb_ref = refs[8 + nr:12 + nr]
        scratch = refs[12 + 2 * nr:18 + 2 * nr]
        finish_ride = _ride_along(refs[8:8 + nr], refs[12 + nr:12 + 2 * nr], ride_modes, refs[18 + 2 * nr:], grid)
        step = pl.program_id(2)

        @pl.when(step == 0)
        def _():
            scratch[0][...] = jnp.zeros_like(scratch[0])
            dlb_ref[...] = jnp.zeros_like(dlb_ref)

        def head(hh):
            cols = pl.ds(hh * HEAD, HEAD)
            return chunk(n_blocks - 1 - step, q_ref.at[:, cols], b_ref.at[:, cols], k_ref.at[:, cols],
                         ad_ref.at[:, cols], v_ref.at[:, cols], lb_ref[:, cols], st_ref.at[hh], do_ref.at[:, cols],
                         dq_ref.at[:, cols],
                         dfz_ref.at[:, cols], dv_ref.at[:, cols], dlb_ref.at[:, cols],
                         *[scr.at[hh] for scr in scratch])

        _lockstep([head(hh) for hh in range(HEADS_PER_STEP)])
        finish_ride()

    def blk(col0):
        return pl.BlockSpec((BLOCK, wide), lambda b, h, s: (b * n_blocks + n_blocks - 1 - s, col0 + h))

    any_spec = pl.BlockSpec(memory_space=pl.ANY)
    per_head = pltpu.VMEM((HEADS_PER_STEP, BLOCK, HEAD), F32)
    return _pcall(
        body, name=name, grid=grid,
        in_specs=[blk(0), blk(0), blk(0), blk(0), blk(2 * n_pairs), pl.BlockSpec((1, wide), lambda b, h, s: (0, h)),
                  pl.BlockSpec((None, HEADS_PER_STEP, HEAD, HEAD),
                               lambda b, h, s: (b * n_blocks + n_blocks - 1 - s, h, 0, 0)),
                  blk(0)] + [any_spec] * nr,
        out_specs=(blk(0), blk(0), blk(0), pl.BlockSpec((None, 1, wide), lambda b, h, s: (b, 0, h)))
        + (any_spec,) * nr,
        out_shape=(jax.ShapeDtypeStruct((t, d), BF16),) * 3 + (jax.ShapeDtypeStruct((bsz, 1, d), F32),)
        + _exchange_shapes(ride, ride_modes),
        scratch_shapes=[per_head] * 6 + _exchange_sems(nr),
        compiler_params=_params("arbitrary", "arbitrary", "arbitrary"),
    )(main, b_all, k_all, a_diag, main, lbrow, states, do, *ride)


SB_GROUP = 3
SB_ROWS = SB_GROUP * BLOCK
SB_DEAD = -104.0
SB_NEAR = ((0, 0), (1, 0), (0, 1))
SB_HEADS_FWD = 4
SB_HEADS_BWD = 2
SB_STATIC_OWN = tuple((r, kb) for kb in range(SB_GROUP - 1, -1, -1) for r in range(kb, SB_GROUP))
SB_STATIC_NEAR = tuple((r, -1 - back) for r, back in SB_NEAR)
N_STATIC = len(SB_STATIC_OWN) + len(SB_STATIC_NEAR)


def _sb_tables():
    j = jnp.bitwise_and(_iota2((2 * BLOCK, 2 * BLOCK), 0), BLOCK - 1)
    s = _iota2((2 * BLOCK, 2 * BLOCK), 1)
    ones = s >= BLOCK
    return (_tri(jnp.logical_or(ones, j > s)), _tri(jnp.logical_or(ones, j <= s)),
            _tri(jnp.logical_or(ones, j < s)))


def _sums(x, table, pieces=2):
    hi = x.astype(BF16)
    if pieces == 1:
        r = _dot(hi, table[:BLOCK])
    else:
        lo = (x - hi.astype(F32)).astype(BF16)
        r = _dot(jnp.concatenate([hi, lo], axis=1), table)
    return r[:, :BLOCK], r[:, BLOCK:]


def _sb_logits(q, ks, scale, causal, pad_row):
    z = _dot_nt(q, ks) * scale
    log_keep = -(jnp.maximum(z, 0.0) + jnp.log(1.0 + jnp.exp(-jnp.abs(z))))
    log_beta = z + log_keep
    if causal is not None:
        log_keep = jnp.where(causal, log_keep, 0.0)
    if pad_row is not None:
        log_keep = log_keep * pad_row
    return z, log_beta, log_keep


def _sb_fwd(qkv, bsz, n_blocks, name):
    t, d3 = qkv.shape
    d = d3 // 3
    n_heads = d // HEAD
    lp = n_blocks * BLOCK
    n_groups = n_blocks // SB_GROUP
    assert n_groups * SB_GROUP == n_blocks
    scale = HEAD ** -0.5

    n_pairs = n_heads // SB_HEADS_FWD
    wide = SB_HEADS_FWD * HEAD

    def body(q_ref, k_ref, v_ref, upper_ref, o_ref, tot_ref, stop_ref, w_keep, lb_keep, c_scr):
        g = pl.program_id(2)
        upper = upper_ref[...]
        causal = _iota2((BLOCK, BLOCK), 1) < _iota2((BLOCK, BLOCK), 0)
        lane = _iota2((1, BLOCK), 1)
        o_ref[...] = jnp.zeros_like(o_ref)
        c_scr[...] = jnp.zeros_like(c_scr)

        def tiles(jobs, keep=None):
            cols = [pl.ds(hh * HEAD, HEAD) for hh, _, _, _, _ in jobs]
            qrows = [pl.ds(r * BLOCK, BLOCK) for _, r, _, _, _ in jobs]
            krows = [pl.ds(pl.multiple_of(m * BLOCK, BLOCK), BLOCK) for _, _, m, _, _ in jobs]
            n_jobs = range(len(jobs))
            lg = [_sb_logits(q_ref[qrows[i], cols[i]], k_ref[krows[i], cols[i]], scale,
                             causal if jobs[i][3] else None, jobs[i][4]) for i in n_jobs]
            sm = [_sums(x[2], upper) for x in lg]
            a_all = []
            for i in n_jobs:
                c = c_scr[qrows[i], cols[i]]
                a = jnp.exp(lg[i][1] + c + sm[i][0])
                a_all.append((jnp.where(causal, a, 0.0) if jobs[i][3] else a).astype(BF16))
                c_scr[qrows[i], cols[i]] = c + sm[i][1]
            out = [_dot(a_all[i], v_ref[krows[i], cols[i]]) for i in n_jobs]
            for i in n_jobs:
                o_ref[qrows[i], cols[i]] += out[i]
                if keep is not None:
                    w_keep[jobs[i][0], keep[i]] = a_all[i]
                    lb_keep[jobs[i][0], keep[i]] = lg[i][1].astype(BF16)

        def pad_row_of(m):
            return jnp.where(jnp.logical_or(m > 0, lane >= N_PAD), 1.0, 0.0)

        base = SB_GROUP * g
        heads = range(SB_HEADS_FWD)
        own = [(hh, r, base + rel, r == rel, pad_row_of(base) if rel == 0 else None)
               for hh in heads for r, rel in SB_STATIC_OWN]
        near = [(hh, r, base + rel, False, None) for hh in heads for r, rel in SB_STATIC_NEAR]
        own_slots = [i for _ in heads for i in range(len(SB_STATIC_OWN))]
        near_slots = [len(SB_STATIC_OWN) + i for _ in heads for i in range(len(SB_STATIC_NEAR))]
        pl.when(g == 0)(lambda: tiles(own, keep=own_slots))
        pl.when(g > 0)(lambda: tiles(own + near, keep=own_slots + near_slots))

        stop_ref[...] = jnp.zeros_like(stop_ref)
        mine = [(hh, r) for hh in heads for r in range(SB_GROUP)]
        reach = [jnp.max(c_scr[pl.ds(r * BLOCK, BLOCK), pl.ds(hh * HEAD, HEAD)]) for hh, r in mine]
        for (hh, r), reach_now in zip(mine, reach):
            def live(carry):
                m, c_max = carry
                return jnp.logical_and(m >= 0, c_max >= SB_DEAD)

            def step(carry, hh=hh, r=r):
                m, _ = carry
                tiles([(hh, r, m, False, pad_row_of(m))])
                return m - 1, jnp.max(c_scr[pl.ds(r * BLOCK, BLOCK), pl.ds(hh * HEAD, HEAD)])

            lowest = jnp.maximum(base - (SB_GROUP - 1 - r), 0)
            m_end, _ = lax.while_loop(live, step, (lowest - 1, reach_now))
            stop_ref[hh, pl.ds(r, 1), :] = jnp.broadcast_to((m_end + 1).astype(F32), (1, 128))
        tot_ref[...] = c_scr[...]

    qblk = pl.BlockSpec((SB_ROWS, wide), lambda b, h, g: (b * n_groups + g, h))
    table = pl.BlockSpec((2 * BLOCK, 2 * BLOCK), lambda b, h, g: (0, 0))
    stops = pl.BlockSpec((None, None, SB_HEADS_FWD, 8, 128), lambda b, h, g: (b, g, h, 0, 0))
    kept = pl.BlockSpec((None, None, SB_HEADS_FWD, N_STATIC, BLOCK, BLOCK), lambda b, h, g: (b, g, h, 0, 0, 0))
    return _pcall(
        body, name=name, grid=(bsz, n_pairs, n_groups),
        in_specs=[qblk, pl.BlockSpec((lp, wide), lambda b, h, g: (b, n_pairs + h)),
                  pl.BlockSpec((lp, wide), lambda b, h, g: (b, 2 * n_pairs + h)), table],
        out_specs=(qblk, qblk, stops, kept, kept),
        out_shape=(jax.ShapeDtypeStruct((t, d), F32), jax.ShapeDtypeStruct((t, d), F32),
                   jax.ShapeDtypeStruct((bsz, n_groups, n_heads, 8, 128), F32))
        + (jax.ShapeDtypeStruct((bsz, n_groups, n_heads, N_STATIC, BLOCK, BLOCK), BF16),) * 2,
        scratch_shapes=[pltpu.VMEM((SB_ROWS, wide), F32)],
        compiler_params=_params("parallel", "parallel", "arbitrary"),
    )(qkv, qkv, qkv, _sb_tables()[0])


def _sb_bwd(qkv, do, tot, stop, w_kept, lb_kept, bsz, n_blocks, name):
    t, d3 = qkv.shape
    d = d3 // 3
    n_heads = d // HEAD
    lp = n_blocks * BLOCK
    n_groups = n_blocks // SB_GROUP
    scale = HEAD ** -0.5

    def body(q_ref, k_ref, v_ref, do_ref, tot_ref, stop_ref, incl_ref, excl_ref, w_kept, lb_kept,
             dq_ref, dk_ref, dv_ref, dk_acc, dv_acc, dq_acc, p_scr, e_scr, dob_scr):
        g = pl.program_id(2)

        @pl.when(g == 0)
        def _():
            dk_acc[...] = jnp.zeros_like(dk_acc)
            dv_acc[...] = jnp.zeros_like(dv_acc)

        incl, excl = incl_ref[...], excl_ref[...]
        causal = _iota2((BLOCK, BLOCK), 1) < _iota2((BLOCK, BLOCK), 0)
        lane = _iota2((1, BLOCK), 1)
        dob_scr[...] = do_ref[...].astype(BF16)
        dq_acc[...] = jnp.zeros_like(dq_acc)
        p_scr[...] = jnp.zeros_like(p_scr)
        e_scr[...] = jnp.zeros_like(e_scr)

        def tiles(jobs, slots=None):
            n_jobs = range(len(jobs))
            cols = [pl.ds(hh * HEAD, HEAD) for hh, _, _, _, _ in jobs]
            qrows = [pl.ds(r * BLOCK, BLOCK) for _, r, _, _, _ in jobs]
            krows = [pl.ds(pl.multiple_of(m * BLOCK, BLOCK), BLOCK) for _, _, m, _, _ in jobs]
            diag = [dg for _, _, _, dg, _ in jobs]
            d_a = [_dot_nt(dob_scr[qrows[i], cols[i]], v_ref[krows[i], cols[i]]) for i in n_jobs]
            if slots is None:
                lg = [_sb_logits(q_ref[qrows[i], cols[i]], k_ref[krows[i], cols[i]], scale,
                                 causal if diag[i] else None, jobs[i][4]) for i in n_jobs]
                log_beta = [x[1] for x in lg]
                sm = [_sums(lg[i][2], incl) for i in n_jobs]
                a_all = []
                for i in n_jobs:
                    p = p_scr[qrows[i], cols[i]]
                    a = jnp.exp(log_beta[i] + (tot_ref[qrows[i], cols[i]] - p - sm[i][0]))
                    a_all.append((jnp.where(causal, a, 0.0) if diag[i] else a).astype(BF16))
                    p_scr[qrows[i], cols[i]] = p + sm[i][1]
            else:
                a_all = [w_kept[hh, s] for hh, s in slots]
                log_beta = [lb_kept[hh, s].astype(F32) for hh, s in slots]
            gr = [d_a[i] * a_all[i].astype(F32) for i in n_jobs]
            dv_part = [_dot_tn(a_all[i], dob_scr[qrows[i], cols[i]]) for i in n_jobs]
            gs = [_sums(gr[i], excl, pieces=1) for i in n_jobs]
            dz_all = []
            for i in n_jobs:
                e = e_scr[qrows[i], cols[i]]
                dz = gr[i] - (gr[i] + e + gs[i][0]) * jnp.exp(log_beta[i])
                if diag[i]:
                    dz = jnp.where(causal, dz, 0.0)
                dz_all.append((dz * scale).astype(BF16))
                e_scr[qrows[i], cols[i]] = e + gs[i][1]
            dk_part = [_dot_tn(dz_all[i], q_ref[qrows[i], cols[i]]) for i in n_jobs]
            dq_part = [_dot(dz_all[i], k_ref[krows[i], cols[i]]) for i in n_jobs]
            for i in n_jobs:
                dv_acc[krows[i], cols[i]] += dv_part[i]
                dk_acc[krows[i], cols[i]] += dk_part[i]
                dq_acc[qrows[i], cols[i]] += dq_part[i]

        def pad_row_of(m):
            return jnp.where(jnp.logical_or(m > 0, lane >= N_PAD), 1.0, 0.0)

        base = SB_GROUP * g
        heads = range(SB_HEADS_BWD)
        mine = [(hh, r) for hh in heads for r in range(SB_GROUP)]
        stops = [jnp.max(stop_ref[hh, pl.ds(r, 1), :]).astype(jnp.int32) for hh, r in mine]
        for (hh, r), stop_now in zip(mine, stops):
            lowest = jnp.maximum(base - (SB_GROUP - 1 - r), 0)

            def step(m, hh=hh, r=r):
                tiles([(hh, r, m, False, pad_row_of(m))])
                return m + 1

            lax.while_loop(lambda m, lowest=lowest: m < lowest, step, jnp.clip(stop_now, 0, lowest))
        slot_of = {tile: i for i, tile in enumerate(SB_STATIC_OWN + SB_STATIC_NEAR)}
        upwards = lambda tile: (tile[1], tile[0])
        own = [(hh,) + tile for hh in heads for tile in sorted(SB_STATIC_OWN, key=upwards)]
        near = [(hh,) + tile for hh in heads for tile in sorted(SB_STATIC_NEAR, key=upwards)]

        def static_batch(order):
            tiles([(hh, r, base + rel, r == rel, None) for hh, r, rel in order],
                  slots=[(hh, slot_of[(r, rel)]) for hh, r, rel in order])

        pl.when(g == 0)(lambda: static_batch(own))
        pl.when(g > 0)(lambda: static_batch(near + own))
        dq_ref[...] = dq_acc[...].astype(BF16)

        @pl.when(g == n_groups - 1)
        def _():
            dk_ref[...] = dk_acc[...].astype(BF16)
            dv_ref[...] = dv_acc[...].astype(BF16)

    n_pairs = n_heads // SB_HEADS_BWD
    wide = SB_HEADS_BWD * HEAD
    qblk = pl.BlockSpec((SB_ROWS, wide), lambda b, h, g: (b * n_groups + g, h))
    kblk = pl.BlockSpec((lp, wide), lambda b, h, g: (b, n_pairs + h))
    vblk = pl.BlockSpec((lp, wide), lambda b, h, g: (b, 2 * n_pairs + h))
    hblk = pl.BlockSpec((lp, wide), lambda b, h, g: (b, h))
    sblk = pl.BlockSpec((None, None, SB_HEADS_BWD, 8, 128), lambda b, h, g: (b, g, h, 0, 0))
    table = pl.BlockSpec((2 * BLOCK, 2 * BLOCK), lambda b, h, g: (0, 0))
    kept = pl.BlockSpec((None, None, SB_HEADS_BWD, N_STATIC, BLOCK, BLOCK), lambda b, h, g: (b, g, h, 0, 0, 0))
    _, incl, excl = _sb_tables()
    return _pcall(
        body, name=name, grid=(bsz, n_pairs, n_groups),
        in_specs=[qblk, kblk, vblk, qblk, qblk, sblk, table, table, kept, kept],
        out_specs=(qblk, hblk, hblk),
        out_shape=(jax.ShapeDtypeStruct((t, d), BF16),) * 3,
        scratch_shapes=[pltpu.VMEM((lp, wide), F32), pltpu.VMEM((lp, wide), F32)]
        + [pltpu.VMEM((SB_ROWS, wide), F32)] * 3 + [pltpu.VMEM((SB_ROWS, wide), BF16)],
        compiler_params=_params("parallel", "parallel", "arbitrary"),
    )(qkv, qkv, qkv, do, tot, stop, incl, excl, w_kept, lb_kept)


def _adamw(w, g, m, v):
    m = ADAM_B1 * m + (1.0 - ADAM_B1) * g
    v = ADAM_B2 * v + (1.0 - ADAM_B2) * (g * g)
    m_hat = m / (1.0 - ADAM_B1 ** ADAM_STEP)
    v_hat = v / (1.0 - ADAM_B2 ** ADAM_STEP)
    delta = -ADAM_LR * (m_hat / (jnp.sqrt(v_hat) + ADAM_EPS) + ADAM_WD * w)
    return delta, m, v


def _update_sharded(w, parts, m, v, name):
    r, c = w.shape
    tr = UPDATE_ROWS if r % UPDATE_ROWS == 0 else r

    def body(w_ref, p_ref, m_ref, v_ref, g_ref, d_ref, nm_ref, nv_ref):
        g = p_ref[0].astype(F32)
        for q in range(1, N_DEV):
            g = g + p_ref[q].astype(F32)
        g_ref[...] = g
        d_ref[...], nm_ref[...], nv_ref[...] = _adamw(w_ref[...], g, m_ref[...], v_ref[...])

    row = pl.BlockSpec((tr, c), lambda i: (i, 0))
    return _pcall(
        body, name=name, grid=(r // tr,),
        in_specs=[row, pl.BlockSpec((N_DEV, tr, c), lambda i: (0, i, 0)), row, row],
        out_specs=(row,) * 4, out_shape=(jax.ShapeDtypeStruct((r, c), F32),) * 4,
        compiler_params=_params("parallel"),
    )(w, parts, m, v)


SMALL_ROWS = 8


def _pack_small(dpre0, dpre1, dpost0, dpost1, dlb, dwon, loss, name):
    d = dpre0.shape[1]
    bsz = dlb.shape[0]

    def body(a0, a1, p0, p1, lb_ref, on_ref, loss_ref, out_ref):
        out_ref[...] = jnp.zeros_like(out_ref)
        out_ref[pl.ds(0, 1), :] = a0[...]
        out_ref[pl.ds(1, 1), :] = a1[...]
        out_ref[pl.ds(2, 1), :] = p0[...]
        out_ref[pl.ds(3, 1), :] = p1[...]
        acc = lb_ref[0]
        for b in range(1, bsz):
            acc = acc + lb_ref[b]
        out_ref[pl.ds(4, 1), :] = acc
        out_ref[pl.ds(5, 1), pl.ds(0, HEAD)] = on_ref[...]
        out_ref[pl.ds(6, 1), pl.ds(0, HEAD)] = loss_ref[pl.ds(0, 1), :]

    return _pcall(body, name=name, out_shape=jax.ShapeDtypeStruct((SMALL_ROWS, d), F32))(
        dpre0, dpre1, dpost0, dpost1, dlb, dwon, loss)


def _update_small(parts, pre, post, lbw, on, moments, name):
    d = pre.shape[1]

    def body(p_ref, pre_ref, post_ref, lbw_ref, on_ref, mpre, mpost, mlb, mon, vpre, vpost, vlb, von,
             loss_ref, *outs):
        def total(r0, nr, width):
            acc = p_ref[0, pl.ds(r0, nr), pl.ds(0, width)]
            for q in range(1, N_DEV):
                acc = acc + p_ref[q, pl.ds(r0, nr), pl.ds(0, width)]
            return acc

        def put(k, w, g, m, v):
            dl, nm, nv = _adamw(w, g, m, v)
            outs[4 * k][...] = g
            outs[4 * k + 1][...] = dl
            outs[4 * k + 2][...] = nm
            outs[4 * k + 3][...] = nv

        put(0, pre_ref[...], total(0, 2, d), mpre[...], vpre[...])
        put(1, post_ref[...], total(2, 2, d), mpost[...], vpost[...])
        a0, a1 = lbw_ref[pl.ds(0, 1), :], lbw_ref[pl.ds(1, 1), :]
        mx = jnp.maximum(a0, a1)
        e0, e1 = jnp.exp(a0 - mx), jnp.exp(a1 - mx)
        p0 = e0 / (e0 + e1)
        g0 = total(4, 1, d) * p0 * (1.0 - p0)
        for r, w, g in ((0, a0, g0), (1, a1, -g0)):
            row = pl.ds(r, 1)
            dl, nm, nv = _adamw(w, g, mlb[row, :], vlb[row, :])
            outs[8][row, :] = g
            outs[9][row, :] = dl
            outs[10][row, :] = nm
            outs[11][row, :] = nv
        put(3, on_ref[...], total(5, 1, HEAD), mon[...], von[...])
        loss_ref[...] = jnp.broadcast_to(total(6, 1, HEAD), loss_ref.shape)

    shapes = []
    for w in (pre, post, lbw, on):
        shapes += [jax.ShapeDtypeStruct(w.shape, F32)] * 4
    return _pcall(body, name=name, out_shape=(jax.ShapeDtypeStruct((8, 128), F32), *shapes))(
        parts, pre, post, lbw, on, *moments)


def kernel(x, meta_tokens, pre_norm, post_norm, hgrn_w_in, hgrn_lb, hgrn_out_norm, hgrn_w_out, sb_w_in, sb_w_out, loss_target, m_meta_tokens, m_pre_norm, m_post_norm, m_hgrn_w_in, m_hgrn_lb, m_hgrn_out_norm, m_hgrn_w_out, m_sb_w_in, m_sb_w_out, v_meta_tokens, v_pre_norm, v_post_norm, v_hgrn_w_in, v_hgrn_lb, v_hgrn_out_norm, v_hgrn_w_out, v_sb_w_in, v_sb_w_out):
    bsz, seq, d = x.shape
    n_blocks = seq // BLOCK + 1
    lp = n_blocks * BLOCK
    s = hgrn_w_in.shape[2]
    dsh = d // N_DEV

    w_in_h, meta_all = _gather_once_per_chip([hgrn_w_in[0].astype(BF16), meta_tokens], "gather_weights")
    meta_full = jnp.transpose(meta_all, (1, 0, 2)).reshape(N_META, d)

    lbrow = jnp.cumsum(jax.nn.softmax(hgrn_lb, axis=0), axis=0)[0:1]

    main0, gate0, yn0, h0 = _norm_inproj(None, pre_norm[0:1], w_in_h, F32, "inproj_hgrn", from_x=(x, meta_full))
    o0, states, decay0, k0, a_diag0, w_in_s, w_out_s, w_out_h = _hgrn_fwd(
        main0, lbrow, bsz, n_blocks,
        [sb_w_in[0].astype(BF16), sb_w_out[0].astype(BF16), hgrn_w_out[0].astype(BF16)], ["gather"] * 3, "hgrn_fwd")
    w_out_s = w_out_s.reshape(d, d)
    w_out_h = w_out_h.reshape(d, d)
    h1, y0 = _mix_out(o0, gate0, h0, w_out_h, hgrn_out_norm, post_norm[0:1], True, "mix_out_hgrn")
    main1, gate1, yn1 = _norm_inproj(h1, pre_norm[1:2], w_in_s, BF16, "inproj_sb")
    o1, tot, stop, w_kept, lb_kept = _sb_fwd(main1, bsz, n_blocks, "sb_fwd")
    dh2, y1, loss_part = _mix_out(o1, gate1, h1, w_out_s, hgrn_out_norm, post_norm[1:2], False, "mix_out_sb",
                                  target=loss_target)

    do1, dgate1, dw_out_s, dpost1, _ = _mix_out_bwd(
        dh2, y1, o1, gate1, w_out_s, hgrn_out_norm, post_norm[1:2], False, "mix_out_sb_bwd")
    dq1, dk1, dv1 = _sb_bwd(main1, do1, tot, stop, w_kept, lb_kept, bsz, n_blocks, "sb_bwd")
    dproj1 = (dq1, dk1, dv1, dgate1)
    dh1, dpre1 = _inproj_bwd_x(dproj1, w_in_s, h1, pre_norm[1:2], dh2, [], [], "inproj_sb_bwd_x")
    dw_in_s = _inproj_bwd_w(yn1, dproj1, s, "inproj_sb_bwd_w")

    do0, dgate0, dw_out_h, dpost0, dwon = _mix_out_bwd(
        dh1, y0, o0, gate0, w_out_h, hgrn_out_norm, post_norm[0:1], True, "mix_out_hgrn_bwd")
    dq0, dfz0, dv0, dlb, p_in_s, p_out_s, p_out_h = _hgrn_bwd(
        main0, decay0, k0, a_diag0, lbrow, states, do0, bsz, n_blocks,
        [dw_in_s, dw_out_s.reshape(N_DEV, dsh, d), dw_out_h.reshape(N_DEV, dsh, d)], ["scatter"] * 3, "hgrn_bwd")
    dproj0 = (dq0, dfz0, dv0, dgate0)
    dw_in_h = _inproj_bwd_w(yn0, dproj0, s, "inproj_hgrn_bwd_w")
    dh0, dpre0, p_in_h = _inproj_bwd_x(
        dproj0, w_in_h, h0, pre_norm[0:1], dh1, [dw_in_h], ["scatter"], "inproj_hgrn_bwd_x")

    dh0 = dh0.reshape(bsz, lp, d)
    grad_x = dh0[:, BLOCK:]
    dmeta = jnp.sum(dh0[:, N_PAD:BLOCK], axis=0)
    dmeta = jnp.transpose(dmeta.reshape(N_META, N_DEV, dsh), (1, 0, 2))
    small = _pack_small(dpre0, dpre1, dpost0, dpost1, dlb, dwon, loss_part, "pack_small")

    p_meta, p_small = _exchange([dmeta, small], ["scatter", "gather"], "exchange_grads")

    u_meta = _update_sharded(meta_tokens, p_meta, m_meta_tokens, v_meta_tokens, "update_meta")
    u_in_h = _update_sharded(hgrn_w_in[0], p_in_h, m_hgrn_w_in[0], v_hgrn_w_in[0], "update_hgrn_w_in")
    u_out_h = _update_sharded(hgrn_w_out[0], p_out_h, m_hgrn_w_out[0], v_hgrn_w_out[0], "update_hgrn_w_out")
    u_in_s = _update_sharded(sb_w_in[0], p_in_s, m_sb_w_in[0], v_sb_w_in[0], "update_sb_w_in")
    u_out_s = _update_sharded(sb_w_out[0], p_out_s, m_sb_w_out[0], v_sb_w_out[0], "update_sb_w_out")
    sm = _update_small(p_small, pre_norm, post_norm, hgrn_lb, hgrn_out_norm,
                       (m_pre_norm, m_post_norm, m_hgrn_lb, m_hgrn_out_norm,
                        v_pre_norm, v_post_norm, v_hgrn_lb, v_hgrn_out_norm), "update_small")
    loss = sm[0][0, 0]
    u_pre, u_post, u_lb, u_on = sm[1:5], sm[5:9], sm[9:13], sm[13:17]

    per_w = [u_meta, u_pre, u_post, tuple(a[None] for a in u_in_h), u_lb, u_on,
             tuple(a[None] for a in u_out_h), tuple(a[None] for a in u_in_s), tuple(a[None] for a in u_out_s)]
    outs = [loss, grad_x]
    for k in range(4):
        outs += [u[k] for u in per_w]
    return tuple(outs)
```

```python
import jax
import jax.numpy as jnp
from jax import lax
from jax.experimental import pallas as pl
from jax.experimental.pallas import tpu as pltpu

F32 = jnp.float32
BF16 = jnp.bfloat16

N_DEV = 8
BLOCK = 128
N_META = 16
N_PAD = BLOCK - N_META
HEAD = 128
SUB = 16
N_SUB = BLOCK // SUB
HALF = 8
N_HALF = SUB // HALF
EPS = 1e-6
ROW_TILE = 3 * BLOCK
UPDATE_ROWS = 256
K_TILES = (2816, 1408, 768, 384, 128)
NEG_BIG = -1e30

ADAM_LR = 0.001
ADAM_B1 = 0.9
ADAM_B2 = 0.999
ADAM_EPS = 1e-08
ADAM_WD = 0.01
ADAM_STEP = 10

VMEM_LIMIT = 56 * 1024 * 1024


def _pcall(body, **kw):
    return pl.pallas_call(body, **kw)


def _params(*sem):
    return pltpu.CompilerParams(dimension_semantics=sem, vmem_limit_bytes=VMEM_LIMIT)


def _dot(a, b):
    return jnp.dot(a, b, preferred_element_type=F32)


def _dot_nt(a, b):
    return lax.dot_general(a, b, (((1,), (1,)), ((), ())), preferred_element_type=F32)


def _dot_tn(a, b):
    return lax.dot_general(a, b, (((0,), (0,)), ((), ())), preferred_element_type=F32)


def _split(x, pieces):
    out = []
    for _ in range(pieces):
        p = x.astype(BF16)
        out.append(p)
        x = x - p.astype(F32)
    return out


def _tri_left(tri, x, pieces=3):
    return sum(_dot(tri, p) for p in _split(x, pieces))


def _iota2(shape, dim):
    return lax.broadcasted_iota(jnp.int32, shape, dim)


def _tri(cond):
    return jnp.where(cond, 1.0, 0.0).astype(BF16)


def _sig_pair(x):
    e = jnp.exp(-jnp.abs(x))
    r = 1.0 / (1.0 + e)
    er = e * r
    pos = x >= 0
    return jnp.where(pos, r, er), jnp.where(pos, er, r)


def _expand(rows):
    return jnp.concatenate([jnp.broadcast_to(r, (SUB, HEAD)) for r in rows], axis=0)


def _exchange_shapes(arrays, modes):
    return tuple(jax.ShapeDtypeStruct((N_DEV,) + tuple(a.shape[1:] if m == "scatter" else a.shape), a.dtype)
                 for a, m in zip(arrays, modes))


def _exchange_sems(n):
    if n == 0:
        return []
    return [pltpu.SemaphoreType.DMA((n, N_DEV - 1)), pltpu.SemaphoreType.DMA((n, N_DEV - 1)),
            pltpu.SemaphoreType.DMA((n,))]


def _exchange_copies(ins, outs, modes, send_sems, recv_sems, local_sems):
    mx, my, mc = lax.axis_index("x"), lax.axis_index("y"), lax.axis_index("c")
    me = 4 * mx + 2 * my + mc

    def src(i, slot):
        return ins[i].at[slot] if modes[i] == "scatter" else ins[i]

    def peer_of(mask):
        px = 1 - mx if mask & 4 else mx
        py = 1 - my if mask & 2 else my
        pc = 1 - mc if mask & 1 else mc
        return px, py, pc

    def copy(i, mask, dst_slot):
        px, py, pc = peer_of(mask)
        return pltpu.make_async_remote_copy(
            src_ref=src(i, 4 * px + 2 * py + pc), dst_ref=outs[i].at[dst_slot],
            send_sem=send_sems.at[i, mask - 1], recv_sem=recv_sems.at[i, mask - 1],
            device_id=(px, py, pc), device_id_type=pl.DeviceIdType.MESH)

    n = len(ins)
    sends = [copy(i, mask, me) for mask in range(1, N_DEV) for i in range(n)]
    own = [pltpu.make_async_copy(src(i, me), outs[i].at[me], local_sems.at[i]) for i in range(n)]
    arrivals = []
    for mask in range(1, N_DEV):
        px, py, pc = peer_of(mask)
        arrivals += [copy(i, mask, 4 * px + 2 * py + pc) for i in range(n)]

    def start():
        for cp in sends + own:
            cp.start()

    def wait():
        for cp in arrivals:
            cp.wait_recv()
        for cp in sends:
            cp.wait_send()
        for cp in own:
            cp.wait()

    return start, wait


def _gather_once_per_chip(arrays, name):
    n = len(arrays)

    def body(*refs):
        ins, outs = refs[:n], refs[n:2 * n]
        send_sems, recv_sems, local_sems = refs[2 * n:]
        mx, my, mc = lax.axis_index("x"), lax.axis_index("y"), lax.axis_index("c")
        me, sibling = (mx, my, mc), (mx, my, 1 - mc)
        chips = [(1 - mx, my), (mx, 1 - my), (1 - mx, 1 - my)]

        def slot(px, py, pc):
            return 4 * px + 2 * py + pc

        def copy(i, k, block, to, src=None):
            return pltpu.make_async_remote_copy(
                src_ref=outs[i].at[slot(*block)] if src is None else src, dst_ref=outs[i].at[slot(*block)],
                send_sem=send_sems.at[i, k], recv_sem=recv_sems.at[i, k],
                device_id=to, device_id_type=pl.DeviceIdType.MESH)

        own = [pltpu.make_async_copy(ins[i], outs[i].at[slot(*me)], local_sems.at[i]) for i in range(n)]
        first = [copy(i, 0, me, sibling, src=ins[i]) for i in range(n)]
        first += [copy(i, 1 + j, me, (*chip, mc), src=ins[i]) for j, chip in enumerate(chips) for i in range(n)]
        for cp in own + first:
            cp.start()
        passed = []
        for j, chip in enumerate(chips):
            for i in range(n):
                copy(i, 1 + j, (*chip, mc), me).wait_recv()
                cp = copy(i, 4 + j, (*chip, mc), sibling)
                cp.start()
                passed.append(cp)
        for i in range(n):
            copy(i, 0, sibling, me).wait_recv()
            for j, chip in enumerate(chips):
                copy(i, 4 + j, (*chip, 1 - mc), me).wait_recv()
        for cp in first + passed:
            cp.wait_send()
        for cp in own:
            cp.wait()

    any_spec = pl.BlockSpec(memory_space=pl.ANY)
    return _pcall(
        body, name=name, out_shape=_exchange_shapes(arrays, ["gather"] * n),
        in_specs=[any_spec] * n, out_specs=tuple([any_spec] * n),
        scratch_shapes=_exchange_sems(n),
    )(*arrays)


def _exchange(arrays, modes, name):
    n = len(arrays)

    def body(*refs):
        start, wait = _exchange_copies(refs[:n], refs[n:2 * n], modes, *refs[2 * n:])
        start()
        wait()

    any_spec = pl.BlockSpec(memory_space=pl.ANY)
    return _pcall(
        body, name=name, out_shape=_exchange_shapes(arrays, modes),
        in_specs=[any_spec] * n, out_specs=tuple([any_spec] * n),
        scratch_shapes=_exchange_sems(n),
    )(*arrays)


def _tile_blocks_of_x(seq, d):
    assert ROW_TILE == 3 * BLOCK and (seq + BLOCK) % ROW_TILE == 0
    per_seq = (seq + BLOCK) // ROW_TILE

    def spec(k):
        return pl.BlockSpec((None, BLOCK, d),
                            lambda i: (i // per_seq, jnp.maximum(3 * (i % per_seq) - 1 + k, 0), 0))

    return [spec(0), spec(1), spec(2)]


def _project(x, wn_ref, w_ref, main_ref, gate_ref, ynt_ref):
    p_n, _, s = w_ref.shape
    n_main = main_ref.shape[1] // s
    y = x * lax.rsqrt(jnp.mean(x * x, axis=-1, keepdims=True) + EPS) * wn_ref[...]
    yb = y.astype(BF16)
    ynt_ref[...] = y.T.astype(BF16)
    for p in range(p_n):
        r = _dot(yb, w_ref[p])
        if p < n_main:
            main_ref[:, p * s:(p + 1) * s] = r.astype(main_ref.dtype)
        else:
            gate_ref[:, (p - n_main) * s:(p - n_main + 1) * s] = r


def _projection_specs(t, d, w_all, main_dtype, tm):
    p_n, _, s = w_all.shape
    row = pl.BlockSpec((tm, d), lambda i: (i, 0))
    return ([pl.BlockSpec((1, d), lambda i: (0, 0)), pl.BlockSpec((p_n, d, s), lambda i: (0, 0, 0))],
            (pl.BlockSpec((tm, 3 * d), lambda i: (i, 0)), row, pl.BlockSpec((d, tm), lambda i: (0, i))),
            (jax.ShapeDtypeStruct((t, 3 * d), main_dtype), jax.ShapeDtypeStruct((t, d), F32),
             jax.ShapeDtypeStruct((d, t), BF16)))


def _norm_inproj(h, wnorm, w_all, main_dtype, name, from_x=None):
    p_n, d, s = w_all.shape
    n_main = 3 * d // s
    tm = ROW_TILE
    if from_x is None:
        t = h.shape[0]
        lead, lead_specs = [h], [pl.BlockSpec((tm, d), lambda i: (i, 0))]
    else:
        x_in, meta = from_x
        t = x_in.shape[0] * (x_in.shape[1] + BLOCK)
        tiles_per_seq = (x_in.shape[1] + BLOCK) // tm
        lead = [x_in, x_in, x_in, meta]
        lead_specs = _tile_blocks_of_x(x_in.shape[1], d) + [pl.BlockSpec((N_META, d), lambda i: (0, 0))]
    n_lead = len(lead)

    def body(*refs):
        wn_ref, w_ref = refs[n_lead:n_lead + 2]
        main_ref, gate_ref, ynt_ref = refs[n_lead + 2:n_lead + 5]
        if from_x is None:
            x = refs[0][...]
        else:
            first_tile = pl.program_id(0) % tiles_per_seq == 0
            meta_block = jnp.concatenate([jnp.zeros((N_PAD, d), F32), refs[3][...]], axis=0)
            x = jnp.concatenate([jnp.where(first_tile, meta_block, refs[0][...]), refs[1][...], refs[2][...]], axis=0)
            refs[n_lead + 5][...] = x
        _project(x, wn_ref, w_ref, main_ref, gate_ref, ynt_ref)

    row = pl.BlockSpec((tm, d), lambda i: (i, 0))
    extra_specs, extra_shapes = ((), ()) if from_x is None else ((row,), (jax.ShapeDtypeStruct((t, d), F32),))
    proj_in, proj_out, proj_shapes = _projection_specs(t, d, w_all, main_dtype, tm)
    return _pcall(
        body, name=name, grid=(t // tm,),
        in_specs=lead_specs + proj_in, out_specs=proj_out + extra_specs, out_shape=proj_shapes + extra_shapes,
        compiler_params=_params("parallel"),
    )(*lead, wnorm, w_all)


def _mix_out(o, gate, h_in, w_out, w_on, w_post, head_norm, name, target=None, next_proj=None):
    t, d = o.shape
    n_heads = d // HEAD
    tm = ROW_TILE
    with_loss = target is not None
    if with_loss:
        tiles_per_seq = (target.shape[1] + BLOCK) // tm

    def body(o_ref, g_ref, h_ref, w_ref, won_ref, wp_ref, *rest):
        if with_loss:
            t0_ref, t1_ref, t2_ref, hout_ref, y_ref, loss_ref, u_scr = rest
        elif next_proj is not None:
            wn2_ref, w2_ref, hout_ref, y_ref, main_ref, gate_ref, ynt_ref, u_scr = rest
        else:
            hout_ref, y_ref, u_scr = rest
        for hh in range(n_heads):
            cs = slice(hh * HEAD, (hh + 1) * HEAD)
            oh = o_ref[:, cs]
            gt = g_ref[:, cs]
            if head_norm:
                oh = oh * lax.rsqrt(jnp.mean(oh * oh, axis=-1, keepdims=True) + EPS) * won_ref[...]
            u_scr[:, cs] = (oh * (gt * jax.nn.sigmoid(gt))).astype(BF16)
        y = _dot(u_scr[...], w_ref[...])
        y_ref[...] = y
        r = y * lax.rsqrt(jnp.mean(y * y, axis=-1, keepdims=True) + EPS) * wp_ref[...]
        h_out = h_ref[...] + r
        if not with_loss:
            hout_ref[...] = h_out
            if next_proj is not None:
                _project(h_out, wn2_ref, w2_ref, main_ref, gate_ref, ynt_ref)
            return
        i = pl.program_id(0)

        @pl.when(i == 0)
        def _():
            loss_ref[...] = jnp.zeros_like(loss_ref)

        tgt = jnp.concatenate([t0_ref[...], t1_ref[...], t2_ref[...]], axis=0)
        real = jnp.logical_or(i % tiles_per_seq > 0, _iota2((tm, d), 0) >= BLOCK)
        err = jnp.where(real, h_out - tgt, 0.0)
        hout_ref[...] = err * (1.0 / d)
        part = jnp.sum(jnp.sum(err * err, axis=-1, keepdims=True), axis=0, keepdims=True)
        loss_ref[...] += part * (0.5 / d)

    row = pl.BlockSpec((tm, d), lambda i: (i, 0))
    in_specs = [row, row, row, pl.BlockSpec((d, d), lambda i: (0, 0)),
                pl.BlockSpec((1, HEAD), lambda i: (0, 0)), pl.BlockSpec((1, d), lambda i: (0, 0))]
    out_specs, out_shape = (row, row), (jax.ShapeDtypeStruct((t, d), F32), jax.ShapeDtypeStruct((t, d), F32))
    args = (o, gate, h_in, w_out, w_on, w_post)
    if with_loss:
        in_specs += _tile_blocks_of_x(target.shape[1], d)
        out_specs += (pl.BlockSpec((8, 128), lambda i: (0, 0)),)
        out_shape += (jax.ShapeDtypeStruct((8, 128), F32),)
        args += (target, target, target)
    elif next_proj is not None:
        proj_in, proj_out, proj_shapes = _projection_specs(t, d, next_proj[1], next_proj[2], tm)
        in_specs += proj_in
        out_specs += proj_out
        out_shape += proj_shapes
        args += (next_proj[0], next_proj[1])
    return _pcall(
        body, name=name, grid=(t // tm,), in_specs=in_specs, out_specs=out_specs, out_shape=out_shape,
        scratch_shapes=[pltpu.VMEM((tm, d), BF16)],
        compiler_params=_params("arbitrary" if with_loss else "parallel"),
    )(*args)


def _mix_out_bwd(dh, y, o, gate, w_out, w_on, w_post, head_norm, name):
    t, d = o.shape
    n_heads = d // HEAD
    tm = ROW_TILE
    last = t // tm - 1

    def body(dh_ref, y_ref, o_ref, g_ref, w_ref, won_ref, wp_ref,
             do_ref, dg_ref, dw_ref, dwp_ref, dwon_ref, u_scr, acc):
        i = pl.program_id(0)

        @pl.when(i == 0)
        def _():
            acc[...] = jnp.zeros_like(acc)
            dwp_ref[...] = jnp.zeros_like(dwp_ref)
            dwon_ref[...] = jnp.zeros_like(dwon_ref)

        yv = y_ref[...]
        rs = lax.rsqrt(jnp.mean(yv * yv, axis=-1, keepdims=True) + EPS)
        yh = yv * rs
        dr = dh_ref[...]
        dwp_ref[...] += jnp.sum(dr * yh, axis=0, keepdims=True)
        wd = dr * wp_ref[...]
        dy = rs * (wd - yh * jnp.mean(wd * yh, axis=-1, keepdims=True))
        dyb = dy.astype(BF16)
        du = _dot_nt(dyb, w_ref[...])
        for hh in range(n_heads):
            cs = slice(hh * HEAD, (hh + 1) * HEAD)
            oh = o_ref[:, cs]
            gt = g_ref[:, cs]
            sg = jax.nn.sigmoid(gt)
            sl = gt * sg
            duh = du[:, cs]
            if head_norm:
                rsh = lax.rsqrt(jnp.mean(oh * oh, axis=-1, keepdims=True) + EPS)
                ohat = oh * rsh
                on = ohat * won_ref[...]
            else:
                on = oh
            u_scr[:, cs] = (on * sl).astype(BF16)
            don = duh * sl
            dg_ref[:, cs] = (duh * on * (sg * (1.0 + gt * (1.0 - sg)))).astype(BF16)
            if head_norm:
                dwon_ref[...] += jnp.sum(don * ohat, axis=0, keepdims=True)
                wdn = don * won_ref[...]
                do_ref[:, cs] = rsh * (wdn - ohat * jnp.mean(wdn * ohat, axis=-1, keepdims=True))
            else:
                do_ref[:, cs] = don
        acc[...] += _dot_tn(u_scr[...], dyb)

        @pl.when(i == last)
        def _():
            dw_ref[...] = acc[...].astype(BF16)

    row = pl.BlockSpec((tm, d), lambda i: (i, 0))
    return _pcall(
        body, name=name, grid=(t // tm,),
        in_specs=[row, row, row, row, pl.BlockSpec((d, d), lambda i: (0, 0)),
                  pl.BlockSpec((1, HEAD), lambda i: (0, 0)), pl.BlockSpec((1, d), lambda i: (0, 0))],
        out_specs=(row, row, pl.BlockSpec((d, d), lambda i: (0, 0)), pl.BlockSpec((1, d), lambda i: (0, 0)),
                   pl.BlockSpec((1, HEAD), lambda i: (0, 0))),
        out_shape=(jax.ShapeDtypeStruct((t, d), F32), jax.ShapeDtypeStruct((t, d), BF16),
                   jax.ShapeDtypeStruct((d, d), BF16), jax.ShapeDtypeStruct((1, d), F32),
                   jax.ShapeDtypeStruct((1, HEAD), F32)),
        scratch_shapes=[pltpu.VMEM((tm, d), BF16), pltpu.VMEM((d, d), F32)],
        compiler_params=_params("arbitrary"),
    )(dh, y, o, gate, w_out, w_on, w_post)


def _inproj_bwd_x(dparts, w_all, h, wnorm, dres, ride, ride_modes, name):
    t, d = h.shape
    p_n, _, s = w_all.shape
    per = d // s
    tm = ROW_TILE
    nr = len(ride)
    grid = (t // tm,)

    def body(*refs):
        d0_ref, d1_ref, d2_ref, d3_ref, w_ref, h_ref, wn_ref, dres_ref = refs[:8]
        dh_ref, dwn_ref = refs[8 + nr:10 + nr]
        finish_ride = _ride_along(refs[8:8 + nr], refs[10 + nr:10 + 2 * nr], ride_modes, refs[10 + 2 * nr:], grid)
        i = pl.program_id(0)

        @pl.when(i == 0)
        def _():
            dwn_ref[...] = jnp.zeros_like(dwn_ref)

        pieces = (d0_ref, d1_ref, d2_ref, d3_ref)
        dyn = jnp.zeros((tm, d), F32)
        for p in range(p_n):
            blk = pieces[p // per][:, (p % per) * s:(p % per + 1) * s]
            dyn = dyn + _dot_nt(blk, w_ref[p])
        x = h_ref[...]
        rs = lax.rsqrt(jnp.mean(x * x, axis=-1, keepdims=True) + EPS)
        xh = x * rs
        dwn_ref[...] += jnp.sum(dyn * xh, axis=0, keepdims=True)
        wd = dyn * wn_ref[...]
        dh_ref[...] = dres_ref[...] + rs * (wd - xh * jnp.mean(wd * xh, axis=-1, keepdims=True))
        finish_ride()

    row = pl.BlockSpec((tm, d), lambda i: (i, 0))
    any_spec = pl.BlockSpec(memory_space=pl.ANY)
    return _pcall(
        body, name=name, grid=grid,
        in_specs=[row, row, row, row, pl.BlockSpec((p_n, d, s), lambda i: (0, 0, 0)),
                  row, pl.BlockSpec((1, d), lambda i: (0, 0)), row] + [any_spec] * nr,
        out_specs=(row, pl.BlockSpec((1, d), lambda i: (0, 0))) + (any_spec,) * nr,
        out_shape=(jax.ShapeDtypeStruct((t, d), F32), jax.ShapeDtypeStruct((1, d), F32))
        + _exchange_shapes(ride, ride_modes),
        scratch_shapes=_exchange_sems(nr),
        compiler_params=_params("arbitrary"),
    )(*dparts, w_all, h, wnorm, dres, *ride)


def _inproj_bwd_w(ynt, dparts, s, name):
    d, t = ynt.shape
    per = d // s
    n_sh = len(dparts) * per
    tk = next(c for c in K_TILES if t % c == 0)
    last = t // tk - 1

    def body(ynt_ref, d0_ref, d1_ref, d2_ref, d3_ref, dw_ref, acc):
        p, i = pl.program_id(0), pl.program_id(1)

        @pl.when(i == 0)
        def _():
            acc[...] = jnp.zeros_like(acc)

        for a, piece in enumerate((d0_ref, d1_ref, d2_ref, d3_ref)):
            @pl.when(p // per == a)
            def _():
                acc[...] += _dot(ynt_ref[...], piece[...])

        @pl.when(i == last)
        def _():
            dw_ref[...] = acc[...].astype(BF16)

    def piece_spec(a):
        return pl.BlockSpec((tk, s), lambda p, i: (jnp.where(p // per == a, i, 0),
                                                   jnp.where(p // per == a, p % per, 0)))

    return _pcall(
        body, name=name, grid=(n_sh, t // tk),
        in_specs=[pl.BlockSpec((d, tk), lambda p, i: (0, i))] + [piece_spec(a) for a in range(4)],
        out_specs=pl.BlockSpec((None, d, s), lambda p, i: (p, 0, 0)),
        out_shape=jax.ShapeDtypeStruct((n_sh, d, s), BF16),
        scratch_shapes=[pltpu.VMEM((d, s), F32)],
        compiler_params=_params("parallel", "arbitrary"),
    )(ynt, *dparts)


def _hgrn_gates(fz_ref, lb, b_ref, k_ref):
    sig, nsig = _sig_pair(fz_ref[...])
    f = lb + (1.0 - lb) * sig
    rr, cc = _iota2((BLOCK, BLOCK), 0), _iota2((BLOCK, BLOCK), 1)
    b_ref[...] = _tri_left(_tri(cc <= rr), jnp.log(f))
    k_ref[...] = (1.0 - lb) * nsig


def _hgrn_common(q_ref, b_ref, k_ref):
    b, k = b_ref[...], k_ref[...]
    bend = [b_ref[pl.ds(SUB * j + SUB - 1, 1), :] for j in range(N_SUB)]
    bref = [jnp.zeros((1, HEAD), F32)] + bend[:-1]
    refrow, bendrow = _expand(bref), _expand(bend)
    e_q = jnp.exp(b - refrow)
    e_k = jnp.exp(bendrow - b)
    qt = q_ref[...] * e_q
    kh = k * e_k
    bl = bend[-1]
    return dict(k=k, b=b, bend=bend, bref=bref, refrow=refrow, e_q=e_q, e_k=e_k, qt=qt, kh=kh, bl=bl)


HEADS_PER_STEP = 8


def _lockstep(chunks):
    live = list(chunks)
    while live:
        still = []
        for gen in live:
            try:
                next(gen)
                still.append(gen)
            except StopIteration:
                pass
        live = still


def _ride_along(ride_ins, ride_outs, modes, sems, grid):
    if not ride_ins:
        return lambda: None
    ids = [pl.program_id(a) for a in range(len(grid))]
    start, wait = _exchange_copies(ride_ins, ride_outs, modes, *sems)
    first, last = ids[0] == 0, ids[0] == grid[0] - 1
    for a in range(1, len(grid)):
        first = jnp.logical_and(first, ids[a] == 0)
        last = jnp.logical_and(last, ids[a] == grid[a] - 1)
    pl.when(first)(start)
    return lambda: pl.when(last)(wait)


def _hgrn_fwd(main, lbrow, bsz, n_blocks, ride, ride_modes, name):
    t, d3 = main.shape
    d = d3 // 3
    n_heads = d // HEAD
    n_pairs = n_heads // HEADS_PER_STEP
    wide = HEADS_PER_STEP * HEAD
    nr = len(ride)
    grid = (bsz, n_pairs, n_blocks)

    def chunk(q_ref, fz_ref, v_ref, lb, o_ref, st_ref, b_scr, k_scr, ad_ref, s_scr, o_acc, ad_scr):
        _hgrn_gates(fz_ref, lb, b_scr, k_scr)
        yield
        c = _hgrn_common(q_ref, b_scr, k_scr)
        yield
        s_t = s_scr[...]
        st_ref[...] = s_t
        vb = v_ref[...].astype(BF16)
        q_state = c["qt"] * _expand([jnp.exp(r) for r in c["bref"]])
        o_state = _dot_nt(q_state.astype(BF16), s_t.astype(BF16))
        js = range(N_SUB - 1)
        khb = c["kh"].astype(BF16)
        lhs = [(c["qt"][SUB * (j + 1):] * jnp.exp(c["refrow"][SUB * (j + 1):] - c["bend"][j])).astype(BF16)
               for j in js]
        yield
        a_js = [_dot_nt(lhs[j], khb[SUB * j:SUB * (j + 1)]) for j in js]
        k_state = c["kh"] * _expand([jnp.exp(c["bl"] - r) for r in c["bend"]])
        s_scr[...] = s_t * jnp.exp(c["bl"]) + _dot_tn(vb, k_state.astype(BF16))
        yield
        o_js = [_dot(a_js[j].astype(BF16), vb[SUB * j:SUB * (j + 1)]) for j in js]
        yield
        o_acc[...] = o_state
        for j in js:
            o_acc[SUB * (j + 1):, :] += o_js[j]
        not_before = [_iota2((HALF, HEAD), 0) >= row for row in range(HALF)]
        lane = _iota2((HALF, HEAD), 1)
        for i in range(N_SUB):
            r0 = SUB * i
            q_h = [q_ref[pl.ds(r0 + HALF * u, HALF), :] for u in range(N_HALF)]
            b_h = [b_scr[pl.ds(r0 + HALF * u, HALF), :] for u in range(N_HALF)]
            a_h = [jnp.zeros((HALF, HEAD), F32) for _ in range(N_HALF)]
            for s in range(SUB):
                brow = b_scr[pl.ds(r0 + s, 1), :]
                krow = k_scr[pl.ds(r0 + s, 1), :]
                is_lane = lane == r0 + s
                for u in range(s // HALF, N_HALF):
                    diff = b_h[u] - brow
                    if u == s // HALF:
                        diff = jnp.where(not_before[s - HALF * u], diff, NEG_BIG)
                    col = jnp.sum(q_h[u] * krow * jnp.exp(diff), axis=-1, keepdims=True)
                    a_h[u] = jnp.where(is_lane, col, a_h[u])
            for u in range(N_HALF):
                ad_scr[pl.ds(r0 + HALF * u, HALF), :] = a_h[u]
            yield
        a_diag = ad_scr[...].astype(BF16)
        ad_ref[...] = a_diag
        o_ref[...] = o_acc[...] + _dot(a_diag, vb)

    def body(*refs):
        q_ref, fz_ref, v_ref, lb_ref = refs[:4]
        o_ref, st_ref, b_out, k_out, ad_out = refs[4 + nr:9 + nr]
        s_scr, o_acc, ad_scr = refs[9 + 2 * nr:12 + 2 * nr]
        finish_ride = _ride_along(refs[4:4 + nr], refs[9 + nr:9 + 2 * nr], ride_modes, refs[12 + 2 * nr:], grid)

        @pl.when(pl.program_id(2) == 0)
        def _():
            s_scr[...] = jnp.zeros_like(s_scr)

        def head(hh):
            cols = pl.ds(hh * HEAD, HEAD)
            return chunk(q_ref.at[:, cols], fz_ref.at[:, cols], v_ref.at[:, cols], lb_ref[:, cols],
                         o_ref.at[:, cols], st_ref.at[hh], b_out.at[:, cols], k_out.at[:, cols],
                         ad_out.at[:, cols], s_scr.at[hh], o_acc.at[hh], ad_scr.at[hh])

        _lockstep([head(hh) for hh in range(HEADS_PER_STEP)])
        finish_ride()

    def blk(col0):
        return pl.BlockSpec((BLOCK, wide), lambda b, h, n: (b * n_blocks + n, col0 + h))

    any_spec = pl.BlockSpec(memory_space=pl.ANY)
    per_head = pltpu.VMEM((HEADS_PER_STEP, BLOCK, HEAD), F32)
    return _pcall(
        body, name=name, grid=grid,
        in_specs=[blk(0), blk(n_pairs), blk(2 * n_pairs), pl.BlockSpec((1, wide), lambda b, h, n: (0, h))]
        + [any_spec] * nr,
        out_specs=(blk(0), pl.BlockSpec((None, HEADS_PER_STEP, HEAD, HEAD), lambda b, h, n: (b * n_blocks + n, h, 0, 0)),
                   blk(0), blk(0), blk(0)) + (any_spec,) * nr,
        out_shape=(jax.ShapeDtypeStruct((t, d), F32),
                   jax.ShapeDtypeStruct((bsz * n_blocks, n_heads, HEAD, HEAD), F32),
                   jax.ShapeDtypeStruct((t, d), F32), jax.ShapeDtypeStruct((t, d), F32),
                   jax.ShapeDtypeStruct((t, d), BF16))
        + _exchange_shapes(ride, ride_modes),
        scratch_shapes=[per_head] * 3 + _exchange_sems(nr),
        compiler_params=_params("arbitrary", "arbitrary", "arbitrary"),
    )(main, main, main, lbrow, *ride)


def _hgrn_bwd(main, b_all, k_all, a_diag, lbrow, states, do, bsz, n_blocks, ride, ride_modes, name):
    t, d3 = main.shape
    d = d3 // 3
    n_pairs = d // HEAD // HEADS_PER_STEP
    wide = HEADS_PER_STEP * HEAD
    nr = len(ride)
    grid = (bsz, n_pairs, n_blocks)

    def chunk(n, q_ref, b_scr, k_scr, ad_ref, v_ref, lb, st_ref, do_ref, dq_ref, dfz_ref, dv_ref, dlb_ref,
              ds_scr, dqt_acc, dkh_acc, dv_acc, dqd_acc, dkd_acc):
        c = _hgrn_common(q_ref, b_scr, k_scr)
        yield
        q, k = q_ref[...], c["k"]
        vb = v_ref[...].astype(BF16)
        dob = do_ref[...].astype(BF16)
        s0_t = st_ref[...]
        ds1_t = ds_scr[...]
        e_ref = _expand([jnp.exp(r) for r in c["bref"]])
        e_end = _expand([jnp.exp(c["bl"] - r) for r in c["bend"]])
        e_bl = jnp.exp(c["bl"])
        q_state = c["qt"] * e_ref
        k_state = c["kh"] * e_end
        dq_state = _dot(dob, s0_t.astype(BF16))
        dk_state = _dot(vb, ds1_t.astype(BF16))
        dv_state = _dot_nt(k_state.astype(BF16), ds1_t.astype(BF16))
        ds_scr[...] = ds1_t * e_bl + _dot_tn(dob, q_state.astype(BF16))
        js = range(N_SUB - 1)
        lo = [slice(SUB * j, SUB * (j + 1)) for j in js]
        khb = c["kh"].astype(BF16)
        dj = [jnp.exp(c["refrow"][SUB * (j + 1):] - c["bend"][j]) for j in js]
        lhs = [(c["qt"][SUB * (j + 1):] * dj[j]).astype(BF16) for j in js]
        yield
        a_js = [_dot_nt(lhs[j], khb[lo[j]]) for j in js]
        da_js = [_dot_nt(dob[SUB * (j + 1):], vb[lo[j]]).astype(BF16) for j in js]
        dqt_acc[...] = dq_state * e_ref
        dkh_acc[...] = dk_state * e_end
        dv_acc[...] = dv_state
        dbl = (jnp.sum(s0_t * ds1_t, axis=0, keepdims=True) * e_bl
               + jnp.sum(k_state * dk_state, axis=0, keepdims=True))
        yield
        dv_js = [_dot_tn(a_js[j].astype(BF16), dob[SUB * (j + 1):]) for j in js]
        dq_js = [_dot(da_js[j], khb[lo[j]]) * dj[j] for j in js]
        dk_js = [_dot_tn(da_js[j], lhs[j]) for j in js]
        yield
        for j in js:
            dv_acc[lo[j], :] += dv_js[j]
            dqt_acc[SUB * (j + 1):, :] += dq_js[j]
            dkh_acc[lo[j], :] += dk_js[j]
        not_before = [_iota2((HALF, HEAD), 0) >= row for row in range(HALF)]
        is_row = [_iota2((HALF, HEAD), 0) == row for row in range(HALF)]
        for i in range(N_SUB):
            r0 = SUB * i
            q_h = [q_ref[pl.ds(r0 + HALF * u, HALF), :] for u in range(N_HALF)]
            b_h = [b_scr[pl.ds(r0 + HALF * u, HALF), :] for u in range(N_HALF)]
            do_h = [do_ref[pl.ds(r0 + HALF * u, HALF), :] for u in range(N_HALF)]
            zero = jnp.zeros((HALF, HEAD), F32)
            dq_h, dk_h = [zero] * N_HALF, [zero] * N_HALF
            for s in range(SUB):
                brow = b_scr[pl.ds(r0 + s, 1), :]
                krow = k_scr[pl.ds(r0 + s, 1), :]
                vrow = v_ref[pl.ds(r0 + s, 1), :]
                dk_row = jnp.zeros((1, HEAD), F32)
                for u in range(s // HALF, N_HALF):
                    diff = b_h[u] - brow
                    if u == s // HALF:
                        diff = jnp.where(not_before[s - HALF * u], diff, NEG_BIG)
                    w = jnp.exp(diff)
                    qw = q_h[u] * w
                    da_col = jnp.sum(do_h[u] * vrow, axis=-1, keepdims=True)
                    dq_h[u] = dq_h[u] + da_col * (w * krow)
                    dk_row = dk_row + jnp.sum(da_col * qw, axis=0, keepdims=True)
                us = s // HALF
                dk_h[us] = jnp.where(is_row[s - HALF * us], dk_row, dk_h[us])
            for u in range(N_HALF):
                rows = pl.ds(r0 + HALF * u, HALF)
                dqd_acc[rows, :] = dq_h[u]
                dkd_acc[rows, :] = dk_h[u]
            yield
        dv_in = _dot_tn(ad_ref[...], dob)
        dq = dqt_acc[...] * c["e_q"] + dqd_acc[...]
        dk = dkh_acc[...] * c["e_k"] + dkd_acc[...]
        rr, cc = _iota2((BLOCK, BLOCK), 0), _iota2((BLOCK, BLOCK), 1)
        db = q * dq - k * dk + jnp.where(_iota2((BLOCK, HEAD), 0) == BLOCK - 1, dbl, 0.0)
        yield
        dg = _tri_left(_tri(cc >= rr), db)
        yield
        real = jnp.logical_or(n > 0, _iota2((BLOCK, HEAD), 0) >= N_PAD)
        df = jnp.where(real, dg / (1.0 - k) - dk, 0.0)
        nsig = k * (1.0 / (1.0 - lb))
        dq_ref[...] = dq.astype(BF16)
        dv_ref[...] = (dv_acc[...] + dv_in).astype(BF16)
        dfz_ref[...] = (df * k * (1.0 - nsig)).astype(BF16)
        dlb_ref[...] += jnp.sum(df * nsig, axis=0, keepdims=True)

    def body(*refs):
        q_ref, b_ref, k_ref, ad_ref, v_ref, lb_ref, st_ref, do_ref = refs[:8]
        dq_ref, dfz_ref, dv_ref, dlb_ref = refs[8 + nr:12 + nr]
        scratch = refs[12 + 2 * nr:18 + 2 * nr]
        finish_ride = _ride_along(refs[8:8 + nr], refs[12 + nr:12 + 2 * nr], ride_modes, refs[18 + 2 * nr:], grid)
        step = pl.program_id(2)

        @pl.when(step == 0)
        def _():
            scratch[0][...] = jnp.zeros_like(scratch[0])
            dlb_ref[...] = jnp.zeros_like(dlb_ref)

        def head(hh):
            cols = pl.ds(hh * HEAD, HEAD)
            return chunk(n_blocks - 1 - step, q_ref.at[:, cols], b_ref.at[:, cols], k_ref.at[:, cols],
                         ad_ref.at[:, cols], v_ref.at[:, cols], lb_ref[:, cols], st_ref.at[hh], do_ref.at[:, cols],
                         dq_ref.at[:, cols],
                         dfz_ref.at[:, cols], dv_ref.at[:, cols], dlb_ref.at[:, cols],
                         *[scr.at[hh] for scr in scratch])

        _lockstep([head(hh) for hh in range(HEADS_PER_STEP)])
        finish_ride()

    def blk(col0):
        return pl.BlockSpec((BLOCK, wide), lambda b, h, s: (b * n_blocks + n_blocks - 1 - s, col0 + h))

    any_spec = pl.BlockSpec(memory_space=pl.ANY)
    per_head = pltpu.VMEM((HEADS_PER_STEP, BLOCK, HEAD), F32)
    return _pcall(
        body, name=name, grid=grid,
        in_specs=[blk(0), blk(0), blk(0), blk(0), blk(2 * n_pairs), pl.BlockSpec((1, wide), lambda b, h, s: (0, h)),
                  pl.BlockSpec((None, HEADS_PER_STEP, HEAD, HEAD),
                               lambda b, h, s: (b * n_blocks + n_blocks - 1 - s, h, 0, 0)),
                  blk(0)] + [any_spec] * nr,
        out_specs=(blk(0), blk(0), blk(0), pl.BlockSpec((None, 1, wide), lambda b, h, s: (b, 0, h)))
        + (any_spec,) * nr,
        out_shape=(jax.ShapeDtypeStruct((t, d), BF16),) * 3 + (jax.ShapeDtypeStruct((bsz, 1, d), F32),)
        + _exchange_shapes(ride, ride_modes),
        scratch_shapes=[per_head] * 6 + _exchange_sems(nr),
        compiler_params=_params("arbitrary", "arbitrary", "arbitrary"),
    )(main, b_all, k_all, a_diag, main, lbrow, states, do, *ride)


SB_GROUP = 3
SB_ROWS = SB_GROUP * BLOCK
SB_DEAD = -104.0
SB_NEAR = ((0, 0), (1, 0), (0, 1))
SB_HEADS_FWD = 4
SB_HEADS_BWD = 2
SB_STATIC_OWN = tuple((r, kb) for kb in range(SB_GROUP - 1, -1, -1) for r in range(kb, SB_GROUP))
SB_STATIC_NEAR = tuple((r, -1 - back) for r, back in SB_NEAR)
N_STATIC = len(SB_STATIC_OWN) + len(SB_STATIC_NEAR)


def _sb_tables():
    j = jnp.bitwise_and(_iota2((2 * BLOCK, 2 * BLOCK), 0), BLOCK - 1)
    s = _iota2((2 * BLOCK, 2 * BLOCK), 1)
    ones = s >= BLOCK
    return (_tri(jnp.logical_or(ones, j > s)), _tri(jnp.logical_or(ones, j <= s)),
            _tri(jnp.logical_or(ones, j < s)))


def _sums(x, table, pieces=2):
    hi = x.astype(BF16)
    if pieces == 1:
        r = _dot(hi, table[:BLOCK])
    else:
        lo = (x - hi.astype(F32)).astype(BF16)
        r = _dot(jnp.concatenate([hi, lo], axis=1), table)
    return r[:, :BLOCK], r[:, BLOCK:]


def _sb_logits(q, ks, scale, causal, pad_row):
    z = _dot_nt(q, ks) * scale
    log_keep = -(jnp.maximum(z, 0.0) + jnp.log(1.0 + jnp.exp(-jnp.abs(z))))
    log_beta = z + log_keep
    if causal is not None:
        log_keep = jnp.where(causal, log_keep, 0.0)
    if pad_row is not None:
        log_keep = log_keep * pad_row
    return z, log_beta, log_keep


def _sb_fwd(qkv, bsz, n_blocks, name):
    t, d3 = qkv.shape
    d = d3 // 3
    n_heads = d // HEAD
    lp = n_blocks * BLOCK
    n_groups = n_blocks // SB_GROUP
    assert n_groups * SB_GROUP == n_blocks
    scale = HEAD ** -0.5

    n_pairs = n_heads // SB_HEADS_FWD
    wide = SB_HEADS_FWD * HEAD

    def body(q_ref, k_ref, v_ref, upper_ref, o_ref, tot_ref, stop_ref, w_keep, lb_keep, c_scr):
        g = pl.program_id(2)
        upper = upper_ref[...]
        causal = _iota2((BLOCK, BLOCK), 1) < _iota2((BLOCK, BLOCK), 0)
        lane = _iota2((1, BLOCK), 1)
        o_ref[...] = jnp.zeros_like(o_ref)
        c_scr[...] = jnp.zeros_like(c_scr)

        def tiles(jobs, keep=None):
            cols = [pl.ds(hh * HEAD, HEAD) for hh, _, _, _, _ in jobs]
            qrows = [pl.ds(r * BLOCK, BLOCK) for _, r, _, _, _ in jobs]
            krows = [pl.ds(pl.multiple_of(m * BLOCK, BLOCK), BLOCK) for _, _, m, _, _ in jobs]
            n_jobs = range(len(jobs))
            lg = [_sb_logits(q_ref[qrows[i], cols[i]], k_ref[krows[i], cols[i]], scale,
                             causal if jobs[i][3] else None, jobs[i][4]) for i in n_jobs]
            sm = [_sums(x[2], upper) for x in lg]
            a_all = []
            for i in n_jobs:
                c = c_scr[qrows[i], cols[i]]
                a = jnp.exp(lg[i][1] + c + sm[i][0])
                a_all.append((jnp.where(causal, a, 0.0) if jobs[i][3] else a).astype(BF16))
                c_scr[qrows[i], cols[i]] = c + sm[i][1]
            out = [_dot(a_all[i], v_ref[krows[i], cols[i]]) for i in n_jobs]
            for i in n_jobs:
                o_ref[qrows[i], cols[i]] += out[i]
                if keep is not None:
                    w_keep[jobs[i][0], keep[i]] = a_all[i]
                    lb_keep[jobs[i][0], keep[i]] = lg[i][1].astype(BF16)

        def pad_row_of(m):
            return jnp.where(jnp.logical_or(m > 0, lane >= N_PAD), 1.0, 0.0)

        base = SB_GROUP * g
        heads = range(SB_HEADS_FWD)
        own = [(hh, r, base + rel, r == rel, pad_row_of(base) if rel == 0 else None)
               for hh in heads for r, rel in SB_STATIC_OWN]
        near = [(hh, r, base + rel, False, None) for hh in heads for r, rel in SB_STATIC_NEAR]
        own_slots = [i for _ in heads for i in range(len(SB_STATIC_OWN))]
        near_slots = [len(SB_STATIC_OWN) + i for _ in heads for i in range(len(SB_STATIC_NEAR))]
        pl.when(g == 0)(lambda: tiles(own, keep=own_slots))
        pl.when(g > 0)(lambda: tiles(own + near, keep=own_slots + near_slots))

        stop_ref[...] = jnp.zeros_like(stop_ref)
        mine = [(hh, r) for hh in heads for r in range(SB_GROUP)]
        reach = [jnp.max(c_scr[pl.ds(r * BLOCK, BLOCK), pl.ds(hh * HEAD, HEAD)]) for hh, r in mine]
        for (hh, r), reach_now in zip(mine, reach):
            def live(carry):
                m, c_max = carry
                return jnp.logical_and(m >= 0, c_max >= SB_DEAD)

            def step(carry, hh=hh, r=r):
                m, _ = carry
                tiles([(hh, r, m, False, pad_row_of(m))])
                return m - 1, jnp.max(c_scr[pl.ds(r * BLOCK, BLOCK), pl.ds(hh * HEAD, HEAD)])

            lowest = jnp.maximum(base - (SB_GROUP - 1 - r), 0)
            m_end, _ = lax.while_loop(live, step, (lowest - 1, reach_now))
            stop_ref[hh, pl.ds(r, 1), :] = jnp.broadcast_to((m_end + 1).astype(F32), (1, 128))
        tot_ref[...] = c_scr[...]

    qblk = pl.BlockSpec((SB_ROWS, wide), lambda b, h, g: (b * n_groups + g, h))
    table = pl.BlockSpec((2 * BLOCK, 2 * BLOCK), lambda b, h, g: (0, 0))
    stops = pl.BlockSpec((None, None, SB_HEADS_FWD, 8, 128), lambda b, h, g: (b, g, h, 0, 0))
    kept = pl.BlockSpec((None, None, SB_HEADS_FWD, N_STATIC, BLOCK, BLOCK), lambda b, h, g: (b, g, h, 0, 0, 0))
    return _pcall(
        body, name=name, grid=(bsz, n_pairs, n_groups),
        in_specs=[qblk, pl.BlockSpec((lp, wide), lambda b, h, g: (b, n_pairs + h)),
                  pl.BlockSpec((lp, wide), lambda b, h, g: (b, 2 * n_pairs + h)), table],
        out_specs=(qblk, qblk, stops, kept, kept),
        out_shape=(jax.ShapeDtypeStruct((t, d), F32), jax.ShapeDtypeStruct((t, d), F32),
                   jax.ShapeDtypeStruct((bsz, n_groups, n_heads, 8, 128), F32))
        + (jax.ShapeDtypeStruct((bsz, n_groups, n_heads, N_STATIC, BLOCK, BLOCK), BF16),) * 2,
        scratch_shapes=[pltpu.VMEM((SB_ROWS, wide), F32)],
        compiler_params=_params("parallel", "parallel", "arbitrary"),
    )(qkv, qkv, qkv, _sb_tables()[0])


def _sb_bwd(qkv, do, tot, stop, w_kept, lb_kept, bsz, n_blocks, name):
    t, d3 = qkv.shape
    d = d3 // 3
    n_heads = d // HEAD
    lp = n_blocks * BLOCK
    n_groups = n_blocks // SB_GROUP
    scale = HEAD ** -0.5

    def body(q_ref, k_ref, v_ref, do_ref, tot_ref, stop_ref, incl_ref, excl_ref, w_kept, lb_kept,
             dq_ref, dk_ref, dv_ref, dk_acc, dv_acc, dq_acc, p_scr, e_scr, dob_scr):
        g = pl.program_id(2)

        @pl.when(g == 0)
        def _():
            dk_acc[...] = jnp.zeros_like(dk_acc)
            dv_acc[...] = jnp.zeros_like(dv_acc)

        incl, excl = incl_ref[...], excl_ref[...]
        causal = _iota2((BLOCK, BLOCK), 1) < _iota2((BLOCK, BLOCK), 0)
        lane = _iota2((1, BLOCK), 1)
        dob_scr[...] = do_ref[...].astype(BF16)
        dq_acc[...] = jnp.zeros_like(dq_acc)
        p_scr[...] = jnp.zeros_like(p_scr)
        e_scr[...] = jnp.zeros_like(e_scr)

        def tiles(jobs, slots=None):
            n_jobs = range(len(jobs))
            cols = [pl.ds(hh * HEAD, HEAD) for hh, _, _, _, _ in jobs]
            qrows = [pl.ds(r * BLOCK, BLOCK) for _, r, _, _, _ in jobs]
            krows = [pl.ds(pl.multiple_of(m * BLOCK, BLOCK), BLOCK) for _, _, m, _, _ in jobs]
            diag = [dg for _, _, _, dg, _ in jobs]
            d_a = [_dot_nt(dob_scr[qrows[i], cols[i]], v_ref[krows[i], cols[i]]) for i in n_jobs]
            if slots is None:
                lg = [_sb_logits(q_ref[qrows[i], cols[i]], k_ref[krows[i], cols[i]], scale,
                                 causal if diag[i] else None, jobs[i][4]) for i in n_jobs]
                log_beta = [x[1] for x in lg]
                sm = [_sums(lg[i][2], incl) for i in n_jobs]
                a_all = []
                for i in n_jobs:
                    p = p_scr[qrows[i], cols[i]]
                    a = jnp.exp(log_beta[i] + (tot_ref[qrows[i], cols[i]] - p - sm[i][0]))
                    a_all.append((jnp.where(causal, a, 0.0) if diag[i] else a).astype(BF16))
                    p_scr[qrows[i], cols[i]] = p + sm[i][1]
            else:
                a_all = [w_kept[hh, s] for hh, s in slots]
                log_beta = [lb_kept[hh, s].astype(F32) for hh, s in slots]
            gr = [d_a[i] * a_all[i].astype(F32) for i in n_jobs]
            dv_part = [_dot_tn(a_all[i], dob_scr[qrows[i], cols[i]]) for i in n_jobs]
            gs = [_sums(gr[i], excl, pieces=1) for i in n_jobs]
            dz_all = []
            for i in n_jobs:
                e = e_scr[qrows[i], cols[i]]
                dz = gr[i] - (gr[i] + e + gs[i][0]) * jnp.exp(log_beta[i])
                if diag[i]:
                    dz = jnp.where(causal, dz, 0.0)
                dz_all.append((dz * scale).astype(BF16))
                e_scr[qrows[i], cols[i]] = e + gs[i][1]
            dk_part = [_dot_tn(dz_all[i], q_ref[qrows[i], cols[i]]) for i in n_jobs]
            dq_part = [_dot(dz_all[i], k_ref[krows[i], cols[i]]) for i in n_jobs]
            for i in n_jobs:
                dv_acc[krows[i], cols[i]] += dv_part[i]
                dk_acc[krows[i], cols[i]] += dk_part[i]
                dq_acc[qrows[i], cols[i]] += dq_part[i]

        def pad_row_of(m):
            return jnp.where(jnp.logical_or(m > 0, lane >= N_PAD), 1.0, 0.0)

        base = SB_GROUP * g
        heads = range(SB_HEADS_BWD)
        mine = [(hh, r) for hh in heads for r in range(SB_GROUP)]
        stops = [jnp.max(stop_ref[hh, pl.ds(r, 1), :]).astype(jnp.int32) for hh, r in mine]
        for (hh, r), stop_now in zip(mine, stops):
            lowest = jnp.maximum(base - (SB_GROUP - 1 - r), 0)

            def step(m, hh=hh, r=r):
                tiles([(hh, r, m, False, pad_row_of(m))])
                return m + 1

            lax.while_loop(lambda m, lowest=lowest: m < lowest, step, jnp.clip(stop_now, 0, lowest))
        slot_of = {tile: i for i, tile in enumerate(SB_STATIC_OWN + SB_STATIC_NEAR)}
        upwards = lambda tile: (tile[1], tile[0])
        own = [(hh,) + tile for hh in heads for tile in sorted(SB_STATIC_OWN, key=upwards)]
        near = [(hh,) + tile for hh in heads for tile in sorted(SB_STATIC_NEAR, key=upwards)]

        def static_batch(order):
            tiles([(hh, r, base + rel, r == rel, None) for hh, r, rel in order],
                  slots=[(hh, slot_of[(r, rel)]) for hh, r, rel in order])

        pl.when(g == 0)(lambda: static_batch(own))
        pl.when(g > 0)(lambda: static_batch(near + own))
        dq_ref[...] = dq_acc[...].astype(BF16)

        @pl.when(g == n_groups - 1)
        def _():
            dk_ref[...] = dk_acc[...].astype(BF16)
            dv_ref[...] = dv_acc[...].astype(BF16)

    n_pairs = n_heads // SB_HEADS_BWD
    wide = SB_HEADS_BWD * HEAD
    qblk = pl.BlockSpec((SB_ROWS, wide), lambda b, h, g: (b * n_groups + g, h))
    kblk = pl.BlockSpec((lp, wide), lambda b, h, g: (b, n_pairs + h))
    vblk = pl.BlockSpec((lp, wide), lambda b, h, g: (b, 2 * n_pairs + h))
    hblk = pl.BlockSpec((lp, wide), lambda b, h, g: (b, h))
    sblk = pl.BlockSpec((None, None, SB_HEADS_BWD, 8, 128), lambda b, h, g: (b, g, h, 0, 0))
    table = pl.BlockSpec((2 * BLOCK, 2 * BLOCK), lambda b, h, g: (0, 0))
    kept = pl.BlockSpec((None, None, SB_HEADS_BWD, N_STATIC, BLOCK, BLOCK), lambda b, h, g: (b, g, h, 0, 0, 0))
    _, incl, excl = _sb_tables()
    return _pcall(
        body, name=name, grid=(bsz, n_pairs, n_groups),
        in_specs=[qblk, kblk, vblk, qblk, qblk, sblk, table, table, kept, kept],
        out_specs=(qblk, hblk, hblk),
        out_shape=(jax.ShapeDtypeStruct((t, d), BF16),) * 3,
        scratch_shapes=[pltpu.VMEM((lp, wide), F32), pltpu.VMEM((lp, wide), F32)]
        + [pltpu.VMEM((SB_ROWS, wide), F32)] * 3 + [pltpu.VMEM((SB_ROWS, wide), BF16)],
        compiler_params=_params("parallel", "parallel", "arbitrary"),
    )(qkv, qkv, qkv, do, tot, stop, incl, excl, w_kept, lb_kept)


def _adamw(w, g, m, v):
    m = ADAM_B1 * m + (1.0 - ADAM_B1) * g
    v = ADAM_B2 * v + (1.0 - ADAM_B2) * (g * g)
    m_hat = m / (1.0 - ADAM_B1 ** ADAM_STEP)
    v_hat = v / (1.0 - ADAM_B2 ** ADAM_STEP)
    delta = -ADAM_LR * (m_hat / (jnp.sqrt(v_hat) + ADAM_EPS) + ADAM_WD * w)
    return delta, m, v


def _update_sharded(w, parts, m, v, name):
    r, c = w.shape
    tr = UPDATE_ROWS if r % UPDATE_ROWS == 0 else r

    def body(w_ref, p_ref, m_ref, v_ref, g_ref, d_ref, nm_ref, nv_ref):
        g = p_ref[0].astype(F32)
        for q in range(1, N_DEV):
            g = g + p_ref[q].astype(F32)
        g_ref[...] = g
        d_ref[...], nm_ref[...], nv_ref[...] = _adamw(w_ref[...], g, m_ref[...], v_ref[...])

    row = pl.BlockSpec((tr, c), lambda i: (i, 0))
    return _pcall(
        body, name=name, grid=(r // tr,),
        in_specs=[row, pl.BlockSpec((N_DEV, tr, c), lambda i: (0, i, 0)), row, row],
        out_specs=(row,) * 4, out_shape=(jax.ShapeDtypeStruct((r, c), F32),) * 4,
        compiler_params=_params("parallel"),
    )(w, parts, m, v)


SMALL_ROWS = 8


def _pack_small(dpre0, dpre1, dpost0, dpost1, dlb, dwon, loss, name):
    d = dpre0.shape[1]
    bsz = dlb.shape[0]

    def body(a0, a1, p0, p1, lb_ref, on_ref, loss_ref, out_ref):
        out_ref[...] = jnp.zeros_like(out_ref)
        out_ref[pl.ds(0, 1), :] = a0[...]
        out_ref[pl.ds(1, 1), :] = a1[...]
        out_ref[pl.ds(2, 1), :] = p0[...]
        out_ref[pl.ds(3, 1), :] = p1[...]
        acc = lb_ref[0]
        for b in range(1, bsz):
            acc = acc + lb_ref[b]
        out_ref[pl.ds(4, 1), :] = acc
        out_ref[pl.ds(5, 1), pl.ds(0, HEAD)] = on_ref[...]
        out_ref[pl.ds(6, 1), pl.ds(0, HEAD)] = loss_ref[pl.ds(0, 1), :]

    return _pcall(body, name=name, out_shape=jax.ShapeDtypeStruct((SMALL_ROWS, d), F32))(
        dpre0, dpre1, dpost0, dpost1, dlb, dwon, loss)


def _update_small(parts, pre, post, lbw, on, moments, name):
    d = pre.shape[1]

    def body(p_ref, pre_ref, post_ref, lbw_ref, on_ref, mpre, mpost, mlb, mon, vpre, vpost, vlb, von,
             loss_ref, *outs):
        def total(r0, nr, width):
            acc = p_ref[0, pl.ds(r0, nr), pl.ds(0, width)]
            for q in range(1, N_DEV):
                acc = acc + p_ref[q, pl.ds(r0, nr), pl.ds(0, width)]
            return acc

        def put(k, w, g, m, v):
            dl, nm, nv = _adamw(w, g, m, v)
            outs[4 * k][...] = g
            outs[4 * k + 1][...] = dl
            outs[4 * k + 2][...] = nm
            outs[4 * k + 3][...] = nv

        put(0, pre_ref[...], total(0, 2, d), mpre[...], vpre[...])
        put(1, post_ref[...], total(2, 2, d), mpost[...], vpost[...])
        a0, a1 = lbw_ref[pl.ds(0, 1), :], lbw_ref[pl.ds(1, 1), :]
        mx = jnp.maximum(a0, a1)
        e0, e1 = jnp.exp(a0 - mx), jnp.exp(a1 - mx)
        p0 = e0 / (e0 + e1)
        g0 = total(4, 1, d) * p0 * (1.0 - p0)
        for r, w, g in ((0, a0, g0), (1, a1, -g0)):
            row = pl.ds(r, 1)
            dl, nm, nv = _adamw(w, g, mlb[row, :], vlb[row, :])
            outs[8][row, :] = g
            outs[9][row, :] = dl
            outs[10][row, :] = nm
            outs[11][row, :] = nv
        put(3, on_ref[...], total(5, 1, HEAD), mon[...], von[...])
        loss_ref[...] = jnp.broadcast_to(total(6, 1, HEAD), loss_ref.shape)

    shapes = []
    for w in (pre, post, lbw, on):
        shapes += [jax.ShapeDtypeStruct(w.shape, F32)] * 4
    return _pcall(body, name=name, out_shape=(jax.ShapeDtypeStruct((8, 128), F32), *shapes))(
        parts, pre, post, lbw, on, *moments)


def kernel(x, meta_tokens, pre_norm, post_norm, hgrn_w_in, hgrn_lb, hgrn_out_norm, hgrn_w_out, sb_w_in, sb_w_out, loss_target, m_meta_tokens, m_pre_norm, m_post_norm, m_hgrn_w_in, m_hgrn_lb, m_hgrn_out_norm, m_hgrn_w_out, m_sb_w_in, m_sb_w_out, v_meta_tokens, v_pre_norm, v_post_norm, v_hgrn_w_in, v_hgrn_lb, v_hgrn_out_norm, v_hgrn_w_out, v_sb_w_in, v_sb_w_out):
    bsz, seq, d = x.shape
    n_blocks = seq // BLOCK + 1
    lp = n_blocks * BLOCK
    s = hgrn_w_in.shape[2]
    dsh = d // N_DEV

    w_in_h, meta_all = _gather_once_per_chip([hgrn_w_in[0].astype(BF16), meta_tokens], "gather_weights")
    meta_full = jnp.transpose(meta_all, (1, 0, 2)).reshape(N_META, d)

    lbrow = jnp.cumsum(jax.nn.softmax(hgrn_lb, axis=0), axis=0)[0:1]

    main0, gate0, yn0, h0 = _norm_inproj(None, pre_norm[0:1], w_in_h, F32, "inproj_hgrn", from_x=(x, meta_full))
    o0, states, decay0, k0, a_diag0, w_in_s, w_out_s, w_out_h = _hgrn_fwd(
        main0, lbrow, bsz, n_blocks,
        [sb_w_in[0].astype(BF16), sb_w_out[0].astype(BF16), hgrn_w_out[0].astype(BF16)], ["gather"] * 3, "hgrn_fwd")
    w_out_s = w_out_s.reshape(d, d)
    w_out_h = w_out_h.reshape(d, d)
    h1, y0, main1, gate1, yn1 = _mix_out(o0, gate0, h0, w_out_h, hgrn_out_norm, post_norm[0:1], True,
                                         "mix_out_hgrn_inproj_sb", next_proj=(pre_norm[1:2], w_in_s, BF16))
    o1, tot, stop, w_kept, lb_kept = _sb_fwd(main1, bsz, n_blocks, "sb_fwd")
    dh2, y1, loss_part = _mix_out(o1, gate1, h1, w_out_s, hgrn_out_norm, post_norm[1:2], False, "mix_out_sb",
                                  target=loss_target)

    do1, dgate1, dw_out_s, dpost1, _ = _mix_out_bwd(
        dh2, y1, o1, gate1, w_out_s, hgrn_out_norm, post_norm[1:2], False, "mix_out_sb_bwd")
    dq1, dk1, dv1 = _sb_bwd(main1, do1, tot, stop, w_kept, lb_kept, bsz, n_blocks, "sb_bwd")
    dproj1 = (dq1, dk1, dv1, dgate1)
    dh1, dpre1 = _inproj_bwd_x(dproj1, w_in_s, h1, pre_norm[1:2], dh2, [], [], "inproj_sb_bwd_x")
    dw_in_s = _inproj_bwd_w(yn1, dproj1, s, "inproj_sb_bwd_w")

    do0, dgate0, dw_out_h, dpost0, dwon = _mix_out_bwd(
        dh1, y0, o0, gate0, w_out_h, hgrn_out_norm, post_norm[0:1], True, "mix_out_hgrn_bwd")
    dq0, dfz0, dv0, dlb, p_in_s, p_out_s, p_out_h = _hgrn_bwd(
        main0, decay0, k0, a_diag0, lbrow, states, do0, bsz, n_blocks,
        [dw_in_s, dw_out_s.reshape(N_DEV, dsh, d), dw_out_h.reshape(N_DEV, dsh, d)], ["scatter"] * 3, "hgrn_bwd")
    dproj0 = (dq0, dfz0, dv0, dgate0)
    dw_in_h = _inproj_bwd_w(yn0, dproj0, s, "inproj_hgrn_bwd_w")
    dh0, dpre0, p_in_h = _inproj_bwd_x(
        dproj0, w_in_h, h0, pre_norm[0:1], dh1, [dw_in_h], ["scatter"], "inproj_hgrn_bwd_x")

    dh0 = dh0.reshape(bsz, lp, d)
    grad_x = dh0[:, BLOCK:]
    dmeta = jnp.sum(dh0[:, N_PAD:BLOCK], axis=0)
    dmeta = jnp.transpose(dmeta.reshape(N_META, N_DEV, dsh), (1, 0, 2))
    small = _pack_small(dpre0, dpre1, dpost0, dpost1, dlb, dwon, loss_part, "pack_small")

    p_meta, p_small = _exchange([dmeta, small], ["scatter", "gather"], "exchange_grads")

    u_meta = _update_sharded(meta_tokens, p_meta, m_meta_tokens, v_meta_tokens, "update_meta")
    u_in_h = _update_sharded(hgrn_w_in[0], p_in_h, m_hgrn_w_in[0], v_hgrn_w_in[0], "update_hgrn_w_in")
    u_out_h = _update_sharded(hgrn_w_out[0], p_out_h, m_hgrn_w_out[0], v_hgrn_w_out[0], "update_hgrn_w_out")
    u_in_s = _update_sharded(sb_w_in[0], p_in_s, m_sb_w_in[0], v_sb_w_in[0], "update_sb_w_in")
    u_out_s = _update_sharded(sb_w_out[0], p_out_s, m_sb_w_out[0], v_sb_w_out[0], "update_sb_w_out")
    sm = _update_small(p_small, pre_norm, post_norm, hgrn_lb, hgrn_out_norm,
                       (m_pre_norm, m_post_norm, m_hgrn_lb, m_hgrn_out_norm,
                        v_pre_norm, v_post_norm, v_hgrn_lb, v_hgrn_out_norm), "update_small")
    loss = sm[0][0, 0]
    u_pre, u_post, u_lb, u_on = sm[1:5], sm[5:9], sm[9:13], sm[13:17]

    per_w = [u_meta, u_pre, u_post, tuple(a[None] for a in u_in_h), u_lb, u_on,
             tuple(a[None] for a in u_out_h), tuple(a[None] for a in u_in_s), tuple(a[None] for a in u_out_s)]
    outs = [loss, grad_x]
    for k in range(4):
        outs += [u[k] for u in per_w]
    return tuple(outs)
```

```python
import jax
import jax.numpy as jnp
from jax import lax
from jax.experimental import pallas as pl
from jax.experimental.pallas import tpu as pltpu

F32 = jnp.float32
BF16 = jnp.bfloat16

N_DEV = 8
BLOCK = 128
N_META = 16
N_PAD = BLOCK - N_META
HEAD = 128
SUB = 16
N_SUB = BLOCK // SUB
HALF = 8
N_HALF = SUB // HALF
EPS = 1e-6
ROW_TILE = 3 * BLOCK
UPDATE_ROWS = 256
K_TILES = (2816, 1408, 768, 384, 128)
NEG_BIG = -1e30

ADAM_LR = 0.001
ADAM_B1 = 0.9
ADAM_B2 = 0.999
ADAM_EPS = 1e-08
ADAM_WD = 0.01
ADAM_STEP = 10

VMEM_LIMIT = 56 * 1024 * 1024


def _pcall(body, **kw):
    return pl.pallas_call(body, **kw)


def _params(*sem):
    return pltpu.CompilerParams(dimension_semantics=sem, vmem_limit_bytes=VMEM_LIMIT)


def _dot(a, b):
    return jnp.dot(a, b, preferred_element_type=F32)


def _dot_nt(a, b):
    return lax.dot_general(a, b, (((1,), (1,)), ((), ())), preferred_element_type=F32)


def _dot_tn(a, b):
    return lax.dot_general(a, b, (((0,), (0,)), ((), ())), preferred_element_type=F32)


def _split(x, pieces):
    out = []
    for _ in range(pieces):
        p = x.astype(BF16)
        out.append(p)
        x = x - p.astype(F32)
    return out


def _tri_left(tri, x, pieces=3):
    return sum(_dot(tri, p) for p in _split(x, pieces))


def _iota2(shape, dim):
    return lax.broadcasted_iota(jnp.int32, shape, dim)


def _tri(cond):
    return jnp.where(cond, 1.0, 0.0).astype(BF16)


def _sig_pair(x):
    e = jnp.exp(-jnp.abs(x))
    r = 1.0 / (1.0 + e)
    er = e * r
    pos = x >= 0
    return jnp.where(pos, r, er), jnp.where(pos, er, r)


def _expand(rows):
    return jnp.concatenate([jnp.broadcast_to(r, (SUB, HEAD)) for r in rows], axis=0)


def _exchange_shapes(arrays, modes):
    return tuple(jax.ShapeDtypeStruct((N_DEV,) + tuple(a.shape[1:] if m == "scatter" else a.shape), a.dtype)
                 for a, m in zip(arrays, modes))


def _exchange_sems(n):
    if n == 0:
        return []
    return [pltpu.SemaphoreType.DMA((n, N_DEV - 1)), pltpu.SemaphoreType.DMA((n, N_DEV - 1)),
            pltpu.SemaphoreType.DMA((n,))]


def _exchange_copies(ins, outs, modes, send_sems, recv_sems, local_sems):
    mx, my, mc = lax.axis_index("x"), lax.axis_index("y"), lax.axis_index("c")
    me = 4 * mx + 2 * my + mc

    def src(i, slot):
        return ins[i].at[slot] if modes[i] == "scatter" else ins[i]

    def peer_of(mask):
        px = 1 - mx if mask & 4 else mx
        py = 1 - my if mask & 2 else my
        pc = 1 - mc if mask & 1 else mc
        return px, py, pc

    def copy(i, mask, dst_slot):
        px, py, pc = peer_of(mask)
        return pltpu.make_async_remote_copy(
            src_ref=src(i, 4 * px + 2 * py + pc), dst_ref=outs[i].at[dst_slot],
            send_sem=send_sems.at[i, mask - 1], recv_sem=recv_sems.at[i, mask - 1],
            device_id=(px, py, pc), device_id_type=pl.DeviceIdType.MESH)

    n = len(ins)
    sends = [copy(i, mask, me) for mask in range(1, N_DEV) for i in range(n)]
    own = [pltpu.make_async_copy(src(i, me), outs[i].at[me], local_sems.at[i]) for i in range(n)]
    arrivals = []
    for mask in range(1, N_DEV):
        px, py, pc = peer_of(mask)
        arrivals += [copy(i, mask, 4 * px + 2 * py + pc) for i in range(n)]

    def start():
        for cp in sends + own:
            cp.start()

    def wait():
        for cp in arrivals:
            cp.wait_recv()
        for cp in sends:
            cp.wait_send()
        for cp in own:
            cp.wait()

    return start, wait


def _gather_once_per_chip(arrays, name):
    n = len(arrays)

    def body(*refs):
        ins, outs = refs[:n], refs[n:2 * n]
        send_sems, recv_sems, local_sems = refs[2 * n:]
        mx, my, mc = lax.axis_index("x"), lax.axis_index("y"), lax.axis_index("c")
        me, sibling = (mx, my, mc), (mx, my, 1 - mc)
        chips = [(1 - mx, my), (mx, 1 - my), (1 - mx, 1 - my)]

        def slot(px, py, pc):
            return 4 * px + 2 * py + pc

        def copy(i, k, block, to, src=None):
            return pltpu.make_async_remote_copy(
                src_ref=outs[i].at[slot(*block)] if src is None else src, dst_ref=outs[i].at[slot(*block)],
                send_sem=send_sems.at[i, k], recv_sem=recv_sems.at[i, k],
                device_id=to, device_id_type=pl.DeviceIdType.MESH)

        own = [pltpu.make_async_copy(ins[i], outs[i].at[slot(*me)], local_sems.at[i]) for i in range(n)]
        first = [copy(i, 0, me, sibling, src=ins[i]) for i in range(n)]
        first += [copy(i, 1 + j, me, (*chip, mc), src=ins[i]) for j, chip in enumerate(chips) for i in range(n)]
        for cp in own + first:
            cp.start()
        passed = []
        for j, chip in enumerate(chips):
            for i in range(n):
                copy(i, 1 + j, (*chip, mc), me).wait_recv()
                cp = copy(i, 4 + j, (*chip, mc), sibling)
                cp.start()
                passed.append(cp)
        for i in range(n):
            copy(i, 0, sibling, me).wait_recv()
            for j, chip in enumerate(chips):
                copy(i, 4 + j, (*chip, 1 - mc), me).wait_recv()
        for cp in first + passed:
            cp.wait_send()
        for cp in own:
            cp.wait()

    any_spec = pl.BlockSpec(memory_space=pl.ANY)
    return _pcall(
        body, name=name, out_shape=_exchange_shapes(arrays, ["gather"] * n),
        in_specs=[any_spec] * n, out_specs=tuple([any_spec] * n),
        scratch_shapes=_exchange_sems(n),
    )(*arrays)


def _exchange(arrays, modes, name):
    n = len(arrays)

    def body(*refs):
        start, wait = _exchange_copies(refs[:n], refs[n:2 * n], modes, *refs[2 * n:])
        start()
        wait()

    any_spec = pl.BlockSpec(memory_space=pl.ANY)
    return _pcall(
        body, name=name, out_shape=_exchange_shapes(arrays, modes),
        in_specs=[any_spec] * n, out_specs=tuple([any_spec] * n),
        scratch_shapes=_exchange_sems(n),
    )(*arrays)


def _tile_blocks_of_x(seq, d):
    assert ROW_TILE == 3 * BLOCK and (seq + BLOCK) % ROW_TILE == 0
    per_seq = (seq + BLOCK) // ROW_TILE

    def spec(k):
        return pl.BlockSpec((None, BLOCK, d),
                            lambda i: (i // per_seq, jnp.maximum(3 * (i % per_seq) - 1 + k, 0), 0))

    return [spec(0), spec(1), spec(2)]


def _project(x, wn_ref, w_ref, main_ref, gate_ref, ynt_ref):
    p_n, _, s = w_ref.shape
    n_main = main_ref.shape[1] // s
    y = x * lax.rsqrt(jnp.mean(x * x, axis=-1, keepdims=True) + EPS) * wn_ref[...]
    yb = y.astype(BF16)
    ynt_ref[...] = y.T.astype(BF16)
    for p in range(p_n):
        r = _dot(yb, w_ref[p])
        if p < n_main:
            main_ref[:, p * s:(p + 1) * s] = r.astype(main_ref.dtype)
        else:
            gate_ref[:, (p - n_main) * s:(p - n_main + 1) * s] = r


def _projection_specs(t, d, w_all, main_dtype, tm):
    p_n, _, s = w_all.shape
    row = pl.BlockSpec((tm, d), lambda i: (i, 0))
    return ([pl.BlockSpec((1, d), lambda i: (0, 0)), pl.BlockSpec((p_n, d, s), lambda i: (0, 0, 0))],
            (pl.BlockSpec((tm, 3 * d), lambda i: (i, 0)), row, pl.BlockSpec((d, tm), lambda i: (0, i))),
            (jax.ShapeDtypeStruct((t, 3 * d), main_dtype), jax.ShapeDtypeStruct((t, d), F32),
             jax.ShapeDtypeStruct((d, t), BF16)))


def _norm_inproj(h, wnorm, w_all, main_dtype, name, from_x=None):
    p_n, d, s = w_all.shape
    n_main = 3 * d // s
    tm = ROW_TILE
    if from_x is None:
        t = h.shape[0]
        lead, lead_specs = [h], [pl.BlockSpec((tm, d), lambda i: (i, 0))]
    else:
        x_in, meta = from_x
        t = x_in.shape[0] * (x_in.shape[1] + BLOCK)
        tiles_per_seq = (x_in.shape[1] + BLOCK) // tm
        lead = [x_in, x_in, x_in, meta]
        lead_specs = _tile_blocks_of_x(x_in.shape[1], d) + [pl.BlockSpec((N_META, d), lambda i: (0, 0))]
    n_lead = len(lead)

    def body(*refs):
        wn_ref, w_ref = refs[n_lead:n_lead + 2]
        main_ref, gate_ref, ynt_ref = refs[n_lead + 2:n_lead + 5]
        if from_x is None:
            x = refs[0][...]
        else:
            first_tile = pl.program_id(0) % tiles_per_seq == 0
            meta_block = jnp.concatenate([jnp.zeros((N_PAD, d), F32), refs[3][...]], axis=0)
            x = jnp.concatenate([jnp.where(first_tile, meta_block, refs[0][...]), refs[1][...], refs[2][...]], axis=0)
            refs[n_lead + 5][...] = x
        _project(x, wn_ref, w_ref, main_ref, gate_ref, ynt_ref)

    row = pl.BlockSpec((tm, d), lambda i: (i, 0))
    extra_specs, extra_shapes = ((), ()) if from_x is None else ((row,), (jax.ShapeDtypeStruct((t, d), F32),))
    proj_in, proj_out, proj_shapes = _projection_specs(t, d, w_all, main_dtype, tm)
    return _pcall(
        body, name=name, grid=(t // tm,),
        in_specs=lead_specs + proj_in, out_specs=proj_out + extra_specs, out_shape=proj_shapes + extra_shapes,
        compiler_params=_params("parallel"),
    )(*lead, wnorm, w_all)


def _mix_out(o, gate, h_in, w_out, w_on, w_post, head_norm, name, target=None, next_proj=None):
    t, d = o.shape
    n_heads = d // HEAD
    tm = ROW_TILE
    with_loss = target is not None
    if with_loss:
        tiles_per_seq = (target.shape[1] + BLOCK) // tm

    def body(o_ref, g_ref, h_ref, w_ref, won_ref, wp_ref, *rest):
        if with_loss:
            t0_ref, t1_ref, t2_ref, hout_ref, y_ref, loss_ref, u_scr = rest
        elif next_proj is not None:
            wn2_ref, w2_ref, hout_ref, y_ref, main_ref, gate_ref, ynt_ref, u_scr = rest
        else:
            hout_ref, y_ref, u_scr = rest
        for hh in range(n_heads):
            cs = slice(hh * HEAD, (hh + 1) * HEAD)
            oh = o_ref[:, cs]
            gt = g_ref[:, cs]
            if head_norm:
                oh = oh * lax.rsqrt(jnp.mean(oh * oh, axis=-1, keepdims=True) + EPS) * won_ref[...]
            u_scr[:, cs] = (oh * (gt * jax.nn.sigmoid(gt))).astype(BF16)
        y = _dot(u_scr[...], w_ref[...])
        y_ref[...] = y
        r = y * lax.rsqrt(jnp.mean(y * y, axis=-1, keepdims=True) + EPS) * wp_ref[...]
        h_out = h_ref[...] + r
        if not with_loss:
            hout_ref[...] = h_out
            if next_proj is not None:
                _project(h_out, wn2_ref, w2_ref, main_ref, gate_ref, ynt_ref)
            return
        i = pl.program_id(0)

        @pl.when(i == 0)
        def _():
            loss_ref[...] = jnp.zeros_like(loss_ref)

        tgt = jnp.concatenate([t0_ref[...], t1_ref[...], t2_ref[...]], axis=0)
        real = jnp.logical_or(i % tiles_per_seq > 0, _iota2((tm, d), 0) >= BLOCK)
        err = jnp.where(real, h_out - tgt, 0.0)
        hout_ref[...] = err * (1.0 / d)
        part = jnp.sum(jnp.sum(err * err, axis=-1, keepdims=True), axis=0, keepdims=True)
        loss_ref[...] += part * (0.5 / d)

    row = pl.BlockSpec((tm, d), lambda i: (i, 0))
    in_specs = [row, row, row, pl.BlockSpec((d, d), lambda i: (0, 0)),
                pl.BlockSpec((1, HEAD), lambda i: (0, 0)), pl.BlockSpec((1, d), lambda i: (0, 0))]
    out_specs, out_shape = (row, row), (jax.ShapeDtypeStruct((t, d), F32), jax.ShapeDtypeStruct((t, d), F32))
    args = (o, gate, h_in, w_out, w_on, w_post)
    if with_loss:
        in_specs += _tile_blocks_of_x(target.shape[1], d)
        out_specs += (pl.BlockSpec((8, 128), lambda i: (0, 0)),)
        out_shape += (jax.ShapeDtypeStruct((8, 128), F32),)
        args += (target, target, target)
    elif next_proj is not None:
        proj_in, proj_out, proj_shapes = _projection_specs(t, d, next_proj[1], next_proj[2], tm)
        in_specs += proj_in
        out_specs += proj_out
        out_shape += proj_shapes
        args += (next_proj[0], next_proj[1])
    return _pcall(
        body, name=name, grid=(t // tm,), in_specs=in_specs, out_specs=out_specs, out_shape=out_shape,
        scratch_shapes=[pltpu.VMEM((tm, d), BF16)],
        compiler_params=_params("arbitrary" if with_loss else "parallel"),
    )(*args)


def _mix_out_bwd_tile(dr, y_ref, o_ref, g_ref, w_ref, won_ref, wp_ref,
                      do_ref, dg_ref, dw_ref, dwp_ref, dwon_ref, u_scr, acc, head_norm, last):
    i = pl.program_id(0)
    n_heads = o_ref.shape[1] // HEAD

    @pl.when(i == 0)
    def _():
        acc[...] = jnp.zeros_like(acc)
        dwp_ref[...] = jnp.zeros_like(dwp_ref)
        dwon_ref[...] = jnp.zeros_like(dwon_ref)

    yv = y_ref[...]
    rs = lax.rsqrt(jnp.mean(yv * yv, axis=-1, keepdims=True) + EPS)
    yh = yv * rs
    dwp_ref[...] += jnp.sum(dr * yh, axis=0, keepdims=True)
    wd = dr * wp_ref[...]
    dy = rs * (wd - yh * jnp.mean(wd * yh, axis=-1, keepdims=True))
    dyb = dy.astype(BF16)
    du = _dot_nt(dyb, w_ref[...])
    for hh in range(n_heads):
        cs = slice(hh * HEAD, (hh + 1) * HEAD)
        oh = o_ref[:, cs]
        gt = g_ref[:, cs]
        sg = jax.nn.sigmoid(gt)
        sl = gt * sg
        duh = du[:, cs]
        if head_norm:
            rsh = lax.rsqrt(jnp.mean(oh * oh, axis=-1, keepdims=True) + EPS)
            ohat = oh * rsh
            on = ohat * won_ref[...]
        else:
            on = oh
        u_scr[:, cs] = (on * sl).astype(BF16)
        don = duh * sl
        dg_ref[:, cs] = (duh * on * (sg * (1.0 + gt * (1.0 - sg)))).astype(BF16)
        if head_norm:
            dwon_ref[...] += jnp.sum(don * ohat, axis=0, keepdims=True)
            wdn = don * won_ref[...]
            do_ref[:, cs] = rsh * (wdn - ohat * jnp.mean(wdn * ohat, axis=-1, keepdims=True))
        else:
            do_ref[:, cs] = don
    acc[...] += _dot_tn(u_scr[...], dyb)

    @pl.when(i == last)
    def _():
        dw_ref[...] = acc[...].astype(BF16)


def _mix_out_bwd_outputs(t, d, tm):
    row = pl.BlockSpec((tm, d), lambda i: (i, 0))
    specs = (row, row, pl.BlockSpec((d, d), lambda i: (0, 0)), pl.BlockSpec((1, d), lambda i: (0, 0)),
             pl.BlockSpec((1, HEAD), lambda i: (0, 0)))
    shapes = (jax.ShapeDtypeStruct((t, d), F32), jax.ShapeDtypeStruct((t, d), BF16),
              jax.ShapeDtypeStruct((d, d), BF16), jax.ShapeDtypeStruct((1, d), F32),
              jax.ShapeDtypeStruct((1, HEAD), F32))
    return specs, shapes, [pltpu.VMEM((tm, d), BF16), pltpu.VMEM((d, d), F32)]


def _mix_out_bwd(dh, y, o, gate, w_out, w_on, w_post, head_norm, name):
    t, d = o.shape
    tm = ROW_TILE
    last = t // tm - 1

    def body(dh_ref, *refs):
        _mix_out_bwd_tile(dh_ref[...], *refs, head_norm, last)

    row = pl.BlockSpec((tm, d), lambda i: (i, 0))
    out_specs, out_shape, scratch = _mix_out_bwd_outputs(t, d, tm)
    return _pcall(
        body, name=name, grid=(t // tm,),
        in_specs=[row, row, row, row, pl.BlockSpec((d, d), lambda i: (0, 0)),
                  pl.BlockSpec((1, HEAD), lambda i: (0, 0)), pl.BlockSpec((1, d), lambda i: (0, 0))],
        out_specs=out_specs, out_shape=out_shape, scratch_shapes=scratch,
        compiler_params=_params("arbitrary"),
    )(dh, y, o, gate, w_out, w_on, w_post)


def _inproj_bwd_x(dparts, w_all, h, wnorm, dres, ride, ride_modes, name, then_mix_bwd=None):
    t, d = h.shape
    p_n, _, s = w_all.shape
    per = d // s
    tm = ROW_TILE
    nr = len(ride)
    grid = (t // tm,)
    n_mix = 0 if then_mix_bwd is None else 6
    assert not (n_mix and nr)

    def body(*refs):
        d0_ref, d1_ref, d2_ref, d3_ref, w_ref, h_ref, wn_ref, dres_ref = refs[:8]
        n_in = 8 + n_mix + nr
        dh_ref, dwn_ref = refs[n_in:n_in + 2]
        finish_ride = _ride_along(refs[8:8 + nr], refs[n_in + 2:n_in + 2 + nr], ride_modes,
                                  refs[n_in + 2 + nr:], grid) if nr else None
        i = pl.program_id(0)

        @pl.when(i == 0)
        def _():
            dwn_ref[...] = jnp.zeros_like(dwn_ref)

        pieces = (d0_ref, d1_ref, d2_ref, d3_ref)
        dyn = jnp.zeros((tm, d), F32)
        for p in range(p_n):
            blk = pieces[p // per][:, (p % per) * s:(p % per + 1) * s]
            dyn = dyn + _dot_nt(blk, w_ref[p])
        x = h_ref[...]
        rs = lax.rsqrt(jnp.mean(x * x, axis=-1, keepdims=True) + EPS)
        xh = x * rs
        dwn_ref[...] += jnp.sum(dyn * xh, axis=0, keepdims=True)
        wd = dyn * wn_ref[...]
        dh = dres_ref[...] + rs * (wd - xh * jnp.mean(wd * xh, axis=-1, keepdims=True))
        dh_ref[...] = dh
        if n_mix:
            _mix_out_bwd_tile(dh, *refs[8:8 + n_mix], *refs[n_in + 2:], then_mix_bwd[6], grid[0] - 1)
        if nr:
            finish_ride()

    row = pl.BlockSpec((tm, d), lambda i: (i, 0))
    any_spec = pl.BlockSpec(memory_space=pl.ANY)
    single = dict(pipeline_mode=pl.Buffered(1)) if n_mix else {}
    in_specs = [row, row, row, row, pl.BlockSpec((p_n, d, s), lambda i: (0, 0, 0), **single),
                row, pl.BlockSpec((1, d), lambda i: (0, 0)), row]
    out_specs = (row, pl.BlockSpec((1, d), lambda i: (0, 0)))
    out_shape = (jax.ShapeDtypeStruct((t, d), F32), jax.ShapeDtypeStruct((1, d), F32))
    args = (*dparts, w_all, h, wnorm, dres)
    if n_mix:
        in_specs += [row, row, row, pl.BlockSpec((d, d), lambda i: (0, 0), **single),
                     pl.BlockSpec((1, HEAD), lambda i: (0, 0)), pl.BlockSpec((1, d), lambda i: (0, 0))]
        mix_specs, mix_shapes, scratch = _mix_out_bwd_outputs(t, d, tm)
        out_specs += mix_specs
        out_shape += mix_shapes
        args += tuple(then_mix_bwd[:6])
    else:
        in_specs += [any_spec] * nr
        out_specs += (any_spec,) * nr
        out_shape += _exchange_shapes(ride, ride_modes)
        scratch = _exchange_sems(nr)
        args += tuple(ride)
    return _pcall(
        body, name=name, grid=grid, in_specs=in_specs, out_specs=out_specs, out_shape=out_shape,
        scratch_shapes=scratch, compiler_params=_params("arbitrary"),
    )(*args)


def _inproj_bwd_w(ynt, dparts, s, name):
    d, t = ynt.shape
    per = d // s
    n_sh = len(dparts) * per
    tk = next(c for c in K_TILES if t % c == 0)
    last = t // tk - 1

    def body(ynt_ref, d0_ref, d1_ref, d2_ref, d3_ref, dw_ref, acc):
        p, i = pl.program_id(0), pl.program_id(1)

        @pl.when(i == 0)
        def _():
            acc[...] = jnp.zeros_like(acc)

        for a, piece in enumerate((d0_ref, d1_ref, d2_ref, d3_ref)):
            @pl.when(p // per == a)
            def _():
                acc[...] += _dot(ynt_ref[...], piece[...])

        @pl.when(i == last)
        def _():
            dw_ref[...] = acc[...].astype(BF16)

    def piece_spec(a):
        return pl.BlockSpec((tk, s), lambda p, i: (jnp.where(p // per == a, i, 0),
                                                   jnp.where(p // per == a, p % per, 0)))

    return _pcall(
        body, name=name, grid=(n_sh, t // tk),
        in_specs=[pl.BlockSpec((d, tk), lambda p, i: (0, i))] + [piece_spec(a) for a in range(4)],
        out_specs=pl.BlockSpec((None, d, s), lambda p, i: (p, 0, 0)),
        out_shape=jax.ShapeDtypeStruct((n_sh, d, s), BF16),
        scratch_shapes=[pltpu.VMEM((d, s), F32)],
        compiler_params=_params("parallel", "arbitrary"),
    )(ynt, *dparts)


def _hgrn_gates(fz_ref, lb, b_ref, k_ref):
    sig, nsig = _sig_pair(fz_ref[...])
    f = lb + (1.0 - lb) * sig
    rr, cc = _iota2((BLOCK, BLOCK), 0), _iota2((BLOCK, BLOCK), 1)
    b_ref[...] = _tri_left(_tri(cc <= rr), jnp.log(f))
    k_ref[...] = (1.0 - lb) * nsig


def _hgrn_common(q_ref, b_ref, k_ref):
    b, k = b_ref[...], k_ref[...]
    bend = [b_ref[pl.ds(SUB * j + SUB - 1, 1), :] for j in range(N_SUB)]
    bref = [jnp.zeros((1, HEAD), F32)] + bend[:-1]
    refrow, bendrow = _expand(bref), _expand(bend)
    e_q = jnp.exp(b - refrow)
    e_k = jnp.exp(bendrow - b)
    qt = q_ref[...] * e_q
    kh = k * e_k
    bl = bend[-1]
    return dict(k=k, b=b, bend=bend, bref=bref, refrow=refrow, e_q=e_q, e_k=e_k, qt=qt, kh=kh, bl=bl)


HEADS_PER_STEP = 8


def _lockstep(chunks):
    live = list(chunks)
    while live:
        still = []
        for gen in live:
            try:
                next(gen)
                still.append(gen)
            except StopIteration:
                pass
        live = still


def _ride_along(ride_ins, ride_outs, modes, sems, grid):
    if not ride_ins:
        return lambda: None
    ids = [pl.program_id(a) for a in range(len(grid))]
    start, wait = _exchange_copies(ride_ins, ride_outs, modes, *sems)
    first, last = ids[0] == 0, ids[0] == grid[0] - 1
    for a in range(1, len(grid)):
        first = jnp.logical_and(first, ids[a] == 0)
        last = jnp.logical_and(last, ids[a] == grid[a] - 1)
    pl.when(first)(start)
    return lambda: pl.when(last)(wait)


def _hgrn_fwd(main, lbrow, bsz, n_blocks, ride, ride_modes, name):
    t, d3 = main.shape
    d = d3 // 3
    n_heads = d // HEAD
    n_pairs = n_heads // HEADS_PER_STEP
    wide = HEADS_PER_STEP * HEAD
    nr = len(ride)
    grid = (bsz, n_pairs, n_blocks)

    def chunk(q_ref, fz_ref, v_ref, lb, o_ref, st_ref, b_scr, k_scr, ad_ref, s_scr, o_acc, ad_scr):
        _hgrn_gates(fz_ref, lb, b_scr, k_scr)
        yield
        c = _hgrn_common(q_ref, b_scr, k_scr)
        yield
        s_t = s_scr[...]
        st_ref[...] = s_t
        vb = v_ref[...].astype(BF16)
        q_state = c["qt"] * _expand([jnp.exp(r) for r in c["bref"]])
        o_state = _dot_nt(q_state.astype(BF16), s_t.astype(BF16))
        js = range(N_SUB - 1)
        khb = c["kh"].astype(BF16)
        lhs = [(c["qt"][SUB * (j + 1):] * jnp.exp(c["refrow"][SUB * (j + 1):] - c["bend"][j])).astype(BF16)
               for j in js]
        yield
        a_js = [_dot_nt(lhs[j], khb[SUB * j:SUB * (j + 1)]) for j in js]
        k_state = c["kh"] * _expand([jnp.exp(c["bl"] - r) for r in c["bend"]])
        s_scr[...] = s_t * jnp.exp(c["bl"]) + _dot_tn(vb, k_state.astype(BF16))
        yield
        o_js = [_dot(a_js[j].astype(BF16), vb[SUB * j:SUB * (j + 1)]) for j in js]
        yield
        o_acc[...] = o_state
        for j in js:
            o_acc[SUB * (j + 1):, :] += o_js[j]
        not_before = [_iota2((HALF, HEAD), 0) >= row for row in range(HALF)]
        lane = _iota2((HALF, HEAD), 1)
        for i in range(N_SUB):
            r0 = SUB * i
            q_h = [q_ref[pl.ds(r0 + HALF * u, HALF), :] for u in range(N_HALF)]
            b_h = [b_scr[pl.ds(r0 + HALF * u, HALF), :] for u in range(N_HALF)]
            a_h = [jnp.zeros((HALF, HEAD), F32) for _ in range(N_HALF)]
            for s in range(SUB):
                brow = b_scr[pl.ds(r0 + s, 1), :]
                krow = k_scr[pl.ds(r0 + s, 1), :]
                is_lane = lane == r0 + s
                for u in range(s // HALF, N_HALF):
                    diff = b_h[u] - brow
                    if u == s // HALF:
                        diff = jnp.where(not_before[s - HALF * u], diff, NEG_BIG)
                    col = jnp.sum(q_h[u] * krow * jnp.exp(diff), axis=-1, keepdims=True)
                    a_h[u] = jnp.where(is_lane, col, a_h[u])
            for u in range(N_HALF):
                ad_scr[pl.ds(r0 + HALF * u, HALF), :] = a_h[u]
            yield
        a_diag = ad_scr[...].astype(BF16)
        ad_ref[...] = a_diag
        o_ref[...] = o_acc[...] + _dot(a_diag, vb)

    def body(*refs):
        q_ref, fz_ref, v_ref, lb_ref = refs[:4]
        o_ref, st_ref, b_out, k_out, ad_out = refs[4 + nr:9 + nr]
        s_scr, o_acc, ad_scr = refs[9 + 2 * nr:12 + 2 * nr]
        finish_ride = _ride_along(refs[4:4 + nr], refs[9 + nr:9 + 2 * nr], ride_modes, refs[12 + 2 * nr:], grid)

        @pl.when(pl.program_id(2) == 0)
        def _():
            s_scr[...] = jnp.zeros_like(s_scr)

        def head(hh):
            cols = pl.ds(hh * HEAD, HEAD)
            return chunk(q_ref.at[:, cols], fz_ref.at[:, cols], v_ref.at[:, cols], lb_ref[:, cols],
                         o_ref.at[:, cols], st_ref.at[hh], b_out.at[:, cols], k_out.at[:, cols],
                         ad_out.at[:, cols], s_scr.at[hh], o_acc.at[hh], ad_scr.at[hh])

        _lockstep([head(hh) for hh in range(HEADS_PER_STEP)])
        finish_ride()

    def blk(col0):
        return pl.BlockSpec((BLOCK, wide), lambda b, h, n: (b * n_blocks + n, col0 + h))

    any_spec = pl.BlockSpec(memory_space=pl.ANY)
    per_head = pltpu.VMEM((HEADS_PER_STEP, BLOCK, HEAD), F32)
    return _pcall(
        body, name=name, grid=grid,
        in_specs=[blk(0), blk(n_pairs), blk(2 * n_pairs), pl.BlockSpec((1, wide), lambda b, h, n: (0, h))]
        + [any_spec] * nr,
        out_specs=(blk(0), pl.BlockSpec((None, HEADS_PER_STEP, HEAD, HEAD), lambda b, h, n: (b * n_blocks + n, h, 0, 0)),
                   blk(0), blk(0), blk(0)) + (any_spec,) * nr,
        out_shape=(jax.ShapeDtypeStruct((t, d), F32),
                   jax.ShapeDtypeStruct((bsz * n_blocks, n_heads, HEAD, HEAD), F32),
                   jax.ShapeDtypeStruct((t, d), F32), jax.ShapeDtypeStruct((t, d), F32),
                   jax.ShapeDtypeStruct((t, d), BF16))
        + _exchange_shapes(ride, ride_modes),
        scratch_shapes=[per_head] * 3 + _exchange_sems(nr),
        compiler_params=_params("arbitrary", "arbitrary", "arbitrary"),
    )(main, main, main, lbrow, *ride)


def _hgrn_bwd(main, b_all, k_all, a_diag, lbrow, states, do, bsz, n_blocks, ride, ride_modes, name):
    t, d3 = main.shape
    d = d3 // 3
    n_pairs = d // HEAD // HEADS_PER_STEP
    wide = HEADS_PER_STEP * HEAD
    nr = len(ride)
    grid = (bsz, n_pairs, n_blocks)

    def chunk(n, q_ref, b_scr, k_scr, ad_ref, v_ref, lb, st_ref, do_ref, dq_ref, dfz_ref, dv_ref, dlb_ref,
              ds_scr, dqt_acc, dkh_acc, dv_acc, dqd_acc, dkd_acc):
        c = _hgrn_common(q_ref, b_scr, k_scr)
        yield
        q, k = q_ref[...], c["k"]
        vb = v_ref[...].astype(BF16)
        dob = do_ref[...].astype(BF16)
        s0_t = st_ref[...]
        ds1_t = ds_scr[...]
        e_ref = _expand([jnp.exp(r) for r in c["bref"]])
        e_end = _expand([jnp.exp(c["bl"] - r) for r in c["bend"]])
        e_bl = jnp.exp(c["bl"])
        q_state = c["qt"] * e_ref
        k_state = c["kh"] * e_end
        dq_state = _dot(dob, s0_t.astype(BF16))
        dk_state = _dot(vb, ds1_t.astype(BF16))
        dv_state = _dot_nt(k_state.astype(BF16), ds1_t.astype(BF16))
        ds_scr[...] = ds1_t * e_bl + _dot_tn(dob, q_state.astype(BF16))
        js = range(N_SUB - 1)
        lo = [slice(SUB * j, SUB * (j + 1)) for j in js]
        khb = c["kh"].astype(BF16)
        dj = [jnp.exp(c["refrow"][SUB * (j + 1):] - c["bend"][j]) for j in js]
        lhs = [(c["qt"][SUB * (j + 1):] * dj[j]).astype(BF16) for j in js]
        yield
        a_js = [_dot_nt(lhs[j], khb[lo[j]]) for j in js]
        da_js = [_dot_nt(dob[SUB * (j + 1):], vb[lo[j]]).astype(BF16) for j in js]
        dqt_acc[...] = dq_state * e_ref
        dkh_acc[...] = dk_state * e_end
        dv_acc[...] = dv_state
        dbl = (jnp.sum(s0_t * ds1_t, axis=0, keepdims=True) * e_bl
               + jnp.sum(k_state * dk_state, axis=0, keepdims=True))
        yield
        dv_js = [_dot_tn(a_js[j].astype(BF16), dob[SUB * (j + 1):]) for j in js]
        dq_js = [_dot(da_js[j], khb[lo[j]]) * dj[j] for j in js]
        dk_js = [_dot_tn(da_js[j], lhs[j]) for j in js]
        yield
        for j in js:
            dv_acc[lo[j], :] += dv_js[j]
            dqt_acc[SUB * (j + 1):, :] += dq_js[j]
            dkh_acc[lo[j], :] += dk_js[j]
        not_before = [_iota2((HALF, HEAD), 0) >= row for row in range(HALF)]
        is_row = [_iota2((HALF, HEAD), 0) == row for row in range(HALF)]
        for i in range(N_SUB):
            r0 = SUB * i
            q_h = [q_ref[pl.ds(r0 + HALF * u, HALF), :] for u in range(N_HALF)]
            b_h = [b_scr[pl.ds(r0 + HALF * u, HALF), :] for u in range(N_HALF)]
            do_h = [do_ref[pl.ds(r0 + HALF * u, HALF), :] for u in range(N_HALF)]
            zero = jnp.zeros((HALF, HEAD), F32)
            dq_h, dk_h = [zero] * N_HALF, [zero] * N_HALF
            for s in range(SUB):
                brow = b_scr[pl.ds(r0 + s, 1), :]
                krow = k_scr[pl.ds(r0 + s, 1), :]
                vrow = v_ref[pl.ds(r0 + s, 1), :]
                dk_row = jnp.zeros((1, HEAD), F32)
                for u in range(s // HALF, N_HALF):
                    diff = b_h[u] - brow
                    if u == s // HALF:
                        diff = jnp.where(not_before[s - HALF * u], diff, NEG_BIG)
                    w = jnp.exp(diff)
                    qw = q_h[u] * w
                    da_col = jnp.sum(do_h[u] * vrow, axis=-1, keepdims=True)
                    dq_h[u] = dq_h[u] + da_col * (w * krow)
                    dk_row = dk_row + jnp.sum(da_col * qw, axis=0, keepdims=True)
                us = s // HALF
                dk_h[us] = jnp.where(is_row[s - HALF * us], dk_row, dk_h[us])
            for u in range(N_HALF):
                rows = pl.ds(r0 + HALF * u, HALF)
                dqd_acc[rows, :] = dq_h[u]
                dkd_acc[rows, :] = dk_h[u]
            yield
        dv_in = _dot_tn(ad_ref[...], dob)
        dq = dqt_acc[...] * c["e_q"] + dqd_acc[...]
        dk = dkh_acc[...] * c["e_k"] + dkd_acc[...]
        rr, cc = _iota2((BLOCK, BLOCK), 0), _iota2((BLOCK, BLOCK), 1)
        db = q * dq - k * dk + jnp.where(_iota2((BLOCK, HEAD), 0) == BLOCK - 1, dbl, 0.0)
        yield
        dg = _tri_left(_tri(cc >= rr), db)
        yield
        real = jnp.logical_or(n > 0, _iota2((BLOCK, HEAD), 0) >= N_PAD)
        df = jnp.where(real, dg / (1.0 - k) - dk, 0.0)
        nsig = k * (1.0 / (1.0 - lb))
        dq_ref[...] = dq.astype(BF16)
        dv_ref[...] = (dv_acc[...] + dv_in).astype(BF16)
        dfz_ref[...] = (df * k * (1.0 - nsig)).astype(BF16)
        dlb_ref[...] += jnp.sum(df * nsig, axis=0, keepdims=True)

    def body(*refs):
        q_ref, b_ref, k_ref, ad_ref, v_ref, lb_ref, st_ref, do_ref = refs[:8]
        dq_ref, dfz_ref, dv_ref, dlb_ref = refs[8 + nr:12 + nr]
        scratch = refs[12 + 2 * nr:18 + 2 * nr]
        finish_ride = _ride_along(refs[8:8 + nr], refs[12 + nr:12 + 2 * nr], ride_modes, refs[18 + 2 * nr:], grid)
        step = pl.program_id(2)

        @pl.when(step == 0)
        def _():
            scratch[0][...] = jnp.zeros_like(scratch[0])
            dlb_ref[...] = jnp.zeros_like(dlb_ref)

        def head(hh):
            cols = pl.ds(hh * HEAD, HEAD)
            return chunk(n_blocks - 1 - step, q_ref.at[:, cols], b_ref.at[:, cols], k_ref.at[:, cols],
                         ad_ref.at[:, cols], v_ref.at[:, cols], lb_ref[:, cols], st_ref.at[hh], do_ref.at[:, cols],
                         dq_ref.at[:, cols],
                         dfz_ref.at[:, cols], dv_ref.at[:, cols], dlb_ref.at[:, cols],
                         *[scr.at[hh] for scr in scratch])

        _lockstep([head(hh) for hh in range(HEADS_PER_STEP)])
        finish_ride()

    def blk(col0):
        return pl.BlockSpec((BLOCK, wide), lambda b, h, s: (b * n_blocks + n_blocks - 1 - s, col0 + h))

    any_spec = pl.BlockSpec(memory_space=pl.ANY)
    per_head = pltpu.VMEM((HEADS_PER_STEP, BLOCK, HEAD), F32)
    return _pcall(
        body, name=name, grid=grid,
        in_specs=[blk(0), blk(0), blk(0), blk(0), blk(2 * n_pairs), pl.BlockSpec((1, wide), lambda b, h, s: (0, h)),
                  pl.BlockSpec((None, HEADS_PER_STEP, HEAD, HEAD),
                               lambda b, h, s: (b * n_blocks + n_blocks - 1 - s, h, 0, 0)),
                  blk(0)] + [any_spec] * nr,
        out_specs=(blk(0), blk(0), blk(0), pl.BlockSpec((None, 1, wide), lambda b, h, s: (b, 0, h)))
        + (any_spec,) * nr,
        out_shape=(jax.ShapeDtypeStruct((t, d), BF16),) * 3 + (jax.ShapeDtypeStruct((bsz, 1, d), F32),)
        + _exchange_shapes(ride, ride_modes),
        scratch_shapes=[per_head] * 6 + _exchange_sems(nr),
        compiler_params=_params("arbitrary", "arbitrary", "arbitrary"),
    )(main, b_all, k_all, a_diag, main, lbrow, states, do, *ride)


SB_GROUP = 3
SB_ROWS = SB_GROUP * BLOCK
SB_DEAD = -104.0
SB_NEAR = ((0, 0), (1, 0), (0, 1))
SB_HEADS_FWD = 4
SB_HEADS_BWD = 2
SB_STATIC_OWN = tuple((r, kb) for kb in range(SB_GROUP - 1, -1, -1) for r in range(kb, SB_GROUP))
SB_STATIC_NEAR = tuple((r, -1 - back) for r, back in SB_NEAR)
N_STATIC = len(SB_STATIC_OWN) + len(SB_STATIC_NEAR)


def _sb_tables():
    j = jnp.bitwise_and(_iota2((2 * BLOCK, 2 * BLOCK), 0), BLOCK - 1)
    s = _iota2((2 * BLOCK, 2 * BLOCK), 1)
    ones = s >= BLOCK
    return (_tri(jnp.logical_or(ones, j > s)), _tri(jnp.logical_or(ones, j <= s)),
            _tri(jnp.logical_or(ones, j < s)))


def _sums(x, table, pieces=2):
    hi = x.astype(BF16)
    if pieces == 1:
        r = _dot(hi, table[:BLOCK])
    else:
        lo = (x - hi.astype(F32)).astype(BF16)
        r = _dot(jnp.concatenate([hi, lo], axis=1), table)
    return r[:, :BLOCK], r[:, BLOCK:]


def _sb_logits(q, ks, scale, causal, pad_row):
    z = _dot_nt(q, ks) * scale
    log_keep = -(jnp.maximum(z, 0.0) + jnp.log(1.0 + jnp.exp(-jnp.abs(z))))
    log_beta = z + log_keep
    if causal is not None:
        log_keep = jnp.where(causal, log_keep, 0.0)
    if pad_row is not None:
        log_keep = log_keep * pad_row
    return z, log_beta, log_keep


def _sb_fwd(qkv, bsz, n_blocks, name):
    t, d3 = qkv.shape
    d = d3 // 3
    n_heads = d // HEAD
    lp = n_blocks * BLOCK
    n_groups = n_blocks // SB_GROUP
    assert n_groups * SB_GROUP == n_blocks
    scale = HEAD ** -0.5

    n_pairs = n_heads // SB_HEADS_FWD
    wide = SB_HEADS_FWD * HEAD

    def body(q_ref, k_ref, v_ref, upper_ref, o_ref, tot_ref, stop_ref, w_keep, lb_keep, c_scr):
        g = pl.program_id(2)
        upper = upper_ref[...]
        causal = _iota2((BLOCK, BLOCK), 1) < _iota2((BLOCK, BLOCK), 0)
        lane = _iota2((1, BLOCK), 1)
        o_ref[...] = jnp.zeros_like(o_ref)
        c_scr[...] = jnp.zeros_like(c_scr)

        def tiles(jobs, keep=None):
            cols = [pl.ds(hh * HEAD, HEAD) for hh, _, _, _, _ in jobs]
            qrows = [pl.ds(r * BLOCK, BLOCK) for _, r, _, _, _ in jobs]
            krows = [pl.ds(pl.multiple_of(m * BLOCK, BLOCK), BLOCK) for _, _, m, _, _ in jobs]
            n_jobs = range(len(jobs))
            lg = [_sb_logits(q_ref[qrows[i], cols[i]], k_ref[krows[i], cols[i]], scale,
                             causal if jobs[i][3] else None, jobs[i][4]) for i in n_jobs]
            sm = [_sums(x[2], upper) for x in lg]
            a_all = []
            for i in n_jobs:
                c = c_scr[qrows[i], cols[i]]
                a = jnp.exp(lg[i][1] + c + sm[i][0])
                a_all.append((jnp.where(causal, a, 0.0) if jobs[i][3] else a).astype(BF16))
                c_scr[qrows[i], cols[i]] = c + sm[i][1]
            out = [_dot(a_all[i], v_ref[krows[i], cols[i]]) for i in n_jobs]
            for i in n_jobs:
                o_ref[qrows[i], cols[i]] += out[i]
                if keep is not None:
                    w_keep[jobs[i][0], keep[i]] = a_all[i]
                    lb_keep[jobs[i][0], keep[i]] = lg[i][1].astype(BF16)

        def pad_row_of(m):
            return jnp.where(jnp.logical_or(m > 0, lane >= N_PAD), 1.0, 0.0)

        base = SB_GROUP * g
        heads = range(SB_HEADS_FWD)
        own = [(hh, r, base + rel, r == rel, pad_row_of(base) if rel == 0 else None)
               for hh in heads for r, rel in SB_STATIC_OWN]
        near = [(hh, r, base + rel, False, None) for hh in heads for r, rel in SB_STATIC_NEAR]
        own_slots = [i for _ in heads for i in range(len(SB_STATIC_OWN))]
        near_slots = [len(SB_STATIC_OWN) + i for _ in heads for i in range(len(SB_STATIC_NEAR))]
        pl.when(g == 0)(lambda: tiles(own, keep=own_slots))
        pl.when(g > 0)(lambda: tiles(own + near, keep=own_slots + near_slots))

        stop_ref[...] = jnp.zeros_like(stop_ref)
        mine = [(hh, r) for hh in heads for r in range(SB_GROUP)]
        reach = [jnp.max(c_scr[pl.ds(r * BLOCK, BLOCK), pl.ds(hh * HEAD, HEAD)]) for hh, r in mine]
        for (hh, r), reach_now in zip(mine, reach):
            def live(carry):
                m, c_max = carry
                return jnp.logical_and(m >= 0, c_max >= SB_DEAD)

            def step(carry, hh=hh, r=r):
                m, _ = carry
                tiles([(hh, r, m, False, pad_row_of(m))])
                return m - 1, jnp.max(c_scr[pl.ds(r * BLOCK, BLOCK), pl.ds(hh * HEAD, HEAD)])

            lowest = jnp.maximum(base - (SB_GROUP - 1 - r), 0)
            m_end, _ = lax.while_loop(live, step, (lowest - 1, reach_now))
            stop_ref[hh, pl.ds(r, 1), :] = jnp.broadcast_to((m_end + 1).astype(F32), (1, 128))
        tot_ref[...] = c_scr[...]

    qblk = pl.BlockSpec((SB_ROWS, wide), lambda b, h, g: (b * n_groups + g, h))
    table = pl.BlockSpec((2 * BLOCK, 2 * BLOCK), lambda b, h, g: (0, 0))
    stops = pl.BlockSpec((None, None, SB_HEADS_FWD, 8, 128), lambda b, h, g: (b, g, h, 0, 0))
    kept = pl.BlockSpec((None, None, SB_HEADS_FWD, N_STATIC, BLOCK, BLOCK), lambda b, h, g: (b, g, h, 0, 0, 0))
    return _pcall(
        body, name=name, grid=(bsz, n_pairs, n_groups),
        in_specs=[qblk, pl.BlockSpec((lp, wide), lambda b, h, g: (b, n_pairs + h)),
                  pl.BlockSpec((lp, wide), lambda b, h, g: (b, 2 * n_pairs + h)), table],
        out_specs=(qblk, qblk, stops, kept, kept),
        out_shape=(jax.ShapeDtypeStruct((t, d), F32), jax.ShapeDtypeStruct((t, d), F32),
                   jax.ShapeDtypeStruct((bsz, n_groups, n_heads, 8, 128), F32))
        + (jax.ShapeDtypeStruct((bsz, n_groups, n_heads, N_STATIC, BLOCK, BLOCK), BF16),) * 2,
        scratch_shapes=[pltpu.VMEM((SB_ROWS, wide), F32)],
        compiler_params=_params("parallel", "parallel", "arbitrary"),
    )(qkv, qkv, qkv, _sb_tables()[0])


def _sb_bwd(qkv, do, tot, stop, w_kept, lb_kept, bsz, n_blocks, name):
    t, d3 = qkv.shape
    d = d3 // 3
    n_heads = d // HEAD
    lp = n_blocks * BLOCK
    n_groups = n_blocks // SB_GROUP
    scale = HEAD ** -0.5

    def body(q_ref, k_ref, v_ref, do_ref, tot_ref, stop_ref, incl_ref, excl_ref, w_kept, lb_kept,
             dq_ref, dk_ref, dv_ref, dk_acc, dv_acc, dq_acc, p_scr, e_scr, dob_scr):
        g = pl.program_id(2)

        @pl.when(g == 0)
        def _():
            dk_acc[...] = jnp.zeros_like(dk_acc)
            dv_acc[...] = jnp.zeros_like(dv_acc)

        incl, excl = incl_ref[...], excl_ref[...]
        causal = _iota2((BLOCK, BLOCK), 1) < _iota2((BLOCK, BLOCK), 0)
        lane = _iota2((1, BLOCK), 1)
        dob_scr[...] = do_ref[...].astype(BF16)
        dq_acc[...] = jnp.zeros_like(dq_acc)
        p_scr[...] = jnp.zeros_like(p_scr)
        e_scr[...] = jnp.zeros_like(e_scr)

        def tiles(jobs, slots=None):
            n_jobs = range(len(jobs))
            cols = [pl.ds(hh * HEAD, HEAD) for hh, _, _, _, _ in jobs]
            qrows = [pl.ds(r * BLOCK, BLOCK) for _, r, _, _, _ in jobs]
            krows = [pl.ds(pl.multiple_of(m * BLOCK, BLOCK), BLOCK) for _, _, m, _, _ in jobs]
            diag = [dg for _, _, _, dg, _ in jobs]
            d_a = [_dot_nt(dob_scr[qrows[i], cols[i]], v_ref[krows[i], cols[i]]) for i in n_jobs]
            if slots is None:
                lg = [_sb_logits(q_ref[qrows[i], cols[i]], k_ref[krows[i], cols[i]], scale,
                                 causal if diag[i] else None, jobs[i][4]) for i in n_jobs]
                log_beta = [x[1] for x in lg]
                sm = [_sums(lg[i][2], incl) for i in n_jobs]
                a_all = []
                for i in n_jobs:
                    p = p_scr[qrows[i], cols[i]]
                    a = jnp.exp(log_beta[i] + (tot_ref[qrows[i], cols[i]] - p - sm[i][0]))
                    a_all.append((jnp.where(causal, a, 0.0) if diag[i] else a).astype(BF16))
                    p_scr[qrows[i], cols[i]] = p + sm[i][1]
            else:
                a_all = [w_kept[hh, s] for hh, s in slots]
                log_beta = [lb_kept[hh, s].astype(F32) for hh, s in slots]
            gr = [d_a[i] * a_all[i].astype(F32) for i in n_jobs]
            dv_part = [_dot_tn(a_all[i], dob_scr[qrows[i], cols[i]]) for i in n_jobs]
            gs = [_sums(gr[i], excl, pieces=1) for i in n_jobs]
            dz_all = []
            for i in n_jobs:
                e = e_scr[qrows[i], cols[i]]
                dz = gr[i] - (gr[i] + e + gs[i][0]) * jnp.exp(log_beta[i])
                if diag[i]:
                    dz = jnp.where(causal, dz, 0.0)
                dz_all.append((dz * scale).astype(BF16))
                e_scr[qrows[i], cols[i]] = e + gs[i][1]
            dk_part = [_dot_tn(dz_all[i], q_ref[qrows[i], cols[i]]) for i in n_jobs]
            dq_part = [_dot(dz_all[i], k_ref[krows[i], cols[i]]) for i in n_jobs]
            for i in n_jobs:
                dv_acc[krows[i], cols[i]] += dv_part[i]
                dk_acc[krows[i], cols[i]] += dk_part[i]
                dq_acc[qrows[i], cols[i]] += dq_part[i]

        def pad_row_of(m):
            return jnp.where(jnp.logical_or(m > 0, lane >= N_PAD), 1.0, 0.0)

        base = SB_GROUP * g
        heads = range(SB_HEADS_BWD)
        mine = [(hh, r) for hh in heads for r in range(SB_GROUP)]
        stops = [jnp.max(stop_ref[hh, pl.ds(r, 1), :]).astype(jnp.int32) for hh, r in mine]
        for (hh, r), stop_now in zip(mine, stops):
            lowest = jnp.maximum(base - (SB_GROUP - 1 - r), 0)

            def step(m, hh=hh, r=r):
                tiles([(hh, r, m, False, pad_row_of(m))])
                return m + 1

            lax.while_loop(lambda m, lowest=lowest: m < lowest, step, jnp.clip(stop_now, 0, lowest))
        slot_of = {tile: i for i, tile in enumerate(SB_STATIC_OWN + SB_STATIC_NEAR)}
        upwards = lambda tile: (tile[1], tile[0])
        own = [(hh,) + tile for hh in heads for tile in sorted(SB_STATIC_OWN, key=upwards)]
        near = [(hh,) + tile for hh in heads for tile in sorted(SB_STATIC_NEAR, key=upwards)]

        def static_batch(order):
            tiles([(hh, r, base + rel, r == rel, None) for hh, r, rel in order],
                  slots=[(hh, slot_of[(r, rel)]) for hh, r, rel in order])

        pl.when(g == 0)(lambda: static_batch(own))
        pl.when(g > 0)(lambda: static_batch(near + own))
        dq_ref[...] = dq_acc[...].astype(BF16)

        @pl.when(g == n_groups - 1)
        def _():
            dk_ref[...] = dk_acc[...].astype(BF16)
            dv_ref[...] = dv_acc[...].astype(BF16)

    n_pairs = n_heads // SB_HEADS_BWD
    wide = SB_HEADS_BWD * HEAD
    qblk = pl.BlockSpec((SB_ROWS, wide), lambda b, h, g: (b * n_groups + g, h))
    kblk = pl.BlockSpec((lp, wide), lambda b, h, g: (b, n_pairs + h))
    vblk = pl.BlockSpec((lp, wide), lambda b, h, g: (b, 2 * n_pairs + h))
    hblk = pl.BlockSpec((lp, wide), lambda b, h, g: (b, h))
    sblk = pl.BlockSpec((None, None, SB_HEADS_BWD, 8, 128), lambda b, h, g: (b, g, h, 0, 0))
    table = pl.BlockSpec((2 * BLOCK, 2 * BLOCK), lambda b, h, g: (0, 0))
    kept = pl.BlockSpec((None, None, SB_HEADS_BWD, N_STATIC, BLOCK, BLOCK), lambda b, h, g: (b, g, h, 0, 0, 0))
    _, incl, excl = _sb_tables()
    return _pcall(
        body, name=name, grid=(bsz, n_pairs, n_groups),
        in_specs=[qblk, kblk, vblk, qblk, qblk, sblk, table, table, kept, kept],
        out_specs=(qblk, hblk, hblk),
        out_shape=(jax.ShapeDtypeStruct((t, d), BF16),) * 3,
        scratch_shapes=[pltpu.VMEM((lp, wide), F32), pltpu.VMEM((lp, wide), F32)]
        + [pltpu.VMEM((SB_ROWS, wide), F32)] * 3 + [pltpu.VMEM((SB_ROWS, wide), BF16)],
        compiler_params=_params("parallel", "parallel", "arbitrary"),
    )(qkv, qkv, qkv, do, tot, stop, incl, excl, w_kept, lb_kept)


def _adamw(w, g, m, v):
    m = ADAM_B1 * m + (1.0 - ADAM_B1) * g
    v = ADAM_B2 * v + (1.0 - ADAM_B2) * (g * g)
    m_hat = m / (1.0 - ADAM_B1 ** ADAM_STEP)
    v_hat = v / (1.0 - ADAM_B2 ** ADAM_STEP)
    delta = -ADAM_LR * (m_hat / (jnp.sqrt(v_hat) + ADAM_EPS) + ADAM_WD * w)
    return delta, m, v


def _update_sharded(w, parts, m, v, name):
    r, c = w.shape
    tr = UPDATE_ROWS if r % UPDATE_ROWS == 0 else r

    def body(w_ref, p_ref, m_ref, v_ref, g_ref, d_ref, nm_ref, nv_ref):
        g = p_ref[0].astype(F32)
        for q in range(1, N_DEV):
            g = g + p_ref[q].astype(F32)
        g_ref[...] = g
        d_ref[...], nm_ref[...], nv_ref[...] = _adamw(w_ref[...], g, m_ref[...], v_ref[...])

    row = pl.BlockSpec((tr, c), lambda i: (i, 0))
    return _pcall(
        body, name=name, grid=(r // tr,),
        in_specs=[row, pl.BlockSpec((N_DEV, tr, c), lambda i: (0, i, 0)), row, row],
        out_specs=(row,) * 4, out_shape=(jax.ShapeDtypeStruct((r, c), F32),) * 4,
        compiler_params=_params("parallel"),
    )(w, parts, m, v)


SMALL_ROWS = 8


def _pack_small(dpre0, dpre1, dpost0, dpost1, dlb, dwon, loss, name):
    d = dpre0.shape[1]
    bsz = dlb.shape[0]

    def body(a0, a1, p0, p1, lb_ref, on_ref, loss_ref, out_ref):
        out_ref[...] = jnp.zeros_like(out_ref)
        out_ref[pl.ds(0, 1), :] = a0[...]
        out_ref[pl.ds(1, 1), :] = a1[...]
        out_ref[pl.ds(2, 1), :] = p0[...]
        out_ref[pl.ds(3, 1), :] = p1[...]
        acc = lb_ref[0]
        for b in range(1, bsz):
            acc = acc + lb_ref[b]
        out_ref[pl.ds(4, 1), :] = acc
        out_ref[pl.ds(5, 1), pl.ds(0, HEAD)] = on_ref[...]
        out_ref[pl.ds(6, 1), pl.ds(0, HEAD)] = loss_ref[pl.ds(0, 1), :]

    return _pcall(body, name=name, out_shape=jax.ShapeDtypeStruct((SMALL_ROWS, d), F32))(
        dpre0, dpre1, dpost0, dpost1, dlb, dwon, loss)


def _update_small(parts, pre, post, lbw, on, moments, name):
    d = pre.shape[1]

    def body(p_ref, pre_ref, post_ref, lbw_ref, on_ref, mpre, mpost, mlb, mon, vpre, vpost, vlb, von,
             loss_ref, *outs):
        def total(r0, nr, width):
            acc = p_ref[0, pl.ds(r0, nr), pl.ds(0, width)]
            for q in range(1, N_DEV):
                acc = acc + p_ref[q, pl.ds(r0, nr), pl.ds(0, width)]
            return acc

        def put(k, w, g, m, v):
            dl, nm, nv = _adamw(w, g, m, v)
            outs[4 * k][...] = g
            outs[4 * k + 1][...] = dl
            outs[4 * k + 2][...] = nm
            outs[4 * k + 3][...] = nv

        put(0, pre_ref[...], total(0, 2, d), mpre[...], vpre[...])
        put(1, post_ref[...], total(2, 2, d), mpost[...], vpost[...])
        a0, a1 = lbw_ref[pl.ds(0, 1), :], lbw_ref[pl.ds(1, 1), :]
        mx = jnp.maximum(a0, a1)
        e0, e1 = jnp.exp(a0 - mx), jnp.exp(a1 - mx)
        p0 = e0 / (e0 + e1)
        g0 = total(4, 1, d) * p0 * (1.0 - p0)
        for r, w, g in ((0, a0, g0), (1, a1, -g0)):
            row = pl.ds(r, 1)
            dl, nm, nv = _adamw(w, g, mlb[row, :], vlb[row, :])
            outs[8][row, :] = g
            outs[9][row, :] = dl
            outs[10][row, :] = nm
            outs[11][row, :] = nv
        put(3, on_ref[...], total(5, 1, HEAD), mon[...], von[...])
        loss_ref[...] = jnp.broadcast_to(total(6, 1, HEAD), loss_ref.shape)

    shapes = []
    for w in (pre, post, lbw, on):
        shapes += [jax.ShapeDtypeStruct(w.shape, F32)] * 4
    return _pcall(body, name=name, out_shape=(jax.ShapeDtypeStruct((8, 128), F32), *shapes))(
        parts, pre, post, lbw, on, *moments)


def kernel(x, meta_tokens, pre_norm, post_norm, hgrn_w_in, hgrn_lb, hgrn_out_norm, hgrn_w_out, sb_w_in, sb_w_out, loss_target, m_meta_tokens, m_pre_norm, m_post_norm, m_hgrn_w_in, m_hgrn_lb, m_hgrn_out_norm, m_hgrn_w_out, m_sb_w_in, m_sb_w_out, v_meta_tokens, v_pre_norm, v_post_norm, v_hgrn_w_in, v_hgrn_lb, v_hgrn_out_norm, v_hgrn_w_out, v_sb_w_in, v_sb_w_out):
    bsz, seq, d = x.shape
    n_blocks = seq // BLOCK + 1
    lp = n_blocks * BLOCK
    s = hgrn_w_in.shape[2]
    dsh = d // N_DEV

    w_in_h, meta_all = _gather_once_per_chip([hgrn_w_in[0].astype(BF16), meta_tokens], "gather_weights")
    meta_full = jnp.transpose(meta_all, (1, 0, 2)).reshape(N_META, d)

    lbrow = jnp.cumsum(jax.nn.softmax(hgrn_lb, axis=0), axis=0)[0:1]

    main0, gate0, yn0, h0 = _norm_inproj(None, pre_norm[0:1], w_in_h, F32, "inproj_hgrn", from_x=(x, meta_full))
    o0, states, decay0, k0, a_diag0, w_in_s, w_out_s, w_out_h = _hgrn_fwd(
        main0, lbrow, bsz, n_blocks,
        [sb_w_in[0].astype(BF16), sb_w_out[0].astype(BF16), hgrn_w_out[0].astype(BF16)], ["gather"] * 3, "hgrn_fwd")
    w_out_s = w_out_s.reshape(d, d)
    w_out_h = w_out_h.reshape(d, d)
    h1, y0, main1, gate1, yn1 = _mix_out(o0, gate0, h0, w_out_h, hgrn_out_norm, post_norm[0:1], True,
                                         "mix_out_hgrn_inproj_sb", next_proj=(pre_norm[1:2], w_in_s, BF16))
    o1, tot, stop, w_kept, lb_kept = _sb_fwd(main1, bsz, n_blocks, "sb_fwd")
    dh2, y1, loss_part = _mix_out(o1, gate1, h1, w_out_s, hgrn_out_norm, post_norm[1:2], False, "mix_out_sb",
                                  target=loss_target)

    do1, dgate1, dw_out_s, dpost1, _ = _mix_out_bwd(
        dh2, y1, o1, gate1, w_out_s, hgrn_out_norm, post_norm[1:2], False, "mix_out_sb_bwd")
    dq1, dk1, dv1 = _sb_bwd(main1, do1, tot, stop, w_kept, lb_kept, bsz, n_blocks, "sb_bwd")
    dproj1 = (dq1, dk1, dv1, dgate1)
    dh1, dpre1, do0, dgate0, dw_out_h, dpost0, dwon = _inproj_bwd_x(
        dproj1, w_in_s, h1, pre_norm[1:2], dh2, [], [], "inproj_sb_bwd_x_mix_out_hgrn_bwd",
        then_mix_bwd=(y0, o0, gate0, w_out_h, hgrn_out_norm, post_norm[0:1], True))
    dw_in_s = _inproj_bwd_w(yn1, dproj1, s, "inproj_sb_bwd_w")

    dq0, dfz0, dv0, dlb, p_in_s, p_out_s, p_out_h = _hgrn_bwd(
        main0, decay0, k0, a_diag0, lbrow, states, do0, bsz, n_blocks,
        [dw_in_s, dw_out_s.reshape(N_DEV, dsh, d), dw_out_h.reshape(N_DEV, dsh, d)], ["scatter"] * 3, "hgrn_bwd")
    dproj0 = (dq0, dfz0, dv0, dgate0)
    dw_in_h = _inproj_bwd_w(yn0, dproj0, s, "inproj_hgrn_bwd_w")
    dh0, dpre0, p_in_h = _inproj_bwd_x(
        dproj0, w_in_h, h0, pre_norm[0:1], dh1, [dw_in_h], ["scatter"], "inproj_hgrn_bwd_x")

    dh0 = dh0.reshape(bsz, lp, d)
    grad_x = dh0[:, BLOCK:]
    dmeta = jnp.sum(dh0[:, N_PAD:BLOCK], axis=0)
    dmeta = jnp.transpose(dmeta.reshape(N_META, N_DEV, dsh), (1, 0, 2))
    small = _pack_small(dpre0, dpre1, dpost0, dpost1, dlb, dwon, loss_part, "pack_small")

    p_meta, p_small = _exchange([dmeta, small], ["scatter", "gather"], "exchange_grads")

    u_meta = _update_sharded(meta_tokens, p_meta, m_meta_tokens, v_meta_tokens, "update_meta")
    u_in_h = _update_sharded(hgrn_w_in[0], p_in_h, m_hgrn_w_in[0], v_hgrn_w_in[0], "update_hgrn_w_in")
    u_out_h = _update_sharded(hgrn_w_out[0], p_out_h, m_hgrn_w_out[0], v_hgrn_w_out[0], "update_hgrn_w_out")
    u_in_s = _update_sharded(sb_w_in[0], p_in_s, m_sb_w_in[0], v_sb_w_in[0], "update_sb_w_in")
    u_out_s = _update_sharded(sb_w_out[0], p_out_s, m_sb_w_out[0], v_sb_w_out[0], "update_sb_w_out")
    sm = _update_small(p_small, pre_norm, post_norm, hgrn_lb, hgrn_out_norm,
                       (m_pre_norm, m_post_norm, m_hgrn_lb, m_hgrn_out_norm,
                        v_pre_norm, v_post_norm, v_hgrn_lb, v_hgrn_out_norm), "update_small")
    loss = sm[0][0, 0]
    u_pre, u_post, u_lb, u_on = sm[1:5], sm[5:9], sm[9:13], sm[13:17]

    per_w = [u_meta, u_pre, u_post, tuple(a[None] for a in u_in_h), u_lb, u_on,
             tuple(a[None] for a in u_out_h), tuple(a[None] for a in u_in_s), tuple(a[None] for a in u_out_s)]
    outs = [loss, grad_x]
    for k in range(4):
        outs += [u[k] for u in per_w]
    return tuple(outs)
```

```python
import jax
import jax.numpy as jnp
from jax import lax
from jax.experimental import pallas as pl
from jax.experimental.pallas import tpu as pltpu

F32 = jnp.float32
BF16 = jnp.bfloat16

N_DEV = 8
BLOCK = 128
N_META = 16
N_PAD = BLOCK - N_META
HEAD = 128
SUB = 16
N_SUB = BLOCK // SUB
HALF = 8
N_HALF = SUB // HALF
EPS = 1e-6
ROW_TILE = 3 * BLOCK
UPDATE_ROWS = 256
K_TILES = (2816, 1408, 768, 384, 128)
NEG_BIG = -1e30

ADAM_LR = 0.001
ADAM_B1 = 0.9
ADAM_B2 = 0.999
ADAM_EPS = 1e-08
ADAM_WD = 0.01
ADAM_STEP = 10

VMEM_LIMIT = 56 * 1024 * 1024


def _pcall(body, **kw):
    return pl.pallas_call(body, **kw)


def _params(*sem):
    return pltpu.CompilerParams(dimension_semantics=sem, vmem_limit_bytes=VMEM_LIMIT)


def _dot(a, b):
    return jnp.dot(a, b, preferred_element_type=F32)


def _dot_nt(a, b):
    return lax.dot_general(a, b, (((1,), (1,)), ((), ())), preferred_element_type=F32)


def _dot_tn(a, b):
    return lax.dot_general(a, b, (((0,), (0,)), ((), ())), preferred_element_type=F32)


def _split(x, pieces):
    out = []
    for _ in range(pieces):
        p = x.astype(BF16)
        out.append(p)
        x = x - p.astype(F32)
    return out


def _tri_left(tri, x, pieces=3):
    return sum(_dot(tri, p) for p in _split(x, pieces))


def _iota2(shape, dim):
    return lax.broadcasted_iota(jnp.int32, shape, dim)


def _tri(cond):
    return jnp.where(cond, 1.0, 0.0).astype(BF16)


def _sig_pair(x):
    e = jnp.exp(-jnp.abs(x))
    r = 1.0 / (1.0 + e)
    er = e * r
    pos = x >= 0
    return jnp.where(pos, r, er), jnp.where(pos, er, r)


def _expand(rows):
    return jnp.concatenate([jnp.broadcast_to(r, (SUB, HEAD)) for r in rows], axis=0)


def _exchange_shapes(arrays, modes):
    return tuple(jax.ShapeDtypeStruct((N_DEV,) + tuple(a.shape[1:] if m == "scatter" else a.shape), a.dtype)
                 for a, m in zip(arrays, modes))


def _exchange_sems(n):
    if n == 0:
        return []
    return [pltpu.SemaphoreType.DMA((n, N_DEV - 1)), pltpu.SemaphoreType.DMA((n, N_DEV - 1)),
            pltpu.SemaphoreType.DMA((n,))]


def _exchange_copies(ins, outs, modes, send_sems, recv_sems, local_sems):
    mx, my, mc = lax.axis_index("x"), lax.axis_index("y"), lax.axis_index("c")
    me = 4 * mx + 2 * my + mc

    def src(i, slot):
        return ins[i].at[slot] if modes[i] == "scatter" else ins[i]

    def peer_of(mask):
        px = 1 - mx if mask & 4 else mx
        py = 1 - my if mask & 2 else my
        pc = 1 - mc if mask & 1 else mc
        return px, py, pc

    def copy(i, mask, dst_slot):
        px, py, pc = peer_of(mask)
        return pltpu.make_async_remote_copy(
            src_ref=src(i, 4 * px + 2 * py + pc), dst_ref=outs[i].at[dst_slot],
            send_sem=send_sems.at[i, mask - 1], recv_sem=recv_sems.at[i, mask - 1],
            device_id=(px, py, pc), device_id_type=pl.DeviceIdType.MESH)

    n = len(ins)
    sends = [copy(i, mask, me) for mask in range(1, N_DEV) for i in range(n)]
    own = [pltpu.make_async_copy(src(i, me), outs[i].at[me], local_sems.at[i]) for i in range(n)]
    arrivals = []
    for mask in range(1, N_DEV):
        px, py, pc = peer_of(mask)
        arrivals += [copy(i, mask, 4 * px + 2 * py + pc) for i in range(n)]

    def start():
        for cp in sends + own:
            cp.start()

    def wait():
        for cp in arrivals:
            cp.wait_recv()
        for cp in sends:
            cp.wait_send()
        for cp in own:
            cp.wait()

    return start, wait


def _gather_once_per_chip(arrays, name):
    n = len(arrays)

    def body(*refs):
        ins, outs = refs[:n], refs[n:2 * n]
        send_sems, recv_sems, local_sems = refs[2 * n:]
        mx, my, mc = lax.axis_index("x"), lax.axis_index("y"), lax.axis_index("c")
        me, sibling = (mx, my, mc), (mx, my, 1 - mc)
        chips = [(1 - mx, my), (mx, 1 - my), (1 - mx, 1 - my)]

        def slot(px, py, pc):
            return 4 * px + 2 * py + pc

        def copy(i, k, block, to, src=None):
            return pltpu.make_async_remote_copy(
                src_ref=outs[i].at[slot(*block)] if src is None else src, dst_ref=outs[i].at[slot(*block)],
                send_sem=send_sems.at[i, k], recv_sem=recv_sems.at[i, k],
                device_id=to, device_id_type=pl.DeviceIdType.MESH)

        own = [pltpu.make_async_copy(ins[i], outs[i].at[slot(*me)], local_sems.at[i]) for i in range(n)]
        first = [copy(i, 0, me, sibling, src=ins[i]) for i in range(n)]
        first += [copy(i, 1 + j, me, (*chip, mc), src=ins[i]) for j, chip in enumerate(chips) for i in range(n)]
        for cp in own + first:
            cp.start()
        passed = []
        for j, chip in enumerate(chips):
            for i in range(n):
                copy(i, 1 + j, (*chip, mc), me).wait_recv()
                cp = copy(i, 4 + j, (*chip, mc), sibling)
                cp.start()
                passed.append(cp)
        for i in range(n):
            copy(i, 0, sibling, me).wait_recv()
            for j, chip in enumerate(chips):
                copy(i, 4 + j, (*chip, 1 - mc), me).wait_recv()
        for cp in first + passed:
            cp.wait_send()
        for cp in own:
            cp.wait()

    any_spec = pl.BlockSpec(memory_space=pl.ANY)
    return _pcall(
        body, name=name, out_shape=_exchange_shapes(arrays, ["gather"] * n),
        in_specs=[any_spec] * n, out_specs=tuple([any_spec] * n),
        scratch_shapes=_exchange_sems(n),
    )(*arrays)


def _exchange(arrays, modes, name):
    n = len(arrays)

    def body(*refs):
        start, wait = _exchange_copies(refs[:n], refs[n:2 * n], modes, *refs[2 * n:])
        start()
        wait()

    any_spec = pl.BlockSpec(memory_space=pl.ANY)
    return _pcall(
        body, name=name, out_shape=_exchange_shapes(arrays, modes),
        in_specs=[any_spec] * n, out_specs=tuple([any_spec] * n),
        scratch_shapes=_exchange_sems(n),
    )(*arrays)


def _tile_blocks_of_x(seq, d):
    assert ROW_TILE == 3 * BLOCK and (seq + BLOCK) % ROW_TILE == 0
    per_seq = (seq + BLOCK) // ROW_TILE

    def spec(k):
        return pl.BlockSpec((None, BLOCK, d),
                            lambda i: (i // per_seq, jnp.maximum(3 * (i % per_seq) - 1 + k, 0), 0))

    return [spec(0), spec(1), spec(2)]


def _project(x, wn_ref, w_ref, main_ref, gate_ref, ynt_ref):
    p_n, _, s = w_ref.shape
    n_main = main_ref.shape[1] // s
    y = x * lax.rsqrt(jnp.mean(x * x, axis=-1, keepdims=True) + EPS) * wn_ref[...]
    yb = y.astype(BF16)
    ynt_ref[...] = y.T.astype(BF16)
    for p in range(p_n):
        r = _dot(yb, w_ref[p])
        if p < n_main:
            main_ref[:, p * s:(p + 1) * s] = r.astype(main_ref.dtype)
        else:
            gate_ref[:, (p - n_main) * s:(p - n_main + 1) * s] = r


def _projection_specs(t, d, w_all, main_dtype, tm):
    p_n, _, s = w_all.shape
    row = pl.BlockSpec((tm, d), lambda i: (i, 0))
    return ([pl.BlockSpec((1, d), lambda i: (0, 0)), pl.BlockSpec((p_n, d, s), lambda i: (0, 0, 0))],
            (pl.BlockSpec((tm, 3 * d), lambda i: (i, 0)), row, pl.BlockSpec((d, tm), lambda i: (0, i))),
            (jax.ShapeDtypeStruct((t, 3 * d), main_dtype), jax.ShapeDtypeStruct((t, d), F32),
             jax.ShapeDtypeStruct((d, t), BF16)))


def _norm_inproj(h, wnorm, w_all, main_dtype, name, from_x=None):
    p_n, d, s = w_all.shape
    n_main = 3 * d // s
    tm = ROW_TILE
    if from_x is None:
        t = h.shape[0]
        lead, lead_specs = [h], [pl.BlockSpec((tm, d), lambda i: (i, 0))]
    else:
        x_in, meta = from_x
        t = x_in.shape[0] * (x_in.shape[1] + BLOCK)
        tiles_per_seq = (x_in.shape[1] + BLOCK) // tm
        lead = [x_in, x_in, x_in, meta]
        lead_specs = _tile_blocks_of_x(x_in.shape[1], d) + [pl.BlockSpec((N_META, d), lambda i: (0, 0))]
    n_lead = len(lead)

    def body(*refs):
        wn_ref, w_ref = refs[n_lead:n_lead + 2]
        main_ref, gate_ref, ynt_ref = refs[n_lead + 2:n_lead + 5]
        if from_x is None:
            x = refs[0][...]
        else:
            first_tile = pl.program_id(0) % tiles_per_seq == 0
            meta_block = jnp.concatenate([jnp.zeros((N_PAD, d), F32), refs[3][...]], axis=0)
            x = jnp.concatenate([jnp.where(first_tile, meta_block, refs[0][...]), refs[1][...], refs[2][...]], axis=0)
            refs[n_lead + 5][...] = x
        _project(x, wn_ref, w_ref, main_ref, gate_ref, ynt_ref)

    row = pl.BlockSpec((tm, d), lambda i: (i, 0))
    extra_specs, extra_shapes = ((), ()) if from_x is None else ((row,), (jax.ShapeDtypeStruct((t, d), F32),))
    proj_in, proj_out, proj_shapes = _projection_specs(t, d, w_all, main_dtype, tm)
    return _pcall(
        body, name=name, grid=(t // tm,),
        in_specs=lead_specs + proj_in, out_specs=proj_out + extra_specs, out_shape=proj_shapes + extra_shapes,
        compiler_params=_params("parallel"),
    )(*lead, wnorm, w_all)


def _mix_out(o, gate, h_in, w_out, w_on, w_post, head_norm, name, target=None, next_proj=None):
    t, d = o.shape
    n_heads = d // HEAD
    tm = ROW_TILE
    with_loss = target is not None
    if with_loss:
        tiles_per_seq = (target.shape[1] + BLOCK) // tm

    def body(o_ref, g_ref, h_ref, w_ref, won_ref, wp_ref, *rest):
        if with_loss:
            t0_ref, t1_ref, t2_ref, hout_ref, loss_ref = rest[:5]
            u_scr = rest[10]
        elif next_proj is not None:
            wn2_ref, w2_ref, hout_ref, y_ref, main_ref, gate_ref, ynt_ref, u_scr = rest
        else:
            hout_ref, y_ref, u_scr = rest
        for hh in range(n_heads):
            cs = slice(hh * HEAD, (hh + 1) * HEAD)
            oh = o_ref[:, cs]
            gt = g_ref[:, cs]
            if head_norm:
                oh = oh * lax.rsqrt(jnp.mean(oh * oh, axis=-1, keepdims=True) + EPS) * won_ref[...]
            u_scr[:, cs] = (oh * (gt * jax.nn.sigmoid(gt))).astype(BF16)
        y = _dot(u_scr[...], w_ref[...])
        r = y * lax.rsqrt(jnp.mean(y * y, axis=-1, keepdims=True) + EPS) * wp_ref[...]
        h_out = h_ref[...] + r
        if not with_loss:
            y_ref[...] = y
            hout_ref[...] = h_out
            if next_proj is not None:
                _project(h_out, wn2_ref, w2_ref, main_ref, gate_ref, ynt_ref)
            return
        i = pl.program_id(0)

        @pl.when(i == 0)
        def _():
            loss_ref[...] = jnp.zeros_like(loss_ref)

        tgt = jnp.concatenate([t0_ref[...], t1_ref[...], t2_ref[...]], axis=0)
        real = jnp.logical_or(i % tiles_per_seq > 0, _iota2((tm, d), 0) >= BLOCK)
        err = jnp.where(real, h_out - tgt, 0.0)
        dh = err * (1.0 / d)
        hout_ref[...] = dh
        part = jnp.sum(jnp.sum(err * err, axis=-1, keepdims=True), axis=0, keepdims=True)
        loss_ref[...] += part * (0.5 / d)
        _mix_out_bwd_tile(dh, y, o_ref, g_ref, w_ref, won_ref, wp_ref, *rest[5:], head_norm, t // tm - 1)

    row = pl.BlockSpec((tm, d), lambda i: (i, 0))
    in_specs = [row, row, row, pl.BlockSpec((d, d), lambda i: (0, 0)),
                pl.BlockSpec((1, HEAD), lambda i: (0, 0)), pl.BlockSpec((1, d), lambda i: (0, 0))]
    out_specs, out_shape = (row, row), (jax.ShapeDtypeStruct((t, d), F32), jax.ShapeDtypeStruct((t, d), F32))
    scratch = [pltpu.VMEM((tm, d), BF16)]
    args = (o, gate, h_in, w_out, w_on, w_post)
    if with_loss:
        in_specs += _tile_blocks_of_x(target.shape[1], d)
        bwd_specs, bwd_shapes, scratch = _mix_out_bwd_outputs(t, d, tm)
        out_specs = (row, pl.BlockSpec((8, 128), lambda i: (0, 0))) + bwd_specs
        out_shape = (jax.ShapeDtypeStruct((t, d), F32), jax.ShapeDtypeStruct((8, 128), F32)) + bwd_shapes
        args += (target, target, target)
    elif next_proj is not None:
        proj_in, proj_out, proj_shapes = _projection_specs(t, d, next_proj[1], next_proj[2], tm)
        in_specs += proj_in
        out_specs += proj_out
        out_shape += proj_shapes
        args += (next_proj[0], next_proj[1])
    return _pcall(
        body, name=name, grid=(t // tm,), in_specs=in_specs, out_specs=out_specs, out_shape=out_shape,
        scratch_shapes=scratch,
        compiler_params=_params("arbitrary" if with_loss else "parallel"),
    )(*args)


def _mix_out_bwd_tile(dr, yv, o_ref, g_ref, w_ref, won_ref, wp_ref,
                      do_ref, dg_ref, dw_ref, dwp_ref, dwon_ref, u_scr, acc, head_norm, last):
    i = pl.program_id(0)
    n_heads = o_ref.shape[1] // HEAD

    @pl.when(i == 0)
    def _():
        acc[...] = jnp.zeros_like(acc)
        dwp_ref[...] = jnp.zeros_like(dwp_ref)
        dwon_ref[...] = jnp.zeros_like(dwon_ref)

    rs = lax.rsqrt(jnp.mean(yv * yv, axis=-1, keepdims=True) + EPS)
    yh = yv * rs
    dwp_ref[...] += jnp.sum(dr * yh, axis=0, keepdims=True)
    wd = dr * wp_ref[...]
    dy = rs * (wd - yh * jnp.mean(wd * yh, axis=-1, keepdims=True))
    dyb = dy.astype(BF16)
    du = _dot_nt(dyb, w_ref[...])
    for hh in range(n_heads):
        cs = slice(hh * HEAD, (hh + 1) * HEAD)
        oh = o_ref[:, cs]
        gt = g_ref[:, cs]
        sg = jax.nn.sigmoid(gt)
        sl = gt * sg
        duh = du[:, cs]
        if head_norm:
            rsh = lax.rsqrt(jnp.mean(oh * oh, axis=-1, keepdims=True) + EPS)
            ohat = oh * rsh
            on = ohat * won_ref[...]
        else:
            on = oh
        u_scr[:, cs] = (on * sl).astype(BF16)
        don = duh * sl
        dg_ref[:, cs] = (duh * on * (sg * (1.0 + gt * (1.0 - sg)))).astype(BF16)
        if head_norm:
            dwon_ref[...] += jnp.sum(don * ohat, axis=0, keepdims=True)
            wdn = don * won_ref[...]
            do_ref[:, cs] = rsh * (wdn - ohat * jnp.mean(wdn * ohat, axis=-1, keepdims=True))
        else:
            do_ref[:, cs] = don
    acc[...] += _dot_tn(u_scr[...], dyb)

    @pl.when(i == last)
    def _():
        dw_ref[...] = acc[...].astype(BF16)


def _mix_out_bwd_outputs(t, d, tm):
    row = pl.BlockSpec((tm, d), lambda i: (i, 0))
    specs = (row, row, pl.BlockSpec((d, d), lambda i: (0, 0)), pl.BlockSpec((1, d), lambda i: (0, 0)),
             pl.BlockSpec((1, HEAD), lambda i: (0, 0)))
    shapes = (jax.ShapeDtypeStruct((t, d), F32), jax.ShapeDtypeStruct((t, d), BF16),
              jax.ShapeDtypeStruct((d, d), BF16), jax.ShapeDtypeStruct((1, d), F32),
              jax.ShapeDtypeStruct((1, HEAD), F32))
    return specs, shapes, [pltpu.VMEM((tm, d), BF16), pltpu.VMEM((d, d), F32)]


def _mix_out_bwd(dh, y, o, gate, w_out, w_on, w_post, head_norm, name):
    t, d = o.shape
    tm = ROW_TILE
    last = t // tm - 1

    def body(dh_ref, y_ref, *refs):
        _mix_out_bwd_tile(dh_ref[...], y_ref[...], *refs, head_norm, last)

    row = pl.BlockSpec((tm, d), lambda i: (i, 0))
    out_specs, out_shape, scratch = _mix_out_bwd_outputs(t, d, tm)
    return _pcall(
        body, name=name, grid=(t // tm,),
        in_specs=[row, row, row, row, pl.BlockSpec((d, d), lambda i: (0, 0)),
                  pl.BlockSpec((1, HEAD), lambda i: (0, 0)), pl.BlockSpec((1, d), lambda i: (0, 0))],
        out_specs=out_specs, out_shape=out_shape, scratch_shapes=scratch,
        compiler_params=_params("arbitrary"),
    )(dh, y, o, gate, w_out, w_on, w_post)


def _inproj_bwd_x(dparts, w_all, h, wnorm, dres, ride, ride_modes, name):
    t, d = h.shape
    p_n, _, s = w_all.shape
    per = d // s
    tm = ROW_TILE
    nr = len(ride)
    grid = (t // tm,)

    def body(*refs):
        d0_ref, d1_ref, d2_ref, d3_ref, w_ref, h_ref, wn_ref, dres_ref = refs[:8]
        dh_ref, dwn_ref = refs[8 + nr:10 + nr]
        finish_ride = _ride_along(refs[8:8 + nr], refs[10 + nr:10 + 2 * nr], ride_modes, refs[10 + 2 * nr:], grid)
        i = pl.program_id(0)

        @pl.when(i == 0)
        def _():
            dwn_ref[...] = jnp.zeros_like(dwn_ref)

        pieces = (d0_ref, d1_ref, d2_ref, d3_ref)
        dyn = jnp.zeros((tm, d), F32)
        for p in range(p_n):
            blk = pieces[p // per][:, (p % per) * s:(p % per + 1) * s]
            dyn = dyn + _dot_nt(blk, w_ref[p])
        x = h_ref[...]
        rs = lax.rsqrt(jnp.mean(x * x, axis=-1, keepdims=True) + EPS)
        xh = x * rs
        dwn_ref[...] += jnp.sum(dyn * xh, axis=0, keepdims=True)
        wd = dyn * wn_ref[...]
        dh_ref[...] = dres_ref[...] + rs * (wd - xh * jnp.mean(wd * xh, axis=-1, keepdims=True))
        finish_ride()

    row = pl.BlockSpec((tm, d), lambda i: (i, 0))
    any_spec = pl.BlockSpec(memory_space=pl.ANY)
    return _pcall(
        body, name=name, grid=grid,
        in_specs=[row, row, row, row, pl.BlockSpec((p_n, d, s), lambda i: (0, 0, 0)),
                  row, pl.BlockSpec((1, d), lambda i: (0, 0)), row] + [any_spec] * nr,
        out_specs=(row, pl.BlockSpec((1, d), lambda i: (0, 0))) + (any_spec,) * nr,
        out_shape=(jax.ShapeDtypeStruct((t, d), F32), jax.ShapeDtypeStruct((1, d), F32))
        + _exchange_shapes(ride, ride_modes),
        scratch_shapes=_exchange_sems(nr),
        compiler_params=_params("arbitrary"),
    )(*dparts, w_all, h, wnorm, dres, *ride)


def _inproj_bwd_w(ynt, dparts, s, name):
    d, t = ynt.shape
    per = d // s
    n_sh = len(dparts) * per
    tk = next(c for c in K_TILES if t % c == 0)
    last = t // tk - 1

    def body(ynt_ref, d0_ref, d1_ref, d2_ref, d3_ref, dw_ref, acc):
        p, i = pl.program_id(0), pl.program_id(1)

        @pl.when(i == 0)
        def _():
            acc[...] = jnp.zeros_like(acc)

        for a, piece in enumerate((d0_ref, d1_ref, d2_ref, d3_ref)):
            @pl.when(p // per == a)
            def _():
                acc[...] += _dot(ynt_ref[...], piece[...])

        @pl.when(i == last)
        def _():
            dw_ref[...] = acc[...].astype(BF16)

    def piece_spec(a):
        return pl.BlockSpec((tk, s), lambda p, i: (jnp.where(p // per == a, i, 0),
                                                   jnp.where(p // per == a, p % per, 0)))

    return _pcall(
        body, name=name, grid=(n_sh, t // tk),
        in_specs=[pl.BlockSpec((d, tk), lambda p, i: (0, i))] + [piece_spec(a) for a in range(4)],
        out_specs=pl.BlockSpec((None, d, s), lambda p, i: (p, 0, 0)),
        out_shape=jax.ShapeDtypeStruct((n_sh, d, s), BF16),
        scratch_shapes=[pltpu.VMEM((d, s), F32)],
        compiler_params=_params("parallel", "arbitrary"),
    )(ynt, *dparts)


def _hgrn_gates(fz_ref, lb, b_ref, k_ref):
    sig, nsig = _sig_pair(fz_ref[...])
    f = lb + (1.0 - lb) * sig
    rr, cc = _iota2((BLOCK, BLOCK), 0), _iota2((BLOCK, BLOCK), 1)
    b_ref[...] = _tri_left(_tri(cc <= rr), jnp.log(f))
    k_ref[...] = (1.0 - lb) * nsig


def _hgrn_common(q_ref, b_ref, k_ref):
    b, k = b_ref[...], k_ref[...]
    bend = [b_ref[pl.ds(SUB * j + SUB - 1, 1), :] for j in range(N_SUB)]
    bref = [jnp.zeros((1, HEAD), F32)] + bend[:-1]
    refrow, bendrow = _expand(bref), _expand(bend)
    e_q = jnp.exp(b - refrow)
    e_k = jnp.exp(bendrow - b)
    qt = q_ref[...] * e_q
    kh = k * e_k
    bl = bend[-1]
    return dict(k=k, b=b, bend=bend, bref=bref, refrow=refrow, e_q=e_q, e_k=e_k, qt=qt, kh=kh, bl=bl)


HEADS_PER_STEP = 8


def _lockstep(chunks):
    live = list(chunks)
    while live:
        still = []
        for gen in live:
            try:
                next(gen)
                still.append(gen)
            except StopIteration:
                pass
        live = still


def _ride_along(ride_ins, ride_outs, modes, sems, grid):
    if not ride_ins:
        return lambda: None
    ids = [pl.program_id(a) for a in range(len(grid))]
    start, wait = _exchange_copies(ride_ins, ride_outs, modes, *sems)
    first, last = ids[0] == 0, ids[0] == grid[0] - 1
    for a in range(1, len(grid)):
        first = jnp.logical_and(first, ids[a] == 0)
        last = jnp.logical_and(last, ids[a] == grid[a] - 1)
    pl.when(first)(start)
    return lambda: pl.when(last)(wait)


def _hgrn_fwd(main, lbrow, bsz, n_blocks, ride, ride_modes, name):
    t, d3 = main.shape
    d = d3 // 3
    n_heads = d // HEAD
    n_pairs = n_heads // HEADS_PER_STEP
    wide = HEADS_PER_STEP * HEAD
    nr = len(ride)
    grid = (bsz, n_pairs, n_blocks)

    def chunk(q_ref, fz_ref, v_ref, lb, o_ref, st_ref, b_scr, k_scr, ad_ref, s_scr, o_acc, ad_scr):
        _hgrn_gates(fz_ref, lb, b_scr, k_scr)
        yield
        c = _hgrn_common(q_ref, b_scr, k_scr)
        yield
        s_t = s_scr[...]
        st_ref[...] = s_t
        vb = v_ref[...].astype(BF16)
        q_state = c["qt"] * _expand([jnp.exp(r) for r in c["bref"]])
        o_state = _dot_nt(q_state.astype(BF16), s_t.astype(BF16))
        js = range(N_SUB - 1)
        khb = c["kh"].astype(BF16)
        lhs = [(c["qt"][SUB * (j + 1):] * jnp.exp(c["refrow"][SUB * (j + 1):] - c["bend"][j])).astype(BF16)
               for j in js]
        yield
        a_js = [_dot_nt(lhs[j], khb[SUB * j:SUB * (j + 1)]) for j in js]
        k_state = c["kh"] * _expand([jnp.exp(c["bl"] - r) for r in c["bend"]])
        s_scr[...] = s_t * jnp.exp(c["bl"]) + _dot_tn(vb, k_state.astype(BF16))
        yield
        o_js = [_dot(a_js[j].astype(BF16), vb[SUB * j:SUB * (j + 1)]) for j in js]
        yield
        o_acc[...] = o_state
        for j in js:
            o_acc[SUB * (j + 1):, :] += o_js[j]
        not_before = [_iota2((HALF, HEAD), 0) >= row for row in range(HALF)]
        lane = _iota2((HALF, HEAD), 1)
        for i in range(N_SUB):
            r0 = SUB * i
            q_h = [q_ref[pl.ds(r0 + HALF * u, HALF), :] for u in range(N_HALF)]
            b_h = [b_scr[pl.ds(r0 + HALF * u, HALF), :] for u in range(N_HALF)]
            a_h = [jnp.zeros((HALF, HEAD), F32) for _ in range(N_HALF)]
            for s in range(SUB):
                brow = b_scr[pl.ds(r0 + s, 1), :]
                krow = k_scr[pl.ds(r0 + s, 1), :]
                is_lane = lane == r0 + s
                for u in range(s // HALF, N_HALF):
                    diff = b_h[u] - brow
                    if u == s // HALF:
                        diff = jnp.where(not_before[s - HALF * u], diff, NEG_BIG)
                    col = jnp.sum(q_h[u] * krow * jnp.exp(diff), axis=-1, keepdims=True)
                    a_h[u] = jnp.where(is_lane, col, a_h[u])
            for u in range(N_HALF):
                ad_scr[pl.ds(r0 + HALF * u, HALF), :] = a_h[u]
            yield
        a_diag = ad_scr[...].astype(BF16)
        ad_ref[...] = a_diag
        o_ref[...] = o_acc[...] + _dot(a_diag, vb)

    def body(*refs):
        q_ref, fz_ref, v_ref, lb_ref = refs[:4]
        o_ref, st_ref, b_out, k_out, ad_out = refs[4 + nr:9 + nr]
        s_scr, o_acc, ad_scr = refs[9 + 2 * nr:12 + 2 * nr]
        finish_ride = _ride_along(refs[4:4 + nr], refs[9 + nr:9 + 2 * nr], ride_modes, refs[12 + 2 * nr:], grid)

        @pl.when(pl.program_id(2) == 0)
        def _():
            s_scr[...] = jnp.zeros_like(s_scr)

        def head(hh):
            cols = pl.ds(hh * HEAD, HEAD)
            return chunk(q_ref.at[:, cols], fz_ref.at[:, cols], v_ref.at[:, cols], lb_ref[:, cols],
                         o_ref.at[:, cols], st_ref.at[hh], b_out.at[:, cols], k_out.at[:, cols],
                         ad_out.at[:, cols], s_scr.at[hh], o_acc.at[hh], ad_scr.at[hh])

        _lockstep([head(hh) for hh in range(HEADS_PER_STEP)])
        finish_ride()

    def blk(col0):
        return pl.BlockSpec((BLOCK, wide), lambda b, h, n: (b * n_blocks + n, col0 + h))

    any_spec = pl.BlockSpec(memory_space=pl.ANY)
    per_head = pltpu.VMEM((HEADS_PER_STEP, BLOCK, HEAD), F32)
    return _pcall(
        body, name=name, grid=grid,
        in_specs=[blk(0), blk(n_pairs), blk(2 * n_pairs), pl.BlockSpec((1, wide), lambda b, h, n: (0, h))]
        + [any_spec] * nr,
        out_specs=(blk(0), pl.BlockSpec((None, HEADS_PER_STEP, HEAD, HEAD), lambda b, h, n: (b * n_blocks + n, h, 0, 0)),
                   blk(0), blk(0), blk(0)) + (any_spec,) * nr,
        out_shape=(jax.ShapeDtypeStruct((t, d), F32),
                   jax.ShapeDtypeStruct((bsz * n_blocks, n_heads, HEAD, HEAD), F32),
                   jax.ShapeDtypeStruct((t, d), F32), jax.ShapeDtypeStruct((t, d), F32),
                   jax.ShapeDtypeStruct((t, d), BF16))
        + _exchange_shapes(ride, ride_modes),
        scratch_shapes=[per_head] * 3 + _exchange_sems(nr),
        compiler_params=_params("arbitrary", "arbitrary", "arbitrary"),
    )(main, main, main, lbrow, *ride)


def _hgrn_bwd(main, b_all, k_all, a_diag, lbrow, states, do, bsz, n_blocks, ride, ride_modes, name):
    t, d3 = main.shape
    d = d3 // 3
    n_pairs = d // HEAD // HEADS_PER_STEP
    wide = HEADS_PER_STEP * HEAD
    nr = len(ride)
    grid = (bsz, n_pairs, n_blocks)

    def chunk(n, q_ref, b_scr, k_scr, ad_ref, v_ref, lb, st_ref, do_ref, dq_ref, dfz_ref, dv_ref, dlb_ref,
              ds_scr, dqt_acc, dkh_acc, dv_acc, dqd_acc, dkd_acc):
        c = _hgrn_common(q_ref, b_scr, k_scr)
        yield
        q, k = q_ref[...], c["k"]
        vb = v_ref[...].astype(BF16)
        dob = do_ref[...].astype(BF16)
        s0_t = st_ref[...]
        ds1_t = ds_scr[...]
        e_ref = _expand([jnp.exp(r) for r in c["bref"]])
        e_end = _expand([jnp.exp(c["bl"] - r) for r in c["bend"]])
        e_bl = jnp.exp(c["bl"])
        q_state = c["qt"] * e_ref
        k_state = c["kh"] * e_end
        dq_state = _dot(dob, s0_t.astype(BF16))
        dk_state = _dot(vb, ds1_t.astype(BF16))
        dv_state = _dot_nt(k_state.astype(BF16), ds1_t.astype(BF16))
        ds_scr[...] = ds1_t * e_bl + _dot_tn(dob, q_state.astype(BF16))
        js = range(N_SUB - 1)
        lo = [slice(SUB * j, SUB * (j + 1)) for j in js]
        khb = c["kh"].astype(BF16)
        dj = [jnp.exp(c["refrow"][SUB * (j + 1):] - c["bend"][j]) for j in js]
        lhs = [(c["qt"][SUB * (j + 1):] * dj[j]).astype(BF16) for j in js]
        yield
        a_js = [_dot_nt(lhs[j], khb[lo[j]]) for j in js]
        da_js = [_dot_nt(dob[SUB * (j + 1):], vb[lo[j]]).astype(BF16) for j in js]
        dqt_acc[...] = dq_state * e_ref
        dkh_acc[...] = dk_state * e_end
        dv_acc[...] = dv_state
        dbl = (jnp.sum(s0_t * ds1_t, axis=0, keepdims=True) * e_bl
               + jnp.sum(k_state * dk_state, axis=0, keepdims=True))
        yield
        dv_js = [_dot_tn(a_js[j].astype(BF16), dob[SUB * (j + 1):]) for j in js]
        dq_js = [_dot(da_js[j], khb[lo[j]]) * dj[j] for j in js]
        dk_js = [_dot_tn(da_js[j], lhs[j]) for j in js]
        yield
        for j in js:
            dv_acc[lo[j], :] += dv_js[j]
            dqt_acc[SUB * (j + 1):, :] += dq_js[j]
            dkh_acc[lo[j], :] += dk_js[j]
        not_before = [_iota2((HALF, HEAD), 0) >= row for row in range(HALF)]
        is_row = [_iota2((HALF, HEAD), 0) == row for row in range(HALF)]
        for i in range(N_SUB):
            r0 = SUB * i
            q_h = [q_ref[pl.ds(r0 + HALF * u, HALF), :] for u in range(N_HALF)]
            b_h = [b_scr[pl.ds(r0 + HALF * u, HALF), :] for u in range(N_HALF)]
            do_h = [do_ref[pl.ds(r0 + HALF * u, HALF), :] for u in range(N_HALF)]
            zero = jnp.zeros((HALF, HEAD), F32)
            dq_h, dk_h = [zero] * N_HALF, [zero] * N_HALF
            for s in range(SUB):
                brow = b_scr[pl.ds(r0 + s, 1), :]
                krow = k_scr[pl.ds(r0 + s, 1), :]
                vrow = v_ref[pl.ds(r0 + s, 1), :]
                dk_row = jnp.zeros((1, HEAD), F32)
                for u in range(s // HALF, N_HALF):
                    diff = b_h[u] - brow
                    if u == s // HALF:
                        diff = jnp.where(not_before[s - HALF * u], diff, NEG_BIG)
                    w = jnp.exp(diff)
                    qw = q_h[u] * w
                    da_col = jnp.sum(do_h[u] * vrow, axis=-1, keepdims=True)
                    dq_h[u] = dq_h[u] + da_col * (w * krow)
                    dk_row = dk_row + jnp.sum(da_col * qw, axis=0, keepdims=True)
                us = s // HALF
                dk_h[us] = jnp.where(is_row[s - HALF * us], dk_row, dk_h[us])
            for u in range(N_HALF):
                rows = pl.ds(r0 + HALF * u, HALF)
                dqd_acc[rows, :] = dq_h[u]
                dkd_acc[rows, :] = dk_h[u]
            yield
        dv_in = _dot_tn(ad_ref[...], dob)
        dq = dqt_acc[...] * c["e_q"] + dqd_acc[...]
        dk = dkh_acc[...] * c["e_k"] + dkd_acc[...]
        rr, cc = _iota2((BLOCK, BLOCK), 0), _iota2((BLOCK, BLOCK), 1)
        db = q * dq - k * dk + jnp.where(_iota2((BLOCK, HEAD), 0) == BLOCK - 1, dbl, 0.0)
        yield
        dg = _tri_left(_tri(cc >= rr), db)
        yield
        real = jnp.logical_or(n > 0, _iota2((BLOCK, HEAD), 0) >= N_PAD)
        df = jnp.where(real, dg / (1.0 - k) - dk, 0.0)
        nsig = k * (1.0 / (1.0 - lb))
        dq_ref[...] = dq.astype(BF16)
        dv_ref[...] = (dv_acc[...] + dv_in).astype(BF16)
        dfz_ref[...] = (df * k * (1.0 - nsig)).astype(BF16)
        dlb_ref[...] += jnp.sum(df * nsig, axis=0, keepdims=True)

    def body(*refs):
        q_ref, b_ref, k_ref, ad_ref, v_ref, lb_ref, st_ref, do_ref = refs[:8]
        dq_ref, dfz_ref, dv_ref, dlb_ref = refs[8 + nr:12 + nr]
        scratch = refs[12 + 2 * nr:18 + 2 * nr]
        finish_ride = _ride_along(refs[8:8 + nr], refs[12 + nr:12 + 2 * nr], ride_modes, refs[18 + 2 * nr:], grid)
        step = pl.program_id(2)

        @pl.when(step == 0)
        def _():
            scratch[0][...] = jnp.zeros_like(scratch[0])
            dlb_ref[...] = jnp.zeros_like(dlb_ref)

        def head(hh):
            cols = pl.ds(hh * HEAD, HEAD)
            return chunk(n_blocks - 1 - step, q_ref.at[:, cols], b_ref.at[:, cols], k_ref.at[:, cols],
                         ad_ref.at[:, cols], v_ref.at[:, cols], lb_ref[:, cols], st_ref.at[hh], do_ref.at[:, cols],
                         dq_ref.at[:, cols],
                         dfz_ref.at[:, cols], dv_ref.at[:, cols], dlb_ref.at[:, cols],
                         *[scr.at[hh] for scr in scratch])

        _lockstep([head(hh) for hh in range(HEADS_PER_STEP)])
        finish_ride()

    def blk(col0):
        return pl.BlockSpec((BLOCK, wide), lambda b, h, s: (b * n_blocks + n_blocks - 1 - s, col0 + h))

    any_spec = pl.BlockSpec(memory_space=pl.ANY)
    per_head = pltpu.VMEM((HEADS_PER_STEP, BLOCK, HEAD), F32)
    return _pcall(
        body, name=name, grid=grid,
        in_specs=[blk(0), blk(0), blk(0), blk(0), blk(2 * n_pairs), pl.BlockSpec((1, wide), lambda b, h, s: (0, h)),
                  pl.BlockSpec((None, HEADS_PER_STEP, HEAD, HEAD),
                               lambda b, h, s: (b * n_blocks + n_blocks - 1 - s, h, 0, 0)),
                  blk(0)] + [any_spec] * nr,
        out_specs=(blk(0), blk(0), blk(0), pl.BlockSpec((None, 1, wide), lambda b, h, s: (b, 0, h)))
        + (any_spec,) * nr,
        out_shape=(jax.ShapeDtypeStruct((t, d), BF16),) * 3 + (jax.ShapeDtypeStruct((bsz, 1, d), F32),)
        + _exchange_shapes(ride, ride_modes),
        scratch_shapes=[per_head] * 6 + _exchange_sems(nr),
        compiler_params=_params("arbitrary", "arbitrary", "arbitrary"),
    )(main, b_all, k_all, a_diag, main, lbrow, states, do, *ride)


SB_GROUP = 3
SB_ROWS = SB_GROUP * BLOCK
SB_DEAD = -104.0
SB_NEAR = ((0, 0), (1, 0), (0, 1))
SB_HEADS_FWD = 4
SB_HEADS_BWD = 2
SB_STATIC_OWN = tuple((r, kb) for kb in range(SB_GROUP - 1, -1, -1) for r in range(kb, SB_GROUP))
SB_STATIC_NEAR = tuple((r, -1 - back) for r, back in SB_NEAR)
N_STATIC = len(SB_STATIC_OWN) + len(SB_STATIC_NEAR)


def _sb_tables():
    j = jnp.bitwise_and(_iota2((2 * BLOCK, 2 * BLOCK), 0), BLOCK - 1)
    s = _iota2((2 * BLOCK, 2 * BLOCK), 1)
    ones = s >= BLOCK
    return (_tri(jnp.logical_or(ones, j > s)), _tri(jnp.logical_or(ones, j <= s)),
            _tri(jnp.logical_or(ones, j < s)))


def _sums(x, table, pieces=2):
    hi = x.astype(BF16)
    if pieces == 1:
        r = _dot(hi, table[:BLOCK])
    else:
        lo = (x - hi.astype(F32)).astype(BF16)
        r = _dot(jnp.concatenate([hi, lo], axis=1), table)
    return r[:, :BLOCK], r[:, BLOCK:]


def _sb_logits(q, ks, scale, causal, pad_row):
    z = _dot_nt(q, ks) * scale
    log_keep = -(jnp.maximum(z, 0.0) + jnp.log(1.0 + jnp.exp(-jnp.abs(z))))
    log_beta = z + log_keep
    if causal is not None:
        log_keep = jnp.where(causal, log_keep, 0.0)
    if pad_row is not None:
        log_keep = log_keep * pad_row
    return z, log_beta, log_keep


def _sb_fwd(qkv, bsz, n_blocks, name):
    t, d3 = qkv.shape
    d = d3 // 3
    n_heads = d // HEAD
    lp = n_blocks * BLOCK
    n_groups = n_blocks // SB_GROUP
    assert n_groups * SB_GROUP == n_blocks
    scale = HEAD ** -0.5

    n_pairs = n_heads // SB_HEADS_FWD
    wide = SB_HEADS_FWD * HEAD

    def body(q_ref, k_ref, v_ref, upper_ref, o_ref, tot_ref, stop_ref, w_keep, lb_keep, c_scr):
        g = pl.program_id(2)
        upper = upper_ref[...]
        causal = _iota2((BLOCK, BLOCK), 1) < _iota2((BLOCK, BLOCK), 0)
        lane = _iota2((1, BLOCK), 1)
        o_ref[...] = jnp.zeros_like(o_ref)
        c_scr[...] = jnp.zeros_like(c_scr)

        def tiles(jobs, keep=None):
            cols = [pl.ds(hh * HEAD, HEAD) for hh, _, _, _, _ in jobs]
            qrows = [pl.ds(r * BLOCK, BLOCK) for _, r, _, _, _ in jobs]
            krows = [pl.ds(pl.multiple_of(m * BLOCK, BLOCK), BLOCK) for _, _, m, _, _ in jobs]
            n_jobs = range(len(jobs))
            lg = [_sb_logits(q_ref[qrows[i], cols[i]], k_ref[krows[i], cols[i]], scale,
                             causal if jobs[i][3] else None, jobs[i][4]) for i in n_jobs]
            sm = [_sums(x[2], upper) for x in lg]
            a_all = []
            for i in n_jobs:
                c = c_scr[qrows[i], cols[i]]
                a = jnp.exp(lg[i][1] + c + sm[i][0])
                a_all.append((jnp.where(causal, a, 0.0) if jobs[i][3] else a).astype(BF16))
                c_scr[qrows[i], cols[i]] = c + sm[i][1]
            out = [_dot(a_all[i], v_ref[krows[i], cols[i]]) for i in n_jobs]
            for i in n_jobs:
                o_ref[qrows[i], cols[i]] += out[i]
                if keep is not None:
                    w_keep[jobs[i][0], keep[i]] = a_all[i]
                    lb_keep[jobs[i][0], keep[i]] = lg[i][1].astype(BF16)

        def pad_row_of(m):
            return jnp.where(jnp.logical_or(m > 0, lane >= N_PAD), 1.0, 0.0)

        base = SB_GROUP * g
        heads = range(SB_HEADS_FWD)
        own = [(hh, r, base + rel, r == rel, pad_row_of(base) if rel == 0 else None)
               for hh in heads for r, rel in SB_STATIC_OWN]
        near = [(hh, r, base + rel, False, None) for hh in heads for r, rel in SB_STATIC_NEAR]
        own_slots = [i for _ in heads for i in range(len(SB_STATIC_OWN))]
        near_slots = [len(SB_STATIC_OWN) + i for _ in heads for i in range(len(SB_STATIC_NEAR))]
        pl.when(g == 0)(lambda: tiles(own, keep=own_slots))
        pl.when(g > 0)(lambda: tiles(own + near, keep=own_slots + near_slots))

        stop_ref[...] = jnp.zeros_like(stop_ref)
        mine = [(hh, r) for hh in heads for r in range(SB_GROUP)]
        reach = [jnp.max(c_scr[pl.ds(r * BLOCK, BLOCK), pl.ds(hh * HEAD, HEAD)]) for hh, r in mine]
        for (hh, r), reach_now in zip(mine, reach):
            def live(carry):
                m, c_max = carry
                return jnp.logical_and(m >= 0, c_max >= SB_DEAD)

            def step(carry, hh=hh, r=r):
                m, _ = carry
                tiles([(hh, r, m, False, pad_row_of(m))])
                return m - 1, jnp.max(c_scr[pl.ds(r * BLOCK, BLOCK), pl.ds(hh * HEAD, HEAD)])

            lowest = jnp.maximum(base - (SB_GROUP - 1 - r), 0)
            m_end, _ = lax.while_loop(live, step, (lowest - 1, reach_now))
            stop_ref[hh, pl.ds(r, 1), :] = jnp.broadcast_to((m_end + 1).astype(F32), (1, 128))
        tot_ref[...] = c_scr[...]

    qblk = pl.BlockSpec((SB_ROWS, wide), lambda b, h, g: (b * n_groups + g, h))
    table = pl.BlockSpec((2 * BLOCK, 2 * BLOCK), lambda b, h, g: (0, 0))
    stops = pl.BlockSpec((None, None, SB_HEADS_FWD, 8, 128), lambda b, h, g: (b, g, h, 0, 0))
    kept = pl.BlockSpec((None, None, SB_HEADS_FWD, N_STATIC, BLOCK, BLOCK), lambda b, h, g: (b, g, h, 0, 0, 0))
    return _pcall(
        body, name=name, grid=(bsz, n_pairs, n_groups),
        in_specs=[qblk, pl.BlockSpec((lp, wide), lambda b, h, g: (b, n_pairs + h)),
                  pl.BlockSpec((lp, wide), lambda b, h, g: (b, 2 * n_pairs + h)), table],
        out_specs=(qblk, qblk, stops, kept, kept),
        out_shape=(jax.ShapeDtypeStruct((t, d), F32), jax.ShapeDtypeStruct((t, d), F32),
                   jax.ShapeDtypeStruct((bsz, n_groups, n_heads, 8, 128), F32))
        + (jax.ShapeDtypeStruct((bsz, n_groups, n_heads, N_STATIC, BLOCK, BLOCK), BF16),) * 2,
        scratch_shapes=[pltpu.VMEM((SB_ROWS, wide), F32)],
        compiler_params=_params("parallel", "parallel", "arbitrary"),
    )(qkv, qkv, qkv, _sb_tables()[0])


def _sb_bwd(qkv, do, tot, stop, w_kept, lb_kept, bsz, n_blocks, name):
    t, d3 = qkv.shape
    d = d3 // 3
    n_heads = d // HEAD
    lp = n_blocks * BLOCK
    n_groups = n_blocks // SB_GROUP
    scale = HEAD ** -0.5

    def body(q_ref, k_ref, v_ref, do_ref, tot_ref, stop_ref, incl_ref, excl_ref, w_kept, lb_kept,
             dq_ref, dk_ref, dv_ref, dk_acc, dv_acc, dq_acc, p_scr, e_scr, dob_scr):
        g = pl.program_id(2)

        @pl.when(g == 0)
        def _():
            dk_acc[...] = jnp.zeros_like(dk_acc)
            dv_acc[...] = jnp.zeros_like(dv_acc)

        incl, excl = incl_ref[...], excl_ref[...]
        causal = _iota2((BLOCK, BLOCK), 1) < _iota2((BLOCK, BLOCK), 0)
        lane = _iota2((1, BLOCK), 1)
        dob_scr[...] = do_ref[...].astype(BF16)
        dq_acc[...] = jnp.zeros_like(dq_acc)
        p_scr[...] = jnp.zeros_like(p_scr)
        e_scr[...] = jnp.zeros_like(e_scr)

        def tiles(jobs, slots=None):
            n_jobs = range(len(jobs))
            cols = [pl.ds(hh * HEAD, HEAD) for hh, _, _, _, _ in jobs]
            qrows = [pl.ds(r * BLOCK, BLOCK) for _, r, _, _, _ in jobs]
            krows = [pl.ds(pl.multiple_of(m * BLOCK, BLOCK), BLOCK) for _, _, m, _, _ in jobs]
            diag = [dg for _, _, _, dg, _ in jobs]
            d_a = [_dot_nt(dob_scr[qrows[i], cols[i]], v_ref[krows[i], cols[i]]) for i in n_jobs]
            if slots is None:
                lg = [_sb_logits(q_ref[qrows[i], cols[i]], k_ref[krows[i], cols[i]], scale,
                                 causal if diag[i] else None, jobs[i][4]) for i in n_jobs]
                log_beta = [x[1] for x in lg]
                sm = [_sums(lg[i][2], incl) for i in n_jobs]
                a_all = []
                for i in n_jobs:
                    p = p_scr[qrows[i], cols[i]]
                    a = jnp.exp(log_beta[i] + (tot_ref[qrows[i], cols[i]] - p - sm[i][0]))
                    a_all.append((jnp.where(causal, a, 0.0) if diag[i] else a).astype(BF16))
                    p_scr[qrows[i], cols[i]] = p + sm[i][1]
            else:
                a_all = [w_kept[hh, s] for hh, s in slots]
                log_beta = [lb_kept[hh, s].astype(F32) for hh, s in slots]
            gr = [d_a[i] * a_all[i].astype(F32) for i in n_jobs]
            dv_part = [_dot_tn(a_all[i], dob_scr[qrows[i], cols[i]]) for i in n_jobs]
            gs = [_sums(gr[i], excl, pieces=1) for i in n_jobs]
            dz_all = []
            for i in n_jobs:
                e = e_scr[qrows[i], cols[i]]
                dz = gr[i] - (gr[i] + e + gs[i][0]) * jnp.exp(log_beta[i])
                if diag[i]:
                    dz = jnp.where(causal, dz, 0.0)
                dz_all.append((dz * scale).astype(BF16))
                e_scr[qrows[i], cols[i]] = e + gs[i][1]
            dk_part = [_dot_tn(dz_all[i], q_ref[qrows[i], cols[i]]) for i in n_jobs]
            dq_part = [_dot(dz_all[i], k_ref[krows[i], cols[i]]) for i in n_jobs]
            for i in n_jobs:
                dv_acc[krows[i], cols[i]] += dv_part[i]
                dk_acc[krows[i], cols[i]] += dk_part[i]
                dq_acc[qrows[i], cols[i]] += dq_part[i]

        def pad_row_of(m):
            return jnp.where(jnp.logical_or(m > 0, lane >= N_PAD), 1.0, 0.0)

        base = SB_GROUP * g
        heads = range(SB_HEADS_BWD)
        mine = [(hh, r) for hh in heads for r in range(SB_GROUP)]
        stops = [jnp.max(stop_ref[hh, pl.ds(r, 1), :]).astype(jnp.int32) for hh, r in mine]
        for (hh, r), stop_now in zip(mine, stops):
            lowest = jnp.maximum(base - (SB_GROUP - 1 - r), 0)

            def step(m, hh=hh, r=r):
                tiles([(hh, r, m, False, pad_row_of(m))])
                return m + 1

            lax.while_loop(lambda m, lowest=lowest: m < lowest, step, jnp.clip(stop_now, 0, lowest))
        slot_of = {tile: i for i, tile in enumerate(SB_STATIC_OWN + SB_STATIC_NEAR)}
        upwards = lambda tile: (tile[1], tile[0])
        own = [(hh,) + tile for hh in heads for tile in sorted(SB_STATIC_OWN, key=upwards)]
        near = [(hh,) + tile for hh in heads for tile in sorted(SB_STATIC_NEAR, key=upwards)]

        def static_batch(order):
            tiles([(hh, r, base + rel, r == rel, None) for hh, r, rel in order],
                  slots=[(hh, slot_of[(r, rel)]) for hh, r, rel in order])

        pl.when(g == 0)(lambda: static_batch(own))
        pl.when(g > 0)(lambda: static_batch(near + own))
        dq_ref[...] = dq_acc[...].astype(BF16)

        @pl.when(g == n_groups - 1)
        def _():
            dk_ref[...] = dk_acc[...].astype(BF16)
            dv_ref[...] = dv_acc[...].astype(BF16)

    n_pairs = n_heads // SB_HEADS_BWD
    wide = SB_HEADS_BWD * HEAD
    qblk = pl.BlockSpec((SB_ROWS, wide), lambda b, h, g: (b * n_groups + g, h))
    kblk = pl.BlockSpec((lp, wide), lambda b, h, g: (b, n_pairs + h))
    vblk = pl.BlockSpec((lp, wide), lambda b, h, g: (b, 2 * n_pairs + h))
    hblk = pl.BlockSpec((lp, wide), lambda b, h, g: (b, h))
    sblk = pl.BlockSpec((None, None, SB_HEADS_BWD, 8, 128), lambda b, h, g: (b, g, h, 0, 0))
    table = pl.BlockSpec((2 * BLOCK, 2 * BLOCK), lambda b, h, g: (0, 0))
    kept = pl.BlockSpec((None, None, SB_HEADS_BWD, N_STATIC, BLOCK, BLOCK), lambda b, h, g: (b, g, h, 0, 0, 0))
    _, incl, excl = _sb_tables()
    return _pcall(
        body, name=name, grid=(bsz, n_pairs, n_groups),
        in_specs=[qblk, kblk, vblk, qblk, qblk, sblk, table, table, kept, kept],
        out_specs=(qblk, hblk, hblk),
        out_shape=(jax.ShapeDtypeStruct((t, d), BF16),) * 3,
        scratch_shapes=[pltpu.VMEM((lp, wide), F32), pltpu.VMEM((lp, wide), F32)]
        + [pltpu.VMEM((SB_ROWS, wide), F32)] * 3 + [pltpu.VMEM((SB_ROWS, wide), BF16)],
        compiler_params=_params("parallel", "parallel", "arbitrary"),
    )(qkv, qkv, qkv, do, tot, stop, incl, excl, w_kept, lb_kept)


def _adamw(w, g, m, v):
    m = ADAM_B1 * m + (1.0 - ADAM_B1) * g
    v = ADAM_B2 * v + (1.0 - ADAM_B2) * (g * g)
    m_hat = m / (1.0 - ADAM_B1 ** ADAM_STEP)
    v_hat = v / (1.0 - ADAM_B2 ** ADAM_STEP)
    delta = -ADAM_LR * (m_hat / (jnp.sqrt(v_hat) + ADAM_EPS) + ADAM_WD * w)
    return delta, m, v


def _update_sharded(w, parts, m, v, name):
    r, c = w.shape
    tr = UPDATE_ROWS if r % UPDATE_ROWS == 0 else r

    def body(w_ref, p_ref, m_ref, v_ref, g_ref, d_ref, nm_ref, nv_ref):
        g = p_ref[0].astype(F32)
        for q in range(1, N_DEV):
            g = g + p_ref[q].astype(F32)
        g_ref[...] = g
        d_ref[...], nm_ref[...], nv_ref[...] = _adamw(w_ref[...], g, m_ref[...], v_ref[...])

    row = pl.BlockSpec((tr, c), lambda i: (i, 0))
    return _pcall(
        body, name=name, grid=(r // tr,),
        in_specs=[row, pl.BlockSpec((N_DEV, tr, c), lambda i: (0, i, 0)), row, row],
        out_specs=(row,) * 4, out_shape=(jax.ShapeDtypeStruct((r, c), F32),) * 4,
        compiler_params=_params("parallel"),
    )(w, parts, m, v)


SMALL_ROWS = 8


def _pack_small(dpre0, dpre1, dpost0, dpost1, dlb, dwon, loss, name):
    d = dpre0.shape[1]
    bsz = dlb.shape[0]

    def body(a0, a1, p0, p1, lb_ref, on_ref, loss_ref, out_ref):
        out_ref[...] = jnp.zeros_like(out_ref)
        out_ref[pl.ds(0, 1), :] = a0[...]
        out_ref[pl.ds(1, 1), :] = a1[...]
        out_ref[pl.ds(2, 1), :] = p0[...]
        out_ref[pl.ds(3, 1), :] = p1[...]
        acc = lb_ref[0]
        for b in range(1, bsz):
            acc = acc + lb_ref[b]
        out_ref[pl.ds(4, 1), :] = acc
        out_ref[pl.ds(5, 1), pl.ds(0, HEAD)] = on_ref[...]
        out_ref[pl.ds(6, 1), pl.ds(0, HEAD)] = loss_ref[pl.ds(0, 1), :]

    return _pcall(body, name=name, out_shape=jax.ShapeDtypeStruct((SMALL_ROWS, d), F32))(
        dpre0, dpre1, dpost0, dpost1, dlb, dwon, loss)


def _update_small(parts, pre, post, lbw, on, moments, name):
    d = pre.shape[1]

    def body(p_ref, pre_ref, post_ref, lbw_ref, on_ref, mpre, mpost, mlb, mon, vpre, vpost, vlb, von,
             loss_ref, *outs):
        def total(r0, nr, width):
            acc = p_ref[0, pl.ds(r0, nr), pl.ds(0, width)]
            for q in range(1, N_DEV):
                acc = acc + p_ref[q, pl.ds(r0, nr), pl.ds(0, width)]
            return acc

        def put(k, w, g, m, v):
            dl, nm, nv = _adamw(w, g, m, v)
            outs[4 * k][...] = g
            outs[4 * k + 1][...] = dl
            outs[4 * k + 2][...] = nm
            outs[4 * k + 3][...] = nv

        put(0, pre_ref[...], total(0, 2, d), mpre[...], vpre[...])
        put(1, post_ref[...], total(2, 2, d), mpost[...], vpost[...])
        a0, a1 = lbw_ref[pl.ds(0, 1), :], lbw_ref[pl.ds(1, 1), :]
        mx = jnp.maximum(a0, a1)
        e0, e1 = jnp.exp(a0 - mx), jnp.exp(a1 - mx)
        p0 = e0 / (e0 + e1)
        g0 = total(4, 1, d) * p0 * (1.0 - p0)
        for r, w, g in ((0, a0, g0), (1, a1, -g0)):
            row = pl.ds(r, 1)
            dl, nm, nv = _adamw(w, g, mlb[row, :], vlb[row, :])
            outs[8][row, :] = g
            outs[9][row, :] = dl
            outs[10][row, :] = nm
            outs[11][row, :] = nv
        put(3, on_ref[...], total(5, 1, HEAD), mon[...], von[...])
        loss_ref[...] = jnp.broadcast_to(total(6, 1, HEAD), loss_ref.shape)

    shapes = []
    for w in (pre, post, lbw, on):
        shapes += [jax.ShapeDtypeStruct(w.shape, F32)] * 4
    return _pcall(body, name=name, out_shape=(jax.ShapeDtypeStruct((8, 128), F32), *shapes))(
        parts, pre, post, lbw, on, *moments)


def kernel(x, meta_tokens, pre_norm, post_norm, hgrn_w_in, hgrn_lb, hgrn_out_norm, hgrn_w_out, sb_w_in, sb_w_out, loss_target, m_meta_tokens, m_pre_norm, m_post_norm, m_hgrn_w_in, m_hgrn_lb, m_hgrn_out_norm, m_hgrn_w_out, m_sb_w_in, m_sb_w_out, v_meta_tokens, v_pre_norm, v_post_norm, v_hgrn_w_in, v_hgrn_lb, v_hgrn_out_norm, v_hgrn_w_out, v_sb_w_in, v_sb_w_out):
    bsz, seq, d = x.shape
    n_blocks = seq // BLOCK + 1
    lp = n_blocks * BLOCK
    s = hgrn_w_in.shape[2]
    dsh = d // N_DEV

    w_in_h, meta_all = _gather_once_per_chip([hgrn_w_in[0].astype(BF16), meta_tokens], "gather_weights")
    meta_full = jnp.transpose(meta_all, (1, 0, 2)).reshape(N_META, d)

    lbrow = jnp.cumsum(jax.nn.softmax(hgrn_lb, axis=0), axis=0)[0:1]

    main0, gate0, yn0, h0 = _norm_inproj(None, pre_norm[0:1], w_in_h, F32, "inproj_hgrn", from_x=(x, meta_full))
    o0, states, decay0, k0, a_diag0, w_in_s, w_out_s, w_out_h = _hgrn_fwd(
        main0, lbrow, bsz, n_blocks,
        [sb_w_in[0].astype(BF16), sb_w_out[0].astype(BF16), hgrn_w_out[0].astype(BF16)], ["gather"] * 3, "hgrn_fwd")
    w_out_s = w_out_s.reshape(d, d)
    w_out_h = w_out_h.reshape(d, d)
    h1, y0, main1, gate1, yn1 = _mix_out(o0, gate0, h0, w_out_h, hgrn_out_norm, post_norm[0:1], True,
                                         "mix_out_hgrn_inproj_sb", next_proj=(pre_norm[1:2], w_in_s, BF16))
    o1, tot, stop, w_kept, lb_kept = _sb_fwd(main1, bsz, n_blocks, "sb_fwd")
    dh2, loss_part, do1, dgate1, dw_out_s, dpost1, _ = _mix_out(
        o1, gate1, h1, w_out_s, hgrn_out_norm, post_norm[1:2], False, "mix_out_sb_loss_bwd", target=loss_target)

    dq1, dk1, dv1 = _sb_bwd(main1, do1, tot, stop, w_kept, lb_kept, bsz, n_blocks, "sb_bwd")
    dproj1 = (dq1, dk1, dv1, dgate1)
    dh1, dpre1 = _inproj_bwd_x(dproj1, w_in_s, h1, pre_norm[1:2], dh2, [], [], "inproj_sb_bwd_x")
    dw_in_s = _inproj_bwd_w(yn1, dproj1, s, "inproj_sb_bwd_w")

    do0, dgate0, dw_out_h, dpost0, dwon = _mix_out_bwd(
        dh1, y0, o0, gate0, w_out_h, hgrn_out_norm, post_norm[0:1], True, "mix_out_hgrn_bwd")
    dq0, dfz0, dv0, dlb, p_in_s, p_out_s, p_out_h = _hgrn_bwd(
        main0, decay0, k0, a_diag0, lbrow, states, do0, bsz, n_blocks,
        [dw_in_s, dw_out_s.reshape(N_DEV, dsh, d), dw_out_h.reshape(N_DEV, dsh, d)], ["scatter"] * 3, "hgrn_bwd")
    dproj0 = (dq0, dfz0, dv0, dgate0)
    dw_in_h = _inproj_bwd_w(yn0, dproj0, s, "inproj_hgrn_bwd_w")
    dh0, dpre0, p_in_h = _inproj_bwd_x(
        dproj0, w_in_h, h0, pre_norm[0:1], dh1, [dw_in_h], ["scatter"], "inproj_hgrn_bwd_x")

    dh0 = dh0.reshape(bsz, lp, d)
    grad_x = dh0[:, BLOCK:]
    dmeta = jnp.sum(dh0[:, N_PAD:BLOCK], axis=0)
    dmeta = jnp.transpose(dmeta.reshape(N_META, N_DEV, dsh), (1, 0, 2))
    small = _pack_small(dpre0, dpre1, dpost0, dpost1, dlb, dwon, loss_part, "pack_small")

    p_meta, p_small = _exchange([dmeta, small], ["scatter", "gather"], "exchange_grads")

    u_meta = _update_sharded(meta_tokens, p_meta, m_meta_tokens, v_meta_tokens, "update_meta")
    u_in_h = _update_sharded(hgrn_w_in[0], p_in_h, m_hgrn_w_in[0], v_hgrn_w_in[0], "update_hgrn_w_in")
    u_out_h = _update_sharded(hgrn_w_out[0], p_out_h, m_hgrn_w_out[0], v_hgrn_w_out[0], "update_hgrn_w_out")
    u_in_s = _update_sharded(sb_w_in[0], p_in_s, m_sb_w_in[0], v_sb_w_in[0], "update_sb_w_in")
    u_out_s = _update_sharded(sb_w_out[0], p_out_s, m_sb_w_out[0], v_sb_w_out[0], "update_sb_w_out")
    sm = _update_small(p_small, pre_norm, post_norm, hgrn_lb, hgrn_out_norm,
                       (m_pre_norm, m_post_norm, m_hgrn_lb, m_hgrn_out_norm,
                        v_pre_norm, v_post_norm, v_hgrn_lb, v_hgrn_out_norm), "update_small")
    loss = sm[0][0, 0]
    u_pre, u_post, u_lb, u_on = sm[1:5], sm[5:9], sm[9:13], sm[13:17]

    per_w = [u_meta, u_pre, u_post, tuple(a[None] for a in u_in_h), u_lb, u_on,
             tuple(a[None] for a in u_out_h), tuple(a[None] for a in u_in_s), tuple(a[None] for a in u_out_s)]
    outs = [loss, grad_x]
    for k in range(4):
        outs += [u[k] for u in per_w]
    return tuple(outs)
```

```python
import jax
import jax.numpy as jnp
from jax import lax
from jax.experimental import pallas as pl
from jax.experimental.pallas import tpu as pltpu

F32 = jnp.float32
BF16 = jnp.bfloat16

N_DEV = 8
BLOCK = 128
N_META = 16
N_PAD = BLOCK - N_META
HEAD = 128
SUB = 16
N_SUB = BLOCK // SUB
HALF = 8
N_HALF = SUB // HALF
EPS = 1e-6
ROW_TILE = 3 * BLOCK
UPDATE_ROWS = 256
K_TILES = (2816, 1408, 768, 384, 128)
NEG_BIG = -1e30

ADAM_LR = 0.001
ADAM_B1 = 0.9
ADAM_B2 = 0.999
ADAM_EPS = 1e-08
ADAM_WD = 0.01
ADAM_STEP = 10

VMEM_LIMIT = 56 * 1024 * 1024


def _pcall(body, **kw):
    return pl.pallas_call(body, **kw)


def _params(*sem):
    return pltpu.CompilerParams(dimension_semantics=sem, vmem_limit_bytes=VMEM_LIMIT)


def _dot(a, b):
    return jnp.dot(a, b, preferred_element_type=F32)


def _dot_nt(a, b):
    return lax.dot_general(a, b, (((1,), (1,)), ((), ())), preferred_element_type=F32)


def _dot_tn(a, b):
    return lax.dot_general(a, b, (((0,), (0,)), ((), ())), preferred_element_type=F32)


def _split(x, pieces):
    out = []
    for _ in range(pieces):
        p = x.astype(BF16)
        out.append(p)
        x = x - p.astype(F32)
    return out


def _tri_left(tri, x, pieces=3):
    return sum(_dot(tri, p) for p in _split(x, pieces))


def _iota2(shape, dim):
    return lax.broadcasted_iota(jnp.int32, shape, dim)


def _tri(cond):
    return jnp.where(cond, 1.0, 0.0).astype(BF16)


def _sig_pair(x):
    e = jnp.exp(-jnp.abs(x))
    r = 1.0 / (1.0 + e)
    er = e * r
    pos = x >= 0
    return jnp.where(pos, r, er), jnp.where(pos, er, r)


def _expand(rows):
    return jnp.concatenate([jnp.broadcast_to(r, (SUB, HEAD)) for r in rows], axis=0)


def _exchange_shapes(arrays, modes):
    return tuple(jax.ShapeDtypeStruct((N_DEV,) + tuple(a.shape[1:] if m == "scatter" else a.shape), a.dtype)
                 for a, m in zip(arrays, modes))


def _exchange_sems(n):
    if n == 0:
        return []
    return [pltpu.SemaphoreType.DMA((n, N_DEV - 1)), pltpu.SemaphoreType.DMA((n, N_DEV - 1)),
            pltpu.SemaphoreType.DMA((n,))]


def _exchange_copies(ins, outs, modes, send_sems, recv_sems, local_sems):
    mx, my, mc = lax.axis_index("x"), lax.axis_index("y"), lax.axis_index("c")
    me = 4 * mx + 2 * my + mc

    def src(i, slot):
        return ins[i].at[slot] if modes[i] == "scatter" else ins[i]

    def peer_of(mask):
        px = 1 - mx if mask & 4 else mx
        py = 1 - my if mask & 2 else my
        pc = 1 - mc if mask & 1 else mc
        return px, py, pc

    def copy(i, mask, dst_slot):
        px, py, pc = peer_of(mask)
        return pltpu.make_async_remote_copy(
            src_ref=src(i, 4 * px + 2 * py + pc), dst_ref=outs[i].at[dst_slot],
            send_sem=send_sems.at[i, mask - 1], recv_sem=recv_sems.at[i, mask - 1],
            device_id=(px, py, pc), device_id_type=pl.DeviceIdType.MESH)

    n = len(ins)
    sends = [copy(i, mask, me) for mask in range(1, N_DEV) for i in range(n)]
    own = [pltpu.make_async_copy(src(i, me), outs[i].at[me], local_sems.at[i]) for i in range(n)]
    arrivals = []
    for mask in range(1, N_DEV):
        px, py, pc = peer_of(mask)
        arrivals += [copy(i, mask, 4 * px + 2 * py + pc) for i in range(n)]

    def start():
        for cp in sends + own:
            cp.start()

    def wait():
        for cp in arrivals:
            cp.wait_recv()
        for cp in sends:
            cp.wait_send()
        for cp in own:
            cp.wait()

    return start, wait


def _gather_once_per_chip(arrays, name):
    n = len(arrays)

    def body(*refs):
        ins, outs = refs[:n], refs[n:2 * n]
        send_sems, recv_sems, local_sems = refs[2 * n:]
        mx, my, mc = lax.axis_index("x"), lax.axis_index("y"), lax.axis_index("c")
        me, sibling = (mx, my, mc), (mx, my, 1 - mc)
        chips = [(1 - mx, my), (mx, 1 - my), (1 - mx, 1 - my)]

        def slot(px, py, pc):
            return 4 * px + 2 * py + pc

        def copy(i, k, block, to, src=None):
            return pltpu.make_async_remote_copy(
                src_ref=outs[i].at[slot(*block)] if src is None else src, dst_ref=outs[i].at[slot(*block)],
                send_sem=send_sems.at[i, k], recv_sem=recv_sems.at[i, k],
                device_id=to, device_id_type=pl.DeviceIdType.MESH)

        own = [pltpu.make_async_copy(ins[i], outs[i].at[slot(*me)], local_sems.at[i]) for i in range(n)]
        first = [copy(i, 0, me, sibling, src=ins[i]) for i in range(n)]
        first += [copy(i, 1 + j, me, (*chip, mc), src=ins[i]) for j, chip in enumerate(chips) for i in range(n)]
        for cp in own + first:
            cp.start()
        passed = []
        for j, chip in enumerate(chips):
            for i in range(n):
                copy(i, 1 + j, (*chip, mc), me).wait_recv()
                cp = copy(i, 4 + j, (*chip, mc), sibling)
                cp.start()
                passed.append(cp)
        for i in range(n):
            copy(i, 0, sibling, me).wait_recv()
            for j, chip in enumerate(chips):
                copy(i, 4 + j, (*chip, 1 - mc), me).wait_recv()
        for cp in first + passed:
            cp.wait_send()
        for cp in own:
            cp.wait()

    any_spec = pl.BlockSpec(memory_space=pl.ANY)
    return _pcall(
        body, name=name, out_shape=_exchange_shapes(arrays, ["gather"] * n),
        in_specs=[any_spec] * n, out_specs=tuple([any_spec] * n),
        scratch_shapes=_exchange_sems(n),
    )(*arrays)


def _exchange(arrays, modes, name):
    n = len(arrays)

    def body(*refs):
        start, wait = _exchange_copies(refs[:n], refs[n:2 * n], modes, *refs[2 * n:])
        start()
        wait()

    any_spec = pl.BlockSpec(memory_space=pl.ANY)
    return _pcall(
        body, name=name, out_shape=_exchange_shapes(arrays, modes),
        in_specs=[any_spec] * n, out_specs=tuple([any_spec] * n),
        scratch_shapes=_exchange_sems(n),
    )(*arrays)


def _tile_blocks_of_x(seq, d):
    assert ROW_TILE == 3 * BLOCK and (seq + BLOCK) % ROW_TILE == 0
    per_seq = (seq + BLOCK) // ROW_TILE

    def spec(k):
        return pl.BlockSpec((None, BLOCK, d),
                            lambda i: (i // per_seq, jnp.maximum(3 * (i % per_seq) - 1 + k, 0), 0))

    return [spec(0), spec(1), spec(2)]


def _project(x, wn_ref, w_ref, main_ref, gate_ref, ynt_ref):
    p_n, _, s = w_ref.shape
    n_main = main_ref.shape[1] // s
    y = x * lax.rsqrt(jnp.mean(x * x, axis=-1, keepdims=True) + EPS) * wn_ref[...]
    yb = y.astype(BF16)
    ynt_ref[...] = y.T.astype(BF16)
    for p in range(p_n):
        r = _dot(yb, w_ref[p])
        if p < n_main:
            main_ref[:, p * s:(p + 1) * s] = r.astype(main_ref.dtype)
        else:
            gate_ref[:, (p - n_main) * s:(p - n_main + 1) * s] = r


def _projection_specs(t, d, w_all, main_dtype, tm):
    p_n, _, s = w_all.shape
    row = pl.BlockSpec((tm, d), lambda i: (i, 0))
    return ([pl.BlockSpec((1, d), lambda i: (0, 0)), pl.BlockSpec((p_n, d, s), lambda i: (0, 0, 0))],
            (pl.BlockSpec((tm, 3 * d), lambda i: (i, 0)), row, pl.BlockSpec((d, tm), lambda i: (0, i))),
            (jax.ShapeDtypeStruct((t, 3 * d), main_dtype), jax.ShapeDtypeStruct((t, d), F32),
             jax.ShapeDtypeStruct((d, t), BF16)))


def _norm_inproj(h, wnorm, w_all, main_dtype, name, from_x=None):
    p_n, d, s = w_all.shape
    n_main = 3 * d // s
    tm = ROW_TILE
    if from_x is None:
        t = h.shape[0]
        lead, lead_specs = [h], [pl.BlockSpec((tm, d), lambda i: (i, 0))]
    else:
        x_in, meta = from_x
        t = x_in.shape[0] * (x_in.shape[1] + BLOCK)
        tiles_per_seq = (x_in.shape[1] + BLOCK) // tm
        lead = [x_in, x_in, x_in, meta]
        lead_specs = _tile_blocks_of_x(x_in.shape[1], d) + [pl.BlockSpec((N_META, d), lambda i: (0, 0))]
    n_lead = len(lead)

    def body(*refs):
        wn_ref, w_ref = refs[n_lead:n_lead + 2]
        main_ref, gate_ref, ynt_ref = refs[n_lead + 2:n_lead + 5]
        if from_x is None:
            x = refs[0][...]
        else:
            first_tile = pl.program_id(0) % tiles_per_seq == 0
            meta_block = jnp.concatenate([jnp.zeros((N_PAD, d), F32), refs[3][...]], axis=0)
            x = jnp.concatenate([jnp.where(first_tile, meta_block, refs[0][...]), refs[1][...], refs[2][...]], axis=0)
            refs[n_lead + 5][...] = x
        _project(x, wn_ref, w_ref, main_ref, gate_ref, ynt_ref)

    row = pl.BlockSpec((tm, d), lambda i: (i, 0))
    extra_specs, extra_shapes = ((), ()) if from_x is None else ((row,), (jax.ShapeDtypeStruct((t, d), F32),))
    proj_in, proj_out, proj_shapes = _projection_specs(t, d, w_all, main_dtype, tm)
    return _pcall(
        body, name=name, grid=(t // tm,),
        in_specs=lead_specs + proj_in, out_specs=proj_out + extra_specs, out_shape=proj_shapes + extra_shapes,
        compiler_params=_params("parallel"),
    )(*lead, wnorm, w_all)


RING = 3


def _row_ring(srcs, bufs, sems, tm, n_steps):
    i = pl.program_id(0)

    def copy(k, step):
        slot = step % RING
        return pltpu.make_async_copy(srcs[k].at[pl.ds(step * tm, tm)], bufs[k].at[slot], sems.at[k, slot])

    @pl.when(i == 0)
    def _():
        for k in range(len(srcs)):
            for step in range(min(2, n_steps)):
                copy(k, step).start()

    @pl.when(i + 2 < n_steps)
    def _():
        for k in range(len(srcs)):
            copy(k, i + 2).start()

    for k in range(len(srcs)):
        copy(k, i).wait()
    return [bufs[k].at[i % RING] for k in range(len(srcs))]


def _mix_out(o, gate, h_in, w_out, w_on, w_post, head_norm, name, target=None, next_proj=None):
    t, d = o.shape
    n_heads = d // HEAD
    tm = ROW_TILE
    with_loss = target is not None
    if with_loss:
        tiles_per_seq = (target.shape[1] + BLOCK) // tm

    def body(o_ref, g_ref, h_ref, w_ref, won_ref, wp_ref, *rest):
        if with_loss:
            t0_ref, t1_ref, t2_ref, hout_ref, loss_ref = rest[:5]
            u_scr = rest[10]
            o_ref, g_ref, h_ref = _row_ring((o_ref, g_ref, h_ref), rest[12:15], rest[15], tm, t // tm)
        elif next_proj is not None:
            wn2_ref, w2_ref, hout_ref, y_ref, main_ref, gate_ref, ynt_ref, u_scr = rest
        else:
            hout_ref, y_ref, u_scr = rest
        for hh in range(n_heads):
            cs = slice(hh * HEAD, (hh + 1) * HEAD)
            oh = o_ref[:, cs]
            gt = g_ref[:, cs]
            if head_norm:
                oh = oh * lax.rsqrt(jnp.mean(oh * oh, axis=-1, keepdims=True) + EPS) * won_ref[...]
            u_scr[:, cs] = (oh * (gt * jax.nn.sigmoid(gt))).astype(BF16)
        y = _dot(u_scr[...], w_ref[...])
        r = y * lax.rsqrt(jnp.mean(y * y, axis=-1, keepdims=True) + EPS) * wp_ref[...]
        h_out = h_ref[...] + r
        if not with_loss:
            y_ref[...] = y
            hout_ref[...] = h_out
            if next_proj is not None:
                _project(h_out, wn2_ref, w2_ref, main_ref, gate_ref, ynt_ref)
            return
        i = pl.program_id(0)

        @pl.when(i == 0)
        def _():
            loss_ref[...] = jnp.zeros_like(loss_ref)

        tgt = jnp.concatenate([t0_ref[...], t1_ref[...], t2_ref[...]], axis=0)
        real = jnp.logical_or(i % tiles_per_seq > 0, _iota2((tm, d), 0) >= BLOCK)
        err = jnp.where(real, h_out - tgt, 0.0)
        dh = err * (1.0 / d)
        hout_ref[...] = dh
        part = jnp.sum(jnp.sum(err * err, axis=-1, keepdims=True), axis=0, keepdims=True)
        loss_ref[...] += part * (0.5 / d)
        _mix_out_bwd_tile(dh, y, o_ref, g_ref, w_ref, won_ref, wp_ref, *rest[5:12], head_norm, t // tm - 1)

    row = pl.BlockSpec((tm, d), lambda i: (i, 0))
    in_specs = [row, row, row, pl.BlockSpec((d, d), lambda i: (0, 0)),
                pl.BlockSpec((1, HEAD), lambda i: (0, 0)), pl.BlockSpec((1, d), lambda i: (0, 0))]
    out_specs, out_shape = (row, row), (jax.ShapeDtypeStruct((t, d), F32), jax.ShapeDtypeStruct((t, d), F32))
    scratch = [pltpu.VMEM((tm, d), BF16)]
    args = (o, gate, h_in, w_out, w_on, w_post)
    if with_loss:
        in_specs += _tile_blocks_of_x(target.shape[1], d)
        bwd_specs, bwd_shapes, scratch = _mix_out_bwd_outputs(t, d, tm)
        in_specs[:3] = [pl.BlockSpec(memory_space=pl.ANY)] * 3
        scratch += [pltpu.VMEM((RING, tm, d), F32)] * 3 + [pltpu.SemaphoreType.DMA((3, RING))]
        out_specs = (row, pl.BlockSpec((8, 128), lambda i: (0, 0))) + bwd_specs
        out_shape = (jax.ShapeDtypeStruct((t, d), F32), jax.ShapeDtypeStruct((8, 128), F32)) + bwd_shapes
        args += (target, target, target)
    elif next_proj is not None:
        proj_in, proj_out, proj_shapes = _projection_specs(t, d, next_proj[1], next_proj[2], tm)
        in_specs += proj_in
        out_specs += proj_out
        out_shape += proj_shapes
        args += (next_proj[0], next_proj[1])
    return _pcall(
        body, name=name, grid=(t // tm,), in_specs=in_specs, out_specs=out_specs, out_shape=out_shape,
        scratch_shapes=scratch,
        compiler_params=_params("arbitrary" if with_loss else "parallel"),
    )(*args)


def _mix_out_bwd_tile(dr, yv, o_ref, g_ref, w_ref, won_ref, wp_ref,
                      do_ref, dg_ref, dw_ref, dwp_ref, dwon_ref, u_scr, acc, head_norm, last):
    i = pl.program_id(0)
    n_heads = o_ref.shape[1] // HEAD

    @pl.when(i == 0)
    def _():
        acc[...] = jnp.zeros_like(acc)
        dwp_ref[...] = jnp.zeros_like(dwp_ref)
        dwon_ref[...] = jnp.zeros_like(dwon_ref)

    rs = lax.rsqrt(jnp.mean(yv * yv, axis=-1, keepdims=True) + EPS)
    yh = yv * rs
    dwp_ref[...] += jnp.sum(dr * yh, axis=0, keepdims=True)
    wd = dr * wp_ref[...]
    dy = rs * (wd - yh * jnp.mean(wd * yh, axis=-1, keepdims=True))
    dyb = dy.astype(BF16)
    du = _dot_nt(dyb, w_ref[...])
    for hh in range(n_heads):
        cs = slice(hh * HEAD, (hh + 1) * HEAD)
        oh = o_ref[:, cs]
        gt = g_ref[:, cs]
        sg = jax.nn.sigmoid(gt)
        sl = gt * sg
        duh = du[:, cs]
        if head_norm:
            rsh = lax.rsqrt(jnp.mean(oh * oh, axis=-1, keepdims=True) + EPS)
            ohat = oh * rsh
            on = ohat * won_ref[...]
        else:
            on = oh
        u_scr[:, cs] = (on * sl).astype(BF16)
        don = duh * sl
        dg_ref[:, cs] = (duh * on * (sg * (1.0 + gt * (1.0 - sg)))).astype(BF16)
        if head_norm:
            dwon_ref[...] += jnp.sum(don * ohat, axis=0, keepdims=True)
            wdn = don * won_ref[...]
            do_ref[:, cs] = rsh * (wdn - ohat * jnp.mean(wdn * ohat, axis=-1, keepdims=True))
        else:
            do_ref[:, cs] = don
    acc[...] += _dot_tn(u_scr[...], dyb)

    @pl.when(i == last)
    def _():
        dw_ref[...] = acc[...].astype(BF16)


def _mix_out_bwd_outputs(t, d, tm):
    row = pl.BlockSpec((tm, d), lambda i: (i, 0))
    specs = (row, row, pl.BlockSpec((d, d), lambda i: (0, 0)), pl.BlockSpec((1, d), lambda i: (0, 0)),
             pl.BlockSpec((1, HEAD), lambda i: (0, 0)))
    shapes = (jax.ShapeDtypeStruct((t, d), F32), jax.ShapeDtypeStruct((t, d), BF16),
              jax.ShapeDtypeStruct((d, d), BF16), jax.ShapeDtypeStruct((1, d), F32),
              jax.ShapeDtypeStruct((1, HEAD), F32))
    return specs, shapes, [pltpu.VMEM((tm, d), BF16), pltpu.VMEM((d, d), F32)]


def _mix_out_bwd(dh, y, o, gate, w_out, w_on, w_post, head_norm, name):
    t, d = o.shape
    tm = ROW_TILE
    last = t // tm - 1

    def body(dh_ref, y_ref, *refs):
        _mix_out_bwd_tile(dh_ref[...], y_ref[...], *refs, head_norm, last)

    row = pl.BlockSpec((tm, d), lambda i: (i, 0))
    out_specs, out_shape, scratch = _mix_out_bwd_outputs(t, d, tm)
    return _pcall(
        body, name=name, grid=(t // tm,),
        in_specs=[row, row, row, row, pl.BlockSpec((d, d), lambda i: (0, 0)),
                  pl.BlockSpec((1, HEAD), lambda i: (0, 0)), pl.BlockSpec((1, d), lambda i: (0, 0))],
        out_specs=out_specs, out_shape=out_shape, scratch_shapes=scratch,
        compiler_params=_params("arbitrary"),
    )(dh, y, o, gate, w_out, w_on, w_post)


def _inproj_bwd_x(dparts, w_all, h, wnorm, dres, ride, ride_modes, name):
    t, d = h.shape
    p_n, _, s = w_all.shape
    per = d // s
    tm = ROW_TILE
    nr = len(ride)
    grid = (t // tm,)

    def body(*refs):
        d0_ref, d1_ref, d2_ref, d3_ref, w_ref, h_ref, wn_ref, dres_ref = refs[:8]
        dh_ref, dwn_ref = refs[8 + nr:10 + nr]
        finish_ride = _ride_along(refs[8:8 + nr], refs[10 + nr:10 + 2 * nr], ride_modes, refs[10 + 2 * nr:], grid)
        i = pl.program_id(0)

        @pl.when(i == 0)
        def _():
            dwn_ref[...] = jnp.zeros_like(dwn_ref)

        pieces = (d0_ref, d1_ref, d2_ref, d3_ref)
        dyn = jnp.zeros((tm, d), F32)
        for p in range(p_n):
            blk = pieces[p // per][:, (p % per) * s:(p % per + 1) * s]
            dyn = dyn + _dot_nt(blk, w_ref[p])
        x = h_ref[...]
        rs = lax.rsqrt(jnp.mean(x * x, axis=-1, keepdims=True) + EPS)
        xh = x * rs
        dwn_ref[...] += jnp.sum(dyn * xh, axis=0, keepdims=True)
        wd = dyn * wn_ref[...]
        dh_ref[...] = dres_ref[...] + rs * (wd - xh * jnp.mean(wd * xh, axis=-1, keepdims=True))
        finish_ride()

    row = pl.BlockSpec((tm, d), lambda i: (i, 0))
    any_spec = pl.BlockSpec(memory_space=pl.ANY)
    return _pcall(
        body, name=name, grid=grid,
        in_specs=[row, row, row, row, pl.BlockSpec((p_n, d, s), lambda i: (0, 0, 0)),
                  row, pl.BlockSpec((1, d), lambda i: (0, 0)), row] + [any_spec] * nr,
        out_specs=(row, pl.BlockSpec((1, d), lambda i: (0, 0))) + (any_spec,) * nr,
        out_shape=(jax.ShapeDtypeStruct((t, d), F32), jax.ShapeDtypeStruct((1, d), F32))
        + _exchange_shapes(ride, ride_modes),
        scratch_shapes=_exchange_sems(nr),
        compiler_params=_params("arbitrary"),
    )(*dparts, w_all, h, wnorm, dres, *ride)


def _inproj_bwd_w(ynt, dparts, s, name):
    d, t = ynt.shape
    per = d // s
    n_sh = len(dparts) * per
    tk = next(c for c in K_TILES if t % c == 0)
    last = t // tk - 1

    def body(ynt_ref, d0_ref, d1_ref, d2_ref, d3_ref, dw_ref, acc):
        p, i = pl.program_id(0), pl.program_id(1)

        @pl.when(i == 0)
        def _():
            acc[...] = jnp.zeros_like(acc)

        for a, piece in enumerate((d0_ref, d1_ref, d2_ref, d3_ref)):
            @pl.when(p // per == a)
            def _():
                acc[...] += _dot(ynt_ref[...], piece[...])

        @pl.when(i == last)
        def _():
            dw_ref[...] = acc[...].astype(BF16)

    def piece_spec(a):
        return pl.BlockSpec((tk, s), lambda p, i: (jnp.where(p // per == a, i, 0),
                                                   jnp.where(p // per == a, p % per, 0)))

    return _pcall(
        body, name=name, grid=(n_sh, t // tk),
        in_specs=[pl.BlockSpec((d, tk), lambda p, i: (0, i))] + [piece_spec(a) for a in range(4)],
        out_specs=pl.BlockSpec((None, d, s), lambda p, i: (p, 0, 0)),
        out_shape=jax.ShapeDtypeStruct((n_sh, d, s), BF16),
        scratch_shapes=[pltpu.VMEM((d, s), F32)],
        compiler_params=_params("parallel", "arbitrary"),
    )(ynt, *dparts)


def _hgrn_gates(fz_ref, lb, b_ref, k_ref):
    sig, nsig = _sig_pair(fz_ref[...])
    f = lb + (1.0 - lb) * sig
    rr, cc = _iota2((BLOCK, BLOCK), 0), _iota2((BLOCK, BLOCK), 1)
    b_ref[...] = _tri_left(_tri(cc <= rr), jnp.log(f))
    k_ref[...] = (1.0 - lb) * nsig


def _hgrn_common(q_ref, b_ref, k_ref):
    b, k = b_ref[...], k_ref[...]
    bend = [b_ref[pl.ds(SUB * j + SUB - 1, 1), :] for j in range(N_SUB)]
    bref = [jnp.zeros((1, HEAD), F32)] + bend[:-1]
    refrow, bendrow = _expand(bref), _expand(bend)
    e_q = jnp.exp(b - refrow)
    e_k = jnp.exp(bendrow - b)
    qt = q_ref[...] * e_q
    kh = k * e_k
    bl = bend[-1]
    return dict(k=k, b=b, bend=bend, bref=bref, refrow=refrow, e_q=e_q, e_k=e_k, qt=qt, kh=kh, bl=bl)


HEADS_PER_STEP = 8


def _lockstep(chunks):
    live = list(chunks)
    while live:
        still = []
        for gen in live:
            try:
                next(gen)
                still.append(gen)
            except StopIteration:
                pass
        live = still


def _ride_along(ride_ins, ride_outs, modes, sems, grid):
    if not ride_ins:
        return lambda: None
    ids = [pl.program_id(a) for a in range(len(grid))]
    start, wait = _exchange_copies(ride_ins, ride_outs, modes, *sems)
    first, last = ids[0] == 0, ids[0] == grid[0] - 1
    for a in range(1, len(grid)):
        first = jnp.logical_and(first, ids[a] == 0)
        last = jnp.logical_and(last, ids[a] == grid[a] - 1)
    pl.when(first)(start)
    return lambda: pl.when(last)(wait)


def _hgrn_fwd(main, lbrow, bsz, n_blocks, ride, ride_modes, name):
    t, d3 = main.shape
    d = d3 // 3
    n_heads = d // HEAD
    n_pairs = n_heads // HEADS_PER_STEP
    wide = HEADS_PER_STEP * HEAD
    nr = len(ride)
    grid = (bsz, n_pairs, n_blocks)

    def chunk(q_ref, fz_ref, v_ref, lb, o_ref, st_ref, b_scr, k_scr, ad_ref, s_scr, o_acc, ad_scr):
        _hgrn_gates(fz_ref, lb, b_scr, k_scr)
        yield
        c = _hgrn_common(q_ref, b_scr, k_scr)
        yield
        s_t = s_scr[...]
        st_ref[...] = s_t
        vb = v_ref[...].astype(BF16)
        q_state = c["qt"] * _expand([jnp.exp(r) for r in c["bref"]])
        o_state = _dot_nt(q_state.astype(BF16), s_t.astype(BF16))
        js = range(N_SUB - 1)
        khb = c["kh"].astype(BF16)
        lhs = [(c["qt"][SUB * (j + 1):] * jnp.exp(c["refrow"][SUB * (j + 1):] - c["bend"][j])).astype(BF16)
               for j in js]
        yield
        a_js = [_dot_nt(lhs[j], khb[SUB * j:SUB * (j + 1)]) for j in js]
        k_state = c["kh"] * _expand([jnp.exp(c["bl"] - r) for r in c["bend"]])
        s_scr[...] = s_t * jnp.exp(c["bl"]) + _dot_tn(vb, k_state.astype(BF16))
        yield
        o_js = [_dot(a_js[j].astype(BF16), vb[SUB * j:SUB * (j + 1)]) for j in js]
        yield
        o_acc[...] = o_state
        for j in js:
            o_acc[SUB * (j + 1):, :] += o_js[j]
        not_before = [_iota2((HALF, HEAD), 0) >= row for row in range(HALF)]
        lane = _iota2((HALF, HEAD), 1)
        for i in range(N_SUB):
            r0 = SUB * i
            q_h = [q_ref[pl.ds(r0 + HALF * u, HALF), :] for u in range(N_HALF)]
            b_h = [b_scr[pl.ds(r0 + HALF * u, HALF), :] for u in range(N_HALF)]
            a_h = [jnp.zeros((HALF, HEAD), F32) for _ in range(N_HALF)]
            for s in range(SUB):
                brow = b_scr[pl.ds(r0 + s, 1), :]
                krow = k_scr[pl.ds(r0 + s, 1), :]
                is_lane = lane == r0 + s
                for u in range(s // HALF, N_HALF):
                    diff = b_h[u] - brow
                    if u == s // HALF:
                        diff = jnp.where(not_before[s - HALF * u], diff, NEG_BIG)
                    col = jnp.sum(q_h[u] * krow * jnp.exp(diff), axis=-1, keepdims=True)
                    a_h[u] = jnp.where(is_lane, col, a_h[u])
            for u in range(N_HALF):
                ad_scr[pl.ds(r0 + HALF * u, HALF), :] = a_h[u]
            yield
        a_diag = ad_scr[...].astype(BF16)
        ad_ref[...] = a_diag
        o_ref[...] = o_acc[...] + _dot(a_diag, vb)

    def body(*refs):
        q_ref, fz_ref, v_ref, lb_ref = refs[:4]
        o_ref, st_ref, b_out, k_out, ad_out = refs[4 + nr:9 + nr]
        s_scr, o_acc, ad_scr = refs[9 + 2 * nr:12 + 2 * nr]
        finish_ride = _ride_along(refs[4:4 + nr], refs[9 + nr:9 + 2 * nr], ride_modes, refs[12 + 2 * nr:], grid)

        @pl.when(pl.program_id(2) == 0)
        def _():
            s_scr[...] = jnp.zeros_like(s_scr)

        def head(hh):
            cols = pl.ds(hh * HEAD, HEAD)
            return chunk(q_ref.at[:, cols], fz_ref.at[:, cols], v_ref.at[:, cols], lb_ref[:, cols],
                         o_ref.at[:, cols], st_ref.at[hh], b_out.at[:, cols], k_out.at[:, cols],
                         ad_out.at[:, cols], s_scr.at[hh], o_acc.at[hh], ad_scr.at[hh])

        _lockstep([head(hh) for hh in range(HEADS_PER_STEP)])
        finish_ride()

    def blk(col0):
        return pl.BlockSpec((BLOCK, wide), lambda b, h, n: (b * n_blocks + n, col0 + h))

    any_spec = pl.BlockSpec(memory_space=pl.ANY)
    per_head = pltpu.VMEM((HEADS_PER_STEP, BLOCK, HEAD), F32)
    return _pcall(
        body, name=name, grid=grid,
        in_specs=[blk(0), blk(n_pairs), blk(2 * n_pairs), pl.BlockSpec((1, wide), lambda b, h, n: (0, h))]
        + [any_spec] * nr,
        out_specs=(blk(0), pl.BlockSpec((None, HEADS_PER_STEP, HEAD, HEAD), lambda b, h, n: (b * n_blocks + n, h, 0, 0)),
                   blk(0), blk(0), blk(0)) + (any_spec,) * nr,
        out_shape=(jax.ShapeDtypeStruct((t, d), F32),
                   jax.ShapeDtypeStruct((bsz * n_blocks, n_heads, HEAD, HEAD), F32),
                   jax.ShapeDtypeStruct((t, d), F32), jax.ShapeDtypeStruct((t, d), F32),
                   jax.ShapeDtypeStruct((t, d), BF16))
        + _exchange_shapes(ride, ride_modes),
        scratch_shapes=[per_head] * 3 + _exchange_sems(nr),
        compiler_params=_params("arbitrary", "arbitrary", "arbitrary"),
    )(main, main, main, lbrow, *ride)


def _hgrn_bwd(main, b_all, k_all, a_diag, lbrow, states, do, bsz, n_blocks, ride, ride_modes, name):
    t, d3 = main.shape
    d = d3 // 3
    n_pairs = d // HEAD // HEADS_PER_STEP
    wide = HEADS_PER_STEP * HEAD
    nr = len(ride)
    grid = (bsz, n_pairs, n_blocks)

    def chunk(n, q_ref, b_scr, k_scr, ad_ref, v_ref, lb, st_ref, do_ref, dq_ref, dfz_ref, dv_ref, dlb_ref,
              ds_scr, dqt_acc, dkh_acc, dv_acc, dqd_acc, dkd_acc):
        c = _hgrn_common(q_ref, b_scr, k_scr)
        yield
        q, k = q_ref[...], c["k"]
        vb = v_ref[...].astype(BF16)
        dob = do_ref[...].astype(BF16)
        s0_t = st_ref[...]
        ds1_t = ds_scr[...]
        e_ref = _expand([jnp.exp(r) for r in c["bref"]])
        e_end = _expand([jnp.exp(c["bl"] - r) for r in c["bend"]])
        e_bl = jnp.exp(c["bl"])
        q_state = c["qt"] * e_ref
        k_state = c["kh"] * e_end
        dq_state = _dot(dob, s0_t.astype(BF16))
        dk_state = _dot(vb, ds1_t.astype(BF16))
        dv_state = _dot_nt(k_state.astype(BF16), ds1_t.astype(BF16))
        ds_scr[...] = ds1_t * e_bl + _dot_tn(dob, q_state.astype(BF16))
        js = range(N_SUB - 1)
        lo = [slice(SUB * j, SUB * (j + 1)) for j in js]
        khb = c["kh"].astype(BF16)
        dj = [jnp.exp(c["refrow"][SUB * (j + 1):] - c["bend"][j]) for j in js]
        lhs = [(c["qt"][SUB * (j + 1):] * dj[j]).astype(BF16) for j in js]
        yield
        a_js = [_dot_nt(lhs[j], khb[lo[j]]) for j in js]
        da_js = [_dot_nt(dob[SUB * (j + 1):], vb[lo[j]]).astype(BF16) for j in js]
        dqt_acc[...] = dq_state * e_ref
        dkh_acc[...] = dk_state * e_end
        dv_acc[...] = dv_state
        dbl = (jnp.sum(s0_t * ds1_t, axis=0, keepdims=True) * e_bl
               + jnp.sum(k_state * dk_state, axis=0, keepdims=True))
        yield
        dv_js = [_dot_tn(a_js[j].astype(BF16), dob[SUB * (j + 1):]) for j in js]
        dq_js = [_dot(da_js[j], khb[lo[j]]) * dj[j] for j in js]
        dk_js = [_dot_tn(da_js[j], lhs[j]) for j in js]
        yield
        for j in js:
            dv_acc[lo[j], :] += dv_js[j]
            dqt_acc[SUB * (j + 1):, :] += dq_js[j]
            dkh_acc[lo[j], :] += dk_js[j]
        not_before = [_iota2((HALF, HEAD), 0) >= row for row in range(HALF)]
        is_row = [_iota2((HALF, HEAD), 0) == row for row in range(HALF)]
        for i in range(N_SUB):
            r0 = SUB * i
            q_h = [q_ref[pl.ds(r0 + HALF * u, HALF), :] for u in range(N_HALF)]
            b_h = [b_scr[pl.ds(r0 + HALF * u, HALF), :] for u in range(N_HALF)]
            do_h = [do_ref[pl.ds(r0 + HALF * u, HALF), :] for u in range(N_HALF)]
            zero = jnp.zeros((HALF, HEAD), F32)
            dq_h, dk_h = [zero] * N_HALF, [zero] * N_HALF
            for s in range(SUB):
                brow = b_scr[pl.ds(r0 + s, 1), :]
                krow = k_scr[pl.ds(r0 + s, 1), :]
                vrow = v_ref[pl.ds(r0 + s, 1), :]
                dk_row = jnp.zeros((1, HEAD), F32)
                for u in range(s // HALF, N_HALF):
                    diff = b_h[u] - brow
                    if u == s // HALF:
                        diff = jnp.where(not_before[s - HALF * u], diff, NEG_BIG)
                    w = jnp.exp(diff)
                    qw = q_h[u] * w
                    da_col = jnp.sum(do_h[u] * vrow, axis=-1, keepdims=True)
                    dq_h[u] = dq_h[u] + da_col * (w * krow)
                    dk_row = dk_row + jnp.sum(da_col * qw, axis=0, keepdims=True)
                us = s // HALF
                dk_h[us] = jnp.where(is_row[s - HALF * us], dk_row, dk_h[us])
            for u in range(N_HALF):
                rows = pl.ds(r0 + HALF * u, HALF)
                dqd_acc[rows, :] = dq_h[u]
                dkd_acc[rows, :] = dk_h[u]
            yield
        dv_in = _dot_tn(ad_ref[...], dob)
        dq = dqt_acc[...] * c["e_q"] + dqd_acc[...]
        dk = dkh_acc[...] * c["e_k"] + dkd_acc[...]
        rr, cc = _iota2((BLOCK, BLOCK), 0), _iota2((BLOCK, BLOCK), 1)
        db = q * dq - k * dk + jnp.where(_iota2((BLOCK, HEAD), 0) == BLOCK - 1, dbl, 0.0)
        yield
        dg = _tri_left(_tri(cc >= rr), db)
        yield
        real = jnp.logical_or(n > 0, _iota2((BLOCK, HEAD), 0) >= N_PAD)
        df = jnp.where(real, dg / (1.0 - k) - dk, 0.0)
        nsig = k * (1.0 / (1.0 - lb))
        dq_ref[...] = dq.astype(BF16)
        dv_ref[...] = (dv_acc[...] + dv_in).astype(BF16)
        dfz_ref[...] = (df * k * (1.0 - nsig)).astype(BF16)
        dlb_ref[...] += jnp.sum(df * nsig, axis=0, keepdims=True)

    def body(*refs):
        q_ref, b_ref, k_ref, ad_ref, v_ref, lb_ref, st_ref, do_ref = refs[:8]
        dq_ref, dfz_ref, dv_ref, dlb_ref = refs[8 + nr:12 + nr]
        scratch = refs[12 + 2 * nr:18 + 2 * nr]
        finish_ride = _ride_along(refs[8:8 + nr], refs[12 + nr:12 + 2 * nr], ride_modes, refs[18 + 2 * nr:], grid)
        step = pl.program_id(2)

        @pl.when(step == 0)
        def _():
            scratch[0][...] = jnp.zeros_like(scratch[0])
            dlb_ref[...] = jnp.zeros_like(dlb_ref)

        def head(hh):
            cols = pl.ds(hh * HEAD, HEAD)
            return chunk(n_blocks - 1 - step, q_ref.at[:, cols], b_ref.at[:, cols], k_ref.at[:, cols],
                         ad_ref.at[:, cols], v_ref.at[:, cols], lb_ref[:, cols], st_ref.at[hh], do_ref.at[:, cols],
                         dq_ref.at[:, cols],
                         dfz_ref.at[:, cols], dv_ref.at[:, cols], dlb_ref.at[:, cols],
                         *[scr.at[hh] for scr in scratch])

        _lockstep([head(hh) for hh in range(HEADS_PER_STEP)])
        finish_ride()

    def blk(col0):
        return pl.BlockSpec((BLOCK, wide), lambda b, h, s: (b * n_blocks + n_blocks - 1 - s, col0 + h))

    any_spec = pl.BlockSpec(memory_space=pl.ANY)
    per_head = pltpu.VMEM((HEADS_PER_STEP, BLOCK, HEAD), F32)
    return _pcall(
        body, name=name, grid=grid,
        in_specs=[blk(0), blk(0), blk(0), blk(0), blk(2 * n_pairs), pl.BlockSpec((1, wide), lambda b, h, s: (0, h)),
                  pl.BlockSpec((None, HEADS_PER_STEP, HEAD, HEAD),
                               lambda b, h, s: (b * n_blocks + n_blocks - 1 - s, h, 0, 0)),
                  blk(0)] + [any_spec] * nr,
        out_specs=(blk(0), blk(0), blk(0), pl.BlockSpec((None, 1, wide), lambda b, h, s: (b, 0, h)))
        + (any_spec,) * nr,
        out_shape=(jax.ShapeDtypeStruct((t, d), BF16),) * 3 + (jax.ShapeDtypeStruct((bsz, 1, d), F32),)
        + _exchange_shapes(ride, ride_modes),
        scratch_shapes=[per_head] * 6 + _exchange_sems(nr),
        compiler_params=_params("arbitrary", "arbitrary", "arbitrary"),
    )(main, b_all, k_all, a_diag, main, lbrow, states, do, *ride)


SB_GROUP = 3
SB_ROWS = SB_GROUP * BLOCK
SB_DEAD = -104.0
SB_NEAR = ((0, 0), (1, 0), (0, 1))
SB_HEADS_FWD = 4
SB_HEADS_BWD = 2
SB_STATIC_OWN = tuple((r, kb) for kb in range(SB_GROUP - 1, -1, -1) for r in range(kb, SB_GROUP))
SB_STATIC_NEAR = tuple((r, -1 - back) for r, back in SB_NEAR)
N_STATIC = len(SB_STATIC_OWN) + len(SB_STATIC_NEAR)


def _sb_tables():
    j = jnp.bitwise_and(_iota2((2 * BLOCK, 2 * BLOCK), 0), BLOCK - 1)
    s = _iota2((2 * BLOCK, 2 * BLOCK), 1)
    ones = s >= BLOCK
    return (_tri(jnp.logical_or(ones, j > s)), _tri(jnp.logical_or(ones, j <= s)),
            _tri(jnp.logical_or(ones, j < s)))


def _sums(x, table, pieces=2):
    hi = x.astype(BF16)
    if pieces == 1:
        r = _dot(hi, table[:BLOCK])
    else:
        lo = (x - hi.astype(F32)).astype(BF16)
        r = _dot(jnp.concatenate([hi, lo], axis=1), table)
    return r[:, :BLOCK], r[:, BLOCK:]


def _sb_logits(q, ks, scale, causal, pad_row):
    z = _dot_nt(q, ks) * scale
    log_keep = -(jnp.maximum(z, 0.0) + jnp.log(1.0 + jnp.exp(-jnp.abs(z))))
    log_beta = z + log_keep
    if causal is not None:
        log_keep = jnp.where(causal, log_keep, 0.0)
    if pad_row is not None:
        log_keep = log_keep * pad_row
    return z, log_beta, log_keep


def _sb_fwd(qkv, bsz, n_blocks, name):
    t, d3 = qkv.shape
    d = d3 // 3
    n_heads = d // HEAD
    lp = n_blocks * BLOCK
    n_groups = n_blocks // SB_GROUP
    assert n_groups * SB_GROUP == n_blocks
    scale = HEAD ** -0.5

    n_pairs = n_heads // SB_HEADS_FWD
    wide = SB_HEADS_FWD * HEAD

    def body(q_ref, k_ref, v_ref, upper_ref, o_ref, tot_ref, stop_ref, w_keep, lb_keep, c_scr):
        g = pl.program_id(2)
        upper = upper_ref[...]
        causal = _iota2((BLOCK, BLOCK), 1) < _iota2((BLOCK, BLOCK), 0)
        lane = _iota2((1, BLOCK), 1)
        o_ref[...] = jnp.zeros_like(o_ref)
        c_scr[...] = jnp.zeros_like(c_scr)

        def tiles(jobs, keep=None):
            cols = [pl.ds(hh * HEAD, HEAD) for hh, _, _, _, _ in jobs]
            qrows = [pl.ds(r * BLOCK, BLOCK) for _, r, _, _, _ in jobs]
            krows = [pl.ds(pl.multiple_of(m * BLOCK, BLOCK), BLOCK) for _, _, m, _, _ in jobs]
            n_jobs = range(len(jobs))
            lg = [_sb_logits(q_ref[qrows[i], cols[i]], k_ref[krows[i], cols[i]], scale,
                             causal if jobs[i][3] else None, jobs[i][4]) for i in n_jobs]
            sm = [_sums(x[2], upper) for x in lg]
            a_all = []
            for i in n_jobs:
                c = c_scr[qrows[i], cols[i]]
                a = jnp.exp(lg[i][1] + c + sm[i][0])
                a_all.append((jnp.where(causal, a, 0.0) if jobs[i][3] else a).astype(BF16))
                c_scr[qrows[i], cols[i]] = c + sm[i][1]
            out = [_dot(a_all[i], v_ref[krows[i], cols[i]]) for i in n_jobs]
            for i in n_jobs:
                o_ref[qrows[i], cols[i]] += out[i]
                if keep is not None:
                    w_keep[jobs[i][0], keep[i]] = a_all[i]
                    lb_keep[jobs[i][0], keep[i]] = lg[i][1].astype(BF16)

        def pad_row_of(m):
            return jnp.where(jnp.logical_or(m > 0, lane >= N_PAD), 1.0, 0.0)

        base = SB_GROUP * g
        heads = range(SB_HEADS_FWD)
        own = [(hh, r, base + rel, r == rel, pad_row_of(base) if rel == 0 else None)
               for hh in heads for r, rel in SB_STATIC_OWN]
        near = [(hh, r, base + rel, False, None) for hh in heads for r, rel in SB_STATIC_NEAR]
        own_slots = [i for _ in heads for i in range(len(SB_STATIC_OWN))]
        near_slots = [len(SB_STATIC_OWN) + i for _ in heads for i in range(len(SB_STATIC_NEAR))]
        pl.when(g == 0)(lambda: tiles(own, keep=own_slots))
        pl.when(g > 0)(lambda: tiles(own + near, keep=own_slots + near_slots))

        stop_ref[...] = jnp.zeros_like(stop_ref)
        mine = [(hh, r) for hh in heads for r in range(SB_GROUP)]
        reach = [jnp.max(c_scr[pl.ds(r * BLOCK, BLOCK), pl.ds(hh * HEAD, HEAD)]) for hh, r in mine]
        for (hh, r), reach_now in zip(mine, reach):
            def live(carry):
                m, c_max = carry
                return jnp.logical_and(m >= 0, c_max >= SB_DEAD)

            def step(carry, hh=hh, r=r):
                m, _ = carry
                tiles([(hh, r, m, False, pad_row_of(m))])
                return m - 1, jnp.max(c_scr[pl.ds(r * BLOCK, BLOCK), pl.ds(hh * HEAD, HEAD)])

            lowest = jnp.maximum(base - (SB_GROUP - 1 - r), 0)
            m_end, _ = lax.while_loop(live, step, (lowest - 1, reach_now))
            stop_ref[hh, pl.ds(r, 1), :] = jnp.broadcast_to((m_end + 1).astype(F32), (1, 128))
        tot_ref[...] = c_scr[...]

    qblk = pl.BlockSpec((SB_ROWS, wide), lambda b, h, g: (b * n_groups + g, h))
    table = pl.BlockSpec((2 * BLOCK, 2 * BLOCK), lambda b, h, g: (0, 0))
    stops = pl.BlockSpec((None, None, SB_HEADS_FWD, 8, 128), lambda b, h, g: (b, g, h, 0, 0))
    kept = pl.BlockSpec((None, None, SB_HEADS_FWD, N_STATIC, BLOCK, BLOCK), lambda b, h, g: (b, g, h, 0, 0, 0))
    return _pcall(
        body, name=name, grid=(bsz, n_pairs, n_groups),
        in_specs=[qblk, pl.BlockSpec((lp, wide), lambda b, h, g: (b, n_pairs + h)),
                  pl.BlockSpec((lp, wide), lambda b, h, g: (b, 2 * n_pairs + h)), table],
        out_specs=(qblk, qblk, stops, kept, kept),
        out_shape=(jax.ShapeDtypeStruct((t, d), F32), jax.ShapeDtypeStruct((t, d), F32),
                   jax.ShapeDtypeStruct((bsz, n_groups, n_heads, 8, 128), F32))
        + (jax.ShapeDtypeStruct((bsz, n_groups, n_heads, N_STATIC, BLOCK, BLOCK), BF16),) * 2,
        scratch_shapes=[pltpu.VMEM((SB_ROWS, wide), F32)],
        compiler_params=_params("parallel", "parallel", "arbitrary"),
    )(qkv, qkv, qkv, _sb_tables()[0])


def _sb_bwd(qkv, do, tot, stop, w_kept, lb_kept, bsz, n_blocks, name):
    t, d3 = qkv.shape
    d = d3 // 3
    n_heads = d // HEAD
    lp = n_blocks * BLOCK
    n_groups = n_blocks // SB_GROUP
    scale = HEAD ** -0.5

    def body(q_ref, k_ref, v_ref, do_ref, tot_ref, stop_ref, incl_ref, excl_ref, w_kept, lb_kept,
             dq_ref, dk_ref, dv_ref, dk_acc, dv_acc, dq_acc, p_scr, e_scr, dob_scr):
        g = pl.program_id(2)

        @pl.when(g == 0)
        def _():
            dk_acc[...] = jnp.zeros_like(dk_acc)
            dv_acc[...] = jnp.zeros_like(dv_acc)

        incl, excl = incl_ref[...], excl_ref[...]
        causal = _iota2((BLOCK, BLOCK), 1) < _iota2((BLOCK, BLOCK), 0)
        lane = _iota2((1, BLOCK), 1)
        dob_scr[...] = do_ref[...].astype(BF16)
        dq_acc[...] = jnp.zeros_like(dq_acc)
        p_scr[...] = jnp.zeros_like(p_scr)
        e_scr[...] = jnp.zeros_like(e_scr)

        def tiles(jobs, slots=None):
            n_jobs = range(len(jobs))
            cols = [pl.ds(hh * HEAD, HEAD) for hh, _, _, _, _ in jobs]
            qrows = [pl.ds(r * BLOCK, BLOCK) for _, r, _, _, _ in jobs]
            krows = [pl.ds(pl.multiple_of(m * BLOCK, BLOCK), BLOCK) for _, _, m, _, _ in jobs]
            diag = [dg for _, _, _, dg, _ in jobs]
            d_a = [_dot_nt(dob_scr[qrows[i], cols[i]], v_ref[krows[i], cols[i]]) for i in n_jobs]
            if slots is None:
                lg = [_sb_logits(q_ref[qrows[i], cols[i]], k_ref[krows[i], cols[i]], scale,
                                 causal if diag[i] else None, jobs[i][4]) for i in n_jobs]
                log_beta = [x[1] for x in lg]
                sm = [_sums(lg[i][2], incl) for i in n_jobs]
                a_all = []
                for i in n_jobs:
                    p = p_scr[qrows[i], cols[i]]
                    a = jnp.exp(log_beta[i] + (tot_ref[qrows[i], cols[i]] - p - sm[i][0]))
                    a_all.append((jnp.where(causal, a, 0.0) if diag[i] else a).astype(BF16))
                    p_scr[qrows[i], cols[i]] = p + sm[i][1]
            else:
                a_all = [w_kept[hh, s] for hh, s in slots]
                log_beta = [lb_kept[hh, s].astype(F32) for hh, s in slots]
            gr = [d_a[i] * a_all[i].astype(F32) for i in n_jobs]
            dv_part = [_dot_tn(a_all[i], dob_scr[qrows[i], cols[i]]) for i in n_jobs]
            gs = [_sums(gr[i], excl, pieces=1) for i in n_jobs]
            dz_all = []
            for i in n_jobs:
                e = e_scr[qrows[i], cols[i]]
                dz = gr[i] - (gr[i] + e + gs[i][0]) * jnp.exp(log_beta[i])
                if diag[i]:
                    dz = jnp.where(causal, dz, 0.0)
                dz_all.append((dz * scale).astype(BF16))
                e_scr[qrows[i], cols[i]] = e + gs[i][1]
            dk_part = [_dot_tn(dz_all[i], q_ref[qrows[i], cols[i]]) for i in n_jobs]
            dq_part = [_dot(dz_all[i], k_ref[krows[i], cols[i]]) for i in n_jobs]
            for i in n_jobs:
                dv_acc[krows[i], cols[i]] += dv_part[i]
                dk_acc[krows[i], cols[i]] += dk_part[i]
                dq_acc[qrows[i], cols[i]] += dq_part[i]

        def pad_row_of(m):
            return jnp.where(jnp.logical_or(m > 0, lane >= N_PAD), 1.0, 0.0)

        base = SB_GROUP * g
        heads = range(SB_HEADS_BWD)
        mine = [(hh, r) for hh in heads for r in range(SB_GROUP)]
        stops = [jnp.max(stop_ref[hh, pl.ds(r, 1), :]).astype(jnp.int32) for hh, r in mine]
        for (hh, r), stop_now in zip(mine, stops):
            lowest = jnp.maximum(base - (SB_GROUP - 1 - r), 0)

            def step(m, hh=hh, r=r):
                tiles([(hh, r, m, False, pad_row_of(m))])
                return m + 1

            lax.while_loop(lambda m, lowest=lowest: m < lowest, step, jnp.clip(stop_now, 0, lowest))
        slot_of = {tile: i for i, tile in enumerate(SB_STATIC_OWN + SB_STATIC_NEAR)}
        upwards = lambda tile: (tile[1], tile[0])
        own = [(hh,) + tile for hh in heads for tile in sorted(SB_STATIC_OWN, key=upwards)]
        near = [(hh,) + tile for hh in heads for tile in sorted(SB_STATIC_NEAR, key=upwards)]

        def static_batch(order):
            tiles([(hh, r, base + rel, r == rel, None) for hh, r, rel in order],
                  slots=[(hh, slot_of[(r, rel)]) for hh, r, rel in order])

        pl.when(g == 0)(lambda: static_batch(own))
        pl.when(g > 0)(lambda: static_batch(near + own))
        dq_ref[...] = dq_acc[...].astype(BF16)

        @pl.when(g == n_groups - 1)
        def _():
            dk_ref[...] = dk_acc[...].astype(BF16)
            dv_ref[...] = dv_acc[...].astype(BF16)

    n_pairs = n_heads // SB_HEADS_BWD
    wide = SB_HEADS_BWD * HEAD
    qblk = pl.BlockSpec((SB_ROWS, wide), lambda b, h, g: (b * n_groups + g, h))
    kblk = pl.BlockSpec((lp, wide), lambda b, h, g: (b, n_pairs + h))
    vblk = pl.BlockSpec((lp, wide), lambda b, h, g: (b, 2 * n_pairs + h))
    hblk = pl.BlockSpec((lp, wide), lambda b, h, g: (b, h))
    sblk = pl.BlockSpec((None, None, SB_HEADS_BWD, 8, 128), lambda b, h, g: (b, g, h, 0, 0))
    table = pl.BlockSpec((2 * BLOCK, 2 * BLOCK), lambda b, h, g: (0, 0))
    kept = pl.BlockSpec((None, None, SB_HEADS_BWD, N_STATIC, BLOCK, BLOCK), lambda b, h, g: (b, g, h, 0, 0, 0))
    _, incl, excl = _sb_tables()
    return _pcall(
        body, name=name, grid=(bsz, n_pairs, n_groups),
        in_specs=[qblk, kblk, vblk, qblk, qblk, sblk, table, table, kept, kept],
        out_specs=(qblk, hblk, hblk),
        out_shape=(jax.ShapeDtypeStruct((t, d), BF16),) * 3,
        scratch_shapes=[pltpu.VMEM((lp, wide), F32), pltpu.VMEM((lp, wide), F32)]
        + [pltpu.VMEM((SB_ROWS, wide), F32)] * 3 + [pltpu.VMEM((SB_ROWS, wide), BF16)],
        compiler_params=_params("parallel", "parallel", "arbitrary"),
    )(qkv, qkv, qkv, do, tot, stop, incl, excl, w_kept, lb_kept)


def _adamw(w, g, m, v):
    m = ADAM_B1 * m + (1.0 - ADAM_B1) * g
    v = ADAM_B2 * v + (1.0 - ADAM_B2) * (g * g)
    m_hat = m / (1.0 - ADAM_B1 ** ADAM_STEP)
    v_hat = v / (1.0 - ADAM_B2 ** ADAM_STEP)
    delta = -ADAM_LR * (m_hat / (jnp.sqrt(v_hat) + ADAM_EPS) + ADAM_WD * w)
    return delta, m, v


def _update_sharded(w, parts, m, v, name):
    r, c = w.shape
    tr = UPDATE_ROWS if r % UPDATE_ROWS == 0 else r

    def body(w_ref, p_ref, m_ref, v_ref, g_ref, d_ref, nm_ref, nv_ref):
        g = p_ref[0].astype(F32)
        for q in range(1, N_DEV):
            g = g + p_ref[q].astype(F32)
        g_ref[...] = g
        d_ref[...], nm_ref[...], nv_ref[...] = _adamw(w_ref[...], g, m_ref[...], v_ref[...])

    row = pl.BlockSpec((tr, c), lambda i: (i, 0))
    return _pcall(
        body, name=name, grid=(r // tr,),
        in_specs=[row, pl.BlockSpec((N_DEV, tr, c), lambda i: (0, i, 0)), row, row],
        out_specs=(row,) * 4, out_shape=(jax.ShapeDtypeStruct((r, c), F32),) * 4,
        compiler_params=_params("parallel"),
    )(w, parts, m, v)


SMALL_ROWS = 8


def _pack_small(dpre0, dpre1, dpost0, dpost1, dlb, dwon, loss, name):
    d = dpre0.shape[1]
    bsz = dlb.shape[0]

    def body(a0, a1, p0, p1, lb_ref, on_ref, loss_ref, out_ref):
        out_ref[...] = jnp.zeros_like(out_ref)
        out_ref[pl.ds(0, 1), :] = a0[...]
        out_ref[pl.ds(1, 1), :] = a1[...]
        out_ref[pl.ds(2, 1), :] = p0[...]
        out_ref[pl.ds(3, 1), :] = p1[...]
        acc = lb_ref[0]
        for b in range(1, bsz):
            acc = acc + lb_ref[b]
        out_ref[pl.ds(4, 1), :] = acc
        out_ref[pl.ds(5, 1), pl.ds(0, HEAD)] = on_ref[...]
        out_ref[pl.ds(6, 1), pl.ds(0, HEAD)] = loss_ref[pl.ds(0, 1), :]

    return _pcall(body, name=name, out_shape=jax.ShapeDtypeStruct((SMALL_ROWS, d), F32))(
        dpre0, dpre1, dpost0, dpost1, dlb, dwon, loss)


def _update_small(parts, pre, post, lbw, on, moments, name):
    d = pre.shape[1]

    def body(p_ref, pre_ref, post_ref, lbw_ref, on_ref, mpre, mpost, mlb, mon, vpre, vpost, vlb, von,
             loss_ref, *outs):
        def total(r0, nr, width):
            acc = p_ref[0, pl.ds(r0, nr), pl.ds(0, width)]
            for q in range(1, N_DEV):
                acc = acc + p_ref[q, pl.ds(r0, nr), pl.ds(0, width)]
            return acc

        def put(k, w, g, m, v):
            dl, nm, nv = _adamw(w, g, m, v)
            outs[4 * k][...] = g
            outs[4 * k + 1][...] = dl
            outs[4 * k + 2][...] = nm
            outs[4 * k + 3][...] = nv

        put(0, pre_ref[...], total(0, 2, d), mpre[...], vpre[...])
        put(1, post_ref[...], total(2, 2, d), mpost[...], vpost[...])
        a0, a1 = lbw_ref[pl.ds(0, 1), :], lbw_ref[pl.ds(1, 1), :]
        mx = jnp.maximum(a0, a1)
        e0, e1 = jnp.exp(a0 - mx), jnp.exp(a1 - mx)
        p0 = e0 / (e0 + e1)
        g0 = total(4, 1, d) * p0 * (1.0 - p0)
        for r, w, g in ((0, a0, g0), (1, a1, -g0)):
            row = pl.ds(r, 1)
            dl, nm, nv = _adamw(w, g, mlb[row, :], vlb[row, :])
            outs[8][row, :] = g
            outs[9][row, :] = dl
            outs[10][row, :] = nm
            outs[11][row, :] = nv
        put(3, on_ref[...], total(5, 1, HEAD), mon[...], von[...])
        loss_ref[...] = jnp.broadcast_to(total(6, 1, HEAD), loss_ref.shape)

    shapes = []
    for w in (pre, post, lbw, on):
        shapes += [jax.ShapeDtypeStruct(w.shape, F32)] * 4
    return _pcall(body, name=name, out_shape=(jax.ShapeDtypeStruct((8, 128), F32), *shapes))(
        parts, pre, post, lbw, on, *moments)


def kernel(x, meta_tokens, pre_norm, post_norm, hgrn_w_in, hgrn_lb, hgrn_out_norm, hgrn_w_out, sb_w_in, sb_w_out, loss_target, m_meta_tokens, m_pre_norm, m_post_norm, m_hgrn_w_in, m_hgrn_lb, m_hgrn_out_norm, m_hgrn_w_out, m_sb_w_in, m_sb_w_out, v_meta_tokens, v_pre_norm, v_post_norm, v_hgrn_w_in, v_hgrn_lb, v_hgrn_out_norm, v_hgrn_w_out, v_sb_w_in, v_sb_w_out):
    bsz, seq, d = x.shape
    n_blocks = seq // BLOCK + 1
    lp = n_blocks * BLOCK
    s = hgrn_w_in.shape[2]
    dsh = d // N_DEV

    w_in_h, meta_all = _gather_once_per_chip([hgrn_w_in[0].astype(BF16), meta_tokens], "gather_weights")
    meta_full = jnp.transpose(meta_all, (1, 0, 2)).reshape(N_META, d)

    lbrow = jnp.cumsum(jax.nn.softmax(hgrn_lb, axis=0), axis=0)[0:1]

    main0, gate0, yn0, h0 = _norm_inproj(None, pre_norm[0:1], w_in_h, F32, "inproj_hgrn", from_x=(x, meta_full))
    o0, states, decay0, k0, a_diag0, w_in_s, w_out_s, w_out_h = _hgrn_fwd(
        main0, lbrow, bsz, n_blocks,
        [sb_w_in[0].astype(BF16), sb_w_out[0].astype(BF16), hgrn_w_out[0].astype(BF16)], ["gather"] * 3, "hgrn_fwd")
    w_out_s = w_out_s.reshape(d, d)
    w_out_h = w_out_h.reshape(d, d)
    h1, y0, main1, gate1, yn1 = _mix_out(o0, gate0, h0, w_out_h, hgrn_out_norm, post_norm[0:1], True,
                                         "mix_out_hgrn_inproj_sb", next_proj=(pre_norm[1:2], w_in_s, BF16))
    o1, tot, stop, w_kept, lb_kept = _sb_fwd(main1, bsz, n_blocks, "sb_fwd")
    dh2, loss_part, do1, dgate1, dw_out_s, dpost1, _ = _mix_out(
        o1, gate1, h1, w_out_s, hgrn_out_norm, post_norm[1:2], False, "mix_out_sb_loss_bwd", target=loss_target)

    dq1, dk1, dv1 = _sb_bwd(main1, do1, tot, stop, w_kept, lb_kept, bsz, n_blocks, "sb_bwd")
    dproj1 = (dq1, dk1, dv1, dgate1)
    dh1, dpre1 = _inproj_bwd_x(dproj1, w_in_s, h1, pre_norm[1:2], dh2, [], [], "inproj_sb_bwd_x")
    dw_in_s = _inproj_bwd_w(yn1, dproj1, s, "inproj_sb_bwd_w")

    do0, dgate0, dw_out_h, dpost0, dwon = _mix_out_bwd(
        dh1, y0, o0, gate0, w_out_h, hgrn_out_norm, post_norm[0:1], True, "mix_out_hgrn_bwd")
    dq0, dfz0, dv0, dlb, p_in_s, p_out_s, p_out_h = _hgrn_bwd(
        main0, decay0, k0, a_diag0, lbrow, states, do0, bsz, n_blocks,
        [dw_in_s, dw_out_s.reshape(N_DEV, dsh, d), dw_out_h.reshape(N_DEV, dsh, d)], ["scatter"] * 3, "hgrn_bwd")
    dproj0 = (dq0, dfz0, dv0, dgate0)
    dw_in_h = _inproj_bwd_w(yn0, dproj0, s, "inproj_hgrn_bwd_w")
    dh0, dpre0, p_in_h = _inproj_bwd_x(
        dproj0, w_in_h, h0, pre_norm[0:1], dh1, [dw_in_h], ["scatter"], "inproj_hgrn_bwd_x")

    dh0 = dh0.reshape(bsz, lp, d)
    grad_x = dh0[:, BLOCK:]
    dmeta = jnp.sum(dh0[:, N_PAD:BLOCK], axis=0)
    dmeta = jnp.transpose(dmeta.reshape(N_META, N_DEV, dsh), (1, 0, 2))
    small = _pack_small(dpre0, dpre1, dpost0, dpost1, dlb, dwon, loss_part, "pack_small")

    p_meta, p_small = _exchange([dmeta, small], ["scatter", "gather"], "exchange_grads")

    u_meta = _update_sharded(meta_tokens, p_meta, m_meta_tokens, v_meta_tokens, "update_meta")
    u_in_h = _update_sharded(hgrn_w_in[0], p_in_h, m_hgrn_w_in[0], v_hgrn_w_in[0], "update_hgrn_w_in")
    u_out_h = _update_sharded(hgrn_w_out[0], p_out_h, m_hgrn_w_out[0], v_hgrn_w_out[0], "update_hgrn_w_out")
    u_in_s = _update_sharded(sb_w_in[0], p_in_s, m_sb_w_in[0], v_sb_w_in[0], "update_sb_w_in")
    u_out_s = _update_sharded(sb_w_out[0], p_out_s, m_sb_w_out[0], v_sb_w_out[0], "update_sb_w_out")
    sm = _update_small(p_small, pre_norm, post_norm, hgrn_lb, hgrn_out_norm,
                       (m_pre_norm, m_post_norm, m_hgrn_lb, m_hgrn_out_norm,
                        v_pre_norm, v_post_norm, v_hgrn_lb, v_hgrn_out_norm), "update_small")
    loss = sm[0][0, 0]
    u_pre, u_post, u_lb, u_on = sm[1:5], sm[5:9], sm[9:13], sm[13:17]

    per_w = [u_meta, u_pre, u_post, tuple(a[None] for a in u_in_h), u_lb, u_on,
             tuple(a[None] for a in u_out_h), tuple(a[None] for a in u_in_s), tuple(a[None] for a in u_out_s)]
    outs = [loss, grad_x]
    for k in range(4):
        outs += [u[k] for u in per_w]
    return tuple(outs)
```

```python
import jax
import jax.numpy as jnp
from jax import lax
from jax.experimental import pallas as pl
from jax.experimental.pallas import tpu as pltpu

F32 = jnp.float32
BF16 = jnp.bfloat16

N_DEV = 8
BLOCK = 128
N_META = 16
N_PAD = BLOCK - N_META
HEAD = 128
SUB = 16
N_SUB = BLOCK // SUB
HALF = 8
N_HALF = SUB // HALF
EPS = 1e-6
ROW_TILE = 3 * BLOCK
UPDATE_ROWS = 256
K_TILES = (2816, 1408, 768, 384, 128)
NEG_BIG = -1e30

ADAM_LR = 0.001
ADAM_B1 = 0.9
ADAM_B2 = 0.999
ADAM_EPS = 1e-08
ADAM_WD = 0.01
ADAM_STEP = 10

VMEM_LIMIT = 56 * 1024 * 1024


def _pcall(body, **kw):
    return pl.pallas_call(body, **kw)


def _params(*sem):
    return pltpu.CompilerParams(dimension_semantics=sem, vmem_limit_bytes=VMEM_LIMIT)


def _dot(a, b):
    return jnp.dot(a, b, preferred_element_type=F32)


def _dot_nt(a, b):
    return lax.dot_general(a, b, (((1,), (1,)), ((), ())), preferred_element_type=F32)


def _dot_tn(a, b):
    return lax.dot_general(a, b, (((0,), (0,)), ((), ())), preferred_element_type=F32)


def _split(x, pieces):
    out = []
    for _ in range(pieces):
        p = x.astype(BF16)
        out.append(p)
        x = x - p.astype(F32)
    return out


def _tri_left(tri, x, pieces=3):
    return sum(_dot(tri, p) for p in _split(x, pieces))


def _iota2(shape, dim):
    return lax.broadcasted_iota(jnp.int32, shape, dim)


def _tri(cond):
    return jnp.where(cond, 1.0, 0.0).astype(BF16)


def _sig_pair(x):
    e = jnp.exp(-jnp.abs(x))
    r = 1.0 / (1.0 + e)
    er = e * r
    pos = x >= 0
    return jnp.where(pos, r, er), jnp.where(pos, er, r)


def _expand(rows):
    return jnp.concatenate([jnp.broadcast_to(r, (SUB, HEAD)) for r in rows], axis=0)


def _exchange_shapes(arrays, modes):
    return tuple(jax.ShapeDtypeStruct((N_DEV,) + tuple(a.shape[1:] if m == "scatter" else a.shape), a.dtype)
                 for a, m in zip(arrays, modes))


def _exchange_sems(n):
    if n == 0:
        return []
    return [pltpu.SemaphoreType.DMA((n, N_DEV - 1)), pltpu.SemaphoreType.DMA((n, N_DEV - 1)),
            pltpu.SemaphoreType.DMA((n,))]


def _exchange_copies(ins, outs, modes, send_sems, recv_sems, local_sems):
    mx, my, mc = lax.axis_index("x"), lax.axis_index("y"), lax.axis_index("c")
    me = 4 * mx + 2 * my + mc

    def src(i, slot):
        return ins[i].at[slot] if modes[i] == "scatter" else ins[i]

    def peer_of(mask):
        px = 1 - mx if mask & 4 else mx
        py = 1 - my if mask & 2 else my
        pc = 1 - mc if mask & 1 else mc
        return px, py, pc

    def copy(i, mask, dst_slot):
        px, py, pc = peer_of(mask)
        return pltpu.make_async_remote_copy(
            src_ref=src(i, 4 * px + 2 * py + pc), dst_ref=outs[i].at[dst_slot],
            send_sem=send_sems.at[i, mask - 1], recv_sem=recv_sems.at[i, mask - 1],
            device_id=(px, py, pc), device_id_type=pl.DeviceIdType.MESH)

    n = len(ins)
    sends = [copy(i, mask, me) for mask in range(1, N_DEV) for i in range(n)]
    own = [pltpu.make_async_copy(src(i, me), outs[i].at[me], local_sems.at[i]) for i in range(n)]
    arrivals = []
    for mask in range(1, N_DEV):
        px, py, pc = peer_of(mask)
        arrivals += [copy(i, mask, 4 * px + 2 * py + pc) for i in range(n)]

    def start():
        for cp in sends + own:
            cp.start()

    def wait():
        for cp in arrivals:
            cp.wait_recv()
        for cp in sends:
            cp.wait_send()
        for cp in own:
            cp.wait()

    return start, wait


def _gather_once_per_chip(arrays, name):
    n = len(arrays)

    def body(*refs):
        ins, outs = refs[:n], refs[n:2 * n]
        send_sems, recv_sems, local_sems = refs[2 * n:]
        mx, my, mc = lax.axis_index("x"), lax.axis_index("y"), lax.axis_index("c")
        me, sibling = (mx, my, mc), (mx, my, 1 - mc)
        chips = [(1 - mx, my), (mx, 1 - my), (1 - mx, 1 - my)]

        def slot(px, py, pc):
            return 4 * px + 2 * py + pc

        def copy(i, k, block, to, src=None):
            return pltpu.make_async_remote_copy(
                src_ref=outs[i].at[slot(*block)] if src is None else src, dst_ref=outs[i].at[slot(*block)],
                send_sem=send_sems.at[i, k], recv_sem=recv_sems.at[i, k],
                device_id=to, device_id_type=pl.DeviceIdType.MESH)

        own = [pltpu.make_async_copy(ins[i], outs[i].at[slot(*me)], local_sems.at[i]) for i in range(n)]
        first = [copy(i, 0, me, sibling, src=ins[i]) for i in range(n)]
        first += [copy(i, 1 + j, me, (*chip, mc), src=ins[i]) for j, chip in enumerate(chips) for i in range(n)]
        for cp in own + first:
            cp.start()
        passed = []
        for j, chip in enumerate(chips):
            for i in range(n):
                copy(i, 1 + j, (*chip, mc), me).wait_recv()
                cp = copy(i, 4 + j, (*chip, mc), sibling)
                cp.start()
                passed.append(cp)
        for i in range(n):
            copy(i, 0, sibling, me).wait_recv()
            for j, chip in enumerate(chips):
                copy(i, 4 + j, (*chip, 1 - mc), me).wait_recv()
        for cp in first + passed:
            cp.wait_send()
        for cp in own:
            cp.wait()

    any_spec = pl.BlockSpec(memory_space=pl.ANY)
    return _pcall(
        body, name=name, out_shape=_exchange_shapes(arrays, ["gather"] * n),
        in_specs=[any_spec] * n, out_specs=tuple([any_spec] * n),
        scratch_shapes=_exchange_sems(n),
    )(*arrays)


def _exchange(arrays, modes, name):
    n = len(arrays)

    def body(*refs):
        start, wait = _exchange_copies(refs[:n], refs[n:2 * n], modes, *refs[2 * n:])
        start()
        wait()

    any_spec = pl.BlockSpec(memory_space=pl.ANY)
    return _pcall(
        body, name=name, out_shape=_exchange_shapes(arrays, modes),
        in_specs=[any_spec] * n, out_specs=tuple([any_spec] * n),
        scratch_shapes=_exchange_sems(n),
    )(*arrays)


def _tile_blocks_of_x(seq, d):
    assert ROW_TILE == 3 * BLOCK and (seq + BLOCK) % ROW_TILE == 0
    per_seq = (seq + BLOCK) // ROW_TILE

    def spec(k):
        return pl.BlockSpec((None, BLOCK, d),
                            lambda i: (i // per_seq, jnp.maximum(3 * (i % per_seq) - 1 + k, 0), 0))

    return [spec(0), spec(1), spec(2)]


def _project(x, wn_ref, w_ref, main_ref, gate_ref, ynt_ref):
    p_n, _, s = w_ref.shape
    n_main = main_ref.shape[1] // s
    y = x * lax.rsqrt(jnp.mean(x * x, axis=-1, keepdims=True) + EPS) * wn_ref[...]
    yb = y.astype(BF16)
    ynt_ref[...] = y.T.astype(BF16)
    for p in range(p_n):
        r = _dot(yb, w_ref[p])
        if p < n_main:
            main_ref[:, p * s:(p + 1) * s] = r.astype(main_ref.dtype)
        else:
            gate_ref[:, (p - n_main) * s:(p - n_main + 1) * s] = r


def _projection_specs(t, d, w_all, main_dtype, tm):
    p_n, _, s = w_all.shape
    row = pl.BlockSpec((tm, d), lambda i: (i, 0))
    return ([pl.BlockSpec((1, d), lambda i: (0, 0)), pl.BlockSpec((p_n, d, s), lambda i: (0, 0, 0))],
            (pl.BlockSpec((tm, 3 * d), lambda i: (i, 0)), row, pl.BlockSpec((d, tm), lambda i: (0, i))),
            (jax.ShapeDtypeStruct((t, 3 * d), main_dtype), jax.ShapeDtypeStruct((t, d), F32),
             jax.ShapeDtypeStruct((d, t), BF16)))


def _norm_inproj(h, wnorm, w_all, main_dtype, name, from_x=None):
    p_n, d, s = w_all.shape
    n_main = 3 * d // s
    tm = ROW_TILE
    if from_x is None:
        t = h.shape[0]
        lead, lead_specs = [h], [pl.BlockSpec((tm, d), lambda i: (i, 0))]
    else:
        x_in, meta = from_x
        t = x_in.shape[0] * (x_in.shape[1] + BLOCK)
        tiles_per_seq = (x_in.shape[1] + BLOCK) // tm
        lead = [x_in, x_in, x_in, meta]
        lead_specs = _tile_blocks_of_x(x_in.shape[1], d) + [pl.BlockSpec((N_META, d), lambda i: (0, 0))]
    n_lead = len(lead)

    def body(*refs):
        wn_ref, w_ref = refs[n_lead:n_lead + 2]
        main_ref, gate_ref, ynt_ref = refs[n_lead + 2:n_lead + 5]
        if from_x is None:
            x = refs[0][...]
        else:
            first_tile = pl.program_id(0) % tiles_per_seq == 0
            meta_block = jnp.concatenate([jnp.zeros((N_PAD, d), F32), refs[3][...]], axis=0)
            x = jnp.concatenate([jnp.where(first_tile, meta_block, refs[0][...]), refs[1][...], refs[2][...]], axis=0)
            refs[n_lead + 5][...] = x
        _project(x, wn_ref, w_ref, main_ref, gate_ref, ynt_ref)

    row = pl.BlockSpec((tm, d), lambda i: (i, 0))
    extra_specs, extra_shapes = ((), ()) if from_x is None else ((row,), (jax.ShapeDtypeStruct((t, d), F32),))
    proj_in, proj_out, proj_shapes = _projection_specs(t, d, w_all, main_dtype, tm)
    return _pcall(
        body, name=name, grid=(t // tm,),
        in_specs=lead_specs + proj_in, out_specs=proj_out + extra_specs, out_shape=proj_shapes + extra_shapes,
        compiler_params=_params("parallel"),
    )(*lead, wnorm, w_all)


def _mix_out(o, gate, h_in, w_out, w_on, w_post, head_norm, name, target=None, next_proj=None):
    t, d = o.shape
    n_heads = d // HEAD
    tm = ROW_TILE
    with_loss = target is not None
    if with_loss:
        tiles_per_seq = (target.shape[1] + BLOCK) // tm

    def body(o_ref, g_ref, h_ref, w_ref, won_ref, wp_ref, *rest):
        if with_loss:
            t0_ref, t1_ref, t2_ref, hout_ref, loss_ref = rest[:5]
            u_scr = rest[10]
        elif next_proj is not None:
            wn2_ref, w2_ref, hout_ref, y_ref, main_ref, gate_ref, ynt_ref, u_scr = rest
        else:
            hout_ref, y_ref, u_scr = rest
        for hh in range(n_heads):
            cs = slice(hh * HEAD, (hh + 1) * HEAD)
            oh = o_ref[:, cs]
            gt = g_ref[:, cs]
            if head_norm:
                oh = oh * lax.rsqrt(jnp.mean(oh * oh, axis=-1, keepdims=True) + EPS) * won_ref[...]
            u_scr[:, cs] = (oh * (gt * jax.nn.sigmoid(gt))).astype(BF16)
        y = _dot(u_scr[...], w_ref[...])
        r = y * lax.rsqrt(jnp.mean(y * y, axis=-1, keepdims=True) + EPS) * wp_ref[...]
        h_out = h_ref[...] + r
        if not with_loss:
            y_ref[...] = y
            hout_ref[...] = h_out
            if next_proj is not None:
                _project(h_out, wn2_ref, w2_ref, main_ref, gate_ref, ynt_ref)
            return
        i = pl.program_id(0)

        @pl.when(i == 0)
        def _():
            loss_ref[...] = jnp.zeros_like(loss_ref)

        tgt = jnp.concatenate([t0_ref[...], t1_ref[...], t2_ref[...]], axis=0)
        real = jnp.logical_or(i % tiles_per_seq > 0, _iota2((tm, d), 0) >= BLOCK)
        err = jnp.where(real, h_out - tgt, 0.0)
        dh = err * (1.0 / d)
        hout_ref[...] = dh
        part = jnp.sum(jnp.sum(err * err, axis=-1, keepdims=True), axis=0, keepdims=True)
        loss_ref[...] += part * (0.5 / d)
        _mix_out_bwd_tile(dh, y, o_ref, g_ref, w_ref, won_ref, wp_ref, *rest[5:], head_norm, t // tm - 1,
                          u_ready=True)

    row = pl.BlockSpec((tm, d), lambda i: (i, 0))
    in_specs = [row, row, row, pl.BlockSpec((d, d), lambda i: (0, 0)),
                pl.BlockSpec((1, HEAD), lambda i: (0, 0)), pl.BlockSpec((1, d), lambda i: (0, 0))]
    out_specs, out_shape = (row, row), (jax.ShapeDtypeStruct((t, d), F32), jax.ShapeDtypeStruct((t, d), F32))
    scratch = [pltpu.VMEM((tm, d), BF16)]
    args = (o, gate, h_in, w_out, w_on, w_post)
    if with_loss:
        in_specs += _tile_blocks_of_x(target.shape[1], d)
        bwd_specs, bwd_shapes, scratch = _mix_out_bwd_outputs(t, d, tm)
        out_specs = (row, pl.BlockSpec((8, 128), lambda i: (0, 0))) + bwd_specs
        out_shape = (jax.ShapeDtypeStruct((t, d), F32), jax.ShapeDtypeStruct((8, 128), F32)) + bwd_shapes
        args += (target, target, target)
    elif next_proj is not None:
        proj_in, proj_out, proj_shapes = _projection_specs(t, d, next_proj[1], next_proj[2], tm)
        in_specs += proj_in
        out_specs += proj_out
        out_shape += proj_shapes
        args += (next_proj[0], next_proj[1])
    return _pcall(
        body, name=name, grid=(t // tm,), in_specs=in_specs, out_specs=out_specs, out_shape=out_shape,
        scratch_shapes=scratch,
        compiler_params=_params("arbitrary" if with_loss else "parallel"),
    )(*args)


def _mix_out_bwd_tile(dr, yv, o_ref, g_ref, w_ref, won_ref, wp_ref,
                      do_ref, dg_ref, dw_ref, dwp_ref, dwon_ref, u_scr, acc, head_norm, last, u_ready=False):
    i = pl.program_id(0)
    n_heads = o_ref.shape[1] // HEAD

    @pl.when(i == 0)
    def _():
        acc[...] = jnp.zeros_like(acc)
        dwp_ref[...] = jnp.zeros_like(dwp_ref)
        dwon_ref[...] = jnp.zeros_like(dwon_ref)

    rs = lax.rsqrt(jnp.mean(yv * yv, axis=-1, keepdims=True) + EPS)
    yh = yv * rs
    dwp_ref[...] += jnp.sum(dr * yh, axis=0, keepdims=True)
    wd = dr * wp_ref[...]
    dy = rs * (wd - yh * jnp.mean(wd * yh, axis=-1, keepdims=True))
    dyb = dy.astype(BF16)
    du = _dot_nt(dyb, w_ref[...])
    for hh in range(n_heads):
        cs = slice(hh * HEAD, (hh + 1) * HEAD)
        oh = o_ref[:, cs]
        gt = g_ref[:, cs]
        sg = jax.nn.sigmoid(gt)
        sl = gt * sg
        duh = du[:, cs]
        if head_norm:
            rsh = lax.rsqrt(jnp.mean(oh * oh, axis=-1, keepdims=True) + EPS)
            ohat = oh * rsh
            on = ohat * won_ref[...]
        else:
            on = oh
        if not u_ready:
            u_scr[:, cs] = (on * sl).astype(BF16)
        don = duh * sl
        dg_ref[:, cs] = (duh * on * (sg * (1.0 + gt * (1.0 - sg)))).astype(BF16)
        if head_norm:
            dwon_ref[...] += jnp.sum(don * ohat, axis=0, keepdims=True)
            wdn = don * won_ref[...]
            do_ref[:, cs] = rsh * (wdn - ohat * jnp.mean(wdn * ohat, axis=-1, keepdims=True))
        else:
            do_ref[:, cs] = don
    acc[...] += _dot_tn(u_scr[...], dyb)

    @pl.when(i == last)
    def _():
        dw_ref[...] = acc[...].astype(BF16)


def _mix_out_bwd_outputs(t, d, tm):
    row = pl.BlockSpec((tm, d), lambda i: (i, 0))
    specs = (row, row, pl.BlockSpec((d, d), lambda i: (0, 0)), pl.BlockSpec((1, d), lambda i: (0, 0)),
             pl.BlockSpec((1, HEAD), lambda i: (0, 0)))
    shapes = (jax.ShapeDtypeStruct((t, d), F32), jax.ShapeDtypeStruct((t, d), BF16),
              jax.ShapeDtypeStruct((d, d), BF16), jax.ShapeDtypeStruct((1, d), F32),
              jax.ShapeDtypeStruct((1, HEAD), F32))
    return specs, shapes, [pltpu.VMEM((tm, d), BF16), pltpu.VMEM((d, d), F32)]


def _mix_out_bwd(dh, y, o, gate, w_out, w_on, w_post, head_norm, name):
    t, d = o.shape
    tm = ROW_TILE
    last = t // tm - 1

    def body(dh_ref, y_ref, *refs):
        _mix_out_bwd_tile(dh_ref[...], y_ref[...], *refs, head_norm, last)

    row = pl.BlockSpec((tm, d), lambda i: (i, 0))
    out_specs, out_shape, scratch = _mix_out_bwd_outputs(t, d, tm)
    return _pcall(
        body, name=name, grid=(t // tm,),
        in_specs=[row, row, row, row, pl.BlockSpec((d, d), lambda i: (0, 0)),
                  pl.BlockSpec((1, HEAD), lambda i: (0, 0)), pl.BlockSpec((1, d), lambda i: (0, 0))],
        out_specs=out_specs, out_shape=out_shape, scratch_shapes=scratch,
        compiler_params=_params("arbitrary"),
    )(dh, y, o, gate, w_out, w_on, w_post)


def _inproj_bwd_x(dparts, w_all, h, wnorm, dres, ride, ride_modes, name):
    t, d = h.shape
    p_n, _, s = w_all.shape
    per = d // s
    tm = ROW_TILE
    nr = len(ride)
    grid = (t // tm,)

    def body(*refs):
        d0_ref, d1_ref, d2_ref, d3_ref, w_ref, h_ref, wn_ref, dres_ref = refs[:8]
        dh_ref, dwn_ref = refs[8 + nr:10 + nr]
        finish_ride = _ride_along(refs[8:8 + nr], refs[10 + nr:10 + 2 * nr], ride_modes, refs[10 + 2 * nr:], grid)
        i = pl.program_id(0)

        @pl.when(i == 0)
        def _():
            dwn_ref[...] = jnp.zeros_like(dwn_ref)

        pieces = (d0_ref, d1_ref, d2_ref, d3_ref)
        dyn = jnp.zeros((tm, d), F32)
        for p in range(p_n):
            blk = pieces[p // per][:, (p % per) * s:(p % per + 1) * s]
            dyn = dyn + _dot_nt(blk, w_ref[p])
        x = h_ref[...]
        rs = lax.rsqrt(jnp.mean(x * x, axis=-1, keepdims=True) + EPS)
        xh = x * rs
        dwn_ref[...] += jnp.sum(dyn * xh, axis=0, keepdims=True)
        wd = dyn * wn_ref[...]
        dh_ref[...] = dres_ref[...] + rs * (wd - xh * jnp.mean(wd * xh, axis=-1, keepdims=True))
        finish_ride()

    row = pl.BlockSpec((tm, d), lambda i: (i, 0))
    any_spec = pl.BlockSpec(memory_space=pl.ANY)
    return _pcall(
        body, name=name, grid=grid,
        in_specs=[row, row, row, row, pl.BlockSpec((p_n, d, s), lambda i: (0, 0, 0)),
                  row, pl.BlockSpec((1, d), lambda i: (0, 0)), row] + [any_spec] * nr,
        out_specs=(row, pl.BlockSpec((1, d), lambda i: (0, 0))) + (any_spec,) * nr,
        out_shape=(jax.ShapeDtypeStruct((t, d), F32), jax.ShapeDtypeStruct((1, d), F32))
        + _exchange_shapes(ride, ride_modes),
        scratch_shapes=_exchange_sems(nr),
        compiler_params=_params("arbitrary"),
    )(*dparts, w_all, h, wnorm, dres, *ride)


def _inproj_bwd_w(ynt, dparts, s, name):
    d, t = ynt.shape
    per = d // s
    n_sh = len(dparts) * per
    tk = next(c for c in K_TILES if t % c == 0)
    last = t // tk - 1

    def body(ynt_ref, d0_ref, d1_ref, d2_ref, d3_ref, dw_ref, acc):
        p, i = pl.program_id(0), pl.program_id(1)

        @pl.when(i == 0)
        def _():
            acc[...] = jnp.zeros_like(acc)

        for a, piece in enumerate((d0_ref, d1_ref, d2_ref, d3_ref)):
            @pl.when(p // per == a)
            def _():
                acc[...] += _dot(ynt_ref[...], piece[...])

        @pl.when(i == last)
        def _():
            dw_ref[...] = acc[...].astype(BF16)

    def piece_spec(a):
        return pl.BlockSpec((tk, s), lambda p, i: (jnp.where(p // per == a, i, 0),
                                                   jnp.where(p // per == a, p % per, 0)))

    return _pcall(
        body, name=name, grid=(n_sh, t // tk),
        in_specs=[pl.BlockSpec((d, tk), lambda p, i: (0, i))] + [piece_spec(a) for a in range(4)],
        out_specs=pl.BlockSpec((None, d, s), lambda p, i: (p, 0, 0)),
        out_shape=jax.ShapeDtypeStruct((n_sh, d, s), BF16),
        scratch_shapes=[pltpu.VMEM((d, s), F32)],
        compiler_params=_params("parallel", "arbitrary"),
    )(ynt, *dparts)


def _hgrn_gates(fz_ref, lb, b_ref, k_ref):
    sig, nsig = _sig_pair(fz_ref[...])
    f = lb + (1.0 - lb) * sig
    rr, cc = _iota2((BLOCK, BLOCK), 0), _iota2((BLOCK, BLOCK), 1)
    b_ref[...] = _tri_left(_tri(cc <= rr), jnp.log(f))
    k_ref[...] = (1.0 - lb) * nsig


def _hgrn_common(q_ref, b_ref, k_ref):
    b, k = b_ref[...], k_ref[...]
    bend = [b_ref[pl.ds(SUB * j + SUB - 1, 1), :] for j in range(N_SUB)]
    bref = [jnp.zeros((1, HEAD), F32)] + bend[:-1]
    refrow, bendrow = _expand(bref), _expand(bend)
    e_q = jnp.exp(b - refrow)
    e_k = jnp.exp(bendrow - b)
    qt = q_ref[...] * e_q
    kh = k * e_k
    bl = bend[-1]
    return dict(k=k, b=b, bend=bend, bref=bref, refrow=refrow, e_q=e_q, e_k=e_k, qt=qt, kh=kh, bl=bl)


HEADS_PER_STEP = 8


def _lockstep(chunks):
    live = list(chunks)
    while live:
        still = []
        for gen in live:
            try:
                next(gen)
                still.append(gen)
            except StopIteration:
                pass
        live = still


def _ride_along(ride_ins, ride_outs, modes, sems, grid):
    if not ride_ins:
        return lambda: None
    ids = [pl.program_id(a) for a in range(len(grid))]
    start, wait = _exchange_copies(ride_ins, ride_outs, modes, *sems)
    first, last = ids[0] == 0, ids[0] == grid[0] - 1
    for a in range(1, len(grid)):
        first = jnp.logical_and(first, ids[a] == 0)
        last = jnp.logical_and(last, ids[a] == grid[a] - 1)
    pl.when(first)(start)
    return lambda: pl.when(last)(wait)


def _hgrn_fwd(main, lbrow, bsz, n_blocks, ride, ride_modes, name):
    t, d3 = main.shape
    d = d3 // 3
    n_heads = d // HEAD
    n_pairs = n_heads // HEADS_PER_STEP
    wide = HEADS_PER_STEP * HEAD
    nr = len(ride)
    grid = (bsz, n_pairs, n_blocks)

    def chunk(q_ref, fz_ref, v_ref, lb, o_ref, st_ref, b_scr, k_scr, ad_ref, s_scr, o_acc, ad_scr):
        _hgrn_gates(fz_ref, lb, b_scr, k_scr)
        yield
        c = _hgrn_common(q_ref, b_scr, k_scr)
        yield
        s_t = s_scr[...]
        st_ref[...] = s_t
        vb = v_ref[...].astype(BF16)
        q_state = c["qt"] * _expand([jnp.exp(r) for r in c["bref"]])
        o_state = _dot_nt(q_state.astype(BF16), s_t.astype(BF16))
        js = range(N_SUB - 1)
        khb = c["kh"].astype(BF16)
        lhs = [(c["qt"][SUB * (j + 1):] * jnp.exp(c["refrow"][SUB * (j + 1):] - c["bend"][j])).astype(BF16)
               for j in js]
        yield
        a_js = [_dot_nt(lhs[j], khb[SUB * j:SUB * (j + 1)]) for j in js]
        k_state = c["kh"] * _expand([jnp.exp(c["bl"] - r) for r in c["bend"]])
        s_scr[...] = s_t * jnp.exp(c["bl"]) + _dot_tn(vb, k_state.astype(BF16))
        yield
        o_js = [_dot(a_js[j].astype(BF16), vb[SUB * j:SUB * (j + 1)]) for j in js]
        yield
        o_acc[...] = o_state
        for j in js:
            o_acc[SUB * (j + 1):, :] += o_js[j]
        not_before = [_iota2((HALF, HEAD), 0) >= row for row in range(HALF)]
        lane = _iota2((HALF, HEAD), 1)
        for i in range(N_SUB):
            r0 = SUB * i
            q_h = [q_ref[pl.ds(r0 + HALF * u, HALF), :] for u in range(N_HALF)]
            b_h = [b_scr[pl.ds(r0 + HALF * u, HALF), :] for u in range(N_HALF)]
            a_h = [jnp.zeros((HALF, HEAD), F32) for _ in range(N_HALF)]
            for s in range(SUB):
                brow = b_scr[pl.ds(r0 + s, 1), :]
                krow = k_scr[pl.ds(r0 + s, 1), :]
                is_lane = lane == r0 + s
                for u in range(s // HALF, N_HALF):
                    diff = b_h[u] - brow
                    if u == s // HALF:
                        diff = jnp.where(not_before[s - HALF * u], diff, NEG_BIG)
                    col = jnp.sum(q_h[u] * krow * jnp.exp(diff), axis=-1, keepdims=True)
                    a_h[u] = jnp.where(is_lane, col, a_h[u])
            for u in range(N_HALF):
                ad_scr[pl.ds(r0 + HALF * u, HALF), :] = a_h[u]
            yield
        a_diag = ad_scr[...].astype(BF16)
        ad_ref[...] = a_diag
        o_ref[...] = o_acc[...] + _dot(a_diag, vb)

    def body(*refs):
        q_ref, fz_ref, v_ref, lb_ref = refs[:4]
        o_ref, st_ref, b_out, k_out, ad_out = refs[4 + nr:9 + nr]
        s_scr, o_acc, ad_scr = refs[9 + 2 * nr:12 + 2 * nr]
        finish_ride = _ride_along(refs[4:4 + nr], refs[9 + nr:9 + 2 * nr], ride_modes, refs[12 + 2 * nr:], grid)

        @pl.when(pl.program_id(2) == 0)
        def _():
            s_scr[...] = jnp.zeros_like(s_scr)

        def head(hh):
            cols = pl.ds(hh * HEAD, HEAD)
            return chunk(q_ref.at[:, cols], fz_ref.at[:, cols], v_ref.at[:, cols], lb_ref[:, cols],
                         o_ref.at[:, cols], st_ref.at[hh], b_out.at[:, cols], k_out.at[:, cols],
                         ad_out.at[:, cols], s_scr.at[hh], o_acc.at[hh], ad_scr.at[hh])

        _lockstep([head(hh) for hh in range(HEADS_PER_STEP)])
        finish_ride()

    def blk(col0):
        return pl.BlockSpec((BLOCK, wide), lambda b, h, n: (b * n_blocks + n, col0 + h))

    any_spec = pl.BlockSpec(memory_space=pl.ANY)
    per_head = pltpu.VMEM((HEADS_PER_STEP, BLOCK, HEAD), F32)
    return _pcall(
        body, name=name, grid=grid,
        in_specs=[blk(0), blk(n_pairs), blk(2 * n_pairs), pl.BlockSpec((1, wide), lambda b, h, n: (0, h))]
        + [any_spec] * nr,
        out_specs=(blk(0), pl.BlockSpec((None, HEADS_PER_STEP, HEAD, HEAD), lambda b, h, n: (b * n_blocks + n, h, 0, 0)),
                   blk(0), blk(0), blk(0)) + (any_spec,) * nr,
        out_shape=(jax.ShapeDtypeStruct((t, d), F32),
                   jax.ShapeDtypeStruct((bsz * n_blocks, n_heads, HEAD, HEAD), F32),
                   jax.ShapeDtypeStruct((t, d), F32), jax.ShapeDtypeStruct((t, d), F32),
                   jax.ShapeDtypeStruct((t, d), BF16))
        + _exchange_shapes(ride, ride_modes),
        scratch_shapes=[per_head] * 3 + _exchange_sems(nr),
        compiler_params=_params("arbitrary", "arbitrary", "arbitrary"),
    )(main, main, main, lbrow, *ride)


def _hgrn_bwd(main, b_all, k_all, a_diag, lbrow, states, do, bsz, n_blocks, ride, ride_modes, name):
    t, d3 = main.shape
    d = d3 // 3
    n_pairs = d // HEAD // HEADS_PER_STEP
    wide = HEADS_PER_STEP * HEAD
    nr = len(ride)
    grid = (bsz, n_pairs, n_blocks)

    def chunk(n, q_ref, b_scr, k_scr, ad_ref, v_ref, lb, st_ref, do_ref, dq_ref, dfz_ref, dv_ref, dlb_ref,
              ds_scr, dqt_acc, dkh_acc, dv_acc, dqd_acc, dkd_acc):
        c = _hgrn_common(q_ref, b_scr, k_scr)
        yield
        q, k = q_ref[...], c["k"]
        vb = v_ref[...].astype(BF16)
        dob = do_ref[...].astype(BF16)
        s0_t = st_ref[...]
        ds1_t = ds_scr[...]
        e_ref = _expand([jnp.exp(r) for r in c["bref"]])
        e_end = _expand([jnp.exp(c["bl"] - r) for r in c["bend"]])
        e_bl = jnp.exp(c["bl"])
        q_state = c["qt"] * e_ref
        k_state = c["kh"] * e_end
        dq_state = _dot(dob, s0_t.astype(BF16))
        dk_state = _dot(vb, ds1_t.astype(BF16))
        dv_state = _dot_nt(k_state.astype(BF16), ds1_t.astype(BF16))
        ds_scr[...] = ds1_t * e_bl + _dot_tn(dob, q_state.astype(BF16))
        js = range(N_SUB - 1)
        lo = [slice(SUB * j, SUB * (j + 1)) for j in js]
        khb = c["kh"].astype(BF16)
        dj = [jnp.exp(c["refrow"][SUB * (j + 1):] - c["bend"][j]) for j in js]
        lhs = [(c["qt"][SUB * (j + 1):] * dj[j]).astype(BF16) for j in js]
        yield
        a_js = [_dot_nt(lhs[j], khb[lo[j]]) for j in js]
        da_js = [_dot_nt(dob[SUB * (j + 1):], vb[lo[j]]).astype(BF16) for j in js]
        dqt_acc[...] = dq_state * e_ref
        dkh_acc[...] = dk_state * e_end
        dv_acc[...] = dv_state
        dbl = (jnp.sum(s0_t * ds1_t, axis=0, keepdims=True) * e_bl
               + jnp.sum(k_state * dk_state, axis=0, keepdims=True))
        yield
        dv_js = [_dot_tn(a_js[j].astype(BF16), dob[SUB * (j + 1):]) for j in js]
        dq_js = [_dot(da_js[j], khb[lo[j]]) * dj[j] for j in js]
        dk_js = [_dot_tn(da_js[j], lhs[j]) for j in js]
        yield
        for j in js:
            dv_acc[lo[j], :] += dv_js[j]
            dqt_acc[SUB * (j + 1):, :] += dq_js[j]
            dkh_acc[lo[j], :] += dk_js[j]
        not_before = [_iota2((HALF, HEAD), 0) >= row for row in range(HALF)]
        is_row = [_iota2((HALF, HEAD), 0) == row for row in range(HALF)]
        for i in range(N_SUB):
            r0 = SUB * i
            q_h = [q_ref[pl.ds(r0 + HALF * u, HALF), :] for u in range(N_HALF)]
            b_h = [b_scr[pl.ds(r0 + HALF * u, HALF), :] for u in range(N_HALF)]
            do_h = [do_ref[pl.ds(r0 + HALF * u, HALF), :] for u in range(N_HALF)]
            zero = jnp.zeros((HALF, HEAD), F32)
            dq_h, dk_h = [zero] * N_HALF, [zero] * N_HALF
            for s in range(SUB):
                brow = b_scr[pl.ds(r0 + s, 1), :]
                krow = k_scr[pl.ds(r0 + s, 1), :]
                vrow = v_ref[pl.ds(r0 + s, 1), :]
                dk_row = jnp.zeros((1, HEAD), F32)
                for u in range(s // HALF, N_HALF):
                    diff = b_h[u] - brow
                    if u == s // HALF:
                        diff = jnp.where(not_before[s - HALF * u], diff, NEG_BIG)
                    w = jnp.exp(diff)
                    qw = q_h[u] * w
                    da_col = jnp.sum(do_h[u] * vrow, axis=-1, keepdims=True)
                    dq_h[u] = dq_h[u] + da_col * (w * krow)
                    dk_row = dk_row + jnp.sum(da_col * qw, axis=0, keepdims=True)
                us = s // HALF
                dk_h[us] = jnp.where(is_row[s - HALF * us], dk_row, dk_h[us])
            for u in range(N_HALF):
                rows = pl.ds(r0 + HALF * u, HALF)
                dqd_acc[rows, :] = dq_h[u]
                dkd_acc[rows, :] = dk_h[u]
            yield
        dv_in = _dot_tn(ad_ref[...], dob)
        dq = dqt_acc[...] * c["e_q"] + dqd_acc[...]
        dk = dkh_acc[...] * c["e_k"] + dkd_acc[...]
        rr, cc = _iota2((BLOCK, BLOCK), 0), _iota2((BLOCK, BLOCK), 1)
        db = q * dq - k * dk + jnp.where(_iota2((BLOCK, HEAD), 0) == BLOCK - 1, dbl, 0.0)
        yield
        dg = _tri_left(_tri(cc >= rr), db)
        yield
        real = jnp.logical_or(n > 0, _iota2((BLOCK, HEAD), 0) >= N_PAD)
        df = jnp.where(real, dg / (1.0 - k) - dk, 0.0)
        nsig = k * (1.0 / (1.0 - lb))
        dq_ref[...] = dq.astype(BF16)
        dv_ref[...] = (dv_acc[...] + dv_in).astype(BF16)
        dfz_ref[...] = (df * k * (1.0 - nsig)).astype(BF16)
        dlb_ref[...] += jnp.sum(df * nsig, axis=0, keepdims=True)

    def body(*refs):
        q_ref, b_ref, k_ref, ad_ref, v_ref, lb_ref, st_ref, do_ref = refs[:8]
        dq_ref, dfz_ref, dv_ref, dlb_ref = refs[8 + nr:12 + nr]
        scratch = refs[12 + 2 * nr:18 + 2 * nr]
        finish_ride = _ride_along(refs[8:8 + nr], refs[12 + nr:12 + 2 * nr], ride_modes, refs[18 + 2 * nr:], grid)
        step = pl.program_id(2)

        @pl.when(step == 0)
        def _():
            scratch[0][...] = jnp.zeros_like(scratch[0])
            dlb_ref[...] = jnp.zeros_like(dlb_ref)

        def head(hh):
            cols = pl.ds(hh * HEAD, HEAD)
            return chunk(n_blocks - 1 - step, q_ref.at[:, cols], b_ref.at[:, cols], k_ref.at[:, cols],
                         ad_ref.at[:, cols], v_ref.at[:, cols], lb_ref[:, cols], st_ref.at[hh], do_ref.at[:, cols],
                         dq_ref.at[:, cols],
                         dfz_ref.at[:, cols], dv_ref.at[:, cols], dlb_ref.at[:, cols],
                         *[scr.at[hh] for scr in scratch])

        _lockstep([head(hh) for hh in range(HEADS_PER_STEP)])
        finish_ride()

    def blk(col0):
        return pl.BlockSpec((BLOCK, wide), lambda b, h, s: (b * n_blocks + n_blocks - 1 - s, col0 + h))

    any_spec = pl.BlockSpec(memory_space=pl.ANY)
    per_head = pltpu.VMEM((HEADS_PER_STEP, BLOCK, HEAD), F32)
    return _pcall(
        body, name=name, grid=grid,
        in_specs=[blk(0), blk(0), blk(0), blk(0), blk(2 * n_pairs), pl.BlockSpec((1, wide), lambda b, h, s: (0, h)),
                  pl.BlockSpec((None, HEADS_PER_STEP, HEAD, HEAD),
                               lambda b, h, s: (b * n_blocks + n_blocks - 1 - s, h, 0, 0)),
                  blk(0)] + [any_spec] * nr,
        out_specs=(blk(0), blk(0), blk(0), pl.BlockSpec((None, 1, wide), lambda b, h, s: (b, 0, h)))
        + (any_spec,) * nr,
        out_shape=(jax.ShapeDtypeStruct((t, d), BF16),) * 3 + (jax.ShapeDtypeStruct((bsz, 1, d), F32),)
        + _exchange_shapes(ride, ride_modes),
        scratch_shapes=[per_head] * 6 + _exchange_sems(nr),
        compiler_params=_params("arbitrary", "arbitrary", "arbitrary"),
    )(main, b_all, k_all, a_diag, main, lbrow, states, do, *ride)


SB_GROUP = 3
SB_ROWS = SB_GROUP * BLOCK
SB_DEAD = -104.0
SB_NEAR = ((0, 0), (1, 0), (0, 1))
SB_HEADS_FWD = 4
SB_HEADS_BWD = 2
SB_STATIC_OWN = tuple((r, kb) for kb in range(SB_GROUP - 1, -1, -1) for r in range(kb, SB_GROUP))
SB_STATIC_NEAR = tuple((r, -1 - back) for r, back in SB_NEAR)
N_STATIC = len(SB_STATIC_OWN) + len(SB_STATIC_NEAR)


def _sb_tables():
    j = jnp.bitwise_and(_iota2((2 * BLOCK, 2 * BLOCK), 0), BLOCK - 1)
    s = _iota2((2 * BLOCK, 2 * BLOCK), 1)
    ones = s >= BLOCK
    return (_tri(jnp.logical_or(ones, j > s)), _tri(jnp.logical_or(ones, j <= s)),
            _tri(jnp.logical_or(ones, j < s)))


def _sums(x, table, pieces=2):
    hi = x.astype(BF16)
    if pieces == 1:
        r = _dot(hi, table[:BLOCK])
    else:
        lo = (x - hi.astype(F32)).astype(BF16)
        r = _dot(jnp.concatenate([hi, lo], axis=1), table)
    return r[:, :BLOCK], r[:, BLOCK:]


def _sb_logits(q, ks, scale, causal, pad_row):
    z = _dot_nt(q, ks) * scale
    log_keep = -(jnp.maximum(z, 0.0) + jnp.log(1.0 + jnp.exp(-jnp.abs(z))))
    log_beta = z + log_keep
    if causal is not None:
        log_keep = jnp.where(causal, log_keep, 0.0)
    if pad_row is not None:
        log_keep = log_keep * pad_row
    return z, log_beta, log_keep


def _sb_fwd(qkv, bsz, n_blocks, name):
    t, d3 = qkv.shape
    d = d3 // 3
    n_heads = d // HEAD
    lp = n_blocks * BLOCK
    n_groups = n_blocks // SB_GROUP
    assert n_groups * SB_GROUP == n_blocks
    scale = HEAD ** -0.5

    n_pairs = n_heads // SB_HEADS_FWD
    wide = SB_HEADS_FWD * HEAD

    def body(q_ref, k_ref, v_ref, upper_ref, o_ref, tot_ref, stop_ref, w_keep, lb_keep, c_scr):
        g = pl.program_id(2)
        upper = upper_ref[...]
        causal = _iota2((BLOCK, BLOCK), 1) < _iota2((BLOCK, BLOCK), 0)
        lane = _iota2((1, BLOCK), 1)
        o_ref[...] = jnp.zeros_like(o_ref)
        c_scr[...] = jnp.zeros_like(c_scr)

        def tiles(jobs, keep=None):
            cols = [pl.ds(hh * HEAD, HEAD) for hh, _, _, _, _ in jobs]
            qrows = [pl.ds(r * BLOCK, BLOCK) for _, r, _, _, _ in jobs]
            krows = [pl.ds(pl.multiple_of(m * BLOCK, BLOCK), BLOCK) for _, _, m, _, _ in jobs]
            n_jobs = range(len(jobs))
            lg = [_sb_logits(q_ref[qrows[i], cols[i]], k_ref[krows[i], cols[i]], scale,
                             causal if jobs[i][3] else None, jobs[i][4]) for i in n_jobs]
            sm = [_sums(x[2], upper) for x in lg]
            a_all = []
            for i in n_jobs:
                c = c_scr[qrows[i], cols[i]]
                a = jnp.exp(lg[i][1] + c + sm[i][0])
                a_all.append((jnp.where(causal, a, 0.0) if jobs[i][3] else a).astype(BF16))
                c_scr[qrows[i], cols[i]] = c + sm[i][1]
            out = [_dot(a_all[i], v_ref[krows[i], cols[i]]) for i in n_jobs]
            for i in n_jobs:
                o_ref[qrows[i], cols[i]] += out[i]
                if keep is not None:
                    w_keep[jobs[i][0], keep[i]] = a_all[i]
                    lb_keep[jobs[i][0], keep[i]] = lg[i][1].astype(BF16)

        def pad_row_of(m):
            return jnp.where(jnp.logical_or(m > 0, lane >= N_PAD), 1.0, 0.0)

        base = SB_GROUP * g
        heads = range(SB_HEADS_FWD)
        own = [(hh, r, base + rel, r == rel, pad_row_of(base) if rel == 0 else None)
               for hh in heads for r, rel in SB_STATIC_OWN]
        near = [(hh, r, base + rel, False, None) for hh in heads for r, rel in SB_STATIC_NEAR]
        own_slots = [i for _ in heads for i in range(len(SB_STATIC_OWN))]
        near_slots = [len(SB_STATIC_OWN) + i for _ in heads for i in range(len(SB_STATIC_NEAR))]
        pl.when(g == 0)(lambda: tiles(own, keep=own_slots))
        pl.when(g > 0)(lambda: tiles(own + near, keep=own_slots + near_slots))

        stop_ref[...] = jnp.zeros_like(stop_ref)
        mine = [(hh, r) for hh in heads for r in range(SB_GROUP)]
        reach = [jnp.max(c_scr[pl.ds(r * BLOCK, BLOCK), pl.ds(hh * HEAD, HEAD)]) for hh, r in mine]
        for (hh, r), reach_now in zip(mine, reach):
            def live(carry):
                m, c_max = carry
                return jnp.logical_and(m >= 0, c_max >= SB_DEAD)

            def step(carry, hh=hh, r=r):
                m, _ = carry
                tiles([(hh, r, m, False, pad_row_of(m))])
                return m - 1, jnp.max(c_scr[pl.ds(r * BLOCK, BLOCK), pl.ds(hh * HEAD, HEAD)])

            lowest = jnp.maximum(base - (SB_GROUP - 1 - r), 0)
            m_end, _ = lax.while_loop(live, step, (lowest - 1, reach_now))
            stop_ref[hh, pl.ds(r, 1), :] = jnp.broadcast_to((m_end + 1).astype(F32), (1, 128))
        tot_ref[...] = c_scr[...]

    qblk = pl.BlockSpec((SB_ROWS, wide), lambda b, h, g: (b * n_groups + g, h))
    table = pl.BlockSpec((2 * BLOCK, 2 * BLOCK), lambda b, h, g: (0, 0))
    stops = pl.BlockSpec((None, None, SB_HEADS_FWD, 8, 128), lambda b, h, g: (b, g, h, 0, 0))
    kept = pl.BlockSpec((None, None, SB_HEADS_FWD, N_STATIC, BLOCK, BLOCK), lambda b, h, g: (b, g, h, 0, 0, 0))
    return _pcall(
        body, name=name, grid=(bsz, n_pairs, n_groups),
        in_specs=[qblk, pl.BlockSpec((lp, wide), lambda b, h, g: (b, n_pairs + h)),
                  pl.BlockSpec((lp, wide), lambda b, h, g: (b, 2 * n_pairs + h)), table],
        out_specs=(qblk, qblk, stops, kept, kept),
        out_shape=(jax.ShapeDtypeStruct((t, d), F32), jax.ShapeDtypeStruct((t, d), F32),
                   jax.ShapeDtypeStruct((bsz, n_groups, n_heads, 8, 128), F32))
        + (jax.ShapeDtypeStruct((bsz, n_groups, n_heads, N_STATIC, BLOCK, BLOCK), BF16),) * 2,
        scratch_shapes=[pltpu.VMEM((SB_ROWS, wide), F32)],
        compiler_params=_params("parallel", "parallel", "arbitrary"),
    )(qkv, qkv, qkv, _sb_tables()[0])


def _sb_bwd(qkv, do, tot, stop, w_kept, lb_kept, bsz, n_blocks, name):
    t, d3 = qkv.shape
    d = d3 // 3
    n_heads = d // HEAD
    lp = n_blocks * BLOCK
    n_groups = n_blocks // SB_GROUP
    scale = HEAD ** -0.5

    def body(q_ref, k_ref, v_ref, do_ref, tot_ref, stop_ref, incl_ref, excl_ref, w_kept, lb_kept,
             dq_ref, dk_ref, dv_ref, dk_acc, dv_acc, dq_acc, p_scr, e_scr, dob_scr):
        g = pl.program_id(2)

        @pl.when(g == 0)
        def _():
            dk_acc[...] = jnp.zeros_like(dk_acc)
            dv_acc[...] = jnp.zeros_like(dv_acc)

        incl, excl = incl_ref[...], excl_ref[...]
        causal = _iota2((BLOCK, BLOCK), 1) < _iota2((BLOCK, BLOCK), 0)
        lane = _iota2((1, BLOCK), 1)
        dob_scr[...] = do_ref[...].astype(BF16)
        dq_acc[...] = jnp.zeros_like(dq_acc)
        p_scr[...] = jnp.zeros_like(p_scr)
        e_scr[...] = jnp.zeros_like(e_scr)

        def tiles(jobs, slots=None):
            n_jobs = range(len(jobs))
            cols = [pl.ds(hh * HEAD, HEAD) for hh, _, _, _, _ in jobs]
            qrows = [pl.ds(r * BLOCK, BLOCK) for _, r, _, _, _ in jobs]
            krows = [pl.ds(pl.multiple_of(m * BLOCK, BLOCK), BLOCK) for _, _, m, _, _ in jobs]
            diag = [dg for _, _, _, dg, _ in jobs]
            d_a = [_dot_nt(dob_scr[qrows[i], cols[i]], v_ref[krows[i], cols[i]]) for i in n_jobs]
            if slots is None:
                lg = [_sb_logits(q_ref[qrows[i], cols[i]], k_ref[krows[i], cols[i]], scale,
                                 causal if diag[i] else None, jobs[i][4]) for i in n_jobs]
                log_beta = [x[1] for x in lg]
                sm = [_sums(lg[i][2], incl) for i in n_jobs]
                a_all = []
                for i in n_jobs:
                    p = p_scr[qrows[i], cols[i]]
                    a = jnp.exp(log_beta[i] + (tot_ref[qrows[i], cols[i]] - p - sm[i][0]))
                    a_all.append((jnp.where(causal, a, 0.0) if diag[i] else a).astype(BF16))
                    p_scr[qrows[i], cols[i]] = p + sm[i][1]
            else:
                a_all = [w_kept[hh, s] for hh, s in slots]
                log_beta = [lb_kept[hh, s].astype(F32) for hh, s in slots]
            gr = [d_a[i] * a_all[i].astype(F32) for i in n_jobs]
            dv_part = [_dot_tn(a_all[i], dob_scr[qrows[i], cols[i]]) for i in n_jobs]
            gs = [_sums(gr[i], excl, pieces=1) for i in n_jobs]
            dz_all = []
            for i in n_jobs:
                e = e_scr[qrows[i], cols[i]]
                dz = gr[i] - (gr[i] + e + gs[i][0]) * jnp.exp(log_beta[i])
                if diag[i]:
                    dz = jnp.where(causal, dz, 0.0)
                dz_all.append((dz * scale).astype(BF16))
                e_scr[qrows[i], cols[i]] = e + gs[i][1]
            dk_part = [_dot_tn(dz_all[i], q_ref[qrows[i], cols[i]]) for i in n_jobs]
            dq_part = [_dot(dz_all[i], k_ref[krows[i], cols[i]]) for i in n_jobs]
            for i in n_jobs:
                dv_acc[krows[i], cols[i]] += dv_part[i]
                dk_acc[krows[i], cols[i]] += dk_part[i]
                dq_acc[qrows[i], cols[i]] += dq_part[i]

        def pad_row_of(m):
            return jnp.where(jnp.logical_or(m > 0, lane >= N_PAD), 1.0, 0.0)

        base = SB_GROUP * g
        heads = range(SB_HEADS_BWD)
        mine = [(hh, r) for hh in heads for r in range(SB_GROUP)]
        stops = [jnp.max(stop_ref[hh, pl.ds(r, 1), :]).astype(jnp.int32) for hh, r in mine]
        for (hh, r), stop_now in zip(mine, stops):
            lowest = jnp.maximum(base - (SB_GROUP - 1 - r), 0)

            def step(m, hh=hh, r=r):
                tiles([(hh, r, m, False, pad_row_of(m))])
                return m + 1

            lax.while_loop(lambda m, lowest=lowest: m < lowest, step, jnp.clip(stop_now, 0, lowest))
        slot_of = {tile: i for i, tile in enumerate(SB_STATIC_OWN + SB_STATIC_NEAR)}
        upwards = lambda tile: (tile[1], tile[0])
        own = [(hh,) + tile for hh in heads for tile in sorted(SB_STATIC_OWN, key=upwards)]
        near = [(hh,) + tile for hh in heads for tile in sorted(SB_STATIC_NEAR, key=upwards)]

        def static_batch(order):
            tiles([(hh, r, base + rel, r == rel, None) for hh, r, rel in order],
                  slots=[(hh, slot_of[(r, rel)]) for hh, r, rel in order])

        pl.when(g == 0)(lambda: static_batch(own))
        pl.when(g > 0)(lambda: static_batch(near + own))
        dq_ref[...] = dq_acc[...].astype(BF16)

        @pl.when(g == n_groups - 1)
        def _():
            dk_ref[...] = dk_acc[...].astype(BF16)
            dv_ref[...] = dv_acc[...].astype(BF16)

    n_pairs = n_heads // SB_HEADS_BWD
    wide = SB_HEADS_BWD * HEAD
    qblk = pl.BlockSpec((SB_ROWS, wide), lambda b, h, g: (b * n_groups + g, h))
    kblk = pl.BlockSpec((lp, wide), lambda b, h, g: (b, n_pairs + h))
    vblk = pl.BlockSpec((lp, wide), lambda b, h, g: (b, 2 * n_pairs + h))
    hblk = pl.BlockSpec((lp, wide), lambda b, h, g: (b, h))
    sblk = pl.BlockSpec((None, None, SB_HEADS_BWD, 8, 128), lambda b, h, g: (b, g, h, 0, 0))
    table = pl.BlockSpec((2 * BLOCK, 2 * BLOCK), lambda b, h, g: (0, 0))
    kept = pl.BlockSpec((None, None, SB_HEADS_BWD, N_STATIC, BLOCK, BLOCK), lambda b, h, g: (b, g, h, 0, 0, 0))
    _, incl, excl = _sb_tables()
    return _pcall(
        body, name=name, grid=(bsz, n_pairs, n_groups),
        in_specs=[qblk, kblk, vblk, qblk, qblk, sblk, table, table, kept, kept],
        out_specs=(qblk, hblk, hblk),
        out_shape=(jax.ShapeDtypeStruct((t, d), BF16),) * 3,
        scratch_shapes=[pltpu.VMEM((lp, wide), F32), pltpu.VMEM((lp, wide), F32)]
        + [pltpu.VMEM((SB_ROWS, wide), F32)] * 3 + [pltpu.VMEM((SB_ROWS, wide), BF16)],
        compiler_params=_params("parallel", "parallel", "arbitrary"),
    )(qkv, qkv, qkv, do, tot, stop, incl, excl, w_kept, lb_kept)


def _adamw(w, g, m, v):
    m = ADAM_B1 * m + (1.0 - ADAM_B1) * g
    v = ADAM_B2 * v + (1.0 - ADAM_B2) * (g * g)
    m_hat = m / (1.0 - ADAM_B1 ** ADAM_STEP)
    v_hat = v / (1.0 - ADAM_B2 ** ADAM_STEP)
    delta = -ADAM_LR * (m_hat / (jnp.sqrt(v_hat) + ADAM_EPS) + ADAM_WD * w)
    return delta, m, v


def _update_sharded(w, parts, m, v, name):
    r, c = w.shape
    tr = UPDATE_ROWS if r % UPDATE_ROWS == 0 else r

    def body(w_ref, p_ref, m_ref, v_ref, g_ref, d_ref, nm_ref, nv_ref):
        g = p_ref[0].astype(F32)
        for q in range(1, N_DEV):
            g = g + p_ref[q].astype(F32)
        g_ref[...] = g
        d_ref[...], nm_ref[...], nv_ref[...] = _adamw(w_ref[...], g, m_ref[...], v_ref[...])

    row = pl.BlockSpec((tr, c), lambda i: (i, 0))
    return _pcall(
        body, name=name, grid=(r // tr,),
        in_specs=[row, pl.BlockSpec((N_DEV, tr, c), lambda i: (0, i, 0)), row, row],
        out_specs=(row,) * 4, out_shape=(jax.ShapeDtypeStruct((r, c), F32),) * 4,
        compiler_params=_params("parallel"),
    )(w, parts, m, v)


SMALL_ROWS = 8


def _pack_small(dpre0, dpre1, dpost0, dpost1, dlb, dwon, loss, name):
    d = dpre0.shape[1]
    bsz = dlb.shape[0]

    def body(a0, a1, p0, p1, lb_ref, on_ref, loss_ref, out_ref):
        out_ref[...] = jnp.zeros_like(out_ref)
        out_ref[pl.ds(0, 1), :] = a0[...]
        out_ref[pl.ds(1, 1), :] = a1[...]
        out_ref[pl.ds(2, 1), :] = p0[...]
        out_ref[pl.ds(3, 1), :] = p1[...]
        acc = lb_ref[0]
        for b in range(1, bsz):
            acc = acc + lb_ref[b]
        out_ref[pl.ds(4, 1), :] = acc
        out_ref[pl.ds(5, 1), pl.ds(0, HEAD)] = on_ref[...]
        out_ref[pl.ds(6, 1), pl.ds(0, HEAD)] = loss_ref[pl.ds(0, 1), :]

    return _pcall(body, name=name, out_shape=jax.ShapeDtypeStruct((SMALL_ROWS, d), F32))(
        dpre0, dpre1, dpost0, dpost1, dlb, dwon, loss)


def _update_small(parts, pre, post, lbw, on, moments, name):
    d = pre.shape[1]

    def body(p_ref, pre_ref, post_ref, lbw_ref, on_ref, mpre, mpost, mlb, mon, vpre, vpost, vlb, von,
             loss_ref, *outs):
        def total(r0, nr, width):
            acc = p_ref[0, pl.ds(r0, nr), pl.ds(0, width)]
            for q in range(1, N_DEV):
                acc = acc + p_ref[q, pl.ds(r0, nr), pl.ds(0, width)]
            return acc

        def put(k, w, g, m, v):
            dl, nm, nv = _adamw(w, g, m, v)
            outs[4 * k][...] = g
            outs[4 * k + 1][...] = dl
            outs[4 * k + 2][...] = nm
            outs[4 * k + 3][...] = nv

        put(0, pre_ref[...], total(0, 2, d), mpre[...], vpre[...])
        put(1, post_ref[...], total(2, 2, d), mpost[...], vpost[...])
        a0, a1 = lbw_ref[pl.ds(0, 1), :], lbw_ref[pl.ds(1, 1), :]
        mx = jnp.maximum(a0, a1)
        e0, e1 = jnp.exp(a0 - mx), jnp.exp(a1 - mx)
        p0 = e0 / (e0 + e1)
        g0 = total(4, 1, d) * p0 * (1.0 - p0)
        for r, w, g in ((0, a0, g0), (1, a1, -g0)):
            row = pl.ds(r, 1)
            dl, nm, nv = _adamw(w, g, mlb[row, :], vlb[row, :])
            outs[8][row, :] = g
            outs[9][row, :] = dl
            outs[10][row, :] = nm
            outs[11][row, :] = nv
        put(3, on_ref[...], total(5, 1, HEAD), mon[...], von[...])
        loss_ref[...] = jnp.broadcast_to(total(6, 1, HEAD), loss_ref.shape)

    shapes = []
    for w in (pre, post, lbw, on):
        shapes += [jax.ShapeDtypeStruct(w.shape, F32)] * 4
    return _pcall(body, name=name, out_shape=(jax.ShapeDtypeStruct((8, 128), F32), *shapes))(
        parts, pre, post, lbw, on, *moments)


def kernel(x, meta_tokens, pre_norm, post_norm, hgrn_w_in, hgrn_lb, hgrn_out_norm, hgrn_w_out, sb_w_in, sb_w_out, loss_target, m_meta_tokens, m_pre_norm, m_post_norm, m_hgrn_w_in, m_hgrn_lb, m_hgrn_out_norm, m_hgrn_w_out, m_sb_w_in, m_sb_w_out, v_meta_tokens, v_pre_norm, v_post_norm, v_hgrn_w_in, v_hgrn_lb, v_hgrn_out_norm, v_hgrn_w_out, v_sb_w_in, v_sb_w_out):
    bsz, seq, d = x.shape
    n_blocks = seq // BLOCK + 1
    lp = n_blocks * BLOCK
    s = hgrn_w_in.shape[2]
    dsh = d // N_DEV

    w_in_h, meta_all = _gather_once_per_chip([hgrn_w_in[0].astype(BF16), meta_tokens], "gather_weights")
    meta_full = jnp.transpose(meta_all, (1, 0, 2)).reshape(N_META, d)

    lbrow = jnp.cumsum(jax.nn.softmax(hgrn_lb, axis=0), axis=0)[0:1]

    main0, gate0, yn0, h0 = _norm_inproj(None, pre_norm[0:1], w_in_h, F32, "inproj_hgrn", from_x=(x, meta_full))
    o0, states, decay0, k0, a_diag0, w_in_s, w_out_s, w_out_h = _hgrn_fwd(
        main0, lbrow, bsz, n_blocks,
        [sb_w_in[0].astype(BF16), sb_w_out[0].astype(BF16), hgrn_w_out[0].astype(BF16)], ["gather"] * 3, "hgrn_fwd")
    w_out_s = w_out_s.reshape(d, d)
    w_out_h = w_out_h.reshape(d, d)
    h1, y0, main1, gate1, yn1 = _mix_out(o0, gate0, h0, w_out_h, hgrn_out_norm, post_norm[0:1], True,
                                         "mix_out_hgrn_inproj_sb", next_proj=(pre_norm[1:2], w_in_s, BF16))
    o1, tot, stop, w_kept, lb_kept = _sb_fwd(main1, bsz, n_blocks, "sb_fwd")
    dh2, loss_part, do1, dgate1, dw_out_s, dpost1, _ = _mix_out(
        o1, gate1, h1, w_out_s, hgrn_out_norm, post_norm[1:2], False, "mix_out_sb_loss_bwd", target=loss_target)

    dq1, dk1, dv1 = _sb_bwd(main1, do1, tot, stop, w_kept, lb_kept, bsz, n_blocks, "sb_bwd")
    dproj1 = (dq1, dk1, dv1, dgate1)
    dh1, dpre1 = _inproj_bwd_x(dproj1, w_in_s, h1, pre_norm[1:2], dh2, [], [], "inproj_sb_bwd_x")
    dw_in_s = _inproj_bwd_w(yn1, dproj1, s, "inproj_sb_bwd_w")

    do0, dgate0, dw_out_h, dpost0, dwon = _mix_out_bwd(
        dh1, y0, o0, gate0, w_out_h, hgrn_out_norm, post_norm[0:1], True, "mix_out_hgrn_bwd")
    dq0, dfz0, dv0, dlb, p_in_s, p_out_s, p_out_h = _hgrn_bwd(
        main0, decay0, k0, a_diag0, lbrow, states, do0, bsz, n_blocks,
        [dw_in_s, dw_out_s.reshape(N_DEV, dsh, d), dw_out_h.reshape(N_DEV, dsh, d)], ["scatter"] * 3, "hgrn_bwd")
    dproj0 = (dq0, dfz0, dv0, dgate0)
    dw_in_h = _inproj_bwd_w(yn0, dproj0, s, "inproj_hgrn_bwd_w")
    dh0, dpre0, p_in_h = _inproj_bwd_x(
        dproj0, w_in_h, h0, pre_norm[0:1], dh1, [dw_in_h], ["scatter"], "inproj_hgrn_bwd_x")

    dh0 = dh0.reshape(bsz, lp, d)
    grad_x = dh0[:, BLOCK:]
    dmeta = jnp.sum(dh0[:, N_PAD:BLOCK], axis=0)
    dmeta = jnp.transpose(dmeta.reshape(N_META, N_DEV, dsh), (1, 0, 2))
    small = _pack_small(dpre0, dpre1, dpost0, dpost1, dlb, dwon, loss_part, "pack_small")

    p_meta, p_small = _exchange([dmeta, small], ["scatter", "gather"], "exchange_grads")

    u_meta = _update_sharded(meta_tokens, p_meta, m_meta_tokens, v_meta_tokens, "update_meta")
    u_in_h = _update_sharded(hgrn_w_in[0], p_in_h, m_hgrn_w_in[0], v_hgrn_w_in[0], "update_hgrn_w_in")
    u_out_h = _update_sharded(hgrn_w_out[0], p_out_h, m_hgrn_w_out[0], v_hgrn_w_out[0], "update_hgrn_w_out")
    u_in_s = _update_sharded(sb_w_in[0], p_in_s, m_sb_w_in[0], v_sb_w_in[0], "update_sb_w_in")
    u_out_s = _update_sharded(sb_w_out[0], p_out_s, m_sb_w_out[0], v_sb_w_out[0], "update_sb_w_out")
    sm = _update_small(p_small, pre_norm, post_norm, hgrn_lb, hgrn_out_norm,
                       (m_pre_norm, m_post_norm, m_hgrn_lb, m_hgrn_out_norm,
                        v_pre_norm, v_post_norm, v_hgrn_lb, v_hgrn_out_norm), "update_small")
    loss = sm[0][0, 0]
    u_pre, u_post, u_lb, u_on = sm[1:5], sm[5:9], sm[9:13], sm[13:17]

    per_w = [u_meta, u_pre, u_post, tuple(a[None] for a in u_in_h), u_lb, u_on,
             tuple(a[None] for a in u_out_h), tuple(a[None] for a in u_in_s), tuple(a[None] for a in u_out_s)]
    outs = [loss, grad_x]
    for k in range(4):
        outs += [u[k] for u in per_w]
    return tuple(outs)
```
